```python
import math
import jax, jax.numpy as jnp
from jax import lax
import numpy as np

D_MODEL = 1024
BATCH = 8
SEQ = 8192
DEPTH = 2

MEM_LEN = 256
HEAD_DIM = 64
N_MIX_HEADS = D_MODEL // HEAD_DIM
N_MEM_HEADS = 4
N_TOK_HEADS = N_MIX_HEADS - N_MEM_HEADS
TOK_WIDTH = N_TOK_HEADS * HEAD_DIM
MEM_WIDTH = N_MEM_HEADS * HEAD_DIM
MIX_WIDTH = TOK_WIDTH + MEM_WIDTH
Q_LORA = 384
KV_LORA = 256
QK_NOPE = 64
QK_ROPE = 32
V_DIM = HEAD_DIM
QK_DIM = QK_NOPE + QK_ROPE
ROPE_THETA = 10000.0
Q_BLOCK = 128
CONV_W = 4
LRU_C = 8.0
N_LRU_BLOCKS = N_TOK_HEADS
LRU_BLOCK = TOK_WIDTH // N_LRU_BLOCKS
ALPHA = (2.0 * DEPTH) ** 0.25
BETA = (8.0 * DEPTH) ** -0.25
NORM_EPS = 1e-6
N_MLA = (DEPTH + 1) // 2
N_LRU = DEPTH // 2
MLA_IN = Q_LORA + KV_LORA + QK_ROPE + MIX_WIDTH + MEM_WIDTH
LRU_IN = TOK_WIDTH + MIX_WIDTH + MEM_WIDTH

kernel_name = "hybrid_mla_rglru_memory_deepnorm"


def _split(t, sizes):
    idx = np.cumsum(sizes)[:-1].tolist()
    return jnp.split(t, idx, axis=-1)


def rms_norm(t, g):
    t32 = t.astype(jnp.float32)
    t32 = t32 * lax.rsqrt(jnp.mean(t32 * t32, axis=-1, keepdims=True) + NORM_EPS)
    return (t32 * g.astype(jnp.float32)).astype(t.dtype)


def layer_norm(t, g, b):
    t32 = t.astype(jnp.float32)
    mu = jnp.mean(t32, axis=-1, keepdims=True)
    var = jnp.mean(jnp.square(t32 - mu), axis=-1, keepdims=True)
    y = (t32 - mu) * lax.rsqrt(var + NORM_EPS)
    return (y * g.astype(jnp.float32) + b.astype(jnp.float32)).astype(t.dtype)


def apply_rope(t, positions):
    half = t.shape[-1] // 2
    inv_freq = ROPE_THETA ** (-jnp.arange(half, dtype=jnp.float32) / half)
    ang = positions.astype(jnp.float32)[..., None] * inv_freq
    cos = jnp.cos(ang)[:, :, None, :].astype(t.dtype)
    sin = jnp.sin(ang)[:, :, None, :].astype(t.dtype)
    t1, t2 = t[..., :half], t[..., half:]
    return jnp.concatenate([t1 * cos - t2 * sin, t1 * sin + t2 * cos], axis=-1)


def causal_attention(q, k, v):
    b, s, h, d = q.shape
    nb = s // Q_BLOCK
    scale = 1.0 / math.sqrt(d)
    qb = q.reshape(b, nb, Q_BLOCK, h, d).transpose(1, 0, 2, 3, 4)
    k_pos = jnp.arange(s)

    def one_block(args):
        q_blk, blk = args
        sc = jnp.einsum('bqhd,bkhd->bhqk', q_blk, k,
                        preferred_element_type=jnp.float32) * scale
        q_pos = blk * Q_BLOCK + jnp.arange(Q_BLOCK)
        mask = k_pos[None, :] <= q_pos[:, None]
        sc = jnp.where(mask[None, None], sc, -jnp.inf)
        p = jax.nn.softmax(sc, axis=-1).astype(v.dtype)
        return jnp.einsum('bhqk,bkhd->bqhd', p, v)

    out = lax.map(one_block, (qb, jnp.arange(nb)))
    return out.transpose(1, 0, 2, 3, 4).reshape(b, s, h, v.shape[-1])


def memory_attention(q, mem_k, mem_v):
    sc = jnp.einsum('bshd,bmhd->bhsm', q, mem_k,
                    preferred_element_type=jnp.float32) / math.sqrt(HEAD_DIM)
    p = jax.nn.softmax(sc, axis=-1).astype(mem_v.dtype)
    return jnp.einsum('bhsm,bmhd->bshd', p, mem_v)


def _lin_rec_combine(left, right):
    a1, b1 = left
    a2, b2 = right
    return a1 * a2, a2 * b1 + b2


def rg_lru_branch(u, conv_w, conv_b, w_r, b_r, w_i, b_i, lam):
    b, s, w = u.shape
    u_pad = jnp.pad(u, ((0, 0), (CONV_W - 1, 0), (0, 0)))
    xc = conv_b + u_pad[:, 0:s] * conv_w[0]
    for tap in range(1, CONV_W):
        xc = xc + u_pad[:, tap:tap + s] * conv_w[tap]
    xb = xc.reshape(b, s, N_LRU_BLOCKS, LRU_BLOCK)
    r = jax.nn.sigmoid(jnp.einsum('bsgi,gij->bsgj', xb, w_r).reshape(b, s, w) + b_r)
    i = jax.nn.sigmoid(jnp.einsum('bsgi,gij->bsgj', xb, w_i).reshape(b, s, w) + b_i)
    log_a = (-LRU_C * jax.nn.softplus(-lam.astype(jnp.float32))) * r.astype(jnp.float32)
    a = jnp.exp(log_a)
    gated_x = jnp.sqrt(-jnp.expm1(2.0 * log_a)) * (i * xc).astype(jnp.float32)
    _, hs = lax.associative_scan(_lin_rec_combine, (a, gated_x), axis=1)
    return hs.astype(u.dtype)


def _fwd_setup_inputs(seed: int = 0) -> dict:
    key = jax.random.key(seed)
    ks = jax.random.split(key, 24)
    f32 = jnp.float32
    nrm = lambda k, shape, s: jax.random.normal(k, shape, f32) * s
    x = nrm(ks[0], (BATCH, SEQ, D_MODEL), 1.0)
    mem = nrm(ks[1], (BATCH, MEM_LEN, D_MODEL), 1.0)
    offset = jax.random.randint(ks[2], (BATCH, 1), 0, 4096, dtype=jnp.int32)
    positions = (offset + jnp.arange(SEQ, dtype=jnp.int32)[None, :]).astype(jnp.int32)
    mla_w_in = nrm(ks[3], (N_MLA, D_MODEL, MLA_IN), D_MODEL ** -0.5)
    mla_q_norm = 1.0 + nrm(ks[4], (N_MLA, Q_LORA), 0.01)
    mla_w_uq = nrm(ks[5], (N_MLA, Q_LORA, N_TOK_HEADS * QK_DIM), Q_LORA ** -0.5)
    mla_kv_norm = 1.0 + nrm(ks[6], (N_MLA, KV_LORA), 0.01)
    mla_w_ukv = nrm(ks[7], (N_MLA, KV_LORA, N_TOK_HEADS * (QK_NOPE + V_DIM)), KV_LORA ** -0.5)
    lru_w_in = nrm(ks[8], (N_LRU, D_MODEL, LRU_IN), D_MODEL ** -0.5)
    lru_conv_w = nrm(ks[9], (N_LRU, CONV_W, TOK_WIDTH), CONV_W ** -0.5)
    lru_conv_b = nrm(ks[10], (N_LRU, TOK_WIDTH), 0.01)
    lru_w_rgate = nrm(ks[11], (N_LRU, N_LRU_BLOCKS, LRU_BLOCK, LRU_BLOCK), LRU_BLOCK ** -0.5)
    lru_b_rgate = nrm(ks[12], (N_LRU, TOK_WIDTH), 0.01)
    lru_w_igate = nrm(ks[13], (N_LRU, N_LRU_BLOCKS, LRU_BLOCK, LRU_BLOCK), LRU_BLOCK ** -0.5)
    lru_b_igate = nrm(ks[14], (N_LRU, TOK_WIDTH), 0.01)
    a_c = jax.random.uniform(ks[15], (N_LRU, TOK_WIDTH), f32, 0.9, 0.999)
    a0 = a_c ** (1.0 / LRU_C)
    lru_lambda = jnp.log(a0) - jnp.log1p(-a0)
    w_mem_kv = nrm(ks[16], (DEPTH, D_MODEL, 2 * MEM_WIDTH), D_MODEL ** -0.5)
    w_out = nrm(ks[17], (DEPTH, MIX_WIDTH, D_MODEL), BETA * MIX_WIDTH ** -0.5)
    ln_g = 1.0 + nrm(ks[18], (DEPTH, D_MODEL), 0.01)
    ln_b = nrm(ks[19], (DEPTH, D_MODEL), 0.01)
    return {"x": x, "mem": mem, "positions": positions,
            "mla_w_in": mla_w_in, "mla_q_norm": mla_q_norm, "mla_w_uq": mla_w_uq,
            "mla_kv_norm": mla_kv_norm, "mla_w_ukv": mla_w_ukv,
            "lru_w_in": lru_w_in, "lru_conv_w": lru_conv_w, "lru_conv_b": lru_conv_b,
            "lru_w_rgate": lru_w_rgate, "lru_b_rgate": lru_b_rgate,
            "lru_w_igate": lru_w_igate, "lru_b_igate": lru_b_igate, "lru_lambda": lru_lambda,
            "w_mem_kv": w_mem_kv, "w_out": w_out, "ln_g": ln_g, "ln_b": ln_b}


def _fwd_reference(x, mem, positions, mla_w_in, mla_q_norm, mla_w_uq, mla_kv_norm, mla_w_ukv,
              lru_w_in, lru_conv_w, lru_conv_b, lru_w_rgate, lru_b_rgate,
              lru_w_igate, lru_b_igate, lru_lambda, w_mem_kv, w_out, ln_g, ln_b):
    b, s, _ = x.shape
    h = x
    for layer in range(DEPTH):
        j = layer // 2
        if layer % 2 == 0:
            z = h @ mla_w_in[j]
            c_q, c_kv, k_r, gate, q_mem = _split(
                z, [Q_LORA, KV_LORA, QK_ROPE, MIX_WIDTH, MEM_WIDTH])
            q = (rms_norm(c_q, mla_q_norm[j]) @ mla_w_uq[j]).reshape(b, s, N_TOK_HEADS, QK_DIM)
            q = jnp.concatenate([q[..., :QK_NOPE], apply_rope(q[..., QK_NOPE:], positions)], axis=-1)
            kv = (rms_norm(c_kv, mla_kv_norm[j]) @ mla_w_ukv[j]).reshape(
                b, s, N_TOK_HEADS, QK_NOPE + V_DIM)
            k_nope, v = kv[..., :QK_NOPE], kv[..., QK_NOPE:]
            k_rope = apply_rope(k_r[:, :, None, :], positions)
            k = jnp.concatenate(
                [k_nope, jnp.broadcast_to(k_rope, (b, s, N_TOK_HEADS, QK_ROPE))], axis=-1)
            tok = causal_attention(q, k, v).reshape(b, s, TOK_WIDTH)
        else:
            z = h @ lru_w_in[j]
            u, gate, q_mem = _split(z, [TOK_WIDTH, MIX_WIDTH, MEM_WIDTH])
            tok = rg_lru_branch(u, lru_conv_w[j], lru_conv_b[j], lru_w_rgate[j], lru_b_rgate[j],
                                lru_w_igate[j], lru_b_igate[j], lru_lambda[j])
        mem_kv = (mem @ w_mem_kv[layer]).reshape(b, MEM_LEN, 2, N_MEM_HEADS, HEAD_DIM)
        mem_out = memory_attention(q_mem.reshape(b, s, N_MEM_HEADS, HEAD_DIM),
                                   mem_kv[:, :, 0], mem_kv[:, :, 1]).reshape(b, s, MEM_WIDTH)
        y = jnp.concatenate([tok, mem_out], axis=-1) * jax.nn.silu(gate)
        h = layer_norm(ALPHA * h + y @ w_out[layer], ln_g[layer], ln_b[layer])
    return h


import jax as _jax
import jax.numpy as _jnp

TWIN_FORMAT = 'train_step'
FWD_PARAMS = ['x', 'mem', 'positions', 'mla_w_in', 'mla_q_norm', 'mla_w_uq', 'mla_kv_norm', 'mla_w_ukv', 'lru_w_in', 'lru_conv_w', 'lru_conv_b', 'lru_w_rgate', 'lru_b_rgate', 'lru_w_igate', 'lru_b_igate', 'lru_lambda', 'w_mem_kv', 'w_out', 'ln_g', 'ln_b']
TWIN_WEIGHTS = ['mla_w_in', 'mla_q_norm', 'mla_w_uq', 'mla_kv_norm', 'mla_w_ukv', 'lru_w_in', 'lru_conv_w', 'lru_conv_b', 'lru_w_rgate', 'lru_b_rgate', 'lru_w_igate', 'lru_b_igate', 'lru_lambda', 'w_mem_kv', 'w_out', 'ln_g', 'ln_b']
TWIN_DIFF_INPUT = 'x'
TWIN_INPUTS = ['x', 'mem', 'positions', 'mla_w_in', 'mla_q_norm', 'mla_w_uq', 'mla_kv_norm', 'mla_w_ukv', 'lru_w_in', 'lru_conv_w', 'lru_conv_b', 'lru_w_rgate', 'lru_b_rgate', 'lru_w_igate', 'lru_b_igate', 'lru_lambda', 'w_mem_kv', 'w_out', 'ln_g', 'ln_b', 'loss_target', 'm_mla_w_in', 'm_mla_q_norm', 'm_mla_w_uq', 'm_mla_kv_norm', 'm_mla_w_ukv', 'm_lru_w_in', 'm_lru_conv_w', 'm_lru_conv_b', 'm_lru_w_rgate', 'm_lru_b_rgate', 'm_lru_w_igate', 'm_lru_b_igate', 'm_lru_lambda', 'm_w_mem_kv', 'm_w_out', 'm_ln_g', 'm_ln_b', 'v_mla_w_in', 'v_mla_q_norm', 'v_mla_w_uq', 'v_mla_kv_norm', 'v_mla_w_ukv', 'v_lru_w_in', 'v_lru_conv_w', 'v_lru_conv_b', 'v_lru_w_rgate', 'v_lru_b_rgate', 'v_lru_w_igate', 'v_lru_b_igate', 'v_lru_lambda', 'v_w_mem_kv', 'v_w_out', 'v_ln_g', 'v_ln_b']
TWIN_OUTPUTS = ['loss', 'grad_x', 'grad_mla_w_in', 'grad_mla_q_norm', 'grad_mla_w_uq', 'grad_mla_kv_norm', 'grad_mla_w_ukv', 'grad_lru_w_in', 'grad_lru_conv_w', 'grad_lru_conv_b', 'grad_lru_w_rgate', 'grad_lru_b_rgate', 'grad_lru_w_igate', 'grad_lru_b_igate', 'grad_lru_lambda', 'grad_w_mem_kv', 'grad_w_out', 'grad_ln_g', 'grad_ln_b', 'delta_mla_w_in', 'delta_mla_q_norm', 'delta_mla_w_uq', 'delta_mla_kv_norm', 'delta_mla_w_ukv', 'delta_lru_w_in', 'delta_lru_conv_w', 'delta_lru_conv_b', 'delta_lru_w_rgate', 'delta_lru_b_rgate', 'delta_lru_w_igate', 'delta_lru_b_igate', 'delta_lru_lambda', 'delta_w_mem_kv', 'delta_w_out', 'delta_ln_g', 'delta_ln_b', 'new_m_mla_w_in', 'new_m_mla_q_norm', 'new_m_mla_w_uq', 'new_m_mla_kv_norm', 'new_m_mla_w_ukv', 'new_m_lru_w_in', 'new_m_lru_conv_w', 'new_m_lru_conv_b', 'new_m_lru_w_rgate', 'new_m_lru_b_rgate', 'new_m_lru_w_igate', 'new_m_lru_b_igate', 'new_m_lru_lambda', 'new_m_w_mem_kv', 'new_m_w_out', 'new_m_ln_g', 'new_m_ln_b', 'new_v_mla_w_in', 'new_v_mla_q_norm', 'new_v_mla_w_uq', 'new_v_mla_kv_norm', 'new_v_mla_w_ukv', 'new_v_lru_w_in', 'new_v_lru_conv_w', 'new_v_lru_conv_b', 'new_v_lru_w_rgate', 'new_v_lru_b_rgate', 'new_v_lru_w_igate', 'new_v_lru_b_igate', 'new_v_lru_lambda', 'new_v_w_mem_kv', 'new_v_w_out', 'new_v_ln_g', 'new_v_ln_b']
TWIN_LEAF_KINDS = {'loss': 'loss', 'grad_x': 'grad_x', 'grad_mla_w_in': 'grad_w', 'grad_mla_q_norm': 'grad_w', 'grad_mla_w_uq': 'grad_w', 'grad_mla_kv_norm': 'grad_w', 'grad_mla_w_ukv': 'grad_w', 'grad_lru_w_in': 'grad_w', 'grad_lru_conv_w': 'grad_w', 'grad_lru_conv_b': 'grad_w', 'grad_lru_w_rgate': 'grad_w', 'grad_lru_b_rgate': 'grad_w', 'grad_lru_w_igate': 'grad_w', 'grad_lru_b_igate': 'grad_w', 'grad_lru_lambda': 'grad_w', 'grad_w_mem_kv': 'grad_w', 'grad_w_out': 'grad_w', 'grad_ln_g': 'grad_w', 'grad_ln_b': 'grad_w', 'delta_mla_w_in': 'delta_w', 'delta_mla_q_norm': 'delta_w', 'delta_mla_w_uq': 'delta_w', 'delta_mla_kv_norm': 'delta_w', 'delta_mla_w_ukv': 'delta_w', 'delta_lru_w_in': 'delta_w', 'delta_lru_conv_w': 'delta_w', 'delta_lru_conv_b': 'delta_w', 'delta_lru_w_rgate': 'delta_w', 'delta_lru_b_rgate': 'delta_w', 'delta_lru_w_igate': 'delta_w', 'delta_lru_b_igate': 'delta_w', 'delta_lru_lambda': 'delta_w', 'delta_w_mem_kv': 'delta_w', 'delta_w_out': 'delta_w', 'delta_ln_g': 'delta_w', 'delta_ln_b': 'delta_w', 'new_m_mla_w_in': 'new_m', 'new_m_mla_q_norm': 'new_m', 'new_m_mla_w_uq': 'new_m', 'new_m_mla_kv_norm': 'new_m', 'new_m_mla_w_ukv': 'new_m', 'new_m_lru_w_in': 'new_m', 'new_m_lru_conv_w': 'new_m', 'new_m_lru_conv_b': 'new_m', 'new_m_lru_w_rgate': 'new_m', 'new_m_lru_b_rgate': 'new_m', 'new_m_lru_w_igate': 'new_m', 'new_m_lru_b_igate': 'new_m', 'new_m_lru_lambda': 'new_m', 'new_m_w_mem_kv': 'new_m', 'new_m_w_out': 'new_m', 'new_m_ln_g': 'new_m', 'new_m_ln_b': 'new_m', 'new_v_mla_w_in': 'new_v', 'new_v_mla_q_norm': 'new_v', 'new_v_mla_w_uq': 'new_v', 'new_v_mla_kv_norm': 'new_v', 'new_v_mla_w_ukv': 'new_v', 'new_v_lru_w_in': 'new_v', 'new_v_lru_conv_w': 'new_v', 'new_v_lru_conv_b': 'new_v', 'new_v_lru_w_rgate': 'new_v', 'new_v_lru_b_rgate': 'new_v', 'new_v_lru_w_igate': 'new_v', 'new_v_lru_b_igate': 'new_v', 'new_v_lru_lambda': 'new_v', 'new_v_w_mem_kv': 'new_v', 'new_v_w_out': 'new_v', 'new_v_ln_g': 'new_v', 'new_v_ln_b': 'new_v'}


def _forward(args):
    return _fwd_reference(*[args[k] for k in FWD_PARAMS])


def _output_shape():
    def fwd():
        inp = _fwd_setup_inputs(0)
        return _fwd_reference(*[inp[k] for k in FWD_PARAMS])
    out = _jax.eval_shape(fwd)
    return out.shape, out.dtype

N_MICROBATCH = 1
ADAM_LR = 0.001
ADAM_B1 = 0.9
ADAM_B2 = 0.999
ADAM_EPS = 1e-08
ADAM_WD = 0.01
ADAM_STEP = 10
PER_EXAMPLE_BATCH_AXIS = {'x': 0, 'mem': 0, 'positions': 0, 'loss_target': 0}
SHARED_INPUTS = []
_WEIGHT_DTYPES = {'mla_w_in': _jnp.float32, 'mla_q_norm': _jnp.float32, 'mla_w_uq': _jnp.float32, 'mla_kv_norm': _jnp.float32, 'mla_w_ukv': _jnp.float32, 'lru_w_in': _jnp.float32, 'lru_conv_w': _jnp.float32, 'lru_conv_b': _jnp.float32, 'lru_w_rgate': _jnp.float32, 'lru_b_rgate': _jnp.float32, 'lru_w_igate': _jnp.float32, 'lru_b_igate': _jnp.float32, 'lru_lambda': _jnp.float32, 'w_mem_kv': _jnp.float32, 'w_out': _jnp.float32, 'ln_g': _jnp.float32, 'ln_b': _jnp.float32}
MOMENT_SCALE = {'mla_w_in': 1.384859e-02, 'mla_q_norm': 1.341314e-02, 'mla_w_uq': 7.933539e-03, 'mla_kv_norm': 2.700204e-02, 'mla_w_ukv': 1.011279e-02, 'lru_w_in': 3.308488e-02, 'lru_conv_w': 4.385130e-02, 'lru_conv_b': 4.966094e-01, 'lru_w_rgate': 2.004225e-02, 'lru_b_rgate': 1.477820e-02, 'lru_w_igate': 3.730731e-02, 'lru_b_igate': 1.407300e-02, 'lru_lambda': 2.176057e-02, 'w_mem_kv': 5.672844e-03, 'w_out': 4.827222e-02, 'ln_g': 4.526266e+01, 'ln_b': 1.118259e+00}


def _to_microbatches(a, axis):
    t = _jnp.moveaxis(a, axis, 0)
    t = t.reshape((N_MICROBATCH, t.shape[0] // N_MICROBATCH) + t.shape[1:])
    return _jnp.moveaxis(t, 1, axis + 1)


def setup_inputs(seed: int = 0) -> dict:
    inp = _fwd_setup_inputs(seed)
    key = _jax.random.fold_in(_jax.random.key(seed), 7919)
    shape, _ = _output_shape()
    out = dict(inp)
    out["loss_target"] = _jax.random.normal(_jax.random.fold_in(key, 0), shape, _jnp.float32)
    for i, name in enumerate(TWIN_WEIGHTS):
        w = inp[name].astype(_jnp.float32)
        if MOMENT_SCALE is None:
            s = _jnp.sqrt(_jnp.mean(_jnp.square(w)) + 1e-30)
        else:
            s = MOMENT_SCALE[name]
        km, kv = _jax.random.split(_jax.random.fold_in(key, i + 1))
        out[name] = w
        out["m_" + name] = s * _jax.random.normal(km, w.shape, _jnp.float32)
        out["v_" + name] = (s * s) * _jax.random.uniform(kv, w.shape, _jnp.float32, 0.5, 1.5)
    if N_MICROBATCH > 1:
        for name, axis in PER_EXAMPLE_BATCH_AXIS.items():
            out[name] = _to_microbatches(out[name], axis)
    return {'x': out['x'], 'mem': out['mem'], 'positions': out['positions'], 'mla_w_in': out['mla_w_in'], 'mla_q_norm': out['mla_q_norm'], 'mla_w_uq': out['mla_w_uq'], 'mla_kv_norm': out['mla_kv_norm'], 'mla_w_ukv': out['mla_w_ukv'], 'lru_w_in': out['lru_w_in'], 'lru_conv_w': out['lru_conv_w'], 'lru_conv_b': out['lru_conv_b'], 'lru_w_rgate': out['lru_w_rgate'], 'lru_b_rgate': out['lru_b_rgate'], 'lru_w_igate': out['lru_w_igate'], 'lru_b_igate': out['lru_b_igate'], 'lru_lambda': out['lru_lambda'], 'w_mem_kv': out['w_mem_kv'], 'w_out': out['w_out'], 'ln_g': out['ln_g'], 'ln_b': out['ln_b'], 'loss_target': out['loss_target'], 'm_mla_w_in': out['m_mla_w_in'], 'm_mla_q_norm': out['m_mla_q_norm'], 'm_mla_w_uq': out['m_mla_w_uq'], 'm_mla_kv_norm': out['m_mla_kv_norm'], 'm_mla_w_ukv': out['m_mla_w_ukv'], 'm_lru_w_in': out['m_lru_w_in'], 'm_lru_conv_w': out['m_lru_conv_w'], 'm_lru_conv_b': out['m_lru_conv_b'], 'm_lru_w_rgate': out['m_lru_w_rgate'], 'm_lru_b_rgate': out['m_lru_b_rgate'], 'm_lru_w_igate': out['m_lru_w_igate'], 'm_lru_b_igate': out['m_lru_b_igate'], 'm_lru_lambda': out['m_lru_lambda'], 'm_w_mem_kv': out['m_w_mem_kv'], 'm_w_out': out['m_w_out'], 'm_ln_g': out['m_ln_g'], 'm_ln_b': out['m_ln_b'], 'v_mla_w_in': out['v_mla_w_in'], 'v_mla_q_norm': out['v_mla_q_norm'], 'v_mla_w_uq': out['v_mla_w_uq'], 'v_mla_kv_norm': out['v_mla_kv_norm'], 'v_mla_w_ukv': out['v_mla_w_ukv'], 'v_lru_w_in': out['v_lru_w_in'], 'v_lru_conv_w': out['v_lru_conv_w'], 'v_lru_conv_b': out['v_lru_conv_b'], 'v_lru_w_rgate': out['v_lru_w_rgate'], 'v_lru_b_rgate': out['v_lru_b_rgate'], 'v_lru_w_igate': out['v_lru_w_igate'], 'v_lru_b_igate': out['v_lru_b_igate'], 'v_lru_lambda': out['v_lru_lambda'], 'v_w_mem_kv': out['v_w_mem_kv'], 'v_w_out': out['v_w_out'], 'v_ln_g': out['v_ln_g'], 'v_ln_b': out['v_ln_b']}


def _loss(weights, diff, rest, loss_target):
    with _jax.named_scope("forward"):
        args = {**rest, TWIN_DIFF_INPUT: diff, **{k: w.astype(_WEIGHT_DTYPES[k]) for k, w in weights.items()}}
        y = _forward(args)
    with _jax.named_scope("loss_head"):
        err = _jnp.square(y.astype(_jnp.float32) - loss_target)
        return 0.5 * _jnp.sum(_jnp.mean(err, axis=-1)) if err.ndim else 0.5 * err


def _adamw(w, g, m, v):
    m = ADAM_B1 * m + (1.0 - ADAM_B1) * g
    v = ADAM_B2 * v + (1.0 - ADAM_B2) * _jnp.square(g)
    m_hat = m / (1.0 - ADAM_B1 ** ADAM_STEP)
    v_hat = v / (1.0 - ADAM_B2 ** ADAM_STEP)
    delta = -ADAM_LR * (m_hat / (_jnp.sqrt(v_hat) + ADAM_EPS) + ADAM_WD * w)
    return delta, m, v


def reference(x, mem, positions, mla_w_in, mla_q_norm, mla_w_uq, mla_kv_norm, mla_w_ukv, lru_w_in, lru_conv_w, lru_conv_b, lru_w_rgate, lru_b_rgate, lru_w_igate, lru_b_igate, lru_lambda, w_mem_kv, w_out, ln_g, ln_b, loss_target, m_mla_w_in, m_mla_q_norm, m_mla_w_uq, m_mla_kv_norm, m_mla_w_ukv, m_lru_w_in, m_lru_conv_w, m_lru_conv_b, m_lru_w_rgate, m_lru_b_rgate, m_lru_w_igate, m_lru_b_igate, m_lru_lambda, m_w_mem_kv, m_w_out, m_ln_g, m_ln_b, v_mla_w_in, v_mla_q_norm, v_mla_w_uq, v_mla_kv_norm, v_mla_w_ukv, v_lru_w_in, v_lru_conv_w, v_lru_conv_b, v_lru_w_rgate, v_lru_b_rgate, v_lru_w_igate, v_lru_b_igate, v_lru_lambda, v_w_mem_kv, v_w_out, v_ln_g, v_ln_b):
    given = dict(x=x, mem=mem, positions=positions, mla_w_in=mla_w_in, mla_q_norm=mla_q_norm, mla_w_uq=mla_w_uq, mla_kv_norm=mla_kv_norm, mla_w_ukv=mla_w_ukv, lru_w_in=lru_w_in, lru_conv_w=lru_conv_w, lru_conv_b=lru_conv_b, lru_w_rgate=lru_w_rgate, lru_b_rgate=lru_b_rgate, lru_w_igate=lru_w_igate, lru_b_igate=lru_b_igate, lru_lambda=lru_lambda, w_mem_kv=w_mem_kv, w_out=w_out, ln_g=ln_g, ln_b=ln_b, loss_target=loss_target, m_mla_w_in=m_mla_w_in, m_mla_q_norm=m_mla_q_norm, m_mla_w_uq=m_mla_w_uq, m_mla_kv_norm=m_mla_kv_norm, m_mla_w_ukv=m_mla_w_ukv, m_lru_w_in=m_lru_w_in, m_lru_conv_w=m_lru_conv_w, m_lru_conv_b=m_lru_conv_b, m_lru_w_rgate=m_lru_w_rgate, m_lru_b_rgate=m_lru_b_rgate, m_lru_w_igate=m_lru_w_igate, m_lru_b_igate=m_lru_b_igate, m_lru_lambda=m_lru_lambda, m_w_mem_kv=m_w_mem_kv, m_w_out=m_w_out, m_ln_g=m_ln_g, m_ln_b=m_ln_b, v_mla_w_in=v_mla_w_in, v_mla_q_norm=v_mla_q_norm, v_mla_w_uq=v_mla_w_uq, v_mla_kv_norm=v_mla_kv_norm, v_mla_w_ukv=v_mla_w_ukv, v_lru_w_in=v_lru_w_in, v_lru_conv_w=v_lru_conv_w, v_lru_conv_b=v_lru_conv_b, v_lru_w_rgate=v_lru_w_rgate, v_lru_b_rgate=v_lru_b_rgate, v_lru_w_igate=v_lru_w_igate, v_lru_b_igate=v_lru_b_igate, v_lru_lambda=v_lru_lambda, v_w_mem_kv=v_w_mem_kv, v_w_out=v_w_out, v_ln_g=v_ln_g, v_ln_b=v_ln_b)
    weights = {n: given[n] for n in TWIN_WEIGHTS}
    shared = {n: given[n] for n in SHARED_INPUTS}
    per_example = {n: given[n] for n in ['x', 'mem', 'positions']}
    grad_fn = _jax.value_and_grad(_loss, argnums=(0, 1))

    def one_microbatch(ex, loss_target):
        ex = dict(ex)
        diff = ex.pop(TWIN_DIFF_INPUT)
        return grad_fn(weights, diff, {**shared, **ex}, loss_target)

    if N_MICROBATCH == 1:
        loss, (grad_w, grad_x) = one_microbatch(per_example, given["loss_target"])
    else:
        def body(carry, xs):
            loss_sum, grad_sum = carry
            l_k, (gw_k, gx_k) = one_microbatch(xs[0], xs[1])
            with _jax.named_scope("update"):
                return (loss_sum + l_k, _jax.tree.map(_jnp.add, grad_sum, gw_k)), gx_k

        init = (_jnp.zeros((), _jnp.float32), _jax.tree.map(_jnp.zeros_like, weights))
        (loss, grad_w), grad_x = _jax.lax.scan(body, init, (per_example, given["loss_target"]))
    with _jax.named_scope("update"):
        delta_w, new_m, new_v = {}, {}, {}
        for n in TWIN_WEIGHTS:
            delta_w[n], new_m[n], new_v[n] = _adamw(weights[n], grad_w[n], given["m_" + n], given["v_" + n])
    return (loss, grad_x, *[grad_w[n] for n in TWIN_WEIGHTS], *[delta_w[n] for n in TWIN_WEIGHTS],
            *[new_m[n] for n in TWIN_WEIGHTS], *[new_v[n] for n in TWIN_WEIGHTS])
```

```python
import functools
import math

import jax
import jax.numpy as jnp
from jax import lax
from jax.experimental import pallas as pl
from jax.experimental.pallas import tpu as pltpu

F32, BF16 = jnp.float32, jnp.bfloat16
MESH = pl.DeviceIdType.MESH

D_MODEL = 1024
N_TOK_HEADS = 12
TOK_W = 768
MEM_W = 256
MEM_LEN = 256
Q_LORA, KV_LORA = 384, 256
HEAD_PAD = 128
QK_W = N_TOK_HEADS * HEAD_PAD
ATT_SCALE = 1.0 / math.sqrt(96.0)
ROPE_THETA = 10000.0
LRU_C = 8.0
ALPHA = 4.0 ** 0.25
NORM_EPS = 1e-6
ADAM_LR, ADAM_B1, ADAM_B2, ADAM_EPS, ADAM_WD, ADAM_STEP = 0.001, 0.9, 0.999, 1e-08, 0.01, 10

TB_PROJ = 512
TB_PROJ_BWD = 256
TB_MIX = 256
TB_LRU = 256
TQ_ATT = 512
VMEM_LIMIT = 56 * 1024 * 1024


def _mm(a, b):
    return jnp.dot(a.astype(BF16), b.astype(BF16), preferred_element_type=F32)


def _mm_nt(a, b):
    return lax.dot_general(a.astype(BF16), b.astype(BF16), (((1,), (1,)), ((), ())), preferred_element_type=F32)


def _mm_tn(a, b):
    return lax.dot_general(a.astype(BF16), b.astype(BF16), (((0,), (0,)), ((), ())), preferred_element_type=F32)


def _rows(tb, w):
    return pl.BlockSpec((tb, w), lambda i: (i, 0))


def _const(shape):
    n = len(shape)
    return pl.BlockSpec(shape, lambda i: (0,) * n)


def _params(sem=("arbitrary",)):
    return pltpu.CompilerParams(dimension_semantics=sem, vmem_limit_bytes=VMEM_LIMIT)


def _iota(shape, dim):
    return lax.broadcasted_iota(jnp.int32, shape, dim)


def _rope_tables(pos_col, inv_lane, tb):
    s = pos_col.shape[0]

    def body(pos_ref, inv_ref, c_ref, sa_ref, sb_ref):
        ang = pos_ref[...].astype(F32) * inv_ref[...]
        lane = _iota(ang.shape, 1)
        cs, sn = jnp.cos(ang), jnp.sin(ang)
        c_ref[...] = jnp.where(lane < 64, 1.0, jnp.where(lane < 96, cs, 0.0))
        sa_ref[...] = jnp.where((lane >= 64) & (lane < 80), -sn, 0.0)
        sb_ref[...] = jnp.where((lane >= 80) & (lane < 96), sn, 0.0)

    shp = jax.ShapeDtypeStruct((s, HEAD_PAD), F32)
    return pl.pallas_call(
        body, name="rope_tables", grid=(s // tb,), out_shape=(shp, shp, shp),
        in_specs=[_rows(tb, 1), _const((1, HEAD_PAD))], out_specs=(_rows(tb, HEAD_PAD),) * 3,
        compiler_params=_params(("parallel",)),
    )(pos_col, inv_lane)


def _rope(t, c, sa, sb):
    return t * c + pltpu.roll(t, 112, 1) * sa + pltpu.roll(t, 16, 1) * sb


def _rope_t(d, c, sa, sb):
    return d * c + pltpu.roll(d * sa, 16, 1) + pltpu.roll(d * sb, 112, 1)


def _rms(c, g):
    r = lax.rsqrt(jnp.mean(c * c, axis=-1, keepdims=True) + NORM_EPS)
    xh = c * r
    return xh * g, xh, r


def _mla_proj_fwd(x, win, gq, gkv, wuq, wukv_k, wukv_v, ctab, satab, sbtab):
    s = x.shape[0]
    tb = min(TB_PROJ, s)

    def body(x_ref, win_ref, gq_ref, gkv_ref, wuq_ref, wk_ref, wv_ref, c_ref, sa_ref, sb_ref,
             gate_ref, qm_ref, cq_ref, ckv_ref, q_ref, k_ref, v_ref):
        z = _mm(x_ref[...], win_ref[...])
        gate_ref[...] = z[:, 0:1024]
        qm_ref[...] = z[:, 1024:1280].astype(BF16)
        cq = z[:, 1280:1664]
        ckv = z[:, 1664:1920]
        cq_ref[...] = cq
        ckv_ref[...] = ckv
        c, sa, sb = c_ref[...], sa_ref[...], sb_ref[...]
        nq, _, _ = _rms(cq, gq_ref[...])
        nkv, _, _ = _rms(ckv, gkv_ref[...])
        qf = _mm(nq, wuq_ref[...])
        kf = _mm(nkv, wk_ref[...])
        v_ref[...] = _mm(nkv, wv_ref[...]).astype(BF16)
        kr = _rope(z[:, 1920:2048], c, sa, sb)
        for h in range(N_TOK_HEADS):
            sl = slice(HEAD_PAD * h, HEAD_PAD * (h + 1))
            q_ref[:, sl] = _rope(qf[:, sl], c, sa, sb).astype(BF16)
            k_ref[:, sl] = (kf[:, sl] + kr).astype(BF16)

    outs = (jax.ShapeDtypeStruct((s, 1024), F32), jax.ShapeDtypeStruct((s, MEM_W), BF16),
            jax.ShapeDtypeStruct((s, Q_LORA), F32), jax.ShapeDtypeStruct((s, KV_LORA), F32),
            jax.ShapeDtypeStruct((s, QK_W), BF16), jax.ShapeDtypeStruct((s, QK_W), BF16),
            jax.ShapeDtypeStruct((s, TOK_W), BF16))
    return pl.pallas_call(
        body, name="mla_proj_fwd", grid=(s // tb,), out_shape=outs,
        in_specs=[_rows(tb, 1024), _const((1024, 2048)), _const((1, Q_LORA)), _const((1, KV_LORA)),
                  _const((Q_LORA, QK_W)), _const((KV_LORA, QK_W)), _const((KV_LORA, TOK_W)),
                  _rows(tb, HEAD_PAD), _rows(tb, HEAD_PAD), _rows(tb, HEAD_PAD)],
        out_specs=(_rows(tb, 1024), _rows(tb, MEM_W), _rows(tb, Q_LORA), _rows(tb, KV_LORA),
                   _rows(tb, QK_W), _rows(tb, QK_W), _rows(tb, TOK_W)),
        compiler_params=_params(("parallel",)),
    )(x, win, gq, gkv, wuq, wukv_k, wukv_v, ctab, satab, sbtab)


def _mla_proj_bwd(x, cq, ckv, dq, dk, dv, dgate, dqm, dres, win, gq, gkv, wuq, wukv_k, wukv_v, ctab, satab, sbtab):
    s = x.shape[0]
    tb = min(TB_PROJ_BWD, s)

    def body(x_ref, cq_ref, ckv_ref, dq_ref, dk_ref, dv_ref, dgate_ref, dqm_ref, dres_ref, win_ref, gq_ref, gkv_ref,
             wuq_ref, wk_ref, wv_ref, c_ref, sa_ref, sb_ref,
             dx_ref, dwin_ref, dwuq_ref, dwk_ref, dwv_ref, dgq_ref, dgkv_ref):
        @pl.when(pl.program_id(0) == 0)
        def _():
            for r in (dwin_ref, dwuq_ref, dwk_ref, dwv_ref, dgq_ref, dgkv_ref):
                r[...] = jnp.zeros_like(r)

        c, sa, sb = c_ref[...], sa_ref[...], sb_ref[...]
        lane = _iota((tb, HEAD_PAD), 1)
        gq, gkv = gq_ref[...], gkv_ref[...]
        nq, xhq, rq = _rms(cq_ref[...], gq)
        nkv, xhk, rk = _rms(ckv_ref[...], gkv)
        dkp = dk_ref[...]
        dqs, dkr = [], jnp.zeros((tb, HEAD_PAD), F32)
        for h in range(N_TOK_HEADS):
            sl = slice(HEAD_PAD * h, HEAD_PAD * (h + 1))
            dqs.append(_rope_t(dq_ref[:, sl], c, sa, sb).astype(BF16))
            dkr = dkr + dkp[:, sl]
        dqf = jnp.concatenate(dqs, axis=1)
        dkr = jnp.where((lane >= 64) & (lane < 96), _rope_t(dkr, c, sa, sb), 0.0)
        dvb = dv_ref[...].astype(BF16)
        dkb = dkp.astype(BF16)
        dnq = _mm_nt(dqf, wuq_ref[...])
        dwuq_ref[...] += _mm_tn(nq, dqf)
        dgq_ref[...] += jnp.sum(dnq * xhq, axis=0, keepdims=True)
        dxh = dnq * gq
        dcq = rq * (dxh - xhq * jnp.mean(dxh * xhq, axis=-1, keepdims=True))
        dnkv = _mm_nt(dkb, wk_ref[...]) + _mm_nt(dvb, wv_ref[...])
        nkvb = nkv.astype(BF16)
        dwk_ref[...] += _mm_tn(nkvb, dkb)
        dwv_ref[...] += _mm_tn(nkvb, dvb)
        dgkv_ref[...] += jnp.sum(dnkv * xhk, axis=0, keepdims=True)
        dxh = dnkv * gkv
        dckv = rk * (dxh - xhk * jnp.mean(dxh * xhk, axis=-1, keepdims=True))
        dz = jnp.concatenate([dgate_ref[...], dqm_ref[...], dcq, dckv, dkr], axis=1).astype(BF16)
        dx_ref[...] = _mm_nt(dz, win_ref[...]) + dres_ref[...]
        dwin_ref[...] += _mm_tn(x_ref[...], dz)

    outs = (jax.ShapeDtypeStruct((s, 1024), F32), jax.ShapeDtypeStruct((1024, 2048), F32),
            jax.ShapeDtypeStruct((Q_LORA, QK_W), F32), jax.ShapeDtypeStruct((KV_LORA, QK_W), F32),
            jax.ShapeDtypeStruct((KV_LORA, TOK_W), F32), jax.ShapeDtypeStruct((1, Q_LORA), F32),
            jax.ShapeDtypeStruct((1, KV_LORA), F32))
    return pl.pallas_call(
        body, name="mla_proj_bwd", grid=(s // tb,), out_shape=outs,
        in_specs=[_rows(tb, 1024), _rows(tb, Q_LORA), _rows(tb, KV_LORA), _rows(tb, QK_W), _rows(tb, QK_W),
                  _rows(tb, TOK_W), _rows(tb, 1024), _rows(tb, MEM_W), _rows(tb, 1024),
                  _const((1024, 2048)), _const((1, Q_LORA)), _const((1, KV_LORA)),
                  _const((Q_LORA, QK_W)), _const((KV_LORA, QK_W)), _const((KV_LORA, TOK_W)),
                  _rows(tb, HEAD_PAD), _rows(tb, HEAD_PAD), _rows(tb, HEAD_PAD)],
        out_specs=(_rows(tb, 1024), _const((1024, 2048)), _const((Q_LORA, QK_W)), _const((KV_LORA, QK_W)),
                   _const((KV_LORA, TOK_W)), _const((1, Q_LORA)), _const((1, KV_LORA))),
        compiler_params=_params(),
    )(x, cq, ckv, dq, dk, dv, dgate, dqm, dres, win, gq, gkv, wuq, wukv_k, wukv_v, ctab, satab, sbtab)


def _attn_fwd(q, k, v):
    s = q.shape[0]
    tq = min(TQ_ATT, s)

    def body(q_ref, k_ref, v_ref, o_ref, lse_ref):
        i = pl.program_id(1)
        lane = _iota((tq, HEAD_PAD), 1)
        row, col = _iota((tq, tq), 0), _iota((tq, tq), 1)
        res = []
        for hh in range(2):
            hs = slice(HEAD_PAD * hh, HEAD_PAD * (hh + 1))
            qh = q_ref[:, hs]

            def tile(j, carry, masked, hs=hs, qh=qh):
                m, l, acc = carry
                st = pl.multiple_of(j * tq, tq)
                sc = _mm_nt(qh, k_ref[pl.ds(st, tq), hs]) * ATT_SCALE
                if masked:
                    sc = jnp.where(col <= row, sc, -jnp.inf)
                m_new = jnp.maximum(m, jnp.max(sc, axis=-1, keepdims=True))
                p = jnp.exp(sc - m_new)
                a = jnp.exp(m - m_new)
                l = a * l + jnp.sum(p, axis=-1, keepdims=True)
                acc = a * acc + _mm(p, v_ref[pl.ds(st, tq), :])
                return m_new, l, acc

            init = (jnp.full((tq, 1), -jnp.inf, F32), jnp.zeros((tq, 1), F32), jnp.zeros((tq, HEAD_PAD), F32))
            carry = lax.fori_loop(0, i, functools.partial(tile, masked=False), init)
            m, l, acc = tile(i, carry, True)
            res.append((acc / l, m + jnp.log(l)))
        o_ref[...] = jnp.where(lane < 64, res[0][0], res[1][0])
        lse_ref[...] = jnp.where(lane < 64, res[0][1], res[1][1])

    shp = jax.ShapeDtypeStruct((s, TOK_W), F32)
    return pl.pallas_call(
        body, name="attn_fwd", grid=(N_TOK_HEADS // 2, s // tq), out_shape=(shp, shp),
        in_specs=[pl.BlockSpec((tq, 2 * HEAD_PAD), lambda j, i: (i, j)),
                  pl.BlockSpec((s, 2 * HEAD_PAD), lambda j, i: (0, j)),
                  pl.BlockSpec((s, HEAD_PAD), lambda j, i: (0, j))],
        out_specs=(pl.BlockSpec((tq, HEAD_PAD), lambda j, i: (i, j)),) * 2,
        compiler_params=_params(("parallel", "arbitrary")),
    )(q, k, v)


def _attn_prep(o, do, lse):
    s = o.shape[0]
    tb = min(TB_PROJ, s)
    npair = N_TOK_HEADS // 2

    def body(o_ref, do_ref, lse_ref, dob_ref, st_ref):
        lane = _iota((tb, HEAD_PAD), 1)
        do = do_ref[...]
        dob_ref[...] = do.astype(BF16)
        prod = do * o_ref[...]
        for j in range(npair):
            sl = slice(HEAD_PAD * j, HEAD_PAD * (j + 1))
            pj = prod[:, sl]
            da = jnp.sum(jnp.where(lane < 64, pj, 0.0), axis=-1, keepdims=True)
            db = jnp.sum(jnp.where(lane >= 64, pj, 0.0), axis=-1, keepdims=True)
            la = lse_ref[:, HEAD_PAD * j:HEAD_PAD * j + 1]
            lb = lse_ref[:, HEAD_PAD * j + 64:HEAD_PAD * j + 65]
            xt = jnp.where(lane == 0, la, jnp.where(lane == 1, lb, jnp.where(lane == 2, da, jnp.where(lane == 3, db, 0.0))))
            st_ref[j] = xt.T[0:8, :]

    return pl.pallas_call(
        body, name="attn_prep", grid=(s // tb,),
        out_shape=(jax.ShapeDtypeStruct((s, TOK_W), BF16), jax.ShapeDtypeStruct((npair, 8, s), F32)),
        in_specs=[_rows(tb, TOK_W)] * 3,
        out_specs=(_rows(tb, TOK_W), pl.BlockSpec((npair, 8, tb), lambda i: (0, 0, i))),
        compiler_params=_params(("parallel",)),
    )(o, do, lse)


def _attn_bwd(q, k, v, dob, stats):
    s = q.shape[0]
    tq = min(TQ_ATT, s)
    nq = s // tq

    def body(k_ref, v_ref, q_ref, do_ref, st_ref, dq_ref, dk_ref, dv_ref):
        jk = pl.program_id(1)

        @pl.when(jk == 0)
        def _():
            dq_ref[...] = jnp.zeros_like(dq_ref)

        v = v_ref[...]
        lane = _iota((tq, HEAD_PAD), 1)
        kpos, qpos = _iota((tq, tq), 0), _iota((tq, tq), 1)
        dv_acc = jnp.zeros((tq, HEAD_PAD), F32)
        dks = []
        for hh in range(2):
            hs = slice(HEAD_PAD * hh, HEAD_PAD * (hh + 1))
            kh = k_ref[:, hs]

            def tile(i, carry, masked, hh=hh, hs=hs, kh=kh):
                dk_acc, dv_acc = carry
                st = pl.multiple_of(i * tq, tq)
                qh = q_ref[pl.ds(st, tq), hs]
                do = do_ref[pl.ds(st, tq), :]
                dom = jnp.where((lane < 64) if hh == 0 else (lane >= 64), do, jnp.zeros_like(do))
                lse = st_ref[0, hh:hh + 1, pl.ds(st, tq)]
                dlt = st_ref[0, 2 + hh:3 + hh, pl.ds(st, tq)]
                sc = _mm_nt(kh, qh) * ATT_SCALE
                if masked:
                    sc = jnp.where(kpos <= qpos, sc, -jnp.inf)
                pt = jnp.exp(sc - lse)
                dv_acc = dv_acc + _mm(pt, dom)
                dpt = _mm_nt(v, dom)
                dst = (pt * (dpt - dlt) * ATT_SCALE).astype(BF16)
                dk_acc = dk_acc + _mm(dst, qh)
                dq_ref[pl.ds(st, tq), hs] += _mm_tn(dst, kh)
                return dk_acc, dv_acc

            carry = tile(jk, (jnp.zeros((tq, HEAD_PAD), F32), dv_acc), True)
            dk_acc, dv_acc = lax.fori_loop(jk + 1, nq, functools.partial(tile, masked=False), carry)
            dks.append(dk_acc)
        dk_ref[...] = jnp.concatenate(dks, axis=1)
        dv_ref[...] = dv_acc

    return pl.pallas_call(
        body, name="attn_bwd", grid=(N_TOK_HEADS // 2, nq),
        out_shape=(jax.ShapeDtypeStruct((s, QK_W), F32), jax.ShapeDtypeStruct((s, QK_W), F32),
                   jax.ShapeDtypeStruct((s, TOK_W), F32)),
        in_specs=[pl.BlockSpec((tq, 2 * HEAD_PAD), lambda j, i: (i, j)),
                  pl.BlockSpec((tq, HEAD_PAD), lambda j, i: (i, j)),
                  pl.BlockSpec((s, 2 * HEAD_PAD), lambda j, i: (0, j)),
                  pl.BlockSpec((s, HEAD_PAD), lambda j, i: (0, j)),
                  pl.BlockSpec((1, 8, s), lambda j, i: (j, 0, 0))],
        out_specs=(pl.BlockSpec((s, 2 * HEAD_PAD), lambda j, i: (0, j)),
                   pl.BlockSpec((tq, 2 * HEAD_PAD), lambda j, i: (i, j)),
                   pl.BlockSpec((tq, HEAD_PAD), lambda j, i: (i, j))),
        compiler_params=_params(("parallel", "arbitrary")),
    )(k, v, q, dob, stats)


def _mem_kv(mem, wmem):
    def body(m_ref, w_ref, o_ref):
        o_ref[0] = _mm(m_ref[...], w_ref[0]).astype(BF16)

    return pl.pallas_call(
        body, name="mem_kv", grid=(2,), out_shape=jax.ShapeDtypeStruct((2, MEM_LEN, 512), BF16),
        in_specs=[_const((MEM_LEN, 1024)), pl.BlockSpec((1, 1024, 512), lambda l: (l, 0, 0))],
        out_specs=pl.BlockSpec((1, MEM_LEN, 512), lambda l: (l, 0, 0)),
        compiler_params=_params(("parallel",)),
    )(mem, wmem)


def _mem_kv_bwd(mem, dmemkv):
    def body(m_ref, d_ref, o_ref):
        o_ref[0] = _mm_tn(m_ref[...], d_ref[0])

    return pl.pallas_call(
        body, name="mem_kv_bwd", grid=(2,), out_shape=jax.ShapeDtypeStruct((2, 1024, 512), F32),
        in_specs=[_const((MEM_LEN, 1024)), pl.BlockSpec((1, MEM_LEN, 512), lambda l: (l, 0, 0))],
        out_specs=pl.BlockSpec((1, 1024, 512), lambda l: (l, 0, 0)),
        compiler_params=_params(("parallel",)),
    )(mem, dmemkv)


def _head_mask(lane, sub):
    return (lane < 64) if sub == 0 else (lane >= 64)


def _mem_attn(qm, kv):
    tb = qm.shape[0]
    lane = _iota((tb, HEAD_PAD), 1)
    outs, ps = [], []
    for pp in range(2):
        qp = qm[:, HEAD_PAD * pp:HEAD_PAD * (pp + 1)]
        kp = kv[:, HEAD_PAD * pp:HEAD_PAD * (pp + 1)]
        vp = kv[:, MEM_W + HEAD_PAD * pp:MEM_W + HEAD_PAD * (pp + 1)]
        pair = None
        for sub in range(2):
            qh = jnp.where(_head_mask(lane, sub), qp, jnp.zeros_like(qp))
            sc = _mm_nt(qh, kp) * 0.125
            e = jnp.exp(sc - jnp.max(sc, axis=-1, keepdims=True))
            p = e / jnp.sum(e, axis=-1, keepdims=True)
            o = _mm(p, vp)
            ps.append(p)
            pair = o if sub == 0 else jnp.where(lane < 64, pair, o)
        outs.append(pair)
    return jnp.concatenate(outs, axis=1), ps


def _mem_attn_bwd(dmo, qm, kv, ps):
    tb = qm.shape[0]
    lane = _iota((tb, HEAD_PAD), 1)
    dqs, dks, dvs = [], [], []
    for pp in range(2):
        qp = qm[:, HEAD_PAD * pp:HEAD_PAD * (pp + 1)]
        kp = kv[:, HEAD_PAD * pp:HEAD_PAD * (pp + 1)]
        vp = kv[:, MEM_W + HEAD_PAD * pp:MEM_W + HEAD_PAD * (pp + 1)]
        dop = dmo[:, HEAD_PAD * pp:HEAD_PAD * (pp + 1)]
        dq_pair, dk_pair, dv_pair = None, None, None
        for sub in range(2):
            msk = _head_mask(lane, sub)
            p = ps[2 * pp + sub]
            qh = jnp.where(msk, qp, jnp.zeros_like(qp))
            doh = jnp.where(msk, dop, 0.0).astype(BF16)
            dv = _mm_tn(p, doh)
            dp = _mm_nt(doh, vp)
            ds = (p * (dp - jnp.sum(dp * p, axis=-1, keepdims=True)) * 0.125).astype(BF16)
            dq = _mm(ds, kp)
            dk = _mm_tn(ds, qh)
            if sub == 0:
                dq_pair, dk_pair, dv_pair = dq, dk, dv
            else:
                dq_pair = jnp.where(lane < 64, dq_pair, dq)
                dk_pair, dv_pair = dk_pair + dk, dv_pair + dv
        dqs.append(dq_pair)
        dks.append(dk_pair)
        dvs.append(dv_pair)
    return jnp.concatenate(dqs, axis=1), jnp.concatenate(dks + dvs, axis=1)


def _mix_core(tok, gate, qm, kv, wout, h_in, g, b):
    mem_out, ps = _mem_attn(qm, kv)
    cat = jnp.concatenate([tok, mem_out], axis=1)
    sg = jax.nn.sigmoid(gate)
    sl = gate * sg
    y = cat * sl
    r = ALPHA * h_in + _mm(y, wout)
    mu = jnp.mean(r, axis=-1, keepdims=True)
    xc = r - mu
    rstd = lax.rsqrt(jnp.mean(xc * xc, axis=-1, keepdims=True) + NORM_EPS)
    xh = xc * rstd
    return xh * g + b, (ps, cat, sg, sl, y, xh, rstd)


def _mix_fwd(tok, gate, qm, kv, wout, h_in, g, b):
    s = tok.shape[0]
    tb = min(TB_MIX, s)

    def body(tok_ref, gate_ref, qm_ref, kv_ref, w_ref, h_ref, g_ref, b_ref, o_ref):
        o_ref[...], _ = _mix_core(tok_ref[...], gate_ref[...], qm_ref[...], kv_ref[...], w_ref[...], h_ref[...],
                                  g_ref[...], b_ref[...])

    return pl.pallas_call(
        body, name="mix_fwd", grid=(s // tb,), out_shape=jax.ShapeDtypeStruct((s, 1024), F32),
        in_specs=[_rows(tb, TOK_W), _rows(tb, 1024), _rows(tb, MEM_W), _const((MEM_LEN, 512)), _const((1024, 1024)),
                  _rows(tb, 1024), _const((1, 1024)), _const((1, 1024))],
        out_specs=_rows(tb, 1024), compiler_params=_params(("parallel",)),
    )(tok, gate, qm, kv, wout, h_in, g, b)


def _mix_bwd(tok, gate, qm, kv, wout, h_in, g, b, up, from_loss):
    s = tok.shape[0]
    tb = min(TB_MIX, s)

    def body(tok_ref, gate_ref, qm_ref, kv_ref, w_ref, h_ref, g_ref, b_ref, up_ref,
             dres_ref, dtok_ref, dgate_ref, dqm_ref, dw_ref, dkv_ref, dg_ref, db_ref, loss_ref):
        @pl.when(pl.program_id(0) == 0)
        def _():
            for r in (dw_ref, dkv_ref, dg_ref, db_ref, loss_ref):
                r[...] = jnp.zeros_like(r)

        gate, qm, kv, wout, g = gate_ref[...], qm_ref[...], kv_ref[...], w_ref[...], g_ref[...]
        h_out, (ps, cat, sg, sl, y, xh, rstd) = _mix_core(tok_ref[...], gate, qm, kv, wout, h_ref[...], g, b_ref[...])
        if from_loss:
            diff = h_out - up_ref[...]
            loss_ref[...] += 0.5 * jnp.sum(jnp.mean(diff * diff, axis=-1, keepdims=True), axis=0, keepdims=True)
            dh = diff * (1.0 / D_MODEL)
        else:
            dh = up_ref[...]
        dg_ref[...] += jnp.sum(dh * xh, axis=0, keepdims=True)
        db_ref[...] += jnp.sum(dh, axis=0, keepdims=True)
        dxh = dh * g
        dr = rstd * (dxh - jnp.mean(dxh, axis=-1, keepdims=True) - xh * jnp.mean(dxh * xh, axis=-1, keepdims=True))
        dres_ref[...] = ALPHA * dr
        drb = dr.astype(BF16)
        dy = _mm_nt(drb, wout)
        dw_ref[...] += _mm_tn(y, drb)
        dcat = dy * sl
        dgate_ref[...] = dy * cat * (sg * (1.0 + gate * (1.0 - sg)))
        dtok_ref[...] = dcat[:, :TOK_W]
        dqm, dkv = _mem_attn_bwd(dcat[:, TOK_W:], qm, kv, ps)
        dqm_ref[...] = dqm
        dkv_ref[...] += dkv

    outs = (jax.ShapeDtypeStruct((s, 1024), F32), jax.ShapeDtypeStruct((s, TOK_W), F32),
            jax.ShapeDtypeStruct((s, 1024), F32), jax.ShapeDtypeStruct((s, MEM_W), F32),
            jax.ShapeDtypeStruct((1024, 1024), F32), jax.ShapeDtypeStruct((MEM_LEN, 512), F32),
            jax.ShapeDtypeStruct((1, 1024), F32), jax.ShapeDtypeStruct((1, 1024), F32),
            jax.ShapeDtypeStruct((1, 1), F32))
    return pl.pallas_call(
        body, name="mix_bwd_loss" if from_loss else "mix_bwd", grid=(s // tb,), out_shape=outs,
        in_specs=[_rows(tb, TOK_W), _rows(tb, 1024), _rows(tb, MEM_W), _const((MEM_LEN, 512)), _const((1024, 1024)),
                  _rows(tb, 1024), _const((1, 1024)), _const((1, 1024)), _rows(tb, 1024)],
        out_specs=(_rows(tb, 1024), _rows(tb, TOK_W), _rows(tb, 1024), _rows(tb, MEM_W), _const((1024, 1024)),
                   _const((MEM_LEN, 512)), _const((1, 1024)), _const((1, 1024)), _const((1, 1))),
        compiler_params=_params(),
    )(tok, gate, qm, kv, wout, h_in, g, b, up)


def _shift_down(u, tail, k):
    if k == 0:
        return u
    r = pltpu.roll(u, k, 0)
    row8 = _iota((8, u.shape[1]), 0)
    head = jnp.where(row8 < k, pltpu.roll(tail, k, 0), r[:8])
    return jnp.concatenate([head, r[8:]], axis=0)


def _shift_up(d, head, k):
    if k == 0:
        return d
    n = d.shape[0]
    r = pltpu.roll(d, n - k, 0)
    row8 = _iota((8, d.shape[1]), 0)
    last = jnp.where(row8 >= 8 - k, pltpu.roll(head, 8 - k, 0), r[n - 8:])
    return jnp.concatenate([r[:n - 8], last], axis=0)


def _scan_down(a, b):
    n = a.shape[0]
    row = _iota(a.shape, 0)
    s = 1
    while s < n:
        ok = row >= s
        a_s = jnp.where(ok, pltpu.roll(a, s, 0), 1.0)
        b_s = jnp.where(ok, pltpu.roll(b, s, 0), 0.0)
        b = a * b_s + b
        a = a * a_s
        s *= 2
    return a, b


def _scan_up(a, b):
    n = a.shape[0]
    row = _iota(a.shape, 0)
    s = 1
    while s < n:
        ok = row < n - s
        a_s = jnp.where(ok, pltpu.roll(a, n - s, 0), 1.0)
        b_s = jnp.where(ok, pltpu.roll(b, n - s, 0), 0.0)
        b = a * b_s + b
        a = a * a_s
        s *= 2
    return a, b


def _neg_expm1(x):
    poly = -x * (1.0 + x * (0.5 + x * (1.0 / 6.0 + x * (1.0 / 24.0 + x * (1.0 / 120.0)))))
    return jnp.where(x > -0.1, poly, 1.0 - jnp.exp(x))


def _softplus(x):
    return jnp.maximum(x, 0.0) + jnp.log(1.0 + jnp.exp(-jnp.abs(x)))


def _lru_gates(u, tail, cw, cb, wr, br, wi, bi, lam):
    us = [_shift_down(u, tail, k) for k in range(4)]
    xc = cb + us[3] * cw[0:1] + us[2] * cw[1:2] + us[1] * cw[2:3] + us[0] * cw[3:4]
    xb = xc.astype(BF16)
    pre_r = jnp.concatenate([_mm(xb[:, 256 * g:256 * (g + 1)], wr[g]) for g in range(3)], axis=1) + br
    pre_i = jnp.concatenate([_mm(xb[:, 256 * g:256 * (g + 1)], wi[g]) for g in range(3)], axis=1) + bi
    rg, ig = jax.nn.sigmoid(pre_r), jax.nn.sigmoid(pre_i)
    clam = -LRU_C * _softplus(-lam)
    la = clam * rg
    a = jnp.exp(la)
    mm = jnp.sqrt(_neg_expm1(2.0 * la))
    return us, xc, xb, rg, ig, clam, la, a, mm


def _lru_fwd(h, win, cw, cb, wr, br, wi, bi, lam):
    s = h.shape[0]
    tb = min(TB_LRU, s)

    def body(h_ref, win_ref, cw_ref, cb_ref, wr_ref, br_ref, wi_ref, bi_ref, lam_ref,
             u_ref, gate_ref, qm_ref, hs_ref, tail_sc, carry_sc):
        @pl.when(pl.program_id(0) == 0)
        def _():
            tail_sc[...] = jnp.zeros_like(tail_sc)
            carry_sc[...] = jnp.zeros_like(carry_sc)

        z = _mm(h_ref[...], win_ref[...])
        u = z[:, :TOK_W]
        u_ref[...] = u
        gate_ref[...] = z[:, TOK_W:TOK_W + 1024]
        qm_ref[...] = z[:, TOK_W + 1024:].astype(BF16)
        _, xc, _, _, ig, _, _, a, mm = _lru_gates(u, tail_sc[...], cw_ref[...], cb_ref[...], wr_ref[...], br_ref[...],
                                                 wi_ref[...], bi_ref[...], lam_ref[...])
        big_a, big_b = _scan_down(a, mm * (ig * xc))
        hs = big_a * carry_sc[0:1, :] + big_b
        hs_ref[...] = hs
        tail_sc[...] = u[tb - 8:, :]
        carry_sc[...] = jnp.broadcast_to(hs[tb - 1:tb, :], carry_sc.shape)

    outs = (jax.ShapeDtypeStruct((s, TOK_W), F32), jax.ShapeDtypeStruct((s, 1024), F32),
            jax.ShapeDtypeStruct((s, MEM_W), BF16), jax.ShapeDtypeStruct((s, TOK_W), F32))
    return pl.pallas_call(
        body, name="lru_fwd", grid=(s // tb,), out_shape=outs,
        in_specs=[_rows(tb, 1024), _const((1024, 2048)), _const((4, TOK_W)), _const((1, TOK_W)),
                  _const((3, 256, 256)), _const((1, TOK_W)), _const((3, 256, 256)), _const((1, TOK_W)),
                  _const((1, TOK_W))],
        out_specs=(_rows(tb, TOK_W), _rows(tb, 1024), _rows(tb, MEM_W), _rows(tb, TOK_W)),
        scratch_shapes=[pltpu.VMEM((8, TOK_W), F32), pltpu.VMEM((8, TOK_W), F32)],
        compiler_params=_params(),
    )(h, win, cw, cb, wr, br, wi, bi, lam)


def _lru_bwd(dhs, dgate, dqm, dres, h, u, hs, win, cw, cb, wr, br, wi, bi, lam):
    s = h.shape[0]
    tb = min(TB_LRU, s)
    nb = s // tb

    def rev(w):
        return pl.BlockSpec((tb, w), lambda i: (nb - 1 - i, 0))

    def prev_tail(w):
        return pl.BlockSpec((8, w), lambda i: (jnp.maximum((nb - 1 - i) * (tb // 8) - 1, 0), 0))

    def body(dhs_ref, dgate_ref, dqm_ref, dres_ref, h_ref, u_ref, hs_ref, ut_ref, hst_ref, win_ref, cw_ref, cb_ref,
             wr_ref, br_ref, wi_ref, bi_ref, lam_ref,
             dh_ref, dwin_ref, dcw_ref, dcb_ref, dwr_ref, dbr_ref, dwi_ref, dbi_ref, dlam_ref, ecar_sc, dxc_sc):
        i = pl.program_id(0)

        @pl.when(i == 0)
        def _():
            for r in (dwin_ref, dcw_ref, dcb_ref, dwr_ref, dbr_ref, dwi_ref, dbi_ref, dlam_ref, ecar_sc, dxc_sc):
                r[...] = jnp.zeros_like(r)

        first = (i == nb - 1)
        u = u_ref[...]
        utail = jnp.where(first, 0.0, ut_ref[...])
        hstail = jnp.where(first, 0.0, hst_ref[...])
        cw, wr, wi, lam = cw_ref[...], wr_ref[...], wi_ref[...], lam_ref[...]
        us, xc, xb, rg, ig, clam, la, a, mm = _lru_gates(u, utail, cw, cb_ref[...], wr, br_ref[...], wi, bi_ref[...], lam)
        row = _iota(a.shape, 0)
        a_next = jnp.where(row < tb - 1, pltpu.roll(a, tb - 1, 0), 1.0)
        big_a, big_b = _scan_up(a_next, dhs_ref[...])
        e = big_a * ecar_sc[0:1, :] + big_b
        ecar_sc[...] = jnp.broadcast_to(a[0:1, :] * e[0:1, :], ecar_sc.shape)
        hs_prev = _shift_down(hs_ref[...], hstail, 1)
        da = e * hs_prev
        ix = ig * xc
        dmm = e * ix
        dix = e * mm
        dla = da * a - dmm * (a * a) / mm
        dlam_ref[...] += jnp.sum(dla * rg, axis=0, keepdims=True)
        dpr = (dla * clam) * rg * (1.0 - rg)
        dpi = (dix * xc) * ig * (1.0 - ig)
        dbr_ref[...] += jnp.sum(dpr, axis=0, keepdims=True)
        dbi_ref[...] += jnp.sum(dpi, axis=0, keepdims=True)
        dprb, dpib = dpr.astype(BF16), dpi.astype(BF16)
        dxc_g = []
        for g in range(3):
            sl = slice(256 * g, 256 * (g + 1))
            dwr_ref[g] += _mm_tn(xb[:, sl], dprb[:, sl])
            dwi_ref[g] += _mm_tn(xb[:, sl], dpib[:, sl])
            dxc_g.append(_mm_nt(dprb[:, sl], wr[g]) + _mm_nt(dpib[:, sl], wi[g]))
        dxc = dix * ig + jnp.concatenate(dxc_g, axis=1)
        dcb_ref[...] += jnp.sum(dxc, axis=0, keepdims=True)
        dcw_ref[...] += jnp.concatenate([jnp.sum(dxc * us[3 - tap], axis=0, keepdims=True) for tap in range(4)], axis=0)
        head = dxc_sc[...]
        du = dxc * cw[3:4]
        for k in range(1, 4):
            du = du + _shift_up(dxc, head, k) * cw[3 - k:4 - k]
        dxc_sc[...] = dxc[:8, :]
        dz = jnp.concatenate([du, dgate_ref[...], dqm_ref[...]], axis=1).astype(BF16)
        dh_ref[...] = _mm_nt(dz, win_ref[...]) + dres_ref[...]
        dwin_ref[...] += _mm_tn(h_ref[...], dz)

        @pl.when(i == nb - 1)
        def _():
            dlam_ref[...] = dlam_ref[...] * (LRU_C * jax.nn.sigmoid(-lam))

    outs = (jax.ShapeDtypeStruct((s, 1024), F32), jax.ShapeDtypeStruct((1024, 2048), F32),
            jax.ShapeDtypeStruct((4, TOK_W), F32), jax.ShapeDtypeStruct((1, TOK_W), F32),
            jax.ShapeDtypeStruct((3, 256, 256), F32), jax.ShapeDtypeStruct((1, TOK_W), F32),
            jax.ShapeDtypeStruct((3, 256, 256), F32), jax.ShapeDtypeStruct((1, TOK_W), F32),
            jax.ShapeDtypeStruct((1, TOK_W), F32))
    return pl.pallas_call(
        body, name="lru_bwd", grid=(nb,), out_shape=outs,
        in_specs=[rev(TOK_W), rev(1024), rev(MEM_W), rev(1024), rev(1024), rev(TOK_W), rev(TOK_W),
                  prev_tail(TOK_W), prev_tail(TOK_W),
                  _const((1024, 2048)), _const((4, TOK_W)), _const((1, TOK_W)), _const((3, 256, 256)),
                  _const((1, TOK_W)), _const((3, 256, 256)), _const((1, TOK_W)), _const((1, TOK_W))],
        out_specs=(rev(1024), _const((1024, 2048)), _const((4, TOK_W)), _const((1, TOK_W)), _const((3, 256, 256)),
                   _const((1, TOK_W)), _const((3, 256, 256)), _const((1, TOK_W)), _const((1, TOK_W))),
        scratch_shapes=[pltpu.VMEM((8, TOK_W), F32), pltpu.VMEM((8, TOK_W), F32)],
        compiler_params=_params(),
    )(dhs, dgate, dqm, dres, h, u, hs, u, hs, win, cw, cb, wr, br, wi, bi, lam)


def _adamw(name, w, g, m, v):
    rows, cols = w.shape
    tb = 256 if rows % 256 == 0 else rows

    def body(w_ref, g_ref, m_ref, v_ref, d_ref, nm_ref, nv_ref):
        g = g_ref[...]
        nm = ADAM_B1 * m_ref[...] + (1.0 - ADAM_B1) * g
        nv = ADAM_B2 * v_ref[...] + (1.0 - ADAM_B2) * (g * g)
        m_hat = nm / (1.0 - ADAM_B1 ** ADAM_STEP)
        v_hat = nv / (1.0 - ADAM_B2 ** ADAM_STEP)
        d_ref[...] = -ADAM_LR * (m_hat / (jnp.sqrt(v_hat) + ADAM_EPS) + ADAM_WD * w_ref[...])
        nm_ref[...] = nm
        nv_ref[...] = nv

    shp = jax.ShapeDtypeStruct((rows, cols), F32)
    return pl.pallas_call(
        body, name="adamw_" + name, grid=(rows // tb,), out_shape=(shp, shp, shp),
        in_specs=[_rows(tb, cols)] * 4, out_specs=(_rows(tb, cols),) * 3,
        compiler_params=_params(("parallel",)),
    )(w, g, m, v)


def _add2(a, b):
    rows = a.shape[0]
    tb = rows // 8 if rows % 64 == 0 else rows

    def body(a_ref, b_ref, o_ref):
        o_ref[...] = a_ref[...] + b_ref[...]

    return pl.pallas_call(
        body, name="add_sibling", grid=(rows // tb,), out_shape=jax.ShapeDtypeStruct(a.shape, F32),
        in_specs=[_rows(tb, 128)] * 2, out_specs=_rows(tb, 128), compiler_params=_params(("parallel",)),
    )(a, b)


def _sum4(r):
    rows = r.shape[1]
    tb = rows // 8 if rows % 64 == 0 else rows

    def body(r_ref, o_ref):
        o_ref[...] = ((r_ref[0] + r_ref[1]) + r_ref[2]) + r_ref[3]

    return pl.pallas_call(
        body, name="sum_chips", grid=(rows // tb,), out_shape=jax.ShapeDtypeStruct((rows, 128), F32),
        in_specs=[pl.BlockSpec((4, tb, 128), lambda i: (0, i, 0))], out_specs=_rows(tb, 128),
        compiler_params=_params(("parallel",)),
    )(r)


_ANY = pl.BlockSpec(memory_space=pl.ANY)


def _place():
    x, y, c = lax.axis_index("x"), lax.axis_index("y"), lax.axis_index("c")
    return x, y, c, [(1 - x, y), (x, 1 - y), (1 - x, 1 - y)]


def _remote(src, dst, ssem, rsem, to):
    return pltpu.make_async_remote_copy(src_ref=src, dst_ref=dst, send_sem=ssem, recv_sem=rsem, device_id=to,
                                        device_id_type=MESH)


def _gather_shards(wsh):
    _, hh, _ = wsh.shape

    def body(w_ref, out_ref, ssems, rsems, lsem):
        x, y, c, chips = _place()
        t = 2 * x + y
        mine = pltpu.make_async_copy(w_ref, out_ref.at[t], lsem)
        mine.start()
        first = [_remote(w_ref.at[c], out_ref.at[t, c], ssems.at[j], rsems.at[j], (cx, cy, c))
                 for j, (cx, cy) in enumerate(chips)]
        for cp in first:
            cp.start()
        passed = []
        for j, (cx, cy) in enumerate(chips):
            got = out_ref.at[2 * cx + cy, c]
            _remote(got, got, ssems.at[j], rsems.at[j], (cx, cy, c)).wait_recv()
            cp = _remote(got, got, ssems.at[3 + j], rsems.at[3 + j], (x, y, 1 - c))
            cp.start()
            passed.append(cp)
        for j, (cx, cy) in enumerate(chips):
            got = out_ref.at[2 * cx + cy, 1 - c]
            _remote(got, got, ssems.at[3 + j], rsems.at[3 + j], (x, y, 1 - c)).wait_recv()
        for cp in first + passed:
            cp.wait_send()
        mine.wait()

    return pl.pallas_call(
        body, name="gather_weights", out_shape=jax.ShapeDtypeStruct((4, 2, hh, 128), wsh.dtype),
        in_specs=[_ANY], out_specs=_ANY,
        scratch_shapes=[pltpu.SemaphoreType.DMA((6,)), pltpu.SemaphoreType.DMA((6,)), pltpu.SemaphoreType.DMA],
    )(wsh)


def _swap_sibling(v):
    def body(v_ref, got_ref, ssem, rsem):
        x, y, c, _ = _place()
        cp = _remote(v_ref, got_ref, ssem, rsem, (x, y, 1 - c))
        cp.start()
        cp.wait()

    return pl.pallas_call(
        body, name="swap_sibling", out_shape=jax.ShapeDtypeStruct(v.shape, v.dtype), in_specs=[_ANY], out_specs=_ANY,
        scratch_shapes=[pltpu.SemaphoreType.DMA, pltpu.SemaphoreType.DMA],
    )(v)


def _scatter_chips(p):
    def body(p_ref, out_ref, ssems, rsems, lsem):
        x, y, c, chips = _place()
        t = 2 * x + y
        mine = pltpu.make_async_copy(p_ref.at[t], out_ref.at[t], lsem)
        mine.start()
        cps = [_remote(p_ref.at[2 * cx + cy], out_ref.at[t], ssems.at[j], rsems.at[j], (cx, cy, c))
               for j, (cx, cy) in enumerate(chips)]
        for cp in cps:
            cp.start()
        for j, (cx, cy) in enumerate(chips):
            got = out_ref.at[2 * cx + cy]
            _remote(got, got, ssems.at[j], rsems.at[j], (cx, cy, c)).wait_recv()
        for cp in cps:
            cp.wait_send()
        mine.wait()

    return pl.pallas_call(
        body, name="scatter_chips", out_shape=jax.ShapeDtypeStruct(p.shape, p.dtype), in_specs=[_ANY], out_specs=_ANY,
        scratch_shapes=[pltpu.SemaphoreType.DMA((3,)), pltpu.SemaphoreType.DMA((3,)), pltpu.SemaphoreType.DMA],
    )(p)


def _share_reduced(piece, hs, rr):
    hh = piece.shape[0]

    def body(t_ref, full_ref, rall_ref, ssems, rsems, lsems):
        x, y, c, _ = _place()
        me = 4 * x + 2 * y + c
        mine_r = t_ref.at[pl.ds(hs, rr)]
        loc = [pltpu.make_async_copy(t_ref, full_ref.at[c], lsems.at[0]),
               pltpu.make_async_copy(mine_r, rall_ref.at[me], lsems.at[1])]
        for cp in loc:
            cp.start()
        sends = [_remote(t_ref, full_ref.at[c], ssems.at[0], rsems.at[0], (x, y, 1 - c))]
        peers = []
        for mask in range(1, 8):
            px = 1 - x if mask & 4 else x
            py = 1 - y if mask & 2 else y
            pc = 1 - c if mask & 1 else c
            peers.append((mask, px, py, pc))
            sends.append(_remote(mine_r, rall_ref.at[me], ssems.at[mask], rsems.at[mask], (px, py, pc)))
        for cp in sends:
            cp.start()
        got = full_ref.at[1 - c]
        _remote(got, got, ssems.at[0], rsems.at[0], (x, y, 1 - c)).wait_recv()
        for mask, px, py, pc in peers:
            got = rall_ref.at[4 * px + 2 * py + pc]
            _remote(got, got, ssems.at[mask], rsems.at[mask], (px, py, pc)).wait_recv()
        for cp in sends:
            cp.wait_send()
        for cp in loc:
            cp.wait()

    return pl.pallas_call(
        body, name="share_reduced",
        out_shape=(jax.ShapeDtypeStruct((2, hh, 128), F32), jax.ShapeDtypeStruct((8, rr, 128), F32)),
        in_specs=[_ANY], out_specs=(_ANY, _ANY),
        scratch_shapes=[pltpu.SemaphoreType.DMA((8,)), pltpu.SemaphoreType.DMA((8,)), pltpu.SemaphoreType.DMA((2,))],
    )(piece)


def _ceil_to(n, m):
    return -(-n // m) * m


def _col_shards(w2d):
    rows, cols = w2d.shape
    return w2d.reshape(rows, 4, cols // 4).transpose(1, 0, 2).reshape(4, rows * (cols // 4))


def _from_col_shards(flat, rows):
    w = flat.shape[1] // rows
    return flat.reshape(4, rows, w).transpose(1, 0, 2).reshape(rows, 4 * w)


def _block_diag4(w):
    eye = jnp.eye(4, dtype=w.dtype)
    return jnp.einsum("gaij,ab->gaibj", w.reshape(3, 4, 64, 64), eye).reshape(3, 256, 256)


def _diag_blocks4(w):
    w5 = w.reshape(3, 4, 64, 4, 64)
    return jnp.stack([w5[:, a, :, a, :] for a in range(4)], axis=1).reshape(12, 64, 64)


def kernel(x, mem, positions, mla_w_in, mla_q_norm, mla_w_uq, mla_kv_norm, mla_w_ukv, lru_w_in, lru_conv_w, lru_conv_b, lru_w_rgate, lru_b_rgate, lru_w_igate, lru_b_igate, lru_lambda, w_mem_kv, w_out, ln_g, ln_b, loss_target, m_mla_w_in, m_mla_q_norm, m_mla_w_uq, m_mla_kv_norm, m_mla_w_ukv, m_lru_w_in, m_lru_conv_w, m_lru_conv_b, m_lru_w_rgate, m_lru_b_rgate, m_lru_w_igate, m_lru_b_igate, m_lru_lambda, m_w_mem_kv, m_w_out, m_ln_g, m_ln_b, v_mla_w_in, v_mla_q_norm, v_mla_w_uq, v_mla_kv_norm, v_mla_w_ukv, v_lru_w_in, v_lru_conv_w, v_lru_conv_b, v_lru_w_rgate, v_lru_b_rgate, v_lru_w_igate, v_lru_b_igate, v_lru_lambda, v_w_mem_kv, v_w_out, v_ln_g, v_ln_b):
    s = x.shape[1]
    c_idx = lax.axis_index("c")
    x2, mem2, tgt2 = x[0], mem[0], loss_target[0]

    small = jnp.concatenate([lru_conv_w[0].reshape(-1), lru_conv_b[0], lru_b_rgate[0], lru_b_igate[0], lru_lambda[0]])
    parts = [mla_w_in[0].reshape(-1), mla_w_uq[0].reshape(-1), mla_w_ukv[0].reshape(-1), lru_w_in[0].reshape(-1),
             w_mem_kv.reshape(-1), w_out.reshape(-1)]
    sizes = [p.shape[0] for p in parts] + [2 * small.shape[0]]
    flat = jnp.concatenate([p.astype(BF16) for p in parts] + [lax.bitcast_convert_type(small, BF16).reshape(-1)])
    hw = _ceil_to(_ceil_to(flat.shape[0], 128) // 128, 32) // 2
    flat = jnp.pad(flat, (0, 2 * hw * 128 - flat.shape[0]))
    wall = _gather_shards(flat.reshape(2, hw, 128)).reshape(4, 2 * hw * 128)
    offs = [0]
    for n in sizes:
        offs.append(offs[-1] + n)
    seg = [wall[:, offs[i]:offs[i + 1]] for i in range(len(sizes))]
    win0 = _from_col_shards(seg[0], 1024)
    wuq = _from_col_shards(seg[1], Q_LORA)
    wukv = _from_col_shards(seg[2], KV_LORA)
    win1 = _from_col_shards(seg[3], 1024)
    wmem = seg[4].reshape(4, 2, 256, 512).transpose(1, 0, 2, 3).reshape(2, 1024, 512)
    wout = seg[5].reshape(4, 2, 256, 1024).transpose(1, 0, 2, 3).reshape(2, 1024, 1024)
    small_all = lax.bitcast_convert_type(seg[6].reshape(4, -1, 2), F32)
    cw = small_all[:, :768].reshape(4, 4, 192).transpose(1, 0, 2).reshape(4, TOK_W)
    cb, br, bi, lam = (small_all[:, 768 + 192 * k:960 + 192 * k].reshape(1, TOK_W) for k in range(4))

    loss, dx, (dwin0, dwuq, dwukv, dwin1, dcw, dcb, dbr, dbi, dlam, dwmem, dwout, dgq, dgkv, dwr, dwi, dg, db) = \
        _local_step(x2, mem2, positions.reshape(s, 1), tgt2, win0, wuq, wukv, win1, wmem, wout, cw, cb, br, bi, lam,
                    mla_q_norm, mla_kv_norm, lru_w_rgate[0], lru_w_igate[0], ln_g, ln_b)

    sharded = [
        ("mla_w_in", _col_shards(dwin0), (1, 1024, 488)),
        ("mla_w_uq", _col_shards(dwuq), (1, Q_LORA, 288)),
        ("mla_w_ukv", _col_shards(dwukv), (1, KV_LORA, 384)),
        ("lru_w_in", _col_shards(dwin1), (1, 1024, 512)),
        ("lru_conv_w", _col_shards(dcw), (1, 4, 192)),
        ("lru_conv_b", _col_shards(dcb), (1, 192)),
        ("lru_b_rgate", _col_shards(dbr), (1, 192)),
        ("lru_b_igate", _col_shards(dbi), (1, 192)),
        ("lru_lambda", _col_shards(dlam), (1, 192)),
        ("w_mem_kv", dwmem.reshape(2, 4, 256 * 512).transpose(1, 0, 2).reshape(4, -1), (2, 256, 512)),
        ("w_out", dwout.reshape(2, 4, 256 * 1024).transpose(1, 0, 2).reshape(4, -1), (2, 256, 1024)),
    ]
    replicated = [
        ("mla_q_norm", dgq.reshape(-1), (1, Q_LORA)),
        ("mla_kv_norm", dgkv.reshape(-1), (1, KV_LORA)),
        ("lru_w_rgate", dwr.reshape(-1), (1, 12, 64, 64)),
        ("lru_w_igate", dwi.reshape(-1), (1, 12, 64, 64)),
        ("ln_g", dg.reshape(-1), (2, 1024)),
        ("ln_b", db.reshape(-1), (2, 1024)),
    ]

    sh = jnp.concatenate([g for _, g, _ in sharded], axis=1)
    n_sh = sh.shape[1]
    hs_rows = _ceil_to(_ceil_to(n_sh, 128) // 128, 16) // 2
    sh = jnp.pad(sh, ((0, 0), (0, 2 * hs_rows * 128 - n_sh))).reshape(4, 2, hs_rows, 128)
    rp = jnp.concatenate([g for _, g, _ in replicated])
    n_rp = rp.shape[0]
    rr = _ceil_to(_ceil_to(n_rp, 128) // 128, 64) // 8
    rp = jnp.pad(rp, (0, 8 * rr * 128 - n_rp)).reshape(4, 2, rr, 128)
    gbuf = jnp.concatenate([sh, rp], axis=2)
    hh = hs_rows + rr
    mine = lax.dynamic_index_in_dim(gbuf, c_idx, axis=1, keepdims=False)
    other = lax.dynamic_index_in_dim(gbuf, 1 - c_idx, axis=1, keepdims=False)
    chip_sum = _add2(mine.reshape(4 * hh, 128), _swap_sibling(other).reshape(4 * hh, 128)).reshape(4, hh, 128)
    piece = _sum4(_scatter_chips(chip_sum))
    full, rall = _share_reduced(piece, hs_rows, rr)
    g_sh = full[:, :hs_rows].reshape(-1)
    g_rp = rall.reshape(-1)

    weights = dict(mla_w_in=mla_w_in, mla_q_norm=mla_q_norm, mla_w_uq=mla_w_uq, mla_kv_norm=mla_kv_norm,
                   mla_w_ukv=mla_w_ukv, lru_w_in=lru_w_in, lru_conv_w=lru_conv_w, lru_conv_b=lru_conv_b,
                   lru_w_rgate=lru_w_rgate, lru_b_rgate=lru_b_rgate, lru_w_igate=lru_w_igate, lru_b_igate=lru_b_igate,
                   lru_lambda=lru_lambda, w_mem_kv=w_mem_kv, w_out=w_out, ln_g=ln_g, ln_b=ln_b)
    m_in = dict(mla_w_in=m_mla_w_in, mla_q_norm=m_mla_q_norm, mla_w_uq=m_mla_w_uq, mla_kv_norm=m_mla_kv_norm,
                mla_w_ukv=m_mla_w_ukv, lru_w_in=m_lru_w_in, lru_conv_w=m_lru_conv_w, lru_conv_b=m_lru_conv_b,
                lru_w_rgate=m_lru_w_rgate, lru_b_rgate=m_lru_b_rgate, lru_w_igate=m_lru_w_igate,
                lru_b_igate=m_lru_b_igate, lru_lambda=m_lru_lambda, w_mem_kv=m_w_mem_kv, w_out=m_w_out, ln_g=m_ln_g,
                ln_b=m_ln_b)
    v_in = dict(mla_w_in=v_mla_w_in, mla_q_norm=v_mla_q_norm, mla_w_uq=v_mla_w_uq, mla_kv_norm=v_mla_kv_norm,
                mla_w_ukv=v_mla_w_ukv, lru_w_in=v_lru_w_in, lru_conv_w=v_lru_conv_w, lru_conv_b=v_lru_conv_b,
                lru_w_rgate=v_lru_w_rgate, lru_b_rgate=v_lru_b_rgate, lru_w_igate=v_lru_w_igate,
                lru_b_igate=v_lru_b_igate, lru_lambda=v_lru_lambda, w_mem_kv=v_w_mem_kv, w_out=v_w_out, ln_g=v_ln_g,
                ln_b=v_ln_b)
    grads, deltas, new_m, new_v = {}, {}, {}, {}
    for group, flat_g in ((sharded, g_sh), (replicated, g_rp)):
        off = 0
        for name, g, shape in group:
            n = math.prod(shape)
            cols = shape[-1]
            g2 = flat_g[off:off + n].reshape(n // cols, cols)
            off += n
            d2, m2, v2 = _adamw(name, weights[name].reshape(g2.shape), g2, m_in[name].reshape(g2.shape),
                                v_in[name].reshape(g2.shape))
            grads[name], deltas[name] = g2.reshape(shape), d2.reshape(shape)
            new_m[name], new_v[name] = m2.reshape(shape), v2.reshape(shape)

    order = ["mla_w_in", "mla_q_norm", "mla_w_uq", "mla_kv_norm", "mla_w_ukv", "lru_w_in", "lru_conv_w", "lru_conv_b",
             "lru_w_rgate", "lru_b_rgate", "lru_w_igate", "lru_b_igate", "lru_lambda", "w_mem_kv", "w_out", "ln_g",
             "ln_b"]
    loss_all = lax.psum(loss, ("x", "y", "c"))
    return (loss_all, dx[None], *[grads[n] for n in order], *[deltas[n] for n in order],
            *[new_m[n] for n in order], *[new_v[n] for n in order])


def _local_step(x2, mem2, pos_col, tgt2, win0, wuq, wukv, win1, wmem, wout, cw, cb, br, bi, lam,
                gq, gkv, w_rgate, w_igate, ln_g, ln_b):
    s = x2.shape[0]
    zpad = jnp.zeros((1024, 64), BF16)
    win0p = jnp.concatenate([win0[:, 672:1696], win0[:, 1696:1952], win0[:, 0:384], win0[:, 384:640],
                             zpad, win0[:, 640:672], zpad[:, :32]], axis=1)
    wuq_p = jnp.pad(wuq.reshape(Q_LORA, 12, 96), ((0, 0), (0, 0), (0, 32))).reshape(Q_LORA, QK_W)
    wukv3 = wukv.reshape(KV_LORA, 12, 128)
    wk_p = jnp.pad(wukv3[:, :, :64], ((0, 0), (0, 0), (0, 64))).reshape(KV_LORA, QK_W)
    wv = wukv3[:, :, 64:].reshape(KV_LORA, TOK_W)
    wr_bd = _block_diag4(w_rgate).astype(BF16)
    wi_bd = _block_diag4(w_igate).astype(BF16)
    half = 16
    inv_freq = ROPE_THETA ** (-jnp.arange(half, dtype=F32) / half)
    inv_lane = jnp.concatenate([jnp.zeros((64,), F32), inv_freq, inv_freq, jnp.zeros((32,), F32)]).reshape(1, HEAD_PAD)

    ctab, satab, sbtab = _rope_tables(pos_col, inv_lane, min(TB_PROJ, s))
    memkv = _mem_kv(mem2, wmem)
    gate0, qm0, cq, ckv, q_p, k_p, v_b = _mla_proj_fwd(x2, win0p, gq, gkv, wuq_p, wk_p, wv,
                                                      ctab, satab, sbtab)
    tok0, lse = _attn_fwd(q_p, k_p, v_b)
    g0, b0, g1, b1 = ln_g[0:1], ln_b[0:1], ln_g[1:2], ln_b[1:2]
    h1 = _mix_fwd(tok0, gate0, qm0, memkv[0], wout[0], x2, g0, b0)
    u1, gate1, qm1, hs1 = _lru_fwd(h1, win1, cw, cb, wr_bd, br, wi_bd, bi, lam)

    dres1, dtok1, dgate1, dqm1, dwout1, dmemkv1, dg1, db1, loss = _mix_bwd(
        hs1, gate1, qm1, memkv[1], wout[1], h1, g1, b1, tgt2, True)
    dh1, dwin1, dcw, dcb, dwr_bd, dbr, dwi_bd, dbi, dlam = _lru_bwd(
        dtok1, dgate1, dqm1, dres1, h1, u1, hs1, win1, cw, cb, wr_bd, br, wi_bd, bi, lam)
    dres0, dtok0, dgate0, dqm0, dwout0, dmemkv0, dg0, db0, _ = _mix_bwd(
        tok0, gate0, qm0, memkv[0], wout[0], x2, g0, b0, dh1, False)
    dob, stats = _attn_prep(tok0, dtok0, lse)
    dq_p, dk_p, dv = _attn_bwd(q_p, k_p, v_b, dob, stats)
    dx, dwin0p, dwuq_p, dwk_p, dwv, dgq, dgkv = _mla_proj_bwd(
        x2, cq, ckv, dq_p, dk_p, dv, dgate0, dqm0, dres0, win0p, gq, gkv, wuq_p, wk_p, wv,
        ctab, satab, sbtab)
    dwmem = _mem_kv_bwd(mem2, jnp.stack([dmemkv0, dmemkv1]))

    dwin0 = jnp.concatenate([dwin0p[:, 1280:1664], dwin0p[:, 1664:1920], dwin0p[:, 1984:2016], dwin0p[:, 0:1024],
                             dwin0p[:, 1024:1280]], axis=1)
    dwuq = dwuq_p.reshape(Q_LORA, 12, 128)[:, :, :96].reshape(Q_LORA, 1152)
    dwukv = jnp.concatenate([dwk_p.reshape(KV_LORA, 12, 128)[:, :, :64], dwv.reshape(KV_LORA, 12, 64)],
                            axis=2).reshape(KV_LORA, 1536)
    dwout = jnp.stack([dwout0, dwout1])
    return loss[0, 0], dx, (dwin0, dwuq, dwukv, dwin1, dcw, dcb, dbr, dbi, dlam, dwmem, dwout, dgq, dgkv,
                            _diag_blocks4(dwr_bd), _diag_blocks4(dwi_bd), jnp.concatenate([dg0, dg1]),
                            jnp.concatenate([db0, db1]))
```

```python
import functools
import math

import jax
import jax.numpy as jnp
from jax import lax
from jax.experimental import pallas as pl
from jax.experimental.pallas import tpu as pltpu

F32, BF16 = jnp.float32, jnp.bfloat16
MESH = pl.DeviceIdType.MESH

D_MODEL = 1024
N_TOK_HEADS = 12
TOK_W = 768
MEM_W = 256
MEM_LEN = 256
Q_LORA, KV_LORA = 384, 256
HEAD_PAD = 128
QK_W = N_TOK_HEADS * HEAD_PAD
ATT_SCALE = 1.0 / math.sqrt(96.0)
ATT_SCALE_LOG2 = ATT_SCALE * math.log2(math.e)
ROPE_THETA = 10000.0
LRU_C = 8.0
ALPHA = 4.0 ** 0.25
NORM_EPS = 1e-6
ADAM_LR, ADAM_B1, ADAM_B2, ADAM_EPS, ADAM_WD, ADAM_STEP = 0.001, 0.9, 0.999, 1e-08, 0.01, 10

TB_PROJ = 512
TB_PROJ_BWD = 256
TB_MIX = 256
TB_LRU = 256
TQ_ATT = 512
TK_ATT = 1024
VMEM_LIMIT = 56 * 1024 * 1024


def _mm(a, b):
    return jnp.dot(a.astype(BF16), b.astype(BF16), preferred_element_type=F32)


def _mm_nt(a, b):
    return lax.dot_general(a.astype(BF16), b.astype(BF16), (((1,), (1,)), ((), ())), preferred_element_type=F32)


def _mm_tn(a, b):
    return lax.dot_general(a.astype(BF16), b.astype(BF16), (((0,), (0,)), ((), ())), preferred_element_type=F32)


def _rows(tb, w):
    return pl.BlockSpec((tb, w), lambda i: (i, 0))


def _const(shape):
    n = len(shape)
    return pl.BlockSpec(shape, lambda i: (0,) * n)


def _params(sem=("arbitrary",)):
    return pltpu.CompilerParams(dimension_semantics=sem, vmem_limit_bytes=VMEM_LIMIT)


def _iota(shape, dim):
    return lax.broadcasted_iota(jnp.int32, shape, dim)


def _rope_tables(pos_col, inv_lane, tb):
    s = pos_col.shape[0]

    def body(pos_ref, inv_ref, c_ref, sa_ref, sb_ref):
        ang = pos_ref[...].astype(F32) * inv_ref[...]
        lane = _iota(ang.shape, 1)
        cs, sn = jnp.cos(ang), jnp.sin(ang)
        c_ref[...] = jnp.where(lane < 64, 1.0, jnp.where(lane < 96, cs, 0.0))
        sa_ref[...] = jnp.where((lane >= 64) & (lane < 80), -sn, 0.0)
        sb_ref[...] = jnp.where((lane >= 80) & (lane < 96), sn, 0.0)

    shp = jax.ShapeDtypeStruct((s, HEAD_PAD), F32)
    return pl.pallas_call(
        body, name="rope_tables", grid=(s // tb,), out_shape=(shp, shp, shp),
        in_specs=[_rows(tb, 1), _const((1, HEAD_PAD))], out_specs=(_rows(tb, HEAD_PAD),) * 3,
        compiler_params=_params(("parallel",)),
    )(pos_col, inv_lane)


def _rope(t, c, sa, sb):
    return t * c + pltpu.roll(t, 112, 1) * sa + pltpu.roll(t, 16, 1) * sb


def _rope_t(d, c, sa, sb):
    return d * c + pltpu.roll(d * sa, 16, 1) + pltpu.roll(d * sb, 112, 1)


def _rms(c, g):
    r = lax.rsqrt(jnp.mean(c * c, axis=-1, keepdims=True) + NORM_EPS)
    xh = c * r
    return xh * g, xh, r


def _mla_proj_fwd(x, win, gq, gkv, wuq, wukv_k, wukv_v, ctab, satab, sbtab):
    s = x.shape[0]
    tb = min(TB_PROJ, s)

    def body(x_ref, win_ref, gq_ref, gkv_ref, wuq_ref, wk_ref, wv_ref, c_ref, sa_ref, sb_ref,
             gate_ref, qm_ref, cq_ref, ckv_ref, q_ref, k_ref, v_ref):
        z = _mm(x_ref[...], win_ref[...])
        gate_ref[...] = z[:, 0:1024]
        qm_ref[...] = z[:, 1024:1280].astype(BF16)
        cq = z[:, 1280:1664]
        ckv = z[:, 1664:1920]
        cq_ref[...] = cq
        ckv_ref[...] = ckv
        c, sa, sb = c_ref[...], sa_ref[...], sb_ref[...]
        nq, _, _ = _rms(cq, gq_ref[...])
        nkv, _, _ = _rms(ckv, gkv_ref[...])
        qf = _mm(nq, wuq_ref[...])
        kf = _mm(nkv, wk_ref[...])
        v_ref[...] = _mm(nkv, wv_ref[...]).astype(BF16)
        kr = _rope(z[:, 1920:2048], c, sa, sb)
        for h in range(N_TOK_HEADS):
            sl = slice(HEAD_PAD * h, HEAD_PAD * (h + 1))
            q_ref[:, sl] = _rope(qf[:, sl], c, sa, sb).astype(BF16)
            k_ref[:, sl] = (kf[:, sl] + kr).astype(BF16)

    outs = (jax.ShapeDtypeStruct((s, 1024), F32), jax.ShapeDtypeStruct((s, MEM_W), BF16),
            jax.ShapeDtypeStruct((s, Q_LORA), F32), jax.ShapeDtypeStruct((s, KV_LORA), F32),
            jax.ShapeDtypeStruct((s, QK_W), BF16), jax.ShapeDtypeStruct((s, QK_W), BF16),
            jax.ShapeDtypeStruct((s, TOK_W), BF16))
    return pl.pallas_call(
        body, name="mla_proj_fwd", grid=(s // tb,), out_shape=outs,
        in_specs=[_rows(tb, 1024), _const((1024, 2048)), _const((1, Q_LORA)), _const((1, KV_LORA)),
                  _const((Q_LORA, QK_W)), _const((KV_LORA, QK_W)), _const((KV_LORA, TOK_W)),
                  _rows(tb, HEAD_PAD), _rows(tb, HEAD_PAD), _rows(tb, HEAD_PAD)],
        out_specs=(_rows(tb, 1024), _rows(tb, MEM_W), _rows(tb, Q_LORA), _rows(tb, KV_LORA),
                   _rows(tb, QK_W), _rows(tb, QK_W), _rows(tb, TOK_W)),
        compiler_params=_params(("parallel",)),
    )(x, win, gq, gkv, wuq, wukv_k, wukv_v, ctab, satab, sbtab)


def _mla_proj_bwd(x, cq, ckv, dq, dk, dv, dgate, dqm, dres, win, gq, gkv, wuq, wukv_k, wukv_v, ctab, satab, sbtab):
    s = x.shape[0]
    tb = min(TB_PROJ_BWD, s)

    def body(x_ref, cq_ref, ckv_ref, dq_ref, dk_ref, dv_ref, dgate_ref, dqm_ref, dres_ref, win_ref, gq_ref, gkv_ref,
             wuq_ref, wk_ref, wv_ref, c_ref, sa_ref, sb_ref,
             dx_ref, dwin_ref, dwuq_ref, dwk_ref, dwv_ref, dgq_ref, dgkv_ref):
        @pl.when(pl.program_id(0) == 0)
        def _():
            for r in (dwin_ref, dwuq_ref, dwk_ref, dwv_ref, dgq_ref, dgkv_ref):
                r[...] = jnp.zeros_like(r)

        c, sa, sb = c_ref[...], sa_ref[...], sb_ref[...]
        lane = _iota((tb, HEAD_PAD), 1)
        gq, gkv = gq_ref[...], gkv_ref[...]
        nq, xhq, rq = _rms(cq_ref[...], gq)
        nkv, xhk, rk = _rms(ckv_ref[...], gkv)
        dkp = dk_ref[...]
        dqs, dkr = [], jnp.zeros((tb, HEAD_PAD), F32)
        for h in range(N_TOK_HEADS):
            sl = slice(HEAD_PAD * h, HEAD_PAD * (h + 1))
            dqs.append(_rope_t(dq_ref[:, sl], c, sa, sb).astype(BF16))
            dkr = dkr + dkp[:, sl]
        dqf = jnp.concatenate(dqs, axis=1)
        dkr = jnp.where((lane >= 64) & (lane < 96), _rope_t(dkr, c, sa, sb), 0.0)
        dvb = dv_ref[...].astype(BF16)
        dkb = dkp.astype(BF16)
        dnq = _mm_nt(dqf, wuq_ref[...])
        dwuq_ref[...] += _mm_tn(nq, dqf)
        dgq_ref[...] += jnp.sum(dnq * xhq, axis=0, keepdims=True)
        dxh = dnq * gq
        dcq = rq * (dxh - xhq * jnp.mean(dxh * xhq, axis=-1, keepdims=True))
        dnkv = _mm_nt(dkb, wk_ref[...]) + _mm_nt(dvb, wv_ref[...])
        nkvb = nkv.astype(BF16)
        dwk_ref[...] += _mm_tn(nkvb, dkb)
        dwv_ref[...] += _mm_tn(nkvb, dvb)
        dgkv_ref[...] += jnp.sum(dnkv * xhk, axis=0, keepdims=True)
        dxh = dnkv * gkv
        dckv = rk * (dxh - xhk * jnp.mean(dxh * xhk, axis=-1, keepdims=True))
        dz = jnp.concatenate([dgate_ref[...], dqm_ref[...], dcq, dckv, dkr], axis=1).astype(BF16)
        dx_ref[...] = _mm_nt(dz, win_ref[...]) + dres_ref[...]
        dwin_ref[...] += _mm_tn(x_ref[...], dz)

    outs = (jax.ShapeDtypeStruct((s, 1024), F32), jax.ShapeDtypeStruct((1024, 2048), F32),
            jax.ShapeDtypeStruct((Q_LORA, QK_W), F32), jax.ShapeDtypeStruct((KV_LORA, QK_W), F32),
            jax.ShapeDtypeStruct((KV_LORA, TOK_W), F32), jax.ShapeDtypeStruct((1, Q_LORA), F32),
            jax.ShapeDtypeStruct((1, KV_LORA), F32))
    return pl.pallas_call(
        body, name="mla_proj_bwd", grid=(s // tb,), out_shape=outs,
        in_specs=[_rows(tb, 1024), _rows(tb, Q_LORA), _rows(tb, KV_LORA), _rows(tb, QK_W), _rows(tb, QK_W),
                  _rows(tb, TOK_W), _rows(tb, 1024), _rows(tb, MEM_W), _rows(tb, 1024),
                  _const((1024, 2048)), _const((1, Q_LORA)), _const((1, KV_LORA)),
                  _const((Q_LORA, QK_W)), _const((KV_LORA, QK_W)), _const((KV_LORA, TOK_W)),
                  _rows(tb, HEAD_PAD), _rows(tb, HEAD_PAD), _rows(tb, HEAD_PAD)],
        out_specs=(_rows(tb, 1024), _const((1024, 2048)), _const((Q_LORA, QK_W)), _const((KV_LORA, QK_W)),
                   _const((KV_LORA, TOK_W)), _const((1, Q_LORA)), _const((1, KV_LORA))),
        compiler_params=_params(),
    )(x, cq, ckv, dq, dk, dv, dgate, dqm, dres, win, gq, gkv, wuq, wukv_k, wukv_v, ctab, satab, sbtab)


def _attn_fwd(q, k, v):
    s = q.shape[0]
    tq = min(TQ_ATT, s)
    tk = min(TK_ATT, s)

    def body(q_ref, k_ref, v_ref, o_ref, lse_ref):
        i = pl.program_id(1)
        nfull = (i * tq) // tk
        lane = _iota((tq, HEAD_PAD), 1)
        qpos = i * tq + _iota((tq, tk), 0)
        col = _iota((tq, tk), 1)

        def head_tile(hh, st, vv, carry, masked):
            hs = slice(HEAD_PAD * hh, HEAD_PAD * (hh + 1))
            m, l, acc = carry
            sc = _mm_nt(q_ref[:, hs], k_ref[pl.ds(st, tk), hs]) * ATT_SCALE_LOG2
            if masked:
                sc = jnp.where(st + col <= qpos, sc, -jnp.inf)
            m_new = jnp.maximum(m, jnp.max(sc, axis=-1, keepdims=True))
            p = jnp.exp2(sc - m_new)
            a = jnp.exp2(m - m_new)
            l = a * l + jnp.sum(p, axis=-1, keepdims=True)
            acc = a * acc + _mm(p, vv)
            return m_new, l, acc

        def tile(j, carry, masked):
            st = pl.multiple_of(j * tk, tk)
            vv = v_ref[pl.ds(st, tk), :]
            return tuple(head_tile(hh, st, vv, carry[hh], masked) for hh in range(2))

        def init():
            return (jnp.full((tq, 1), -jnp.inf, F32), jnp.zeros((tq, 1), F32), jnp.zeros((tq, HEAD_PAD), F32))

        carry = lax.fori_loop(0, nfull, functools.partial(tile, masked=False), (init(), init()))
        (ma, la, acca), (mb, lb, accb) = tile(nfull, carry, True)
        o_ref[...] = jnp.where(lane < 64, acca / la, accb / lb)
        lse_ref[...] = jnp.where(lane < 64, ma + jnp.log2(la), mb + jnp.log2(lb))

    shp = jax.ShapeDtypeStruct((s, TOK_W), F32)
    return pl.pallas_call(
        body, name="attn_fwd", grid=(N_TOK_HEADS // 2, s // tq), out_shape=(shp, shp),
        in_specs=[pl.BlockSpec((tq, 2 * HEAD_PAD), lambda j, i: (i, j)),
                  pl.BlockSpec((s, 2 * HEAD_PAD), lambda j, i: (0, j)),
                  pl.BlockSpec((s, HEAD_PAD), lambda j, i: (0, j))],
        out_specs=(pl.BlockSpec((tq, HEAD_PAD), lambda j, i: (i, j)),) * 2,
        compiler_params=_params(("parallel", "arbitrary")),
    )(q, k, v)


def _attn_prep(o, do, lse):
    s = o.shape[0]
    tb = min(TB_PROJ, s)
    npair = N_TOK_HEADS // 2

    def body(o_ref, do_ref, lse_ref, dob_ref, st_ref):
        lane = _iota((tb, HEAD_PAD), 1)
        do = do_ref[...]
        dob_ref[...] = do.astype(BF16)
        prod = do * o_ref[...]
        for j in range(npair):
            sl = slice(HEAD_PAD * j, HEAD_PAD * (j + 1))
            pj = prod[:, sl]
            da = jnp.sum(jnp.where(lane < 64, pj, 0.0), axis=-1, keepdims=True)
            db = jnp.sum(jnp.where(lane >= 64, pj, 0.0), axis=-1, keepdims=True)
            la = lse_ref[:, HEAD_PAD * j:HEAD_PAD * j + 1]
            lb = lse_ref[:, HEAD_PAD * j + 64:HEAD_PAD * j + 65]
            xt = jnp.where(lane == 0, la, jnp.where(lane == 1, lb, jnp.where(lane == 2, da, jnp.where(lane == 3, db, 0.0))))
            st_ref[j] = xt.T[0:8, :]

    return pl.pallas_call(
        body, name="attn_prep", grid=(s // tb,),
        out_shape=(jax.ShapeDtypeStruct((s, TOK_W), BF16), jax.ShapeDtypeStruct((npair, 8, s), F32)),
        in_specs=[_rows(tb, TOK_W)] * 3,
        out_specs=(_rows(tb, TOK_W), pl.BlockSpec((npair, 8, tb), lambda i: (0, 0, i))),
        compiler_params=_params(("parallel",)),
    )(o, do, lse)


def _attn_bwd(q, k, v, dob, stats):
    s = q.shape[0]
    tq = min(TQ_ATT, s)
    nq = s // tq

    def body(k_ref, v_ref, q_ref, do_ref, st_ref, dq_ref, dk_ref, dv_ref):
        jk = pl.program_id(1)

        @pl.when(jk == 0)
        def _():
            dq_ref[...] = jnp.zeros_like(dq_ref)

        v = v_ref[...]
        lane = _iota((tq, HEAD_PAD), 1)
        kpos, qpos = _iota((tq, tq), 0), _iota((tq, tq), 1)

        def head_tile(hh, st, do, carry, masked):
            hs = slice(HEAD_PAD * hh, HEAD_PAD * (hh + 1))
            dk_acc, dv_acc = carry
            kh = k_ref[:, hs]
            qh = q_ref[pl.ds(st, tq), hs]
            dom = jnp.where((lane < 64) if hh == 0 else (lane >= 64), do, jnp.zeros_like(do))
            lse = st_ref[0, hh:hh + 1, pl.ds(st, tq)]
            dlt = st_ref[0, 2 + hh:3 + hh, pl.ds(st, tq)]
            sc = _mm_nt(kh, qh) * ATT_SCALE_LOG2
            if masked:
                sc = jnp.where(kpos <= qpos, sc, -jnp.inf)
            pt = jnp.exp2(sc - lse)
            dv_acc = dv_acc + _mm(pt, dom)
            dpt = _mm_nt(v, dom)
            dst = (pt * (dpt - dlt) * ATT_SCALE).astype(BF16)
            dk_acc = dk_acc + _mm(dst, qh)
            dq_ref[pl.ds(st, tq), hs] += _mm_tn(dst, kh)
            return dk_acc, dv_acc

        def tile(i, carry, masked):
            st = pl.multiple_of(i * tq, tq)
            do = do_ref[pl.ds(st, tq), :]
            dka, dkb, dv_acc = carry
            dka, dv_acc = head_tile(0, st, do, (dka, dv_acc), masked)
            dkb, dv_acc = head_tile(1, st, do, (dkb, dv_acc), masked)
            return dka, dkb, dv_acc

        zero = jnp.zeros((tq, HEAD_PAD), F32)
        carry = tile(jk, (zero, zero, zero), True)
        dka, dkb, dv_acc = lax.fori_loop(jk + 1, nq, functools.partial(tile, masked=False), carry)
        dk_ref[...] = jnp.concatenate([dka, dkb], axis=1)
        dv_ref[...] = dv_acc

    return pl.pallas_call(
        body, name="attn_bwd", grid=(N_TOK_HEADS // 2, nq),
        out_shape=(jax.ShapeDtypeStruct((s, QK_W), F32), jax.ShapeDtypeStruct((s, QK_W), F32),
                   jax.ShapeDtypeStruct((s, TOK_W), F32)),
        in_specs=[pl.BlockSpec((tq, 2 * HEAD_PAD), lambda j, i: (i, j)),
                  pl.BlockSpec((tq, HEAD_PAD), lambda j, i: (i, j)),
                  pl.BlockSpec((s, 2 * HEAD_PAD), lambda j, i: (0, j)),
                  pl.BlockSpec((s, HEAD_PAD), lambda j, i: (0, j)),
                  pl.BlockSpec((1, 8, s), lambda j, i: (j, 0, 0))],
        out_specs=(pl.BlockSpec((s, 2 * HEAD_PAD), lambda j, i: (0, j)),
                   pl.BlockSpec((tq, 2 * HEAD_PAD), lambda j, i: (i, j)),
                   pl.BlockSpec((tq, HEAD_PAD), lambda j, i: (i, j))),
        compiler_params=_params(("parallel", "arbitrary")),
    )(k, v, q, dob, stats)


def _mem_kv(mem, wmem):
    def body(m_ref, w_ref, o_ref):
        o_ref[0] = _mm(m_ref[...], w_ref[0]).astype(BF16)

    return pl.pallas_call(
        body, name="mem_kv", grid=(2,), out_shape=jax.ShapeDtypeStruct((2, MEM_LEN, 512), BF16),
        in_specs=[_const((MEM_LEN, 1024)), pl.BlockSpec((1, 1024, 512), lambda l: (l, 0, 0))],
        out_specs=pl.BlockSpec((1, MEM_LEN, 512), lambda l: (l, 0, 0)),
        compiler_params=_params(("parallel",)),
    )(mem, wmem)


def _mem_kv_bwd(mem, dmemkv):
    def body(m_ref, d_ref, o_ref):
        o_ref[0] = _mm_tn(m_ref[...], d_ref[0])

    return pl.pallas_call(
        body, name="mem_kv_bwd", grid=(2,), out_shape=jax.ShapeDtypeStruct((2, 1024, 512), F32),
        in_specs=[_const((MEM_LEN, 1024)), pl.BlockSpec((1, MEM_LEN, 512), lambda l: (l, 0, 0))],
        out_specs=pl.BlockSpec((1, 1024, 512), lambda l: (l, 0, 0)),
        compiler_params=_params(("parallel",)),
    )(mem, dmemkv)


def _head_mask(lane, sub):
    return (lane < 64) if sub == 0 else (lane >= 64)


def _mem_attn(qm, kv):
    tb = qm.shape[0]
    lane = _iota((tb, HEAD_PAD), 1)
    outs, ps = [], []
    for pp in range(2):
        qp = qm[:, HEAD_PAD * pp:HEAD_PAD * (pp + 1)]
        kp = kv[:, HEAD_PAD * pp:HEAD_PAD * (pp + 1)]
        vp = kv[:, MEM_W + HEAD_PAD * pp:MEM_W + HEAD_PAD * (pp + 1)]
        pair = None
        for sub in range(2):
            qh = jnp.where(_head_mask(lane, sub), qp, jnp.zeros_like(qp))
            sc = _mm_nt(qh, kp) * 0.125
            e = jnp.exp(sc - jnp.max(sc, axis=-1, keepdims=True))
            p = e / jnp.sum(e, axis=-1, keepdims=True)
            o = _mm(p, vp)
            ps.append(p)
            pair = o if sub == 0 else jnp.where(lane < 64, pair, o)
        outs.append(pair)
    return jnp.concatenate(outs, axis=1), ps


def _mem_attn_bwd(dmo, qm, kv, ps):
    tb = qm.shape[0]
    lane = _iota((tb, HEAD_PAD), 1)
    dqs, dks, dvs = [], [], []
    for pp in range(2):
        qp = qm[:, HEAD_PAD * pp:HEAD_PAD * (pp + 1)]
        kp = kv[:, HEAD_PAD * pp:HEAD_PAD * (pp + 1)]
        vp = kv[:, MEM_W + HEAD_PAD * pp:MEM_W + HEAD_PAD * (pp + 1)]
        dop = dmo[:, HEAD_PAD * pp:HEAD_PAD * (pp + 1)]
        dq_pair, dk_pair, dv_pair = None, None, None
        for sub in range(2):
            msk = _head_mask(lane, sub)
            p = ps[2 * pp + sub]
            qh = jnp.where(msk, qp, jnp.zeros_like(qp))
            doh = jnp.where(msk, dop, 0.0).astype(BF16)
            dv = _mm_tn(p, doh)
            dp = _mm_nt(doh, vp)
            ds = (p * (dp - jnp.sum(dp * p, axis=-1, keepdims=True)) * 0.125).astype(BF16)
            dq = _mm(ds, kp)
            dk = _mm_tn(ds, qh)
            if sub == 0:
                dq_pair, dk_pair, dv_pair = dq, dk, dv
            else:
                dq_pair = jnp.where(lane < 64, dq_pair, dq)
                dk_pair, dv_pair = dk_pair + dk, dv_pair + dv
        dqs.append(dq_pair)
        dks.append(dk_pair)
        dvs.append(dv_pair)
    return jnp.concatenate(dqs, axis=1), jnp.concatenate(dks + dvs, axis=1)


def _mix_core(tok, gate, qm, kv, wout, h_in, g, b):
    mem_out, ps = _mem_attn(qm, kv)
    cat = jnp.concatenate([tok, mem_out], axis=1)
    sg = jax.nn.sigmoid(gate)
    sl = gate * sg
    y = cat * sl
    r = ALPHA * h_in + _mm(y, wout)
    mu = jnp.mean(r, axis=-1, keepdims=True)
    xc = r - mu
    rstd = lax.rsqrt(jnp.mean(xc * xc, axis=-1, keepdims=True) + NORM_EPS)
    xh = xc * rstd
    return xh * g + b, (ps, cat, sg, sl, y, xh, rstd)


def _mix_fwd(tok, gate, qm, kv, wout, h_in, g, b):
    s = tok.shape[0]
    tb = min(TB_MIX, s)

    def body(tok_ref, gate_ref, qm_ref, kv_ref, w_ref, h_ref, g_ref, b_ref, o_ref):
        o_ref[...], _ = _mix_core(tok_ref[...], gate_ref[...], qm_ref[...], kv_ref[...], w_ref[...], h_ref[...],
                                  g_ref[...], b_ref[...])

    return pl.pallas_call(
        body, name="mix_fwd", grid=(s // tb,), out_shape=jax.ShapeDtypeStruct((s, 1024), F32),
        in_specs=[_rows(tb, TOK_W), _rows(tb, 1024), _rows(tb, MEM_W), _const((MEM_LEN, 512)), _const((1024, 1024)),
                  _rows(tb, 1024), _const((1, 1024)), _const((1, 1024))],
        out_specs=_rows(tb, 1024), compiler_params=_params(("parallel",)),
    )(tok, gate, qm, kv, wout, h_in, g, b)


def _mix_bwd(tok, gate, qm, kv, wout, h_in, g, b, up, from_loss):
    s = tok.shape[0]
    tb = min(TB_MIX, s)

    def body(tok_ref, gate_ref, qm_ref, kv_ref, w_ref, h_ref, g_ref, b_ref, up_ref,
             dres_ref, dtok_ref, dgate_ref, dqm_ref, dw_ref, dkv_ref, dg_ref, db_ref, loss_ref):
        @pl.when(pl.program_id(0) == 0)
        def _():
            for r in (dw_ref, dkv_ref, dg_ref, db_ref, loss_ref):
                r[...] = jnp.zeros_like(r)

        gate, qm, kv, wout, g = gate_ref[...], qm_ref[...], kv_ref[...], w_ref[...], g_ref[...]
        h_out, (ps, cat, sg, sl, y, xh, rstd) = _mix_core(tok_ref[...], gate, qm, kv, wout, h_ref[...], g, b_ref[...])
        if from_loss:
            diff = h_out - up_ref[...]
            loss_ref[...] += 0.5 * jnp.sum(jnp.mean(diff * diff, axis=-1, keepdims=True), axis=0, keepdims=True)
            dh = diff * (1.0 / D_MODEL)
        else:
            dh = up_ref[...]
        dg_ref[...] += jnp.sum(dh * xh, axis=0, keepdims=True)
        db_ref[...] += jnp.sum(dh, axis=0, keepdims=True)
        dxh = dh * g
        dr = rstd * (dxh - jnp.mean(dxh, axis=-1, keepdims=True) - xh * jnp.mean(dxh * xh, axis=-1, keepdims=True))
        dres_ref[...] = ALPHA * dr
        drb = dr.astype(BF16)
        dy = _mm_nt(drb, wout)
        dw_ref[...] += _mm_tn(y, drb)
        dcat = dy * sl
        dgate_ref[...] = dy * cat * (sg * (1.0 + gate * (1.0 - sg)))
        dtok_ref[...] = dcat[:, :TOK_W]
        dqm, dkv = _mem_attn_bwd(dcat[:, TOK_W:], qm, kv, ps)
        dqm_ref[...] = dqm
        dkv_ref[...] += dkv

    outs = (jax.ShapeDtypeStruct((s, 1024), F32), jax.ShapeDtypeStruct((s, TOK_W), F32),
            jax.ShapeDtypeStruct((s, 1024), F32), jax.ShapeDtypeStruct((s, MEM_W), F32),
            jax.ShapeDtypeStruct((1024, 1024), F32), jax.ShapeDtypeStruct((MEM_LEN, 512), F32),
            jax.ShapeDtypeStruct((1, 1024), F32), jax.ShapeDtypeStruct((1, 1024), F32),
            jax.ShapeDtypeStruct((1, 1), F32))
    return pl.pallas_call(
        body, name="mix_bwd_loss" if from_loss else "mix_bwd", grid=(s // tb,), out_shape=outs,
        in_specs=[_rows(tb, TOK_W), _rows(tb, 1024), _rows(tb, MEM_W), _const((MEM_LEN, 512)), _const((1024, 1024)),
                  _rows(tb, 1024), _const((1, 1024)), _const((1, 1024)), _rows(tb, 1024)],
        out_specs=(_rows(tb, 1024), _rows(tb, TOK_W), _rows(tb, 1024), _rows(tb, MEM_W), _const((1024, 1024)),
                   _const((MEM_LEN, 512)), _const((1, 1024)), _const((1, 1024)), _const((1, 1))),
        compiler_params=_params(),
    )(tok, gate, qm, kv, wout, h_in, g, b, up)


def _shift_down(u, tail, k):
    if k == 0:
        return u
    r = pltpu.roll(u, k, 0)
    row8 = _iota((8, u.shape[1]), 0)
    head = jnp.where(row8 < k, pltpu.roll(tail, k, 0), r[:8])
    return jnp.concatenate([head, r[8:]], axis=0)


def _shift_up(d, head, k):
    if k == 0:
        return d
    n = d.shape[0]
    r = pltpu.roll(d, n - k, 0)
    row8 = _iota((8, d.shape[1]), 0)
    last = jnp.where(row8 >= 8 - k, pltpu.roll(head, 8 - k, 0), r[n - 8:])
    return jnp.concatenate([r[:n - 8], last], axis=0)


def _scan_down(a, b):
    n = a.shape[0]
    row = _iota(a.shape, 0)
    s = 1
    while s < n:
        ok = row >= s
        a_s = jnp.where(ok, pltpu.roll(a, s, 0), 1.0)
        b_s = jnp.where(ok, pltpu.roll(b, s, 0), 0.0)
        b = a * b_s + b
        a = a * a_s
        s *= 2
    return a, b


def _scan_up(a, b):
    n = a.shape[0]
    row = _iota(a.shape, 0)
    s = 1
    while s < n:
        ok = row < n - s
        a_s = jnp.where(ok, pltpu.roll(a, n - s, 0), 1.0)
        b_s = jnp.where(ok, pltpu.roll(b, n - s, 0), 0.0)
        b = a * b_s + b
        a = a * a_s
        s *= 2
    return a, b


def _neg_expm1(x):
    poly = -x * (1.0 + x * (0.5 + x * (1.0 / 6.0 + x * (1.0 / 24.0 + x * (1.0 / 120.0)))))
    return jnp.where(x > -0.1, poly, 1.0 - jnp.exp(x))


def _softplus(x):
    return jnp.maximum(x, 0.0) + jnp.log(1.0 + jnp.exp(-jnp.abs(x)))


def _lru_gates(u, tail, cw, cb, wr, br, wi, bi, lam):
    us = [_shift_down(u, tail, k) for k in range(4)]
    xc = cb + us[3] * cw[0:1] + us[2] * cw[1:2] + us[1] * cw[2:3] + us[0] * cw[3:4]
    xb = xc.astype(BF16)
    pre_r = jnp.concatenate([_mm(xb[:, 256 * g:256 * (g + 1)], wr[g]) for g in range(3)], axis=1) + br
    pre_i = jnp.concatenate([_mm(xb[:, 256 * g:256 * (g + 1)], wi[g]) for g in range(3)], axis=1) + bi
    rg, ig = jax.nn.sigmoid(pre_r), jax.nn.sigmoid(pre_i)
    clam = -LRU_C * _softplus(-lam)
    la = clam * rg
    a = jnp.exp(la)
    mm = jnp.sqrt(_neg_expm1(2.0 * la))
    return us, xc, xb, rg, ig, clam, la, a, mm


def _lru_fwd(h, win, cw, cb, wr, br, wi, bi, lam):
    s = h.shape[0]
    tb = min(TB_LRU, s)

    def body(h_ref, win_ref, cw_ref, cb_ref, wr_ref, br_ref, wi_ref, bi_ref, lam_ref,
             u_ref, gate_ref, qm_ref, hs_ref, tail_sc, carry_sc):
        @pl.when(pl.program_id(0) == 0)
        def _():
            tail_sc[...] = jnp.zeros_like(tail_sc)
            carry_sc[...] = jnp.zeros_like(carry_sc)

        z = _mm(h_ref[...], win_ref[...])
        u = z[:, :TOK_W]
        u_ref[...] = u
        gate_ref[...] = z[:, TOK_W:TOK_W + 1024]
        qm_ref[...] = z[:, TOK_W + 1024:].astype(BF16)
        _, xc, _, _, ig, _, _, a, mm = _lru_gates(u, tail_sc[...], cw_ref[...], cb_ref[...], wr_ref[...], br_ref[...],
                                                 wi_ref[...], bi_ref[...], lam_ref[...])
        big_a, big_b = _scan_down(a, mm * (ig * xc))
        hs = big_a * carry_sc[0:1, :] + big_b
        hs_ref[...] = hs
        tail_sc[...] = u[tb - 8:, :]
        carry_sc[...] = jnp.broadcast_to(hs[tb - 1:tb, :], carry_sc.shape)

    outs = (jax.ShapeDtypeStruct((s, TOK_W), F32), jax.ShapeDtypeStruct((s, 1024), F32),
            jax.ShapeDtypeStruct((s, MEM_W), BF16), jax.ShapeDtypeStruct((s, TOK_W), F32))
    return pl.pallas_call(
        body, name="lru_fwd", grid=(s // tb,), out_shape=outs,
        in_specs=[_rows(tb, 1024), _const((1024, 2048)), _const((4, TOK_W)), _const((1, TOK_W)),
                  _const((3, 256, 256)), _const((1, TOK_W)), _const((3, 256, 256)), _const((1, TOK_W)),
                  _const((1, TOK_W))],
        out_specs=(_rows(tb, TOK_W), _rows(tb, 1024), _rows(tb, MEM_W), _rows(tb, TOK_W)),
        scratch_shapes=[pltpu.VMEM((8, TOK_W), F32), pltpu.VMEM((8, TOK_W), F32)],
        compiler_params=_params(),
    )(h, win, cw, cb, wr, br, wi, bi, lam)


def _lru_bwd(dhs, dgate, dqm, dres, h, u, hs, win, cw, cb, wr, br, wi, bi, lam):
    s = h.shape[0]
    tb = min(TB_LRU, s)
    nb = s // tb

    def rev(w):
        return pl.BlockSpec((tb, w), lambda i: (nb - 1 - i, 0))

    def prev_tail(w):
        return pl.BlockSpec((8, w), lambda i: (jnp.maximum((nb - 1 - i) * (tb // 8) - 1, 0), 0))

    def body(dhs_ref, dgate_ref, dqm_ref, dres_ref, h_ref, u_ref, hs_ref, ut_ref, hst_ref, win_ref, cw_ref, cb_ref,
             wr_ref, br_ref, wi_ref, bi_ref, lam_ref,
             dh_ref, dwin_ref, dcw_ref, dcb_ref, dwr_ref, dbr_ref, dwi_ref, dbi_ref, dlam_ref, ecar_sc, dxc_sc):
        i = pl.program_id(0)

        @pl.when(i == 0)
        def _():
            for r in (dwin_ref, dcw_ref, dcb_ref, dwr_ref, dbr_ref, dwi_ref, dbi_ref, dlam_ref, ecar_sc, dxc_sc):
                r[...] = jnp.zeros_like(r)

        first = (i == nb - 1)
        u = u_ref[...]
        utail = jnp.where(first, 0.0, ut_ref[...])
        hstail = jnp.where(first, 0.0, hst_ref[...])
        cw, wr, wi, lam = cw_ref[...], wr_ref[...], wi_ref[...], lam_ref[...]
        us, xc, xb, rg, ig, clam, la, a, mm = _lru_gates(u, utail, cw, cb_ref[...], wr, br_ref[...], wi, bi_ref[...], lam)
        row = _iota(a.shape, 0)
        a_next = jnp.where(row < tb - 1, pltpu.roll(a, tb - 1, 0), 1.0)
        big_a, big_b = _scan_up(a_next, dhs_ref[...])
        e = big_a * ecar_sc[0:1, :] + big_b
        ecar_sc[...] = jnp.broadcast_to(a[0:1, :] * e[0:1, :], ecar_sc.shape)
        hs_prev = _shift_down(hs_ref[...], hstail, 1)
        da = e * hs_prev
        ix = ig * xc
        dmm = e * ix
        dix = e * mm
        dla = da * a - dmm * (a * a) / mm
        dlam_ref[...] += jnp.sum(dla * rg, axis=0, keepdims=True)
        dpr = (dla * clam) * rg * (1.0 - rg)
        dpi = (dix * xc) * ig * (1.0 - ig)
        dbr_ref[...] += jnp.sum(dpr, axis=0, keepdims=True)
        dbi_ref[...] += jnp.sum(dpi, axis=0, keepdims=True)
        dprb, dpib = dpr.astype(BF16), dpi.astype(BF16)
        dxc_g = []
        for g in range(3):
            sl = slice(256 * g, 256 * (g + 1))
            dwr_ref[g] += _mm_tn(xb[:, sl], dprb[:, sl])
            dwi_ref[g] += _mm_tn(xb[:, sl], dpib[:, sl])
            dxc_g.append(_mm_nt(dprb[:, sl], wr[g]) + _mm_nt(dpib[:, sl], wi[g]))
        dxc = dix * ig + jnp.concatenate(dxc_g, axis=1)
        dcb_ref[...] += jnp.sum(dxc, axis=0, keepdims=True)
        dcw_ref[...] += jnp.concatenate([jnp.sum(dxc * us[3 - tap], axis=0, keepdims=True) for tap in range(4)], axis=0)
        head = dxc_sc[...]
        du = dxc * cw[3:4]
        for k in range(1, 4):
            du = du + _shift_up(dxc, head, k) * cw[3 - k:4 - k]
        dxc_sc[...] = dxc[:8, :]
        dz = jnp.concatenate([du, dgate_ref[...], dqm_ref[...]], axis=1).astype(BF16)
        dh_ref[...] = _mm_nt(dz, win_ref[...]) + dres_ref[...]
        dwin_ref[...] += _mm_tn(h_ref[...], dz)

        @pl.when(i == nb - 1)
        def _():
            dlam_ref[...] = dlam_ref[...] * (LRU_C * jax.nn.sigmoid(-lam))

    outs = (jax.ShapeDtypeStruct((s, 1024), F32), jax.ShapeDtypeStruct((1024, 2048), F32),
            jax.ShapeDtypeStruct((4, TOK_W), F32), jax.ShapeDtypeStruct((1, TOK_W), F32),
            jax.ShapeDtypeStruct((3, 256, 256), F32), jax.ShapeDtypeStruct((1, TOK_W), F32),
            jax.ShapeDtypeStruct((3, 256, 256), F32), jax.ShapeDtypeStruct((1, TOK_W), F32),
            jax.ShapeDtypeStruct((1, TOK_W), F32))
    return pl.pallas_call(
        body, name="lru_bwd", grid=(nb,), out_shape=outs,
        in_specs=[rev(TOK_W), rev(1024), rev(MEM_W), rev(1024), rev(1024), rev(TOK_W), rev(TOK_W),
                  prev_tail(TOK_W), prev_tail(TOK_W),
                  _const((1024, 2048)), _const((4, TOK_W)), _const((1, TOK_W)), _const((3, 256, 256)),
                  _const((1, TOK_W)), _const((3, 256, 256)), _const((1, TOK_W)), _const((1, TOK_W))],
        out_specs=(rev(1024), _const((1024, 2048)), _const((4, TOK_W)), _const((1, TOK_W)), _const((3, 256, 256)),
                   _const((1, TOK_W)), _const((3, 256, 256)), _const((1, TOK_W)), _const((1, TOK_W))),
        scratch_shapes=[pltpu.VMEM((8, TOK_W), F32), pltpu.VMEM((8, TOK_W), F32)],
        compiler_params=_params(),
    )(dhs, dgate, dqm, dres, h, u, hs, u, hs, win, cw, cb, wr, br, wi, bi, lam)


def _adamw(name, w, g, m, v):
    rows, cols = w.shape
    tb = 256 if rows % 256 == 0 else rows

    def body(w_ref, g_ref, m_ref, v_ref, d_ref, nm_ref, nv_ref):
        g = g_ref[...]
        nm = ADAM_B1 * m_ref[...] + (1.0 - ADAM_B1) * g
        nv = ADAM_B2 * v_ref[...] + (1.0 - ADAM_B2) * (g * g)
        m_hat = nm / (1.0 - ADAM_B1 ** ADAM_STEP)
        v_hat = nv / (1.0 - ADAM_B2 ** ADAM_STEP)
        d_ref[...] = -ADAM_LR * (m_hat / (jnp.sqrt(v_hat) + ADAM_EPS) + ADAM_WD * w_ref[...])
        nm_ref[...] = nm
        nv_ref[...] = nv

    shp = jax.ShapeDtypeStruct((rows, cols), F32)
    return pl.pallas_call(
        body, name="adamw_" + name, grid=(rows // tb,), out_shape=(shp, shp, shp),
        in_specs=[_rows(tb, cols)] * 4, out_specs=(_rows(tb, cols),) * 3,
        compiler_params=_params(("parallel",)),
    )(w, g, m, v)


def _add2(a, b):
    rows = a.shape[0]
    tb = rows // 8 if rows % 64 == 0 else rows

    def body(a_ref, b_ref, o_ref):
        o_ref[...] = a_ref[...] + b_ref[...]

    return pl.pallas_call(
        body, name="add_sibling", grid=(rows // tb,), out_shape=jax.ShapeDtypeStruct(a.shape, F32),
        in_specs=[_rows(tb, 128)] * 2, out_specs=_rows(tb, 128), compiler_params=_params(("parallel",)),
    )(a, b)


def _sum4(r):
    rows = r.shape[1]
    tb = rows // 8 if rows % 64 == 0 else rows

    def body(r_ref, o_ref):
        o_ref[...] = ((r_ref[0] + r_ref[1]) + r_ref[2]) + r_ref[3]

    return pl.pallas_call(
        body, name="sum_chips", grid=(rows // tb,), out_shape=jax.ShapeDtypeStruct((rows, 128), F32),
        in_specs=[pl.BlockSpec((4, tb, 128), lambda i: (0, i, 0))], out_specs=_rows(tb, 128),
        compiler_params=_params(("parallel",)),
    )(r)


_ANY = pl.BlockSpec(memory_space=pl.ANY)


def _place():
    x, y, c = lax.axis_index("x"), lax.axis_index("y"), lax.axis_index("c")
    return x, y, c, [(1 - x, y), (x, 1 - y), (1 - x, 1 - y)]


def _remote(src, dst, ssem, rsem, to):
    return pltpu.make_async_remote_copy(src_ref=src, dst_ref=dst, send_sem=ssem, recv_sem=rsem, device_id=to,
                                        device_id_type=MESH)


def _gather_shards(wsh):
    _, hh, _ = wsh.shape

    def body(w_ref, out_ref, ssems, rsems, lsem):
        x, y, c, chips = _place()
        t = 2 * x + y
        mine = pltpu.make_async_copy(w_ref, out_ref.at[t], lsem)
        mine.start()
        first = [_remote(w_ref.at[c], out_ref.at[t, c], ssems.at[j], rsems.at[j], (cx, cy, c))
                 for j, (cx, cy) in enumerate(chips)]
        for cp in first:
            cp.start()
        passed = []
        for j, (cx, cy) in enumerate(chips):
            got = out_ref.at[2 * cx + cy, c]
            _remote(got, got, ssems.at[j], rsems.at[j], (cx, cy, c)).wait_recv()
            cp = _remote(got, got, ssems.at[3 + j], rsems.at[3 + j], (x, y, 1 - c))
            cp.start()
            passed.append(cp)
        for j, (cx, cy) in enumerate(chips):
            got = out_ref.at[2 * cx + cy, 1 - c]
            _remote(got, got, ssems.at[3 + j], rsems.at[3 + j], (x, y, 1 - c)).wait_recv()
        for cp in first + passed:
            cp.wait_send()
        mine.wait()

    return pl.pallas_call(
        body, name="gather_weights", out_shape=jax.ShapeDtypeStruct((4, 2, hh, 128), wsh.dtype),
        in_specs=[_ANY], out_specs=_ANY,
        scratch_shapes=[pltpu.SemaphoreType.DMA((6,)), pltpu.SemaphoreType.DMA((6,)), pltpu.SemaphoreType.DMA],
    )(wsh)


def _swap_sibling(v):
    def body(v_ref, got_ref, ssem, rsem):
        x, y, c, _ = _place()
        cp = _remote(v_ref, got_ref, ssem, rsem, (x, y, 1 - c))
        cp.start()
        cp.wait()

    return pl.pallas_call(
        body, name="swap_sibling", out_shape=jax.ShapeDtypeStruct(v.shape, v.dtype), in_specs=[_ANY], out_specs=_ANY,
        scratch_shapes=[pltpu.SemaphoreType.DMA, pltpu.SemaphoreType.DMA],
    )(v)


def _scatter_chips(p):
    def body(p_ref, out_ref, ssems, rsems, lsem):
        x, y, c, chips = _place()
        t = 2 * x + y
        mine = pltpu.make_async_copy(p_ref.at[t], out_ref.at[t], lsem)
        mine.start()
        cps = [_remote(p_ref.at[2 * cx + cy], out_ref.at[t], ssems.at[j], rsems.at[j], (cx, cy, c))
               for j, (cx, cy) in enumerate(chips)]
        for cp in cps:
            cp.start()
        for j, (cx, cy) in enumerate(chips):
            got = out_ref.at[2 * cx + cy]
            _remote(got, got, ssems.at[j], rsems.at[j], (cx, cy, c)).wait_recv()
        for cp in cps:
            cp.wait_send()
        mine.wait()

    return pl.pallas_call(
        body, name="scatter_chips", out_shape=jax.ShapeDtypeStruct(p.shape, p.dtype), in_specs=[_ANY], out_specs=_ANY,
        scratch_shapes=[pltpu.SemaphoreType.DMA((3,)), pltpu.SemaphoreType.DMA((3,)), pltpu.SemaphoreType.DMA],
    )(p)


def _share_reduced(piece, hs, rr):
    hh = piece.shape[0]

    def body(t_ref, full_ref, rall_ref, ssems, rsems, lsems):
        x, y, c, _ = _place()
        me = 4 * x + 2 * y + c
        mine_r = t_ref.at[pl.ds(hs, rr)]
        loc = [pltpu.make_async_copy(t_ref, full_ref.at[c], lsems.at[0]),
               pltpu.make_async_copy(mine_r, rall_ref.at[me], lsems.at[1])]
        for cp in loc:
            cp.start()
        sends = [_remote(t_ref, full_ref.at[c], ssems.at[0], rsems.at[0], (x, y, 1 - c))]
        peers = []
        for mask in range(1, 8):
            px = 1 - x if mask & 4 else x
            py = 1 - y if mask & 2 else y
            pc = 1 - c if mask & 1 else c
            peers.append((mask, px, py, pc))
            sends.append(_remote(mine_r, rall_ref.at[me], ssems.at[mask], rsems.at[mask], (px, py, pc)))
        for cp in sends:
            cp.start()
        got = full_ref.at[1 - c]
        _remote(got, got, ssems.at[0], rsems.at[0], (x, y, 1 - c)).wait_recv()
        for mask, px, py, pc in peers:
            got = rall_ref.at[4 * px + 2 * py + pc]
            _remote(got, got, ssems.at[mask], rsems.at[mask], (px, py, pc)).wait_recv()
        for cp in sends:
            cp.wait_send()
        for cp in loc:
            cp.wait()

    return pl.pallas_call(
        body, name="share_reduced",
        out_shape=(jax.ShapeDtypeStruct((2, hh, 128), F32), jax.ShapeDtypeStruct((8, rr, 128), F32)),
        in_specs=[_ANY], out_specs=(_ANY, _ANY),
        scratch_shapes=[pltpu.SemaphoreType.DMA((8,)), pltpu.SemaphoreType.DMA((8,)), pltpu.SemaphoreType.DMA((2,))],
    )(piece)


def _ceil_to(n, m):
    return -(-n // m) * m


def _col_shards(w2d):
    rows, cols = w2d.shape
    return w2d.reshape(rows, 4, cols // 4).transpose(1, 0, 2).reshape(4, rows * (cols // 4))


def _from_col_shards(flat, rows):
    w = flat.shape[1] // rows
    return flat.reshape(4, rows, w).transpose(1, 0, 2).reshape(rows, 4 * w)


def _block_diag4(w):
    eye = jnp.eye(4, dtype=w.dtype)
    return jnp.einsum("gaij,ab->gaibj", w.reshape(3, 4, 64, 64), eye).reshape(3, 256, 256)


def _diag_blocks4(w):
    w5 = w.reshape(3, 4, 64, 4, 64)
    return jnp.stack([w5[:, a, :, a, :] for a in range(4)], axis=1).reshape(12, 64, 64)


def kernel(x, mem, positions, mla_w_in, mla_q_norm, mla_w_uq, mla_kv_norm, mla_w_ukv, lru_w_in, lru_conv_w, lru_conv_b, lru_w_rgate, lru_b_rgate, lru_w_igate, lru_b_igate, lru_lambda, w_mem_kv, w_out, ln_g, ln_b, loss_target, m_mla_w_in, m_mla_q_norm, m_mla_w_uq, m_mla_kv_norm, m_mla_w_ukv, m_lru_w_in, m_lru_conv_w, m_lru_conv_b, m_lru_w_rgate, m_lru_b_rgate, m_lru_w_igate, m_lru_b_igate, m_lru_lambda, m_w_mem_kv, m_w_out, m_ln_g, m_ln_b, v_mla_w_in, v_mla_q_norm, v_mla_w_uq, v_mla_kv_norm, v_mla_w_ukv, v_lru_w_in, v_lru_conv_w, v_lru_conv_b, v_lru_w_rgate, v_lru_b_rgate, v_lru_w_igate, v_lru_b_igate, v_lru_lambda, v_w_mem_kv, v_w_out, v_ln_g, v_ln_b):
    s = x.shape[1]
    c_idx = lax.axis_index("c")
    x2, mem2, tgt2 = x[0], mem[0], loss_target[0]

    small = jnp.concatenate([lru_conv_w[0].reshape(-1), lru_conv_b[0], lru_b_rgate[0], lru_b_igate[0], lru_lambda[0]])
    parts = [mla_w_in[0].reshape(-1), mla_w_uq[0].reshape(-1), mla_w_ukv[0].reshape(-1), lru_w_in[0].reshape(-1),
             w_mem_kv.reshape(-1), w_out.reshape(-1)]
    sizes = [p.shape[0] for p in parts] + [2 * small.shape[0]]
    flat = jnp.concatenate([p.astype(BF16) for p in parts] + [lax.bitcast_convert_type(small, BF16).reshape(-1)])
    hw = _ceil_to(_ceil_to(flat.shape[0], 128) // 128, 32) // 2
    flat = jnp.pad(flat, (0, 2 * hw * 128 - flat.shape[0]))
    wall = _gather_shards(flat.reshape(2, hw, 128)).reshape(4, 2 * hw * 128)
    offs = [0]
    for n in sizes:
        offs.append(offs[-1] + n)
    seg = [wall[:, offs[i]:offs[i + 1]] for i in range(len(sizes))]
    win0 = _from_col_shards(seg[0], 1024)
    wuq = _from_col_shards(seg[1], Q_LORA)
    wukv = _from_col_shards(seg[2], KV_LORA)
    win1 = _from_col_shards(seg[3], 1024)
    wmem = seg[4].reshape(4, 2, 256, 512).transpose(1, 0, 2, 3).reshape(2, 1024, 512)
    wout = seg[5].reshape(4, 2, 256, 1024).transpose(1, 0, 2, 3).reshape(2, 1024, 1024)
    small_all = lax.bitcast_convert_type(seg[6].reshape(4, -1, 2), F32)
    cw = small_all[:, :768].reshape(4, 4, 192).transpose(1, 0, 2).reshape(4, TOK_W)
    cb, br, bi, lam = (small_all[:, 768 + 192 * k:960 + 192 * k].reshape(1, TOK_W) for k in range(4))

    loss, dx, (dwin0, dwuq, dwukv, dwin1, dcw, dcb, dbr, dbi, dlam, dwmem, dwout, dgq, dgkv, dwr, dwi, dg, db) = \
        _local_step(x2, mem2, positions.reshape(s, 1), tgt2, win0, wuq, wukv, win1, wmem, wout, cw, cb, br, bi, lam,
                    mla_q_norm, mla_kv_norm, lru_w_rgate[0], lru_w_igate[0], ln_g, ln_b)

    sharded = [
        ("mla_w_in", _col_shards(dwin0), (1, 1024, 488)),
        ("mla_w_uq", _col_shards(dwuq), (1, Q_LORA, 288)),
        ("mla_w_ukv", _col_shards(dwukv), (1, KV_LORA, 384)),
        ("lru_w_in", _col_shards(dwin1), (1, 1024, 512)),
        ("lru_conv_w", _col_shards(dcw), (1, 4, 192)),
        ("lru_conv_b", _col_shards(dcb), (1, 192)),
        ("lru_b_rgate", _col_shards(dbr), (1, 192)),
        ("lru_b_igate", _col_shards(dbi), (1, 192)),
        ("lru_lambda", _col_shards(dlam), (1, 192)),
        ("w_mem_kv", dwmem.reshape(2, 4, 256 * 512).transpose(1, 0, 2).reshape(4, -1), (2, 256, 512)),
        ("w_out", dwout.reshape(2, 4, 256 * 1024).transpose(1, 0, 2).reshape(4, -1), (2, 256, 1024)),
    ]
    replicated = [
        ("mla_q_norm", dgq.reshape(-1), (1, Q_LORA)),
        ("mla_kv_norm", dgkv.reshape(-1), (1, KV_LORA)),
        ("lru_w_rgate", dwr.reshape(-1), (1, 12, 64, 64)),
        ("lru_w_igate", dwi.reshape(-1), (1, 12, 64, 64)),
        ("ln_g", dg.reshape(-1), (2, 1024)),
        ("ln_b", db.reshape(-1), (2, 1024)),
    ]

    sh = jnp.concatenate([g for _, g, _ in sharded], axis=1)
    n_sh = sh.shape[1]
    hs_rows = _ceil_to(_ceil_to(n_sh, 128) // 128, 16) // 2
    sh = jnp.pad(sh, ((0, 0), (0, 2 * hs_rows * 128 - n_sh))).reshape(4, 2, hs_rows, 128)
    rp = jnp.concatenate([g for _, g, _ in replicated])
    n_rp = rp.shape[0]
    rr = _ceil_to(_ceil_to(n_rp, 128) // 128, 64) // 8
    rp = jnp.pad(rp, (0, 8 * rr * 128 - n_rp)).reshape(4, 2, rr, 128)
    gbuf = jnp.concatenate([sh, rp], axis=2)
    hh = hs_rows + rr
    mine = lax.dynamic_index_in_dim(gbuf, c_idx, axis=1, keepdims=False)
    other = lax.dynamic_index_in_dim(gbuf, 1 - c_idx, axis=1, keepdims=False)
    chip_sum = _add2(mine.reshape(4 * hh, 128), _swap_sibling(other).reshape(4 * hh, 128)).reshape(4, hh, 128)
    piece = _sum4(_scatter_chips(chip_sum))
    full, rall = _share_reduced(piece, hs_rows, rr)
    g_sh = full[:, :hs_rows].reshape(-1)
    g_rp = rall.reshape(-1)

    weights = dict(mla_w_in=mla_w_in, mla_q_norm=mla_q_norm, mla_w_uq=mla_w_uq, mla_kv_norm=mla_kv_norm,
                   mla_w_ukv=mla_w_ukv, lru_w_in=lru_w_in, lru_conv_w=lru_conv_w, lru_conv_b=lru_conv_b,
                   lru_w_rgate=lru_w_rgate, lru_b_rgate=lru_b_rgate, lru_w_igate=lru_w_igate, lru_b_igate=lru_b_igate,
                   lru_lambda=lru_lambda, w_mem_kv=w_mem_kv, w_out=w_out, ln_g=ln_g, ln_b=ln_b)
    m_in = dict(mla_w_in=m_mla_w_in, mla_q_norm=m_mla_q_norm, mla_w_uq=m_mla_w_uq, mla_kv_norm=m_mla_kv_norm,
                mla_w_ukv=m_mla_w_ukv, lru_w_in=m_lru_w_in, lru_conv_w=m_lru_conv_w, lru_conv_b=m_lru_conv_b,
                lru_w_rgate=m_lru_w_rgate, lru_b_rgate=m_lru_b_rgate, lru_w_igate=m_lru_w_igate,
                lru_b_igate=m_lru_b_igate, lru_lambda=m_lru_lambda, w_mem_kv=m_w_mem_kv, w_out=m_w_out, ln_g=m_ln_g,
                ln_b=m_ln_b)
    v_in = dict(mla_w_in=v_mla_w_in, mla_q_norm=v_mla_q_norm, mla_w_uq=v_mla_w_uq, mla_kv_norm=v_mla_kv_norm,
                mla_w_ukv=v_mla_w_ukv, lru_w_in=v_lru_w_in, lru_conv_w=v_lru_conv_w, lru_conv_b=v_lru_conv_b,
                lru_w_rgate=v_lru_w_rgate, lru_b_rgate=v_lru_b_rgate, lru_w_igate=v_lru_w_igate,
                lru_b_igate=v_lru_b_igate, lru_lambda=v_lru_lambda, w_mem_kv=v_w_mem_kv, w_out=v_w_out, ln_g=v_ln_g,
                ln_b=v_ln_b)
    grads, deltas, new_m, new_v = {}, {}, {}, {}
    for group, flat_g in ((sharded, g_sh), (replicated, g_rp)):
        off = 0
        for name, g, shape in group:
            n = math.prod(shape)
            cols = shape[-1]
            g2 = flat_g[off:off + n].reshape(n // cols, cols)
            off += n
            d2, m2, v2 = _adamw(name, weights[name].reshape(g2.shape), g2, m_in[name].reshape(g2.shape),
                                v_in[name].reshape(g2.shape))
            grads[name], deltas[name] = g2.reshape(shape), d2.reshape(shape)
            new_m[name], new_v[name] = m2.reshape(shape), v2.reshape(shape)

    order = ["mla_w_in", "mla_q_norm", "mla_w_uq", "mla_kv_norm", "mla_w_ukv", "lru_w_in", "lru_conv_w", "lru_conv_b",
             "lru_w_rgate", "lru_b_rgate", "lru_w_igate", "lru_b_igate", "lru_lambda", "w_mem_kv", "w_out", "ln_g",
             "ln_b"]
    loss_all = lax.psum(loss, ("x", "y", "c"))
    return (loss_all, dx[None], *[grads[n] for n in order], *[deltas[n] for n in order],
            *[new_m[n] for n in order], *[new_v[n] for n in order])


def _local_step(x2, mem2, pos_col, tgt2, win0, wuq, wukv, win1, wmem, wout, cw, cb, br, bi, lam,
                gq, gkv, w_rgate, w_igate, ln_g, ln_b):
    s = x2.shape[0]
    zpad = jnp.zeros((1024, 64), BF16)
    win0p = jnp.concatenate([win0[:, 672:1696], win0[:, 1696:1952], win0[:, 0:384], win0[:, 384:640],
                             zpad, win0[:, 640:672], zpad[:, :32]], axis=1)
    wuq_p = jnp.pad(wuq.reshape(Q_LORA, 12, 96), ((0, 0), (0, 0), (0, 32))).reshape(Q_LORA, QK_W)
    wukv3 = wukv.reshape(KV_LORA, 12, 128)
    wk_p = jnp.pad(wukv3[:, :, :64], ((0, 0), (0, 0), (0, 64))).reshape(KV_LORA, QK_W)
    wv = wukv3[:, :, 64:].reshape(KV_LORA, TOK_W)
    wr_bd = _block_diag4(w_rgate).astype(BF16)
    wi_bd = _block_diag4(w_igate).astype(BF16)
    half = 16
    inv_freq = ROPE_THETA ** (-jnp.arange(half, dtype=F32) / half)
    inv_lane = jnp.concatenate([jnp.zeros((64,), F32), inv_freq, inv_freq, jnp.zeros((32,), F32)]).reshape(1, HEAD_PAD)

    ctab, satab, sbtab = _rope_tables(pos_col, inv_lane, min(TB_PROJ, s))
    memkv = _mem_kv(mem2, wmem)
    gate0, qm0, cq, ckv, q_p, k_p, v_b = _mla_proj_fwd(x2, win0p, gq, gkv, wuq_p, wk_p, wv,
                                                      ctab, satab, sbtab)
    tok0, lse = _attn_fwd(q_p, k_p, v_b)
    g0, b0, g1, b1 = ln_g[0:1], ln_b[0:1], ln_g[1:2], ln_b[1:2]
    h1 = _mix_fwd(tok0, gate0, qm0, memkv[0], wout[0], x2, g0, b0)
    u1, gate1, qm1, hs1 = _lru_fwd(h1, win1, cw, cb, wr_bd, br, wi_bd, bi, lam)

    dres1, dtok1, dgate1, dqm1, dwout1, dmemkv1, dg1, db1, loss = _mix_bwd(
        hs1, gate1, qm1, memkv[1], wout[1], h1, g1, b1, tgt2, True)
    dh1, dwin1, dcw, dcb, dwr_bd, dbr, dwi_bd, dbi, dlam = _lru_bwd(
        dtok1, dgate1, dqm1, dres1, h1, u1, hs1, win1, cw, cb, wr_bd, br, wi_bd, bi, lam)
    dres0, dtok0, dgate0, dqm0, dwout0, dmemkv0, dg0, db0, _ = _mix_bwd(
        tok0, gate0, qm0, memkv[0], wout[0], x2, g0, b0, dh1, False)
    dob, stats = _attn_prep(tok0, dtok0, lse)
    dq_p, dk_p, dv = _attn_bwd(q_p, k_p, v_b, dob, stats)
    dx, dwin0p, dwuq_p, dwk_p, dwv, dgq, dgkv = _mla_proj_bwd(
        x2, cq, ckv, dq_p, dk_p, dv, dgate0, dqm0, dres0, win0p, gq, gkv, wuq_p, wk_p, wv,
        ctab, satab, sbtab)
    dwmem = _mem_kv_bwd(mem2, jnp.stack([dmemkv0, dmemkv1]))

    dwin0 = jnp.concatenate([dwin0p[:, 1280:1664], dwin0p[:, 1664:1920], dwin0p[:, 1984:2016], dwin0p[:, 0:1024],
                             dwin0p[:, 1024:1280]], axis=1)
    dwuq = dwuq_p.reshape(Q_LORA, 12, 128)[:, :, :96].reshape(Q_LORA, 1152)
    dwukv = jnp.concatenate([dwk_p.reshape(KV_LORA, 12, 128)[:, :, :64], dwv.reshape(KV_LORA, 12, 64)],
                            axis=2).reshape(KV_LORA, 1536)
    dwout = jnp.stack([dwout0, dwout1])
    return loss[0, 0], dx, (dwin0, dwuq, dwukv, dwin1, dcw, dcb, dbr, dbi, dlam, dwmem, dwout, dgq, dgkv,
                            _diag_blocks4(dwr_bd), _diag_blocks4(dwi_bd), jnp.concatenate([dg0, dg1]),
                            jnp.concatenate([db0, db1]))
```

```python
import functools
import math

import jax
import jax.numpy as jnp
from jax import lax
from jax.experimental import pallas as pl
from jax.experimental.pallas import tpu as pltpu

F32, BF16 = jnp.float32, jnp.bfloat16
MESH = pl.DeviceIdType.MESH

D_MODEL = 1024
N_TOK_HEADS = 12
TOK_W = 768
MEM_W = 256
MEM_LEN = 256
Q_LORA, KV_LORA = 384, 256
HEAD_PAD = 128
QK_W = N_TOK_HEADS * HEAD_PAD
ATT_SCALE = 1.0 / math.sqrt(96.0)
ATT_SCALE_LOG2 = ATT_SCALE * math.log2(math.e)
ROPE_THETA = 10000.0
LRU_C = 8.0
ALPHA = 4.0 ** 0.25
NORM_EPS = 1e-6
ADAM_LR, ADAM_B1, ADAM_B2, ADAM_EPS, ADAM_WD, ADAM_STEP = 0.001, 0.9, 0.999, 1e-08, 0.01, 10

TB_PROJ = 512
TB_PROJ_BWD = 256
TB_MIX = 256
TB_LRU = 256
TQ_ATT = 512
TQ_ATT_FWD = 1024
TK_ATT = 1024
VMEM_LIMIT = 56 * 1024 * 1024


def _mm(a, b):
    return jnp.dot(a.astype(BF16), b.astype(BF16), preferred_element_type=F32)


def _mm_nt(a, b):
    return lax.dot_general(a.astype(BF16), b.astype(BF16), (((1,), (1,)), ((), ())), preferred_element_type=F32)


def _mm_tn(a, b):
    return lax.dot_general(a.astype(BF16), b.astype(BF16), (((0,), (0,)), ((), ())), preferred_element_type=F32)


def _rows(tb, w):
    return pl.BlockSpec((tb, w), lambda i: (i, 0))


def _const(shape):
    n = len(shape)
    return pl.BlockSpec(shape, lambda i: (0,) * n)


def _params(sem=("arbitrary",)):
    return pltpu.CompilerParams(dimension_semantics=sem, vmem_limit_bytes=VMEM_LIMIT)


def _iota(shape, dim):
    return lax.broadcasted_iota(jnp.int32, shape, dim)


def _rope_tables(pos_col, inv_lane, tb):
    s = pos_col.shape[0]

    def body(pos_ref, inv_ref, c_ref, sa_ref, sb_ref):
        ang = pos_ref[...].astype(F32) * inv_ref[...]
        lane = _iota(ang.shape, 1)
        cs, sn = jnp.cos(ang), jnp.sin(ang)
        c_ref[...] = jnp.where(lane < 64, 1.0, jnp.where(lane < 96, cs, 0.0))
        sa_ref[...] = jnp.where((lane >= 64) & (lane < 80), -sn, 0.0)
        sb_ref[...] = jnp.where((lane >= 80) & (lane < 96), sn, 0.0)

    shp = jax.ShapeDtypeStruct((s, HEAD_PAD), F32)
    return pl.pallas_call(
        body, name="rope_tables", grid=(s // tb,), out_shape=(shp, shp, shp),
        in_specs=[_rows(tb, 1), _const((1, HEAD_PAD))], out_specs=(_rows(tb, HEAD_PAD),) * 3,
        compiler_params=_params(("parallel",)),
    )(pos_col, inv_lane)


def _rope(t, c, sa, sb):
    return t * c + pltpu.roll(t, 112, 1) * sa + pltpu.roll(t, 16, 1) * sb


def _rope_t(d, c, sa, sb):
    return d * c + pltpu.roll(d * sa, 16, 1) + pltpu.roll(d * sb, 112, 1)


def _rms(c, g):
    r = lax.rsqrt(jnp.mean(c * c, axis=-1, keepdims=True) + NORM_EPS)
    xh = c * r
    return xh * g, xh, r


def _mla_proj_fwd(x, win, gq, gkv, wuq, wukv_k, wukv_v, ctab, satab, sbtab):
    s = x.shape[0]
    tb = min(TB_PROJ, s)

    def body(x_ref, win_ref, gq_ref, gkv_ref, wuq_ref, wk_ref, wv_ref, c_ref, sa_ref, sb_ref,
             gate_ref, qm_ref, cq_ref, ckv_ref, q_ref, qt_ref, k_ref, v_ref, vt_ref):
        z = _mm(x_ref[...], win_ref[...])
        gate_ref[...] = z[:, 0:1024]
        qm_ref[...] = z[:, 1024:1280].astype(BF16)
        cq = z[:, 1280:1664]
        ckv = z[:, 1664:1920]
        cq_ref[...] = cq
        ckv_ref[...] = ckv
        c, sa, sb = c_ref[...], sa_ref[...], sb_ref[...]
        nq, _, _ = _rms(cq, gq_ref[...])
        nkv, _, _ = _rms(ckv, gkv_ref[...])
        qf = _mm(nq, wuq_ref[...])
        kf = _mm(nkv, wk_ref[...])
        vf = _mm(nkv, wv_ref[...])
        v_ref[...] = vf.astype(BF16)
        for j in range(N_TOK_HEADS // 2):
            sl = slice(HEAD_PAD * j, HEAD_PAD * (j + 1))
            vt_ref[sl, :] = vf[:, sl].T.astype(BF16)
        kr = _rope(z[:, 1920:2048], c, sa, sb)
        for h in range(N_TOK_HEADS):
            sl = slice(HEAD_PAD * h, HEAD_PAD * (h + 1))
            qh = _rope(qf[:, sl], c, sa, sb)
            q_ref[:, sl] = qh.astype(BF16)
            qt_ref[sl, :] = qh.T.astype(BF16)
            k_ref[:, sl] = (kf[:, sl] + kr).astype(BF16)

    outs = (jax.ShapeDtypeStruct((s, 1024), F32), jax.ShapeDtypeStruct((s, MEM_W), BF16),
            jax.ShapeDtypeStruct((s, Q_LORA), F32), jax.ShapeDtypeStruct((s, KV_LORA), F32),
            jax.ShapeDtypeStruct((s, QK_W), BF16), jax.ShapeDtypeStruct((QK_W, s), BF16),
            jax.ShapeDtypeStruct((s, QK_W), BF16),
            jax.ShapeDtypeStruct((s, TOK_W), BF16), jax.ShapeDtypeStruct((TOK_W, s), BF16))

    def cols(w):
        return pl.BlockSpec((w, tb), lambda i: (0, i))

    return pl.pallas_call(
        body, name="mla_proj_fwd", grid=(s // tb,), out_shape=outs,
        in_specs=[_rows(tb, 1024), _const((1024, 2048)), _const((1, Q_LORA)), _const((1, KV_LORA)),
                  _const((Q_LORA, QK_W)), _const((KV_LORA, QK_W)), _const((KV_LORA, TOK_W)),
                  _rows(tb, HEAD_PAD), _rows(tb, HEAD_PAD), _rows(tb, HEAD_PAD)],
        out_specs=(_rows(tb, 1024), _rows(tb, MEM_W), _rows(tb, Q_LORA), _rows(tb, KV_LORA),
                   _rows(tb, QK_W), cols(QK_W), _rows(tb, QK_W), _rows(tb, TOK_W), cols(TOK_W)),
        compiler_params=_params(("parallel",)),
    )(x, win, gq, gkv, wuq, wukv_k, wukv_v, ctab, satab, sbtab)


def _mla_proj_bwd(x, cq, ckv, dq, dkt, dvt, dgate, dqm, dres, win, gq, gkv, wuq, wukv_k, wukv_v, ctab, satab, sbtab):
    s = x.shape[0]
    tb = min(TB_PROJ_BWD, s)

    def body(x_ref, cq_ref, ckv_ref, dq_ref, dkt_ref, dvt_ref, dgate_ref, dqm_ref, dres_ref, win_ref, gq_ref, gkv_ref,
             wuq_ref, wk_ref, wv_ref, c_ref, sa_ref, sb_ref,
             dx_ref, dwin_ref, dwuq_ref, dwk_ref, dwv_ref, dgq_ref, dgkv_ref):
        @pl.when(pl.program_id(0) == 0)
        def _():
            for r in (dwin_ref, dwuq_ref, dwk_ref, dwv_ref, dgq_ref, dgkv_ref):
                r[...] = jnp.zeros_like(r)

        c, sa, sb = c_ref[...], sa_ref[...], sb_ref[...]
        lane = _iota((tb, HEAD_PAD), 1)
        gq, gkv = gq_ref[...], gkv_ref[...]
        nq, xhq, rq = _rms(cq_ref[...], gq)
        nkv, xhk, rk = _rms(ckv_ref[...], gkv)
        dkp = dkt_ref[...].T
        dqs, dkr = [], jnp.zeros((tb, HEAD_PAD), F32)
        for h in range(N_TOK_HEADS):
            sl = slice(HEAD_PAD * h, HEAD_PAD * (h + 1))
            dqs.append(_rope_t(dq_ref[:, sl], c, sa, sb).astype(BF16))
            dkr = dkr + dkp[:, sl]
        dqf = jnp.concatenate(dqs, axis=1)
        dkr = jnp.where((lane >= 64) & (lane < 96), _rope_t(dkr, c, sa, sb), 0.0)
        dvb = dvt_ref[...].T.astype(BF16)
        dkb = dkp.astype(BF16)
        dnq = _mm_nt(dqf, wuq_ref[...])
        dwuq_ref[...] += _mm_tn(nq, dqf)
        dgq_ref[...] += jnp.sum(dnq * xhq, axis=0, keepdims=True)
        dxh = dnq * gq
        dcq = rq * (dxh - xhq * jnp.mean(dxh * xhq, axis=-1, keepdims=True))
        dnkv = _mm_nt(dkb, wk_ref[...]) + _mm_nt(dvb, wv_ref[...])
        nkvb = nkv.astype(BF16)
        dwk_ref[...] += _mm_tn(nkvb, dkb)
        dwv_ref[...] += _mm_tn(nkvb, dvb)
        dgkv_ref[...] += jnp.sum(dnkv * xhk, axis=0, keepdims=True)
        dxh = dnkv * gkv
        dckv = rk * (dxh - xhk * jnp.mean(dxh * xhk, axis=-1, keepdims=True))
        dz = jnp.concatenate([dgate_ref[...], dqm_ref[...], dcq, dckv, dkr], axis=1).astype(BF16)
        dx_ref[...] = _mm_nt(dz, win_ref[...]) + dres_ref[...]
        dwin_ref[...] += _mm_tn(x_ref[...], dz)

    outs = (jax.ShapeDtypeStruct((s, 1024), F32), jax.ShapeDtypeStruct((1024, 2048), F32),
            jax.ShapeDtypeStruct((Q_LORA, QK_W), F32), jax.ShapeDtypeStruct((KV_LORA, QK_W), F32),
            jax.ShapeDtypeStruct((KV_LORA, TOK_W), F32), jax.ShapeDtypeStruct((1, Q_LORA), F32),
            jax.ShapeDtypeStruct((1, KV_LORA), F32))
    return pl.pallas_call(
        body, name="mla_proj_bwd", grid=(s // tb,), out_shape=outs,
        in_specs=[_rows(tb, 1024), _rows(tb, Q_LORA), _rows(tb, KV_LORA), _rows(tb, QK_W),
                  pl.BlockSpec((QK_W, tb), lambda i: (0, i)), pl.BlockSpec((TOK_W, tb), lambda i: (0, i)),
                  _rows(tb, 1024), _rows(tb, MEM_W), _rows(tb, 1024),
                  _const((1024, 2048)), _const((1, Q_LORA)), _const((1, KV_LORA)),
                  _const((Q_LORA, QK_W)), _const((KV_LORA, QK_W)), _const((KV_LORA, TOK_W)),
                  _rows(tb, HEAD_PAD), _rows(tb, HEAD_PAD), _rows(tb, HEAD_PAD)],
        out_specs=(_rows(tb, 1024), _const((1024, 2048)), _const((Q_LORA, QK_W)), _const((KV_LORA, QK_W)),
                   _const((KV_LORA, TOK_W)), _const((1, Q_LORA)), _const((1, KV_LORA))),
        compiler_params=_params(),
    )(x, cq, ckv, dq, dkt, dvt, dgate, dqm, dres, win, gq, gkv, wuq, wukv_k, wukv_v, ctab, satab, sbtab)


def _attn_fwd(q, k, vt):
    s = q.shape[0]
    tq = min(TQ_ATT_FWD, s)
    tk = min(TK_ATT, s)

    def body(q_ref, k_ref, vt_ref, o_ref, lse_ref):
        i = pl.program_id(1)
        nfull = (i * tq) // tk
        krow = _iota((tk, tq), 0)
        qpos = i * tq + _iota((tk, tq), 1)

        def head_tile(hh, st, carry, masked):
            hs = slice(HEAD_PAD * hh, HEAD_PAD * (hh + 1))
            m, l, acc = carry
            sc = _mm_nt(k_ref[pl.ds(st, tk), hs], q_ref[:, hs]) * ATT_SCALE_LOG2
            if masked:
                sc = jnp.where(st + krow <= qpos, sc, -jnp.inf)
            m_new = jnp.maximum(m, jnp.max(sc, axis=0, keepdims=True))
            p = jnp.exp2(sc - m_new)
            a = jnp.exp2(m - m_new)
            l = a * l + jnp.sum(p, axis=0, keepdims=True)
            acc = a * acc + _mm(vt_ref[64 * hh:64 * (hh + 1), pl.ds(st, tk)], p)
            return m_new, l, acc

        def tile(j, carry, masked):
            st = pl.multiple_of(j * tk, tk)
            return tuple(head_tile(hh, st, carry[hh], masked) for hh in range(2))

        def init():
            return (jnp.full((1, tq), -jnp.inf, F32), jnp.zeros((1, tq), F32), jnp.zeros((64, tq), F32))

        carry = lax.fori_loop(0, nfull, functools.partial(tile, masked=False), (init(), init()))
        (ma, la, acca), (mb, lb, accb) = tile(nfull, carry, True)
        o_ref[...] = jnp.concatenate([acca / la, accb / lb], axis=0).T
        lse_ref[...] = jnp.concatenate([jnp.broadcast_to(ma + jnp.log2(la), (64, tq)),
                                        jnp.broadcast_to(mb + jnp.log2(lb), (64, tq))], axis=0).T

    shp = jax.ShapeDtypeStruct((s, TOK_W), F32)
    return pl.pallas_call(
        body, name="attn_fwd", grid=(N_TOK_HEADS // 2, s // tq), out_shape=(shp, shp),
        in_specs=[pl.BlockSpec((tq, 2 * HEAD_PAD), lambda j, i: (i, j)),
                  pl.BlockSpec((s, 2 * HEAD_PAD), lambda j, i: (0, j)),
                  pl.BlockSpec((HEAD_PAD, s), lambda j, i: (j, 0))],
        out_specs=(pl.BlockSpec((tq, HEAD_PAD), lambda j, i: (i, j)),) * 2,
        compiler_params=_params(("parallel", "arbitrary")),
    )(q, k, vt)


def _attn_prep(o, do, lse):
    s = o.shape[0]
    tb = min(TB_PROJ, s)
    npair = N_TOK_HEADS // 2

    def body(o_ref, do_ref, lse_ref, dob_ref, dot_ref, st_ref):
        lane = _iota((tb, HEAD_PAD), 1)
        do = do_ref[...]
        dob_ref[...] = do.astype(BF16)
        prod = do * o_ref[...]
        for j in range(npair):
            sl = slice(HEAD_PAD * j, HEAD_PAD * (j + 1))
            dot_ref[sl, :] = do[:, sl].T.astype(BF16)
            pj = prod[:, sl]
            da = jnp.sum(jnp.where(lane < 64, pj, 0.0), axis=-1, keepdims=True)
            db = jnp.sum(jnp.where(lane >= 64, pj, 0.0), axis=-1, keepdims=True)
            la = lse_ref[:, HEAD_PAD * j:HEAD_PAD * j + 1]
            lb = lse_ref[:, HEAD_PAD * j + 64:HEAD_PAD * j + 65]
            st_ref[j] = jnp.where(lane == 0, la, jnp.where(lane == 1, lb, jnp.where(lane == 2, da,
                                                                                     jnp.where(lane == 3, db, 0.0))))

    return pl.pallas_call(
        body, name="attn_prep", grid=(s // tb,),
        out_shape=(jax.ShapeDtypeStruct((s, TOK_W), BF16), jax.ShapeDtypeStruct((TOK_W, s), BF16),
                   jax.ShapeDtypeStruct((npair, s, HEAD_PAD), F32)),
        in_specs=[_rows(tb, TOK_W)] * 3,
        out_specs=(_rows(tb, TOK_W), pl.BlockSpec((TOK_W, tb), lambda i: (0, i)),
                   pl.BlockSpec((npair, tb, HEAD_PAD), lambda i: (0, i, 0))),
        compiler_params=_params(("parallel",)),
    )(o, do, lse)


def _attn_bwd(q, qt, k, v, dob, dobt, stats):
    s = q.shape[0]
    t = min(TQ_ATT, s)
    nq = s // t

    def body(q_ref, qt_ref, do_ref, dot_ref, st_ref, k_ref, v_ref, dq_ref, dkt_ref, dvt_ref):
        i = pl.program_id(1)

        @pl.when(i == 0)
        def _():
            dkt_ref[...] = jnp.zeros_like(dkt_ref)
            dvt_ref[...] = jnp.zeros_like(dvt_ref)

        lane = _iota((t, HEAD_PAD), 1)
        qpos, kcol = _iota((t, t), 0), _iota((t, t), 1)
        do = do_ref[...]
        stats = st_ref[0]

        def head_tile(hh, ks, dq_acc, masked):
            hs = slice(HEAD_PAD * hh, HEAD_PAD * (hh + 1))
            qh = q_ref[:, hs]
            kh = k_ref[pl.ds(ks, t), hs]
            dom = jnp.where((lane < 64) if hh == 0 else (lane >= 64), do, jnp.zeros_like(do))
            lse = stats[:, hh:hh + 1]
            dlt = stats[:, 2 + hh:3 + hh]
            sc = _mm_nt(qh, kh) * ATT_SCALE_LOG2
            if masked:
                sc = jnp.where(kcol <= qpos, sc, -jnp.inf)
            p = jnp.exp2(sc - lse)
            dp = _mm_nt(dom, v_ref[pl.ds(ks, t), :])
            ds = (p * (dp - dlt) * ATT_SCALE).astype(BF16)
            dvt_ref[64 * hh:64 * (hh + 1), pl.ds(ks, t)] += _mm(dot_ref[64 * hh:64 * (hh + 1), :], p)
            dkt_ref[HEAD_PAD * hh:HEAD_PAD * hh + 96, pl.ds(ks, t)] += _mm(qt_ref[HEAD_PAD * hh:HEAD_PAD * hh + 96, :], ds)
            return dq_acc + _mm(ds, kh)

        def tile(j, carry, masked):
            ks = pl.multiple_of(j * t, t)
            return tuple(head_tile(hh, ks, carry[hh], masked) for hh in range(2))

        zero = jnp.zeros((t, HEAD_PAD), F32)
        carry = lax.fori_loop(0, i, functools.partial(tile, masked=False), (zero, zero))
        dqa, dqb = tile(i, carry, True)
        dq_ref[...] = jnp.concatenate([dqa, dqb], axis=1)

    return pl.pallas_call(
        body, name="attn_bwd", grid=(N_TOK_HEADS // 2, nq),
        out_shape=(jax.ShapeDtypeStruct((s, QK_W), F32), jax.ShapeDtypeStruct((QK_W, s), F32),
                   jax.ShapeDtypeStruct((TOK_W, s), F32)),
        in_specs=[pl.BlockSpec((t, 2 * HEAD_PAD), lambda j, i: (i, j)),
                  pl.BlockSpec((2 * HEAD_PAD, t), lambda j, i: (j, i)),
                  pl.BlockSpec((t, HEAD_PAD), lambda j, i: (i, j)),
                  pl.BlockSpec((HEAD_PAD, t), lambda j, i: (j, i)),
                  pl.BlockSpec((1, t, HEAD_PAD), lambda j, i: (j, i, 0)),
                  pl.BlockSpec((s, 2 * HEAD_PAD), lambda j, i: (0, j)),
                  pl.BlockSpec((s, HEAD_PAD), lambda j, i: (0, j))],
        out_specs=(pl.BlockSpec((t, 2 * HEAD_PAD), lambda j, i: (i, j)),
                   pl.BlockSpec((2 * HEAD_PAD, s), lambda j, i: (j, 0)),
                   pl.BlockSpec((HEAD_PAD, s), lambda j, i: (j, 0))),
        compiler_params=_params(("parallel", "arbitrary")),
    )(q, qt, dob, dobt, stats, k, v)


def _mem_kv(mem, wmem):
    def body(m_ref, w_ref, o_ref):
        o_ref[0] = _mm(m_ref[...], w_ref[0]).astype(BF16)

    return pl.pallas_call(
        body, name="mem_kv", grid=(2,), out_shape=jax.ShapeDtypeStruct((2, MEM_LEN, 512), BF16),
        in_specs=[_const((MEM_LEN, 1024)), pl.BlockSpec((1, 1024, 512), lambda l: (l, 0, 0))],
        out_specs=pl.BlockSpec((1, MEM_LEN, 512), lambda l: (l, 0, 0)),
        compiler_params=_params(("parallel",)),
    )(mem, wmem)


def _mem_kv_bwd(mem, dmemkv):
    def body(m_ref, d_ref, o_ref):
        o_ref[0] = _mm_tn(m_ref[...], d_ref[0])

    return pl.pallas_call(
        body, name="mem_kv_bwd", grid=(2,), out_shape=jax.ShapeDtypeStruct((2, 1024, 512), F32),
        in_specs=[_const((MEM_LEN, 1024)), pl.BlockSpec((1, MEM_LEN, 512), lambda l: (l, 0, 0))],
        out_specs=pl.BlockSpec((1, 1024, 512), lambda l: (l, 0, 0)),
        compiler_params=_params(("parallel",)),
    )(mem, dmemkv)


def _head_mask(lane, sub):
    return (lane < 64) if sub == 0 else (lane >= 64)


def _mem_attn(qm, kv):
    tb = qm.shape[0]
    lane = _iota((tb, HEAD_PAD), 1)
    outs, ps = [], []
    for pp in range(2):
        qp = qm[:, HEAD_PAD * pp:HEAD_PAD * (pp + 1)]
        kp = kv[:, HEAD_PAD * pp:HEAD_PAD * (pp + 1)]
        vp = kv[:, MEM_W + HEAD_PAD * pp:MEM_W + HEAD_PAD * (pp + 1)]
        pair = None
        for sub in range(2):
            qh = jnp.where(_head_mask(lane, sub), qp, jnp.zeros_like(qp))
            sc = _mm_nt(qh, kp) * 0.125
            e = jnp.exp(sc - jnp.max(sc, axis=-1, keepdims=True))
            p = e / jnp.sum(e, axis=-1, keepdims=True)
            o = _mm(p, vp)
            ps.append(p)
            pair = o if sub == 0 else jnp.where(lane < 64, pair, o)
        outs.append(pair)
    return jnp.concatenate(outs, axis=1), ps


def _mem_attn_bwd(dmo, qm, kv, ps):
    tb = qm.shape[0]
    lane = _iota((tb, HEAD_PAD), 1)
    dqs, dks, dvs = [], [], []
    for pp in range(2):
        qp = qm[:, HEAD_PAD * pp:HEAD_PAD * (pp + 1)]
        kp = kv[:, HEAD_PAD * pp:HEAD_PAD * (pp + 1)]
        vp = kv[:, MEM_W + HEAD_PAD * pp:MEM_W + HEAD_PAD * (pp + 1)]
        dop = dmo[:, HEAD_PAD * pp:HEAD_PAD * (pp + 1)]
        dq_pair, dk_pair, dv_pair = None, None, None
        for sub in range(2):
            msk = _head_mask(lane, sub)
            p = ps[2 * pp + sub]
            qh = jnp.where(msk, qp, jnp.zeros_like(qp))
            doh = jnp.where(msk, dop, 0.0).astype(BF16)
            dv = _mm_tn(p, doh)
            dp = _mm_nt(doh, vp)
            ds = (p * (dp - jnp.sum(dp * p, axis=-1, keepdims=True)) * 0.125).astype(BF16)
            dq = _mm(ds, kp)
            dk = _mm_tn(ds, qh)
            if sub == 0:
                dq_pair, dk_pair, dv_pair = dq, dk, dv
            else:
                dq_pair = jnp.where(lane < 64, dq_pair, dq)
                dk_pair, dv_pair = dk_pair + dk, dv_pair + dv
        dqs.append(dq_pair)
        dks.append(dk_pair)
        dvs.append(dv_pair)
    return jnp.concatenate(dqs, axis=1), jnp.concatenate(dks + dvs, axis=1)


def _mix_core(tok, gate, qm, kv, wout, h_in, g, b):
    mem_out, ps = _mem_attn(qm, kv)
    cat = jnp.concatenate([tok, mem_out], axis=1)
    sg = jax.nn.sigmoid(gate)
    sl = gate * sg
    y = cat * sl
    r = ALPHA * h_in + _mm(y, wout)
    mu = jnp.mean(r, axis=-1, keepdims=True)
    xc = r - mu
    rstd = lax.rsqrt(jnp.mean(xc * xc, axis=-1, keepdims=True) + NORM_EPS)
    xh = xc * rstd
    return xh * g + b, (ps, cat, sg, sl, y, xh, rstd)


def _mix_fwd(tok, gate, qm, kv, wout, h_in, g, b):
    s = tok.shape[0]
    tb = min(TB_MIX, s)

    def body(tok_ref, gate_ref, qm_ref, kv_ref, w_ref, h_ref, g_ref, b_ref, o_ref):
        o_ref[...], _ = _mix_core(tok_ref[...], gate_ref[...], qm_ref[...], kv_ref[...], w_ref[...], h_ref[...],
                                  g_ref[...], b_ref[...])

    return pl.pallas_call(
        body, name="mix_fwd", grid=(s // tb,), out_shape=jax.ShapeDtypeStruct((s, 1024), F32),
        in_specs=[_rows(tb, TOK_W), _rows(tb, 1024), _rows(tb, MEM_W), _const((MEM_LEN, 512)), _const((1024, 1024)),
                  _rows(tb, 1024), _const((1, 1024)), _const((1, 1024))],
        out_specs=_rows(tb, 1024), compiler_params=_params(("parallel",)),
    )(tok, gate, qm, kv, wout, h_in, g, b)


def _mix_bwd(tok, gate, qm, kv, wout, h_in, g, b, up, from_loss):
    s = tok.shape[0]
    tb = min(TB_MIX, s)

    def body(tok_ref, gate_ref, qm_ref, kv_ref, w_ref, h_ref, g_ref, b_ref, up_ref,
             dres_ref, dtok_ref, dgate_ref, dqm_ref, dw_ref, dkv_ref, dg_ref, db_ref, loss_ref):
        @pl.when(pl.program_id(0) == 0)
        def _():
            for r in (dw_ref, dkv_ref, dg_ref, db_ref, loss_ref):
                r[...] = jnp.zeros_like(r)

        gate, qm, kv, wout, g = gate_ref[...], qm_ref[...], kv_ref[...], w_ref[...], g_ref[...]
        h_out, (ps, cat, sg, sl, y, xh, rstd) = _mix_core(tok_ref[...], gate, qm, kv, wout, h_ref[...], g, b_ref[...])
        if from_loss:
            diff = h_out - up_ref[...]
            loss_ref[...] += 0.5 * jnp.sum(jnp.mean(diff * diff, axis=-1, keepdims=True), axis=0, keepdims=True)
            dh = diff * (1.0 / D_MODEL)
        else:
            dh = up_ref[...]
        dg_ref[...] += jnp.sum(dh * xh, axis=0, keepdims=True)
        db_ref[...] += jnp.sum(dh, axis=0, keepdims=True)
        dxh = dh * g
        dr = rstd * (dxh - jnp.mean(dxh, axis=-1, keepdims=True) - xh * jnp.mean(dxh * xh, axis=-1, keepdims=True))
        dres_ref[...] = ALPHA * dr
        drb = dr.astype(BF16)
        dy = _mm_nt(drb, wout)
        dw_ref[...] += _mm_tn(y, drb)
        dcat = dy * sl
        dgate_ref[...] = dy * cat * (sg * (1.0 + gate * (1.0 - sg)))
        dtok_ref[...] = dcat[:, :TOK_W]
        dqm, dkv = _mem_attn_bwd(dcat[:, TOK_W:], qm, kv, ps)
        dqm_ref[...] = dqm
        dkv_ref[...] += dkv

    outs = (jax.ShapeDtypeStruct((s, 1024), F32), jax.ShapeDtypeStruct((s, TOK_W), F32),
            jax.ShapeDtypeStruct((s, 1024), F32), jax.ShapeDtypeStruct((s, MEM_W), F32),
            jax.ShapeDtypeStruct((1024, 1024), F32), jax.ShapeDtypeStruct((MEM_LEN, 512), F32),
            jax.ShapeDtypeStruct((1, 1024), F32), jax.ShapeDtypeStruct((1, 1024), F32),
            jax.ShapeDtypeStruct((1, 1), F32))
    return pl.pallas_call(
        body, name="mix_bwd_loss" if from_loss else "mix_bwd", grid=(s // tb,), out_shape=outs,
        in_specs=[_rows(tb, TOK_W), _rows(tb, 1024), _rows(tb, MEM_W), _const((MEM_LEN, 512)), _const((1024, 1024)),
                  _rows(tb, 1024), _const((1, 1024)), _const((1, 1024)), _rows(tb, 1024)],
        out_specs=(_rows(tb, 1024), _rows(tb, TOK_W), _rows(tb, 1024), _rows(tb, MEM_W), _const((1024, 1024)),
                   _const((MEM_LEN, 512)), _const((1, 1024)), _const((1, 1024)), _const((1, 1))),
        compiler_params=_params(),
    )(tok, gate, qm, kv, wout, h_in, g, b, up)


def _shift_down(u, tail, k):
    if k == 0:
        return u
    r = pltpu.roll(u, k, 0)
    row8 = _iota((8, u.shape[1]), 0)
    head = jnp.where(row8 < k, pltpu.roll(tail, k, 0), r[:8])
    return jnp.concatenate([head, r[8:]], axis=0)


def _shift_up(d, head, k):
    if k == 0:
        return d
    n = d.shape[0]
    r = pltpu.roll(d, n - k, 0)
    row8 = _iota((8, d.shape[1]), 0)
    last = jnp.where(row8 >= 8 - k, pltpu.roll(head, 8 - k, 0), r[n - 8:])
    return jnp.concatenate([r[:n - 8], last], axis=0)


def _scan_down(a, b):
    n = a.shape[0]
    row = _iota(a.shape, 0)
    s = 1
    while s < n:
        ok = row >= s
        a_s = jnp.where(ok, pltpu.roll(a, s, 0), 1.0)
        b_s = jnp.where(ok, pltpu.roll(b, s, 0), 0.0)
        b = a * b_s + b
        a = a * a_s
        s *= 2
    return a, b


def _scan_up(a, b):
    n = a.shape[0]
    row = _iota(a.shape, 0)
    s = 1
    while s < n:
        ok = row < n - s
        a_s = jnp.where(ok, pltpu.roll(a, n - s, 0), 1.0)
        b_s = jnp.where(ok, pltpu.roll(b, n - s, 0), 0.0)
        b = a * b_s + b
        a = a * a_s
        s *= 2
    return a, b


def _neg_expm1(x):
    poly = -x * (1.0 + x * (0.5 + x * (1.0 / 6.0 + x * (1.0 / 24.0 + x * (1.0 / 120.0)))))
    return jnp.where(x > -0.1, poly, 1.0 - jnp.exp(x))


def _softplus(x):
    return jnp.maximum(x, 0.0) + jnp.log(1.0 + jnp.exp(-jnp.abs(x)))


def _lru_gates(u, tail, cw, cb, wr, br, wi, bi, lam):
    us = [_shift_down(u, tail, k) for k in range(4)]
    xc = cb + us[3] * cw[0:1] + us[2] * cw[1:2] + us[1] * cw[2:3] + us[0] * cw[3:4]
    xb = xc.astype(BF16)
    pre_r = jnp.concatenate([_mm(xb[:, 256 * g:256 * (g + 1)], wr[g]) for g in range(3)], axis=1) + br
    pre_i = jnp.concatenate([_mm(xb[:, 256 * g:256 * (g + 1)], wi[g]) for g in range(3)], axis=1) + bi
    rg, ig = jax.nn.sigmoid(pre_r), jax.nn.sigmoid(pre_i)
    clam = -LRU_C * _softplus(-lam)
    la = clam * rg
    a = jnp.exp(la)
    mm = jnp.sqrt(_neg_expm1(2.0 * la))
    return us, xc, xb, rg, ig, clam, la, a, mm


def _lru_fwd(h, win, cw, cb, wr, br, wi, bi, lam):
    s = h.shape[0]
    tb = min(TB_LRU, s)

    def body(h_ref, win_ref, cw_ref, cb_ref, wr_ref, br_ref, wi_ref, bi_ref, lam_ref,
             u_ref, gate_ref, qm_ref, hs_ref, tail_sc, carry_sc):
        @pl.when(pl.program_id(0) == 0)
        def _():
            tail_sc[...] = jnp.zeros_like(tail_sc)
            carry_sc[...] = jnp.zeros_like(carry_sc)

        z = _mm(h_ref[...], win_ref[...])
        u = z[:, :TOK_W]
        u_ref[...] = u
        gate_ref[...] = z[:, TOK_W:TOK_W + 1024]
        qm_ref[...] = z[:, TOK_W + 1024:].astype(BF16)
        _, xc, _, _, ig, _, _, a, mm = _lru_gates(u, tail_sc[...], cw_ref[...], cb_ref[...], wr_ref[...], br_ref[...],
                                                 wi_ref[...], bi_ref[...], lam_ref[...])
        big_a, big_b = _scan_down(a, mm * (ig * xc))
        hs = big_a * carry_sc[0:1, :] + big_b
        hs_ref[...] = hs
        tail_sc[...] = u[tb - 8:, :]
        carry_sc[...] = jnp.broadcast_to(hs[tb - 1:tb, :], carry_sc.shape)

    outs = (jax.ShapeDtypeStruct((s, TOK_W), F32), jax.ShapeDtypeStruct((s, 1024), F32),
            jax.ShapeDtypeStruct((s, MEM_W), BF16), jax.ShapeDtypeStruct((s, TOK_W), F32))
    return pl.pallas_call(
        body, name="lru_fwd", grid=(s // tb,), out_shape=outs,
        in_specs=[_rows(tb, 1024), _const((1024, 2048)), _const((4, TOK_W)), _const((1, TOK_W)),
                  _const((3, 256, 256)), _const((1, TOK_W)), _const((3, 256, 256)), _const((1, TOK_W)),
                  _const((1, TOK_W))],
        out_specs=(_rows(tb, TOK_W), _rows(tb, 1024), _rows(tb, MEM_W), _rows(tb, TOK_W)),
        scratch_shapes=[pltpu.VMEM((8, TOK_W), F32), pltpu.VMEM((8, TOK_W), F32)],
        compiler_params=_params(),
    )(h, win, cw, cb, wr, br, wi, bi, lam)


def _lru_bwd(dhs, dgate, dqm, dres, h, u, hs, win, cw, cb, wr, br, wi, bi, lam):
    s = h.shape[0]
    tb = min(TB_LRU, s)
    nb = s // tb

    def rev(w):
        return pl.BlockSpec((tb, w), lambda i: (nb - 1 - i, 0))

    def prev_tail(w):
        return pl.BlockSpec((8, w), lambda i: (jnp.maximum((nb - 1 - i) * (tb // 8) - 1, 0), 0))

    def body(dhs_ref, dgate_ref, dqm_ref, dres_ref, h_ref, u_ref, hs_ref, ut_ref, hst_ref, win_ref, cw_ref, cb_ref,
             wr_ref, br_ref, wi_ref, bi_ref, lam_ref,
             dh_ref, dwin_ref, dcw_ref, dcb_ref, dwr_ref, dbr_ref, dwi_ref, dbi_ref, dlam_ref, ecar_sc, dxc_sc):
        i = pl.program_id(0)

        @pl.when(i == 0)
        def _():
            for r in (dwin_ref, dcw_ref, dcb_ref, dwr_ref, dbr_ref, dwi_ref, dbi_ref, dlam_ref, ecar_sc, dxc_sc):
                r[...] = jnp.zeros_like(r)

        first = (i == nb - 1)
        u = u_ref[...]
        utail = jnp.where(first, 0.0, ut_ref[...])
        hstail = jnp.where(first, 0.0, hst_ref[...])
        cw, wr, wi, lam = cw_ref[...], wr_ref[...], wi_ref[...], lam_ref[...]
        us, xc, xb, rg, ig, clam, la, a, mm = _lru_gates(u, utail, cw, cb_ref[...], wr, br_ref[...], wi, bi_ref[...], lam)
        row = _iota(a.shape, 0)
        a_next = jnp.where(row < tb - 1, pltpu.roll(a, tb - 1, 0), 1.0)
        big_a, big_b = _scan_up(a_next, dhs_ref[...])
        e = big_a * ecar_sc[0:1, :] + big_b
        ecar_sc[...] = jnp.broadcast_to(a[0:1, :] * e[0:1, :], ecar_sc.shape)
        hs_prev = _shift_down(hs_ref[...], hstail, 1)
        da = e * hs_prev
        ix = ig * xc
        dmm = e * ix
        dix = e * mm
        dla = da * a - dmm * (a * a) / mm
        dlam_ref[...] += jnp.sum(dla * rg, axis=0, keepdims=True)
        dpr = (dla * clam) * rg * (1.0 - rg)
        dpi = (dix * xc) * ig * (1.0 - ig)
        dbr_ref[...] += jnp.sum(dpr, axis=0, keepdims=True)
        dbi_ref[...] += jnp.sum(dpi, axis=0, keepdims=True)
        dprb, dpib = dpr.astype(BF16), dpi.astype(BF16)
        dxc_g = []
        for g in range(3):
            sl = slice(256 * g, 256 * (g + 1))
            dwr_ref[g] += _mm_tn(xb[:, sl], dprb[:, sl])
            dwi_ref[g] += _mm_tn(xb[:, sl], dpib[:, sl])
            dxc_g.append(_mm_nt(dprb[:, sl], wr[g]) + _mm_nt(dpib[:, sl], wi[g]))
        dxc = dix * ig + jnp.concatenate(dxc_g, axis=1)
        dcb_ref[...] += jnp.sum(dxc, axis=0, keepdims=True)
        dcw_ref[...] += jnp.concatenate([jnp.sum(dxc * us[3 - tap], axis=0, keepdims=True) for tap in range(4)], axis=0)
        head = dxc_sc[...]
        du = dxc * cw[3:4]
        for k in range(1, 4):
            du = du + _shift_up(dxc, head, k) * cw[3 - k:4 - k]
        dxc_sc[...] = dxc[:8, :]
        dz = jnp.concatenate([du, dgate_ref[...], dqm_ref[...]], axis=1).astype(BF16)
        dh_ref[...] = _mm_nt(dz, win_ref[...]) + dres_ref[...]
        dwin_ref[...] += _mm_tn(h_ref[...], dz)

        @pl.when(i == nb - 1)
        def _():
            dlam_ref[...] = dlam_ref[...] * (LRU_C * jax.nn.sigmoid(-lam))

    outs = (jax.ShapeDtypeStruct((s, 1024), F32), jax.ShapeDtypeStruct((1024, 2048), F32),
            jax.ShapeDtypeStruct((4, TOK_W), F32), jax.ShapeDtypeStruct((1, TOK_W), F32),
            jax.ShapeDtypeStruct((3, 256, 256), F32), jax.ShapeDtypeStruct((1, TOK_W), F32),
            jax.ShapeDtypeStruct((3, 256, 256), F32), jax.ShapeDtypeStruct((1, TOK_W), F32),
            jax.ShapeDtypeStruct((1, TOK_W), F32))
    return pl.pallas_call(
        body, name="lru_bwd", grid=(nb,), out_shape=outs,
        in_specs=[rev(TOK_W), rev(1024), rev(MEM_W), rev(1024), rev(1024), rev(TOK_W), rev(TOK_W),
                  prev_tail(TOK_W), prev_tail(TOK_W),
                  _const((1024, 2048)), _const((4, TOK_W)), _const((1, TOK_W)), _const((3, 256, 256)),
                  _const((1, TOK_W)), _const((3, 256, 256)), _const((1, TOK_W)), _const((1, TOK_W))],
        out_specs=(rev(1024), _const((1024, 2048)), _const((4, TOK_W)), _const((1, TOK_W)), _const((3, 256, 256)),
                   _const((1, TOK_W)), _const((3, 256, 256)), _const((1, TOK_W)), _const((1, TOK_W))),
        scratch_shapes=[pltpu.VMEM((8, TOK_W), F32), pltpu.VMEM((8, TOK_W), F32)],
        compiler_params=_params(),
    )(dhs, dgate, dqm, dres, h, u, hs, u, hs, win, cw, cb, wr, br, wi, bi, lam)


def _adamw(name, w, g, m, v):
    rows, cols = w.shape
    tb = 256 if rows % 256 == 0 else rows

    def body(w_ref, g_ref, m_ref, v_ref, d_ref, nm_ref, nv_ref):
        g = g_ref[...]
        nm = ADAM_B1 * m_ref[...] + (1.0 - ADAM_B1) * g
        nv = ADAM_B2 * v_ref[...] + (1.0 - ADAM_B2) * (g * g)
        m_hat = nm / (1.0 - ADAM_B1 ** ADAM_STEP)
        v_hat = nv / (1.0 - ADAM_B2 ** ADAM_STEP)
        d_ref[...] = -ADAM_LR * (m_hat / (jnp.sqrt(v_hat) + ADAM_EPS) + ADAM_WD * w_ref[...])
        nm_ref[...] = nm
        nv_ref[...] = nv

    shp = jax.ShapeDtypeStruct((rows, cols), F32)
    return pl.pallas_call(
        body, name="adamw_" + name, grid=(rows // tb,), out_shape=(shp, shp, shp),
        in_specs=[_rows(tb, cols)] * 4, out_specs=(_rows(tb, cols),) * 3,
        compiler_params=_params(("parallel",)),
    )(w, g, m, v)


def _add2(a, b):
    rows = a.shape[0]
    tb = rows // 8 if rows % 64 == 0 else rows

    def body(a_ref, b_ref, o_ref):
        o_ref[...] = a_ref[...] + b_ref[...]

    return pl.pallas_call(
        body, name="add_sibling", grid=(rows // tb,), out_shape=jax.ShapeDtypeStruct(a.shape, F32),
        in_specs=[_rows(tb, 128)] * 2, out_specs=_rows(tb, 128), compiler_params=_params(("parallel",)),
    )(a, b)


def _sum4(r):
    rows = r.shape[1]
    tb = rows // 8 if rows % 64 == 0 else rows

    def body(r_ref, o_ref):
        o_ref[...] = ((r_ref[0] + r_ref[1]) + r_ref[2]) + r_ref[3]

    return pl.pallas_call(
        body, name="sum_chips", grid=(rows // tb,), out_shape=jax.ShapeDtypeStruct((rows, 128), F32),
        in_specs=[pl.BlockSpec((4, tb, 128), lambda i: (0, i, 0))], out_specs=_rows(tb, 128),
        compiler_params=_params(("parallel",)),
    )(r)


_ANY = pl.BlockSpec(memory_space=pl.ANY)


def _place():
    x, y, c = lax.axis_index("x"), lax.axis_index("y"), lax.axis_index("c")
    return x, y, c, [(1 - x, y), (x, 1 - y), (1 - x, 1 - y)]


def _remote(src, dst, ssem, rsem, to):
    return pltpu.make_async_remote_copy(src_ref=src, dst_ref=dst, send_sem=ssem, recv_sem=rsem, device_id=to,
                                        device_id_type=MESH)


def _gather_shards(wsh):
    _, hh, _ = wsh.shape

    def body(w_ref, out_ref, ssems, rsems, lsem):
        x, y, c, chips = _place()
        t = 2 * x + y
        mine = pltpu.make_async_copy(w_ref, out_ref.at[t], lsem)
        mine.start()
        first = [_remote(w_ref.at[c], out_ref.at[t, c], ssems.at[j], rsems.at[j], (cx, cy, c))
                 for j, (cx, cy) in enumerate(chips)]
        for cp in first:
            cp.start()
        passed = []
        for j, (cx, cy) in enumerate(chips):
            got = out_ref.at[2 * cx + cy, c]
            _remote(got, got, ssems.at[j], rsems.at[j], (cx, cy, c)).wait_recv()
            cp = _remote(got, got, ssems.at[3 + j], rsems.at[3 + j], (x, y, 1 - c))
            cp.start()
            passed.append(cp)
        for j, (cx, cy) in enumerate(chips):
            got = out_ref.at[2 * cx + cy, 1 - c]
            _remote(got, got, ssems.at[3 + j], rsems.at[3 + j], (x, y, 1 - c)).wait_recv()
        for cp in first + passed:
            cp.wait_send()
        mine.wait()

    return pl.pallas_call(
        body, name="gather_weights", out_shape=jax.ShapeDtypeStruct((4, 2, hh, 128), wsh.dtype),
        in_specs=[_ANY], out_specs=_ANY,
        scratch_shapes=[pltpu.SemaphoreType.DMA((6,)), pltpu.SemaphoreType.DMA((6,)), pltpu.SemaphoreType.DMA],
    )(wsh)


def _swap_sibling(v):
    def body(v_ref, got_ref, ssem, rsem):
        x, y, c, _ = _place()
        cp = _remote(v_ref, got_ref, ssem, rsem, (x, y, 1 - c))
        cp.start()
        cp.wait()

    return pl.pallas_call(
        body, name="swap_sibling", out_shape=jax.ShapeDtypeStruct(v.shape, v.dtype), in_specs=[_ANY], out_specs=_ANY,
        scratch_shapes=[pltpu.SemaphoreType.DMA, pltpu.SemaphoreType.DMA],
    )(v)


def _scatter_chips(p):
    def body(p_ref, out_ref, ssems, rsems, lsem):
        x, y, c, chips = _place()
        t = 2 * x + y
        mine = pltpu.make_async_copy(p_ref.at[t], out_ref.at[t], lsem)
        mine.start()
        cps = [_remote(p_ref.at[2 * cx + cy], out_ref.at[t], ssems.at[j], rsems.at[j], (cx, cy, c))
               for j, (cx, cy) in enumerate(chips)]
        for cp in cps:
            cp.start()
        for j, (cx, cy) in enumerate(chips):
            got = out_ref.at[2 * cx + cy]
            _remote(got, got, ssems.at[j], rsems.at[j], (cx, cy, c)).wait_recv()
        for cp in cps:
            cp.wait_send()
        mine.wait()

    return pl.pallas_call(
        body, name="scatter_chips", out_shape=jax.ShapeDtypeStruct(p.shape, p.dtype), in_specs=[_ANY], out_specs=_ANY,
        scratch_shapes=[pltpu.SemaphoreType.DMA((3,)), pltpu.SemaphoreType.DMA((3,)), pltpu.SemaphoreType.DMA],
    )(p)


def _share_reduced(piece, hs, rr):
    hh = piece.shape[0]

    def body(t_ref, full_ref, rall_ref, ssems, rsems, lsems):
        x, y, c, _ = _place()
        me = 4 * x + 2 * y + c
        mine_r = t_ref.at[pl.ds(hs, rr)]
        loc = [pltpu.make_async_copy(t_ref, full_ref.at[c], lsems.at[0]),
               pltpu.make_async_copy(mine_r, rall_ref.at[me], lsems.at[1])]
        for cp in loc:
            cp.start()
        sends = [_remote(t_ref, full_ref.at[c], ssems.at[0], rsems.at[0], (x, y, 1 - c))]
        peers = []
        for mask in range(1, 8):
            px = 1 - x if mask & 4 else x
            py = 1 - y if mask & 2 else y
            pc = 1 - c if mask & 1 else c
            peers.append((mask, px, py, pc))
            sends.append(_remote(mine_r, rall_ref.at[me], ssems.at[mask], rsems.at[mask], (px, py, pc)))
        for cp in sends:
            cp.start()
        got = full_ref.at[1 - c]
        _remote(got, got, ssems.at[0], rsems.at[0], (x, y, 1 - c)).wait_recv()
        for mask, px, py, pc in peers:
            got = rall_ref.at[4 * px + 2 * py + pc]
            _remote(got, got, ssems.at[mask], rsems.at[mask], (px, py, pc)).wait_recv()
        for cp in sends:
            cp.wait_send()
        for cp in loc:
            cp.wait()

    return pl.pallas_call(
        body, name="share_reduced",
        out_shape=(jax.ShapeDtypeStruct((2, hh, 128), F32), jax.ShapeDtypeStruct((8, rr, 128), F32)),
        in_specs=[_ANY], out_specs=(_ANY, _ANY),
        scratch_shapes=[pltpu.SemaphoreType.DMA((8,)), pltpu.SemaphoreType.DMA((8,)), pltpu.SemaphoreType.DMA((2,))],
    )(piece)


def _ceil_to(n, m):
    return -(-n // m) * m


def _col_shards(w2d):
    rows, cols = w2d.shape
    return w2d.reshape(rows, 4, cols // 4).transpose(1, 0, 2).reshape(4, rows * (cols // 4))


def _from_col_shards(flat, rows):
    w = flat.shape[1] // rows
    return flat.reshape(4, rows, w).transpose(1, 0, 2).reshape(rows, 4 * w)


def _block_diag4(w):
    eye = jnp.eye(4, dtype=w.dtype)
    return jnp.einsum("gaij,ab->gaibj", w.reshape(3, 4, 64, 64), eye).reshape(3, 256, 256)


def _diag_blocks4(w):
    w5 = w.reshape(3, 4, 64, 4, 64)
    return jnp.stack([w5[:, a, :, a, :] for a in range(4)], axis=1).reshape(12, 64, 64)


def kernel(x, mem, positions, mla_w_in, mla_q_norm, mla_w_uq, mla_kv_norm, mla_w_ukv, lru_w_in, lru_conv_w, lru_conv_b, lru_w_rgate, lru_b_rgate, lru_w_igate, lru_b_igate, lru_lambda, w_mem_kv, w_out, ln_g, ln_b, loss_target, m_mla_w_in, m_mla_q_norm, m_mla_w_uq, m_mla_kv_norm, m_mla_w_ukv, m_lru_w_in, m_lru_conv_w, m_lru_conv_b, m_lru_w_rgate, m_lru_b_rgate, m_lru_w_igate, m_lru_b_igate, m_lru_lambda, m_w_mem_kv, m_w_out, m_ln_g, m_ln_b, v_mla_w_in, v_mla_q_norm, v_mla_w_uq, v_mla_kv_norm, v_mla_w_ukv, v_lru_w_in, v_lru_conv_w, v_lru_conv_b, v_lru_w_rgate, v_lru_b_rgate, v_lru_w_igate, v_lru_b_igate, v_lru_lambda, v_w_mem_kv, v_w_out, v_ln_g, v_ln_b):
    s = x.shape[1]
    c_idx = lax.axis_index("c")
    x2, mem2, tgt2 = x[0], mem[0], loss_target[0]

    small = jnp.concatenate([lru_conv_w[0].reshape(-1), lru_conv_b[0], lru_b_rgate[0], lru_b_igate[0], lru_lambda[0]])
    parts = [mla_w_in[0].reshape(-1), mla_w_uq[0].reshape(-1), mla_w_ukv[0].reshape(-1), lru_w_in[0].reshape(-1),
             w_mem_kv.reshape(-1), w_out.reshape(-1)]
    sizes = [p.shape[0] for p in parts] + [2 * small.shape[0]]
    flat = jnp.concatenate([p.astype(BF16) for p in parts] + [lax.bitcast_convert_type(small, BF16).reshape(-1)])
    hw = _ceil_to(_ceil_to(flat.shape[0], 128) // 128, 32) // 2
    flat = jnp.pad(flat, (0, 2 * hw * 128 - flat.shape[0]))
    wall = _gather_shards(flat.reshape(2, hw, 128)).reshape(4, 2 * hw * 128)
    offs = [0]
    for n in sizes:
        offs.append(offs[-1] + n)
    seg = [wall[:, offs[i]:offs[i + 1]] for i in range(len(sizes))]
    win0 = _from_col_shards(seg[0], 1024)
    wuq = _from_col_shards(seg[1], Q_LORA)
    wukv = _from_col_shards(seg[2], KV_LORA)
    win1 = _from_col_shards(seg[3], 1024)
    wmem = seg[4].reshape(4, 2, 256, 512).transpose(1, 0, 2, 3).reshape(2, 1024, 512)
    wout = seg[5].reshape(4, 2, 256, 1024).transpose(1, 0, 2, 3).reshape(2, 1024, 1024)
    small_all = lax.bitcast_convert_type(seg[6].reshape(4, -1, 2), F32)
    cw = small_all[:, :768].reshape(4, 4, 192).transpose(1, 0, 2).reshape(4, TOK_W)
    cb, br, bi, lam = (small_all[:, 768 + 192 * k:960 + 192 * k].reshape(1, TOK_W) for k in range(4))

    loss, dx, (dwin0, dwuq, dwukv, dwin1, dcw, dcb, dbr, dbi, dlam, dwmem, dwout, dgq, dgkv, dwr, dwi, dg, db) = \
        _local_step(x2, mem2, positions.reshape(s, 1), tgt2, win0, wuq, wukv, win1, wmem, wout, cw, cb, br, bi, lam,
                    mla_q_norm, mla_kv_norm, lru_w_rgate[0], lru_w_igate[0], ln_g, ln_b)

    sharded = [
        ("mla_w_in", _col_shards(dwin0), (1, 1024, 488)),
        ("mla_w_uq", _col_shards(dwuq), (1, Q_LORA, 288)),
        ("mla_w_ukv", _col_shards(dwukv), (1, KV_LORA, 384)),
        ("lru_w_in", _col_shards(dwin1), (1, 1024, 512)),
        ("lru_conv_w", _col_shards(dcw), (1, 4, 192)),
        ("lru_conv_b", _col_shards(dcb), (1, 192)),
        ("lru_b_rgate", _col_shards(dbr), (1, 192)),
        ("lru_b_igate", _col_shards(dbi), (1, 192)),
        ("lru_lambda", _col_shards(dlam), (1, 192)),
        ("w_mem_kv", dwmem.reshape(2, 4, 256 * 512).transpose(1, 0, 2).reshape(4, -1), (2, 256, 512)),
        ("w_out", dwout.reshape(2, 4, 256 * 1024).transpose(1, 0, 2).reshape(4, -1), (2, 256, 1024)),
    ]
    replicated = [
        ("mla_q_norm", dgq.reshape(-1), (1, Q_LORA)),
        ("mla_kv_norm", dgkv.reshape(-1), (1, KV_LORA)),
        ("lru_w_rgate", dwr.reshape(-1), (1, 12, 64, 64)),
        ("lru_w_igate", dwi.reshape(-1), (1, 12, 64, 64)),
        ("ln_g", dg.reshape(-1), (2, 1024)),
        ("ln_b", db.reshape(-1), (2, 1024)),
    ]

    sh = jnp.concatenate([g for _, g, _ in sharded], axis=1)
    n_sh = sh.shape[1]
    hs_rows = _ceil_to(_ceil_to(n_sh, 128) // 128, 16) // 2
    sh = jnp.pad(sh, ((0, 0), (0, 2 * hs_rows * 128 - n_sh))).reshape(4, 2, hs_rows, 128)
    rp = jnp.concatenate([g for _, g, _ in replicated])
    n_rp = rp.shape[0]
    rr = _ceil_to(_ceil_to(n_rp, 128) // 128, 64) // 8
    rp = jnp.pad(rp, (0, 8 * rr * 128 - n_rp)).reshape(4, 2, rr, 128)
    gbuf = jnp.concatenate([sh, rp], axis=2)
    hh = hs_rows + rr
    mine = lax.dynamic_index_in_dim(gbuf, c_idx, axis=1, keepdims=False)
    other = lax.dynamic_index_in_dim(gbuf, 1 - c_idx, axis=1, keepdims=False)
    chip_sum = _add2(mine.reshape(4 * hh, 128), _swap_sibling(other).reshape(4 * hh, 128)).reshape(4, hh, 128)
    piece = _sum4(_scatter_chips(chip_sum))
    full, rall = _share_reduced(piece, hs_rows, rr)
    g_sh = full[:, :hs_rows].reshape(-1)
    g_rp = rall.reshape(-1)

    weights = dict(mla_w_in=mla_w_in, mla_q_norm=mla_q_norm, mla_w_uq=mla_w_uq, mla_kv_norm=mla_kv_norm,
                   mla_w_ukv=mla_w_ukv, lru_w_in=lru_w_in, lru_conv_w=lru_conv_w, lru_conv_b=lru_conv_b,
                   lru_w_rgate=lru_w_rgate, lru_b_rgate=lru_b_rgate, lru_w_igate=lru_w_igate, lru_b_igate=lru_b_igate,
                   lru_lambda=lru_lambda, w_mem_kv=w_mem_kv, w_out=w_out, ln_g=ln_g, ln_b=ln_b)
    m_in = dict(mla_w_in=m_mla_w_in, mla_q_norm=m_mla_q_norm, mla_w_uq=m_mla_w_uq, mla_kv_norm=m_mla_kv_norm,
                mla_w_ukv=m_mla_w_ukv, lru_w_in=m_lru_w_in, lru_conv_w=m_lru_conv_w, lru_conv_b=m_lru_conv_b,
                lru_w_rgate=m_lru_w_rgate, lru_b_rgate=m_lru_b_rgate, lru_w_igate=m_lru_w_igate,
                lru_b_igate=m_lru_b_igate, lru_lambda=m_lru_lambda, w_mem_kv=m_w_mem_kv, w_out=m_w_out, ln_g=m_ln_g,
                ln_b=m_ln_b)
    v_in = dict(mla_w_in=v_mla_w_in, mla_q_norm=v_mla_q_norm, mla_w_uq=v_mla_w_uq, mla_kv_norm=v_mla_kv_norm,
                mla_w_ukv=v_mla_w_ukv, lru_w_in=v_lru_w_in, lru_conv_w=v_lru_conv_w, lru_conv_b=v_lru_conv_b,
                lru_w_rgate=v_lru_w_rgate, lru_b_rgate=v_lru_b_rgate, lru_w_igate=v_lru_w_igate,
                lru_b_igate=v_lru_b_igate, lru_lambda=v_lru_lambda, w_mem_kv=v_w_mem_kv, w_out=v_w_out, ln_g=v_ln_g,
                ln_b=v_ln_b)
    grads, deltas, new_m, new_v = {}, {}, {}, {}
    for group, flat_g in ((sharded, g_sh), (replicated, g_rp)):
        off = 0
        for name, g, shape in group:
            n = math.prod(shape)
            cols = shape[-1]
            g2 = flat_g[off:off + n].reshape(n // cols, cols)
            off += n
            d2, m2, v2 = _adamw(name, weights[name].reshape(g2.shape), g2, m_in[name].reshape(g2.shape),
                                v_in[name].reshape(g2.shape))
            grads[name], deltas[name] = g2.reshape(shape), d2.reshape(shape)
            new_m[name], new_v[name] = m2.reshape(shape), v2.reshape(shape)

    order = ["mla_w_in", "mla_q_norm", "mla_w_uq", "mla_kv_norm", "mla_w_ukv", "lru_w_in", "lru_conv_w", "lru_conv_b",
             "lru_w_rgate", "lru_b_rgate", "lru_w_igate", "lru_b_igate", "lru_lambda", "w_mem_kv", "w_out", "ln_g",
             "ln_b"]
    loss_all = lax.psum(loss, ("x", "y", "c"))
    return (loss_all, dx[None], *[grads[n] for n in order], *[deltas[n] for n in order],
            *[new_m[n] for n in order], *[new_v[n] for n in order])


def _local_step(x2, mem2, pos_col, tgt2, win0, wuq, wukv, win1, wmem, wout, cw, cb, br, bi, lam,
                gq, gkv, w_rgate, w_igate, ln_g, ln_b):
    s = x2.shape[0]
    zpad = jnp.zeros((1024, 64), BF16)
    win0p = jnp.concatenate([win0[:, 672:1696], win0[:, 1696:1952], win0[:, 0:384], win0[:, 384:640],
                             zpad, win0[:, 640:672], zpad[:, :32]], axis=1)
    wuq_p = jnp.pad(wuq.reshape(Q_LORA, 12, 96), ((0, 0), (0, 0), (0, 32))).reshape(Q_LORA, QK_W)
    wukv3 = wukv.reshape(KV_LORA, 12, 128)
    wk_p = jnp.pad(wukv3[:, :, :64], ((0, 0), (0, 0), (0, 64))).reshape(KV_LORA, QK_W)
    wv = wukv3[:, :, 64:].reshape(KV_LORA, TOK_W)
    wr_bd = _block_diag4(w_rgate).astype(BF16)
    wi_bd = _block_diag4(w_igate).astype(BF16)
    half = 16
    inv_freq = ROPE_THETA ** (-jnp.arange(half, dtype=F32) / half)
    inv_lane = jnp.concatenate([jnp.zeros((64,), F32), inv_freq, inv_freq, jnp.zeros((32,), F32)]).reshape(1, HEAD_PAD)

    ctab, satab, sbtab = _rope_tables(pos_col, inv_lane, min(TB_PROJ, s))
    memkv = _mem_kv(mem2, wmem)
    gate0, qm0, cq, ckv, q_p, q_t, k_p, v_b, v_t = _mla_proj_fwd(x2, win0p, gq, gkv, wuq_p, wk_p, wv,
                                                                ctab, satab, sbtab)
    tok0, lse = _attn_fwd(q_p, k_p, v_t)
    g0, b0, g1, b1 = ln_g[0:1], ln_b[0:1], ln_g[1:2], ln_b[1:2]
    h1 = _mix_fwd(tok0, gate0, qm0, memkv[0], wout[0], x2, g0, b0)
    u1, gate1, qm1, hs1 = _lru_fwd(h1, win1, cw, cb, wr_bd, br, wi_bd, bi, lam)

    dres1, dtok1, dgate1, dqm1, dwout1, dmemkv1, dg1, db1, loss = _mix_bwd(
        hs1, gate1, qm1, memkv[1], wout[1], h1, g1, b1, tgt2, True)
    dh1, dwin1, dcw, dcb, dwr_bd, dbr, dwi_bd, dbi, dlam = _lru_bwd(
        dtok1, dgate1, dqm1, dres1, h1, u1, hs1, win1, cw, cb, wr_bd, br, wi_bd, bi, lam)
    dres0, dtok0, dgate0, dqm0, dwout0, dmemkv0, dg0, db0, _ = _mix_bwd(
        tok0, gate0, qm0, memkv[0], wout[0], x2, g0, b0, dh1, False)
    dob, dobt, stats = _attn_prep(tok0, dtok0, lse)
    dq_p, dk_t, dv_t = _attn_bwd(q_p, q_t, k_p, v_b, dob, dobt, stats)
    dx, dwin0p, dwuq_p, dwk_p, dwv, dgq, dgkv = _mla_proj_bwd(
        x2, cq, ckv, dq_p, dk_t, dv_t, dgate0, dqm0, dres0, win0p, gq, gkv, wuq_p, wk_p, wv,
        ctab, satab, sbtab)
    dwmem = _mem_kv_bwd(mem2, jnp.stack([dmemkv0, dmemkv1]))

    dwin0 = jnp.concatenate([dwin0p[:, 1280:1664], dwin0p[:, 1664:1920], dwin0p[:, 1984:2016], dwin0p[:, 0:1024],
                             dwin0p[:, 1024:1280]], axis=1)
    dwuq = dwuq_p.reshape(Q_LORA, 12, 128)[:, :, :96].reshape(Q_LORA, 1152)
    dwukv = jnp.concatenate([dwk_p.reshape(KV_LORA, 12, 128)[:, :, :64], dwv.reshape(KV_LORA, 12, 64)],
                            axis=2).reshape(KV_LORA, 1536)
    dwout = jnp.stack([dwout0, dwout1])
    return loss[0, 0], dx, (dwin0, dwuq, dwukv, dwin1, dcw, dcb, dbr, dbi, dlam, dwmem, dwout, dgq, dgkv,
                            _diag_blocks4(dwr_bd), _diag_blocks4(dwi_bd), jnp.concatenate([dg0, dg1]),
                            jnp.concatenate([db0, db1]))
```

```python
import functools
import math

import jax
import jax.numpy as jnp
from jax import lax
from jax.experimental import pallas as pl
from jax.experimental.pallas import tpu as pltpu

F32, BF16 = jnp.float32, jnp.bfloat16
MESH = pl.DeviceIdType.MESH

D_MODEL = 1024
N_TOK_HEADS = 12
TOK_W = 768
MEM_W = 256
MEM_LEN = 256
Q_LORA, KV_LORA = 384, 256
HEAD_PAD = 128
QK_W = N_TOK_HEADS * HEAD_PAD
ATT_SCALE = 1.0 / math.sqrt(96.0)
ATT_SCALE_LOG2 = ATT_SCALE * math.log2(math.e)
ROPE_THETA = 10000.0
LRU_C = 8.0
ALPHA = 4.0 ** 0.25
NORM_EPS = 1e-6
ADAM_LR, ADAM_B1, ADAM_B2, ADAM_EPS, ADAM_WD, ADAM_STEP = 0.001, 0.9, 0.999, 1e-08, 0.01, 10

TB_PROJ = 512
TB_PROJ_BWD = 256
TB_MIX = 256
TB_LRU = 256
TQ_ATT = 512
TQ_ATT_FWD = 1024
TK_ATT = 1024
VMEM_LIMIT = 56 * 1024 * 1024


def _mm(a, b):
    return jnp.dot(a.astype(BF16), b.astype(BF16), preferred_element_type=F32)


def _mm_nt(a, b):
    return lax.dot_general(a.astype(BF16), b.astype(BF16), (((1,), (1,)), ((), ())), preferred_element_type=F32)


def _mm_tn(a, b):
    return lax.dot_general(a.astype(BF16), b.astype(BF16), (((0,), (0,)), ((), ())), preferred_element_type=F32)


def _rows(tb, w):
    return pl.BlockSpec((tb, w), lambda i: (i, 0))


def _const(shape):
    n = len(shape)
    return pl.BlockSpec(shape, lambda i: (0,) * n)


def _params(sem=("arbitrary",)):
    return pltpu.CompilerParams(dimension_semantics=sem, vmem_limit_bytes=VMEM_LIMIT)


def _iota(shape, dim):
    return lax.broadcasted_iota(jnp.int32, shape, dim)


def _rope_tables(pos_col, inv_lane, tb):
    s = pos_col.shape[0]

    def body(pos_ref, inv_ref, c_ref, sa_ref, sb_ref):
        ang = pos_ref[...].astype(F32) * inv_ref[...]
        lane = _iota(ang.shape, 1)
        cs, sn = jnp.cos(ang), jnp.sin(ang)
        c_ref[...] = jnp.where(lane < 64, 1.0, jnp.where(lane < 96, cs, 0.0))
        sa_ref[...] = jnp.where((lane >= 64) & (lane < 80), -sn, 0.0)
        sb_ref[...] = jnp.where((lane >= 80) & (lane < 96), sn, 0.0)

    shp = jax.ShapeDtypeStruct((s, HEAD_PAD), F32)
    return pl.pallas_call(
        body, name="rope_tables", grid=(s // tb,), out_shape=(shp, shp, shp),
        in_specs=[_rows(tb, 1), _const((1, HEAD_PAD))], out_specs=(_rows(tb, HEAD_PAD),) * 3,
        compiler_params=_params(("parallel",)),
    )(pos_col, inv_lane)


def _rope(t, c, sa, sb):
    return t * c + pltpu.roll(t, 112, 1) * sa + pltpu.roll(t, 16, 1) * sb


def _rope_t(d, c, sa, sb):
    return d * c + pltpu.roll(d * sa, 16, 1) + pltpu.roll(d * sb, 112, 1)


def _rms(c, g):
    r = lax.rsqrt(jnp.mean(c * c, axis=-1, keepdims=True) + NORM_EPS)
    xh = c * r
    return xh * g, xh, r


def _mla_proj_fwd(x, win, gq, gkv, wuq, wukv_k, wukv_v, ctab, satab, sbtab):
    s = x.shape[0]
    tb = min(TB_PROJ, s)

    def body(x_ref, win_ref, gq_ref, gkv_ref, wuq_ref, wk_ref, wv_ref, c_ref, sa_ref, sb_ref,
             gate_ref, qm_ref, cq_ref, ckv_ref, q_ref, qt_ref, k_ref, v_ref, vt_ref):
        z = _mm(x_ref[...], win_ref[...])
        gate_ref[...] = z[:, 0:1024]
        qm_ref[...] = z[:, 1024:1280].astype(BF16)
        cq = z[:, 1280:1664]
        ckv = z[:, 1664:1920]
        cq_ref[...] = cq
        ckv_ref[...] = ckv
        c, sa, sb = c_ref[...], sa_ref[...], sb_ref[...]
        nq, _, _ = _rms(cq, gq_ref[...])
        nkv, _, _ = _rms(ckv, gkv_ref[...])
        qf = _mm(nq, wuq_ref[...])
        kf = _mm(nkv, wk_ref[...])
        vf = _mm(nkv, wv_ref[...])
        v_ref[...] = vf.astype(BF16)
        for j in range(N_TOK_HEADS // 2):
            sl = slice(HEAD_PAD * j, HEAD_PAD * (j + 1))
            vt_ref[sl, :] = vf[:, sl].T.astype(BF16)
        kr = _rope(z[:, 1920:2048], c, sa, sb)
        for h in range(N_TOK_HEADS):
            sl = slice(HEAD_PAD * h, HEAD_PAD * (h + 1))
            qh = _rope(qf[:, sl], c, sa, sb)
            q_ref[:, sl] = qh.astype(BF16)
            qt_ref[sl, :] = qh.T.astype(BF16)
            k_ref[:, sl] = (kf[:, sl] + kr).astype(BF16)

    outs = (jax.ShapeDtypeStruct((s, 1024), F32), jax.ShapeDtypeStruct((s, MEM_W), BF16),
            jax.ShapeDtypeStruct((s, Q_LORA), F32), jax.ShapeDtypeStruct((s, KV_LORA), F32),
            jax.ShapeDtypeStruct((s, QK_W), BF16), jax.ShapeDtypeStruct((QK_W, s), BF16),
            jax.ShapeDtypeStruct((s, QK_W), BF16),
            jax.ShapeDtypeStruct((s, TOK_W), BF16), jax.ShapeDtypeStruct((TOK_W, s), BF16))

    def cols(w):
        return pl.BlockSpec((w, tb), lambda i: (0, i))

    return pl.pallas_call(
        body, name="mla_proj_fwd", grid=(s // tb,), out_shape=outs,
        in_specs=[_rows(tb, 1024), _const((1024, 2048)), _const((1, Q_LORA)), _const((1, KV_LORA)),
                  _const((Q_LORA, QK_W)), _const((KV_LORA, QK_W)), _const((KV_LORA, TOK_W)),
                  _rows(tb, HEAD_PAD), _rows(tb, HEAD_PAD), _rows(tb, HEAD_PAD)],
        out_specs=(_rows(tb, 1024), _rows(tb, MEM_W), _rows(tb, Q_LORA), _rows(tb, KV_LORA),
                   _rows(tb, QK_W), cols(QK_W), _rows(tb, QK_W), _rows(tb, TOK_W), cols(TOK_W)),
        compiler_params=_params(("parallel",)),
    )(x, win, gq, gkv, wuq, wukv_k, wukv_v, ctab, satab, sbtab)


def _mla_proj_bwd(x, cq, ckv, dq, dkt, dvt, dgate, dqm, dres, win, gq, gkv, wuq, wukv_k, wukv_v, ctab, satab, sbtab,
                  exchange=None):
    s = x.shape[0]
    tb = min(TB_PROJ_BWD, s)

    def body(x_ref, cq_ref, ckv_ref, dq_ref, dkt_ref, dvt_ref, dgate_ref, dqm_ref, dres_ref, win_ref, gq_ref, gkv_ref,
             wuq_ref, wk_ref, wv_ref, c_ref, sa_ref, sb_ref,
             dx_ref, dwin_ref, dwuq_ref, dwk_ref, dwv_ref, dgq_ref, dgkv_ref):
        @pl.when(pl.program_id(0) == 0)
        def _():
            for r in (dwin_ref, dwuq_ref, dwk_ref, dwv_ref, dgq_ref, dgkv_ref):
                r[...] = jnp.zeros_like(r)

        c, sa, sb = c_ref[...], sa_ref[...], sb_ref[...]
        lane = _iota((tb, HEAD_PAD), 1)
        gq, gkv = gq_ref[...], gkv_ref[...]
        nq, xhq, rq = _rms(cq_ref[...], gq)
        nkv, xhk, rk = _rms(ckv_ref[...], gkv)
        dkp = dkt_ref[...].T
        dqs, dkr = [], jnp.zeros((tb, HEAD_PAD), F32)
        for h in range(N_TOK_HEADS):
            sl = slice(HEAD_PAD * h, HEAD_PAD * (h + 1))
            dqs.append(_rope_t(dq_ref[:, sl], c, sa, sb).astype(BF16))
            dkr = dkr + dkp[:, sl]
        dqf = jnp.concatenate(dqs, axis=1)
        dkr = jnp.where((lane >= 64) & (lane < 96), _rope_t(dkr, c, sa, sb), 0.0)
        dvb = dvt_ref[...].T.astype(BF16)
        dkb = dkp.astype(BF16)
        dnq = _mm_nt(dqf, wuq_ref[...])
        dwuq_ref[...] += _mm_tn(nq, dqf)
        dgq_ref[...] += jnp.sum(dnq * xhq, axis=0, keepdims=True)
        dxh = dnq * gq
        dcq = rq * (dxh - xhq * jnp.mean(dxh * xhq, axis=-1, keepdims=True))
        dnkv = _mm_nt(dkb, wk_ref[...]) + _mm_nt(dvb, wv_ref[...])
        nkvb = nkv.astype(BF16)
        dwk_ref[...] += _mm_tn(nkvb, dkb)
        dwv_ref[...] += _mm_tn(nkvb, dvb)
        dgkv_ref[...] += jnp.sum(dnkv * xhk, axis=0, keepdims=True)
        dxh = dnkv * gkv
        dckv = rk * (dxh - xhk * jnp.mean(dxh * xhk, axis=-1, keepdims=True))
        dz = jnp.concatenate([dgate_ref[...], dqm_ref[...], dcq, dckv, dkr], axis=1).astype(BF16)
        dx_ref[...] = _mm_nt(dz, win_ref[...]) + dres_ref[...]
        dwin_ref[...] += _mm_tn(x_ref[...], dz)

    outs = (jax.ShapeDtypeStruct((s, 1024), F32), jax.ShapeDtypeStruct((1024, 2048), F32),
            jax.ShapeDtypeStruct((Q_LORA, QK_W), F32), jax.ShapeDtypeStruct((KV_LORA, QK_W), F32),
            jax.ShapeDtypeStruct((KV_LORA, TOK_W), F32), jax.ShapeDtypeStruct((1, Q_LORA), F32),
            jax.ShapeDtypeStruct((1, KV_LORA), F32))
    return _run(
        body, name="mla_proj_bwd", grid=(s // tb,), out_shape=outs,
        in_specs=[_rows(tb, 1024), _rows(tb, Q_LORA), _rows(tb, KV_LORA), _rows(tb, QK_W),
                  pl.BlockSpec((QK_W, tb), lambda i: (0, i)), pl.BlockSpec((TOK_W, tb), lambda i: (0, i)),
                  _rows(tb, 1024), _rows(tb, MEM_W), _rows(tb, 1024),
                  _const((1024, 2048)), _const((1, Q_LORA)), _const((1, KV_LORA)),
                  _const((Q_LORA, QK_W)), _const((KV_LORA, QK_W)), _const((KV_LORA, TOK_W)),
                  _rows(tb, HEAD_PAD), _rows(tb, HEAD_PAD), _rows(tb, HEAD_PAD)],
        out_specs=(_rows(tb, 1024), _const((1024, 2048)), _const((Q_LORA, QK_W)), _const((KV_LORA, QK_W)),
                   _const((KV_LORA, TOK_W)), _const((1, Q_LORA)), _const((1, KV_LORA))),
        args=(x, cq, ckv, dq, dkt, dvt, dgate, dqm, dres, win, gq, gkv, wuq, wukv_k, wukv_v, ctab, satab, sbtab),
        sem=("arbitrary",), exchange=exchange)


def _attn_fwd(q, k, vt, exchange=None):
    s = q.shape[0]
    tq = min(TQ_ATT_FWD, s)
    tk = min(TK_ATT, s)

    def body(q_ref, k_ref, vt_ref, o_ref, lse_ref):
        i = pl.program_id(1)
        nfull = (i * tq) // tk
        krow = _iota((tk, tq), 0)
        qpos = i * tq + _iota((tk, tq), 1)

        def head_tile(hh, st, carry, masked):
            hs = slice(HEAD_PAD * hh, HEAD_PAD * (hh + 1))
            m, l, acc = carry
            sc = _mm_nt(k_ref[pl.ds(st, tk), hs], q_ref[:, hs]) * ATT_SCALE_LOG2
            if masked:
                sc = jnp.where(st + krow <= qpos, sc, -jnp.inf)
            m_new = jnp.maximum(m, jnp.max(sc, axis=0, keepdims=True))
            p = jnp.exp2(sc - m_new)
            a = jnp.exp2(m - m_new)
            l = a * l + jnp.sum(p, axis=0, keepdims=True)
            acc = a * acc + _mm(vt_ref[64 * hh:64 * (hh + 1), pl.ds(st, tk)], p)
            return m_new, l, acc

        def tile(j, carry, masked):
            st = pl.multiple_of(j * tk, tk)
            return tuple(head_tile(hh, st, carry[hh], masked) for hh in range(2))

        def init():
            return (jnp.full((1, tq), -jnp.inf, F32), jnp.zeros((1, tq), F32), jnp.zeros((64, tq), F32))

        carry = lax.fori_loop(0, nfull, functools.partial(tile, masked=False), (init(), init()))
        (ma, la, acca), (mb, lb, accb) = tile(nfull, carry, True)
        o_ref[...] = jnp.concatenate([acca / la, accb / lb], axis=0).T
        lse_ref[...] = jnp.concatenate([jnp.broadcast_to(ma + jnp.log2(la), (64, tq)),
                                        jnp.broadcast_to(mb + jnp.log2(lb), (64, tq))], axis=0).T

    shp = jax.ShapeDtypeStruct((s, TOK_W), F32)
    return _run(
        body, name="attn_fwd", grid=(N_TOK_HEADS // 2, s // tq), out_shape=(shp, shp),
        in_specs=[pl.BlockSpec((tq, 2 * HEAD_PAD), lambda j, i: (i, j)),
                  pl.BlockSpec((s, 2 * HEAD_PAD), lambda j, i: (0, j)),
                  pl.BlockSpec((HEAD_PAD, s), lambda j, i: (j, 0))],
        out_specs=(pl.BlockSpec((tq, HEAD_PAD), lambda j, i: (i, j)),) * 2,
        args=(q, k, vt), sem=("parallel", "arbitrary"), exchange=exchange)


def _attn_prep(o, do, lse):
    s = o.shape[0]
    tb = min(TB_PROJ, s)
    npair = N_TOK_HEADS // 2

    def body(o_ref, do_ref, lse_ref, dob_ref, dot_ref, st_ref):
        lane = _iota((tb, HEAD_PAD), 1)
        do = do_ref[...]
        dob_ref[...] = do.astype(BF16)
        prod = do * o_ref[...]
        for j in range(npair):
            sl = slice(HEAD_PAD * j, HEAD_PAD * (j + 1))
            dot_ref[sl, :] = do[:, sl].T.astype(BF16)
            pj = prod[:, sl]
            da = jnp.sum(jnp.where(lane < 64, pj, 0.0), axis=-1, keepdims=True)
            db = jnp.sum(jnp.where(lane >= 64, pj, 0.0), axis=-1, keepdims=True)
            la = lse_ref[:, HEAD_PAD * j:HEAD_PAD * j + 1]
            lb = lse_ref[:, HEAD_PAD * j + 64:HEAD_PAD * j + 65]
            st_ref[j] = jnp.where(lane == 0, la, jnp.where(lane == 1, lb, jnp.where(lane == 2, da,
                                                                                     jnp.where(lane == 3, db, 0.0))))

    return pl.pallas_call(
        body, name="attn_prep", grid=(s // tb,),
        out_shape=(jax.ShapeDtypeStruct((s, TOK_W), BF16), jax.ShapeDtypeStruct((TOK_W, s), BF16),
                   jax.ShapeDtypeStruct((npair, s, HEAD_PAD), F32)),
        in_specs=[_rows(tb, TOK_W)] * 3,
        out_specs=(_rows(tb, TOK_W), pl.BlockSpec((TOK_W, tb), lambda i: (0, i)),
                   pl.BlockSpec((npair, tb, HEAD_PAD), lambda i: (0, i, 0))),
        compiler_params=_params(("parallel",)),
    )(o, do, lse)


def _attn_bwd(q, qt, k, v, dob, dobt, stats, exchange=None):
    s = q.shape[0]
    t = min(TQ_ATT, s)
    nq = s // t

    def body(q_ref, qt_ref, do_ref, dot_ref, st_ref, k_ref, v_ref, dq_ref, dkt_ref, dvt_ref):
        i = pl.program_id(1)

        @pl.when(i == 0)
        def _():
            dkt_ref[...] = jnp.zeros_like(dkt_ref)
            dvt_ref[...] = jnp.zeros_like(dvt_ref)

        lane = _iota((t, HEAD_PAD), 1)
        qpos, kcol = _iota((t, t), 0), _iota((t, t), 1)
        do = do_ref[...]
        stats = st_ref[0]

        def head_tile(hh, ks, dq_acc, masked):
            hs = slice(HEAD_PAD * hh, HEAD_PAD * (hh + 1))
            qh = q_ref[:, hs]
            kh = k_ref[pl.ds(ks, t), hs]
            dom = jnp.where((lane < 64) if hh == 0 else (lane >= 64), do, jnp.zeros_like(do))
            lse = stats[:, hh:hh + 1]
            dlt = stats[:, 2 + hh:3 + hh]
            sc = _mm_nt(qh, kh) * ATT_SCALE_LOG2
            if masked:
                sc = jnp.where(kcol <= qpos, sc, -jnp.inf)
            p = jnp.exp2(sc - lse)
            dp = _mm_nt(dom, v_ref[pl.ds(ks, t), :])
            ds = (p * (dp - dlt) * ATT_SCALE).astype(BF16)
            dvt_ref[64 * hh:64 * (hh + 1), pl.ds(ks, t)] += _mm(dot_ref[64 * hh:64 * (hh + 1), :], p)
            dkt_ref[HEAD_PAD * hh:HEAD_PAD * hh + 96, pl.ds(ks, t)] += _mm(qt_ref[HEAD_PAD * hh:HEAD_PAD * hh + 96, :], ds)
            return dq_acc + _mm(ds, kh)

        def tile(j, carry, masked):
            ks = pl.multiple_of(j * t, t)
            return tuple(head_tile(hh, ks, carry[hh], masked) for hh in range(2))

        zero = jnp.zeros((t, HEAD_PAD), F32)
        carry = lax.fori_loop(0, i, functools.partial(tile, masked=False), (zero, zero))
        dqa, dqb = tile(i, carry, True)
        dq_ref[...] = jnp.concatenate([dqa, dqb], axis=1)

    return _run(
        body, name="attn_bwd", grid=(N_TOK_HEADS // 2, nq),
        out_shape=(jax.ShapeDtypeStruct((s, QK_W), F32), jax.ShapeDtypeStruct((QK_W, s), F32),
                   jax.ShapeDtypeStruct((TOK_W, s), F32)),
        in_specs=[pl.BlockSpec((t, 2 * HEAD_PAD), lambda j, i: (i, j)),
                  pl.BlockSpec((2 * HEAD_PAD, t), lambda j, i: (j, i)),
                  pl.BlockSpec((t, HEAD_PAD), lambda j, i: (i, j)),
                  pl.BlockSpec((HEAD_PAD, t), lambda j, i: (j, i)),
                  pl.BlockSpec((1, t, HEAD_PAD), lambda j, i: (j, i, 0)),
                  pl.BlockSpec((s, 2 * HEAD_PAD), lambda j, i: (0, j)),
                  pl.BlockSpec((s, HEAD_PAD), lambda j, i: (0, j))],
        out_specs=(pl.BlockSpec((t, 2 * HEAD_PAD), lambda j, i: (i, j)),
                   pl.BlockSpec((2 * HEAD_PAD, s), lambda j, i: (j, 0)),
                   pl.BlockSpec((HEAD_PAD, s), lambda j, i: (j, 0))),
        args=(q, qt, dob, dobt, stats, k, v), sem=("parallel", "arbitrary"), exchange=exchange)


def _mem_kv(mem, wmem):
    def body(m_ref, w_ref, o_ref):
        o_ref[0] = _mm(m_ref[...], w_ref[0]).astype(BF16)

    return pl.pallas_call(
        body, name="mem_kv", grid=(2,), out_shape=jax.ShapeDtypeStruct((2, MEM_LEN, 512), BF16),
        in_specs=[_const((MEM_LEN, 1024)), pl.BlockSpec((1, 1024, 512), lambda l: (l, 0, 0))],
        out_specs=pl.BlockSpec((1, MEM_LEN, 512), lambda l: (l, 0, 0)),
        compiler_params=_params(("parallel",)),
    )(mem, wmem)


def _mem_kv_bwd(mem, dmemkv):
    def body(m_ref, d_ref, o_ref):
        o_ref[...] = _mm_tn(m_ref[...], d_ref[...])

    return pl.pallas_call(
        body, name="mem_kv_bwd", grid=(1,), out_shape=jax.ShapeDtypeStruct((1024, 512), F32),
        in_specs=[_const((MEM_LEN, 1024)), _const((MEM_LEN, 512))], out_specs=_const((1024, 512)),
        compiler_params=_params(("arbitrary",)),
    )(mem, dmemkv)


def _head_mask(lane, sub):
    return (lane < 64) if sub == 0 else (lane >= 64)


def _mem_attn(qm, kv):
    tb = qm.shape[0]
    lane = _iota((tb, HEAD_PAD), 1)
    outs, ps = [], []
    for pp in range(2):
        qp = qm[:, HEAD_PAD * pp:HEAD_PAD * (pp + 1)]
        kp = kv[:, HEAD_PAD * pp:HEAD_PAD * (pp + 1)]
        vp = kv[:, MEM_W + HEAD_PAD * pp:MEM_W + HEAD_PAD * (pp + 1)]
        pair = None
        for sub in range(2):
            qh = jnp.where(_head_mask(lane, sub), qp, jnp.zeros_like(qp))
            sc = _mm_nt(qh, kp) * 0.125
            e = jnp.exp(sc - jnp.max(sc, axis=-1, keepdims=True))
            p = e / jnp.sum(e, axis=-1, keepdims=True)
            o = _mm(p, vp)
            ps.append(p)
            pair = o if sub == 0 else jnp.where(lane < 64, pair, o)
        outs.append(pair)
    return jnp.concatenate(outs, axis=1), ps


def _mem_attn_bwd(dmo, qm, kv, ps):
    tb = qm.shape[0]
    lane = _iota((tb, HEAD_PAD), 1)
    dqs, dks, dvs = [], [], []
    for pp in range(2):
        qp = qm[:, HEAD_PAD * pp:HEAD_PAD * (pp + 1)]
        kp = kv[:, HEAD_PAD * pp:HEAD_PAD * (pp + 1)]
        vp = kv[:, MEM_W + HEAD_PAD * pp:MEM_W + HEAD_PAD * (pp + 1)]
        dop = dmo[:, HEAD_PAD * pp:HEAD_PAD * (pp + 1)]
        dq_pair, dk_pair, dv_pair = None, None, None
        for sub in range(2):
            msk = _head_mask(lane, sub)
            p = ps[2 * pp + sub]
            qh = jnp.where(msk, qp, jnp.zeros_like(qp))
            doh = jnp.where(msk, dop, 0.0).astype(BF16)
            dv = _mm_tn(p, doh)
            dp = _mm_nt(doh, vp)
            ds = (p * (dp - jnp.sum(dp * p, axis=-1, keepdims=True)) * 0.125).astype(BF16)
            dq = _mm(ds, kp)
            dk = _mm_tn(ds, qh)
            if sub == 0:
                dq_pair, dk_pair, dv_pair = dq, dk, dv
            else:
                dq_pair = jnp.where(lane < 64, dq_pair, dq)
                dk_pair, dv_pair = dk_pair + dk, dv_pair + dv
        dqs.append(dq_pair)
        dks.append(dk_pair)
        dvs.append(dv_pair)
    return jnp.concatenate(dqs, axis=1), jnp.concatenate(dks + dvs, axis=1)


def _mix_core(tok, gate, qm, kv, wout, h_in, g, b):
    mem_out, ps = _mem_attn(qm, kv)
    cat = jnp.concatenate([tok, mem_out], axis=1)
    sg = jax.nn.sigmoid(gate)
    sl = gate * sg
    y = cat * sl
    r = ALPHA * h_in + _mm(y, wout)
    mu = jnp.mean(r, axis=-1, keepdims=True)
    xc = r - mu
    rstd = lax.rsqrt(jnp.mean(xc * xc, axis=-1, keepdims=True) + NORM_EPS)
    xh = xc * rstd
    return xh * g + b, (ps, cat, sg, sl, y, xh, rstd)


def _mix_fwd(tok, gate, qm, kv, wout, h_in, g, b):
    s = tok.shape[0]
    tb = min(TB_MIX, s)

    def body(tok_ref, gate_ref, qm_ref, kv_ref, w_ref, h_ref, g_ref, b_ref, o_ref):
        o_ref[...], _ = _mix_core(tok_ref[...], gate_ref[...], qm_ref[...], kv_ref[...], w_ref[...], h_ref[...],
                                  g_ref[...], b_ref[...])

    return pl.pallas_call(
        body, name="mix_fwd", grid=(s // tb,), out_shape=jax.ShapeDtypeStruct((s, 1024), F32),
        in_specs=[_rows(tb, TOK_W), _rows(tb, 1024), _rows(tb, MEM_W), _const((MEM_LEN, 512)), _const((1024, 1024)),
                  _rows(tb, 1024), _const((1, 1024)), _const((1, 1024))],
        out_specs=_rows(tb, 1024), compiler_params=_params(("parallel",)),
    )(tok, gate, qm, kv, wout, h_in, g, b)


def _mix_bwd(tok, gate, qm, kv, wout, h_in, g, b, up, from_loss, exchange=None):
    s = tok.shape[0]
    tb = min(TB_MIX, s)

    def body(tok_ref, gate_ref, qm_ref, kv_ref, w_ref, h_ref, g_ref, b_ref, up_ref,
             dres_ref, dtok_ref, dgate_ref, dqm_ref, dw_ref, dkv_ref, dg_ref, db_ref, loss_ref):
        @pl.when(pl.program_id(0) == 0)
        def _():
            for r in (dw_ref, dkv_ref, dg_ref, db_ref, loss_ref):
                r[...] = jnp.zeros_like(r)

        gate, qm, kv, wout, g = gate_ref[...], qm_ref[...], kv_ref[...], w_ref[...], g_ref[...]
        h_out, (ps, cat, sg, sl, y, xh, rstd) = _mix_core(tok_ref[...], gate, qm, kv, wout, h_ref[...], g, b_ref[...])
        if from_loss:
            diff = h_out - up_ref[...]
            loss_ref[...] += 0.5 * jnp.sum(jnp.mean(diff * diff, axis=-1, keepdims=True), axis=0, keepdims=True)
            dh = diff * (1.0 / D_MODEL)
        else:
            dh = up_ref[...]
        dg_ref[...] += jnp.sum(dh * xh, axis=0, keepdims=True)
        db_ref[...] += jnp.sum(dh, axis=0, keepdims=True)
        dxh = dh * g
        dr = rstd * (dxh - jnp.mean(dxh, axis=-1, keepdims=True) - xh * jnp.mean(dxh * xh, axis=-1, keepdims=True))
        dres_ref[...] = ALPHA * dr
        drb = dr.astype(BF16)
        dy = _mm_nt(drb, wout)
        dw_ref[...] += _mm_tn(y, drb)
        dcat = dy * sl
        dgate_ref[...] = dy * cat * (sg * (1.0 + gate * (1.0 - sg)))
        dtok_ref[...] = dcat[:, :TOK_W]
        dqm, dkv = _mem_attn_bwd(dcat[:, TOK_W:], qm, kv, ps)
        dqm_ref[...] = dqm
        dkv_ref[...] += dkv

    outs = (jax.ShapeDtypeStruct((s, 1024), F32), jax.ShapeDtypeStruct((s, TOK_W), F32),
            jax.ShapeDtypeStruct((s, 1024), F32), jax.ShapeDtypeStruct((s, MEM_W), F32),
            jax.ShapeDtypeStruct((1024, 1024), F32), jax.ShapeDtypeStruct((MEM_LEN, 512), F32),
            jax.ShapeDtypeStruct((1, 1024), F32), jax.ShapeDtypeStruct((1, 1024), F32),
            jax.ShapeDtypeStruct((1, 1), F32))
    return _run(
        body, name="mix_bwd_loss" if from_loss else "mix_bwd", grid=(s // tb,), out_shape=outs,
        in_specs=[_rows(tb, TOK_W), _rows(tb, 1024), _rows(tb, MEM_W), _const((MEM_LEN, 512)), _const((1024, 1024)),
                  _rows(tb, 1024), _const((1, 1024)), _const((1, 1024)), _rows(tb, 1024)],
        out_specs=(_rows(tb, 1024), _rows(tb, TOK_W), _rows(tb, 1024), _rows(tb, MEM_W), _const((1024, 1024)),
                   _const((MEM_LEN, 512)), _const((1, 1024)), _const((1, 1024)), _const((1, 1))),
        args=(tok, gate, qm, kv, wout, h_in, g, b, up), sem=("arbitrary",), exchange=exchange)


def _shift_down(u, tail, k):
    if k == 0:
        return u
    r = pltpu.roll(u, k, 0)
    row8 = _iota((8, u.shape[1]), 0)
    head = jnp.where(row8 < k, pltpu.roll(tail, k, 0), r[:8])
    return jnp.concatenate([head, r[8:]], axis=0)


def _shift_up(d, head, k):
    if k == 0:
        return d
    n = d.shape[0]
    r = pltpu.roll(d, n - k, 0)
    row8 = _iota((8, d.shape[1]), 0)
    last = jnp.where(row8 >= 8 - k, pltpu.roll(head, 8 - k, 0), r[n - 8:])
    return jnp.concatenate([r[:n - 8], last], axis=0)


def _scan_down(a, b):
    n = a.shape[0]
    row = _iota(a.shape, 0)
    s = 1
    while s < n:
        ok = row >= s
        a_s = jnp.where(ok, pltpu.roll(a, s, 0), 1.0)
        b_s = jnp.where(ok, pltpu.roll(b, s, 0), 0.0)
        b = a * b_s + b
        a = a * a_s
        s *= 2
    return a, b


def _scan_up(a, b):
    n = a.shape[0]
    row = _iota(a.shape, 0)
    s = 1
    while s < n:
        ok = row < n - s
        a_s = jnp.where(ok, pltpu.roll(a, n - s, 0), 1.0)
        b_s = jnp.where(ok, pltpu.roll(b, n - s, 0), 0.0)
        b = a * b_s + b
        a = a * a_s
        s *= 2
    return a, b


def _neg_expm1(x):
    poly = -x * (1.0 + x * (0.5 + x * (1.0 / 6.0 + x * (1.0 / 24.0 + x * (1.0 / 120.0)))))
    return jnp.where(x > -0.1, poly, 1.0 - jnp.exp(x))


def _softplus(x):
    return jnp.maximum(x, 0.0) + jnp.log(1.0 + jnp.exp(-jnp.abs(x)))


def _lru_gates(u, tail, cw, cb, wr, br, wi, bi, lam):
    us = [_shift_down(u, tail, k) for k in range(4)]
    xc = cb + us[3] * cw[0:1] + us[2] * cw[1:2] + us[1] * cw[2:3] + us[0] * cw[3:4]
    xb = xc.astype(BF16)
    pre_r = jnp.concatenate([_mm(xb[:, 256 * g:256 * (g + 1)], wr[g]) for g in range(3)], axis=1) + br
    pre_i = jnp.concatenate([_mm(xb[:, 256 * g:256 * (g + 1)], wi[g]) for g in range(3)], axis=1) + bi
    rg, ig = jax.nn.sigmoid(pre_r), jax.nn.sigmoid(pre_i)
    clam = -LRU_C * _softplus(-lam)
    la = clam * rg
    a = jnp.exp(la)
    mm = jnp.sqrt(_neg_expm1(2.0 * la))
    return us, xc, xb, rg, ig, clam, la, a, mm


def _lru_fwd(h, win, cw, cb, wr, br, wi, bi, lam):
    s = h.shape[0]
    tb = min(TB_LRU, s)

    def body(h_ref, win_ref, cw_ref, cb_ref, wr_ref, br_ref, wi_ref, bi_ref, lam_ref,
             u_ref, gate_ref, qm_ref, hs_ref, tail_sc, carry_sc):
        @pl.when(pl.program_id(0) == 0)
        def _():
            tail_sc[...] = jnp.zeros_like(tail_sc)
            carry_sc[...] = jnp.zeros_like(carry_sc)

        z = _mm(h_ref[...], win_ref[...])
        u = z[:, :TOK_W]
        u_ref[...] = u
        gate_ref[...] = z[:, TOK_W:TOK_W + 1024]
        qm_ref[...] = z[:, TOK_W + 1024:].astype(BF16)
        _, xc, _, _, ig, _, _, a, mm = _lru_gates(u, tail_sc[...], cw_ref[...], cb_ref[...], wr_ref[...], br_ref[...],
                                                 wi_ref[...], bi_ref[...], lam_ref[...])
        big_a, big_b = _scan_down(a, mm * (ig * xc))
        hs = big_a * carry_sc[0:1, :] + big_b
        hs_ref[...] = hs
        tail_sc[...] = u[tb - 8:, :]
        carry_sc[...] = jnp.broadcast_to(hs[tb - 1:tb, :], carry_sc.shape)

    outs = (jax.ShapeDtypeStruct((s, TOK_W), F32), jax.ShapeDtypeStruct((s, 1024), F32),
            jax.ShapeDtypeStruct((s, MEM_W), BF16), jax.ShapeDtypeStruct((s, TOK_W), F32))
    return pl.pallas_call(
        body, name="lru_fwd", grid=(s // tb,), out_shape=outs,
        in_specs=[_rows(tb, 1024), _const((1024, 2048)), _const((4, TOK_W)), _const((1, TOK_W)),
                  _const((3, 256, 256)), _const((1, TOK_W)), _const((3, 256, 256)), _const((1, TOK_W)),
                  _const((1, TOK_W))],
        out_specs=(_rows(tb, TOK_W), _rows(tb, 1024), _rows(tb, MEM_W), _rows(tb, TOK_W)),
        scratch_shapes=[pltpu.VMEM((8, TOK_W), F32), pltpu.VMEM((8, TOK_W), F32)],
        compiler_params=_params(),
    )(h, win, cw, cb, wr, br, wi, bi, lam)


def _lru_bwd(dhs, dgate, dqm, dres, h, u, hs, win, cw, cb, wr, br, wi, bi, lam):
    s = h.shape[0]
    tb = min(TB_LRU, s)
    nb = s // tb

    def rev(w):
        return pl.BlockSpec((tb, w), lambda i: (nb - 1 - i, 0))

    def prev_tail(w):
        return pl.BlockSpec((8, w), lambda i: (jnp.maximum((nb - 1 - i) * (tb // 8) - 1, 0), 0))

    def body(dhs_ref, dgate_ref, dqm_ref, dres_ref, h_ref, u_ref, hs_ref, ut_ref, hst_ref, win_ref, cw_ref, cb_ref,
             wr_ref, br_ref, wi_ref, bi_ref, lam_ref,
             dh_ref, dwin_ref, dcw_ref, dcb_ref, dwr_ref, dbr_ref, dwi_ref, dbi_ref, dlam_ref, ecar_sc, dxc_sc):
        i = pl.program_id(0)

        @pl.when(i == 0)
        def _():
            for r in (dwin_ref, dcw_ref, dcb_ref, dwr_ref, dbr_ref, dwi_ref, dbi_ref, dlam_ref, ecar_sc, dxc_sc):
                r[...] = jnp.zeros_like(r)

        first = (i == nb - 1)
        u = u_ref[...]
        utail = jnp.where(first, 0.0, ut_ref[...])
        hstail = jnp.where(first, 0.0, hst_ref[...])
        cw, wr, wi, lam = cw_ref[...], wr_ref[...], wi_ref[...], lam_ref[...]
        us, xc, xb, rg, ig, clam, la, a, mm = _lru_gates(u, utail, cw, cb_ref[...], wr, br_ref[...], wi, bi_ref[...], lam)
        row = _iota(a.shape, 0)
        a_next = jnp.where(row < tb - 1, pltpu.roll(a, tb - 1, 0), 1.0)
        big_a, big_b = _scan_up(a_next, dhs_ref[...])
        e = big_a * ecar_sc[0:1, :] + big_b
        ecar_sc[...] = jnp.broadcast_to(a[0:1, :] * e[0:1, :], ecar_sc.shape)
        hs_prev = _shift_down(hs_ref[...], hstail, 1)
        da = e * hs_prev
        ix = ig * xc
        dmm = e * ix
        dix = e * mm
        dla = da * a - dmm * (a * a) / mm
        dlam_ref[...] += jnp.sum(dla * rg, axis=0, keepdims=True)
        dpr = (dla * clam) * rg * (1.0 - rg)
        dpi = (dix * xc) * ig * (1.0 - ig)
        dbr_ref[...] += jnp.sum(dpr, axis=0, keepdims=True)
        dbi_ref[...] += jnp.sum(dpi, axis=0, keepdims=True)
        dprb, dpib = dpr.astype(BF16), dpi.astype(BF16)
        dxc_g = []
        for g in range(3):
            sl = slice(256 * g, 256 * (g + 1))
            dwr_ref[g] += _mm_tn(xb[:, sl], dprb[:, sl])
            dwi_ref[g] += _mm_tn(xb[:, sl], dpib[:, sl])
            dxc_g.append(_mm_nt(dprb[:, sl], wr[g]) + _mm_nt(dpib[:, sl], wi[g]))
        dxc = dix * ig + jnp.concatenate(dxc_g, axis=1)
        dcb_ref[...] += jnp.sum(dxc, axis=0, keepdims=True)
        dcw_ref[...] += jnp.concatenate([jnp.sum(dxc * us[3 - tap], axis=0, keepdims=True) for tap in range(4)], axis=0)
        head = dxc_sc[...]
        du = dxc * cw[3:4]
        for k in range(1, 4):
            du = du + _shift_up(dxc, head, k) * cw[3 - k:4 - k]
        dxc_sc[...] = dxc[:8, :]
        dz = jnp.concatenate([du, dgate_ref[...], dqm_ref[...]], axis=1).astype(BF16)
        dh_ref[...] = _mm_nt(dz, win_ref[...]) + dres_ref[...]
        dwin_ref[...] += _mm_tn(h_ref[...], dz)

        @pl.when(i == nb - 1)
        def _():
            dlam_ref[...] = dlam_ref[...] * (LRU_C * jax.nn.sigmoid(-lam))

    outs = (jax.ShapeDtypeStruct((s, 1024), F32), jax.ShapeDtypeStruct((1024, 2048), F32),
            jax.ShapeDtypeStruct((4, TOK_W), F32), jax.ShapeDtypeStruct((1, TOK_W), F32),
            jax.ShapeDtypeStruct((3, 256, 256), F32), jax.ShapeDtypeStruct((1, TOK_W), F32),
            jax.ShapeDtypeStruct((3, 256, 256), F32), jax.ShapeDtypeStruct((1, TOK_W), F32),
            jax.ShapeDtypeStruct((1, TOK_W), F32))
    return pl.pallas_call(
        body, name="lru_bwd", grid=(nb,), out_shape=outs,
        in_specs=[rev(TOK_W), rev(1024), rev(MEM_W), rev(1024), rev(1024), rev(TOK_W), rev(TOK_W),
                  prev_tail(TOK_W), prev_tail(TOK_W),
                  _const((1024, 2048)), _const((4, TOK_W)), _const((1, TOK_W)), _const((3, 256, 256)),
                  _const((1, TOK_W)), _const((3, 256, 256)), _const((1, TOK_W)), _const((1, TOK_W))],
        out_specs=(rev(1024), _const((1024, 2048)), _const((4, TOK_W)), _const((1, TOK_W)), _const((3, 256, 256)),
                   _const((1, TOK_W)), _const((3, 256, 256)), _const((1, TOK_W)), _const((1, TOK_W))),
        scratch_shapes=[pltpu.VMEM((8, TOK_W), F32), pltpu.VMEM((8, TOK_W), F32)],
        compiler_params=_params(),
    )(dhs, dgate, dqm, dres, h, u, hs, u, hs, win, cw, cb, wr, br, wi, bi, lam)


def _adamw(name, w, g, m, v):
    rows, cols = w.shape
    tb = 256 if rows % 256 == 0 else rows

    def body(w_ref, g_ref, m_ref, v_ref, d_ref, nm_ref, nv_ref):
        g = g_ref[...]
        nm = ADAM_B1 * m_ref[...] + (1.0 - ADAM_B1) * g
        nv = ADAM_B2 * v_ref[...] + (1.0 - ADAM_B2) * (g * g)
        m_hat = nm / (1.0 - ADAM_B1 ** ADAM_STEP)
        v_hat = nv / (1.0 - ADAM_B2 ** ADAM_STEP)
        d_ref[...] = -ADAM_LR * (m_hat / (jnp.sqrt(v_hat) + ADAM_EPS) + ADAM_WD * w_ref[...])
        nm_ref[...] = nm
        nv_ref[...] = nv

    shp = jax.ShapeDtypeStruct((rows, cols), F32)
    return pl.pallas_call(
        body, name="adamw_" + name, grid=(rows // tb,), out_shape=(shp, shp, shp),
        in_specs=[_rows(tb, cols)] * 4, out_specs=(_rows(tb, cols),) * 3,
        compiler_params=_params(("parallel",)),
    )(w, g, m, v)


def _row_block(rows, cap=2048):
    return max(t for t in range(8, cap + 1, 8) if rows % t == 0)


def _add2(a, b):
    rows = a.shape[0]
    tb = _row_block(rows)

    def body(a_ref, b_ref, o_ref):
        o_ref[...] = a_ref[...] + b_ref[...]

    return pl.pallas_call(
        body, name="add_sibling", grid=(rows // tb,), out_shape=jax.ShapeDtypeStruct(a.shape, F32),
        in_specs=[_rows(tb, 128)] * 2, out_specs=_rows(tb, 128), compiler_params=_params(("parallel",)),
    )(a, b)


def _sum4(r):
    rows = r.shape[1]
    tb = _row_block(rows, 1024)

    def body(r_ref, o_ref):
        o_ref[...] = ((r_ref[0] + r_ref[1]) + r_ref[2]) + r_ref[3]

    return pl.pallas_call(
        body, name="sum_chips", grid=(rows // tb,), out_shape=jax.ShapeDtypeStruct((rows, 128), F32),
        in_specs=[pl.BlockSpec((4, tb, 128), lambda i: (0, i, 0))], out_specs=_rows(tb, 128),
        compiler_params=_params(("parallel",)),
    )(r)


_ANY = pl.BlockSpec(memory_space=pl.ANY)


def _place():
    x, y, c = lax.axis_index("x"), lax.axis_index("y"), lax.axis_index("c")
    return x, y, c, [(1 - x, y), (x, 1 - y), (1 - x, 1 - y)]


def _remote(src, dst, ssem, rsem, to):
    return pltpu.make_async_remote_copy(src_ref=src, dst_ref=dst, send_sem=ssem, recv_sem=rsem, device_id=to,
                                        device_id_type=MESH)


class _Exchange:
    def __init__(self, ins, out_shape, sems, start, finish):
        self.ins, self.out_shape, self.sems, self.start, self.finish = ins, out_shape, sems, start, finish


def _run(body, *, name, grid, in_specs, out_specs, out_shape, args, scratch=(), sem, exchange=None):
    if exchange is None:
        return pl.pallas_call(body, name=name, grid=grid, out_shape=tuple(out_shape), in_specs=list(in_specs),
                              out_specs=tuple(out_specs), scratch_shapes=list(scratch),
                              compiler_params=_params(sem))(*args)
    n_in, n_out, n_sc = len(args), len(out_shape), len(scratch)
    k_in, k_out = len(exchange.ins), len(exchange.out_shape)

    def fused(*refs):
        ins, refs = refs[:n_in], refs[n_in:]
        xin, refs = refs[:k_in], refs[k_in:]
        outs, refs = refs[:n_out], refs[n_out:]
        xout, refs = refs[:k_out], refs[k_out:]
        sc, xsem = refs[:n_sc], refs[n_sc:]
        first = pl.program_id(0) == 0
        last = pl.program_id(0) == grid[0] - 1
        for a in range(1, len(grid)):
            first = first & (pl.program_id(a) == 0)
            last = last & (pl.program_id(a) == grid[a] - 1)

        @pl.when(first)
        def _():
            exchange.start(xin, xout, xsem)

        body(*ins, *outs, *sc)

        @pl.when(last)
        def _():
            exchange.finish(xin, xout, xsem)

    return pl.pallas_call(
        fused, name=name, grid=grid, out_shape=(*out_shape, *exchange.out_shape),
        in_specs=[*in_specs, *[_ANY] * k_in], out_specs=(*out_specs, *[_ANY] * k_out),
        scratch_shapes=[*scratch, *exchange.sems],
        compiler_params=_params(("arbitrary",) * len(grid)),
    )(*args, *exchange.ins)


def _run_exchange(exchange, name):
    def body(*refs):
        k_in, k_out = len(exchange.ins), len(exchange.out_shape)
        xin, xout, xsem = refs[:k_in], refs[k_in:k_in + k_out], refs[k_in + k_out:]
        exchange.start(xin, xout, xsem)
        exchange.finish(xin, xout, xsem)

    return pl.pallas_call(
        body, name=name, out_shape=tuple(exchange.out_shape), in_specs=[_ANY] * len(exchange.ins),
        out_specs=tuple([_ANY] * len(exchange.out_shape)), scratch_shapes=list(exchange.sems),
    )(*exchange.ins)


def _gather_shards(wsh):
    _, hh, _ = wsh.shape

    def first_hop(w_ref, out_ref, ssems, rsems, lsem):
        x, y, c, chips = _place()
        t = 2 * x + y
        mine = pltpu.make_async_copy(w_ref, out_ref.at[t], lsem)
        first = [_remote(w_ref.at[c], out_ref.at[t, c], ssems.at[j], rsems.at[j], (cx, cy, c))
                 for j, (cx, cy) in enumerate(chips)]
        return mine, first

    def start(xin, xout, xsem):
        mine, first = first_hop(xin[0], xout[0], *xsem)
        mine.start()
        for cp in first:
            cp.start()

    def finish(xin, xout, xsem):
        out_ref, (ssems, rsems, _) = xout[0], xsem
        mine, first = first_hop(xin[0], out_ref, *xsem)
        x, y, c, chips = _place()
        passed = []
        for j, (cx, cy) in enumerate(chips):
            got = out_ref.at[2 * cx + cy, c]
            _remote(got, got, ssems.at[j], rsems.at[j], (cx, cy, c)).wait_recv()
            cp = _remote(got, got, ssems.at[3 + j], rsems.at[3 + j], (x, y, 1 - c))
            cp.start()
            passed.append(cp)
        for j, (cx, cy) in enumerate(chips):
            got = out_ref.at[2 * cx + cy, 1 - c]
            _remote(got, got, ssems.at[3 + j], rsems.at[3 + j], (x, y, 1 - c)).wait_recv()
        for cp in first + passed:
            cp.wait_send()
        mine.wait()

    return _Exchange([wsh], [jax.ShapeDtypeStruct((4, 2, hh, 128), wsh.dtype)],
                     [pltpu.SemaphoreType.DMA((6,)), pltpu.SemaphoreType.DMA((6,)), pltpu.SemaphoreType.DMA],
                     start, finish)


def _swap_sibling(v):
    def copy(xin, xout, xsem):
        x, y, c, _ = _place()
        return _remote(xin[0], xout[0], xsem[0], xsem[1], (x, y, 1 - c))

    return _Exchange([v], [jax.ShapeDtypeStruct(v.shape, v.dtype)],
                     [pltpu.SemaphoreType.DMA, pltpu.SemaphoreType.DMA],
                     lambda *a: copy(*a).start(), lambda *a: copy(*a).wait())


def _scatter_chips(p):
    def copies(p_ref, out_ref, ssems, rsems, lsem):
        x, y, c, chips = _place()
        t = 2 * x + y
        mine = pltpu.make_async_copy(p_ref.at[t], out_ref.at[t], lsem)
        cps = [_remote(p_ref.at[2 * cx + cy], out_ref.at[t], ssems.at[j], rsems.at[j], (cx, cy, c))
               for j, (cx, cy) in enumerate(chips)]
        return mine, cps

    def start(xin, xout, xsem):
        mine, cps = copies(xin[0], xout[0], *xsem)
        mine.start()
        for cp in cps:
            cp.start()

    def finish(xin, xout, xsem):
        out_ref, (ssems, rsems, _) = xout[0], xsem
        mine, cps = copies(xin[0], out_ref, *xsem)
        x, y, c, chips = _place()
        for j, (cx, cy) in enumerate(chips):
            got = out_ref.at[2 * cx + cy]
            _remote(got, got, ssems.at[j], rsems.at[j], (cx, cy, c)).wait_recv()
        for cp in cps:
            cp.wait_send()
        mine.wait()

    return _Exchange([p], [jax.ShapeDtypeStruct(p.shape, p.dtype)],
                     [pltpu.SemaphoreType.DMA((3,)), pltpu.SemaphoreType.DMA((3,)), pltpu.SemaphoreType.DMA],
                     start, finish)


def _share_reduced(piece, hs, rr):
    hh = piece.shape[0]

    def copies(t_ref, full_ref, rall_ref, ssems, rsems, lsems):
        x, y, c, _ = _place()
        me = 4 * x + 2 * y + c
        mine_r = t_ref.at[pl.ds(hs, rr)]
        loc = [pltpu.make_async_copy(t_ref, full_ref.at[c], lsems.at[0]),
               pltpu.make_async_copy(mine_r, rall_ref.at[me], lsems.at[1])]
        sends = [_remote(t_ref, full_ref.at[c], ssems.at[0], rsems.at[0], (x, y, 1 - c))]
        peers = []
        for mask in range(1, 8):
            px = 1 - x if mask & 4 else x
            py = 1 - y if mask & 2 else y
            pc = 1 - c if mask & 1 else c
            peers.append((mask, px, py, pc))
            sends.append(_remote(mine_r, rall_ref.at[me], ssems.at[mask], rsems.at[mask], (px, py, pc)))
        return loc, sends, peers

    def start(xin, xout, xsem):
        loc, sends, _ = copies(xin[0], *xout, *xsem)
        for cp in loc + sends:
            cp.start()

    def finish(xin, xout, xsem):
        (full_ref, rall_ref), (ssems, rsems, _) = xout, xsem
        loc, sends, peers = copies(xin[0], *xout, *xsem)
        x, y, c, _ = _place()
        got = full_ref.at[1 - c]
        _remote(got, got, ssems.at[0], rsems.at[0], (x, y, 1 - c)).wait_recv()
        for mask, px, py, pc in peers:
            got = rall_ref.at[4 * px + 2 * py + pc]
            _remote(got, got, ssems.at[mask], rsems.at[mask], (px, py, pc)).wait_recv()
        for cp in sends:
            cp.wait_send()
        for cp in loc:
            cp.wait()

    return _Exchange([piece], [jax.ShapeDtypeStruct((2, hh, 128), F32), jax.ShapeDtypeStruct((8, rr, 128), F32)],
                     [pltpu.SemaphoreType.DMA((8,)), pltpu.SemaphoreType.DMA((8,)), pltpu.SemaphoreType.DMA((2,))],
                     start, finish)


def _ceil_to(n, m):
    return -(-n // m) * m


def _pack_bf16(parts):
    flat = jnp.concatenate(parts)
    hw = _ceil_to(_ceil_to(flat.shape[0], 128) // 128, 32) // 2
    return jnp.pad(flat, (0, 2 * hw * 128 - flat.shape[0])).reshape(2, hw, 128), [p.shape[0] for p in parts]


def _segments(wall, sizes):
    wall = wall.reshape(4, -1)
    offs = [0]
    for n in sizes:
        offs.append(offs[-1] + n)
    return [wall[:, offs[i]:offs[i + 1]] for i in range(len(sizes))]


class _GradReduce:
    def __init__(self, sharded, replicated, c_idx):
        self.sharded = [(n, g.shape[1]) for n, g in sharded]
        self.replicated = [(n, g.shape[0]) for n, g in replicated]
        sh = jnp.concatenate([g for _, g in sharded], axis=1)
        n_sh = sh.shape[1]
        self.hs = _ceil_to(_ceil_to(n_sh, 128) // 128, 16) // 2
        sh = jnp.pad(sh, ((0, 0), (0, 2 * self.hs * 128 - n_sh))).reshape(4, 2, self.hs, 128)
        rp = jnp.concatenate([g for _, g in replicated])
        self.rr = _ceil_to(_ceil_to(rp.shape[0], 128) // 128, 64) // 8
        rp = jnp.pad(rp, (0, 8 * self.rr * 128 - rp.shape[0])).reshape(4, 2, self.rr, 128)
        gbuf = jnp.concatenate([sh, rp], axis=2)
        self.hh = self.hs + self.rr
        self.mine = lax.dynamic_index_in_dim(gbuf, c_idx, axis=1, keepdims=False).reshape(4 * self.hh, 128)
        self.other = lax.dynamic_index_in_dim(gbuf, 1 - c_idx, axis=1, keepdims=False).reshape(4 * self.hh, 128)

    def swap(self):
        return _swap_sibling(self.other)

    def swapped(self, got):
        self.chip_sum = _add2(self.mine, got).reshape(4, self.hh, 128)

    def scatter(self):
        return _scatter_chips(self.chip_sum)

    def scattered(self, landed):
        self.piece = _sum4(landed)

    def share(self):
        return _share_reduced(self.piece, self.hs, self.rr)

    def shared(self, full, rall):
        self.full, self.rall = full, rall

    def reduced(self):
        out = {}
        for group, flat in ((self.sharded, self.full[:, :self.hs].reshape(-1)), (self.replicated, self.rall.reshape(-1))):
            off = 0
            for name, n in group:
                out[name] = flat[off:off + n]
                off += n
        return out


def _col_shards(w2d):
    rows, cols = w2d.shape
    return w2d.reshape(rows, 4, cols // 4).transpose(1, 0, 2).reshape(4, rows * (cols // 4))


def _from_col_shards(flat, rows):
    w = flat.shape[1] // rows
    return flat.reshape(4, rows, w).transpose(1, 0, 2).reshape(rows, 4 * w)


def _block_diag4(w):
    eye = jnp.eye(4, dtype=w.dtype)
    return jnp.einsum("gaij,ab->gaibj", w.reshape(3, 4, 64, 64), eye).reshape(3, 256, 256)


def _diag_blocks4(w):
    w5 = w.reshape(3, 4, 64, 4, 64)
    return jnp.stack([w5[:, a, :, a, :] for a in range(4)], axis=1).reshape(12, 64, 64)


def kernel(x, mem, positions, mla_w_in, mla_q_norm, mla_w_uq, mla_kv_norm, mla_w_ukv, lru_w_in, lru_conv_w, lru_conv_b, lru_w_rgate, lru_b_rgate, lru_w_igate, lru_b_igate, lru_lambda, w_mem_kv, w_out, ln_g, ln_b, loss_target, m_mla_w_in, m_mla_q_norm, m_mla_w_uq, m_mla_kv_norm, m_mla_w_ukv, m_lru_w_in, m_lru_conv_w, m_lru_conv_b, m_lru_w_rgate, m_lru_b_rgate, m_lru_w_igate, m_lru_b_igate, m_lru_lambda, m_w_mem_kv, m_w_out, m_ln_g, m_ln_b, v_mla_w_in, v_mla_q_norm, v_mla_w_uq, v_mla_kv_norm, v_mla_w_ukv, v_lru_w_in, v_lru_conv_w, v_lru_conv_b, v_lru_w_rgate, v_lru_b_rgate, v_lru_w_igate, v_lru_b_igate, v_lru_lambda, v_w_mem_kv, v_w_out, v_ln_g, v_ln_b):
    s = x.shape[1]
    c_idx = lax.axis_index("c")
    x2, mem2, tgt2 = x[0], mem[0], loss_target[0]

    early = [mla_w_in[0], mla_w_uq[0], mla_w_ukv[0], w_mem_kv, w_out[0]]
    flat, early_sizes = _pack_bf16([p.reshape(-1).astype(BF16) for p in early])
    seg = _segments(_run_exchange(_gather_shards(flat), "gather_weights")[0], early_sizes)
    win0 = _from_col_shards(seg[0], 1024)
    wuq = _from_col_shards(seg[1], Q_LORA)
    wukv = _from_col_shards(seg[2], KV_LORA)
    wmem = seg[3].reshape(4, 2, 256, 512).transpose(1, 0, 2, 3).reshape(2, 1024, 512)
    wout0 = seg[4].reshape(1024, 1024)

    small = jnp.concatenate([lru_conv_w[0].reshape(-1), lru_conv_b[0], lru_b_rgate[0], lru_b_igate[0], lru_lambda[0]])
    late = [lru_w_in[0].reshape(-1).astype(BF16), w_out[1].reshape(-1).astype(BF16),
            lax.bitcast_convert_type(small, BF16).reshape(-1)]
    flat_late, late_sizes = _pack_bf16(late)

    def late_weights(landed):
        seg = _segments(landed[0], late_sizes)
        small_all = lax.bitcast_convert_type(seg[2].reshape(4, -1, 2), F32)
        cw = small_all[:, :768].reshape(4, 4, 192).transpose(1, 0, 2).reshape(4, TOK_W)
        cb, br, bi, lam = (small_all[:, 768 + 192 * k:960 + 192 * k].reshape(1, TOK_W) for k in range(4))
        return _from_col_shards(seg[0], 1024), seg[1].reshape(1024, 1024), cw, cb, br, bi, lam

    def reduce_late(g):
        return _GradReduce(
            [("lru_w_in", _col_shards(g["lru_w_in"])), ("lru_conv_w", _col_shards(g["lru_conv_w"])),
             ("lru_conv_b", _col_shards(g["lru_conv_b"])), ("lru_b_rgate", _col_shards(g["lru_b_rgate"])),
             ("lru_b_igate", _col_shards(g["lru_b_igate"])), ("lru_lambda", _col_shards(g["lru_lambda"])),
             ("w_mem_kv1", g["w_mem_kv1"].reshape(4, -1)), ("w_out1", g["w_out1"].reshape(4, -1))],
            [("lru_w_rgate", g["lru_w_rgate"].reshape(-1)), ("lru_w_igate", g["lru_w_igate"].reshape(-1)),
             ("ln_g1", g["ln_g1"].reshape(-1)), ("ln_b1", g["ln_b1"].reshape(-1))], c_idx)

    g0, late_red = _local_step(
        x2, mem2, positions.reshape(s, 1), tgt2, win0, wuq, wukv, wmem, wout0, mla_q_norm, mla_kv_norm,
        lru_w_rgate[0], lru_w_igate[0], ln_g, ln_b, late_weights, _gather_shards(flat_late), reduce_late)

    early_red = _GradReduce(
        [("mla_w_in", _col_shards(g0["mla_w_in"])), ("mla_w_uq", _col_shards(g0["mla_w_uq"])),
         ("mla_w_ukv", _col_shards(g0["mla_w_ukv"])), ("w_mem_kv0", g0["w_mem_kv0"].reshape(4, -1)),
         ("w_out0", g0["w_out0"].reshape(4, -1))],
        [("mla_q_norm", g0["mla_q_norm"].reshape(-1)), ("mla_kv_norm", g0["mla_kv_norm"].reshape(-1)),
         ("ln_g0", g0["ln_g0"].reshape(-1)), ("ln_b0", g0["ln_b0"].reshape(-1)), ("loss", g0["loss"].reshape(-1))],
        c_idx)
    early_red.swapped(*_run_exchange(early_red.swap(), "swap_sibling"))
    early_red.scattered(*_run_exchange(early_red.scatter(), "scatter_chips"))
    early_red.shared(*_run_exchange(early_red.share(), "share_reduced"))
    red = {**late_red.reduced(), **early_red.reduced()}
    red["w_mem_kv"] = jnp.concatenate([red["w_mem_kv0"], red["w_mem_kv1"]])
    red["w_out"] = jnp.concatenate([red["w_out0"], red["w_out1"]])
    red["ln_g"] = jnp.concatenate([red["ln_g0"], red["ln_g1"]])
    red["ln_b"] = jnp.concatenate([red["ln_b0"], red["ln_b1"]])

    weights = dict(mla_w_in=mla_w_in, mla_q_norm=mla_q_norm, mla_w_uq=mla_w_uq, mla_kv_norm=mla_kv_norm,
                   mla_w_ukv=mla_w_ukv, lru_w_in=lru_w_in, lru_conv_w=lru_conv_w, lru_conv_b=lru_conv_b,
                   lru_w_rgate=lru_w_rgate, lru_b_rgate=lru_b_rgate, lru_w_igate=lru_w_igate, lru_b_igate=lru_b_igate,
                   lru_lambda=lru_lambda, w_mem_kv=w_mem_kv, w_out=w_out, ln_g=ln_g, ln_b=ln_b)
    m_in = dict(mla_w_in=m_mla_w_in, mla_q_norm=m_mla_q_norm, mla_w_uq=m_mla_w_uq, mla_kv_norm=m_mla_kv_norm,
                mla_w_ukv=m_mla_w_ukv, lru_w_in=m_lru_w_in, lru_conv_w=m_lru_conv_w, lru_conv_b=m_lru_conv_b,
                lru_w_rgate=m_lru_w_rgate, lru_b_rgate=m_lru_b_rgate, lru_w_igate=m_lru_w_igate,
                lru_b_igate=m_lru_b_igate, lru_lambda=m_lru_lambda, w_mem_kv=m_w_mem_kv, w_out=m_w_out, ln_g=m_ln_g,
                ln_b=m_ln_b)
    v_in = dict(mla_w_in=v_mla_w_in, mla_q_norm=v_mla_q_norm, mla_w_uq=v_mla_w_uq, mla_kv_norm=v_mla_kv_norm,
                mla_w_ukv=v_mla_w_ukv, lru_w_in=v_lru_w_in, lru_conv_w=v_lru_conv_w, lru_conv_b=v_lru_conv_b,
                lru_w_rgate=v_lru_w_rgate, lru_b_rgate=v_lru_b_rgate, lru_w_igate=v_lru_w_igate,
                lru_b_igate=v_lru_b_igate, lru_lambda=v_lru_lambda, w_mem_kv=v_w_mem_kv, w_out=v_w_out, ln_g=v_ln_g,
                ln_b=v_ln_b)
    order = ["mla_w_in", "mla_q_norm", "mla_w_uq", "mla_kv_norm", "mla_w_ukv", "lru_w_in", "lru_conv_w", "lru_conv_b",
             "lru_w_rgate", "lru_b_rgate", "lru_w_igate", "lru_b_igate", "lru_lambda", "w_mem_kv", "w_out", "ln_g",
             "ln_b"]
    grads, deltas, new_m, new_v = {}, {}, {}, {}
    for name in order:
        shape = weights[name].shape
        two_d = (math.prod(shape[:-1]), shape[-1])
        g2 = red[name].reshape(two_d)
        d2, m2, v2 = _adamw(name, weights[name].reshape(two_d), g2, m_in[name].reshape(two_d),
                            v_in[name].reshape(two_d))
        grads[name], deltas[name] = g2.reshape(shape), d2.reshape(shape)
        new_m[name], new_v[name] = m2.reshape(shape), v2.reshape(shape)
    return (red["loss"][0], g0["x"][None], *[grads[n] for n in order], *[deltas[n] for n in order],
            *[new_m[n] for n in order], *[new_v[n] for n in order])


def _local_step(x2, mem2, pos_col, tgt2, win0, wuq, wukv, wmem, wout0, gq, gkv, w_rgate, w_igate, ln_g, ln_b,
                late_weights, gather_late=None, reduce_late=None):
    s = x2.shape[0]
    zpad = jnp.zeros((1024, 64), BF16)
    win0p = jnp.concatenate([win0[:, 672:1696], win0[:, 1696:1952], win0[:, 0:384], win0[:, 384:640],
                             zpad, win0[:, 640:672], zpad[:, :32]], axis=1)
    wuq_p = jnp.pad(wuq.reshape(Q_LORA, 12, 96), ((0, 0), (0, 0), (0, 32))).reshape(Q_LORA, QK_W)
    wukv3 = wukv.reshape(KV_LORA, 12, 128)
    wk_p = jnp.pad(wukv3[:, :, :64], ((0, 0), (0, 0), (0, 64))).reshape(KV_LORA, QK_W)
    wv = wukv3[:, :, 64:].reshape(KV_LORA, TOK_W)
    wr_bd = _block_diag4(w_rgate).astype(BF16)
    wi_bd = _block_diag4(w_igate).astype(BF16)
    half = 16
    inv_freq = ROPE_THETA ** (-jnp.arange(half, dtype=F32) / half)
    inv_lane = jnp.concatenate([jnp.zeros((64,), F32), inv_freq, inv_freq, jnp.zeros((32,), F32)]).reshape(1, HEAD_PAD)

    ctab, satab, sbtab = _rope_tables(pos_col, inv_lane, min(TB_PROJ, s))
    memkv = _mem_kv(mem2, wmem)
    gate0, qm0, cq, ckv, q_p, q_t, k_p, v_b, v_t = _mla_proj_fwd(x2, win0p, gq, gkv, wuq_p, wk_p, wv,
                                                                ctab, satab, sbtab)
    tok0, lse, *landed = _attn_fwd(q_p, k_p, v_t, exchange=gather_late)
    win1, wout1, cw, cb, br, bi, lam = late_weights(landed)
    g0, b0, g1, b1 = ln_g[0:1], ln_b[0:1], ln_g[1:2], ln_b[1:2]
    h1 = _mix_fwd(tok0, gate0, qm0, memkv[0], wout0, x2, g0, b0)
    u1, gate1, qm1, hs1 = _lru_fwd(h1, win1, cw, cb, wr_bd, br, wi_bd, bi, lam)

    dres1, dtok1, dgate1, dqm1, dwout1, dmemkv1, dg1, db1, loss = _mix_bwd(
        hs1, gate1, qm1, memkv[1], wout1, h1, g1, b1, tgt2, True)
    dh1, dwin1, dcw, dcb, dwr_bd, dbr, dwi_bd, dbi, dlam = _lru_bwd(
        dtok1, dgate1, dqm1, dres1, h1, u1, hs1, win1, cw, cb, wr_bd, br, wi_bd, bi, lam)
    late = {"lru_w_in": dwin1, "lru_conv_w": dcw, "lru_conv_b": dcb, "lru_b_rgate": dbr, "lru_b_igate": dbi,
            "lru_lambda": dlam, "w_mem_kv1": _mem_kv_bwd(mem2, dmemkv1), "w_out1": dwout1,
            "lru_w_rgate": _diag_blocks4(dwr_bd), "lru_w_igate": _diag_blocks4(dwi_bd), "ln_g1": dg1, "ln_b1": db1}
    red = reduce_late(late) if reduce_late is not None else None

    dres0, dtok0, dgate0, dqm0, dwout0, dmemkv0, dg0, db0, _, *got = _mix_bwd(
        tok0, gate0, qm0, memkv[0], wout0, x2, g0, b0, dh1, False, exchange=red.swap() if red else None)
    if red:
        red.swapped(*got)
    dob, dobt, stats = _attn_prep(tok0, dtok0, lse)
    dq_p, dk_t, dv_t, *got = _attn_bwd(q_p, q_t, k_p, v_b, dob, dobt, stats,
                                       exchange=red.scatter() if red else None)
    if red:
        red.scattered(*got)
    dx, dwin0p, dwuq_p, dwk_p, dwv, dgq, dgkv, *got = _mla_proj_bwd(
        x2, cq, ckv, dq_p, dk_t, dv_t, dgate0, dqm0, dres0, win0p, gq, gkv, wuq_p, wk_p, wv,
        ctab, satab, sbtab, exchange=red.share() if red else None)
    if red:
        red.shared(*got)

    dwin0 = jnp.concatenate([dwin0p[:, 1280:1664], dwin0p[:, 1664:1920], dwin0p[:, 1984:2016], dwin0p[:, 0:1024],
                             dwin0p[:, 1024:1280]], axis=1)
    dwuq = dwuq_p.reshape(Q_LORA, 12, 128)[:, :, :96].reshape(Q_LORA, 1152)
    dwukv = jnp.concatenate([dwk_p.reshape(KV_LORA, 12, 128)[:, :, :64], dwv.reshape(KV_LORA, 12, 64)],
                            axis=2).reshape(KV_LORA, 1536)
    early = {"x": dx, "loss": loss, "mla_w_in": dwin0, "mla_w_uq": dwuq, "mla_w_ukv": dwukv,
             "w_mem_kv0": _mem_kv_bwd(mem2, dmemkv0), "w_out0": dwout0, "mla_q_norm": dgq, "mla_kv_norm": dgkv,
             "ln_g0": dg0, "ln_b0": db0}
    return early, (red if red else late)
```

```python
import functools
import math

import jax
import jax.numpy as jnp
from jax import lax
from jax.experimental import pallas as pl
from jax.experimental.pallas import tpu as pltpu

F32, BF16 = jnp.float32, jnp.bfloat16
MESH = pl.DeviceIdType.MESH

D_MODEL = 1024
N_TOK_HEADS = 12
TOK_W = 768
MEM_W = 256
MEM_LEN = 256
Q_LORA, KV_LORA = 384, 256
HEAD_PAD = 128
QK_W = N_TOK_HEADS * HEAD_PAD
ATT_SCALE = 1.0 / math.sqrt(96.0)
ATT_SCALE_LOG2 = ATT_SCALE * math.log2(math.e)
ROPE_THETA = 10000.0
LRU_C = 8.0
ALPHA = 4.0 ** 0.25
NORM_EPS = 1e-6
ADAM_LR, ADAM_B1, ADAM_B2, ADAM_EPS, ADAM_WD, ADAM_STEP = 0.001, 0.9, 0.999, 1e-08, 0.01, 10

TB_PROJ = 512
TB_PROJ_BWD = 256
TB_MIX = 256
TB_LRU = 256
TQ_ATT = 512
TQ_ATT_FWD = 1024
TK_ATT = 1024
VMEM_LIMIT = 56 * 1024 * 1024


def _mm(a, b):
    return jnp.dot(a.astype(BF16), b.astype(BF16), preferred_element_type=F32)


def _mm_nt(a, b):
    return lax.dot_general(a.astype(BF16), b.astype(BF16), (((1,), (1,)), ((), ())), preferred_element_type=F32)


def _mm_tn(a, b):
    return lax.dot_general(a.astype(BF16), b.astype(BF16), (((0,), (0,)), ((), ())), preferred_element_type=F32)


def _rows(tb, w):
    return pl.BlockSpec((tb, w), lambda i: (i, 0))


def _const(shape):
    n = len(shape)
    return pl.BlockSpec(shape, lambda i: (0,) * n)


def _params(sem=("arbitrary",)):
    return pltpu.CompilerParams(dimension_semantics=sem, vmem_limit_bytes=VMEM_LIMIT)


def _iota(shape, dim):
    return lax.broadcasted_iota(jnp.int32, shape, dim)


def _rope_tables(pos_col, inv_lane, tb):
    s = pos_col.shape[0]

    def body(pos_ref, inv_ref, c_ref, sa_ref, sb_ref):
        ang = pos_ref[...].astype(F32) * inv_ref[...]
        lane = _iota(ang.shape, 1)
        cs, sn = jnp.cos(ang), jnp.sin(ang)
        c_ref[...] = jnp.where(lane < 64, 1.0, jnp.where(lane < 96, cs, 0.0))
        sa_ref[...] = jnp.where((lane >= 64) & (lane < 80), -sn, 0.0)
        sb_ref[...] = jnp.where((lane >= 80) & (lane < 96), sn, 0.0)

    shp = jax.ShapeDtypeStruct((s, HEAD_PAD), F32)
    return pl.pallas_call(
        body, name="rope_tables", grid=(s // tb,), out_shape=(shp, shp, shp),
        in_specs=[_rows(tb, 1), _const((1, HEAD_PAD))], out_specs=(_rows(tb, HEAD_PAD),) * 3,
        compiler_params=_params(("parallel",)),
    )(pos_col, inv_lane)


def _rope(t, c, sa, sb):
    return t * c + pltpu.roll(t, 112, 1) * sa + pltpu.roll(t, 16, 1) * sb


def _rope_t(d, c, sa, sb):
    return d * c + pltpu.roll(d * sa, 16, 1) + pltpu.roll(d * sb, 112, 1)


def _rms(c, g):
    r = lax.rsqrt(jnp.mean(c * c, axis=-1, keepdims=True) + NORM_EPS)
    xh = c * r
    return xh * g, xh, r


def _mla_proj_fwd(x, win, gq, gkv, wuq, wukv_k, wukv_v, ctab, satab, sbtab):
    s = x.shape[0]
    tb = min(TB_PROJ, s)

    def body(x_ref, win_ref, gq_ref, gkv_ref, wuq_ref, wk_ref, wv_ref, c_ref, sa_ref, sb_ref,
             gate_ref, qm_ref, cq_ref, ckv_ref, q_ref, qt_ref, k_ref, v_ref, vt_ref):
        z = _mm(x_ref[...], win_ref[...])
        gate_ref[...] = z[:, 0:1024]
        qm_ref[...] = z[:, 1024:1280].astype(BF16)
        cq = z[:, 1280:1664]
        ckv = z[:, 1664:1920]
        cq_ref[...] = cq
        ckv_ref[...] = ckv
        c, sa, sb = c_ref[...], sa_ref[...], sb_ref[...]
        nq, _, _ = _rms(cq, gq_ref[...])
        nkv, _, _ = _rms(ckv, gkv_ref[...])
        qf = _mm(nq, wuq_ref[...])
        kf = _mm(nkv, wk_ref[...])
        vf = _mm(nkv, wv_ref[...])
        v_ref[...] = vf.astype(BF16)
        for j in range(N_TOK_HEADS // 2):
            sl = slice(HEAD_PAD * j, HEAD_PAD * (j + 1))
            vt_ref[sl, :] = vf[:, sl].T.astype(BF16)
        kr = _rope(z[:, 1920:2048], c, sa, sb)
        for h in range(N_TOK_HEADS):
            sl = slice(HEAD_PAD * h, HEAD_PAD * (h + 1))
            qh = _rope(qf[:, sl], c, sa, sb) * ATT_SCALE_LOG2
            q_ref[:, sl] = qh.astype(BF16)
            qt_ref[sl, :] = qh.T.astype(BF16)
            k_ref[:, sl] = (kf[:, sl] + kr).astype(BF16)

    outs = (jax.ShapeDtypeStruct((s, 1024), F32), jax.ShapeDtypeStruct((s, MEM_W), BF16),
            jax.ShapeDtypeStruct((s, Q_LORA), F32), jax.ShapeDtypeStruct((s, KV_LORA), F32),
            jax.ShapeDtypeStruct((s, QK_W), BF16), jax.ShapeDtypeStruct((QK_W, s), BF16),
            jax.ShapeDtypeStruct((s, QK_W), BF16),
            jax.ShapeDtypeStruct((s, TOK_W), BF16), jax.ShapeDtypeStruct((TOK_W, s), BF16))

    def cols(w):
        return pl.BlockSpec((w, tb), lambda i: (0, i))

    return pl.pallas_call(
        body, name="mla_proj_fwd", grid=(s // tb,), out_shape=outs,
        in_specs=[_rows(tb, 1024), _const((1024, 2048)), _const((1, Q_LORA)), _const((1, KV_LORA)),
                  _const((Q_LORA, QK_W)), _const((KV_LORA, QK_W)), _const((KV_LORA, TOK_W)),
                  _rows(tb, HEAD_PAD), _rows(tb, HEAD_PAD), _rows(tb, HEAD_PAD)],
        out_specs=(_rows(tb, 1024), _rows(tb, MEM_W), _rows(tb, Q_LORA), _rows(tb, KV_LORA),
                   _rows(tb, QK_W), cols(QK_W), _rows(tb, QK_W), _rows(tb, TOK_W), cols(TOK_W)),
        compiler_params=_params(("parallel",)),
    )(x, win, gq, gkv, wuq, wukv_k, wukv_v, ctab, satab, sbtab)


def _mla_proj_bwd(x, cq, ckv, dq, dkt, dvt, dgate, dqm, dres, win, gq, gkv, wuq, wukv_k, wukv_v, ctab, satab, sbtab,
                  exchange=None):
    s = x.shape[0]
    tb = min(TB_PROJ_BWD, s)

    def body(x_ref, cq_ref, ckv_ref, dq_ref, dkt_ref, dvt_ref, dgate_ref, dqm_ref, dres_ref, win_ref, gq_ref, gkv_ref,
             wuq_ref, wk_ref, wv_ref, c_ref, sa_ref, sb_ref,
             dx_ref, dwin_ref, dwuq_ref, dwk_ref, dwv_ref, dgq_ref, dgkv_ref):
        @pl.when(pl.program_id(0) == 0)
        def _():
            for r in (dwin_ref, dwuq_ref, dwk_ref, dwv_ref, dgq_ref, dgkv_ref):
                r[...] = jnp.zeros_like(r)

        c, sa, sb = c_ref[...], sa_ref[...], sb_ref[...]
        lane = _iota((tb, HEAD_PAD), 1)
        gq, gkv = gq_ref[...], gkv_ref[...]
        nq, xhq, rq = _rms(cq_ref[...], gq)
        nkv, xhk, rk = _rms(ckv_ref[...], gkv)
        dkp = dkt_ref[...].T * math.log(2.0)
        dqs, dkr = [], jnp.zeros((tb, HEAD_PAD), F32)
        for h in range(N_TOK_HEADS):
            sl = slice(HEAD_PAD * h, HEAD_PAD * (h + 1))
            dqs.append(_rope_t(dq_ref[:, sl], c, sa, sb).astype(BF16))
            dkr = dkr + dkp[:, sl]
        dqf = jnp.concatenate(dqs, axis=1)
        dkr = jnp.where((lane >= 64) & (lane < 96), _rope_t(dkr, c, sa, sb), 0.0)
        dvb = dvt_ref[...].T.astype(BF16)
        dkb = dkp.astype(BF16)
        dnq = _mm_nt(dqf, wuq_ref[...])
        dwuq_ref[...] += _mm_tn(nq, dqf)
        dgq_ref[...] += jnp.sum(dnq * xhq, axis=0, keepdims=True)
        dxh = dnq * gq
        dcq = rq * (dxh - xhq * jnp.mean(dxh * xhq, axis=-1, keepdims=True))
        dnkv = _mm_nt(dkb, wk_ref[...]) + _mm_nt(dvb, wv_ref[...])
        nkvb = nkv.astype(BF16)
        dwk_ref[...] += _mm_tn(nkvb, dkb)
        dwv_ref[...] += _mm_tn(nkvb, dvb)
        dgkv_ref[...] += jnp.sum(dnkv * xhk, axis=0, keepdims=True)
        dxh = dnkv * gkv
        dckv = rk * (dxh - xhk * jnp.mean(dxh * xhk, axis=-1, keepdims=True))
        dz = jnp.concatenate([dgate_ref[...], dqm_ref[...], dcq, dckv, dkr], axis=1).astype(BF16)
        dx_ref[...] = _mm_nt(dz, win_ref[...]) + dres_ref[...]
        dwin_ref[...] += _mm_tn(x_ref[...], dz)

    outs = (jax.ShapeDtypeStruct((s, 1024), F32), jax.ShapeDtypeStruct((1024, 2048), F32),
            jax.ShapeDtypeStruct((Q_LORA, QK_W), F32), jax.ShapeDtypeStruct((KV_LORA, QK_W), F32),
            jax.ShapeDtypeStruct((KV_LORA, TOK_W), F32), jax.ShapeDtypeStruct((1, Q_LORA), F32),
            jax.ShapeDtypeStruct((1, KV_LORA), F32))
    return _run(
        body, name="mla_proj_bwd", grid=(s // tb,), out_shape=outs,
        in_specs=[_rows(tb, 1024), _rows(tb, Q_LORA), _rows(tb, KV_LORA), _rows(tb, QK_W),
                  pl.BlockSpec((QK_W, tb), lambda i: (0, i)), pl.BlockSpec((TOK_W, tb), lambda i: (0, i)),
                  _rows(tb, 1024), _rows(tb, MEM_W), _rows(tb, 1024),
                  _const((1024, 2048)), _const((1, Q_LORA)), _const((1, KV_LORA)),
                  _const((Q_LORA, QK_W)), _const((KV_LORA, QK_W)), _const((KV_LORA, TOK_W)),
                  _rows(tb, HEAD_PAD), _rows(tb, HEAD_PAD), _rows(tb, HEAD_PAD)],
        out_specs=(_rows(tb, 1024), _const((1024, 2048)), _const((Q_LORA, QK_W)), _const((KV_LORA, QK_W)),
                   _const((KV_LORA, TOK_W)), _const((1, Q_LORA)), _const((1, KV_LORA))),
        args=(x, cq, ckv, dq, dkt, dvt, dgate, dqm, dres, win, gq, gkv, wuq, wukv_k, wukv_v, ctab, satab, sbtab),
        sem=("arbitrary",), exchange=exchange)


def _attn_fwd(q, k, vt, exchange=None):
    s = q.shape[0]
    tq = min(TQ_ATT_FWD, s)
    tk = min(TK_ATT, s)

    def body(q_ref, k_ref, vt_ref, o_ref, lse_ref):
        i = pl.program_id(1)
        nfull = (i * tq) // tk
        krow = _iota((tk, tq), 0)
        qpos = i * tq + _iota((tk, tq), 1)

        def head_tile(hh, st, carry, masked):
            hs = slice(HEAD_PAD * hh, HEAD_PAD * (hh + 1))
            m, l, acc = carry
            sc = _mm_nt(k_ref[pl.ds(st, tk), hs], q_ref[:, hs])
            if masked:
                sc = jnp.where(st + krow <= qpos, sc, -jnp.inf)
            m_new = jnp.maximum(m, jnp.max(sc, axis=0, keepdims=True))
            p = jnp.exp2(sc - m_new)
            a = jnp.exp2(m - m_new)
            l = a * l + jnp.sum(p, axis=0, keepdims=True)
            acc = a * acc + _mm(vt_ref[64 * hh:64 * (hh + 1), pl.ds(st, tk)], p)
            return m_new, l, acc

        def tile(j, carry, masked):
            st = pl.multiple_of(j * tk, tk)
            return tuple(head_tile(hh, st, carry[hh], masked) for hh in range(2))

        def init():
            return (jnp.full((1, tq), -jnp.inf, F32), jnp.zeros((1, tq), F32), jnp.zeros((64, tq), F32))

        carry = lax.fori_loop(0, nfull, functools.partial(tile, masked=False), (init(), init()))
        (ma, la, acca), (mb, lb, accb) = tile(nfull, carry, True)
        o_ref[...] = jnp.concatenate([acca / la, accb / lb], axis=0).T
        lse_ref[...] = jnp.concatenate([jnp.broadcast_to(ma + jnp.log2(la), (64, tq)),
                                        jnp.broadcast_to(mb + jnp.log2(lb), (64, tq))], axis=0).T

    shp = jax.ShapeDtypeStruct((s, TOK_W), F32)
    return _run(
        body, name="attn_fwd", grid=(N_TOK_HEADS // 2, s // tq), out_shape=(shp, shp),
        in_specs=[pl.BlockSpec((tq, 2 * HEAD_PAD), lambda j, i: (i, j)),
                  pl.BlockSpec((s, 2 * HEAD_PAD), lambda j, i: (0, j)),
                  pl.BlockSpec((HEAD_PAD, s), lambda j, i: (j, 0))],
        out_specs=(pl.BlockSpec((tq, HEAD_PAD), lambda j, i: (i, j)),) * 2,
        args=(q, k, vt), sem=("parallel", "arbitrary"), exchange=exchange)


def _attn_prep(o, do, lse):
    s = o.shape[0]
    tb = min(TB_PROJ, s)
    npair = N_TOK_HEADS // 2

    def body(o_ref, do_ref, lse_ref, dob_ref, dot_ref, st_ref):
        lane = _iota((tb, HEAD_PAD), 1)
        do = do_ref[...]
        dob_ref[...] = do.astype(BF16)
        prod = do * o_ref[...]
        for j in range(npair):
            sl = slice(HEAD_PAD * j, HEAD_PAD * (j + 1))
            dot_ref[sl, :] = do[:, sl].T.astype(BF16)
            pj = prod[:, sl]
            da = jnp.sum(jnp.where(lane < 64, pj, 0.0), axis=-1, keepdims=True)
            db = jnp.sum(jnp.where(lane >= 64, pj, 0.0), axis=-1, keepdims=True)
            la = lse_ref[:, HEAD_PAD * j:HEAD_PAD * j + 1]
            lb = lse_ref[:, HEAD_PAD * j + 64:HEAD_PAD * j + 65]
            st_ref[j] = jnp.where(lane == 0, la, jnp.where(lane == 1, lb, jnp.where(lane == 2, da,
                                                                                     jnp.where(lane == 3, db, 0.0))))

    return pl.pallas_call(
        body, name="attn_prep", grid=(s // tb,),
        out_shape=(jax.ShapeDtypeStruct((s, TOK_W), BF16), jax.ShapeDtypeStruct((TOK_W, s), BF16),
                   jax.ShapeDtypeStruct((npair, s, HEAD_PAD), F32)),
        in_specs=[_rows(tb, TOK_W)] * 3,
        out_specs=(_rows(tb, TOK_W), pl.BlockSpec((TOK_W, tb), lambda i: (0, i)),
                   pl.BlockSpec((npair, tb, HEAD_PAD), lambda i: (0, i, 0))),
        compiler_params=_params(("parallel",)),
    )(o, do, lse)


def _attn_bwd(q, qt, k, v, dob, dobt, stats, exchange=None):
    s = q.shape[0]
    t = min(TQ_ATT, s)
    nq = s // t

    def body(q_ref, qt_ref, do_ref, dot_ref, st_ref, k_ref, v_ref, dq_ref, dkt_ref, dvt_ref):
        i = pl.program_id(1)

        @pl.when(i == 0)
        def _():
            dkt_ref[...] = jnp.zeros_like(dkt_ref)
            dvt_ref[...] = jnp.zeros_like(dvt_ref)

        lane = _iota((t, HEAD_PAD), 1)
        qpos, kcol = _iota((t, t), 0), _iota((t, t), 1)
        do = do_ref[...]
        stats = st_ref[0]

        def head_tile(hh, ks, dq_acc, masked):
            hs = slice(HEAD_PAD * hh, HEAD_PAD * (hh + 1))
            qh = q_ref[:, hs]
            kh = k_ref[pl.ds(ks, t), hs]
            dom = jnp.where((lane < 64) if hh == 0 else (lane >= 64), do, jnp.zeros_like(do))
            lse = stats[:, hh:hh + 1]
            dlt = stats[:, 2 + hh:3 + hh]
            sc = _mm_nt(qh, kh)
            if masked:
                sc = jnp.where(kcol <= qpos, sc, -jnp.inf)
            p = jnp.exp2(sc - lse)
            dp = _mm_nt(dom, v_ref[pl.ds(ks, t), :])
            ds = (p * (dp - dlt)).astype(BF16)
            dvt_ref[64 * hh:64 * (hh + 1), pl.ds(ks, t)] += _mm(dot_ref[64 * hh:64 * (hh + 1), :], p)
            dkt_ref[HEAD_PAD * hh:HEAD_PAD * hh + 96, pl.ds(ks, t)] += _mm(qt_ref[HEAD_PAD * hh:HEAD_PAD * hh + 96, :], ds)
            return dq_acc + _mm(ds, kh)

        def tile(j, carry, masked):
            ks = pl.multiple_of(j * t, t)
            return tuple(head_tile(hh, ks, carry[hh], masked) for hh in range(2))

        zero = jnp.zeros((t, HEAD_PAD), F32)
        carry = lax.fori_loop(0, i, functools.partial(tile, masked=False), (zero, zero))
        dqa, dqb = tile(i, carry, True)
        dq_ref[...] = jnp.concatenate([dqa, dqb], axis=1) * ATT_SCALE

    return _run(
        body, name="attn_bwd", grid=(N_TOK_HEADS // 2, nq),
        out_shape=(jax.ShapeDtypeStruct((s, QK_W), F32), jax.ShapeDtypeStruct((QK_W, s), F32),
                   jax.ShapeDtypeStruct((TOK_W, s), F32)),
        in_specs=[pl.BlockSpec((t, 2 * HEAD_PAD), lambda j, i: (i, j)),
                  pl.BlockSpec((2 * HEAD_PAD, t), lambda j, i: (j, i)),
                  pl.BlockSpec((t, HEAD_PAD), lambda j, i: (i, j)),
                  pl.BlockSpec((HEAD_PAD, t), lambda j, i: (j, i)),
                  pl.BlockSpec((1, t, HEAD_PAD), lambda j, i: (j, i, 0)),
                  pl.BlockSpec((s, 2 * HEAD_PAD), lambda j, i: (0, j)),
                  pl.BlockSpec((s, HEAD_PAD), lambda j, i: (0, j))],
        out_specs=(pl.BlockSpec((t, 2 * HEAD_PAD), lambda j, i: (i, j)),
                   pl.BlockSpec((2 * HEAD_PAD, s), lambda j, i: (j, 0)),
                   pl.BlockSpec((HEAD_PAD, s), lambda j, i: (j, 0))),
        args=(q, qt, dob, dobt, stats, k, v), sem=("parallel", "arbitrary"), exchange=exchange)


def _mem_kv(mem, wmem):
    def body(m_ref, w_ref, o_ref):
        o_ref[0] = _mm(m_ref[...], w_ref[0]).astype(BF16)

    return pl.pallas_call(
        body, name="mem_kv", grid=(2,), out_shape=jax.ShapeDtypeStruct((2, MEM_LEN, 512), BF16),
        in_specs=[_const((MEM_LEN, 1024)), pl.BlockSpec((1, 1024, 512), lambda l: (l, 0, 0))],
        out_specs=pl.BlockSpec((1, MEM_LEN, 512), lambda l: (l, 0, 0)),
        compiler_params=_params(("parallel",)),
    )(mem, wmem)


def _mem_kv_bwd(mem, dmemkv):
    def body(m_ref, d_ref, o_ref):
        o_ref[...] = _mm_tn(m_ref[...], d_ref[...])

    return pl.pallas_call(
        body, name="mem_kv_bwd", grid=(1,), out_shape=jax.ShapeDtypeStruct((1024, 512), F32),
        in_specs=[_const((MEM_LEN, 1024)), _const((MEM_LEN, 512))], out_specs=_const((1024, 512)),
        compiler_params=_params(("arbitrary",)),
    )(mem, dmemkv)


def _head_mask(lane, sub):
    return (lane < 64) if sub == 0 else (lane >= 64)


def _mem_attn(qm, kv):
    tb = qm.shape[0]
    lane = _iota((tb, HEAD_PAD), 1)
    outs, ps = [], []
    for pp in range(2):
        qp = qm[:, HEAD_PAD * pp:HEAD_PAD * (pp + 1)]
        kp = kv[:, HEAD_PAD * pp:HEAD_PAD * (pp + 1)]
        vp = kv[:, MEM_W + HEAD_PAD * pp:MEM_W + HEAD_PAD * (pp + 1)]
        pair = None
        for sub in range(2):
            qh = jnp.where(_head_mask(lane, sub), qp, jnp.zeros_like(qp))
            sc = _mm_nt(qh, kp) * 0.125
            e = jnp.exp(sc - jnp.max(sc, axis=-1, keepdims=True))
            p = e / jnp.sum(e, axis=-1, keepdims=True)
            o = _mm(p, vp)
            ps.append(p)
            pair = o if sub == 0 else jnp.where(lane < 64, pair, o)
        outs.append(pair)
    return jnp.concatenate(outs, axis=1), ps


def _mem_attn_bwd(dmo, qm, kv, ps):
    tb = qm.shape[0]
    lane = _iota((tb, HEAD_PAD), 1)
    dqs, dks, dvs = [], [], []
    for pp in range(2):
        qp = qm[:, HEAD_PAD * pp:HEAD_PAD * (pp + 1)]
        kp = kv[:, HEAD_PAD * pp:HEAD_PAD * (pp + 1)]
        vp = kv[:, MEM_W + HEAD_PAD * pp:MEM_W + HEAD_PAD * (pp + 1)]
        dop = dmo[:, HEAD_PAD * pp:HEAD_PAD * (pp + 1)]
        dq_pair, dk_pair, dv_pair = None, None, None
        for sub in range(2):
            msk = _head_mask(lane, sub)
            p = ps[2 * pp + sub]
            qh = jnp.where(msk, qp, jnp.zeros_like(qp))
            doh = jnp.where(msk, dop, 0.0).astype(BF16)
            dv = _mm_tn(p, doh)
            dp = _mm_nt(doh, vp)
            ds = (p * (dp - jnp.sum(dp * p, axis=-1, keepdims=True)) * 0.125).astype(BF16)
            dq = _mm(ds, kp)
            dk = _mm_tn(ds, qh)
            if sub == 0:
                dq_pair, dk_pair, dv_pair = dq, dk, dv
            else:
                dq_pair = jnp.where(lane < 64, dq_pair, dq)
                dk_pair, dv_pair = dk_pair + dk, dv_pair + dv
        dqs.append(dq_pair)
        dks.append(dk_pair)
        dvs.append(dv_pair)
    return jnp.concatenate(dqs, axis=1), jnp.concatenate(dks + dvs, axis=1)


def _mix_core(tok, gate, qm, kv, wout, h_in, g, b):
    mem_out, ps = _mem_attn(qm, kv)
    cat = jnp.concatenate([tok, mem_out], axis=1)
    sg = jax.nn.sigmoid(gate)
    sl = gate * sg
    y = cat * sl
    r = ALPHA * h_in + _mm(y, wout)
    mu = jnp.mean(r, axis=-1, keepdims=True)
    xc = r - mu
    rstd = lax.rsqrt(jnp.mean(xc * xc, axis=-1, keepdims=True) + NORM_EPS)
    xh = xc * rstd
    return xh * g + b, (ps, cat, sg, sl, y, xh, rstd)


def _mix_fwd(tok, gate, qm, kv, wout, h_in, g, b):
    s = tok.shape[0]
    tb = min(TB_MIX, s)

    def body(tok_ref, gate_ref, qm_ref, kv_ref, w_ref, h_ref, g_ref, b_ref, o_ref):
        o_ref[...], _ = _mix_core(tok_ref[...], gate_ref[...], qm_ref[...], kv_ref[...], w_ref[...], h_ref[...],
                                  g_ref[...], b_ref[...])

    return pl.pallas_call(
        body, name="mix_fwd", grid=(s // tb,), out_shape=jax.ShapeDtypeStruct((s, 1024), F32),
        in_specs=[_rows(tb, TOK_W), _rows(tb, 1024), _rows(tb, MEM_W), _const((MEM_LEN, 512)), _const((1024, 1024)),
                  _rows(tb, 1024), _const((1, 1024)), _const((1, 1024))],
        out_specs=_rows(tb, 1024), compiler_params=_params(("parallel",)),
    )(tok, gate, qm, kv, wout, h_in, g, b)


def _mix_bwd(tok, gate, qm, kv, wout, h_in, g, b, up, from_loss, exchange=None):
    s = tok.shape[0]
    tb = min(TB_MIX, s)

    def body(tok_ref, gate_ref, qm_ref, kv_ref, w_ref, h_ref, g_ref, b_ref, up_ref,
             dres_ref, dtok_ref, dgate_ref, dqm_ref, dw_ref, dkv_ref, dg_ref, db_ref, loss_ref):
        @pl.when(pl.program_id(0) == 0)
        def _():
            for r in (dw_ref, dkv_ref, dg_ref, db_ref, loss_ref):
                r[...] = jnp.zeros_like(r)

        gate, qm, kv, wout, g = gate_ref[...], qm_ref[...], kv_ref[...], w_ref[...], g_ref[...]
        h_out, (ps, cat, sg, sl, y, xh, rstd) = _mix_core(tok_ref[...], gate, qm, kv, wout, h_ref[...], g, b_ref[...])
        if from_loss:
            diff = h_out - up_ref[...]
            loss_ref[...] += 0.5 * jnp.sum(jnp.mean(diff * diff, axis=-1, keepdims=True), axis=0, keepdims=True)
            dh = diff * (1.0 / D_MODEL)
        else:
            dh = up_ref[...]
        dg_ref[...] += jnp.sum(dh * xh, axis=0, keepdims=True)
        db_ref[...] += jnp.sum(dh, axis=0, keepdims=True)
        dxh = dh * g
        dr = rstd * (dxh - jnp.mean(dxh, axis=-1, keepdims=True) - xh * jnp.mean(dxh * xh, axis=-1, keepdims=True))
        dres_ref[...] = ALPHA * dr
        drb = dr.astype(BF16)
        dy = _mm_nt(drb, wout)
        dw_ref[...] += _mm_tn(y, drb)
        dcat = dy * sl
        dgate_ref[...] = dy * cat * (sg * (1.0 + gate * (1.0 - sg)))
        dtok_ref[...] = dcat[:, :TOK_W]
        dqm, dkv = _mem_attn_bwd(dcat[:, TOK_W:], qm, kv, ps)
        dqm_ref[...] = dqm
        dkv_ref[...] += dkv

    outs = (jax.ShapeDtypeStruct((s, 1024), F32), jax.ShapeDtypeStruct((s, TOK_W), F32),
            jax.ShapeDtypeStruct((s, 1024), F32), jax.ShapeDtypeStruct((s, MEM_W), F32),
            jax.ShapeDtypeStruct((1024, 1024), F32), jax.ShapeDtypeStruct((MEM_LEN, 512), F32),
            jax.ShapeDtypeStruct((1, 1024), F32), jax.ShapeDtypeStruct((1, 1024), F32),
            jax.ShapeDtypeStruct((1, 1), F32))
    return _run(
        body, name="mix_bwd_loss" if from_loss else "mix_bwd", grid=(s // tb,), out_shape=outs,
        in_specs=[_rows(tb, TOK_W), _rows(tb, 1024), _rows(tb, MEM_W), _const((MEM_LEN, 512)), _const((1024, 1024)),
                  _rows(tb, 1024), _const((1, 1024)), _const((1, 1024)), _rows(tb, 1024)],
        out_specs=(_rows(tb, 1024), _rows(tb, TOK_W), _rows(tb, 1024), _rows(tb, MEM_W), _const((1024, 1024)),
                   _const((MEM_LEN, 512)), _const((1, 1024)), _const((1, 1024)), _const((1, 1))),
        args=(tok, gate, qm, kv, wout, h_in, g, b, up), sem=("arbitrary",), exchange=exchange)


def _shift_down(u, tail, k):
    if k == 0:
        return u
    r = pltpu.roll(u, k, 0)
    row8 = _iota((8, u.shape[1]), 0)
    head = jnp.where(row8 < k, pltpu.roll(tail, k, 0), r[:8])
    return jnp.concatenate([head, r[8:]], axis=0)


def _shift_up(d, head, k):
    if k == 0:
        return d
    n = d.shape[0]
    r = pltpu.roll(d, n - k, 0)
    row8 = _iota((8, d.shape[1]), 0)
    last = jnp.where(row8 >= 8 - k, pltpu.roll(head, 8 - k, 0), r[n - 8:])
    return jnp.concatenate([r[:n - 8], last], axis=0)


def _scan_down(a, b):
    n = a.shape[0]
    row = _iota(a.shape, 0)
    s = 1
    while s < n:
        ok = row >= s
        a_s = jnp.where(ok, pltpu.roll(a, s, 0), 1.0)
        b_s = jnp.where(ok, pltpu.roll(b, s, 0), 0.0)
        b = a * b_s + b
        a = a * a_s
        s *= 2
    return a, b


def _scan_up(a, b):
    n = a.shape[0]
    row = _iota(a.shape, 0)
    s = 1
    while s < n:
        ok = row < n - s
        a_s = jnp.where(ok, pltpu.roll(a, n - s, 0), 1.0)
        b_s = jnp.where(ok, pltpu.roll(b, n - s, 0), 0.0)
        b = a * b_s + b
        a = a * a_s
        s *= 2
    return a, b


def _neg_expm1(x):
    poly = -x * (1.0 + x * (0.5 + x * (1.0 / 6.0 + x * (1.0 / 24.0 + x * (1.0 / 120.0)))))
    return jnp.where(x > -0.1, poly, 1.0 - jnp.exp(x))


def _softplus(x):
    return jnp.maximum(x, 0.0) + jnp.log(1.0 + jnp.exp(-jnp.abs(x)))


def _lru_gates(u, tail, cw, cb, wr, br, wi, bi, lam):
    us = [_shift_down(u, tail, k) for k in range(4)]
    xc = cb + us[3] * cw[0:1] + us[2] * cw[1:2] + us[1] * cw[2:3] + us[0] * cw[3:4]
    xb = xc.astype(BF16)
    pre_r = jnp.concatenate([_mm(xb[:, 256 * g:256 * (g + 1)], wr[g]) for g in range(3)], axis=1) + br
    pre_i = jnp.concatenate([_mm(xb[:, 256 * g:256 * (g + 1)], wi[g]) for g in range(3)], axis=1) + bi
    rg, ig = jax.nn.sigmoid(pre_r), jax.nn.sigmoid(pre_i)
    clam = -LRU_C * _softplus(-lam)
    la = clam * rg
    a = jnp.exp(la)
    mm = jnp.sqrt(_neg_expm1(2.0 * la))
    return us, xc, xb, rg, ig, clam, la, a, mm


def _lru_fwd(h, win, cw, cb, wr, br, wi, bi, lam):
    s = h.shape[0]
    tb = min(TB_LRU, s)

    def body(h_ref, win_ref, cw_ref, cb_ref, wr_ref, br_ref, wi_ref, bi_ref, lam_ref,
             u_ref, gate_ref, qm_ref, hs_ref, tail_sc, carry_sc):
        @pl.when(pl.program_id(0) == 0)
        def _():
            tail_sc[...] = jnp.zeros_like(tail_sc)
            carry_sc[...] = jnp.zeros_like(carry_sc)

        z = _mm(h_ref[...], win_ref[...])
        u = z[:, :TOK_W]
        u_ref[...] = u
        gate_ref[...] = z[:, TOK_W:TOK_W + 1024]
        qm_ref[...] = z[:, TOK_W + 1024:].astype(BF16)
        _, xc, _, _, ig, _, _, a, mm = _lru_gates(u, tail_sc[...], cw_ref[...], cb_ref[...], wr_ref[...], br_ref[...],
                                                 wi_ref[...], bi_ref[...], lam_ref[...])
        big_a, big_b = _scan_down(a, mm * (ig * xc))
        hs = big_a * carry_sc[0:1, :] + big_b
        hs_ref[...] = hs
        tail_sc[...] = u[tb - 8:, :]
        carry_sc[...] = jnp.broadcast_to(hs[tb - 1:tb, :], carry_sc.shape)

    outs = (jax.ShapeDtypeStruct((s, TOK_W), F32), jax.ShapeDtypeStruct((s, 1024), F32),
            jax.ShapeDtypeStruct((s, MEM_W), BF16), jax.ShapeDtypeStruct((s, TOK_W), F32))
    return pl.pallas_call(
        body, name="lru_fwd", grid=(s // tb,), out_shape=outs,
        in_specs=[_rows(tb, 1024), _const((1024, 2048)), _const((4, TOK_W)), _const((1, TOK_W)),
                  _const((3, 256, 256)), _const((1, TOK_W)), _const((3, 256, 256)), _const((1, TOK_W)),
                  _const((1, TOK_W))],
        out_specs=(_rows(tb, TOK_W), _rows(tb, 1024), _rows(tb, MEM_W), _rows(tb, TOK_W)),
        scratch_shapes=[pltpu.VMEM((8, TOK_W), F32), pltpu.VMEM((8, TOK_W), F32)],
        compiler_params=_params(),
    )(h, win, cw, cb, wr, br, wi, bi, lam)


def _lru_bwd(dhs, dgate, dqm, dres, h, u, hs, win, cw, cb, wr, br, wi, bi, lam):
    s = h.shape[0]
    tb = min(TB_LRU, s)
    nb = s // tb

    def rev(w):
        return pl.BlockSpec((tb, w), lambda i: (nb - 1 - i, 0))

    def prev_tail(w):
        return pl.BlockSpec((8, w), lambda i: (jnp.maximum((nb - 1 - i) * (tb // 8) - 1, 0), 0))

    def body(dhs_ref, dgate_ref, dqm_ref, dres_ref, h_ref, u_ref, hs_ref, ut_ref, hst_ref, win_ref, cw_ref, cb_ref,
             wr_ref, br_ref, wi_ref, bi_ref, lam_ref,
             dh_ref, dwin_ref, dcw_ref, dcb_ref, dwr_ref, dbr_ref, dwi_ref, dbi_ref, dlam_ref, ecar_sc, dxc_sc):
        i = pl.program_id(0)

        @pl.when(i == 0)
        def _():
            for r in (dwin_ref, dcw_ref, dcb_ref, dwr_ref, dbr_ref, dwi_ref, dbi_ref, dlam_ref, ecar_sc, dxc_sc):
                r[...] = jnp.zeros_like(r)

        first = (i == nb - 1)
        u = u_ref[...]
        utail = jnp.where(first, 0.0, ut_ref[...])
        hstail = jnp.where(first, 0.0, hst_ref[...])
        cw, wr, wi, lam = cw_ref[...], wr_ref[...], wi_ref[...], lam_ref[...]
        us, xc, xb, rg, ig, clam, la, a, mm = _lru_gates(u, utail, cw, cb_ref[...], wr, br_ref[...], wi, bi_ref[...], lam)
        row = _iota(a.shape, 0)
        a_next = jnp.where(row < tb - 1, pltpu.roll(a, tb - 1, 0), 1.0)
        big_a, big_b = _scan_up(a_next, dhs_ref[...])
        e = big_a * ecar_sc[0:1, :] + big_b
        ecar_sc[...] = jnp.broadcast_to(a[0:1, :] * e[0:1, :], ecar_sc.shape)
        hs_prev = _shift_down(hs_ref[...], hstail, 1)
        da = e * hs_prev
        ix = ig * xc
        dmm = e * ix
        dix = e * mm
        dla = da * a - dmm * (a * a) / mm
        dlam_ref[...] += jnp.sum(dla * rg, axis=0, keepdims=True)
        dpr = (dla * clam) * rg * (1.0 - rg)
        dpi = (dix * xc) * ig * (1.0 - ig)
        dbr_ref[...] += jnp.sum(dpr, axis=0, keepdims=True)
        dbi_ref[...] += jnp.sum(dpi, axis=0, keepdims=True)
        dprb, dpib = dpr.astype(BF16), dpi.astype(BF16)
        dxc_g = []
        for g in range(3):
            sl = slice(256 * g, 256 * (g + 1))
            dwr_ref[g] += _mm_tn(xb[:, sl], dprb[:, sl])
            dwi_ref[g] += _mm_tn(xb[:, sl], dpib[:, sl])
            dxc_g.append(_mm_nt(dprb[:, sl], wr[g]) + _mm_nt(dpib[:, sl], wi[g]))
        dxc = dix * ig + jnp.concatenate(dxc_g, axis=1)
        dcb_ref[...] += jnp.sum(dxc, axis=0, keepdims=True)
        dcw_ref[...] += jnp.concatenate([jnp.sum(dxc * us[3 - tap], axis=0, keepdims=True) for tap in range(4)], axis=0)
        head = dxc_sc[...]
        du = dxc * cw[3:4]
        for k in range(1, 4):
            du = du + _shift_up(dxc, head, k) * cw[3 - k:4 - k]
        dxc_sc[...] = dxc[:8, :]
        dz = jnp.concatenate([du, dgate_ref[...], dqm_ref[...]], axis=1).astype(BF16)
        dh_ref[...] = _mm_nt(dz, win_ref[...]) + dres_ref[...]
        dwin_ref[...] += _mm_tn(h_ref[...], dz)

        @pl.when(i == nb - 1)
        def _():
            dlam_ref[...] = dlam_ref[...] * (LRU_C * jax.nn.sigmoid(-lam))

    outs = (jax.ShapeDtypeStruct((s, 1024), F32), jax.ShapeDtypeStruct((1024, 2048), F32),
            jax.ShapeDtypeStruct((4, TOK_W), F32), jax.ShapeDtypeStruct((1, TOK_W), F32),
            jax.ShapeDtypeStruct((3, 256, 256), F32), jax.ShapeDtypeStruct((1, TOK_W), F32),
            jax.ShapeDtypeStruct((3, 256, 256), F32), jax.ShapeDtypeStruct((1, TOK_W), F32),
            jax.ShapeDtypeStruct((1, TOK_W), F32))
    return pl.pallas_call(
        body, name="lru_bwd", grid=(nb,), out_shape=outs,
        in_specs=[rev(TOK_W), rev(1024), rev(MEM_W), rev(1024), rev(1024), rev(TOK_W), rev(TOK_W),
                  prev_tail(TOK_W), prev_tail(TOK_W),
                  _const((1024, 2048)), _const((4, TOK_W)), _const((1, TOK_W)), _const((3, 256, 256)),
                  _const((1, TOK_W)), _const((3, 256, 256)), _const((1, TOK_W)), _const((1, TOK_W))],
        out_specs=(rev(1024), _const((1024, 2048)), _const((4, TOK_W)), _const((1, TOK_W)), _const((3, 256, 256)),
                   _const((1, TOK_W)), _const((3, 256, 256)), _const((1, TOK_W)), _const((1, TOK_W))),
        scratch_shapes=[pltpu.VMEM((8, TOK_W), F32), pltpu.VMEM((8, TOK_W), F32)],
        compiler_params=_params(),
    )(dhs, dgate, dqm, dres, h, u, hs, u, hs, win, cw, cb, wr, br, wi, bi, lam)


def _adamw(name, w, g, m, v):
    rows, cols = w.shape
    tb = 256 if rows % 256 == 0 else rows

    def body(w_ref, g_ref, m_ref, v_ref, d_ref, nm_ref, nv_ref):
        g = g_ref[...]
        nm = ADAM_B1 * m_ref[...] + (1.0 - ADAM_B1) * g
        nv = ADAM_B2 * v_ref[...] + (1.0 - ADAM_B2) * (g * g)
        m_hat = nm / (1.0 - ADAM_B1 ** ADAM_STEP)
        v_hat = nv / (1.0 - ADAM_B2 ** ADAM_STEP)
        d_ref[...] = -ADAM_LR * (m_hat / (jnp.sqrt(v_hat) + ADAM_EPS) + ADAM_WD * w_ref[...])
        nm_ref[...] = nm
        nv_ref[...] = nv

    shp = jax.ShapeDtypeStruct((rows, cols), F32)
    return pl.pallas_call(
        body, name="adamw_" + name, grid=(rows // tb,), out_shape=(shp, shp, shp),
        in_specs=[_rows(tb, cols)] * 4, out_specs=(_rows(tb, cols),) * 3,
        compiler_params=_params(("parallel",)),
    )(w, g, m, v)


def _row_block(rows, cap=2048):
    return max(t for t in range(8, cap + 1, 8) if rows % t == 0)


def _add2(a, b):
    rows = a.shape[0]
    tb = _row_block(rows)

    def body(a_ref, b_ref, o_ref):
        o_ref[...] = a_ref[...] + b_ref[...]

    return pl.pallas_call(
        body, name="add_sibling", grid=(rows // tb,), out_shape=jax.ShapeDtypeStruct(a.shape, F32),
        in_specs=[_rows(tb, 128)] * 2, out_specs=_rows(tb, 128), compiler_params=_params(("parallel",)),
    )(a, b)


def _sum_slots(landed, own, rows):
    tb = _row_block(rows, 1024)

    def body(l_ref, o_ref, out_ref):
        t = 2 * lax.axis_index("x") + lax.axis_index("y")
        r = [jnp.where(t == s, o_ref[s], l_ref[s].astype(F32)) for s in range(4)]
        out_ref[...] = ((r[0] + r[1]) + r[2]) + r[3]

    return pl.pallas_call(
        body, name="sum_chips", grid=(rows // tb,), out_shape=jax.ShapeDtypeStruct((rows, 128), F32),
        in_specs=[pl.BlockSpec((4, tb, 128), lambda i: (0, i, 0))] * 2, out_specs=_rows(tb, 128),
        compiler_params=_params(("parallel",)),
    )(landed, own)


_ANY = pl.BlockSpec(memory_space=pl.ANY)


def _place():
    x, y, c = lax.axis_index("x"), lax.axis_index("y"), lax.axis_index("c")
    return x, y, c, [(1 - x, y), (x, 1 - y), (1 - x, 1 - y)]


def _remote(src, dst, ssem, rsem, to):
    return pltpu.make_async_remote_copy(src_ref=src, dst_ref=dst, send_sem=ssem, recv_sem=rsem, device_id=to,
                                        device_id_type=MESH)


class _Exchange:
    def __init__(self, ins, out_shape, sems, start, finish):
        self.ins, self.out_shape, self.sems, self.start, self.finish = ins, out_shape, sems, start, finish


def _run(body, *, name, grid, in_specs, out_specs, out_shape, args, scratch=(), sem, exchange=None):
    if exchange is None:
        return pl.pallas_call(body, name=name, grid=grid, out_shape=tuple(out_shape), in_specs=list(in_specs),
                              out_specs=tuple(out_specs), scratch_shapes=list(scratch),
                              compiler_params=_params(sem))(*args)
    n_in, n_out, n_sc = len(args), len(out_shape), len(scratch)
    k_in, k_out = len(exchange.ins), len(exchange.out_shape)

    def fused(*refs):
        ins, refs = refs[:n_in], refs[n_in:]
        xin, refs = refs[:k_in], refs[k_in:]
        outs, refs = refs[:n_out], refs[n_out:]
        xout, refs = refs[:k_out], refs[k_out:]
        sc, xsem = refs[:n_sc], refs[n_sc:]
        first = pl.program_id(0) == 0
        last = pl.program_id(0) == grid[0] - 1
        for a in range(1, len(grid)):
            first = first & (pl.program_id(a) == 0)
            last = last & (pl.program_id(a) == grid[a] - 1)

        @pl.when(first)
        def _():
            exchange.start(xin, xout, xsem)

        body(*ins, *outs, *sc)

        @pl.when(last)
        def _():
            exchange.finish(xin, xout, xsem)

    return pl.pallas_call(
        fused, name=name, grid=grid, out_shape=(*out_shape, *exchange.out_shape),
        in_specs=[*in_specs, *[_ANY] * k_in], out_specs=(*out_specs, *[_ANY] * k_out),
        scratch_shapes=[*scratch, *exchange.sems],
        compiler_params=_params(("arbitrary",) * len(grid)),
    )(*args, *exchange.ins)


def _run_exchange(exchange, name):
    def body(*refs):
        k_in, k_out = len(exchange.ins), len(exchange.out_shape)
        xin, xout, xsem = refs[:k_in], refs[k_in:k_in + k_out], refs[k_in + k_out:]
        exchange.start(xin, xout, xsem)
        exchange.finish(xin, xout, xsem)

    return pl.pallas_call(
        body, name=name, out_shape=tuple(exchange.out_shape), in_specs=[_ANY] * len(exchange.ins),
        out_specs=tuple([_ANY] * len(exchange.out_shape)), scratch_shapes=list(exchange.sems),
    )(*exchange.ins)


def _gather_shards(wsh):
    _, hh, _ = wsh.shape

    def first_hop(w_ref, out_ref, ssems, rsems):
        x, y, c, chips = _place()
        t = 2 * x + y
        return [_remote(w_ref.at[c], out_ref.at[t, c], ssems.at[j], rsems.at[j], (cx, cy, c))
                for j, (cx, cy) in enumerate(chips)]

    def start(xin, xout, xsem):
        for cp in first_hop(xin[0], xout[0], *xsem):
            cp.start()

    def finish(xin, xout, xsem):
        out_ref, (ssems, rsems) = xout[0], xsem
        first = first_hop(xin[0], out_ref, *xsem)
        x, y, c, chips = _place()
        passed = []
        for j, (cx, cy) in enumerate(chips):
            got = out_ref.at[2 * cx + cy, c]
            _remote(got, got, ssems.at[j], rsems.at[j], (cx, cy, c)).wait_recv()
            cp = _remote(got, got, ssems.at[3 + j], rsems.at[3 + j], (x, y, 1 - c))
            cp.start()
            passed.append(cp)
        for j, (cx, cy) in enumerate(chips):
            got = out_ref.at[2 * cx + cy, 1 - c]
            _remote(got, got, ssems.at[3 + j], rsems.at[3 + j], (x, y, 1 - c)).wait_recv()
        for cp in first + passed:
            cp.wait_send()

    return _Exchange([wsh], [jax.ShapeDtypeStruct((4, 2, hh, 128), wsh.dtype)],
                     [pltpu.SemaphoreType.DMA((6,)), pltpu.SemaphoreType.DMA((6,))], start, finish)


def _gathered(landed, own):
    t = 2 * lax.axis_index("x") + lax.axis_index("y")
    return lax.dynamic_update_slice(landed, own[None], (t, 0, 0, 0))


def _swap_sibling(v):
    def copy(xin, xout, xsem):
        x, y, c, _ = _place()
        return _remote(xin[0], xout[0], xsem[0], xsem[1], (x, y, 1 - c))

    return _Exchange([v], [jax.ShapeDtypeStruct(v.shape, v.dtype)],
                     [pltpu.SemaphoreType.DMA, pltpu.SemaphoreType.DMA],
                     lambda *a: copy(*a).start(), lambda *a: copy(*a).wait())


def _scatter_chips(parts):
    n = len(parts)

    def copies(xin, xout, ssems, rsems):
        x, y, c, chips = _place()
        t = 2 * x + y
        return [_remote(xin[k].at[2 * cx + cy], xout[k].at[t], ssems.at[n * j + k], rsems.at[n * j + k], (cx, cy, c))
                for j, (cx, cy) in enumerate(chips) for k in range(n)]

    def start(xin, xout, xsem):
        for cp in copies(xin, xout, *xsem):
            cp.start()

    def finish(xin, xout, xsem):
        ssems, rsems = xsem
        x, y, c, chips = _place()
        for j, (cx, cy) in enumerate(chips):
            for k in range(n):
                got = xout[k].at[2 * cx + cy]
                _remote(got, got, ssems.at[n * j + k], rsems.at[n * j + k], (cx, cy, c)).wait_recv()
        for cp in copies(xin, xout, *xsem):
            cp.wait_send()

    return _Exchange(parts, [jax.ShapeDtypeStruct(a.shape, a.dtype) for a in parts],
                     [pltpu.SemaphoreType.DMA((3 * n,)), pltpu.SemaphoreType.DMA((3 * n,))], start, finish)


def _share_reduced(piece, eighth):
    def copies(t_ref, mine_r, sib_ref, rall_ref, ssems, rsems, lsem):
        x, y, c, _ = _place()
        me = 4 * x + 2 * y + c
        loc = pltpu.make_async_copy(mine_r, rall_ref.at[me], lsem)
        sends = [_remote(t_ref, sib_ref, ssems.at[0], rsems.at[0], (x, y, 1 - c))]
        peers = []
        for mask in range(1, 8):
            px = 1 - x if mask & 4 else x
            py = 1 - y if mask & 2 else y
            pc = 1 - c if mask & 1 else c
            peers.append((mask, px, py, pc))
            sends.append(_remote(mine_r, rall_ref.at[me], ssems.at[mask], rsems.at[mask], (px, py, pc)))
        return loc, sends, peers

    def start(xin, xout, xsem):
        loc, sends, _ = copies(*xin, *xout, *xsem)
        for cp in [loc] + sends:
            cp.start()

    def finish(xin, xout, xsem):
        (sib_ref, rall_ref), (ssems, rsems, _) = xout, xsem
        loc, sends, peers = copies(*xin, *xout, *xsem)
        x, y, c, _ = _place()
        _remote(sib_ref, sib_ref, ssems.at[0], rsems.at[0], (x, y, 1 - c)).wait_recv()
        for mask, px, py, pc in peers:
            got = rall_ref.at[4 * px + 2 * py + pc]
            _remote(got, got, ssems.at[mask], rsems.at[mask], (px, py, pc)).wait_recv()
        for cp in sends:
            cp.wait_send()
        loc.wait()

    return _Exchange([piece, eighth],
                     [jax.ShapeDtypeStruct(piece.shape, F32), jax.ShapeDtypeStruct((8, *eighth.shape), F32)],
                     [pltpu.SemaphoreType.DMA((8,)), pltpu.SemaphoreType.DMA((8,)), pltpu.SemaphoreType.DMA],
                     start, finish)


def _ceil_to(n, m):
    return -(-n // m) * m


def _pack_bf16(parts):
    flat = jnp.concatenate(parts)
    hw = _ceil_to(_ceil_to(flat.shape[0], 128) // 128, 32) // 2
    return jnp.pad(flat, (0, 2 * hw * 128 - flat.shape[0])).reshape(2, hw, 128), [p.shape[0] for p in parts]


def _segments(wall, sizes):
    wall = wall.reshape(4, -1)
    offs = [0]
    for n in sizes:
        offs.append(offs[-1] + n)
    return [wall[:, offs[i]:offs[i + 1]] for i in range(len(sizes))]


class _GradReduce:
    def __init__(self, sharded, replicated, c_idx, wire_bf16=False):
        self.c_idx, self.wire_bf16 = c_idx, wire_bf16
        self.sharded = [(n, g.shape[1]) for n, g in sharded]
        self.replicated = [(n, g.shape[0]) for n, g in replicated]
        sh = jnp.concatenate([g for _, g in sharded], axis=1)
        n_sh = sh.shape[1]
        self.hs = _ceil_to(_ceil_to(n_sh, 128) // 128, 256) // 2
        sh = jnp.pad(sh, ((0, 0), (0, 2 * self.hs * 128 - n_sh))).reshape(4, 2, self.hs, 128)
        rp = jnp.concatenate([g for _, g in replicated])
        self.rr = _ceil_to(_ceil_to(rp.shape[0], 128) // 128, 64) // 8
        rp = jnp.pad(rp, (0, 8 * self.rr * 128 - rp.shape[0])).reshape(4, 2, self.rr, 128)
        gbuf = jnp.concatenate([sh, rp], axis=2)
        self.hh = self.hs + self.rr
        self.mine = lax.dynamic_index_in_dim(gbuf, c_idx, axis=1, keepdims=False).reshape(4 * self.hh, 128)
        self.other = lax.dynamic_index_in_dim(gbuf, 1 - c_idx, axis=1, keepdims=False).reshape(4 * self.hh, 128)

    def swap(self):
        return _swap_sibling(self.other)

    def swapped(self, got):
        self.chip_sum = _add2(self.mine, got).reshape(4, self.hh, 128)

    def scatter(self):
        self.own_r = self.chip_sum[:, self.hs:]
        if self.wire_bf16:
            return _scatter_chips([self.chip_sum[:, :self.hs].astype(BF16), self.own_r])
        return _scatter_chips([self.chip_sum])

    def scattered(self, landed, landed_r=None):
        if landed_r is None:
            landed_r = landed[:, self.hs:]
        self.piece = _sum_slots(landed, self.chip_sum, self.hs)
        self.eighth = _sum_slots(landed_r, self.own_r, self.rr)

    def share(self):
        return _share_reduced(self.piece, self.eighth)

    def shared(self, sibling, rall):
        mine, sib = self.piece.reshape(-1), sibling.reshape(-1)
        self.shard = jnp.where(self.c_idx == 0, jnp.concatenate([mine, sib]), jnp.concatenate([sib, mine]))
        self.rall = rall

    def reduced(self):
        out = {}
        for group, flat in ((self.sharded, self.shard), (self.replicated, self.rall.reshape(-1))):
            off = 0
            for name, n in group:
                out[name] = flat[off:off + n]
                off += n
        return out


def _col_shards(w2d):
    rows, cols = w2d.shape
    return w2d.reshape(rows, 4, cols // 4).transpose(1, 0, 2).reshape(4, rows * (cols // 4))


def _from_col_shards(flat, rows):
    w = flat.shape[1] // rows
    return flat.reshape(4, rows, w).transpose(1, 0, 2).reshape(rows, 4 * w)


def _block_diag4(w):
    eye = jnp.eye(4, dtype=w.dtype)
    return jnp.einsum("gaij,ab->gaibj", w.reshape(3, 4, 64, 64), eye).reshape(3, 256, 256)


def _diag_blocks4(w):
    w5 = w.reshape(3, 4, 64, 4, 64)
    return jnp.stack([w5[:, a, :, a, :] for a in range(4)], axis=1).reshape(12, 64, 64)


def kernel(x, mem, positions, mla_w_in, mla_q_norm, mla_w_uq, mla_kv_norm, mla_w_ukv, lru_w_in, lru_conv_w, lru_conv_b, lru_w_rgate, lru_b_rgate, lru_w_igate, lru_b_igate, lru_lambda, w_mem_kv, w_out, ln_g, ln_b, loss_target, m_mla_w_in, m_mla_q_norm, m_mla_w_uq, m_mla_kv_norm, m_mla_w_ukv, m_lru_w_in, m_lru_conv_w, m_lru_conv_b, m_lru_w_rgate, m_lru_b_rgate, m_lru_w_igate, m_lru_b_igate, m_lru_lambda, m_w_mem_kv, m_w_out, m_ln_g, m_ln_b, v_mla_w_in, v_mla_q_norm, v_mla_w_uq, v_mla_kv_norm, v_mla_w_ukv, v_lru_w_in, v_lru_conv_w, v_lru_conv_b, v_lru_w_rgate, v_lru_b_rgate, v_lru_w_igate, v_lru_b_igate, v_lru_lambda, v_w_mem_kv, v_w_out, v_ln_g, v_ln_b):
    s = x.shape[1]
    c_idx = lax.axis_index("c")
    x2, mem2, tgt2 = x[0], mem[0], loss_target[0]

    early = [mla_w_in[0], mla_w_uq[0], mla_w_ukv[0], w_mem_kv, w_out[0]]
    flat, early_sizes = _pack_bf16([p.reshape(-1).astype(BF16) for p in early])
    seg = _segments(_gathered(_run_exchange(_gather_shards(flat), "gather_weights")[0], flat), early_sizes)
    win0 = _from_col_shards(seg[0], 1024)
    wuq = _from_col_shards(seg[1], Q_LORA)
    wukv = _from_col_shards(seg[2], KV_LORA)
    wmem = seg[3].reshape(4, 2, 256, 512).transpose(1, 0, 2, 3).reshape(2, 1024, 512)
    wout0 = seg[4].reshape(1024, 1024)

    small = jnp.concatenate([lru_conv_w[0].reshape(-1), lru_conv_b[0], lru_b_rgate[0], lru_b_igate[0], lru_lambda[0]])
    late = [lru_w_in[0].reshape(-1).astype(BF16), w_out[1].reshape(-1).astype(BF16),
            lax.bitcast_convert_type(small, BF16).reshape(-1)]
    flat_late, late_sizes = _pack_bf16(late)

    def late_weights(landed):
        seg = _segments(_gathered(landed[0], flat_late), late_sizes)
        small_all = lax.bitcast_convert_type(seg[2].reshape(4, -1, 2), F32)
        cw = small_all[:, :768].reshape(4, 4, 192).transpose(1, 0, 2).reshape(4, TOK_W)
        cb, br, bi, lam = (small_all[:, 768 + 192 * k:960 + 192 * k].reshape(1, TOK_W) for k in range(4))
        return _from_col_shards(seg[0], 1024), seg[1].reshape(1024, 1024), cw, cb, br, bi, lam

    def reduce_late(g):
        return _GradReduce(
            [("lru_w_in", _col_shards(g["lru_w_in"])), ("lru_conv_w", _col_shards(g["lru_conv_w"])),
             ("lru_conv_b", _col_shards(g["lru_conv_b"])), ("lru_b_rgate", _col_shards(g["lru_b_rgate"])),
             ("lru_b_igate", _col_shards(g["lru_b_igate"])), ("lru_lambda", _col_shards(g["lru_lambda"])),
             ("w_mem_kv1", g["w_mem_kv1"].reshape(4, -1)), ("w_out1", g["w_out1"].reshape(4, -1))],
            [("lru_w_rgate", g["lru_w_rgate"].reshape(-1)), ("lru_w_igate", g["lru_w_igate"].reshape(-1)),
             ("ln_g1", g["ln_g1"].reshape(-1)), ("ln_b1", g["ln_b1"].reshape(-1))], c_idx)

    g0, late_red = _local_step(
        x2, mem2, positions.reshape(s, 1), tgt2, win0, wuq, wukv, wmem, wout0, mla_q_norm, mla_kv_norm,
        lru_w_rgate[0], lru_w_igate[0], ln_g, ln_b, late_weights, _gather_shards(flat_late), reduce_late)

    early_red = _GradReduce(
        [("mla_w_in", _col_shards(g0["mla_w_in"])), ("mla_w_uq", _col_shards(g0["mla_w_uq"])),
         ("mla_w_ukv", _col_shards(g0["mla_w_ukv"])), ("w_mem_kv0", g0["w_mem_kv0"].reshape(4, -1)),
         ("w_out0", g0["w_out0"].reshape(4, -1))],
        [("mla_q_norm", g0["mla_q_norm"].reshape(-1)), ("mla_kv_norm", g0["mla_kv_norm"].reshape(-1)),
         ("ln_g0", g0["ln_g0"].reshape(-1)), ("ln_b0", g0["ln_b0"].reshape(-1)), ("loss", g0["loss"].reshape(-1))],
        c_idx, wire_bf16=True)
    early_red.swapped(*_run_exchange(early_red.swap(), "swap_sibling"))
    early_red.scattered(*_run_exchange(early_red.scatter(), "scatter_chips"))
    early_red.shared(*_run_exchange(early_red.share(), "share_reduced"))
    red = {**late_red.reduced(), **early_red.reduced()}
    red["w_mem_kv"] = jnp.concatenate([red["w_mem_kv0"], red["w_mem_kv1"]])
    red["w_out"] = jnp.concatenate([red["w_out0"], red["w_out1"]])
    red["ln_g"] = jnp.concatenate([red["ln_g0"], red["ln_g1"]])
    red["ln_b"] = jnp.concatenate([red["ln_b0"], red["ln_b1"]])

    weights = dict(mla_w_in=mla_w_in, mla_q_norm=mla_q_norm, mla_w_uq=mla_w_uq, mla_kv_norm=mla_kv_norm,
                   mla_w_ukv=mla_w_ukv, lru_w_in=lru_w_in, lru_conv_w=lru_conv_w, lru_conv_b=lru_conv_b,
                   lru_w_rgate=lru_w_rgate, lru_b_rgate=lru_b_rgate, lru_w_igate=lru_w_igate, lru_b_igate=lru_b_igate,
                   lru_lambda=lru_lambda, w_mem_kv=w_mem_kv, w_out=w_out, ln_g=ln_g, ln_b=ln_b)
    m_in = dict(mla_w_in=m_mla_w_in, mla_q_norm=m_mla_q_norm, mla_w_uq=m_mla_w_uq, mla_kv_norm=m_mla_kv_norm,
                mla_w_ukv=m_mla_w_ukv, lru_w_in=m_lru_w_in, lru_conv_w=m_lru_conv_w, lru_conv_b=m_lru_conv_b,
                lru_w_rgate=m_lru_w_rgate, lru_b_rgate=m_lru_b_rgate, lru_w_igate=m_lru_w_igate,
                lru_b_igate=m_lru_b_igate, lru_lambda=m_lru_lambda, w_mem_kv=m_w_mem_kv, w_out=m_w_out, ln_g=m_ln_g,
                ln_b=m_ln_b)
    v_in = dict(mla_w_in=v_mla_w_in, mla_q_norm=v_mla_q_norm, mla_w_uq=v_mla_w_uq, mla_kv_norm=v_mla_kv_norm,
                mla_w_ukv=v_mla_w_ukv, lru_w_in=v_lru_w_in, lru_conv_w=v_lru_conv_w, lru_conv_b=v_lru_conv_b,
                lru_w_rgate=v_lru_w_rgate, lru_b_rgate=v_lru_b_rgate, lru_w_igate=v_lru_w_igate,
                lru_b_igate=v_lru_b_igate, lru_lambda=v_lru_lambda, w_mem_kv=v_w_mem_kv, w_out=v_w_out, ln_g=v_ln_g,
                ln_b=v_ln_b)
    order = ["mla_w_in", "mla_q_norm", "mla_w_uq", "mla_kv_norm", "mla_w_ukv", "lru_w_in", "lru_conv_w", "lru_conv_b",
             "lru_w_rgate", "lru_b_rgate", "lru_w_igate", "lru_b_igate", "lru_lambda", "w_mem_kv", "w_out", "ln_g",
             "ln_b"]
    grads, deltas, new_m, new_v = {}, {}, {}, {}
    for name in order:
        shape = weights[name].shape
        two_d = (math.prod(shape[:-1]), shape[-1])
        g2 = red[name].reshape(two_d)
        d2, m2, v2 = _adamw(name, weights[name].reshape(two_d), g2, m_in[name].reshape(two_d),
                            v_in[name].reshape(two_d))
        grads[name], deltas[name] = g2.reshape(shape), d2.reshape(shape)
        new_m[name], new_v[name] = m2.reshape(shape), v2.reshape(shape)
    return (red["loss"][0], g0["x"][None], *[grads[n] for n in order], *[deltas[n] for n in order],
            *[new_m[n] for n in order], *[new_v[n] for n in order])


def _local_step(x2, mem2, pos_col, tgt2, win0, wuq, wukv, wmem, wout0, gq, gkv, w_rgate, w_igate, ln_g, ln_b,
                late_weights, gather_late=None, reduce_late=None):
    s = x2.shape[0]
    zpad = jnp.zeros((1024, 64), BF16)
    win0p = jnp.concatenate([win0[:, 672:1696], win0[:, 1696:1952], win0[:, 0:384], win0[:, 384:640],
                             zpad, win0[:, 640:672], zpad[:, :32]], axis=1)
    wuq_p = jnp.pad(wuq.reshape(Q_LORA, 12, 96), ((0, 0), (0, 0), (0, 32))).reshape(Q_LORA, QK_W)
    wukv3 = wukv.reshape(KV_LORA, 12, 128)
    wk_p = jnp.pad(wukv3[:, :, :64], ((0, 0), (0, 0), (0, 64))).reshape(KV_LORA, QK_W)
    wv = wukv3[:, :, 64:].reshape(KV_LORA, TOK_W)
    wr_bd = _block_diag4(w_rgate).astype(BF16)
    wi_bd = _block_diag4(w_igate).astype(BF16)
    half = 16
    inv_freq = ROPE_THETA ** (-jnp.arange(half, dtype=F32) / half)
    inv_lane = jnp.concatenate([jnp.zeros((64,), F32), inv_freq, inv_freq, jnp.zeros((32,), F32)]).reshape(1, HEAD_PAD)

    ctab, satab, sbtab = _rope_tables(pos_col, inv_lane, min(TB_PROJ, s))
    memkv = _mem_kv(mem2, wmem)
    gate0, qm0, cq, ckv, q_p, q_t, k_p, v_b, v_t = _mla_proj_fwd(x2, win0p, gq, gkv, wuq_p, wk_p, wv,
                                                                ctab, satab, sbtab)
    tok0, lse, *landed = _attn_fwd(q_p, k_p, v_t, exchange=gather_late)
    win1, wout1, cw, cb, br, bi, lam = late_weights(landed)
    g0, b0, g1, b1 = ln_g[0:1], ln_b[0:1], ln_g[1:2], ln_b[1:2]
    h1 = _mix_fwd(tok0, gate0, qm0, memkv[0], wout0, x2, g0, b0)
    u1, gate1, qm1, hs1 = _lru_fwd(h1, win1, cw, cb, wr_bd, br, wi_bd, bi, lam)

    dres1, dtok1, dgate1, dqm1, dwout1, dmemkv1, dg1, db1, loss = _mix_bwd(
        hs1, gate1, qm1, memkv[1], wout1, h1, g1, b1, tgt2, True)
    dh1, dwin1, dcw, dcb, dwr_bd, dbr, dwi_bd, dbi, dlam = _lru_bwd(
        dtok1, dgate1, dqm1, dres1, h1, u1, hs1, win1, cw, cb, wr_bd, br, wi_bd, bi, lam)
    late = {"lru_w_in": dwin1, "lru_conv_w": dcw, "lru_conv_b": dcb, "lru_b_rgate": dbr, "lru_b_igate": dbi,
            "lru_lambda": dlam, "w_mem_kv1": _mem_kv_bwd(mem2, dmemkv1), "w_out1": dwout1,
            "lru_w_rgate": _diag_blocks4(dwr_bd), "lru_w_igate": _diag_blocks4(dwi_bd), "ln_g1": dg1, "ln_b1": db1}
    red = reduce_late(late) if reduce_late is not None else None

    dres0, dtok0, dgate0, dqm0, dwout0, dmemkv0, dg0, db0, _, *got = _mix_bwd(
        tok0, gate0, qm0, memkv[0], wout0, x2, g0, b0, dh1, False, exchange=red.swap() if red else None)
    if red:
        red.swapped(*got)
    dob, dobt, stats = _attn_prep(tok0, dtok0, lse)
    dq_p, dk_t, dv_t, *got = _attn_bwd(q_p, q_t, k_p, v_b, dob, dobt, stats,
                                       exchange=red.scatter() if red else None)
    if red:
        red.scattered(*got)
    dx, dwin0p, dwuq_p, dwk_p, dwv, dgq, dgkv, *got = _mla_proj_bwd(
        x2, cq, ckv, dq_p, dk_t, dv_t, dgate0, dqm0, dres0, win0p, gq, gkv, wuq_p, wk_p, wv,
        ctab, satab, sbtab, exchange=red.share() if red else None)
    if red:
        red.shared(*got)

    dwin0 = jnp.concatenate([dwin0p[:, 1280:1664], dwin0p[:, 1664:1920], dwin0p[:, 1984:2016], dwin0p[:, 0:1024],
                             dwin0p[:, 1024:1280]], axis=1)
    dwuq = dwuq_p.reshape(Q_LORA, 12, 128)[:, :, :96].reshape(Q_LORA, 1152)
    dwukv = jnp.concatenate([dwk_p.reshape(KV_LORA, 12, 128)[:, :, :64], dwv.reshape(KV_LORA, 12, 64)],
                            axis=2).reshape(KV_LORA, 1536)
    early = {"x": dx, "loss": loss, "mla_w_in": dwin0, "mla_w_uq": dwuq, "mla_w_ukv": dwukv,
             "w_mem_kv0": _mem_kv_bwd(mem2, dmemkv0), "w_out0": dwout0, "mla_q_norm": dgq, "mla_kv_norm": dgkv,
             "ln_g0": dg0, "ln_b0": db0}
    return early, (red if red else late)
```

```python
import functools
import math

import jax
import jax.numpy as jnp
from jax import lax
from jax.experimental import pallas as pl
from jax.experimental.pallas import tpu as pltpu

F32, BF16 = jnp.float32, jnp.bfloat16
MESH = pl.DeviceIdType.MESH

D_MODEL = 1024
N_TOK_HEADS = 12
TOK_W = 768
MEM_W = 256
MEM_LEN = 256
Q_LORA, KV_LORA = 384, 256
HEAD_PAD = 128
QK_W = N_TOK_HEADS * HEAD_PAD
ATT_SCALE = 1.0 / math.sqrt(96.0)
ATT_SCALE_LOG2 = ATT_SCALE * math.log2(math.e)
ROPE_THETA = 10000.0
LRU_C = 8.0
ALPHA = 4.0 ** 0.25
NORM_EPS = 1e-6
ADAM_LR, ADAM_B1, ADAM_B2, ADAM_EPS, ADAM_WD, ADAM_STEP = 0.001, 0.9, 0.999, 1e-08, 0.01, 10

TB_PROJ = 512
TB_PROJ_BWD = 256
TB_MIX = 512
TB_LRU = 256
TQ_ATT = 512
TQ_ATT_FWD = 1024
TK_ATT = 1024
VMEM_LIMIT = 56 * 1024 * 1024


def _mm(a, b):
    return jnp.dot(a.astype(BF16), b.astype(BF16), preferred_element_type=F32)


def _mm_nt(a, b):
    return lax.dot_general(a.astype(BF16), b.astype(BF16), (((1,), (1,)), ((), ())), preferred_element_type=F32)


def _mm_tn(a, b):
    return lax.dot_general(a.astype(BF16), b.astype(BF16), (((0,), (0,)), ((), ())), preferred_element_type=F32)


def _rows(tb, w):
    return pl.BlockSpec((tb, w), lambda i: (i, 0))


def _const(shape):
    n = len(shape)
    return pl.BlockSpec(shape, lambda i: (0,) * n)


def _params(sem=("arbitrary",)):
    return pltpu.CompilerParams(dimension_semantics=sem, vmem_limit_bytes=VMEM_LIMIT)


def _iota(shape, dim):
    return lax.broadcasted_iota(jnp.int32, shape, dim)


def _rope_tables(pos_col, inv_lane, tb):
    s = pos_col.shape[0]

    def body(pos_ref, inv_ref, c_ref, sa_ref, sb_ref):
        ang = pos_ref[...].astype(F32) * inv_ref[...]
        lane = _iota(ang.shape, 1)
        cs, sn = jnp.cos(ang), jnp.sin(ang)
        c_ref[...] = jnp.where(lane < 64, 1.0, jnp.where(lane < 96, cs, 0.0))
        sa_ref[...] = jnp.where((lane >= 64) & (lane < 80), -sn, 0.0)
        sb_ref[...] = jnp.where((lane >= 80) & (lane < 96), sn, 0.0)

    shp = jax.ShapeDtypeStruct((s, HEAD_PAD), F32)
    return pl.pallas_call(
        body, name="rope_tables", grid=(s // tb,), out_shape=(shp, shp, shp),
        in_specs=[_rows(tb, 1), _const((1, HEAD_PAD))], out_specs=(_rows(tb, HEAD_PAD),) * 3,
        compiler_params=_params(("parallel",)),
    )(pos_col, inv_lane)


def _rope(t, c, sa, sb):
    return t * c + pltpu.roll(t, 112, 1) * sa + pltpu.roll(t, 16, 1) * sb


def _rope_t(d, c, sa, sb):
    return d * c + pltpu.roll(d * sa, 16, 1) + pltpu.roll(d * sb, 112, 1)


def _rms(c, g):
    r = lax.rsqrt(jnp.mean(c * c, axis=-1, keepdims=True) + NORM_EPS)
    xh = c * r
    return xh * g, xh, r


def _mla_proj_fwd(x, win, gq, gkv, wuq, wukv_k, wukv_v, ctab, satab, sbtab):
    s = x.shape[0]
    tb = min(TB_PROJ, s)

    def body(x_ref, win_ref, gq_ref, gkv_ref, wuq_ref, wk_ref, wv_ref, c_ref, sa_ref, sb_ref,
             gate_ref, qm_ref, cq_ref, ckv_ref, q_ref, qt_ref, k_ref, v_ref, vt_ref):
        z = _mm(x_ref[...], win_ref[...])
        gate_ref[...] = z[:, 0:1024]
        qm_ref[...] = z[:, 1024:1280].astype(BF16)
        cq = z[:, 1280:1664]
        ckv = z[:, 1664:1920]
        cq_ref[...] = cq
        ckv_ref[...] = ckv
        c, sa, sb = c_ref[...], sa_ref[...], sb_ref[...]
        nq, _, _ = _rms(cq, gq_ref[...])
        nkv, _, _ = _rms(ckv, gkv_ref[...])
        qf = _mm(nq, wuq_ref[...])
        kf = _mm(nkv, wk_ref[...])
        vf = _mm(nkv, wv_ref[...])
        v_ref[...] = vf.astype(BF16)
        for j in range(N_TOK_HEADS // 2):
            sl = slice(HEAD_PAD * j, HEAD_PAD * (j + 1))
            vt_ref[sl, :] = vf[:, sl].T.astype(BF16)
        kr = _rope(z[:, 1920:2048], c, sa, sb)
        for h in range(N_TOK_HEADS):
            sl = slice(HEAD_PAD * h, HEAD_PAD * (h + 1))
            qh = _rope(qf[:, sl], c, sa, sb) * ATT_SCALE_LOG2
            q_ref[:, sl] = qh.astype(BF16)
            qt_ref[sl, :] = qh.T.astype(BF16)
            k_ref[:, sl] = (kf[:, sl] + kr).astype(BF16)

    outs = (jax.ShapeDtypeStruct((s, 1024), F32), jax.ShapeDtypeStruct((s, MEM_W), BF16),
            jax.ShapeDtypeStruct((s, Q_LORA), F32), jax.ShapeDtypeStruct((s, KV_LORA), F32),
            jax.ShapeDtypeStruct((s, QK_W), BF16), jax.ShapeDtypeStruct((QK_W, s), BF16),
            jax.ShapeDtypeStruct((s, QK_W), BF16),
            jax.ShapeDtypeStruct((s, TOK_W), BF16), jax.ShapeDtypeStruct((TOK_W, s), BF16))

    def cols(w):
        return pl.BlockSpec((w, tb), lambda i: (0, i))

    return pl.pallas_call(
        body, name="mla_proj_fwd", grid=(s // tb,), out_shape=outs,
        in_specs=[_rows(tb, 1024), _const((1024, 2048)), _const((1, Q_LORA)), _const((1, KV_LORA)),
                  _const((Q_LORA, QK_W)), _const((KV_LORA, QK_W)), _const((KV_LORA, TOK_W)),
                  _rows(tb, HEAD_PAD), _rows(tb, HEAD_PAD), _rows(tb, HEAD_PAD)],
        out_specs=(_rows(tb, 1024), _rows(tb, MEM_W), _rows(tb, Q_LORA), _rows(tb, KV_LORA),
                   _rows(tb, QK_W), cols(QK_W), _rows(tb, QK_W), _rows(tb, TOK_W), cols(TOK_W)),
        compiler_params=_params(("parallel",)),
    )(x, win, gq, gkv, wuq, wukv_k, wukv_v, ctab, satab, sbtab)


def _mla_proj_bwd(x, cq, ckv, dq, dkt, dvt, dgate, dqm, dres, win, gq, gkv, wuq, wukv_k, wukv_v, ctab, satab, sbtab):
    s = x.shape[0]
    tb = min(TB_PROJ_BWD, s)

    def body(x_ref, cq_ref, ckv_ref, dq_ref, dkt_ref, dvt_ref, dgate_ref, dqm_ref, dres_ref, win_ref, gq_ref, gkv_ref,
             wuq_ref, wk_ref, wv_ref, c_ref, sa_ref, sb_ref,
             dx_ref, dwin_ref, dwuq_ref, dwk_ref, dwv_ref, dgq_ref, dgkv_ref):
        @pl.when(pl.program_id(0) == 0)
        def _():
            for r in (dwin_ref, dwuq_ref, dwk_ref, dwv_ref, dgq_ref, dgkv_ref):
                r[...] = jnp.zeros_like(r)

        c, sa, sb = c_ref[...], sa_ref[...], sb_ref[...]
        lane = _iota((tb, HEAD_PAD), 1)
        gq, gkv = gq_ref[...], gkv_ref[...]
        nq, xhq, rq = _rms(cq_ref[...], gq)
        nkv, xhk, rk = _rms(ckv_ref[...], gkv)
        dkp = dkt_ref[...].T * math.log(2.0)
        dqs, dkr = [], jnp.zeros((tb, HEAD_PAD), F32)
        for h in range(N_TOK_HEADS):
            sl = slice(HEAD_PAD * h, HEAD_PAD * (h + 1))
            dqs.append(_rope_t(dq_ref[:, sl], c, sa, sb).astype(BF16))
            dkr = dkr + dkp[:, sl]
        dqf = jnp.concatenate(dqs, axis=1)
        dkr = jnp.where((lane >= 64) & (lane < 96), _rope_t(dkr, c, sa, sb), 0.0)
        dvb = dvt_ref[...].T.astype(BF16)
        dkb = dkp.astype(BF16)
        dnq = _mm_nt(dqf, wuq_ref[...])
        dwuq_ref[...] += _mm_tn(nq, dqf)
        dgq_ref[...] += jnp.sum(dnq * xhq, axis=0, keepdims=True)
        dxh = dnq * gq
        dcq = rq * (dxh - xhq * jnp.mean(dxh * xhq, axis=-1, keepdims=True))
        dnkv = _mm_nt(dkb, wk_ref[...]) + _mm_nt(dvb, wv_ref[...])
        nkvb = nkv.astype(BF16)
        dwk_ref[...] += _mm_tn(nkvb, dkb)
        dwv_ref[...] += _mm_tn(nkvb, dvb)
        dgkv_ref[...] += jnp.sum(dnkv * xhk, axis=0, keepdims=True)
        dxh = dnkv * gkv
        dckv = rk * (dxh - xhk * jnp.mean(dxh * xhk, axis=-1, keepdims=True))
        dz = jnp.concatenate([dgate_ref[...], dqm_ref[...], dcq, dckv, dkr], axis=1).astype(BF16)
        dx_ref[...] = _mm_nt(dz, win_ref[...]) + dres_ref[...]
        dwin_ref[...] += _mm_tn(x_ref[...], dz)

    outs = (jax.ShapeDtypeStruct((s, 1024), F32), jax.ShapeDtypeStruct((1024, 2048), F32),
            jax.ShapeDtypeStruct((Q_LORA, QK_W), F32), jax.ShapeDtypeStruct((KV_LORA, QK_W), F32),
            jax.ShapeDtypeStruct((KV_LORA, TOK_W), F32), jax.ShapeDtypeStruct((1, Q_LORA), F32),
            jax.ShapeDtypeStruct((1, KV_LORA), F32))
    return _run(
        body, name="mla_proj_bwd", grid=(s // tb,), out_shape=outs,
        in_specs=[_rows(tb, 1024), _rows(tb, Q_LORA), _rows(tb, KV_LORA), _rows(tb, QK_W),
                  pl.BlockSpec((QK_W, tb), lambda i: (0, i)), pl.BlockSpec((TOK_W, tb), lambda i: (0, i)),
                  _rows(tb, 1024), _rows(tb, MEM_W), _rows(tb, 1024),
                  _const((1024, 2048)), _const((1, Q_LORA)), _const((1, KV_LORA)),
                  _const((Q_LORA, QK_W)), _const((KV_LORA, QK_W)), _const((KV_LORA, TOK_W)),
                  _rows(tb, HEAD_PAD), _rows(tb, HEAD_PAD), _rows(tb, HEAD_PAD)],
        out_specs=(_rows(tb, 1024), _const((1024, 2048)), _const((Q_LORA, QK_W)), _const((KV_LORA, QK_W)),
                   _const((KV_LORA, TOK_W)), _const((1, Q_LORA)), _const((1, KV_LORA))),
        args=(x, cq, ckv, dq, dkt, dvt, dgate, dqm, dres, win, gq, gkv, wuq, wukv_k, wukv_v, ctab, satab, sbtab),
        sem=("arbitrary",))


def _attn_fwd(q, k, vt, exchange=None):
    s = q.shape[0]
    tq = min(TQ_ATT_FWD, s)
    tk = min(TK_ATT, s)

    def body(q_ref, k_ref, vt_ref, o_ref, lse_ref):
        i = pl.program_id(1)
        nfull = (i * tq) // tk
        krow = _iota((tk, tq), 0)
        qpos = i * tq + _iota((tk, tq), 1)

        def head_tile(hh, st, carry, masked):
            hs = slice(HEAD_PAD * hh, HEAD_PAD * (hh + 1))
            m, l, acc = carry
            sc = _mm_nt(k_ref[pl.ds(st, tk), hs], q_ref[:, hs])
            if masked:
                sc = jnp.where(st + krow <= qpos, sc, -jnp.inf)
            m_new = jnp.maximum(m, jnp.max(sc, axis=0, keepdims=True))
            p = jnp.exp2(sc - m_new)
            a = jnp.exp2(m - m_new)
            l = a * l + jnp.sum(p, axis=0, keepdims=True)
            acc = a * acc + _mm(vt_ref[64 * hh:64 * (hh + 1), pl.ds(st, tk)], p)
            return m_new, l, acc

        def tile(j, carry, masked):
            st = pl.multiple_of(j * tk, tk)
            return tuple(head_tile(hh, st, carry[hh], masked) for hh in range(2))

        def init():
            return (jnp.full((1, tq), -jnp.inf, F32), jnp.zeros((1, tq), F32), jnp.zeros((64, tq), F32))

        carry = lax.fori_loop(0, nfull, functools.partial(tile, masked=False), (init(), init()))
        (ma, la, acca), (mb, lb, accb) = tile(nfull, carry, True)
        o_ref[...] = jnp.concatenate([acca / la, accb / lb], axis=0).T
        lse_ref[...] = jnp.concatenate([jnp.broadcast_to(ma + jnp.log2(la), (64, tq)),
                                        jnp.broadcast_to(mb + jnp.log2(lb), (64, tq))], axis=0).T

    shp = jax.ShapeDtypeStruct((s, TOK_W), F32)
    return _run(
        body, name="attn_fwd", grid=(N_TOK_HEADS // 2, s // tq), out_shape=(shp, shp),
        in_specs=[pl.BlockSpec((tq, 2 * HEAD_PAD), lambda j, i: (i, j)),
                  pl.BlockSpec((s, 2 * HEAD_PAD), lambda j, i: (0, j)),
                  pl.BlockSpec((HEAD_PAD, s), lambda j, i: (j, 0))],
        out_specs=(pl.BlockSpec((tq, HEAD_PAD), lambda j, i: (i, j)),) * 2,
        args=(q, k, vt), sem=("parallel", "arbitrary"), exchange=exchange)


def _attn_prep(o, do, lse):
    s = o.shape[0]
    tb = min(TB_PROJ, s)
    npair = N_TOK_HEADS // 2

    def body(o_ref, do_ref, lse_ref, dob_ref, dot_ref, st_ref):
        lane = _iota((tb, HEAD_PAD), 1)
        do = do_ref[...]
        dob_ref[...] = do.astype(BF16)
        prod = do * o_ref[...]
        for j in range(npair):
            sl = slice(HEAD_PAD * j, HEAD_PAD * (j + 1))
            dot_ref[sl, :] = do[:, sl].T.astype(BF16)
            pj = prod[:, sl]
            da = jnp.sum(jnp.where(lane < 64, pj, 0.0), axis=-1, keepdims=True)
            db = jnp.sum(jnp.where(lane >= 64, pj, 0.0), axis=-1, keepdims=True)
            la = lse_ref[:, HEAD_PAD * j:HEAD_PAD * j + 1]
            lb = lse_ref[:, HEAD_PAD * j + 64:HEAD_PAD * j + 65]
            st_ref[j] = jnp.where(lane == 0, la, jnp.where(lane == 1, lb, jnp.where(lane == 2, da,
                                                                                     jnp.where(lane == 3, db, 0.0))))

    return pl.pallas_call(
        body, name="attn_prep", grid=(s // tb,),
        out_shape=(jax.ShapeDtypeStruct((s, TOK_W), BF16), jax.ShapeDtypeStruct((TOK_W, s), BF16),
                   jax.ShapeDtypeStruct((npair, s, HEAD_PAD), F32)),
        in_specs=[_rows(tb, TOK_W)] * 3,
        out_specs=(_rows(tb, TOK_W), pl.BlockSpec((TOK_W, tb), lambda i: (0, i)),
                   pl.BlockSpec((npair, tb, HEAD_PAD), lambda i: (0, i, 0))),
        compiler_params=_params(("parallel",)),
    )(o, do, lse)


def _attn_bwd(q, qt, k, v, dob, dobt, stats, exchange=None):
    s = q.shape[0]
    t = min(TQ_ATT, s)
    nq = s // t

    def body(q_ref, qt_ref, do_ref, dot_ref, st_ref, k_ref, v_ref, dq_ref, dkt_ref, dvt_ref):
        i = pl.program_id(1)

        @pl.when(i == 0)
        def _():
            dkt_ref[...] = jnp.zeros_like(dkt_ref)
            dvt_ref[...] = jnp.zeros_like(dvt_ref)

        lane = _iota((t, HEAD_PAD), 1)
        qpos, kcol = _iota((t, t), 0), _iota((t, t), 1)
        do = do_ref[...]
        stats = st_ref[0]

        def head_tile(hh, ks, dq_acc, masked):
            hs = slice(HEAD_PAD * hh, HEAD_PAD * (hh + 1))
            qh = q_ref[:, hs]
            kh = k_ref[pl.ds(ks, t), hs]
            dom = jnp.where((lane < 64) if hh == 0 else (lane >= 64), do, jnp.zeros_like(do))
            lse = stats[:, hh:hh + 1]
            dlt = stats[:, 2 + hh:3 + hh]
            sc = _mm_nt(qh, kh)
            if masked:
                sc = jnp.where(kcol <= qpos, sc, -jnp.inf)
            p = jnp.exp2(sc - lse)
            dp = _mm_nt(dom, v_ref[pl.ds(ks, t), :])
            ds = (p * (dp - dlt)).astype(BF16)
            dvt_ref[64 * hh:64 * (hh + 1), pl.ds(ks, t)] += _mm(dot_ref[64 * hh:64 * (hh + 1), :], p)
            dkt_ref[HEAD_PAD * hh:HEAD_PAD * hh + 96, pl.ds(ks, t)] += _mm(qt_ref[HEAD_PAD * hh:HEAD_PAD * hh + 96, :], ds)
            return dq_acc + _mm(ds, kh)

        def tile(j, carry, masked):
            ks = pl.multiple_of(j * t, t)
            return tuple(head_tile(hh, ks, carry[hh], masked) for hh in range(2))

        zero = jnp.zeros((t, HEAD_PAD), F32)
        carry = lax.fori_loop(0, i, functools.partial(tile, masked=False), (zero, zero))
        dqa, dqb = tile(i, carry, True)
        dq_ref[...] = jnp.concatenate([dqa, dqb], axis=1) * ATT_SCALE

    return _run(
        body, name="attn_bwd", grid=(N_TOK_HEADS // 2, nq),
        out_shape=(jax.ShapeDtypeStruct((s, QK_W), F32), jax.ShapeDtypeStruct((QK_W, s), F32),
                   jax.ShapeDtypeStruct((TOK_W, s), F32)),
        in_specs=[pl.BlockSpec((t, 2 * HEAD_PAD), lambda j, i: (i, j)),
                  pl.BlockSpec((2 * HEAD_PAD, t), lambda j, i: (j, i)),
                  pl.BlockSpec((t, HEAD_PAD), lambda j, i: (i, j)),
                  pl.BlockSpec((HEAD_PAD, t), lambda j, i: (j, i)),
                  pl.BlockSpec((1, t, HEAD_PAD), lambda j, i: (j, i, 0)),
                  pl.BlockSpec((s, 2 * HEAD_PAD), lambda j, i: (0, j)),
                  pl.BlockSpec((s, HEAD_PAD), lambda j, i: (0, j))],
        out_specs=(pl.BlockSpec((t, 2 * HEAD_PAD), lambda j, i: (i, j)),
                   pl.BlockSpec((2 * HEAD_PAD, s), lambda j, i: (j, 0)),
                   pl.BlockSpec((HEAD_PAD, s), lambda j, i: (j, 0))),
        args=(q, qt, dob, dobt, stats, k, v), sem=("parallel", "arbitrary"), exchange=exchange)


def _mem_kv(mem, wmem):
    def body(m_ref, w_ref, o_ref):
        o_ref[0] = _mm(m_ref[...], w_ref[0]).astype(BF16)

    return pl.pallas_call(
        body, name="mem_kv", grid=(2,), out_shape=jax.ShapeDtypeStruct((2, MEM_LEN, 512), BF16),
        in_specs=[_const((MEM_LEN, 1024)), pl.BlockSpec((1, 1024, 512), lambda l: (l, 0, 0))],
        out_specs=pl.BlockSpec((1, MEM_LEN, 512), lambda l: (l, 0, 0)),
        compiler_params=_params(("parallel",)),
    )(mem, wmem)


def _mem_kv_bwd(mem, dmemkv):
    def body(m_ref, d_ref, o_ref):
        o_ref[...] = _mm_tn(m_ref[...], d_ref[...])

    return pl.pallas_call(
        body, name="mem_kv_bwd", grid=(1,), out_shape=jax.ShapeDtypeStruct((1024, 512), F32),
        in_specs=[_const((MEM_LEN, 1024)), _const((MEM_LEN, 512))], out_specs=_const((1024, 512)),
        compiler_params=_params(("arbitrary",)),
    )(mem, dmemkv)


def _head_mask(lane, sub):
    return (lane < 64) if sub == 0 else (lane >= 64)


def _mem_attn(qm, kv):
    tb = qm.shape[0]
    lane = _iota((tb, HEAD_PAD), 1)
    outs, ps = [], []
    for pp in range(2):
        qp = qm[:, HEAD_PAD * pp:HEAD_PAD * (pp + 1)]
        kp = kv[:, HEAD_PAD * pp:HEAD_PAD * (pp + 1)]
        vp = kv[:, MEM_W + HEAD_PAD * pp:MEM_W + HEAD_PAD * (pp + 1)]
        pair = None
        for sub in range(2):
            qh = jnp.where(_head_mask(lane, sub), qp, jnp.zeros_like(qp))
            sc = _mm_nt(qh, kp) * 0.125
            e = jnp.exp(sc - jnp.max(sc, axis=-1, keepdims=True))
            p = e / jnp.sum(e, axis=-1, keepdims=True)
            o = _mm(p, vp)
            ps.append(p)
            pair = o if sub == 0 else jnp.where(lane < 64, pair, o)
        outs.append(pair)
    return jnp.concatenate(outs, axis=1), ps


def _mem_attn_bwd(dmo, qm, kv, ps):
    tb = qm.shape[0]
    lane = _iota((tb, HEAD_PAD), 1)
    dqs, dks, dvs = [], [], []
    for pp in range(2):
        qp = qm[:, HEAD_PAD * pp:HEAD_PAD * (pp + 1)]
        kp = kv[:, HEAD_PAD * pp:HEAD_PAD * (pp + 1)]
        vp = kv[:, MEM_W + HEAD_PAD * pp:MEM_W + HEAD_PAD * (pp + 1)]
        dop = dmo[:, HEAD_PAD * pp:HEAD_PAD * (pp + 1)]
        dq_pair, dk_pair, dv_pair = None, None, None
        for sub in range(2):
            msk = _head_mask(lane, sub)
            p = ps[2 * pp + sub]
            qh = jnp.where(msk, qp, jnp.zeros_like(qp))
            doh = jnp.where(msk, dop, 0.0).astype(BF16)
            dv = _mm_tn(p, doh)
            dp = _mm_nt(doh, vp)
            ds = (p * (dp - jnp.sum(dp * p, axis=-1, keepdims=True)) * 0.125).astype(BF16)
            dq = _mm(ds, kp)
            dk = _mm_tn(ds, qh)
            if sub == 0:
                dq_pair, dk_pair, dv_pair = dq, dk, dv
            else:
                dq_pair = jnp.where(lane < 64, dq_pair, dq)
                dk_pair, dv_pair = dk_pair + dk, dv_pair + dv
        dqs.append(dq_pair)
        dks.append(dk_pair)
        dvs.append(dv_pair)
    return jnp.concatenate(dqs, axis=1), jnp.concatenate(dks + dvs, axis=1)


def _mix_core(tok, gate, qm, kv, wout, h_in, g, b):
    mem_out, ps = _mem_attn(qm, kv)
    cat = jnp.concatenate([tok, mem_out], axis=1)
    sg = jax.nn.sigmoid(gate)
    sl = gate * sg
    y = cat * sl
    r = ALPHA * h_in + _mm(y, wout)
    mu = jnp.mean(r, axis=-1, keepdims=True)
    xc = r - mu
    rstd = lax.rsqrt(jnp.mean(xc * xc, axis=-1, keepdims=True) + NORM_EPS)
    xh = xc * rstd
    return xh * g + b, (ps, cat, sg, sl, y, xh, rstd)


def _mix_fwd(tok, gate, qm, kv, wout, h_in, g, b):
    s = tok.shape[0]
    tb = min(TB_MIX, s)

    def body(tok_ref, gate_ref, qm_ref, kv_ref, w_ref, h_ref, g_ref, b_ref, o_ref):
        o_ref[...], _ = _mix_core(tok_ref[...], gate_ref[...], qm_ref[...], kv_ref[...], w_ref[...], h_ref[...],
                                  g_ref[...], b_ref[...])

    return pl.pallas_call(
        body, name="mix_fwd", grid=(s // tb,), out_shape=jax.ShapeDtypeStruct((s, 1024), F32),
        in_specs=[_rows(tb, TOK_W), _rows(tb, 1024), _rows(tb, MEM_W), _const((MEM_LEN, 512)), _const((1024, 1024)),
                  _rows(tb, 1024), _const((1, 1024)), _const((1, 1024))],
        out_specs=_rows(tb, 1024), compiler_params=_params(("parallel",)),
    )(tok, gate, qm, kv, wout, h_in, g, b)


def _mix_bwd(tok, gate, qm, kv, wout, h_in, g, b, up, from_loss, exchange=None):
    s = tok.shape[0]
    tb = min(TB_MIX, s)

    def body(tok_ref, gate_ref, qm_ref, kv_ref, w_ref, h_ref, g_ref, b_ref, up_ref,
             dres_ref, dtok_ref, dgate_ref, dqm_ref, dw_ref, dkv_ref, dg_ref, db_ref, loss_ref):
        @pl.when(pl.program_id(0) == 0)
        def _():
            for r in (dw_ref, dkv_ref, dg_ref, db_ref, loss_ref):
                r[...] = jnp.zeros_like(r)

        gate, qm, kv, wout, g = gate_ref[...], qm_ref[...], kv_ref[...], w_ref[...], g_ref[...]
        h_out, (ps, cat, sg, sl, y, xh, rstd) = _mix_core(tok_ref[...], gate, qm, kv, wout, h_ref[...], g, b_ref[...])
        if from_loss:
            diff = h_out - up_ref[...]
            loss_ref[...] += 0.5 * jnp.sum(jnp.mean(diff * diff, axis=-1, keepdims=True), axis=0, keepdims=True)
            dh = diff * (1.0 / D_MODEL)
        else:
            dh = up_ref[...]
        dg_ref[...] += jnp.sum(dh * xh, axis=0, keepdims=True)
        db_ref[...] += jnp.sum(dh, axis=0, keepdims=True)
        dxh = dh * g
        dr = rstd * (dxh - jnp.mean(dxh, axis=-1, keepdims=True) - xh * jnp.mean(dxh * xh, axis=-1, keepdims=True))
        dres_ref[...] = ALPHA * dr
        drb = dr.astype(BF16)
        dy = _mm_nt(drb, wout)
        dw_ref[...] += _mm_tn(y, drb)
        dcat = dy * sl
        dgate_ref[...] = dy * cat * (sg * (1.0 + gate * (1.0 - sg)))
        dtok_ref[...] = dcat[:, :TOK_W]
        dqm, dkv = _mem_attn_bwd(dcat[:, TOK_W:], qm, kv, ps)
        dqm_ref[...] = dqm
        dkv_ref[...] += dkv

    outs = (jax.ShapeDtypeStruct((s, 1024), F32), jax.ShapeDtypeStruct((s, TOK_W), F32),
            jax.ShapeDtypeStruct((s, 1024), F32), jax.ShapeDtypeStruct((s, MEM_W), F32),
            jax.ShapeDtypeStruct((1024, 1024), F32), jax.ShapeDtypeStruct((MEM_LEN, 512), F32),
            jax.ShapeDtypeStruct((1, 1024), F32), jax.ShapeDtypeStruct((1, 1024), F32),
            jax.ShapeDtypeStruct((1, 1), F32))
    return _run(
        body, name="mix_bwd_loss" if from_loss else "mix_bwd", grid=(s // tb,), out_shape=outs,
        in_specs=[_rows(tb, TOK_W), _rows(tb, 1024), _rows(tb, MEM_W), _const((MEM_LEN, 512)), _const((1024, 1024)),
                  _rows(tb, 1024), _const((1, 1024)), _const((1, 1024)), _rows(tb, 1024)],
        out_specs=(_rows(tb, 1024), _rows(tb, TOK_W), _rows(tb, 1024), _rows(tb, MEM_W), _const((1024, 1024)),
                   _const((MEM_LEN, 512)), _const((1, 1024)), _const((1, 1024)), _const((1, 1))),
        args=(tok, gate, qm, kv, wout, h_in, g, b, up), sem=("arbitrary",), exchange=exchange)


def _shift_down(u, tail, k):
    if k == 0:
        return u
    r = pltpu.roll(u, k, 0)
    row8 = _iota((8, u.shape[1]), 0)
    head = jnp.where(row8 < k, pltpu.roll(tail, k, 0), r[:8])
    return jnp.concatenate([head, r[8:]], axis=0)


def _shift_up(d, head, k):
    if k == 0:
        return d
    n = d.shape[0]
    r = pltpu.roll(d, n - k, 0)
    row8 = _iota((8, d.shape[1]), 0)
    last = jnp.where(row8 >= 8 - k, pltpu.roll(head, 8 - k, 0), r[n - 8:])
    return jnp.concatenate([r[:n - 8], last], axis=0)


def _scan_down(a, b):
    n = a.shape[0]
    row = _iota(a.shape, 0)
    s = 1
    while s < n:
        ok = row >= s
        a_s = jnp.where(ok, pltpu.roll(a, s, 0), 1.0)
        b_s = jnp.where(ok, pltpu.roll(b, s, 0), 0.0)
        b = a * b_s + b
        a = a * a_s
        s *= 2
    return a, b


def _scan_up(a, b):
    n = a.shape[0]
    row = _iota(a.shape, 0)
    s = 1
    while s < n:
        ok = row < n - s
        a_s = jnp.where(ok, pltpu.roll(a, n - s, 0), 1.0)
        b_s = jnp.where(ok, pltpu.roll(b, n - s, 0), 0.0)
        b = a * b_s + b
        a = a * a_s
        s *= 2
    return a, b


def _neg_expm1(x):
    poly = -x * (1.0 + x * (0.5 + x * (1.0 / 6.0 + x * (1.0 / 24.0 + x * (1.0 / 120.0)))))
    return jnp.where(x > -0.1, poly, 1.0 - jnp.exp(x))


def _softplus(x):
    return jnp.maximum(x, 0.0) + jnp.log(1.0 + jnp.exp(-jnp.abs(x)))


def _lru_gates(u, tail, cw, cb, wr, br, wi, bi, lam):
    us = [_shift_down(u, tail, k) for k in range(4)]
    xc = cb + us[3] * cw[0:1] + us[2] * cw[1:2] + us[1] * cw[2:3] + us[0] * cw[3:4]
    xb = xc.astype(BF16)
    pre_r = jnp.concatenate([_mm(xb[:, 256 * g:256 * (g + 1)], wr[g]) for g in range(3)], axis=1) + br
    pre_i = jnp.concatenate([_mm(xb[:, 256 * g:256 * (g + 1)], wi[g]) for g in range(3)], axis=1) + bi
    rg, ig = jax.nn.sigmoid(pre_r), jax.nn.sigmoid(pre_i)
    clam = -LRU_C * _softplus(-lam)
    la = clam * rg
    a = jnp.exp(la)
    mm = jnp.sqrt(_neg_expm1(2.0 * la))
    return us, xc, xb, rg, ig, clam, la, a, mm


def _lru_fwd(h, win, cw, cb, wr, br, wi, bi, lam):
    s = h.shape[0]
    tb = min(TB_LRU, s)

    def body(h_ref, win_ref, cw_ref, cb_ref, wr_ref, br_ref, wi_ref, bi_ref, lam_ref,
             u_ref, gate_ref, qm_ref, hs_ref, tail_sc, carry_sc):
        @pl.when(pl.program_id(0) == 0)
        def _():
            tail_sc[...] = jnp.zeros_like(tail_sc)
            carry_sc[...] = jnp.zeros_like(carry_sc)

        z = _mm(h_ref[...], win_ref[...])
        u = z[:, :TOK_W]
        u_ref[...] = u
        gate_ref[...] = z[:, TOK_W:TOK_W + 1024]
        qm_ref[...] = z[:, TOK_W + 1024:].astype(BF16)
        _, xc, _, _, ig, _, _, a, mm = _lru_gates(u, tail_sc[...], cw_ref[...], cb_ref[...], wr_ref[...], br_ref[...],
                                                 wi_ref[...], bi_ref[...], lam_ref[...])
        big_a, big_b = _scan_down(a, mm * (ig * xc))
        hs = big_a * carry_sc[0:1, :] + big_b
        hs_ref[...] = hs
        tail_sc[...] = u[tb - 8:, :]
        carry_sc[...] = jnp.broadcast_to(hs[tb - 1:tb, :], carry_sc.shape)

    outs = (jax.ShapeDtypeStruct((s, TOK_W), F32), jax.ShapeDtypeStruct((s, 1024), F32),
            jax.ShapeDtypeStruct((s, MEM_W), BF16), jax.ShapeDtypeStruct((s, TOK_W), F32))
    return pl.pallas_call(
        body, name="lru_fwd", grid=(s // tb,), out_shape=outs,
        in_specs=[_rows(tb, 1024), _const((1024, 2048)), _const((4, TOK_W)), _const((1, TOK_W)),
                  _const((3, 256, 256)), _const((1, TOK_W)), _const((3, 256, 256)), _const((1, TOK_W)),
                  _const((1, TOK_W))],
        out_specs=(_rows(tb, TOK_W), _rows(tb, 1024), _rows(tb, MEM_W), _rows(tb, TOK_W)),
        scratch_shapes=[pltpu.VMEM((8, TOK_W), F32), pltpu.VMEM((8, TOK_W), F32)],
        compiler_params=_params(),
    )(h, win, cw, cb, wr, br, wi, bi, lam)


def _lru_bwd(dhs, dgate, dqm, dres, h, u, hs, win, cw, cb, wr, br, wi, bi, lam):
    s = h.shape[0]
    tb = min(TB_LRU, s)
    nb = s // tb

    def rev(w):
        return pl.BlockSpec((tb, w), lambda i: (nb - 1 - i, 0))

    def prev_tail(w):
        return pl.BlockSpec((8, w), lambda i: (jnp.maximum((nb - 1 - i) * (tb // 8) - 1, 0), 0))

    def body(dhs_ref, dgate_ref, dqm_ref, dres_ref, h_ref, u_ref, hs_ref, ut_ref, hst_ref, win_ref, cw_ref, cb_ref,
             wr_ref, br_ref, wi_ref, bi_ref, lam_ref,
             dh_ref, dwin_ref, dcw_ref, dcb_ref, dwr_ref, dbr_ref, dwi_ref, dbi_ref, dlam_ref, ecar_sc, dxc_sc):
        i = pl.program_id(0)

        @pl.when(i == 0)
        def _():
            for r in (dwin_ref, dcw_ref, dcb_ref, dwr_ref, dbr_ref, dwi_ref, dbi_ref, dlam_ref, ecar_sc, dxc_sc):
                r[...] = jnp.zeros_like(r)

        first = (i == nb - 1)
        u = u_ref[...]
        utail = jnp.where(first, 0.0, ut_ref[...])
        hstail = jnp.where(first, 0.0, hst_ref[...])
        cw, wr, wi, lam = cw_ref[...], wr_ref[...], wi_ref[...], lam_ref[...]
        us, xc, xb, rg, ig, clam, la, a, mm = _lru_gates(u, utail, cw, cb_ref[...], wr, br_ref[...], wi, bi_ref[...], lam)
        row = _iota(a.shape, 0)
        a_next = jnp.where(row < tb - 1, pltpu.roll(a, tb - 1, 0), 1.0)
        big_a, big_b = _scan_up(a_next, dhs_ref[...])
        e = big_a * ecar_sc[0:1, :] + big_b
        ecar_sc[...] = jnp.broadcast_to(a[0:1, :] * e[0:1, :], ecar_sc.shape)
        hs_prev = _shift_down(hs_ref[...], hstail, 1)
        da = e * hs_prev
        ix = ig * xc
        dmm = e * ix
        dix = e * mm
        dla = da * a - dmm * (a * a) / mm
        dlam_ref[...] += jnp.sum(dla * rg, axis=0, keepdims=True)
        dpr = (dla * clam) * rg * (1.0 - rg)
        dpi = (dix * xc) * ig * (1.0 - ig)
        dbr_ref[...] += jnp.sum(dpr, axis=0, keepdims=True)
        dbi_ref[...] += jnp.sum(dpi, axis=0, keepdims=True)
        dprb, dpib = dpr.astype(BF16), dpi.astype(BF16)
        dxc_g = []
        for g in range(3):
            sl = slice(256 * g, 256 * (g + 1))
            dwr_ref[g] += _mm_tn(xb[:, sl], dprb[:, sl])
            dwi_ref[g] += _mm_tn(xb[:, sl], dpib[:, sl])
            dxc_g.append(_mm_nt(dprb[:, sl], wr[g]) + _mm_nt(dpib[:, sl], wi[g]))
        dxc = dix * ig + jnp.concatenate(dxc_g, axis=1)
        dcb_ref[...] += jnp.sum(dxc, axis=0, keepdims=True)
        dcw_ref[...] += jnp.concatenate([jnp.sum(dxc * us[3 - tap], axis=0, keepdims=True) for tap in range(4)], axis=0)
        head = dxc_sc[...]
        du = dxc * cw[3:4]
        for k in range(1, 4):
            du = du + _shift_up(dxc, head, k) * cw[3 - k:4 - k]
        dxc_sc[...] = dxc[:8, :]
        dz = jnp.concatenate([du, dgate_ref[...], dqm_ref[...]], axis=1).astype(BF16)
        dh_ref[...] = _mm_nt(dz, win_ref[...]) + dres_ref[...]
        dwin_ref[...] += _mm_tn(h_ref[...], dz)

        @pl.when(i == nb - 1)
        def _():
            dlam_ref[...] = dlam_ref[...] * (LRU_C * jax.nn.sigmoid(-lam))

    outs = (jax.ShapeDtypeStruct((s, 1024), F32), jax.ShapeDtypeStruct((1024, 2048), F32),
            jax.ShapeDtypeStruct((4, TOK_W), F32), jax.ShapeDtypeStruct((1, TOK_W), F32),
            jax.ShapeDtypeStruct((3, 256, 256), F32), jax.ShapeDtypeStruct((1, TOK_W), F32),
            jax.ShapeDtypeStruct((3, 256, 256), F32), jax.ShapeDtypeStruct((1, TOK_W), F32),
            jax.ShapeDtypeStruct((1, TOK_W), F32))
    return pl.pallas_call(
        body, name="lru_bwd", grid=(nb,), out_shape=outs,
        in_specs=[rev(TOK_W), rev(1024), rev(MEM_W), rev(1024), rev(1024), rev(TOK_W), rev(TOK_W),
                  prev_tail(TOK_W), prev_tail(TOK_W),
                  _const((1024, 2048)), _const((4, TOK_W)), _const((1, TOK_W)), _const((3, 256, 256)),
                  _const((1, TOK_W)), _const((3, 256, 256)), _const((1, TOK_W)), _const((1, TOK_W))],
        out_specs=(rev(1024), _const((1024, 2048)), _const((4, TOK_W)), _const((1, TOK_W)), _const((3, 256, 256)),
                   _const((1, TOK_W)), _const((3, 256, 256)), _const((1, TOK_W)), _const((1, TOK_W))),
        scratch_shapes=[pltpu.VMEM((8, TOK_W), F32), pltpu.VMEM((8, TOK_W), F32)],
        compiler_params=_params(),
    )(dhs, dgate, dqm, dres, h, u, hs, u, hs, win, cw, cb, wr, br, wi, bi, lam)


def _adamw(name, w, g, m, v):
    rows, cols = w.shape
    tb = 256 if rows % 256 == 0 else rows

    def body(w_ref, g_ref, m_ref, v_ref, d_ref, nm_ref, nv_ref):
        g = g_ref[...]
        nm = ADAM_B1 * m_ref[...] + (1.0 - ADAM_B1) * g
        nv = ADAM_B2 * v_ref[...] + (1.0 - ADAM_B2) * (g * g)
        m_hat = nm / (1.0 - ADAM_B1 ** ADAM_STEP)
        v_hat = nv / (1.0 - ADAM_B2 ** ADAM_STEP)
        d_ref[...] = -ADAM_LR * (m_hat / (jnp.sqrt(v_hat) + ADAM_EPS) + ADAM_WD * w_ref[...])
        nm_ref[...] = nm
        nv_ref[...] = nv

    shp = jax.ShapeDtypeStruct((rows, cols), F32)
    return pl.pallas_call(
        body, name="adamw_" + name, grid=(rows // tb,), out_shape=(shp, shp, shp),
        in_specs=[_rows(tb, cols)] * 4, out_specs=(_rows(tb, cols),) * 3,
        compiler_params=_params(("parallel",)),
    )(w, g, m, v)


def _row_block(rows, cap=2048):
    return max(t for t in range(8, cap + 1, 8) if rows % t == 0)


def _add2(a, b):
    rows = a.shape[0]
    tb = _row_block(rows)

    def body(a_ref, b_ref, o_ref):
        o_ref[...] = a_ref[...] + b_ref[...]

    return pl.pallas_call(
        body, name="add_sibling", grid=(rows // tb,), out_shape=jax.ShapeDtypeStruct(a.shape, F32),
        in_specs=[_rows(tb, 128)] * 2, out_specs=_rows(tb, 128), compiler_params=_params(("parallel",)),
    )(a, b)


def _sum_slots(landed, own, rows):
    tb = _row_block(rows, 1024)

    def body(l_ref, o_ref, out_ref):
        t = 2 * lax.axis_index("x") + lax.axis_index("y")
        r = [jnp.where(t == s, o_ref[s], l_ref[s].astype(F32)) for s in range(4)]
        out_ref[...] = ((r[0] + r[1]) + r[2]) + r[3]

    return pl.pallas_call(
        body, name="sum_chips", grid=(rows // tb,), out_shape=jax.ShapeDtypeStruct((rows, 128), F32),
        in_specs=[pl.BlockSpec((4, tb, 128), lambda i: (0, i, 0))] * 2, out_specs=_rows(tb, 128),
        compiler_params=_params(("parallel",)),
    )(landed, own)


_ANY = pl.BlockSpec(memory_space=pl.ANY)


def _place():
    x, y, c = lax.axis_index("x"), lax.axis_index("y"), lax.axis_index("c")
    return x, y, c, [(1 - x, y), (x, 1 - y), (1 - x, 1 - y)]


def _remote(src, dst, ssem, rsem, to):
    return pltpu.make_async_remote_copy(src_ref=src, dst_ref=dst, send_sem=ssem, recv_sem=rsem, device_id=to,
                                        device_id_type=MESH)


class _Exchange:
    def __init__(self, ins, out_shape, sems, start, finish):
        self.ins, self.out_shape, self.sems, self.start, self.finish = ins, out_shape, sems, start, finish


def _run(body, *, name, grid, in_specs, out_specs, out_shape, args, scratch=(), sem, exchange=None):
    if exchange is None:
        return pl.pallas_call(body, name=name, grid=grid, out_shape=tuple(out_shape), in_specs=list(in_specs),
                              out_specs=tuple(out_specs), scratch_shapes=list(scratch),
                              compiler_params=_params(sem))(*args)
    n_in, n_out, n_sc = len(args), len(out_shape), len(scratch)
    k_in, k_out = len(exchange.ins), len(exchange.out_shape)

    def fused(*refs):
        ins, refs = refs[:n_in], refs[n_in:]
        xin, refs = refs[:k_in], refs[k_in:]
        outs, refs = refs[:n_out], refs[n_out:]
        xout, refs = refs[:k_out], refs[k_out:]
        sc, xsem = refs[:n_sc], refs[n_sc:]
        first = pl.program_id(0) == 0
        last = pl.program_id(0) == grid[0] - 1
        for a in range(1, len(grid)):
            first = first & (pl.program_id(a) == 0)
            last = last & (pl.program_id(a) == grid[a] - 1)

        @pl.when(first)
        def _():
            exchange.start(xin, xout, xsem)

        body(*ins, *outs, *sc)

        @pl.when(last)
        def _():
            exchange.finish(xin, xout, xsem)

    return pl.pallas_call(
        fused, name=name, grid=grid, out_shape=(*out_shape, *exchange.out_shape),
        in_specs=[*in_specs, *[_ANY] * k_in], out_specs=(*out_specs, *[_ANY] * k_out),
        scratch_shapes=[*scratch, *exchange.sems],
        compiler_params=_params(("arbitrary",) * len(grid)),
    )(*args, *exchange.ins)


def _run_exchange(exchange, name):
    def body(*refs):
        k_in, k_out = len(exchange.ins), len(exchange.out_shape)
        xin, xout, xsem = refs[:k_in], refs[k_in:k_in + k_out], refs[k_in + k_out:]
        exchange.start(xin, xout, xsem)
        exchange.finish(xin, xout, xsem)

    return pl.pallas_call(
        body, name=name, out_shape=tuple(exchange.out_shape), in_specs=[_ANY] * len(exchange.ins),
        out_specs=tuple([_ANY] * len(exchange.out_shape)), scratch_shapes=list(exchange.sems),
    )(*exchange.ins)


def _gather_shards(wsh):
    _, hh, _ = wsh.shape

    def first_hop(w_ref, out_ref, ssems, rsems):
        x, y, c, chips = _place()
        t = 2 * x + y
        return [_remote(w_ref.at[c], out_ref.at[t, c], ssems.at[j], rsems.at[j], (cx, cy, c))
                for j, (cx, cy) in enumerate(chips)]

    def start(xin, xout, xsem):
        for cp in first_hop(xin[0], xout[0], *xsem):
            cp.start()

    def finish(xin, xout, xsem):
        out_ref, (ssems, rsems) = xout[0], xsem
        first = first_hop(xin[0], out_ref, *xsem)
        x, y, c, chips = _place()
        passed = []
        for j, (cx, cy) in enumerate(chips):
            got = out_ref.at[2 * cx + cy, c]
            _remote(got, got, ssems.at[j], rsems.at[j], (cx, cy, c)).wait_recv()
            cp = _remote(got, got, ssems.at[3 + j], rsems.at[3 + j], (x, y, 1 - c))
            cp.start()
            passed.append(cp)
        for j, (cx, cy) in enumerate(chips):
            got = out_ref.at[2 * cx + cy, 1 - c]
            _remote(got, got, ssems.at[3 + j], rsems.at[3 + j], (x, y, 1 - c)).wait_recv()
        for cp in first + passed:
            cp.wait_send()

    return _Exchange([wsh], [jax.ShapeDtypeStruct((4, 2, hh, 128), wsh.dtype)],
                     [pltpu.SemaphoreType.DMA((6,)), pltpu.SemaphoreType.DMA((6,))], start, finish)


def _gathered(landed, own):
    t = 2 * lax.axis_index("x") + lax.axis_index("y")
    return lax.dynamic_update_slice(landed, own[None], (t, 0, 0, 0))


def _swap_sibling(v):
    def copy(xin, xout, xsem):
        x, y, c, _ = _place()
        return _remote(xin[0], xout[0], xsem[0], xsem[1], (x, y, 1 - c))

    return _Exchange([v], [jax.ShapeDtypeStruct(v.shape, v.dtype)],
                     [pltpu.SemaphoreType.DMA, pltpu.SemaphoreType.DMA],
                     lambda *a: copy(*a).start(), lambda *a: copy(*a).wait())


def _scatter_chips(parts):
    n = len(parts)

    def copies(xin, xout, ssems, rsems):
        x, y, c, chips = _place()
        t = 2 * x + y
        return [_remote(xin[k].at[2 * cx + cy], xout[k].at[t], ssems.at[n * j + k], rsems.at[n * j + k], (cx, cy, c))
                for j, (cx, cy) in enumerate(chips) for k in range(n)]

    def start(xin, xout, xsem):
        for cp in copies(xin, xout, *xsem):
            cp.start()

    def finish(xin, xout, xsem):
        ssems, rsems = xsem
        x, y, c, chips = _place()
        for j, (cx, cy) in enumerate(chips):
            for k in range(n):
                got = xout[k].at[2 * cx + cy]
                _remote(got, got, ssems.at[n * j + k], rsems.at[n * j + k], (cx, cy, c)).wait_recv()
        for cp in copies(xin, xout, *xsem):
            cp.wait_send()

    return _Exchange(parts, [jax.ShapeDtypeStruct(a.shape, a.dtype) for a in parts],
                     [pltpu.SemaphoreType.DMA((3 * n,)), pltpu.SemaphoreType.DMA((3 * n,))], start, finish)


def _share_reduced(piece, eighth):
    def copies(t_ref, mine_r, sib_ref, rall_ref, ssems, rsems, lsem):
        x, y, c, _ = _place()
        me = 4 * x + 2 * y + c
        loc = pltpu.make_async_copy(mine_r, rall_ref.at[me], lsem)
        sends = [_remote(t_ref, sib_ref, ssems.at[0], rsems.at[0], (x, y, 1 - c))]
        peers = []
        for mask in range(1, 8):
            px = 1 - x if mask & 4 else x
            py = 1 - y if mask & 2 else y
            pc = 1 - c if mask & 1 else c
            peers.append((mask, px, py, pc))
            sends.append(_remote(mine_r, rall_ref.at[me], ssems.at[mask], rsems.at[mask], (px, py, pc)))
        return loc, sends, peers

    def start(xin, xout, xsem):
        loc, sends, _ = copies(*xin, *xout, *xsem)
        for cp in [loc] + sends:
            cp.start()

    def finish(xin, xout, xsem):
        (sib_ref, rall_ref), (ssems, rsems, _) = xout, xsem
        loc, sends, peers = copies(*xin, *xout, *xsem)
        x, y, c, _ = _place()
        _remote(sib_ref, sib_ref, ssems.at[0], rsems.at[0], (x, y, 1 - c)).wait_recv()
        for mask, px, py, pc in peers:
            got = rall_ref.at[4 * px + 2 * py + pc]
            _remote(got, got, ssems.at[mask], rsems.at[mask], (px, py, pc)).wait_recv()
        for cp in sends:
            cp.wait_send()
        loc.wait()

    return _Exchange([piece, eighth],
                     [jax.ShapeDtypeStruct(piece.shape, F32), jax.ShapeDtypeStruct((8, *eighth.shape), F32)],
                     [pltpu.SemaphoreType.DMA((8,)), pltpu.SemaphoreType.DMA((8,)), pltpu.SemaphoreType.DMA],
                     start, finish)


def _ceil_to(n, m):
    return -(-n // m) * m


def _pack_bf16(parts):
    flat = jnp.concatenate(parts)
    hw = _ceil_to(_ceil_to(flat.shape[0], 128) // 128, 32) // 2
    return jnp.pad(flat, (0, 2 * hw * 128 - flat.shape[0])).reshape(2, hw, 128), [p.shape[0] for p in parts]


def _segments(wall, sizes):
    wall = wall.reshape(4, -1)
    offs = [0]
    for n in sizes:
        offs.append(offs[-1] + n)
    return [wall[:, offs[i]:offs[i + 1]] for i in range(len(sizes))]


class _GradReduce:
    def __init__(self, sharded, replicated, c_idx, wire_bf16=False):
        self.c_idx, self.wire_bf16 = c_idx, wire_bf16
        self.sharded = [(n, g.shape[1]) for n, g in sharded]
        self.replicated = [(n, g.shape[0]) for n, g in replicated]
        sh = jnp.concatenate([g for _, g in sharded], axis=1)
        n_sh = sh.shape[1]
        self.hs = _ceil_to(_ceil_to(n_sh, 128) // 128, 256) // 2
        sh = jnp.pad(sh, ((0, 0), (0, 2 * self.hs * 128 - n_sh))).reshape(4, 2, self.hs, 128)
        rp = jnp.concatenate([g for _, g in replicated])
        self.rr = _ceil_to(_ceil_to(rp.shape[0], 128) // 128, 64) // 8
        rp = jnp.pad(rp, (0, 8 * self.rr * 128 - rp.shape[0])).reshape(4, 2, self.rr, 128)
        gbuf = jnp.concatenate([sh, rp], axis=2)
        self.hh = self.hs + self.rr
        self.mine = lax.dynamic_index_in_dim(gbuf, c_idx, axis=1, keepdims=False).reshape(4 * self.hh, 128)
        self.other = lax.dynamic_index_in_dim(gbuf, 1 - c_idx, axis=1, keepdims=False).reshape(4 * self.hh, 128)

    def swap(self):
        return _swap_sibling(self.other)

    def swapped(self, got):
        self.chip_sum = _add2(self.mine, got).reshape(4, self.hh, 128)

    def scatter(self):
        self.own_r = self.chip_sum[:, self.hs:]
        if self.wire_bf16:
            return _scatter_chips([self.chip_sum[:, :self.hs].astype(BF16), self.own_r])
        return _scatter_chips([self.chip_sum])

    def scattered(self, landed, landed_r=None):
        if landed_r is None:
            landed_r = landed[:, self.hs:]
        self.piece = _sum_slots(landed, self.chip_sum, self.hs)
        self.eighth = _sum_slots(landed_r, self.own_r, self.rr)

    def share(self):
        return _share_reduced(self.piece, self.eighth)

    def shared(self, sibling, rall):
        mine, sib = self.piece.reshape(-1), sibling.reshape(-1)
        self.shard = jnp.where(self.c_idx == 0, jnp.concatenate([mine, sib]), jnp.concatenate([sib, mine]))
        self.rall = rall

    def reduced(self):
        out = {}
        for group, flat in ((self.sharded, self.shard), (self.replicated, self.rall.reshape(-1))):
            off = 0
            for name, n in group:
                out[name] = flat[off:off + n]
                off += n
        return out


def _col_shards(w2d):
    rows, cols = w2d.shape
    return w2d.reshape(rows, 4, cols // 4).transpose(1, 0, 2).reshape(4, rows * (cols // 4))


def _from_col_shards(flat, rows):
    w = flat.shape[1] // rows
    return flat.reshape(4, rows, w).transpose(1, 0, 2).reshape(rows, 4 * w)


def _block_diag4(w):
    eye = jnp.eye(4, dtype=w.dtype)
    return jnp.einsum("gaij,ab->gaibj", w.reshape(3, 4, 64, 64), eye).reshape(3, 256, 256)


def _diag_blocks4(w):
    w5 = w.reshape(3, 4, 64, 4, 64)
    return jnp.stack([w5[:, a, :, a, :] for a in range(4)], axis=1).reshape(12, 64, 64)


def kernel(x, mem, positions, mla_w_in, mla_q_norm, mla_w_uq, mla_kv_norm, mla_w_ukv, lru_w_in, lru_conv_w, lru_conv_b, lru_w_rgate, lru_b_rgate, lru_w_igate, lru_b_igate, lru_lambda, w_mem_kv, w_out, ln_g, ln_b, loss_target, m_mla_w_in, m_mla_q_norm, m_mla_w_uq, m_mla_kv_norm, m_mla_w_ukv, m_lru_w_in, m_lru_conv_w, m_lru_conv_b, m_lru_w_rgate, m_lru_b_rgate, m_lru_w_igate, m_lru_b_igate, m_lru_lambda, m_w_mem_kv, m_w_out, m_ln_g, m_ln_b, v_mla_w_in, v_mla_q_norm, v_mla_w_uq, v_mla_kv_norm, v_mla_w_ukv, v_lru_w_in, v_lru_conv_w, v_lru_conv_b, v_lru_w_rgate, v_lru_b_rgate, v_lru_w_igate, v_lru_b_igate, v_lru_lambda, v_w_mem_kv, v_w_out, v_ln_g, v_ln_b):
    s = x.shape[1]
    c_idx = lax.axis_index("c")
    x2, mem2, tgt2 = x[0], mem[0], loss_target[0]

    early = [mla_w_in[0], mla_w_uq[0], mla_w_ukv[0], w_mem_kv, w_out[0]]
    flat, early_sizes = _pack_bf16([p.reshape(-1).astype(BF16) for p in early])
    seg = _segments(_gathered(_run_exchange(_gather_shards(flat), "gather_weights")[0], flat), early_sizes)
    win0 = _from_col_shards(seg[0], 1024)
    wuq = _from_col_shards(seg[1], Q_LORA)
    wukv = _from_col_shards(seg[2], KV_LORA)
    wmem = seg[3].reshape(4, 2, 256, 512).transpose(1, 0, 2, 3).reshape(2, 1024, 512)
    wout0 = seg[4].reshape(1024, 1024)

    small = jnp.concatenate([lru_conv_w[0].reshape(-1), lru_conv_b[0], lru_b_rgate[0], lru_b_igate[0], lru_lambda[0]])
    late = [lru_w_in[0].reshape(-1).astype(BF16), w_out[1].reshape(-1).astype(BF16),
            lax.bitcast_convert_type(small, BF16).reshape(-1)]
    flat_late, late_sizes = _pack_bf16(late)

    def late_weights(landed):
        seg = _segments(_gathered(landed[0], flat_late), late_sizes)
        small_all = lax.bitcast_convert_type(seg[2].reshape(4, -1, 2), F32)
        cw = small_all[:, :768].reshape(4, 4, 192).transpose(1, 0, 2).reshape(4, TOK_W)
        cb, br, bi, lam = (small_all[:, 768 + 192 * k:960 + 192 * k].reshape(1, TOK_W) for k in range(4))
        return _from_col_shards(seg[0], 1024), seg[1].reshape(1024, 1024), cw, cb, br, bi, lam

    def reduce_late(g):
        return _GradReduce(
            [("lru_w_in", _col_shards(g["lru_w_in"])), ("lru_conv_w", _col_shards(g["lru_conv_w"])),
             ("lru_conv_b", _col_shards(g["lru_conv_b"])), ("lru_b_rgate", _col_shards(g["lru_b_rgate"])),
             ("lru_b_igate", _col_shards(g["lru_b_igate"])), ("lru_lambda", _col_shards(g["lru_lambda"])),
             ("w_mem_kv1", g["w_mem_kv1"].reshape(4, -1)), ("w_out1", g["w_out1"].reshape(4, -1))],
            [("lru_w_rgate", g["lru_w_rgate"].reshape(-1)), ("lru_w_igate", g["lru_w_igate"].reshape(-1)),
             ("ln_g1", g["ln_g1"].reshape(-1)), ("ln_b1", g["ln_b1"].reshape(-1))], c_idx)

    g0, late_red = _local_step(
        x2, mem2, positions.reshape(s, 1), tgt2, win0, wuq, wukv, wmem, wout0, mla_q_norm, mla_kv_norm,
        lru_w_rgate[0], lru_w_igate[0], ln_g, ln_b, late_weights, _gather_shards(flat_late), reduce_late)

    early_red = _GradReduce(
        [("mla_w_in", _col_shards(g0["mla_w_in"])), ("mla_w_uq", _col_shards(g0["mla_w_uq"])),
         ("mla_w_ukv", _col_shards(g0["mla_w_ukv"])), ("w_mem_kv0", g0["w_mem_kv0"].reshape(4, -1)),
         ("w_out0", g0["w_out0"].reshape(4, -1))],
        [("mla_q_norm", g0["mla_q_norm"].reshape(-1)), ("mla_kv_norm", g0["mla_kv_norm"].reshape(-1)),
         ("ln_g0", g0["ln_g0"].reshape(-1)), ("ln_b0", g0["ln_b0"].reshape(-1)), ("loss", g0["loss"].reshape(-1))],
        c_idx, wire_bf16=True)
    early_red.swapped(*_run_exchange(early_red.swap(), "swap_sibling"))
    early_red.scattered(*_run_exchange(early_red.scatter(), "scatter_chips"))
    early_red.shared(*_run_exchange(early_red.share(), "share_reduced"))
    red = {**late_red.reduced(), **early_red.reduced()}
    red["w_mem_kv"] = jnp.concatenate([red["w_mem_kv0"], red["w_mem_kv1"]])
    red["w_out"] = jnp.concatenate([red["w_out0"], red["w_out1"]])
    red["ln_g"] = jnp.concatenate([red["ln_g0"], red["ln_g1"]])
    red["ln_b"] = jnp.concatenate([red["ln_b0"], red["ln_b1"]])

    weights = dict(mla_w_in=mla_w_in, mla_q_norm=mla_q_norm, mla_w_uq=mla_w_uq, mla_kv_norm=mla_kv_norm,
                   mla_w_ukv=mla_w_ukv, lru_w_in=lru_w_in, lru_conv_w=lru_conv_w, lru_conv_b=lru_conv_b,
                   lru_w_rgate=lru_w_rgate, lru_b_rgate=lru_b_rgate, lru_w_igate=lru_w_igate, lru_b_igate=lru_b_igate,
                   lru_lambda=lru_lambda, w_mem_kv=w_mem_kv, w_out=w_out, ln_g=ln_g, ln_b=ln_b)
    m_in = dict(mla_w_in=m_mla_w_in, mla_q_norm=m_mla_q_norm, mla_w_uq=m_mla_w_uq, mla_kv_norm=m_mla_kv_norm,
                mla_w_ukv=m_mla_w_ukv, lru_w_in=m_lru_w_in, lru_conv_w=m_lru_conv_w, lru_conv_b=m_lru_conv_b,
                lru_w_rgate=m_lru_w_rgate, lru_b_rgate=m_lru_b_rgate, lru_w_igate=m_lru_w_igate,
                lru_b_igate=m_lru_b_igate, lru_lambda=m_lru_lambda, w_mem_kv=m_w_mem_kv, w_out=m_w_out, ln_g=m_ln_g,
                ln_b=m_ln_b)
    v_in = dict(mla_w_in=v_mla_w_in, mla_q_norm=v_mla_q_norm, mla_w_uq=v_mla_w_uq, mla_kv_norm=v_mla_kv_norm,
                mla_w_ukv=v_mla_w_ukv, lru_w_in=v_lru_w_in, lru_conv_w=v_lru_conv_w, lru_conv_b=v_lru_conv_b,
                lru_w_rgate=v_lru_w_rgate, lru_b_rgate=v_lru_b_rgate, lru_w_igate=v_lru_w_igate,
                lru_b_igate=v_lru_b_igate, lru_lambda=v_lru_lambda, w_mem_kv=v_w_mem_kv, w_out=v_w_out, ln_g=v_ln_g,
                ln_b=v_ln_b)
    order = ["mla_w_in", "mla_q_norm", "mla_w_uq", "mla_kv_norm", "mla_w_ukv", "lru_w_in", "lru_conv_w", "lru_conv_b",
             "lru_w_rgate", "lru_b_rgate", "lru_w_igate", "lru_b_igate", "lru_lambda", "w_mem_kv", "w_out", "ln_g",
             "ln_b"]
    grads, deltas, new_m, new_v = {}, {}, {}, {}
    for name in order:
        shape = weights[name].shape
        two_d = (math.prod(shape[:-1]), shape[-1])
        g2 = red[name].reshape(two_d)
        d2, m2, v2 = _adamw(name, weights[name].reshape(two_d), g2, m_in[name].reshape(two_d),
                            v_in[name].reshape(two_d))
        grads[name], deltas[name] = g2.reshape(shape), d2.reshape(shape)
        new_m[name], new_v[name] = m2.reshape(shape), v2.reshape(shape)
    return (red["loss"][0], g0["x"][None], *[grads[n] for n in order], *[deltas[n] for n in order],
            *[new_m[n] for n in order], *[new_v[n] for n in order])


def _local_step(x2, mem2, pos_col, tgt2, win0, wuq, wukv, wmem, wout0, gq, gkv, w_rgate, w_igate, ln_g, ln_b,
                late_weights, gather_late=None, reduce_late=None):
    s = x2.shape[0]
    zpad = jnp.zeros((1024, 64), BF16)
    win0p = jnp.concatenate([win0[:, 672:1696], win0[:, 1696:1952], win0[:, 0:384], win0[:, 384:640],
                             zpad, win0[:, 640:672], zpad[:, :32]], axis=1)
    wuq_p = jnp.pad(wuq.reshape(Q_LORA, 12, 96), ((0, 0), (0, 0), (0, 32))).reshape(Q_LORA, QK_W)
    wukv3 = wukv.reshape(KV_LORA, 12, 128)
    wk_p = jnp.pad(wukv3[:, :, :64], ((0, 0), (0, 0), (0, 64))).reshape(KV_LORA, QK_W)
    wv = wukv3[:, :, 64:].reshape(KV_LORA, TOK_W)
    wr_bd = _block_diag4(w_rgate).astype(BF16)
    wi_bd = _block_diag4(w_igate).astype(BF16)
    half = 16
    inv_freq = ROPE_THETA ** (-jnp.arange(half, dtype=F32) / half)
    inv_lane = jnp.concatenate([jnp.zeros((64,), F32), inv_freq, inv_freq, jnp.zeros((32,), F32)]).reshape(1, HEAD_PAD)

    ctab, satab, sbtab = _rope_tables(pos_col, inv_lane, min(TB_PROJ, s))
    memkv = _mem_kv(mem2, wmem)
    gate0, qm0, cq, ckv, q_p, q_t, k_p, v_b, v_t = _mla_proj_fwd(x2, win0p, gq, gkv, wuq_p, wk_p, wv,
                                                                ctab, satab, sbtab)
    tok0, lse, *landed = _attn_fwd(q_p, k_p, v_t, exchange=gather_late)
    win1, wout1, cw, cb, br, bi, lam = late_weights(landed)
    g0, b0, g1, b1 = ln_g[0:1], ln_b[0:1], ln_g[1:2], ln_b[1:2]
    h1 = _mix_fwd(tok0, gate0, qm0, memkv[0], wout0, x2, g0, b0)
    u1, gate1, qm1, hs1 = _lru_fwd(h1, win1, cw, cb, wr_bd, br, wi_bd, bi, lam)

    dres1, dtok1, dgate1, dqm1, dwout1, dmemkv1, dg1, db1, loss = _mix_bwd(
        hs1, gate1, qm1, memkv[1], wout1, h1, g1, b1, tgt2, True)
    dh1, dwin1, dcw, dcb, dwr_bd, dbr, dwi_bd, dbi, dlam = _lru_bwd(
        dtok1, dgate1, dqm1, dres1, h1, u1, hs1, win1, cw, cb, wr_bd, br, wi_bd, bi, lam)
    late = {"lru_w_in": dwin1, "lru_conv_w": dcw, "lru_conv_b": dcb, "lru_b_rgate": dbr, "lru_b_igate": dbi,
            "lru_lambda": dlam, "w_mem_kv1": _mem_kv_bwd(mem2, dmemkv1), "w_out1": dwout1,
            "lru_w_rgate": _diag_blocks4(dwr_bd), "lru_w_igate": _diag_blocks4(dwi_bd), "ln_g1": dg1, "ln_b1": db1}
    red = reduce_late(late) if reduce_late is not None else None

    dres0, dtok0, dgate0, dqm0, dwout0, dmemkv0, dg0, db0, _, *got = _mix_bwd(
        tok0, gate0, qm0, memkv[0], wout0, x2, g0, b0, dh1, False, exchange=red.swap() if red else None)
    if red:
        red.swapped(*got)
    dob, dobt, stats = _attn_prep(tok0, dtok0, lse)
    dq_p, dk_t, dv_t, *got = _attn_bwd(q_p, q_t, k_p, v_b, dob, dobt, stats,
                                       exchange=red.scatter() if red else None)
    if red:
        red.scattered(*got)
    if red:
        red.shared(*_run_exchange(red.share(), "share_reduced"))
    dx, dwin0p, dwuq_p, dwk_p, dwv, dgq, dgkv = _mla_proj_bwd(
        x2, cq, ckv, dq_p, dk_t, dv_t, dgate0, dqm0, dres0, win0p, gq, gkv, wuq_p, wk_p, wv,
        ctab, satab, sbtab)

    dwin0 = jnp.concatenate([dwin0p[:, 1280:1664], dwin0p[:, 1664:1920], dwin0p[:, 1984:2016], dwin0p[:, 0:1024],
                             dwin0p[:, 1024:1280]], axis=1)
    dwuq = dwuq_p.reshape(Q_LORA, 12, 128)[:, :, :96].reshape(Q_LORA, 1152)
    dwukv = jnp.concatenate([dwk_p.reshape(KV_LORA, 12, 128)[:, :, :64], dwv.reshape(KV_LORA, 12, 64)],
                            axis=2).reshape(KV_LORA, 1536)
    early = {"x": dx, "loss": loss, "mla_w_in": dwin0, "mla_w_uq": dwuq, "mla_w_ukv": dwukv,
             "w_mem_kv0": _mem_kv_bwd(mem2, dmemkv0), "w_out0": dwout0, "mla_q_norm": dgq, "mla_kv_norm": dgkv,
             "ln_g0": dg0, "ln_b0": db0}
    return early, (red if red else late)
```

```python
import functools
import math

import jax
import jax.numpy as jnp
from jax import lax
from jax.experimental import pallas as pl
from jax.experimental.pallas import tpu as pltpu

F32, BF16 = jnp.float32, jnp.bfloat16
MESH = pl.DeviceIdType.MESH

D_MODEL = 1024
N_TOK_HEADS = 12
TOK_W = 768
MEM_W = 256
MEM_LEN = 256
Q_LORA, KV_LORA = 384, 256
HEAD_PAD = 128
QK_W = N_TOK_HEADS * HEAD_PAD
ATT_SCALE = 1.0 / math.sqrt(96.0)
ATT_SCALE_LOG2 = ATT_SCALE * math.log2(math.e)
ROPE_THETA = 10000.0
LRU_C = 8.0
ALPHA = 4.0 ** 0.25
NORM_EPS = 1e-6
ADAM_LR, ADAM_B1, ADAM_B2, ADAM_EPS, ADAM_WD, ADAM_STEP = 0.001, 0.9, 0.999, 1e-08, 0.01, 10

TB_PROJ = 512
TB_PROJ_BWD = 256
TB_MIX = 512
TB_LRU = 256
TQ_ATT = 512
TQ_ATT_FWD = 1024
TK_ATT = 1024
VMEM_LIMIT = 56 * 1024 * 1024


def _mm(a, b):
    return jnp.dot(a.astype(BF16), b.astype(BF16), preferred_element_type=F32)


def _mm_nt(a, b):
    return lax.dot_general(a.astype(BF16), b.astype(BF16), (((1,), (1,)), ((), ())), preferred_element_type=F32)


def _mm_tn(a, b):
    return lax.dot_general(a.astype(BF16), b.astype(BF16), (((0,), (0,)), ((), ())), preferred_element_type=F32)


def _rows(tb, w):
    return pl.BlockSpec((tb, w), lambda i: (i, 0))


def _const(shape):
    n = len(shape)
    return pl.BlockSpec(shape, lambda i: (0,) * n)


def _params(sem=("arbitrary",)):
    return pltpu.CompilerParams(dimension_semantics=sem, vmem_limit_bytes=VMEM_LIMIT)


def _iota(shape, dim):
    return lax.broadcasted_iota(jnp.int32, shape, dim)


def _rope_tables(pos_col, inv_lane, tb):
    s = pos_col.shape[0]

    def body(pos_ref, inv_ref, c_ref, sa_ref, sb_ref):
        ang = pos_ref[...].astype(F32) * inv_ref[...]
        lane = _iota(ang.shape, 1)
        cs, sn = jnp.cos(ang), jnp.sin(ang)
        c_ref[...] = jnp.where(lane < 64, 1.0, jnp.where(lane < 96, cs, 0.0))
        sa_ref[...] = jnp.where((lane >= 64) & (lane < 80), -sn, 0.0)
        sb_ref[...] = jnp.where((lane >= 80) & (lane < 96), sn, 0.0)

    shp = jax.ShapeDtypeStruct((s, HEAD_PAD), F32)
    return pl.pallas_call(
        body, name="rope_tables", grid=(s // tb,), out_shape=(shp, shp, shp),
        in_specs=[_rows(tb, 1), _const((1, HEAD_PAD))], out_specs=(_rows(tb, HEAD_PAD),) * 3,
        compiler_params=_params(("parallel",)),
    )(pos_col, inv_lane)


def _rope(t, c, sa, sb):
    return t * c + pltpu.roll(t, 112, 1) * sa + pltpu.roll(t, 16, 1) * sb


def _rope_t(d, c, sa, sb):
    return d * c + pltpu.roll(d * sa, 16, 1) + pltpu.roll(d * sb, 112, 1)


def _rms(c, g):
    r = lax.rsqrt(jnp.mean(c * c, axis=-1, keepdims=True) + NORM_EPS)
    xh = c * r
    return xh * g, xh, r


def _mla_proj_fwd(x, win, gq, gkv, wuq, wukv_k, wukv_v, ctab, satab, sbtab, exchange=None):
    s = x.shape[0]
    tb = min(TB_PROJ, s)

    def body(x_ref, win_ref, gq_ref, gkv_ref, wuq_ref, wk_ref, wv_ref, c_ref, sa_ref, sb_ref,
             gate_ref, qm_ref, cq_ref, ckv_ref, q_ref, qt_ref, k_ref, v_ref, vt_ref):
        z = _mm(x_ref[...], win_ref[...])
        gate_ref[...] = z[:, 0:1024]
        qm_ref[...] = z[:, 1024:1280].astype(BF16)
        cq = z[:, 1280:1664]
        ckv = z[:, 1664:1920]
        cq_ref[...] = cq
        ckv_ref[...] = ckv
        c, sa, sb = c_ref[...], sa_ref[...], sb_ref[...]
        nq, _, _ = _rms(cq, gq_ref[...])
        nkv, _, _ = _rms(ckv, gkv_ref[...])
        qf = _mm(nq, wuq_ref[...])
        kf = _mm(nkv, wk_ref[...])
        vf = _mm(nkv, wv_ref[...])
        v_ref[...] = vf.astype(BF16)
        for j in range(N_TOK_HEADS // 2):
            sl = slice(HEAD_PAD * j, HEAD_PAD * (j + 1))
            vt_ref[sl, :] = vf[:, sl].T.astype(BF16)
        kr = _rope(z[:, 1920:2048], c, sa, sb)
        for h in range(N_TOK_HEADS):
            sl = slice(HEAD_PAD * h, HEAD_PAD * (h + 1))
            qh = _rope(qf[:, sl], c, sa, sb) * ATT_SCALE_LOG2
            q_ref[:, sl] = qh.astype(BF16)
            qt_ref[sl, :] = qh.T.astype(BF16)
            k_ref[:, sl] = (kf[:, sl] + kr).astype(BF16)

    outs = (jax.ShapeDtypeStruct((s, 1024), F32), jax.ShapeDtypeStruct((s, MEM_W), BF16),
            jax.ShapeDtypeStruct((s, Q_LORA), F32), jax.ShapeDtypeStruct((s, KV_LORA), F32),
            jax.ShapeDtypeStruct((s, QK_W), BF16), jax.ShapeDtypeStruct((QK_W, s), BF16),
            jax.ShapeDtypeStruct((s, QK_W), BF16),
            jax.ShapeDtypeStruct((s, TOK_W), BF16), jax.ShapeDtypeStruct((TOK_W, s), BF16))

    def cols(w):
        return pl.BlockSpec((w, tb), lambda i: (0, i))

    return _run(
        body, name="mla_proj_fwd", grid=(s // tb,), out_shape=outs,
        in_specs=[_rows(tb, 1024), _const((1024, 2048)), _const((1, Q_LORA)), _const((1, KV_LORA)),
                  _const((Q_LORA, QK_W)), _const((KV_LORA, QK_W)), _const((KV_LORA, TOK_W)),
                  _rows(tb, HEAD_PAD), _rows(tb, HEAD_PAD), _rows(tb, HEAD_PAD)],
        out_specs=(_rows(tb, 1024), _rows(tb, MEM_W), _rows(tb, Q_LORA), _rows(tb, KV_LORA),
                   _rows(tb, QK_W), cols(QK_W), _rows(tb, QK_W), _rows(tb, TOK_W), cols(TOK_W)),
        args=(x, win, gq, gkv, wuq, wukv_k, wukv_v, ctab, satab, sbtab), sem=("parallel",), exchange=exchange)


def _mla_proj_bwd(x, cq, ckv, dq, dkt, dvt, dgate, dqm, dres, win, gq, gkv, wuq, wukv_k, wukv_v, ctab, satab, sbtab):
    s = x.shape[0]
    tb = min(TB_PROJ_BWD, s)

    def body(x_ref, cq_ref, ckv_ref, dq_ref, dkt_ref, dvt_ref, dgate_ref, dqm_ref, dres_ref, win_ref, gq_ref, gkv_ref,
             wuq_ref, wk_ref, wv_ref, c_ref, sa_ref, sb_ref,
             dx_ref, dwin_ref, dwuq_ref, dwk_ref, dwv_ref, dgq_ref, dgkv_ref):
        @pl.when(pl.program_id(0) == 0)
        def _():
            for r in (dwin_ref, dwuq_ref, dwk_ref, dwv_ref, dgq_ref, dgkv_ref):
                r[...] = jnp.zeros_like(r)

        c, sa, sb = c_ref[...], sa_ref[...], sb_ref[...]
        lane = _iota((tb, HEAD_PAD), 1)
        gq, gkv = gq_ref[...], gkv_ref[...]
        nq, xhq, rq = _rms(cq_ref[...], gq)
        nkv, xhk, rk = _rms(ckv_ref[...], gkv)
        dkp = dkt_ref[...].T * math.log(2.0)
        dqs, dkr = [], jnp.zeros((tb, HEAD_PAD), F32)
        for h in range(N_TOK_HEADS):
            sl = slice(HEAD_PAD * h, HEAD_PAD * (h + 1))
            dqs.append(_rope_t(dq_ref[:, sl], c, sa, sb).astype(BF16))
            dkr = dkr + dkp[:, sl]
        dqf = jnp.concatenate(dqs, axis=1)
        dkr = jnp.where((lane >= 64) & (lane < 96), _rope_t(dkr, c, sa, sb), 0.0)
        dvb = dvt_ref[...].T.astype(BF16)
        dkb = dkp.astype(BF16)
        dnq = _mm_nt(dqf, wuq_ref[...])
        dwuq_ref[...] += _mm_tn(nq, dqf)
        dgq_ref[...] += jnp.sum(dnq * xhq, axis=0, keepdims=True)
        dxh = dnq * gq
        dcq = rq * (dxh - xhq * jnp.mean(dxh * xhq, axis=-1, keepdims=True))
        dnkv = _mm_nt(dkb, wk_ref[...]) + _mm_nt(dvb, wv_ref[...])
        nkvb = nkv.astype(BF16)
        dwk_ref[...] += _mm_tn(nkvb, dkb)
        dwv_ref[...] += _mm_tn(nkvb, dvb)
        dgkv_ref[...] += jnp.sum(dnkv * xhk, axis=0, keepdims=True)
        dxh = dnkv * gkv
        dckv = rk * (dxh - xhk * jnp.mean(dxh * xhk, axis=-1, keepdims=True))
        dz = jnp.concatenate([dgate_ref[...], dqm_ref[...], dcq, dckv, dkr], axis=1).astype(BF16)
        dx_ref[...] = _mm_nt(dz, win_ref[...]) + dres_ref[...]
        dwin_ref[...] += _mm_tn(x_ref[...], dz)

    outs = (jax.ShapeDtypeStruct((s, 1024), F32), jax.ShapeDtypeStruct((1024, 2048), F32),
            jax.ShapeDtypeStruct((Q_LORA, QK_W), F32), jax.ShapeDtypeStruct((KV_LORA, QK_W), F32),
            jax.ShapeDtypeStruct((KV_LORA, TOK_W), F32), jax.ShapeDtypeStruct((1, Q_LORA), F32),
            jax.ShapeDtypeStruct((1, KV_LORA), F32))
    return _run(
        body, name="mla_proj_bwd", grid=(s // tb,), out_shape=outs,
        in_specs=[_rows(tb, 1024), _rows(tb, Q_LORA), _rows(tb, KV_LORA), _rows(tb, QK_W),
                  pl.BlockSpec((QK_W, tb), lambda i: (0, i)), pl.BlockSpec((TOK_W, tb), lambda i: (0, i)),
                  _rows(tb, 1024), _rows(tb, MEM_W), _rows(tb, 1024),
                  _const((1024, 2048)), _const((1, Q_LORA)), _const((1, KV_LORA)),
                  _const((Q_LORA, QK_W)), _const((KV_LORA, QK_W)), _const((KV_LORA, TOK_W)),
                  _rows(tb, HEAD_PAD), _rows(tb, HEAD_PAD), _rows(tb, HEAD_PAD)],
        out_specs=(_rows(tb, 1024), _const((1024, 2048)), _const((Q_LORA, QK_W)), _const((KV_LORA, QK_W)),
                   _const((KV_LORA, TOK_W)), _const((1, Q_LORA)), _const((1, KV_LORA))),
        args=(x, cq, ckv, dq, dkt, dvt, dgate, dqm, dres, win, gq, gkv, wuq, wukv_k, wukv_v, ctab, satab, sbtab),
        sem=("arbitrary",))


def _attn_fwd(q, k, vt, exchange=None):
    s = q.shape[0]
    tq = min(TQ_ATT_FWD, s)
    tk = min(TK_ATT, s)

    def body(q_ref, k_ref, vt_ref, o_ref, lse_ref):
        i = pl.program_id(1)
        nfull = (i * tq) // tk
        krow = _iota((tk, tq), 0)
        qpos = i * tq + _iota((tk, tq), 1)

        def head_tile(hh, st, carry, masked):
            hs = slice(HEAD_PAD * hh, HEAD_PAD * (hh + 1))
            m, l, acc = carry
            sc = _mm_nt(k_ref[pl.ds(st, tk), hs], q_ref[:, hs])
            if masked:
                sc = jnp.where(st + krow <= qpos, sc, -jnp.inf)
            m_new = jnp.maximum(m, jnp.max(sc, axis=0, keepdims=True))
            p = jnp.exp2(sc - m_new)
            a = jnp.exp2(m - m_new)
            l = a * l + jnp.sum(p, axis=0, keepdims=True)
            acc = a * acc + _mm(vt_ref[64 * hh:64 * (hh + 1), pl.ds(st, tk)], p)
            return m_new, l, acc

        def tile(j, carry, masked):
            st = pl.multiple_of(j * tk, tk)
            return tuple(head_tile(hh, st, carry[hh], masked) for hh in range(2))

        def init():
            return (jnp.full((1, tq), -jnp.inf, F32), jnp.zeros((1, tq), F32), jnp.zeros((64, tq), F32))

        carry = lax.fori_loop(0, nfull, functools.partial(tile, masked=False), (init(), init()))
        (ma, la, acca), (mb, lb, accb) = tile(nfull, carry, True)
        o_ref[...] = jnp.concatenate([acca / la, accb / lb], axis=0).T
        lse_ref[...] = jnp.concatenate([jnp.broadcast_to(ma + jnp.log2(la), (64, tq)),
                                        jnp.broadcast_to(mb + jnp.log2(lb), (64, tq))], axis=0).T

    shp = jax.ShapeDtypeStruct((s, TOK_W), F32)
    return _run(
        body, name="attn_fwd", grid=(N_TOK_HEADS // 2, s // tq), out_shape=(shp, shp),
        in_specs=[pl.BlockSpec((tq, 2 * HEAD_PAD), lambda j, i: (i, j)),
                  pl.BlockSpec((s, 2 * HEAD_PAD), lambda j, i: (0, j)),
                  pl.BlockSpec((HEAD_PAD, s), lambda j, i: (j, 0))],
        out_specs=(pl.BlockSpec((tq, HEAD_PAD), lambda j, i: (i, j)),) * 2,
        args=(q, k, vt), sem=("parallel", "arbitrary"), exchange=exchange)


def _attn_prep(o, do, lse):
    s = o.shape[0]
    tb = min(TB_PROJ, s)
    npair = N_TOK_HEADS // 2

    def body(o_ref, do_ref, lse_ref, dob_ref, dot_ref, st_ref):
        lane = _iota((tb, HEAD_PAD), 1)
        do = do_ref[...]
        dob_ref[...] = do.astype(BF16)
        prod = do * o_ref[...]
        for j in range(npair):
            sl = slice(HEAD_PAD * j, HEAD_PAD * (j + 1))
            dot_ref[sl, :] = do[:, sl].T.astype(BF16)
            pj = prod[:, sl]
            da = jnp.sum(jnp.where(lane < 64, pj, 0.0), axis=-1, keepdims=True)
            db = jnp.sum(jnp.where(lane >= 64, pj, 0.0), axis=-1, keepdims=True)
            la = lse_ref[:, HEAD_PAD * j:HEAD_PAD * j + 1]
            lb = lse_ref[:, HEAD_PAD * j + 64:HEAD_PAD * j + 65]
            st_ref[j] = jnp.where(lane == 0, la, jnp.where(lane == 1, lb, jnp.where(lane == 2, da,
                                                                                     jnp.where(lane == 3, db, 0.0))))

    return pl.pallas_call(
        body, name="attn_prep", grid=(s // tb,),
        out_shape=(jax.ShapeDtypeStruct((s, TOK_W), BF16), jax.ShapeDtypeStruct((TOK_W, s), BF16),
                   jax.ShapeDtypeStruct((npair, s, HEAD_PAD), F32)),
        in_specs=[_rows(tb, TOK_W)] * 3,
        out_specs=(_rows(tb, TOK_W), pl.BlockSpec((TOK_W, tb), lambda i: (0, i)),
                   pl.BlockSpec((npair, tb, HEAD_PAD), lambda i: (0, i, 0))),
        compiler_params=_params(("parallel",)),
    )(o, do, lse)


def _attn_bwd(q, qt, k, v, dob, dobt, stats, exchange=None):
    s = q.shape[0]
    t = min(TQ_ATT, s)
    nq = s // t

    def body(q_ref, qt_ref, do_ref, dot_ref, st_ref, k_ref, v_ref, dq_ref, dkt_ref, dvt_ref):
        i = pl.program_id(1)

        @pl.when(i == 0)
        def _():
            dkt_ref[...] = jnp.zeros_like(dkt_ref)
            dvt_ref[...] = jnp.zeros_like(dvt_ref)

        lane = _iota((t, HEAD_PAD), 1)
        qpos, kcol = _iota((t, t), 0), _iota((t, t), 1)
        do = do_ref[...]
        stats = st_ref[0]

        def head_tile(hh, ks, dq_acc, masked):
            hs = slice(HEAD_PAD * hh, HEAD_PAD * (hh + 1))
            qh = q_ref[:, hs]
            kh = k_ref[pl.ds(ks, t), hs]
            dom = jnp.where((lane < 64) if hh == 0 else (lane >= 64), do, jnp.zeros_like(do))
            lse = stats[:, hh:hh + 1]
            dlt = stats[:, 2 + hh:3 + hh]
            sc = _mm_nt(qh, kh)
            if masked:
                sc = jnp.where(kcol <= qpos, sc, -jnp.inf)
            p = jnp.exp2(sc - lse)
            dp = _mm_nt(dom, v_ref[pl.ds(ks, t), :])
            ds = (p * (dp - dlt)).astype(BF16)
            dvt_ref[64 * hh:64 * (hh + 1), pl.ds(ks, t)] += _mm(dot_ref[64 * hh:64 * (hh + 1), :], p)
            dkt_ref[HEAD_PAD * hh:HEAD_PAD * hh + 96, pl.ds(ks, t)] += _mm(qt_ref[HEAD_PAD * hh:HEAD_PAD * hh + 96, :], ds)
            return dq_acc + _mm(ds, kh)

        def tile(j, carry, masked):
            ks = pl.multiple_of(j * t, t)
            return tuple(head_tile(hh, ks, carry[hh], masked) for hh in range(2))

        zero = jnp.zeros((t, HEAD_PAD), F32)
        carry = lax.fori_loop(0, i, functools.partial(tile, masked=False), (zero, zero))
        dqa, dqb = tile(i, carry, True)
        dq_ref[...] = jnp.concatenate([dqa, dqb], axis=1) * ATT_SCALE

    return _run(
        body, name="attn_bwd", grid=(N_TOK_HEADS // 2, nq),
        out_shape=(jax.ShapeDtypeStruct((s, QK_W), F32), jax.ShapeDtypeStruct((QK_W, s), F32),
                   jax.ShapeDtypeStruct((TOK_W, s), F32)),
        in_specs=[pl.BlockSpec((t, 2 * HEAD_PAD), lambda j, i: (i, j)),
                  pl.BlockSpec((2 * HEAD_PAD, t), lambda j, i: (j, i)),
                  pl.BlockSpec((t, HEAD_PAD), lambda j, i: (i, j)),
                  pl.BlockSpec((HEAD_PAD, t), lambda j, i: (j, i)),
                  pl.BlockSpec((1, t, HEAD_PAD), lambda j, i: (j, i, 0)),
                  pl.BlockSpec((s, 2 * HEAD_PAD), lambda j, i: (0, j)),
                  pl.BlockSpec((s, HEAD_PAD), lambda j, i: (0, j))],
        out_specs=(pl.BlockSpec((t, 2 * HEAD_PAD), lambda j, i: (i, j)),
                   pl.BlockSpec((2 * HEAD_PAD, s), lambda j, i: (j, 0)),
                   pl.BlockSpec((HEAD_PAD, s), lambda j, i: (j, 0))),
        args=(q, qt, dob, dobt, stats, k, v), sem=("parallel", "arbitrary"), exchange=exchange)


def _mem_kv(mem, wmem):
    def body(m_ref, w_ref, o_ref):
        o_ref[0] = _mm(m_ref[...], w_ref[0]).astype(BF16)

    return pl.pallas_call(
        body, name="mem_kv", grid=(2,), out_shape=jax.ShapeDtypeStruct((2, MEM_LEN, 512), BF16),
        in_specs=[_const((MEM_LEN, 1024)), pl.BlockSpec((1, 1024, 512), lambda l: (l, 0, 0))],
        out_specs=pl.BlockSpec((1, MEM_LEN, 512), lambda l: (l, 0, 0)),
        compiler_params=_params(("parallel",)),
    )(mem, wmem)


def _mem_kv_bwd(mem, dmemkv):
    def body(m_ref, d_ref, o_ref):
        o_ref[...] = _mm_tn(m_ref[...], d_ref[...])

    return pl.pallas_call(
        body, name="mem_kv_bwd", grid=(1,), out_shape=jax.ShapeDtypeStruct((1024, 512), F32),
        in_specs=[_const((MEM_LEN, 1024)), _const((MEM_LEN, 512))], out_specs=_const((1024, 512)),
        compiler_params=_params(("arbitrary",)),
    )(mem, dmemkv)


def _head_mask(lane, sub):
    return (lane < 64) if sub == 0 else (lane >= 64)


def _mem_attn(qm, kv):
    tb = qm.shape[0]
    lane = _iota((tb, HEAD_PAD), 1)
    outs, ps = [], []
    for pp in range(2):
        qp = qm[:, HEAD_PAD * pp:HEAD_PAD * (pp + 1)]
        kp = kv[:, HEAD_PAD * pp:HEAD_PAD * (pp + 1)]
        vp = kv[:, MEM_W + HEAD_PAD * pp:MEM_W + HEAD_PAD * (pp + 1)]
        pair = None
        for sub in range(2):
            qh = jnp.where(_head_mask(lane, sub), qp, jnp.zeros_like(qp))
            sc = _mm_nt(qh, kp) * 0.125
            e = jnp.exp(sc - jnp.max(sc, axis=-1, keepdims=True))
            p = e / jnp.sum(e, axis=-1, keepdims=True)
            o = _mm(p, vp)
            ps.append(p)
            pair = o if sub == 0 else jnp.where(lane < 64, pair, o)
        outs.append(pair)
    return jnp.concatenate(outs, axis=1), ps


def _mem_attn_bwd(dmo, qm, kv, ps):
    tb = qm.shape[0]
    lane = _iota((tb, HEAD_PAD), 1)
    dqs, dks, dvs = [], [], []
    for pp in range(2):
        qp = qm[:, HEAD_PAD * pp:HEAD_PAD * (pp + 1)]
        kp = kv[:, HEAD_PAD * pp:HEAD_PAD * (pp + 1)]
        vp = kv[:, MEM_W + HEAD_PAD * pp:MEM_W + HEAD_PAD * (pp + 1)]
        dop = dmo[:, HEAD_PAD * pp:HEAD_PAD * (pp + 1)]
        dq_pair, dk_pair, dv_pair = None, None, None
        for sub in range(2):
            msk = _head_mask(lane, sub)
            p = ps[2 * pp + sub]
            qh = jnp.where(msk, qp, jnp.zeros_like(qp))
            doh = jnp.where(msk, dop, 0.0).astype(BF16)
            dv = _mm_tn(p, doh)
            dp = _mm_nt(doh, vp)
            ds = (p * (dp - jnp.sum(dp * p, axis=-1, keepdims=True)) * 0.125).astype(BF16)
            dq = _mm(ds, kp)
            dk = _mm_tn(ds, qh)
            if sub == 0:
                dq_pair, dk_pair, dv_pair = dq, dk, dv
            else:
                dq_pair = jnp.where(lane < 64, dq_pair, dq)
                dk_pair, dv_pair = dk_pair + dk, dv_pair + dv
        dqs.append(dq_pair)
        dks.append(dk_pair)
        dvs.append(dv_pair)
    return jnp.concatenate(dqs, axis=1), jnp.concatenate(dks + dvs, axis=1)


def _mix_core(tok, gate, qm, kv, wout, h_in, g, b):
    mem_out, ps = _mem_attn(qm, kv)
    cat = jnp.concatenate([tok, mem_out], axis=1)
    sg = jax.nn.sigmoid(gate)
    sl = gate * sg
    y = cat * sl
    r = ALPHA * h_in + _mm(y, wout)
    mu = jnp.mean(r, axis=-1, keepdims=True)
    xc = r - mu
    rstd = lax.rsqrt(jnp.mean(xc * xc, axis=-1, keepdims=True) + NORM_EPS)
    xh = xc * rstd
    return xh * g + b, (ps, cat, sg, sl, y, xh, rstd)


def _mix_fwd(tok, gate, qm, kv, wout, h_in, g, b):
    s = tok.shape[0]
    tb = min(TB_MIX, s)

    def body(tok_ref, gate_ref, qm_ref, kv_ref, w_ref, h_ref, g_ref, b_ref, o_ref):
        o_ref[...], _ = _mix_core(tok_ref[...], gate_ref[...], qm_ref[...], kv_ref[...], w_ref[...], h_ref[...],
                                  g_ref[...], b_ref[...])

    return pl.pallas_call(
        body, name="mix_fwd", grid=(s // tb,), out_shape=jax.ShapeDtypeStruct((s, 1024), F32),
        in_specs=[_rows(tb, TOK_W), _rows(tb, 1024), _rows(tb, MEM_W), _const((MEM_LEN, 512)), _const((1024, 1024)),
                  _rows(tb, 1024), _const((1, 1024)), _const((1, 1024))],
        out_specs=_rows(tb, 1024), compiler_params=_params(("parallel",)),
    )(tok, gate, qm, kv, wout, h_in, g, b)


def _mix_bwd(tok, gate, qm, kv, wout, h_in, g, b, up, from_loss, exchange=None):
    s = tok.shape[0]
    tb = min(TB_MIX, s)

    def body(tok_ref, gate_ref, qm_ref, kv_ref, w_ref, h_ref, g_ref, b_ref, up_ref,
             dres_ref, dtok_ref, dgate_ref, dqm_ref, dw_ref, dkv_ref, dg_ref, db_ref, loss_ref):
        @pl.when(pl.program_id(0) == 0)
        def _():
            for r in (dw_ref, dkv_ref, dg_ref, db_ref, loss_ref):
                r[...] = jnp.zeros_like(r)

        gate, qm, kv, wout, g = gate_ref[...], qm_ref[...], kv_ref[...], w_ref[...], g_ref[...]
        h_out, (ps, cat, sg, sl, y, xh, rstd) = _mix_core(tok_ref[...], gate, qm, kv, wout, h_ref[...], g, b_ref[...])
        if from_loss:
            diff = h_out - up_ref[...]
            loss_ref[...] += 0.5 * jnp.sum(jnp.mean(diff * diff, axis=-1, keepdims=True), axis=0, keepdims=True)
            dh = diff * (1.0 / D_MODEL)
        else:
            dh = up_ref[...]
        dg_ref[...] += jnp.sum(dh * xh, axis=0, keepdims=True)
        db_ref[...] += jnp.sum(dh, axis=0, keepdims=True)
        dxh = dh * g
        dr = rstd * (dxh - jnp.mean(dxh, axis=-1, keepdims=True) - xh * jnp.mean(dxh * xh, axis=-1, keepdims=True))
        dres_ref[...] = ALPHA * dr
        drb = dr.astype(BF16)
        dy = _mm_nt(drb, wout)
        dw_ref[...] += _mm_tn(y, drb)
        dcat = dy * sl
        dgate_ref[...] = dy * cat * (sg * (1.0 + gate * (1.0 - sg)))
        dtok_ref[...] = dcat[:, :TOK_W]
        dqm, dkv = _mem_attn_bwd(dcat[:, TOK_W:], qm, kv, ps)
        dqm_ref[...] = dqm
        dkv_ref[...] += dkv

    outs = (jax.ShapeDtypeStruct((s, 1024), F32), jax.ShapeDtypeStruct((s, TOK_W), F32),
            jax.ShapeDtypeStruct((s, 1024), F32), jax.ShapeDtypeStruct((s, MEM_W), F32),
            jax.ShapeDtypeStruct((1024, 1024), F32), jax.ShapeDtypeStruct((MEM_LEN, 512), F32),
            jax.ShapeDtypeStruct((1, 1024), F32), jax.ShapeDtypeStruct((1, 1024), F32),
            jax.ShapeDtypeStruct((1, 1), F32))
    return _run(
        body, name="mix_bwd_loss" if from_loss else "mix_bwd", grid=(s // tb,), out_shape=outs,
        in_specs=[_rows(tb, TOK_W), _rows(tb, 1024), _rows(tb, MEM_W), _const((MEM_LEN, 512)), _const((1024, 1024)),
                  _rows(tb, 1024), _const((1, 1024)), _const((1, 1024)), _rows(tb, 1024)],
        out_specs=(_rows(tb, 1024), _rows(tb, TOK_W), _rows(tb, 1024), _rows(tb, MEM_W), _const((1024, 1024)),
                   _const((MEM_LEN, 512)), _const((1, 1024)), _const((1, 1024)), _const((1, 1))),
        args=(tok, gate, qm, kv, wout, h_in, g, b, up), sem=("arbitrary",), exchange=exchange)


def _shift_down(u, tail, k):
    if k == 0:
        return u
    r = pltpu.roll(u, k, 0)
    row8 = _iota((8, u.shape[1]), 0)
    head = jnp.where(row8 < k, pltpu.roll(tail, k, 0), r[:8])
    return jnp.concatenate([head, r[8:]], axis=0)


def _shift_up(d, head, k):
    if k == 0:
        return d
    n = d.shape[0]
    r = pltpu.roll(d, n - k, 0)
    row8 = _iota((8, d.shape[1]), 0)
    last = jnp.where(row8 >= 8 - k, pltpu.roll(head, 8 - k, 0), r[n - 8:])
    return jnp.concatenate([r[:n - 8], last], axis=0)


def _scan_down(a, b):
    n = a.shape[0]
    row = _iota(a.shape, 0)
    s = 1
    while s < n:
        ok = row >= s
        a_s = jnp.where(ok, pltpu.roll(a, s, 0), 1.0)
        b_s = jnp.where(ok, pltpu.roll(b, s, 0), 0.0)
        b = a * b_s + b
        a = a * a_s
        s *= 2
    return a, b


def _scan_up(a, b):
    n = a.shape[0]
    row = _iota(a.shape, 0)
    s = 1
    while s < n:
        ok = row < n - s
        a_s = jnp.where(ok, pltpu.roll(a, n - s, 0), 1.0)
        b_s = jnp.where(ok, pltpu.roll(b, n - s, 0), 0.0)
        b = a * b_s + b
        a = a * a_s
        s *= 2
    return a, b


def _neg_expm1(x):
    poly = -x * (1.0 + x * (0.5 + x * (1.0 / 6.0 + x * (1.0 / 24.0 + x * (1.0 / 120.0)))))
    return jnp.where(x > -0.1, poly, 1.0 - jnp.exp(x))


def _softplus(x):
    return jnp.maximum(x, 0.0) + jnp.log(1.0 + jnp.exp(-jnp.abs(x)))


def _lru_gates(u, tail, cw, cb, wr, br, wi, bi, lam):
    us = [_shift_down(u, tail, k) for k in range(4)]
    xc = cb + us[3] * cw[0:1] + us[2] * cw[1:2] + us[1] * cw[2:3] + us[0] * cw[3:4]
    xb = xc.astype(BF16)
    pre_r = jnp.concatenate([_mm(xb[:, 256 * g:256 * (g + 1)], wr[g]) for g in range(3)], axis=1) + br
    pre_i = jnp.concatenate([_mm(xb[:, 256 * g:256 * (g + 1)], wi[g]) for g in range(3)], axis=1) + bi
    rg, ig = jax.nn.sigmoid(pre_r), jax.nn.sigmoid(pre_i)
    clam = -LRU_C * _softplus(-lam)
    la = clam * rg
    a = jnp.exp(la)
    mm = jnp.sqrt(_neg_expm1(2.0 * la))
    return us, xc, xb, rg, ig, clam, la, a, mm


def _lru_fwd(h, win, cw, cb, wr, br, wi, bi, lam):
    s = h.shape[0]
    tb = min(TB_LRU, s)

    def body(h_ref, win_ref, cw_ref, cb_ref, wr_ref, br_ref, wi_ref, bi_ref, lam_ref,
             u_ref, gate_ref, qm_ref, hs_ref, tail_sc, carry_sc):
        @pl.when(pl.program_id(0) == 0)
        def _():
            tail_sc[...] = jnp.zeros_like(tail_sc)
            carry_sc[...] = jnp.zeros_like(carry_sc)

        z = _mm(h_ref[...], win_ref[...])
        u = z[:, :TOK_W]
        u_ref[...] = u
        gate_ref[...] = z[:, TOK_W:TOK_W + 1024]
        qm_ref[...] = z[:, TOK_W + 1024:].astype(BF16)
        _, xc, _, _, ig, _, _, a, mm = _lru_gates(u, tail_sc[...], cw_ref[...], cb_ref[...], wr_ref[...], br_ref[...],
                                                 wi_ref[...], bi_ref[...], lam_ref[...])
        big_a, big_b = _scan_down(a, mm * (ig * xc))
        hs = big_a * carry_sc[0:1, :] + big_b
        hs_ref[...] = hs
        tail_sc[...] = u[tb - 8:, :]
        carry_sc[...] = jnp.broadcast_to(hs[tb - 1:tb, :], carry_sc.shape)

    outs = (jax.ShapeDtypeStruct((s, TOK_W), F32), jax.ShapeDtypeStruct((s, 1024), F32),
            jax.ShapeDtypeStruct((s, MEM_W), BF16), jax.ShapeDtypeStruct((s, TOK_W), F32))
    return pl.pallas_call(
        body, name="lru_fwd", grid=(s // tb,), out_shape=outs,
        in_specs=[_rows(tb, 1024), _const((1024, 2048)), _const((4, TOK_W)), _const((1, TOK_W)),
                  _const((3, 256, 256)), _const((1, TOK_W)), _const((3, 256, 256)), _const((1, TOK_W)),
                  _const((1, TOK_W))],
        out_specs=(_rows(tb, TOK_W), _rows(tb, 1024), _rows(tb, MEM_W), _rows(tb, TOK_W)),
        scratch_shapes=[pltpu.VMEM((8, TOK_W), F32), pltpu.VMEM((8, TOK_W), F32)],
        compiler_params=_params(),
    )(h, win, cw, cb, wr, br, wi, bi, lam)


def _lru_bwd(dhs, dgate, dqm, dres, h, u, hs, win, cw, cb, wr, br, wi, bi, lam):
    s = h.shape[0]
    tb = min(TB_LRU, s)
    nb = s // tb

    def rev(w):
        return pl.BlockSpec((tb, w), lambda i: (nb - 1 - i, 0))

    def prev_tail(w):
        return pl.BlockSpec((8, w), lambda i: (jnp.maximum((nb - 1 - i) * (tb // 8) - 1, 0), 0))

    def body(dhs_ref, dgate_ref, dqm_ref, dres_ref, h_ref, u_ref, hs_ref, ut_ref, hst_ref, win_ref, cw_ref, cb_ref,
             wr_ref, br_ref, wi_ref, bi_ref, lam_ref,
             dh_ref, dwin_ref, dcw_ref, dcb_ref, dwr_ref, dbr_ref, dwi_ref, dbi_ref, dlam_ref, ecar_sc, dxc_sc):
        i = pl.program_id(0)

        @pl.when(i == 0)
        def _():
            for r in (dwin_ref, dcw_ref, dcb_ref, dwr_ref, dbr_ref, dwi_ref, dbi_ref, dlam_ref, ecar_sc, dxc_sc):
                r[...] = jnp.zeros_like(r)

        first = (i == nb - 1)
        u = u_ref[...]
        utail = jnp.where(first, 0.0, ut_ref[...])
        hstail = jnp.where(first, 0.0, hst_ref[...])
        cw, wr, wi, lam = cw_ref[...], wr_ref[...], wi_ref[...], lam_ref[...]
        us, xc, xb, rg, ig, clam, la, a, mm = _lru_gates(u, utail, cw, cb_ref[...], wr, br_ref[...], wi, bi_ref[...], lam)
        row = _iota(a.shape, 0)
        a_next = jnp.where(row < tb - 1, pltpu.roll(a, tb - 1, 0), 1.0)
        big_a, big_b = _scan_up(a_next, dhs_ref[...])
        e = big_a * ecar_sc[0:1, :] + big_b
        ecar_sc[...] = jnp.broadcast_to(a[0:1, :] * e[0:1, :], ecar_sc.shape)
        hs_prev = _shift_down(hs_ref[...], hstail, 1)
        da = e * hs_prev
        ix = ig * xc
        dmm = e * ix
        dix = e * mm
        dla = da * a - dmm * (a * a) / mm
        dlam_ref[...] += jnp.sum(dla * rg, axis=0, keepdims=True)
        dpr = (dla * clam) * rg * (1.0 - rg)
        dpi = (dix * xc) * ig * (1.0 - ig)
        dbr_ref[...] += jnp.sum(dpr, axis=0, keepdims=True)
        dbi_ref[...] += jnp.sum(dpi, axis=0, keepdims=True)
        dprb, dpib = dpr.astype(BF16), dpi.astype(BF16)
        dxc_g = []
        for g in range(3):
            sl = slice(256 * g, 256 * (g + 1))
            dwr_ref[g] += _mm_tn(xb[:, sl], dprb[:, sl])
            dwi_ref[g] += _mm_tn(xb[:, sl], dpib[:, sl])
            dxc_g.append(_mm_nt(dprb[:, sl], wr[g]) + _mm_nt(dpib[:, sl], wi[g]))
        dxc = dix * ig + jnp.concatenate(dxc_g, axis=1)
        dcb_ref[...] += jnp.sum(dxc, axis=0, keepdims=True)
        dcw_ref[...] += jnp.concatenate([jnp.sum(dxc * us[3 - tap], axis=0, keepdims=True) for tap in range(4)], axis=0)
        head = dxc_sc[...]
        du = dxc * cw[3:4]
        for k in range(1, 4):
            du = du + _shift_up(dxc, head, k) * cw[3 - k:4 - k]
        dxc_sc[...] = dxc[:8, :]
        dz = jnp.concatenate([du, dgate_ref[...], dqm_ref[...]], axis=1).astype(BF16)
        dh_ref[...] = _mm_nt(dz, win_ref[...]) + dres_ref[...]
        dwin_ref[...] += _mm_tn(h_ref[...], dz)

        @pl.when(i == nb - 1)
        def _():
            dlam_ref[...] = dlam_ref[...] * (LRU_C * jax.nn.sigmoid(-lam))

    outs = (jax.ShapeDtypeStruct((s, 1024), F32), jax.ShapeDtypeStruct((1024, 2048), F32),
            jax.ShapeDtypeStruct((4, TOK_W), F32), jax.ShapeDtypeStruct((1, TOK_W), F32),
            jax.ShapeDtypeStruct((3, 256, 256), F32), jax.ShapeDtypeStruct((1, TOK_W), F32),
            jax.ShapeDtypeStruct((3, 256, 256), F32), jax.ShapeDtypeStruct((1, TOK_W), F32),
            jax.ShapeDtypeStruct((1, TOK_W), F32))
    return pl.pallas_call(
        body, name="lru_bwd", grid=(nb,), out_shape=outs,
        in_specs=[rev(TOK_W), rev(1024), rev(MEM_W), rev(1024), rev(1024), rev(TOK_W), rev(TOK_W),
                  prev_tail(TOK_W), prev_tail(TOK_W),
                  _const((1024, 2048)), _const((4, TOK_W)), _const((1, TOK_W)), _const((3, 256, 256)),
                  _const((1, TOK_W)), _const((3, 256, 256)), _const((1, TOK_W)), _const((1, TOK_W))],
        out_specs=(rev(1024), _const((1024, 2048)), _const((4, TOK_W)), _const((1, TOK_W)), _const((3, 256, 256)),
                   _const((1, TOK_W)), _const((3, 256, 256)), _const((1, TOK_W)), _const((1, TOK_W))),
        scratch_shapes=[pltpu.VMEM((8, TOK_W), F32), pltpu.VMEM((8, TOK_W), F32)],
        compiler_params=_params(),
    )(dhs, dgate, dqm, dres, h, u, hs, u, hs, win, cw, cb, wr, br, wi, bi, lam)


def _adamw(name, w, g, m, v):
    rows, cols = w.shape
    tb = 256 if rows % 256 == 0 else rows

    def body(w_ref, g_ref, m_ref, v_ref, d_ref, nm_ref, nv_ref):
        g = g_ref[...]
        nm = ADAM_B1 * m_ref[...] + (1.0 - ADAM_B1) * g
        nv = ADAM_B2 * v_ref[...] + (1.0 - ADAM_B2) * (g * g)
        m_hat = nm / (1.0 - ADAM_B1 ** ADAM_STEP)
        v_hat = nv / (1.0 - ADAM_B2 ** ADAM_STEP)
        d_ref[...] = -ADAM_LR * (m_hat / (jnp.sqrt(v_hat) + ADAM_EPS) + ADAM_WD * w_ref[...])
        nm_ref[...] = nm
        nv_ref[...] = nv

    shp = jax.ShapeDtypeStruct((rows, cols), F32)
    return pl.pallas_call(
        body, name="adamw_" + name, grid=(rows // tb,), out_shape=(shp, shp, shp),
        in_specs=[_rows(tb, cols)] * 4, out_specs=(_rows(tb, cols),) * 3,
        compiler_params=_params(("parallel",)),
    )(w, g, m, v)


def _row_block(rows, cap=2048):
    return max(t for t in range(8, cap + 1, 8) if rows % t == 0)


def _add2(a, b):
    rows = a.shape[0]
    tb = _row_block(rows)

    def body(a_ref, b_ref, o_ref):
        o_ref[...] = a_ref[...] + b_ref[...]

    return pl.pallas_call(
        body, name="add_sibling", grid=(rows // tb,), out_shape=jax.ShapeDtypeStruct(a.shape, F32),
        in_specs=[_rows(tb, 128)] * 2, out_specs=_rows(tb, 128), compiler_params=_params(("parallel",)),
    )(a, b)


def _sum_slots(landed, own, rows):
    tb = _row_block(rows, 1024)

    def body(l_ref, o_ref, out_ref):
        t = 2 * lax.axis_index("x") + lax.axis_index("y")
        r = [jnp.where(t == s, o_ref[s], l_ref[s].astype(F32)) for s in range(4)]
        out_ref[...] = ((r[0] + r[1]) + r[2]) + r[3]

    return pl.pallas_call(
        body, name="sum_chips", grid=(rows // tb,), out_shape=jax.ShapeDtypeStruct((rows, 128), F32),
        in_specs=[pl.BlockSpec((4, tb, 128), lambda i: (0, i, 0))] * 2, out_specs=_rows(tb, 128),
        compiler_params=_params(("parallel",)),
    )(landed, own)


_ANY = pl.BlockSpec(memory_space=pl.ANY)


def _place():
    x, y, c = lax.axis_index("x"), lax.axis_index("y"), lax.axis_index("c")
    return x, y, c, [(1 - x, y), (x, 1 - y), (1 - x, 1 - y)]


def _remote(src, dst, ssem, rsem, to):
    return pltpu.make_async_remote_copy(src_ref=src, dst_ref=dst, send_sem=ssem, recv_sem=rsem, device_id=to,
                                        device_id_type=MESH)


class _Exchange:
    def __init__(self, ins, out_shape, sems, start, finish):
        self.ins, self.out_shape, self.sems, self.start, self.finish = ins, out_shape, sems, start, finish


def _run(body, *, name, grid, in_specs, out_specs, out_shape, args, scratch=(), sem, exchange=None):
    if exchange is None:
        return pl.pallas_call(body, name=name, grid=grid, out_shape=tuple(out_shape), in_specs=list(in_specs),
                              out_specs=tuple(out_specs), scratch_shapes=list(scratch),
                              compiler_params=_params(sem))(*args)
    n_in, n_out, n_sc = len(args), len(out_shape), len(scratch)
    k_in, k_out = len(exchange.ins), len(exchange.out_shape)

    def fused(*refs):
        ins, refs = refs[:n_in], refs[n_in:]
        xin, refs = refs[:k_in], refs[k_in:]
        outs, refs = refs[:n_out], refs[n_out:]
        xout, refs = refs[:k_out], refs[k_out:]
        sc, xsem = refs[:n_sc], refs[n_sc:]
        first = pl.program_id(0) == 0
        last = pl.program_id(0) == grid[0] - 1
        for a in range(1, len(grid)):
            first = first & (pl.program_id(a) == 0)
            last = last & (pl.program_id(a) == grid[a] - 1)

        @pl.when(first)
        def _():
            exchange.start(xin, xout, xsem)

        body(*ins, *outs, *sc)

        @pl.when(last)
        def _():
            exchange.finish(xin, xout, xsem)

    return pl.pallas_call(
        fused, name=name, grid=grid, out_shape=(*out_shape, *exchange.out_shape),
        in_specs=[*in_specs, *[_ANY] * k_in], out_specs=(*out_specs, *[_ANY] * k_out),
        scratch_shapes=[*scratch, *exchange.sems],
        compiler_params=_params(("arbitrary",) * len(grid)),
    )(*args, *exchange.ins)


def _run_exchange(exchange, name):
    def body(*refs):
        k_in, k_out = len(exchange.ins), len(exchange.out_shape)
        xin, xout, xsem = refs[:k_in], refs[k_in:k_in + k_out], refs[k_in + k_out:]
        exchange.start(xin, xout, xsem)
        exchange.finish(xin, xout, xsem)

    return pl.pallas_call(
        body, name=name, out_shape=tuple(exchange.out_shape), in_specs=[_ANY] * len(exchange.ins),
        out_specs=tuple([_ANY] * len(exchange.out_shape)), scratch_shapes=list(exchange.sems),
    )(*exchange.ins)


def _gather_shards(wsh):
    _, hh, _ = wsh.shape

    def first_hop(w_ref, out_ref, ssems, rsems):
        x, y, c, chips = _place()
        t = 2 * x + y
        return [_remote(w_ref.at[c], out_ref.at[t, c], ssems.at[j], rsems.at[j], (cx, cy, c))
                for j, (cx, cy) in enumerate(chips)]

    def start(xin, xout, xsem):
        for cp in first_hop(xin[0], xout[0], *xsem):
            cp.start()

    def finish(xin, xout, xsem):
        out_ref, (ssems, rsems) = xout[0], xsem
        first = first_hop(xin[0], out_ref, *xsem)
        x, y, c, chips = _place()
        passed = []
        for j, (cx, cy) in enumerate(chips):
            got = out_ref.at[2 * cx + cy, c]
            _remote(got, got, ssems.at[j], rsems.at[j], (cx, cy, c)).wait_recv()
            cp = _remote(got, got, ssems.at[3 + j], rsems.at[3 + j], (x, y, 1 - c))
            cp.start()
            passed.append(cp)
        for j, (cx, cy) in enumerate(chips):
            got = out_ref.at[2 * cx + cy, 1 - c]
            _remote(got, got, ssems.at[3 + j], rsems.at[3 + j], (x, y, 1 - c)).wait_recv()
        for cp in first + passed:
            cp.wait_send()

    return _Exchange([wsh], [jax.ShapeDtypeStruct((4, 2, hh, 128), wsh.dtype)],
                     [pltpu.SemaphoreType.DMA((6,)), pltpu.SemaphoreType.DMA((6,))], start, finish)


def _gathered(landed, own):
    t = 2 * lax.axis_index("x") + lax.axis_index("y")
    return lax.dynamic_update_slice(landed, own[None], (t, 0, 0, 0))


def _swap_sibling(v):
    def copy(xin, xout, xsem):
        x, y, c, _ = _place()
        return _remote(xin[0], xout[0], xsem[0], xsem[1], (x, y, 1 - c))

    return _Exchange([v], [jax.ShapeDtypeStruct(v.shape, v.dtype)],
                     [pltpu.SemaphoreType.DMA, pltpu.SemaphoreType.DMA],
                     lambda *a: copy(*a).start(), lambda *a: copy(*a).wait())


def _scatter_chips(parts):
    n = len(parts)

    def copies(xin, xout, ssems, rsems):
        x, y, c, chips = _place()
        t = 2 * x + y
        return [_remote(xin[k].at[2 * cx + cy], xout[k].at[t], ssems.at[n * j + k], rsems.at[n * j + k], (cx, cy, c))
                for j, (cx, cy) in enumerate(chips) for k in range(n)]

    def start(xin, xout, xsem):
        for cp in copies(xin, xout, *xsem):
            cp.start()

    def finish(xin, xout, xsem):
        ssems, rsems = xsem
        x, y, c, chips = _place()
        for j, (cx, cy) in enumerate(chips):
            for k in range(n):
                got = xout[k].at[2 * cx + cy]
                _remote(got, got, ssems.at[n * j + k], rsems.at[n * j + k], (cx, cy, c)).wait_recv()
        for cp in copies(xin, xout, *xsem):
            cp.wait_send()

    return _Exchange(parts, [jax.ShapeDtypeStruct(a.shape, a.dtype) for a in parts],
                     [pltpu.SemaphoreType.DMA((3 * n,)), pltpu.SemaphoreType.DMA((3 * n,))], start, finish)


def _share_reduced(piece, eighth):
    def copies(t_ref, mine_r, sib_ref, rall_ref, ssems, rsems, lsem):
        x, y, c, _ = _place()
        me = 4 * x + 2 * y + c
        loc = pltpu.make_async_copy(mine_r, rall_ref.at[me], lsem)
        sends = [_remote(t_ref, sib_ref, ssems.at[0], rsems.at[0], (x, y, 1 - c))]
        peers = []
        for mask in range(1, 8):
            px = 1 - x if mask & 4 else x
            py = 1 - y if mask & 2 else y
            pc = 1 - c if mask & 1 else c
            peers.append((mask, px, py, pc))
            sends.append(_remote(mine_r, rall_ref.at[me], ssems.at[mask], rsems.at[mask], (px, py, pc)))
        return loc, sends, peers

    def start(xin, xout, xsem):
        loc, sends, _ = copies(*xin, *xout, *xsem)
        for cp in [loc] + sends:
            cp.start()

    def finish(xin, xout, xsem):
        (sib_ref, rall_ref), (ssems, rsems, _) = xout, xsem
        loc, sends, peers = copies(*xin, *xout, *xsem)
        x, y, c, _ = _place()
        _remote(sib_ref, sib_ref, ssems.at[0], rsems.at[0], (x, y, 1 - c)).wait_recv()
        for mask, px, py, pc in peers:
            got = rall_ref.at[4 * px + 2 * py + pc]
            _remote(got, got, ssems.at[mask], rsems.at[mask], (px, py, pc)).wait_recv()
        for cp in sends:
            cp.wait_send()
        loc.wait()

    return _Exchange([piece, eighth],
                     [jax.ShapeDtypeStruct(piece.shape, F32), jax.ShapeDtypeStruct((8, *eighth.shape), F32)],
                     [pltpu.SemaphoreType.DMA((8,)), pltpu.SemaphoreType.DMA((8,)), pltpu.SemaphoreType.DMA],
                     start, finish)


def _ceil_to(n, m):
    return -(-n // m) * m


def _pack_bf16(parts):
    blocks = [p.reshape(-1, 128) for p in parts]
    rows = jnp.concatenate([jnp.pad(b, ((0, -b.shape[0] % 16), (0, 0))) for b in blocks])
    hw = _ceil_to(rows.shape[0], 32) // 2
    return jnp.pad(rows, ((0, 2 * hw - rows.shape[0]), (0, 0))).reshape(2, hw, 128)


def _segments(wall, parts):
    wall = wall.reshape(4, -1, 128)
    out, row = [], 0
    for p in parts:
        n = p.size // 128
        out.append(wall[:, row:row + n].reshape(4, *p.shape))
        row += _ceil_to(n, 16)
    return out


class _GradReduce:
    def __init__(self, sharded, replicated, c_idx, wire_bf16=False):
        self.c_idx, self.wire_bf16 = c_idx, wire_bf16
        self.rowwise = [(n, g.shape[1:]) for n, g in sharded if math.prod(g.shape[1:]) % 128 == 0]
        self.small = [(n, g.shape[1:]) for n, g in sharded if math.prod(g.shape[1:]) % 128 != 0]
        self.replicated = [(n, g.shape[0]) for n, g in replicated]
        by_name = dict(sharded)
        blocks = [by_name[n].reshape(4, -1, 128) for n, _ in self.rowwise]
        if self.small:
            rest = jnp.concatenate([by_name[n].reshape(4, -1) for n, _ in self.small], axis=1)
            blocks.append(jnp.pad(rest, ((0, 0), (0, -rest.shape[1] % 128))).reshape(4, -1, 128))
        sh = jnp.concatenate([jnp.pad(b, ((0, 0), (0, -b.shape[1] % 8), (0, 0))) for b in blocks], axis=1)
        self.hs = _ceil_to(sh.shape[1], 256) // 2
        sh = jnp.pad(sh, ((0, 0), (0, 2 * self.hs - sh.shape[1]), (0, 0))).reshape(4, 2, self.hs, 128)
        rp = jnp.concatenate([g for _, g in replicated])
        self.rr = _ceil_to(_ceil_to(rp.shape[0], 128) // 128, 64) // 8
        rp = jnp.pad(rp, (0, 8 * self.rr * 128 - rp.shape[0])).reshape(4, 2, self.rr, 128)
        gbuf = jnp.concatenate([sh, rp], axis=2)
        self.hh = self.hs + self.rr
        self.mine = lax.dynamic_index_in_dim(gbuf, c_idx, axis=1, keepdims=False).reshape(4 * self.hh, 128)
        self.other = lax.dynamic_index_in_dim(gbuf, 1 - c_idx, axis=1, keepdims=False).reshape(4 * self.hh, 128)

    def swap(self):
        return _swap_sibling(self.other)

    def swapped(self, got):
        self.chip_sum = _add2(self.mine, got).reshape(4, self.hh, 128)

    def scatter(self):
        self.own_r = self.chip_sum[:, self.hs:]
        if self.wire_bf16:
            return _scatter_chips([self.chip_sum[:, :self.hs].astype(BF16), self.own_r])
        return _scatter_chips([self.chip_sum])

    def scattered(self, landed, landed_r=None):
        if landed_r is None:
            landed_r = landed[:, self.hs:]
        self.piece = _sum_slots(landed, self.chip_sum, self.hs)
        self.eighth = _sum_slots(landed_r, self.own_r, self.rr)

    def share(self):
        return _share_reduced(self.piece, self.eighth)

    def shared(self, sibling, rall):
        mine, sib = self.piece, sibling
        self.shard = jnp.where(self.c_idx == 0, jnp.concatenate([mine, sib]), jnp.concatenate([sib, mine]))
        self.rall = rall

    def reduced(self):
        out, row = {}, 0
        for name, shape in self.rowwise:
            rows = math.prod(shape) // 128
            out[name] = self.shard[row:row + rows].reshape(shape)
            row += _ceil_to(rows, 8)
        for group, flat in ((self.small, self.shard[row:].reshape(-1)), (self.replicated, self.rall.reshape(-1))):
            off = 0
            for name, shape in group:
                n = math.prod(shape) if isinstance(shape, tuple) else shape
                out[name] = flat[off:off + n]
                off += n
        return out


def _col_shards(w2d):
    rows, cols = w2d.shape
    return w2d.reshape(rows, 4, cols // 4).transpose(1, 0, 2)


_WIN0_PARTS = ((0, 384, 1280), (384, 640, 1664), (640, 672, 1984), (672, 1696, 0), (1696, 1952, 1024))


def _win0_aligned(shards):
    def cols(a, b):
        return [shards[s][:, max(a, 488 * s) - 488 * s:min(b, 488 * (s + 1)) - 488 * s]
                for s in range(4) if max(a, 488 * s) < min(b, 488 * (s + 1))]

    zeros = jnp.zeros((1024, 64), shards.dtype)
    return jnp.concatenate(cols(672, 1696) + cols(1696, 1952) + cols(0, 384) + cols(384, 640)
                           + [zeros] + cols(640, 672) + [zeros[:, :32]], axis=1)


def _win0_shards(dwin0p):
    shards = []
    for s in range(4):
        lo, hi = 488 * s, 488 * (s + 1)
        cols = [dwin0p[:, p + max(lo, a) - a:p + min(hi, b) - a] for a, b, p in _WIN0_PARTS if max(lo, a) < min(hi, b)]
        shards.append(jnp.concatenate(cols, axis=1))
    return jnp.stack(shards)


def _block_diag4(w):
    eye = jnp.eye(4, dtype=w.dtype)
    return jnp.einsum("gaij,ab->gaibj", w.reshape(3, 4, 64, 64), eye).reshape(3, 256, 256)


def _diag_blocks4(w):
    w5 = w.reshape(3, 4, 64, 4, 64)
    return jnp.stack([w5[:, a, :, a, :] for a in range(4)], axis=1).reshape(12, 64, 64)


def kernel(x, mem, positions, mla_w_in, mla_q_norm, mla_w_uq, mla_kv_norm, mla_w_ukv, lru_w_in, lru_conv_w, lru_conv_b, lru_w_rgate, lru_b_rgate, lru_w_igate, lru_b_igate, lru_lambda, w_mem_kv, w_out, ln_g, ln_b, loss_target, m_mla_w_in, m_mla_q_norm, m_mla_w_uq, m_mla_kv_norm, m_mla_w_ukv, m_lru_w_in, m_lru_conv_w, m_lru_conv_b, m_lru_w_rgate, m_lru_b_rgate, m_lru_w_igate, m_lru_b_igate, m_lru_lambda, m_w_mem_kv, m_w_out, m_ln_g, m_ln_b, v_mla_w_in, v_mla_q_norm, v_mla_w_uq, v_mla_kv_norm, v_mla_w_ukv, v_lru_w_in, v_lru_conv_w, v_lru_conv_b, v_lru_w_rgate, v_lru_b_rgate, v_lru_w_igate, v_lru_b_igate, v_lru_lambda, v_w_mem_kv, v_w_out, v_ln_g, v_ln_b):
    s = x.shape[1]
    c_idx = lax.axis_index("c")
    x2, mem2, tgt2 = x[0], mem[0], loss_target[0]

    first = [p.astype(BF16) for p in (mla_w_in[0], mla_w_uq[0], mla_w_ukv[0])]
    buf = _pack_bf16(first)
    mla_shards = _segments(_gathered(_run_exchange(_gather_shards(buf), "gather_weights")[0], buf), first)

    mid = [w_mem_kv.astype(BF16), w_out[0].astype(BF16)]
    buf_mid = _pack_bf16(mid)

    def mid_weights(landed):
        wmem, wout0 = _segments(_gathered(landed[0], buf_mid), mid)
        return wmem.transpose(1, 0, 2, 3).reshape(2, 1024, 512), wout0.reshape(1024, 1024)

    small = jnp.concatenate([lru_conv_w[0].reshape(-1), lru_conv_b[0], lru_b_rgate[0], lru_b_igate[0], lru_lambda[0]])
    late = [lru_w_in[0].astype(BF16), w_out[1].astype(BF16), lax.bitcast_convert_type(small, BF16)]
    buf_late = _pack_bf16(late)

    def late_weights(landed):
        win1, wout1, small_bits = _segments(_gathered(landed[0], buf_late), late)
        small_all = lax.bitcast_convert_type(small_bits, F32)
        cw = small_all[:, :768].reshape(4, 4, 192).transpose(1, 0, 2).reshape(4, TOK_W)
        cb, br, bi, lam = (small_all[:, 768 + 192 * k:960 + 192 * k].reshape(1, TOK_W) for k in range(4))
        return win1.transpose(1, 0, 2).reshape(1024, 2048), wout1.reshape(1024, 1024), cw, cb, br, bi, lam

    def reduce_late(g):
        return _GradReduce(
            [("lru_w_in", _col_shards(g["lru_w_in"])), ("lru_conv_w", _col_shards(g["lru_conv_w"])),
             ("lru_conv_b", _col_shards(g["lru_conv_b"])), ("lru_b_rgate", _col_shards(g["lru_b_rgate"])),
             ("lru_b_igate", _col_shards(g["lru_b_igate"])), ("lru_lambda", _col_shards(g["lru_lambda"])),
             ("w_mem_kv1", g["w_mem_kv1"].reshape(4, 256, 512)), ("w_out1", g["w_out1"].reshape(4, 256, 1024))],
            [("lru_w_rgate", g["lru_w_rgate"].reshape(-1)), ("lru_w_igate", g["lru_w_igate"].reshape(-1)),
             ("ln_g1", g["ln_g1"].reshape(-1)), ("ln_b1", g["ln_b1"].reshape(-1))], c_idx)

    g0, late_red = _local_step(
        x2, mem2, positions.reshape(s, 1), tgt2, *mla_shards, mla_q_norm, mla_kv_norm, lru_w_rgate[0], lru_w_igate[0],
        ln_g, ln_b, mid_weights, late_weights, _gather_shards(buf_mid), _gather_shards(buf_late), reduce_late)

    early_red = _GradReduce(
        [("mla_w_in", g0["mla_w_in"]), ("mla_w_uq", _col_shards(g0["mla_w_uq"])),
         ("mla_w_ukv", _col_shards(g0["mla_w_ukv"])), ("w_mem_kv0", g0["w_mem_kv0"].reshape(4, 256, 512)),
         ("w_out0", g0["w_out0"].reshape(4, 256, 1024))],
        [("mla_q_norm", g0["mla_q_norm"].reshape(-1)), ("mla_kv_norm", g0["mla_kv_norm"].reshape(-1)),
         ("ln_g0", g0["ln_g0"].reshape(-1)), ("ln_b0", g0["ln_b0"].reshape(-1)), ("loss", g0["loss"].reshape(-1))],
        c_idx, wire_bf16=True)
    early_red.swapped(*_run_exchange(early_red.swap(), "swap_sibling"))
    early_red.scattered(*_run_exchange(early_red.scatter(), "scatter_chips"))
    early_red.shared(*_run_exchange(early_red.share(), "share_reduced"))
    red = {**late_red.reduced(), **early_red.reduced()}
    red["w_mem_kv"] = jnp.concatenate([red["w_mem_kv0"], red["w_mem_kv1"]])
    red["w_out"] = jnp.concatenate([red["w_out0"], red["w_out1"]])
    red["ln_g"] = jnp.concatenate([red["ln_g0"], red["ln_g1"]])
    red["ln_b"] = jnp.concatenate([red["ln_b0"], red["ln_b1"]])

    weights = dict(mla_w_in=mla_w_in, mla_q_norm=mla_q_norm, mla_w_uq=mla_w_uq, mla_kv_norm=mla_kv_norm,
                   mla_w_ukv=mla_w_ukv, lru_w_in=lru_w_in, lru_conv_w=lru_conv_w, lru_conv_b=lru_conv_b,
                   lru_w_rgate=lru_w_rgate, lru_b_rgate=lru_b_rgate, lru_w_igate=lru_w_igate, lru_b_igate=lru_b_igate,
                   lru_lambda=lru_lambda, w_mem_kv=w_mem_kv, w_out=w_out, ln_g=ln_g, ln_b=ln_b)
    m_in = dict(mla_w_in=m_mla_w_in, mla_q_norm=m_mla_q_norm, mla_w_uq=m_mla_w_uq, mla_kv_norm=m_mla_kv_norm,
                mla_w_ukv=m_mla_w_ukv, lru_w_in=m_lru_w_in, lru_conv_w=m_lru_conv_w, lru_conv_b=m_lru_conv_b,
                lru_w_rgate=m_lru_w_rgate, lru_b_rgate=m_lru_b_rgate, lru_w_igate=m_lru_w_igate,
                lru_b_igate=m_lru_b_igate, lru_lambda=m_lru_lambda, w_mem_kv=m_w_mem_kv, w_out=m_w_out, ln_g=m_ln_g,
                ln_b=m_ln_b)
    v_in = dict(mla_w_in=v_mla_w_in, mla_q_norm=v_mla_q_norm, mla_w_uq=v_mla_w_uq, mla_kv_norm=v_mla_kv_norm,
                mla_w_ukv=v_mla_w_ukv, lru_w_in=v_lru_w_in, lru_conv_w=v_lru_conv_w, lru_conv_b=v_lru_conv_b,
                lru_w_rgate=v_lru_w_rgate, lru_b_rgate=v_lru_b_rgate, lru_w_igate=v_lru_w_igate,
                lru_b_igate=v_lru_b_igate, lru_lambda=v_lru_lambda, w_mem_kv=v_w_mem_kv, w_out=v_w_out, ln_g=v_ln_g,
                ln_b=v_ln_b)
    order = ["mla_w_in", "mla_q_norm", "mla_w_uq", "mla_kv_norm", "mla_w_ukv", "lru_w_in", "lru_conv_w", "lru_conv_b",
             "lru_w_rgate", "lru_b_rgate", "lru_w_igate", "lru_b_igate", "lru_lambda", "w_mem_kv", "w_out", "ln_g",
             "ln_b"]
    grads, deltas, new_m, new_v = {}, {}, {}, {}
    for name in order:
        shape = weights[name].shape
        two_d = (math.prod(shape[:-1]), shape[-1])
        g2 = red[name].reshape(two_d)
        d2, m2, v2 = _adamw(name, weights[name].reshape(two_d), g2, m_in[name].reshape(two_d),
                            v_in[name].reshape(two_d))
        grads[name], deltas[name] = g2.reshape(shape), d2.reshape(shape)
        new_m[name], new_v[name] = m2.reshape(shape), v2.reshape(shape)
    return (red["loss"][0], g0["x"][None], *[grads[n] for n in order], *[deltas[n] for n in order],
            *[new_m[n] for n in order], *[new_v[n] for n in order])


def _local_step(x2, mem2, pos_col, tgt2, win0_sh, wuq_sh, wukv_sh, gq, gkv, w_rgate, w_igate, ln_g, ln_b,
                mid_weights, late_weights, gather_mid=None, gather_late=None, reduce_late=None):
    s = x2.shape[0]
    win0p = _win0_aligned(win0_sh)
    wuq_p = jnp.pad(wuq_sh.reshape(4, Q_LORA, 3, 96).transpose(1, 0, 2, 3).reshape(Q_LORA, 12, 96),
                    ((0, 0), (0, 0), (0, 32))).reshape(Q_LORA, QK_W)
    wukv3 = wukv_sh.reshape(4, KV_LORA, 3, 128).transpose(1, 0, 2, 3).reshape(KV_LORA, 12, 128)
    wk_p = jnp.pad(wukv3[:, :, :64], ((0, 0), (0, 0), (0, 64))).reshape(KV_LORA, QK_W)
    wv = wukv3[:, :, 64:].reshape(KV_LORA, TOK_W)
    wr_bd = _block_diag4(w_rgate).astype(BF16)
    wi_bd = _block_diag4(w_igate).astype(BF16)
    half = 16
    inv_freq = ROPE_THETA ** (-jnp.arange(half, dtype=F32) / half)
    inv_lane = jnp.concatenate([jnp.zeros((64,), F32), inv_freq, inv_freq, jnp.zeros((32,), F32)]).reshape(1, HEAD_PAD)

    ctab, satab, sbtab = _rope_tables(pos_col, inv_lane, min(TB_PROJ, s))
    gate0, qm0, cq, ckv, q_p, q_t, k_p, v_b, v_t, *landed = _mla_proj_fwd(
        x2, win0p, gq, gkv, wuq_p, wk_p, wv, ctab, satab, sbtab, exchange=gather_mid)
    wmem, wout0 = mid_weights(landed)
    memkv = _mem_kv(mem2, wmem)
    tok0, lse, *landed = _attn_fwd(q_p, k_p, v_t, exchange=gather_late)
    win1, wout1, cw, cb, br, bi, lam = late_weights(landed)
    g0, b0, g1, b1 = ln_g[0:1], ln_b[0:1], ln_g[1:2], ln_b[1:2]
    h1 = _mix_fwd(tok0, gate0, qm0, memkv[0], wout0, x2, g0, b0)
    u1, gate1, qm1, hs1 = _lru_fwd(h1, win1, cw, cb, wr_bd, br, wi_bd, bi, lam)

    dres1, dtok1, dgate1, dqm1, dwout1, dmemkv1, dg1, db1, loss = _mix_bwd(
        hs1, gate1, qm1, memkv[1], wout1, h1, g1, b1, tgt2, True)
    dh1, dwin1, dcw, dcb, dwr_bd, dbr, dwi_bd, dbi, dlam = _lru_bwd(
        dtok1, dgate1, dqm1, dres1, h1, u1, hs1, win1, cw, cb, wr_bd, br, wi_bd, bi, lam)
    late = {"lru_w_in": dwin1, "lru_conv_w": dcw, "lru_conv_b": dcb, "lru_b_rgate": dbr, "lru_b_igate": dbi,
            "lru_lambda": dlam, "w_mem_kv1": _mem_kv_bwd(mem2, dmemkv1), "w_out1": dwout1,
            "lru_w_rgate": _diag_blocks4(dwr_bd), "lru_w_igate": _diag_blocks4(dwi_bd), "ln_g1": dg1, "ln_b1": db1}
    red = reduce_late(late) if reduce_late is not None else None

    dres0, dtok0, dgate0, dqm0, dwout0, dmemkv0, dg0, db0, _, *got = _mix_bwd(
        tok0, gate0, qm0, memkv[0], wout0, x2, g0, b0, dh1, False, exchange=red.swap() if red else None)
    if red:
        red.swapped(*got)
    dob, dobt, stats = _attn_prep(tok0, dtok0, lse)
    dq_p, dk_t, dv_t, *got = _attn_bwd(q_p, q_t, k_p, v_b, dob, dobt, stats,
                                       exchange=red.scatter() if red else None)
    if red:
        red.scattered(*got)
    if red:
        red.shared(*_run_exchange(red.share(), "share_reduced"))
    dx, dwin0p, dwuq_p, dwk_p, dwv, dgq, dgkv = _mla_proj_bwd(
        x2, cq, ckv, dq_p, dk_t, dv_t, dgate0, dqm0, dres0, win0p, gq, gkv, wuq_p, wk_p, wv,
        ctab, satab, sbtab)

    dwin0 = _win0_shards(dwin0p)
    dwuq = dwuq_p.reshape(Q_LORA, 12, 128)[:, :, :96].reshape(Q_LORA, 1152)
    dwukv = jnp.concatenate([dwk_p.reshape(KV_LORA, 12, 128)[:, :, :64], dwv.reshape(KV_LORA, 12, 64)],
                            axis=2).reshape(KV_LORA, 1536)
    early = {"x": dx, "loss": loss, "mla_w_in": dwin0, "mla_w_uq": dwuq, "mla_w_ukv": dwukv,
             "w_mem_kv0": _mem_kv_bwd(mem2, dmemkv0), "w_out0": dwout0, "mla_q_norm": dgq, "mla_kv_norm": dgkv,
             "ln_g0": dg0, "ln_b0": db0}
    return early, (red if red else late)
```

```python
import functools
import math

import jax
import jax.numpy as jnp
from jax import lax
from jax.experimental import pallas as pl
from jax.experimental.pallas import tpu as pltpu

F32, BF16 = jnp.float32, jnp.bfloat16
MESH = pl.DeviceIdType.MESH

D_MODEL = 1024
N_TOK_HEADS = 12
TOK_W = 768
MEM_W = 256
MEM_LEN = 256
Q_LORA, KV_LORA = 384, 256
HEAD_PAD = 128
QK_W = N_TOK_HEADS * HEAD_PAD
ATT_SCALE = 1.0 / math.sqrt(96.0)
ATT_SCALE_LOG2 = ATT_SCALE * math.log2(math.e)
ROPE_THETA = 10000.0
LRU_C = 8.0
ALPHA = 4.0 ** 0.25
NORM_EPS = 1e-6
ADAM_LR, ADAM_B1, ADAM_B2, ADAM_EPS, ADAM_WD, ADAM_STEP = 0.001, 0.9, 0.999, 1e-08, 0.01, 10

TB_PROJ = 512
TB_PROJ_BWD = 256
TB_MIX = 512
TB_LRU = 256
TQ_ATT = 512
TQ_ATT_FWD = 1024
TK_ATT = 1024
VMEM_LIMIT = 56 * 1024 * 1024


def _mm(a, b):
    return jnp.dot(a.astype(BF16), b.astype(BF16), preferred_element_type=F32)


def _mm_nt(a, b):
    return lax.dot_general(a.astype(BF16), b.astype(BF16), (((1,), (1,)), ((), ())), preferred_element_type=F32)


def _mm_tn(a, b):
    return lax.dot_general(a.astype(BF16), b.astype(BF16), (((0,), (0,)), ((), ())), preferred_element_type=F32)


def _rows(tb, w):
    return pl.BlockSpec((tb, w), lambda i: (i, 0))


def _const(shape):
    n = len(shape)
    return pl.BlockSpec(shape, lambda i: (0,) * n)


def _params(sem=("arbitrary",)):
    return pltpu.CompilerParams(dimension_semantics=sem, vmem_limit_bytes=VMEM_LIMIT)


def _iota(shape, dim):
    return lax.broadcasted_iota(jnp.int32, shape, dim)


def _rope_tables(pos, inv_lane):
    ang = pos.astype(F32) * inv_lane
    lane = _iota(ang.shape, 1)
    cs, sn = jnp.cos(ang), jnp.sin(ang)
    return (jnp.where(lane < 64, 1.0, jnp.where(lane < 96, cs, 0.0)),
            jnp.where((lane >= 64) & (lane < 80), -sn, 0.0), jnp.where((lane >= 80) & (lane < 96), sn, 0.0))


def _rope(t, c, sa, sb):
    return t * c + pltpu.roll(t, 112, 1) * sa + pltpu.roll(t, 16, 1) * sb


def _rope_t(d, c, sa, sb):
    return d * c + pltpu.roll(d * sa, 16, 1) + pltpu.roll(d * sb, 112, 1)


def _rms(c, g):
    r = lax.rsqrt(jnp.mean(c * c, axis=-1, keepdims=True) + NORM_EPS)
    xh = c * r
    return xh * g, xh, r


def _mla_proj_fwd(x, win, gq, gkv, wuq, wukv_k, wukv_v, pos_col, inv_lane, exchange=None):
    s = x.shape[0]
    tb = min(TB_PROJ, s)

    def body(x_ref, win_ref, gq_ref, gkv_ref, wuq_ref, wk_ref, wv_ref, pos_ref, inv_ref,
             gate_ref, qm_ref, cq_ref, ckv_ref, q_ref, qt_ref, k_ref, v_ref, vt_ref, c_ref, sa_ref, sb_ref):
        z = _mm(x_ref[...], win_ref[...])
        gate_ref[...] = z[:, 0:1024]
        qm_ref[...] = z[:, 1024:1280].astype(BF16)
        cq = z[:, 1280:1664]
        ckv = z[:, 1664:1920]
        cq_ref[...] = cq
        ckv_ref[...] = ckv
        c, sa, sb = _rope_tables(pos_ref[...], inv_ref[...])
        c_ref[...], sa_ref[...], sb_ref[...] = c, sa, sb
        nq, _, _ = _rms(cq, gq_ref[...])
        nkv, _, _ = _rms(ckv, gkv_ref[...])
        qf = _mm(nq, wuq_ref[...])
        kf = _mm(nkv, wk_ref[...])
        vf = _mm(nkv, wv_ref[...])
        v_ref[...] = vf.astype(BF16)
        for j in range(N_TOK_HEADS // 2):
            sl = slice(HEAD_PAD * j, HEAD_PAD * (j + 1))
            vt_ref[sl, :] = vf[:, sl].T.astype(BF16)
        kr = _rope(z[:, 1920:2048], c, sa, sb)
        for h in range(N_TOK_HEADS):
            sl = slice(HEAD_PAD * h, HEAD_PAD * (h + 1))
            qh = _rope(qf[:, sl], c, sa, sb) * ATT_SCALE_LOG2
            q_ref[:, sl] = qh.astype(BF16)
            qt_ref[sl, :] = qh.T.astype(BF16)
            k_ref[:, sl] = (kf[:, sl] + kr).astype(BF16)

    outs = (jax.ShapeDtypeStruct((s, 1024), F32), jax.ShapeDtypeStruct((s, MEM_W), BF16),
            jax.ShapeDtypeStruct((s, Q_LORA), F32), jax.ShapeDtypeStruct((s, KV_LORA), F32),
            jax.ShapeDtypeStruct((s, QK_W), BF16), jax.ShapeDtypeStruct((QK_W, s), BF16),
            jax.ShapeDtypeStruct((s, QK_W), BF16),
            jax.ShapeDtypeStruct((s, TOK_W), BF16), jax.ShapeDtypeStruct((TOK_W, s), BF16),
            *[jax.ShapeDtypeStruct((s, HEAD_PAD), F32)] * 3)

    def cols(w):
        return pl.BlockSpec((w, tb), lambda i: (0, i))

    return _run(
        body, name="mla_proj_fwd", grid=(s // tb,), out_shape=outs,
        in_specs=[_rows(tb, 1024), _const((1024, 2048)), _const((1, Q_LORA)), _const((1, KV_LORA)),
                  _const((Q_LORA, QK_W)), _const((KV_LORA, QK_W)), _const((KV_LORA, TOK_W)),
                  _rows(tb, 1), _const((1, HEAD_PAD))],
        out_specs=(_rows(tb, 1024), _rows(tb, MEM_W), _rows(tb, Q_LORA), _rows(tb, KV_LORA),
                   _rows(tb, QK_W), cols(QK_W), _rows(tb, QK_W), _rows(tb, TOK_W), cols(TOK_W),
                   _rows(tb, HEAD_PAD), _rows(tb, HEAD_PAD), _rows(tb, HEAD_PAD)),
        args=(x, win, gq, gkv, wuq, wukv_k, wukv_v, pos_col, inv_lane), sem=("parallel",), exchange=exchange)


def _mla_proj_bwd(x, cq, ckv, dq, dkt, dvt, dgate, dqm, dres, win, gq, gkv, wuq, wukv_k, wukv_v, ctab, satab, sbtab):
    s = x.shape[0]
    tb = min(TB_PROJ_BWD, s)

    def body(x_ref, cq_ref, ckv_ref, dq_ref, dkt_ref, dvt_ref, dgate_ref, dqm_ref, dres_ref, win_ref, gq_ref, gkv_ref,
             wuq_ref, wk_ref, wv_ref, c_ref, sa_ref, sb_ref,
             dx_ref, dwin_ref, dwuq_ref, dwk_ref, dwv_ref, dgq_ref, dgkv_ref):
        @pl.when(pl.program_id(0) == 0)
        def _():
            for r in (dwin_ref, dwuq_ref, dwk_ref, dwv_ref, dgq_ref, dgkv_ref):
                r[...] = jnp.zeros_like(r)

        c, sa, sb = c_ref[...], sa_ref[...], sb_ref[...]
        lane = _iota((tb, HEAD_PAD), 1)
        gq, gkv = gq_ref[...], gkv_ref[...]
        nq, xhq, rq = _rms(cq_ref[...], gq)
        nkv, xhk, rk = _rms(ckv_ref[...], gkv)
        dkp = dkt_ref[...].T * math.log(2.0)
        dqs, dkr = [], jnp.zeros((tb, HEAD_PAD), F32)
        for h in range(N_TOK_HEADS):
            sl = slice(HEAD_PAD * h, HEAD_PAD * (h + 1))
            dqs.append(_rope_t(dq_ref[:, sl], c, sa, sb).astype(BF16))
            dkr = dkr + dkp[:, sl]
        dqf = jnp.concatenate(dqs, axis=1)
        dkr = jnp.where((lane >= 64) & (lane < 96), _rope_t(dkr, c, sa, sb), 0.0)
        dvb = dvt_ref[...].T.astype(BF16)
        dkb = dkp.astype(BF16)
        dnq = _mm_nt(dqf, wuq_ref[...])
        dwuq_ref[...] += _mm_tn(nq, dqf)
        dgq_ref[...] += jnp.sum(dnq * xhq, axis=0, keepdims=True)
        dxh = dnq * gq
        dcq = rq * (dxh - xhq * jnp.mean(dxh * xhq, axis=-1, keepdims=True))
        dnkv = _mm_nt(dkb, wk_ref[...]) + _mm_nt(dvb, wv_ref[...])
        nkvb = nkv.astype(BF16)
        dwk_ref[...] += _mm_tn(nkvb, dkb)
        dwv_ref[...] += _mm_tn(nkvb, dvb)
        dgkv_ref[...] += jnp.sum(dnkv * xhk, axis=0, keepdims=True)
        dxh = dnkv * gkv
        dckv = rk * (dxh - xhk * jnp.mean(dxh * xhk, axis=-1, keepdims=True))
        dz = jnp.concatenate([dgate_ref[...], dqm_ref[...], dcq, dckv, dkr], axis=1).astype(BF16)
        dx_ref[...] = _mm_nt(dz, win_ref[...]) + dres_ref[...]
        dwin_ref[...] += _mm_tn(x_ref[...], dz)

    outs = (jax.ShapeDtypeStruct((s, 1024), F32), jax.ShapeDtypeStruct((1024, 2048), F32),
            jax.ShapeDtypeStruct((Q_LORA, QK_W), F32), jax.ShapeDtypeStruct((KV_LORA, QK_W), F32),
            jax.ShapeDtypeStruct((KV_LORA, TOK_W), F32), jax.ShapeDtypeStruct((1, Q_LORA), F32),
            jax.ShapeDtypeStruct((1, KV_LORA), F32))
    return _run(
        body, name="mla_proj_bwd", grid=(s // tb,), out_shape=outs,
        in_specs=[_rows(tb, 1024), _rows(tb, Q_LORA), _rows(tb, KV_LORA), _rows(tb, QK_W),
                  pl.BlockSpec((QK_W, tb), lambda i: (0, i)), pl.BlockSpec((TOK_W, tb), lambda i: (0, i)),
                  _rows(tb, 1024), _rows(tb, MEM_W), _rows(tb, 1024),
                  _const((1024, 2048)), _const((1, Q_LORA)), _const((1, KV_LORA)),
                  _const((Q_LORA, QK_W)), _const((KV_LORA, QK_W)), _const((KV_LORA, TOK_W)),
                  _rows(tb, HEAD_PAD), _rows(tb, HEAD_PAD), _rows(tb, HEAD_PAD)],
        out_specs=(_rows(tb, 1024), _const((1024, 2048)), _const((Q_LORA, QK_W)), _const((KV_LORA, QK_W)),
                   _const((KV_LORA, TOK_W)), _const((1, Q_LORA)), _const((1, KV_LORA))),
        args=(x, cq, ckv, dq, dkt, dvt, dgate, dqm, dres, win, gq, gkv, wuq, wukv_k, wukv_v, ctab, satab, sbtab),
        sem=("arbitrary",))


def _attn_fwd(q, k, vt, exchange=None):
    s = q.shape[0]
    tq = min(TQ_ATT_FWD, s)
    tk = min(TK_ATT, s)

    def body(q_ref, k_ref, vt_ref, o_ref, lse_ref):
        i = pl.program_id(1)
        nfull = (i * tq) // tk
        krow = _iota((tk, tq), 0)
        qpos = i * tq + _iota((tk, tq), 1)

        def head_tile(hh, st, carry, masked):
            hs = slice(HEAD_PAD * hh, HEAD_PAD * (hh + 1))
            m, l, acc = carry
            sc = _mm_nt(k_ref[pl.ds(st, tk), hs], q_ref[:, hs])
            if masked:
                sc = jnp.where(st + krow <= qpos, sc, -jnp.inf)
            m_new = jnp.maximum(m, jnp.max(sc, axis=0, keepdims=True))
            p = jnp.exp2(sc - m_new)
            a = jnp.exp2(m - m_new)
            l = a * l + jnp.sum(p, axis=0, keepdims=True)
            acc = a * acc + _mm(vt_ref[64 * hh:64 * (hh + 1), pl.ds(st, tk)], p)
            return m_new, l, acc

        def tile(j, carry, masked):
            st = pl.multiple_of(j * tk, tk)
            return tuple(head_tile(hh, st, carry[hh], masked) for hh in range(2))

        def init():
            return (jnp.full((1, tq), -jnp.inf, F32), jnp.zeros((1, tq), F32), jnp.zeros((64, tq), F32))

        carry = lax.fori_loop(0, nfull, functools.partial(tile, masked=False), (init(), init()))
        (ma, la, acca), (mb, lb, accb) = tile(nfull, carry, True)
        o_ref[...] = jnp.concatenate([acca / la, accb / lb], axis=0).T
        lse_ref[...] = jnp.concatenate([jnp.broadcast_to(ma + jnp.log2(la), (64, tq)),
                                        jnp.broadcast_to(mb + jnp.log2(lb), (64, tq))], axis=0).T

    shp = jax.ShapeDtypeStruct((s, TOK_W), F32)
    return _run(
        body, name="attn_fwd", grid=(N_TOK_HEADS // 2, s // tq), out_shape=(shp, shp),
        in_specs=[pl.BlockSpec((tq, 2 * HEAD_PAD), lambda j, i: (i, j)),
                  pl.BlockSpec((s, 2 * HEAD_PAD), lambda j, i: (0, j)),
                  pl.BlockSpec((HEAD_PAD, s), lambda j, i: (j, 0))],
        out_specs=(pl.BlockSpec((tq, HEAD_PAD), lambda j, i: (i, j)),) * 2,
        args=(q, k, vt), sem=("parallel", "arbitrary"), exchange=exchange)


def _attn_stats(o, do, lse_ref, dob_ref, dot_ref, st_ref):
    lane = _iota((o.shape[0], HEAD_PAD), 1)
    dob_ref[...] = do.astype(BF16)
    prod = do * o
    for j in range(N_TOK_HEADS // 2):
        sl = slice(HEAD_PAD * j, HEAD_PAD * (j + 1))
        dot_ref[sl, :] = do[:, sl].T.astype(BF16)
        pj = prod[:, sl]
        da = jnp.sum(jnp.where(lane < 64, pj, 0.0), axis=-1, keepdims=True)
        db = jnp.sum(jnp.where(lane >= 64, pj, 0.0), axis=-1, keepdims=True)
        la = lse_ref[:, HEAD_PAD * j:HEAD_PAD * j + 1]
        lb = lse_ref[:, HEAD_PAD * j + 64:HEAD_PAD * j + 65]
        st_ref[j] = jnp.where(lane == 0, la, jnp.where(lane == 1, lb, jnp.where(lane == 2, da,
                                                                                 jnp.where(lane == 3, db, 0.0))))


def _attn_bwd(q, qt, k, v, dob, dobt, stats, exchange=None):
    s = q.shape[0]
    t = min(TQ_ATT, s)
    nq = s // t

    def body(q_ref, qt_ref, do_ref, dot_ref, st_ref, k_ref, v_ref, dq_ref, dkt_ref, dvt_ref):
        i = pl.program_id(1)

        @pl.when(i == 0)
        def _():
            dkt_ref[...] = jnp.zeros_like(dkt_ref)
            dvt_ref[...] = jnp.zeros_like(dvt_ref)

        lane = _iota((t, HEAD_PAD), 1)
        qpos, kcol = _iota((t, t), 0), _iota((t, t), 1)
        do = do_ref[...]
        stats = st_ref[0]

        def head_tile(hh, ks, dq_acc, masked):
            hs = slice(HEAD_PAD * hh, HEAD_PAD * (hh + 1))
            qh = q_ref[:, hs]
            kh = k_ref[pl.ds(ks, t), hs]
            dom = jnp.where((lane < 64) if hh == 0 else (lane >= 64), do, jnp.zeros_like(do))
            lse = stats[:, hh:hh + 1]
            dlt = stats[:, 2 + hh:3 + hh]
            sc = _mm_nt(qh, kh)
            if masked:
                sc = jnp.where(kcol <= qpos, sc, -jnp.inf)
            p = jnp.exp2(sc - lse)
            dp = _mm_nt(dom, v_ref[pl.ds(ks, t), :])
            ds = (p * (dp - dlt)).astype(BF16)
            dvt_ref[64 * hh:64 * (hh + 1), pl.ds(ks, t)] += _mm(dot_ref[64 * hh:64 * (hh + 1), :], p)
            dkt_ref[HEAD_PAD * hh:HEAD_PAD * hh + 96, pl.ds(ks, t)] += _mm(qt_ref[HEAD_PAD * hh:HEAD_PAD * hh + 96, :], ds)
            return dq_acc + _mm(ds, kh)

        def tile(j, carry, masked):
            ks = pl.multiple_of(j * t, t)
            return tuple(head_tile(hh, ks, carry[hh], masked) for hh in range(2))

        zero = jnp.zeros((t, HEAD_PAD), F32)
        carry = lax.fori_loop(0, i, functools.partial(tile, masked=False), (zero, zero))
        dqa, dqb = tile(i, carry, True)
        dq_ref[...] = jnp.concatenate([dqa, dqb], axis=1) * ATT_SCALE

    return _run(
        body, name="attn_bwd", grid=(N_TOK_HEADS // 2, nq),
        out_shape=(jax.ShapeDtypeStruct((s, QK_W), F32), jax.ShapeDtypeStruct((QK_W, s), F32),
                   jax.ShapeDtypeStruct((TOK_W, s), F32)),
        in_specs=[pl.BlockSpec((t, 2 * HEAD_PAD), lambda j, i: (i, j)),
                  pl.BlockSpec((2 * HEAD_PAD, t), lambda j, i: (j, i)),
                  pl.BlockSpec((t, HEAD_PAD), lambda j, i: (i, j)),
                  pl.BlockSpec((HEAD_PAD, t), lambda j, i: (j, i)),
                  pl.BlockSpec((1, t, HEAD_PAD), lambda j, i: (j, i, 0)),
                  pl.BlockSpec((s, 2 * HEAD_PAD), lambda j, i: (0, j)),
                  pl.BlockSpec((s, HEAD_PAD), lambda j, i: (0, j))],
        out_specs=(pl.BlockSpec((t, 2 * HEAD_PAD), lambda j, i: (i, j)),
                   pl.BlockSpec((2 * HEAD_PAD, s), lambda j, i: (j, 0)),
                   pl.BlockSpec((HEAD_PAD, s), lambda j, i: (j, 0))),
        args=(q, qt, dob, dobt, stats, k, v), sem=("parallel", "arbitrary"), exchange=exchange)


def _mem_kv(mem, wmem):
    def body(m_ref, w_ref, o_ref):
        o_ref[0] = _mm(m_ref[...], w_ref[0]).astype(BF16)

    return pl.pallas_call(
        body, name="mem_kv", grid=(2,), out_shape=jax.ShapeDtypeStruct((2, MEM_LEN, 512), BF16),
        in_specs=[_const((MEM_LEN, 1024)), pl.BlockSpec((1, 1024, 512), lambda l: (l, 0, 0))],
        out_specs=pl.BlockSpec((1, MEM_LEN, 512), lambda l: (l, 0, 0)),
        compiler_params=_params(("parallel",)),
    )(mem, wmem)


def _mem_kv_bwd(mem, dmemkv):
    def body(m_ref, d_ref, o_ref):
        o_ref[...] = _mm_tn(m_ref[...], d_ref[...])

    return pl.pallas_call(
        body, name="mem_kv_bwd", grid=(1,), out_shape=jax.ShapeDtypeStruct((1024, 512), F32),
        in_specs=[_const((MEM_LEN, 1024)), _const((MEM_LEN, 512))], out_specs=_const((1024, 512)),
        compiler_params=_params(("arbitrary",)),
    )(mem, dmemkv)


def _head_mask(lane, sub):
    return (lane < 64) if sub == 0 else (lane >= 64)


def _mem_attn(qm, kv):
    tb = qm.shape[0]
    lane = _iota((tb, HEAD_PAD), 1)
    outs, ps = [], []
    for pp in range(2):
        qp = qm[:, HEAD_PAD * pp:HEAD_PAD * (pp + 1)]
        kp = kv[:, HEAD_PAD * pp:HEAD_PAD * (pp + 1)]
        vp = kv[:, MEM_W + HEAD_PAD * pp:MEM_W + HEAD_PAD * (pp + 1)]
        pair = None
        for sub in range(2):
            qh = jnp.where(_head_mask(lane, sub), qp, jnp.zeros_like(qp))
            sc = _mm_nt(qh, kp) * 0.125
            e = jnp.exp(sc - jnp.max(sc, axis=-1, keepdims=True))
            p = e / jnp.sum(e, axis=-1, keepdims=True)
            o = _mm(p, vp)
            ps.append(p)
            pair = o if sub == 0 else jnp.where(lane < 64, pair, o)
        outs.append(pair)
    return jnp.concatenate(outs, axis=1), ps


def _mem_attn_bwd(dmo, qm, kv, ps):
    tb = qm.shape[0]
    lane = _iota((tb, HEAD_PAD), 1)
    dqs, dks, dvs = [], [], []
    for pp in range(2):
        qp = qm[:, HEAD_PAD * pp:HEAD_PAD * (pp + 1)]
        kp = kv[:, HEAD_PAD * pp:HEAD_PAD * (pp + 1)]
        vp = kv[:, MEM_W + HEAD_PAD * pp:MEM_W + HEAD_PAD * (pp + 1)]
        dop = dmo[:, HEAD_PAD * pp:HEAD_PAD * (pp + 1)]
        dq_pair, dk_pair, dv_pair = None, None, None
        for sub in range(2):
            msk = _head_mask(lane, sub)
            p = ps[2 * pp + sub]
            qh = jnp.where(msk, qp, jnp.zeros_like(qp))
            doh = jnp.where(msk, dop, 0.0).astype(BF16)
            dv = _mm_tn(p, doh)
            dp = _mm_nt(doh, vp)
            ds = (p * (dp - jnp.sum(dp * p, axis=-1, keepdims=True)) * 0.125).astype(BF16)
            dq = _mm(ds, kp)
            dk = _mm_tn(ds, qh)
            if sub == 0:
                dq_pair, dk_pair, dv_pair = dq, dk, dv
            else:
                dq_pair = jnp.where(lane < 64, dq_pair, dq)
                dk_pair, dv_pair = dk_pair + dk, dv_pair + dv
        dqs.append(dq_pair)
        dks.append(dk_pair)
        dvs.append(dv_pair)
    return jnp.concatenate(dqs, axis=1), jnp.concatenate(dks + dvs, axis=1)


def _mix_core(tok, gate, qm, kv, wout, h_in, g, b):
    mem_out, ps = _mem_attn(qm, kv)
    cat = jnp.concatenate([tok, mem_out], axis=1)
    sg = jax.nn.sigmoid(gate)
    sl = gate * sg
    y = cat * sl
    r = ALPHA * h_in + _mm(y, wout)
    mu = jnp.mean(r, axis=-1, keepdims=True)
    xc = r - mu
    rstd = lax.rsqrt(jnp.mean(xc * xc, axis=-1, keepdims=True) + NORM_EPS)
    xh = xc * rstd
    return xh * g + b, (ps, cat, sg, sl, y, xh, rstd)


def _mix_fwd(tok, gate, qm, kv, wout, h_in, g, b):
    s = tok.shape[0]
    tb = min(TB_MIX, s)

    def body(tok_ref, gate_ref, qm_ref, kv_ref, w_ref, h_ref, g_ref, b_ref, o_ref):
        o_ref[...], _ = _mix_core(tok_ref[...], gate_ref[...], qm_ref[...], kv_ref[...], w_ref[...], h_ref[...],
                                  g_ref[...], b_ref[...])

    return pl.pallas_call(
        body, name="mix_fwd", grid=(s // tb,), out_shape=jax.ShapeDtypeStruct((s, 1024), F32),
        in_specs=[_rows(tb, TOK_W), _rows(tb, 1024), _rows(tb, MEM_W), _const((MEM_LEN, 512)), _const((1024, 1024)),
                  _rows(tb, 1024), _const((1, 1024)), _const((1, 1024))],
        out_specs=_rows(tb, 1024), compiler_params=_params(("parallel",)),
    )(tok, gate, qm, kv, wout, h_in, g, b)


def _mix_bwd(tok, gate, qm, kv, wout, h_in, g, b, up, from_loss, lse=None, exchange=None):
    s = tok.shape[0]
    tb = min(TB_MIX, s)
    n_in = 9 if lse is None else 10
    n_tok_out = 1 if lse is None else 3

    def body(*refs):
        tok_ref, gate_ref, qm_ref, kv_ref, w_ref, h_ref, g_ref, b_ref, up_ref = refs[:9]
        dres_ref, tok_out = refs[n_in], refs[n_in + 1:n_in + 1 + n_tok_out]
        dgate_ref, dqm_ref, dw_ref, dkv_ref, dg_ref, db_ref, loss_ref = refs[n_in + 1 + n_tok_out:]

        @pl.when(pl.program_id(0) == 0)
        def _():
            for r in (dw_ref, dkv_ref, dg_ref, db_ref, loss_ref):
                r[...] = jnp.zeros_like(r)

        gate, qm, kv, wout, g = gate_ref[...], qm_ref[...], kv_ref[...], w_ref[...], g_ref[...]
        h_out, (ps, cat, sg, sl, y, xh, rstd) = _mix_core(tok_ref[...], gate, qm, kv, wout, h_ref[...], g, b_ref[...])
        if from_loss:
            diff = h_out - up_ref[...]
            loss_ref[...] += 0.5 * jnp.sum(jnp.mean(diff * diff, axis=-1, keepdims=True), axis=0, keepdims=True)
            dh = diff * (1.0 / D_MODEL)
        else:
            dh = up_ref[...]
        dg_ref[...] += jnp.sum(dh * xh, axis=0, keepdims=True)
        db_ref[...] += jnp.sum(dh, axis=0, keepdims=True)
        dxh = dh * g
        dr = rstd * (dxh - jnp.mean(dxh, axis=-1, keepdims=True) - xh * jnp.mean(dxh * xh, axis=-1, keepdims=True))
        dres_ref[...] = ALPHA * dr
        drb = dr.astype(BF16)
        dy = _mm_nt(drb, wout)
        dw_ref[...] += _mm_tn(y, drb)
        dcat = dy * sl
        dgate_ref[...] = dy * cat * (sg * (1.0 + gate * (1.0 - sg)))
        if lse is None:
            tok_out[0][...] = dcat[:, :TOK_W]
        else:
            _attn_stats(tok_ref[...], dcat[:, :TOK_W], refs[9], *tok_out)
        dqm, dkv = _mem_attn_bwd(dcat[:, TOK_W:], qm, kv, ps)
        dqm_ref[...] = dqm
        dkv_ref[...] += dkv

    npair = N_TOK_HEADS // 2
    tok_shapes = [jax.ShapeDtypeStruct((s, TOK_W), F32)] if lse is None else [
        jax.ShapeDtypeStruct((s, TOK_W), BF16), jax.ShapeDtypeStruct((TOK_W, s), BF16),
        jax.ShapeDtypeStruct((npair, s, HEAD_PAD), F32)]
    tok_specs = [_rows(tb, TOK_W)] if lse is None else [
        _rows(tb, TOK_W), pl.BlockSpec((TOK_W, tb), lambda i: (0, i)),
        pl.BlockSpec((npair, tb, HEAD_PAD), lambda i: (0, i, 0))]
    outs = (jax.ShapeDtypeStruct((s, 1024), F32), *tok_shapes,
            jax.ShapeDtypeStruct((s, 1024), F32), jax.ShapeDtypeStruct((s, MEM_W), F32),
            jax.ShapeDtypeStruct((1024, 1024), F32), jax.ShapeDtypeStruct((MEM_LEN, 512), F32),
            jax.ShapeDtypeStruct((1, 1024), F32), jax.ShapeDtypeStruct((1, 1024), F32),
            jax.ShapeDtypeStruct((1, 1), F32))
    args = (tok, gate, qm, kv, wout, h_in, g, b, up) + (() if lse is None else (lse,))
    return _run(
        body, name="mix_bwd_loss" if from_loss else "mix_bwd", grid=(s // tb,), out_shape=outs,
        in_specs=[_rows(tb, TOK_W), _rows(tb, 1024), _rows(tb, MEM_W), _const((MEM_LEN, 512)), _const((1024, 1024)),
                  _rows(tb, 1024), _const((1, 1024)), _const((1, 1024)), _rows(tb, 1024)]
        + ([] if lse is None else [_rows(tb, TOK_W)]),
        out_specs=(_rows(tb, 1024), *tok_specs, _rows(tb, 1024), _rows(tb, MEM_W), _const((1024, 1024)),
                   _const((MEM_LEN, 512)), _const((1, 1024)), _const((1, 1024)), _const((1, 1))),
        args=args, sem=("arbitrary",), exchange=exchange)


def _shift_down(u, tail, k):
    if k == 0:
        return u
    r = pltpu.roll(u, k, 0)
    row8 = _iota((8, u.shape[1]), 0)
    head = jnp.where(row8 < k, pltpu.roll(tail, k, 0), r[:8])
    return jnp.concatenate([head, r[8:]], axis=0)


def _shift_up(d, head, k):
    if k == 0:
        return d
    n = d.shape[0]
    r = pltpu.roll(d, n - k, 0)
    row8 = _iota((8, d.shape[1]), 0)
    last = jnp.where(row8 >= 8 - k, pltpu.roll(head, 8 - k, 0), r[n - 8:])
    return jnp.concatenate([r[:n - 8], last], axis=0)


def _scan_down(a, b):
    n = a.shape[0]
    row = _iota(a.shape, 0)
    s = 1
    while s < n:
        ok = row >= s
        a_s = jnp.where(ok, pltpu.roll(a, s, 0), 1.0)
        b_s = jnp.where(ok, pltpu.roll(b, s, 0), 0.0)
        b = a * b_s + b
        a = a * a_s
        s *= 2
    return a, b


def _scan_up(a, b):
    n = a.shape[0]
    row = _iota(a.shape, 0)
    s = 1
    while s < n:
        ok = row < n - s
        a_s = jnp.where(ok, pltpu.roll(a, n - s, 0), 1.0)
        b_s = jnp.where(ok, pltpu.roll(b, n - s, 0), 0.0)
        b = a * b_s + b
        a = a * a_s
        s *= 2
    return a, b


def _neg_expm1(x):
    poly = -x * (1.0 + x * (0.5 + x * (1.0 / 6.0 + x * (1.0 / 24.0 + x * (1.0 / 120.0)))))
    return jnp.where(x > -0.1, poly, 1.0 - jnp.exp(x))


def _softplus(x):
    return jnp.maximum(x, 0.0) + jnp.log(1.0 + jnp.exp(-jnp.abs(x)))


def _lru_gates(u, tail, cw, cb, wr, br, wi, bi, lam):
    us = [_shift_down(u, tail, k) for k in range(4)]
    xc = cb + us[3] * cw[0:1] + us[2] * cw[1:2] + us[1] * cw[2:3] + us[0] * cw[3:4]
    xb = xc.astype(BF16)
    pre_r = jnp.concatenate([_mm(xb[:, 256 * g:256 * (g + 1)], wr[g]) for g in range(3)], axis=1) + br
    pre_i = jnp.concatenate([_mm(xb[:, 256 * g:256 * (g + 1)], wi[g]) for g in range(3)], axis=1) + bi
    rg, ig = jax.nn.sigmoid(pre_r), jax.nn.sigmoid(pre_i)
    clam = -LRU_C * _softplus(-lam)
    la = clam * rg
    a = jnp.exp(la)
    mm = jnp.sqrt(_neg_expm1(2.0 * la))
    return us, xc, xb, rg, ig, clam, la, a, mm


def _lru_fwd(h, win, cw, cb, wr, br, wi, bi, lam):
    s = h.shape[0]
    tb = min(TB_LRU, s)

    def body(h_ref, win_ref, cw_ref, cb_ref, wr_ref, br_ref, wi_ref, bi_ref, lam_ref,
             u_ref, gate_ref, qm_ref, hs_ref, tail_sc, carry_sc):
        @pl.when(pl.program_id(0) == 0)
        def _():
            tail_sc[...] = jnp.zeros_like(tail_sc)
            carry_sc[...] = jnp.zeros_like(carry_sc)

        z = _mm(h_ref[...], win_ref[...])
        u = z[:, :TOK_W]
        u_ref[...] = u
        gate_ref[...] = z[:, TOK_W:TOK_W + 1024]
        qm_ref[...] = z[:, TOK_W + 1024:].astype(BF16)
        _, xc, _, _, ig, _, _, a, mm = _lru_gates(u, tail_sc[...], cw_ref[...], cb_ref[...], wr_ref[...], br_ref[...],
                                                 wi_ref[...], bi_ref[...], lam_ref[...])
        big_a, big_b = _scan_down(a, mm * (ig * xc))
        hs = big_a * carry_sc[0:1, :] + big_b
        hs_ref[...] = hs
        tail_sc[...] = u[tb - 8:, :]
        carry_sc[...] = jnp.broadcast_to(hs[tb - 1:tb, :], carry_sc.shape)

    outs = (jax.ShapeDtypeStruct((s, TOK_W), F32), jax.ShapeDtypeStruct((s, 1024), F32),
            jax.ShapeDtypeStruct((s, MEM_W), BF16), jax.ShapeDtypeStruct((s, TOK_W), F32))
    return pl.pallas_call(
        body, name="lru_fwd", grid=(s // tb,), out_shape=outs,
        in_specs=[_rows(tb, 1024), _const((1024, 2048)), _const((4, TOK_W)), _const((1, TOK_W)),
                  _const((3, 256, 256)), _const((1, TOK_W)), _const((3, 256, 256)), _const((1, TOK_W)),
                  _const((1, TOK_W))],
        out_specs=(_rows(tb, TOK_W), _rows(tb, 1024), _rows(tb, MEM_W), _rows(tb, TOK_W)),
        scratch_shapes=[pltpu.VMEM((8, TOK_W), F32), pltpu.VMEM((8, TOK_W), F32)],
        compiler_params=_params(),
    )(h, win, cw, cb, wr, br, wi, bi, lam)


def _lru_bwd(dhs, dgate, dqm, dres, h, u, hs, win, cw, cb, wr, br, wi, bi, lam):
    s = h.shape[0]
    tb = min(TB_LRU, s)
    nb = s // tb

    def rev(w):
        return pl.BlockSpec((tb, w), lambda i: (nb - 1 - i, 0))

    def prev_tail(w):
        return pl.BlockSpec((8, w), lambda i: (jnp.maximum((nb - 1 - i) * (tb // 8) - 1, 0), 0))

    def body(dhs_ref, dgate_ref, dqm_ref, dres_ref, h_ref, u_ref, hs_ref, ut_ref, hst_ref, win_ref, cw_ref, cb_ref,
             wr_ref, br_ref, wi_ref, bi_ref, lam_ref,
             dh_ref, dwin_ref, dcw_ref, dcb_ref, dwr_ref, dbr_ref, dwi_ref, dbi_ref, dlam_ref, ecar_sc, dxc_sc):
        i = pl.program_id(0)

        @pl.when(i == 0)
        def _():
            for r in (dwin_ref, dcw_ref, dcb_ref, dwr_ref, dbr_ref, dwi_ref, dbi_ref, dlam_ref, ecar_sc, dxc_sc):
                r[...] = jnp.zeros_like(r)

        first = (i == nb - 1)
        u = u_ref[...]
        utail = jnp.where(first, 0.0, ut_ref[...])
        hstail = jnp.where(first, 0.0, hst_ref[...])
        cw, wr, wi, lam = cw_ref[...], wr_ref[...], wi_ref[...], lam_ref[...]
        us, xc, xb, rg, ig, clam, la, a, mm = _lru_gates(u, utail, cw, cb_ref[...], wr, br_ref[...], wi, bi_ref[...], lam)
        row = _iota(a.shape, 0)
        a_next = jnp.where(row < tb - 1, pltpu.roll(a, tb - 1, 0), 1.0)
        big_a, big_b = _scan_up(a_next, dhs_ref[...])
        e = big_a * ecar_sc[0:1, :] + big_b
        ecar_sc[...] = jnp.broadcast_to(a[0:1, :] * e[0:1, :], ecar_sc.shape)
        hs_prev = _shift_down(hs_ref[...], hstail, 1)
        da = e * hs_prev
        ix = ig * xc
        dmm = e * ix
        dix = e * mm
        dla = da * a - dmm * (a * a) / mm
        dlam_ref[...] += jnp.sum(dla * rg, axis=0, keepdims=True)
        dpr = (dla * clam) * rg * (1.0 - rg)
        dpi = (dix * xc) * ig * (1.0 - ig)
        dbr_ref[...] += jnp.sum(dpr, axis=0, keepdims=True)
        dbi_ref[...] += jnp.sum(dpi, axis=0, keepdims=True)
        dprb, dpib = dpr.astype(BF16), dpi.astype(BF16)
        dxc_g = []
        for g in range(3):
            sl = slice(256 * g, 256 * (g + 1))
            dwr_ref[g] += _mm_tn(xb[:, sl], dprb[:, sl])
            dwi_ref[g] += _mm_tn(xb[:, sl], dpib[:, sl])
            dxc_g.append(_mm_nt(dprb[:, sl], wr[g]) + _mm_nt(dpib[:, sl], wi[g]))
        dxc = dix * ig + jnp.concatenate(dxc_g, axis=1)
        dcb_ref[...] += jnp.sum(dxc, axis=0, keepdims=True)
        dcw_ref[...] += jnp.concatenate([jnp.sum(dxc * us[3 - tap], axis=0, keepdims=True) for tap in range(4)], axis=0)
        head = dxc_sc[...]
        du = dxc * cw[3:4]
        for k in range(1, 4):
            du = du + _shift_up(dxc, head, k) * cw[3 - k:4 - k]
        dxc_sc[...] = dxc[:8, :]
        dz = jnp.concatenate([du, dgate_ref[...], dqm_ref[...]], axis=1).astype(BF16)
        dh_ref[...] = _mm_nt(dz, win_ref[...]) + dres_ref[...]
        dwin_ref[...] += _mm_tn(h_ref[...], dz)

        @pl.when(i == nb - 1)
        def _():
            dlam_ref[...] = dlam_ref[...] * (LRU_C * jax.nn.sigmoid(-lam))

    outs = (jax.ShapeDtypeStruct((s, 1024), F32), jax.ShapeDtypeStruct((1024, 2048), F32),
            jax.ShapeDtypeStruct((4, TOK_W), F32), jax.ShapeDtypeStruct((1, TOK_W), F32),
            jax.ShapeDtypeStruct((3, 256, 256), F32), jax.ShapeDtypeStruct((1, TOK_W), F32),
            jax.ShapeDtypeStruct((3, 256, 256), F32), jax.ShapeDtypeStruct((1, TOK_W), F32),
            jax.ShapeDtypeStruct((1, TOK_W), F32))
    return pl.pallas_call(
        body, name="lru_bwd", grid=(nb,), out_shape=outs,
        in_specs=[rev(TOK_W), rev(1024), rev(MEM_W), rev(1024), rev(1024), rev(TOK_W), rev(TOK_W),
                  prev_tail(TOK_W), prev_tail(TOK_W),
                  _const((1024, 2048)), _const((4, TOK_W)), _const((1, TOK_W)), _const((3, 256, 256)),
                  _const((1, TOK_W)), _const((3, 256, 256)), _const((1, TOK_W)), _const((1, TOK_W))],
        out_specs=(rev(1024), _const((1024, 2048)), _const((4, TOK_W)), _const((1, TOK_W)), _const((3, 256, 256)),
                   _const((1, TOK_W)), _const((3, 256, 256)), _const((1, TOK_W)), _const((1, TOK_W))),
        scratch_shapes=[pltpu.VMEM((8, TOK_W), F32), pltpu.VMEM((8, TOK_W), F32)],
        compiler_params=_params(),
    )(dhs, dgate, dqm, dres, h, u, hs, u, hs, win, cw, cb, wr, br, wi, bi, lam)


def _adamw(name, w, g, m, v):
    rows, cols = w.shape
    tb = 256 if rows % 256 == 0 else rows

    def body(w_ref, g_ref, m_ref, v_ref, d_ref, nm_ref, nv_ref):
        g = g_ref[...]
        nm = ADAM_B1 * m_ref[...] + (1.0 - ADAM_B1) * g
        nv = ADAM_B2 * v_ref[...] + (1.0 - ADAM_B2) * (g * g)
        m_hat = nm / (1.0 - ADAM_B1 ** ADAM_STEP)
        v_hat = nv / (1.0 - ADAM_B2 ** ADAM_STEP)
        d_ref[...] = -ADAM_LR * (m_hat / (jnp.sqrt(v_hat) + ADAM_EPS) + ADAM_WD * w_ref[...])
        nm_ref[...] = nm
        nv_ref[...] = nv

    shp = jax.ShapeDtypeStruct((rows, cols), F32)
    return pl.pallas_call(
        body, name="adamw_" + name, grid=(rows // tb,), out_shape=(shp, shp, shp),
        in_specs=[_rows(tb, cols)] * 4, out_specs=(_rows(tb, cols),) * 3,
        compiler_params=_params(("parallel",)),
    )(w, g, m, v)


def _row_block(rows, cap=2048):
    return max(t for t in range(8, cap + 1, 8) if rows % t == 0)


def _add2(a, b):
    rows = a.shape[0]
    tb = _row_block(rows)

    def body(a_ref, b_ref, o_ref):
        o_ref[...] = a_ref[...] + b_ref[...]

    return pl.pallas_call(
        body, name="add_sibling", grid=(rows // tb,), out_shape=jax.ShapeDtypeStruct(a.shape, F32),
        in_specs=[_rows(tb, 128)] * 2, out_specs=_rows(tb, 128), compiler_params=_params(("parallel",)),
    )(a, b)


def _sum_slots(landed, own, rows):
    tb = _row_block(rows, 1024)

    def body(l_ref, o_ref, out_ref):
        t = 2 * lax.axis_index("x") + lax.axis_index("y")
        r = [jnp.where(t == s, o_ref[s], l_ref[s].astype(F32)) for s in range(4)]
        out_ref[...] = ((r[0] + r[1]) + r[2]) + r[3]

    return pl.pallas_call(
        body, name="sum_chips", grid=(rows // tb,), out_shape=jax.ShapeDtypeStruct((rows, 128), F32),
        in_specs=[pl.BlockSpec((4, tb, 128), lambda i: (0, i, 0))] * 2, out_specs=_rows(tb, 128),
        compiler_params=_params(("parallel",)),
    )(landed, own)


_ANY = pl.BlockSpec(memory_space=pl.ANY)


def _place():
    x, y, c = lax.axis_index("x"), lax.axis_index("y"), lax.axis_index("c")
    return x, y, c, [(1 - x, y), (x, 1 - y), (1 - x, 1 - y)]


def _remote(src, dst, ssem, rsem, to):
    return pltpu.make_async_remote_copy(src_ref=src, dst_ref=dst, send_sem=ssem, recv_sem=rsem, device_id=to,
                                        device_id_type=MESH)


class _Exchange:
    def __init__(self, ins, out_shape, sems, start, finish):
        self.ins, self.out_shape, self.sems, self.start, self.finish = ins, out_shape, sems, start, finish


def _run(body, *, name, grid, in_specs, out_specs, out_shape, args, scratch=(), sem, exchange=None):
    if exchange is None:
        return pl.pallas_call(body, name=name, grid=grid, out_shape=tuple(out_shape), in_specs=list(in_specs),
                              out_specs=tuple(out_specs), scratch_shapes=list(scratch),
                              compiler_params=_params(sem))(*args)
    n_in, n_out, n_sc = len(args), len(out_shape), len(scratch)
    k_in, k_out = len(exchange.ins), len(exchange.out_shape)

    def fused(*refs):
        ins, refs = refs[:n_in], refs[n_in:]
        xin, refs = refs[:k_in], refs[k_in:]
        outs, refs = refs[:n_out], refs[n_out:]
        xout, refs = refs[:k_out], refs[k_out:]
        sc, xsem = refs[:n_sc], refs[n_sc:]
        first = pl.program_id(0) == 0
        last = pl.program_id(0) == grid[0] - 1
        for a in range(1, len(grid)):
            first = first & (pl.program_id(a) == 0)
            last = last & (pl.program_id(a) == grid[a] - 1)

        @pl.when(first)
        def _():
            exchange.start(xin, xout, xsem)

        body(*ins, *outs, *sc)

        @pl.when(last)
        def _():
            exchange.finish(xin, xout, xsem)

    return pl.pallas_call(
        fused, name=name, grid=grid, out_shape=(*out_shape, *exchange.out_shape),
        in_specs=[*in_specs, *[_ANY] * k_in], out_specs=(*out_specs, *[_ANY] * k_out),
        scratch_shapes=[*scratch, *exchange.sems],
        compiler_params=_params(("arbitrary",) * len(grid)),
    )(*args, *exchange.ins)


def _run_exchange(exchange, name):
    def body(*refs):
        k_in, k_out = len(exchange.ins), len(exchange.out_shape)
        xin, xout, xsem = refs[:k_in], refs[k_in:k_in + k_out], refs[k_in + k_out:]
        exchange.start(xin, xout, xsem)
        exchange.finish(xin, xout, xsem)

    return pl.pallas_call(
        body, name=name, out_shape=tuple(exchange.out_shape), in_specs=[_ANY] * len(exchange.ins),
        out_specs=tuple([_ANY] * len(exchange.out_shape)), scratch_shapes=list(exchange.sems),
    )(*exchange.ins)


def _gather_shards(wsh):
    _, hh, _ = wsh.shape

    def first_hop(w_ref, out_ref, ssems, rsems):
        x, y, c, chips = _place()
        t = 2 * x + y
        return [_remote(w_ref.at[c], out_ref.at[t, c], ssems.at[j], rsems.at[j], (cx, cy, c))
                for j, (cx, cy) in enumerate(chips)]

    def start(xin, xout, xsem):
        for cp in first_hop(xin[0], xout[0], *xsem):
            cp.start()

    def finish(xin, xout, xsem):
        out_ref, (ssems, rsems) = xout[0], xsem
        first = first_hop(xin[0], out_ref, *xsem)
        x, y, c, chips = _place()
        passed = []
        for j, (cx, cy) in enumerate(chips):
            got = out_ref.at[2 * cx + cy, c]
            _remote(got, got, ssems.at[j], rsems.at[j], (cx, cy, c)).wait_recv()
            cp = _remote(got, got, ssems.at[3 + j], rsems.at[3 + j], (x, y, 1 - c))
            cp.start()
            passed.append(cp)
        for j, (cx, cy) in enumerate(chips):
            got = out_ref.at[2 * cx + cy, 1 - c]
            _remote(got, got, ssems.at[3 + j], rsems.at[3 + j], (x, y, 1 - c)).wait_recv()
        for cp in first + passed:
            cp.wait_send()

    return _Exchange([wsh], [jax.ShapeDtypeStruct((4, 2, hh, 128), wsh.dtype)],
                     [pltpu.SemaphoreType.DMA((6,)), pltpu.SemaphoreType.DMA((6,))], start, finish)


def _gathered(landed, own):
    t = 2 * lax.axis_index("x") + lax.axis_index("y")
    return lax.dynamic_update_slice(landed, own[None], (t, 0, 0, 0))


def _swap_sibling(v):
    def copy(xin, xout, xsem):
        x, y, c, _ = _place()
        return _remote(xin[0], xout[0], xsem[0], xsem[1], (x, y, 1 - c))

    return _Exchange([v], [jax.ShapeDtypeStruct(v.shape, v.dtype)],
                     [pltpu.SemaphoreType.DMA, pltpu.SemaphoreType.DMA],
                     lambda *a: copy(*a).start(), lambda *a: copy(*a).wait())


def _scatter_chips(parts):
    n = len(parts)

    def copies(xin, xout, ssems, rsems):
        x, y, c, chips = _place()
        t = 2 * x + y
        return [_remote(xin[k].at[2 * cx + cy], xout[k].at[t], ssems.at[n * j + k], rsems.at[n * j + k], (cx, cy, c))
                for j, (cx, cy) in enumerate(chips) for k in range(n)]

    def start(xin, xout, xsem):
        for cp in copies(xin, xout, *xsem):
            cp.start()

    def finish(xin, xout, xsem):
        ssems, rsems = xsem
        x, y, c, chips = _place()
        for j, (cx, cy) in enumerate(chips):
            for k in range(n):
                got = xout[k].at[2 * cx + cy]
                _remote(got, got, ssems.at[n * j + k], rsems.at[n * j + k], (cx, cy, c)).wait_recv()
        for cp in copies(xin, xout, *xsem):
            cp.wait_send()

    return _Exchange(parts, [jax.ShapeDtypeStruct(a.shape, a.dtype) for a in parts],
                     [pltpu.SemaphoreType.DMA((3 * n,)), pltpu.SemaphoreType.DMA((3 * n,))], start, finish)


def _share_reduced(piece, eighth):
    def copies(t_ref, mine_r, sib_ref, rall_ref, ssems, rsems, lsem):
        x, y, c, _ = _place()
        me = 4 * x + 2 * y + c
        loc = pltpu.make_async_copy(mine_r, rall_ref.at[me], lsem)
        sends = [_remote(t_ref, sib_ref, ssems.at[0], rsems.at[0], (x, y, 1 - c))]
        peers = []
        for mask in range(1, 8):
            px = 1 - x if mask & 4 else x
            py = 1 - y if mask & 2 else y
            pc = 1 - c if mask & 1 else c
            peers.append((mask, px, py, pc))
            sends.append(_remote(mine_r, rall_ref.at[me], ssems.at[mask], rsems.at[mask], (px, py, pc)))
        return loc, sends, peers

    def start(xin, xout, xsem):
        loc, sends, _ = copies(*xin, *xout, *xsem)
        for cp in [loc] + sends:
            cp.start()

    def finish(xin, xout, xsem):
        (sib_ref, rall_ref), (ssems, rsems, _) = xout, xsem
        loc, sends, peers = copies(*xin, *xout, *xsem)
        x, y, c, _ = _place()
        _remote(sib_ref, sib_ref, ssems.at[0], rsems.at[0], (x, y, 1 - c)).wait_recv()
        for mask, px, py, pc in peers:
            got = rall_ref.at[4 * px + 2 * py + pc]
            _remote(got, got, ssems.at[mask], rsems.at[mask], (px, py, pc)).wait_recv()
        for cp in sends:
            cp.wait_send()
        loc.wait()

    return _Exchange([piece, eighth],
                     [jax.ShapeDtypeStruct(piece.shape, F32), jax.ShapeDtypeStruct((8, *eighth.shape), F32)],
                     [pltpu.SemaphoreType.DMA((8,)), pltpu.SemaphoreType.DMA((8,)), pltpu.SemaphoreType.DMA],
                     start, finish)


def _ceil_to(n, m):
    return -(-n // m) * m


def _pack_bf16(parts):
    blocks = [p.reshape(-1, 128) for p in parts]
    rows = jnp.concatenate([jnp.pad(b, ((0, -b.shape[0] % 16), (0, 0))) for b in blocks])
    hw = _ceil_to(rows.shape[0], 32) // 2
    return jnp.pad(rows, ((0, 2 * hw - rows.shape[0]), (0, 0))).reshape(2, hw, 128)


def _segments(wall, parts):
    wall = wall.reshape(4, -1, 128)
    out, row = [], 0
    for p in parts:
        n = p.size // 128
        out.append(wall[:, row:row + n].reshape(4, *p.shape))
        row += _ceil_to(n, 16)
    return out


class _GradReduce:
    def __init__(self, sharded, replicated, c_idx, wire_bf16=False):
        self.c_idx, self.wire_bf16 = c_idx, wire_bf16
        self.rowwise = [(n, g.shape[1:]) for n, g in sharded if math.prod(g.shape[1:]) % 128 == 0]
        self.small = [(n, g.shape[1:]) for n, g in sharded if math.prod(g.shape[1:]) % 128 != 0]
        self.replicated = [(n, g.shape[0]) for n, g in replicated]
        by_name = dict(sharded)
        blocks = [by_name[n].reshape(4, -1, 128) for n, _ in self.rowwise]
        if self.small:
            rest = jnp.concatenate([by_name[n].reshape(4, -1) for n, _ in self.small], axis=1)
            blocks.append(jnp.pad(rest, ((0, 0), (0, -rest.shape[1] % 128))).reshape(4, -1, 128))
        sh = jnp.concatenate([jnp.pad(b, ((0, 0), (0, -b.shape[1] % 8), (0, 0))) for b in blocks], axis=1)
        self.hs = _ceil_to(sh.shape[1], 256) // 2
        sh = jnp.pad(sh, ((0, 0), (0, 2 * self.hs - sh.shape[1]), (0, 0))).reshape(4, 2, self.hs, 128)
        rp = jnp.concatenate([g for _, g in replicated])
        self.rr = _ceil_to(_ceil_to(rp.shape[0], 128) // 128, 64) // 8
        rp = jnp.pad(rp, (0, 8 * self.rr * 128 - rp.shape[0])).reshape(4, 2, self.rr, 128)
        gbuf = jnp.concatenate([sh, rp], axis=2)
        self.hh = self.hs + self.rr
        self.mine = lax.dynamic_index_in_dim(gbuf, c_idx, axis=1, keepdims=False).reshape(4 * self.hh, 128)
        self.other = lax.dynamic_index_in_dim(gbuf, 1 - c_idx, axis=1, keepdims=False).reshape(4 * self.hh, 128)

    def swap(self):
        return _swap_sibling(self.other)

    def swapped(self, got):
        self.chip_sum = _add2(self.mine, got).reshape(4, self.hh, 128)

    def scatter(self):
        self.own_r = self.chip_sum[:, self.hs:]
        if self.wire_bf16:
            return _scatter_chips([self.chip_sum[:, :self.hs].astype(BF16), self.own_r])
        return _scatter_chips([self.chip_sum])

    def scattered(self, landed, landed_r=None):
        if landed_r is None:
            landed_r = landed[:, self.hs:]
        self.piece = _sum_slots(landed, self.chip_sum, self.hs)
        self.eighth = _sum_slots(landed_r, self.own_r, self.rr)

    def share(self):
        return _share_reduced(self.piece, self.eighth)

    def shared(self, sibling, rall):
        mine, sib = self.piece, sibling
        self.shard = jnp.where(self.c_idx == 0, jnp.concatenate([mine, sib]), jnp.concatenate([sib, mine]))
        self.rall = rall

    def reduced(self):
        out, row = {}, 0
        for name, shape in self.rowwise:
            rows = math.prod(shape) // 128
            out[name] = self.shard[row:row + rows].reshape(shape)
            row += _ceil_to(rows, 8)
        for group, flat in ((self.small, self.shard[row:].reshape(-1)), (self.replicated, self.rall.reshape(-1))):
            off = 0
            for name, shape in group:
                n = math.prod(shape) if isinstance(shape, tuple) else shape
                out[name] = flat[off:off + n]
                off += n
        return out


def _col_shards(w2d):
    rows, cols = w2d.shape
    return w2d.reshape(rows, 4, cols // 4).transpose(1, 0, 2)


_WIN0_PARTS = ((0, 384, 1280), (384, 640, 1664), (640, 672, 1984), (672, 1696, 0), (1696, 1952, 1024))


def _win0_aligned(shards):
    def cols(a, b):
        return [shards[s][:, max(a, 488 * s) - 488 * s:min(b, 488 * (s + 1)) - 488 * s]
                for s in range(4) if max(a, 488 * s) < min(b, 488 * (s + 1))]

    zeros = jnp.zeros((1024, 64), shards.dtype)
    return jnp.concatenate(cols(672, 1696) + cols(1696, 1952) + cols(0, 384) + cols(384, 640)
                           + [zeros] + cols(640, 672) + [zeros[:, :32]], axis=1)


def _win0_shards(dwin0p):
    shards = []
    for s in range(4):
        lo, hi = 488 * s, 488 * (s + 1)
        cols = [dwin0p[:, p + max(lo, a) - a:p + min(hi, b) - a] for a, b, p in _WIN0_PARTS if max(lo, a) < min(hi, b)]
        shards.append(jnp.concatenate(cols, axis=1))
    return jnp.stack(shards)


def _block_diag4(w):
    eye = jnp.eye(4, dtype=w.dtype)
    return jnp.einsum("gaij,ab->gaibj", w.reshape(3, 4, 64, 64), eye).reshape(3, 256, 256)


def _diag_blocks4(w):
    w5 = w.reshape(3, 4, 64, 4, 64)
    return jnp.stack([w5[:, a, :, a, :] for a in range(4)], axis=1).reshape(12, 64, 64)


def kernel(x, mem, positions, mla_w_in, mla_q_norm, mla_w_uq, mla_kv_norm, mla_w_ukv, lru_w_in, lru_conv_w, lru_conv_b, lru_w_rgate, lru_b_rgate, lru_w_igate, lru_b_igate, lru_lambda, w_mem_kv, w_out, ln_g, ln_b, loss_target, m_mla_w_in, m_mla_q_norm, m_mla_w_uq, m_mla_kv_norm, m_mla_w_ukv, m_lru_w_in, m_lru_conv_w, m_lru_conv_b, m_lru_w_rgate, m_lru_b_rgate, m_lru_w_igate, m_lru_b_igate, m_lru_lambda, m_w_mem_kv, m_w_out, m_ln_g, m_ln_b, v_mla_w_in, v_mla_q_norm, v_mla_w_uq, v_mla_kv_norm, v_mla_w_ukv, v_lru_w_in, v_lru_conv_w, v_lru_conv_b, v_lru_w_rgate, v_lru_b_rgate, v_lru_w_igate, v_lru_b_igate, v_lru_lambda, v_w_mem_kv, v_w_out, v_ln_g, v_ln_b):
    s = x.shape[1]
    c_idx = lax.axis_index("c")
    x2, mem2, tgt2 = x[0], mem[0], loss_target[0]

    first = [p.astype(BF16) for p in (mla_w_in[0], mla_w_uq[0], mla_w_ukv[0])]
    buf = _pack_bf16(first)
    mla_shards = _segments(_gathered(_run_exchange(_gather_shards(buf), "gather_weights")[0], buf), first)

    mid = [w_mem_kv.astype(BF16), w_out[0].astype(BF16)]
    buf_mid = _pack_bf16(mid)

    def mid_weights(landed):
        wmem, wout0 = _segments(_gathered(landed[0], buf_mid), mid)
        return wmem.transpose(1, 0, 2, 3).reshape(2, 1024, 512), wout0.reshape(1024, 1024)

    small = jnp.concatenate([lru_conv_w[0].reshape(-1), lru_conv_b[0], lru_b_rgate[0], lru_b_igate[0], lru_lambda[0]])
    late = [lru_w_in[0].astype(BF16), w_out[1].astype(BF16), lax.bitcast_convert_type(small, BF16)]
    buf_late = _pack_bf16(late)

    def late_weights(landed):
        win1, wout1, small_bits = _segments(_gathered(landed[0], buf_late), late)
        small_all = lax.bitcast_convert_type(small_bits, F32)
        cw = small_all[:, :768].reshape(4, 4, 192).transpose(1, 0, 2).reshape(4, TOK_W)
        cb, br, bi, lam = (small_all[:, 768 + 192 * k:960 + 192 * k].reshape(1, TOK_W) for k in range(4))
        return win1.transpose(1, 0, 2).reshape(1024, 2048), wout1.reshape(1024, 1024), cw, cb, br, bi, lam

    def reduce_late(g):
        return _GradReduce(
            [("lru_w_in", _col_shards(g["lru_w_in"])), ("lru_conv_w", _col_shards(g["lru_conv_w"])),
             ("lru_conv_b", _col_shards(g["lru_conv_b"])), ("lru_b_rgate", _col_shards(g["lru_b_rgate"])),
             ("lru_b_igate", _col_shards(g["lru_b_igate"])), ("lru_lambda", _col_shards(g["lru_lambda"])),
             ("w_mem_kv1", g["w_mem_kv1"].reshape(4, 256, 512)), ("w_out1", g["w_out1"].reshape(4, 256, 1024))],
            [("lru_w_rgate", g["lru_w_rgate"].reshape(-1)), ("lru_w_igate", g["lru_w_igate"].reshape(-1)),
             ("ln_g1", g["ln_g1"].reshape(-1)), ("ln_b1", g["ln_b1"].reshape(-1))], c_idx)

    g0, late_red = _local_step(
        x2, mem2, positions.reshape(s, 1), tgt2, *mla_shards, mla_q_norm, mla_kv_norm, lru_w_rgate[0], lru_w_igate[0],
        ln_g, ln_b, mid_weights, late_weights, _gather_shards(buf_mid), _gather_shards(buf_late), reduce_late)

    early_red = _GradReduce(
        [("mla_w_in", g0["mla_w_in"]), ("mla_w_uq", _col_shards(g0["mla_w_uq"])),
         ("mla_w_ukv", _col_shards(g0["mla_w_ukv"])), ("w_mem_kv0", g0["w_mem_kv0"].reshape(4, 256, 512)),
         ("w_out0", g0["w_out0"].reshape(4, 256, 1024))],
        [("mla_q_norm", g0["mla_q_norm"].reshape(-1)), ("mla_kv_norm", g0["mla_kv_norm"].reshape(-1)),
         ("ln_g0", g0["ln_g0"].reshape(-1)), ("ln_b0", g0["ln_b0"].reshape(-1)), ("loss", g0["loss"].reshape(-1))],
        c_idx, wire_bf16=True)
    early_red.swapped(*_run_exchange(early_red.swap(), "swap_sibling"))
    early_red.scattered(*_run_exchange(early_red.scatter(), "scatter_chips"))
    early_red.shared(*_run_exchange(early_red.share(), "share_reduced"))
    red = {**late_red.reduced(), **early_red.reduced()}
    red["w_mem_kv"] = jnp.concatenate([red["w_mem_kv0"], red["w_mem_kv1"]])
    red["w_out"] = jnp.concatenate([red["w_out0"], red["w_out1"]])
    red["ln_g"] = jnp.concatenate([red["ln_g0"], red["ln_g1"]])
    red["ln_b"] = jnp.concatenate([red["ln_b0"], red["ln_b1"]])

    weights = dict(mla_w_in=mla_w_in, mla_q_norm=mla_q_norm, mla_w_uq=mla_w_uq, mla_kv_norm=mla_kv_norm,
                   mla_w_ukv=mla_w_ukv, lru_w_in=lru_w_in, lru_conv_w=lru_conv_w, lru_conv_b=lru_conv_b,
                   lru_w_rgate=lru_w_rgate, lru_b_rgate=lru_b_rgate, lru_w_igate=lru_w_igate, lru_b_igate=lru_b_igate,
                   lru_lambda=lru_lambda, w_mem_kv=w_mem_kv, w_out=w_out, ln_g=ln_g, ln_b=ln_b)
    m_in = dict(mla_w_in=m_mla_w_in, mla_q_norm=m_mla_q_norm, mla_w_uq=m_mla_w_uq, mla_kv_norm=m_mla_kv_norm,
                mla_w_ukv=m_mla_w_ukv, lru_w_in=m_lru_w_in, lru_conv_w=m_lru_conv_w, lru_conv_b=m_lru_conv_b,
                lru_w_rgate=m_lru_w_rgate, lru_b_rgate=m_lru_b_rgate, lru_w_igate=m_lru_w_igate,
                lru_b_igate=m_lru_b_igate, lru_lambda=m_lru_lambda, w_mem_kv=m_w_mem_kv, w_out=m_w_out, ln_g=m_ln_g,
                ln_b=m_ln_b)
    v_in = dict(mla_w_in=v_mla_w_in, mla_q_norm=v_mla_q_norm, mla_w_uq=v_mla_w_uq, mla_kv_norm=v_mla_kv_norm,
                mla_w_ukv=v_mla_w_ukv, lru_w_in=v_lru_w_in, lru_conv_w=v_lru_conv_w, lru_conv_b=v_lru_conv_b,
                lru_w_rgate=v_lru_w_rgate, lru_b_rgate=v_lru_b_rgate, lru_w_igate=v_lru_w_igate,
                lru_b_igate=v_lru_b_igate, lru_lambda=v_lru_lambda, w_mem_kv=v_w_mem_kv, w_out=v_w_out, ln_g=v_ln_g,
                ln_b=v_ln_b)
    order = ["mla_w_in", "mla_q_norm", "mla_w_uq", "mla_kv_norm", "mla_w_ukv", "lru_w_in", "lru_conv_w", "lru_conv_b",
             "lru_w_rgate", "lru_b_rgate", "lru_w_igate", "lru_b_igate", "lru_lambda", "w_mem_kv", "w_out", "ln_g",
             "ln_b"]
    grads, deltas, new_m, new_v = {}, {}, {}, {}
    for name in order:
        shape = weights[name].shape
        two_d = (math.prod(shape[:-1]), shape[-1])
        g2 = red[name].reshape(two_d)
        d2, m2, v2 = _adamw(name, weights[name].reshape(two_d), g2, m_in[name].reshape(two_d),
                            v_in[name].reshape(two_d))
        grads[name], deltas[name] = g2.reshape(shape), d2.reshape(shape)
        new_m[name], new_v[name] = m2.reshape(shape), v2.reshape(shape)
    return (red["loss"][0], g0["x"][None], *[grads[n] for n in order], *[deltas[n] for n in order],
            *[new_m[n] for n in order], *[new_v[n] for n in order])


def _local_step(x2, mem2, pos_col, tgt2, win0_sh, wuq_sh, wukv_sh, gq, gkv, w_rgate, w_igate, ln_g, ln_b,
                mid_weights, late_weights, gather_mid=None, gather_late=None, reduce_late=None):
    s = x2.shape[0]
    win0p = _win0_aligned(win0_sh)
    wuq_p = jnp.pad(wuq_sh.reshape(4, Q_LORA, 3, 96).transpose(1, 0, 2, 3).reshape(Q_LORA, 12, 96),
                    ((0, 0), (0, 0), (0, 32))).reshape(Q_LORA, QK_W)
    wukv3 = wukv_sh.reshape(4, KV_LORA, 3, 128).transpose(1, 0, 2, 3).reshape(KV_LORA, 12, 128)
    wk_p = jnp.pad(wukv3[:, :, :64], ((0, 0), (0, 0), (0, 64))).reshape(KV_LORA, QK_W)
    wv = wukv3[:, :, 64:].reshape(KV_LORA, TOK_W)
    wr_bd = _block_diag4(w_rgate).astype(BF16)
    wi_bd = _block_diag4(w_igate).astype(BF16)
    half = 16
    inv_freq = ROPE_THETA ** (-jnp.arange(half, dtype=F32) / half)
    inv_lane = jnp.concatenate([jnp.zeros((64,), F32), inv_freq, inv_freq, jnp.zeros((32,), F32)]).reshape(1, HEAD_PAD)

    gate0, qm0, cq, ckv, q_p, q_t, k_p, v_b, v_t, ctab, satab, sbtab, *landed = _mla_proj_fwd(
        x2, win0p, gq, gkv, wuq_p, wk_p, wv, pos_col, inv_lane, exchange=gather_mid)
    wmem, wout0 = mid_weights(landed)
    memkv = _mem_kv(mem2, wmem)
    tok0, lse, *landed = _attn_fwd(q_p, k_p, v_t, exchange=gather_late)
    win1, wout1, cw, cb, br, bi, lam = late_weights(landed)
    g0, b0, g1, b1 = ln_g[0:1], ln_b[0:1], ln_g[1:2], ln_b[1:2]
    h1 = _mix_fwd(tok0, gate0, qm0, memkv[0], wout0, x2, g0, b0)
    u1, gate1, qm1, hs1 = _lru_fwd(h1, win1, cw, cb, wr_bd, br, wi_bd, bi, lam)

    dres1, dtok1, dgate1, dqm1, dwout1, dmemkv1, dg1, db1, loss = _mix_bwd(
        hs1, gate1, qm1, memkv[1], wout1, h1, g1, b1, tgt2, True)
    dh1, dwin1, dcw, dcb, dwr_bd, dbr, dwi_bd, dbi, dlam = _lru_bwd(
        dtok1, dgate1, dqm1, dres1, h1, u1, hs1, win1, cw, cb, wr_bd, br, wi_bd, bi, lam)
    late = {"lru_w_in": dwin1, "lru_conv_w": dcw, "lru_conv_b": dcb, "lru_b_rgate": dbr, "lru_b_igate": dbi,
            "lru_lambda": dlam, "w_mem_kv1": _mem_kv_bwd(mem2, dmemkv1), "w_out1": dwout1,
            "lru_w_rgate": _diag_blocks4(dwr_bd), "lru_w_igate": _diag_blocks4(dwi_bd), "ln_g1": dg1, "ln_b1": db1}
    red = reduce_late(late) if reduce_late is not None else None

    dres0, dob, dobt, stats, dgate0, dqm0, dwout0, dmemkv0, dg0, db0, _, *got = _mix_bwd(
        tok0, gate0, qm0, memkv[0], wout0, x2, g0, b0, dh1, False, lse=lse, exchange=red.swap() if red else None)
    if red:
        red.swapped(*got)
    dq_p, dk_t, dv_t, *got = _attn_bwd(q_p, q_t, k_p, v_b, dob, dobt, stats,
                                       exchange=red.scatter() if red else None)
    if red:
        red.scattered(*got)
    if red:
        red.shared(*_run_exchange(red.share(), "share_reduced"))
    dx, dwin0p, dwuq_p, dwk_p, dwv, dgq, dgkv = _mla_proj_bwd(
        x2, cq, ckv, dq_p, dk_t, dv_t, dgate0, dqm0, dres0, win0p, gq, gkv, wuq_p, wk_p, wv,
        ctab, satab, sbtab)

    dwin0 = _win0_shards(dwin0p)
    dwuq = dwuq_p.reshape(Q_LORA, 12, 128)[:, :, :96].reshape(Q_LORA, 1152)
    dwukv = jnp.concatenate([dwk_p.reshape(KV_LORA, 12, 128)[:, :, :64], dwv.reshape(KV_LORA, 12, 64)],
                            axis=2).reshape(KV_LORA, 1536)
    early = {"x": dx, "loss": loss, "mla_w_in": dwin0, "mla_w_uq": dwuq, "mla_w_ukv": dwukv,
             "w_mem_kv0": _mem_kv_bwd(mem2, dmemkv0), "w_out0": dwout0, "mla_q_norm": dgq, "mla_kv_norm": dgkv,
             "ln_g0": dg0, "ln_b0": db0}
    return early, (red if red else late)
```

```python
import functools
import math

import jax
import jax.numpy as jnp
from jax import lax
from jax.experimental import pallas as pl
from jax.experimental.pallas import tpu as pltpu

F32, BF16 = jnp.float32, jnp.bfloat16
MESH = pl.DeviceIdType.MESH

D_MODEL = 1024
N_TOK_HEADS = 12
TOK_W = 768
MEM_W = 256
MEM_LEN = 256
Q_LORA, KV_LORA = 384, 256
HEAD_PAD = 128
QK_W = N_TOK_HEADS * HEAD_PAD
ATT_SCALE = 1.0 / math.sqrt(96.0)
ATT_SCALE_LOG2 = ATT_SCALE * math.log2(math.e)
ROPE_THETA = 10000.0
LRU_C = 8.0
ALPHA = 4.0 ** 0.25
NORM_EPS = 1e-6
ADAM_LR, ADAM_B1, ADAM_B2, ADAM_EPS, ADAM_WD, ADAM_STEP = 0.001, 0.9, 0.999, 1e-08, 0.01, 10

TB_PROJ = 512
TB_PROJ_BWD = 256
TB_MIX = 512
TB_LRU = 256
TQ_ATT = 512
TQ_ATT_FWD = 1024
TK_ATT = 1024
VMEM_LIMIT = 56 * 1024 * 1024


def _mm(a, b):
    return jnp.dot(a.astype(BF16), b.astype(BF16), preferred_element_type=F32)


def _mm_nt(a, b):
    return lax.dot_general(a.astype(BF16), b.astype(BF16), (((1,), (1,)), ((), ())), preferred_element_type=F32)


def _mm_tn(a, b):
    return lax.dot_general(a.astype(BF16), b.astype(BF16), (((0,), (0,)), ((), ())), preferred_element_type=F32)


def _rows(tb, w):
    return pl.BlockSpec((tb, w), lambda i: (i, 0))


def _const(shape):
    n = len(shape)
    return pl.BlockSpec(shape, lambda i: (0,) * n)


def _params(sem=("arbitrary",)):
    return pltpu.CompilerParams(dimension_semantics=sem, vmem_limit_bytes=VMEM_LIMIT)


def _iota(shape, dim):
    return lax.broadcasted_iota(jnp.int32, shape, dim)


def _rope_tables(pos, inv_lane):
    ang = pos.astype(F32) * inv_lane
    lane = _iota(ang.shape, 1)
    cs, sn = jnp.cos(ang), jnp.sin(ang)
    return (jnp.where(lane < 64, 1.0, jnp.where(lane < 96, cs, 0.0)),
            jnp.where((lane >= 64) & (lane < 80), -sn, 0.0), jnp.where((lane >= 80) & (lane < 96), sn, 0.0))


def _rope(t, c, sa, sb):
    return t * c + pltpu.roll(t, 112, 1) * sa + pltpu.roll(t, 16, 1) * sb


def _rope_t(d, c, sa, sb):
    return d * c + pltpu.roll(d * sa, 16, 1) + pltpu.roll(d * sb, 112, 1)


def _rms(c, g):
    r = lax.rsqrt(jnp.mean(c * c, axis=-1, keepdims=True) + NORM_EPS)
    xh = c * r
    return xh * g, xh, r


def _mla_proj_fwd(x, win, gq, gkv, wuq, wukv_k, wukv_v, pos_col, inv_lane, exchange=None):
    s = x.shape[0]
    tb = min(TB_PROJ, s)

    def body(x_ref, win_ref, gq_ref, gkv_ref, wuq_ref, wk_ref, wv_ref, pos_ref, inv_ref,
             gate_ref, qm_ref, cq_ref, ckv_ref, q_ref, qt_ref, k_ref, v_ref, vt_ref, c_ref, sa_ref, sb_ref):
        z = _mm(x_ref[...], win_ref[...])
        gate_ref[...] = z[:, 0:1024]
        qm_ref[...] = z[:, 1024:1280].astype(BF16)
        cq = z[:, 1280:1664]
        ckv = z[:, 1664:1920]
        cq_ref[...] = cq
        ckv_ref[...] = ckv
        c, sa, sb = _rope_tables(pos_ref[...], inv_ref[...])
        c_ref[...], sa_ref[...], sb_ref[...] = c, sa, sb
        nq, _, _ = _rms(cq, gq_ref[...])
        nkv, _, _ = _rms(ckv, gkv_ref[...])
        qf = _mm(nq, wuq_ref[...])
        kf = _mm(nkv, wk_ref[...])
        vf = _mm(nkv, wv_ref[...])
        v_ref[...] = vf.astype(BF16)
        for j in range(N_TOK_HEADS // 2):
            sl = slice(HEAD_PAD * j, HEAD_PAD * (j + 1))
            vt_ref[sl, :] = vf[:, sl].T.astype(BF16)
        kr = _rope(z[:, 1920:2048], c, sa, sb)
        for h in range(N_TOK_HEADS):
            sl = slice(HEAD_PAD * h, HEAD_PAD * (h + 1))
            qh = _rope(qf[:, sl], c, sa, sb) * ATT_SCALE_LOG2
            q_ref[:, sl] = qh.astype(BF16)
            qt_ref[sl, :] = qh.T.astype(BF16)
            k_ref[:, sl] = (kf[:, sl] + kr).astype(BF16)

    outs = (jax.ShapeDtypeStruct((s, 1024), F32), jax.ShapeDtypeStruct((s, MEM_W), BF16),
            jax.ShapeDtypeStruct((s, Q_LORA), F32), jax.ShapeDtypeStruct((s, KV_LORA), F32),
            jax.ShapeDtypeStruct((s, QK_W), BF16), jax.ShapeDtypeStruct((QK_W, s), BF16),
            jax.ShapeDtypeStruct((s, QK_W), BF16),
            jax.ShapeDtypeStruct((s, TOK_W), BF16), jax.ShapeDtypeStruct((TOK_W, s), BF16),
            *[jax.ShapeDtypeStruct((s, HEAD_PAD), F32)] * 3)

    def cols(w):
        return pl.BlockSpec((w, tb), lambda i: (0, i))

    return _run(
        body, name="mla_proj_fwd", grid=(s // tb,), out_shape=outs,
        in_specs=[_rows(tb, 1024), _const((1024, 2048)), _const((1, Q_LORA)), _const((1, KV_LORA)),
                  _const((Q_LORA, QK_W)), _const((KV_LORA, QK_W)), _const((KV_LORA, TOK_W)),
                  _rows(tb, 1), _const((1, HEAD_PAD))],
        out_specs=(_rows(tb, 1024), _rows(tb, MEM_W), _rows(tb, Q_LORA), _rows(tb, KV_LORA),
                   _rows(tb, QK_W), cols(QK_W), _rows(tb, QK_W), _rows(tb, TOK_W), cols(TOK_W),
                   _rows(tb, HEAD_PAD), _rows(tb, HEAD_PAD), _rows(tb, HEAD_PAD)),
        args=(x, win, gq, gkv, wuq, wukv_k, wukv_v, pos_col, inv_lane), sem=("parallel",), exchange=exchange)


def _mla_proj_bwd(x, cq, ckv, dq, dkt, dvt, dgate, dqm, dres, win, gq, gkv, wuq, wukv_k, wukv_v, ctab, satab, sbtab):
    s = x.shape[0]
    tb = min(TB_PROJ_BWD, s)

    def body(x_ref, cq_ref, ckv_ref, dq_ref, dkt_ref, dvt_ref, dgate_ref, dqm_ref, dres_ref, win_ref, gq_ref, gkv_ref,
             wuq_ref, wk_ref, wv_ref, c_ref, sa_ref, sb_ref,
             dx_ref, dwin_ref, dwuq_ref, dwk_ref, dwv_ref, dgq_ref, dgkv_ref):
        @pl.when(pl.program_id(0) == 0)
        def _():
            for r in (dwin_ref, dwuq_ref, dwk_ref, dwv_ref, dgq_ref, dgkv_ref):
                r[...] = jnp.zeros_like(r)

        c, sa, sb = c_ref[...], sa_ref[...], sb_ref[...]
        lane = _iota((tb, HEAD_PAD), 1)
        gq, gkv = gq_ref[...], gkv_ref[...]
        nq, xhq, rq = _rms(cq_ref[...], gq)
        nkv, xhk, rk = _rms(ckv_ref[...], gkv)
        dkp = dkt_ref[...].T * math.log(2.0)
        dqs, dkr = [], jnp.zeros((tb, HEAD_PAD), F32)
        for h in range(N_TOK_HEADS):
            sl = slice(HEAD_PAD * h, HEAD_PAD * (h + 1))
            dqs.append(_rope_t(dq_ref[:, sl], c, sa, sb).astype(BF16))
            dkr = dkr + dkp[:, sl]
        dqf = jnp.concatenate(dqs, axis=1)
        dkr = jnp.where((lane >= 64) & (lane < 96), _rope_t(dkr, c, sa, sb), 0.0)
        dvb = dvt_ref[...].T.astype(BF16)
        dkb = dkp.astype(BF16)
        dnq = _mm_nt(dqf, wuq_ref[...])
        dwuq_ref[...] += _mm_tn(nq, dqf)
        dgq_ref[...] += jnp.sum(dnq * xhq, axis=0, keepdims=True)
        dxh = dnq * gq
        dcq = rq * (dxh - xhq * jnp.mean(dxh * xhq, axis=-1, keepdims=True))
        dnkv = _mm_nt(dkb, wk_ref[...]) + _mm_nt(dvb, wv_ref[...])
        nkvb = nkv.astype(BF16)
        dwk_ref[...] += _mm_tn(nkvb, dkb)
        dwv_ref[...] += _mm_tn(nkvb, dvb)
        dgkv_ref[...] += jnp.sum(dnkv * xhk, axis=0, keepdims=True)
        dxh = dnkv * gkv
        dckv = rk * (dxh - xhk * jnp.mean(dxh * xhk, axis=-1, keepdims=True))
        dz = jnp.concatenate([dgate_ref[...], dqm_ref[...], dcq, dckv, dkr], axis=1).astype(BF16)
        dx_ref[...] = _mm_nt(dz, win_ref[...]) + dres_ref[...]
        dwin_ref[...] += _mm_tn(x_ref[...], dz)

    outs = (jax.ShapeDtypeStruct((s, 1024), F32), jax.ShapeDtypeStruct((1024, 2048), F32),
            jax.ShapeDtypeStruct((Q_LORA, QK_W), F32), jax.ShapeDtypeStruct((KV_LORA, QK_W), F32),
            jax.ShapeDtypeStruct((KV_LORA, TOK_W), F32), jax.ShapeDtypeStruct((1, Q_LORA), F32),
            jax.ShapeDtypeStruct((1, KV_LORA), F32))
    return _run(
        body, name="mla_proj_bwd", grid=(s // tb,), out_shape=outs,
        in_specs=[_rows(tb, 1024), _rows(tb, Q_LORA), _rows(tb, KV_LORA), _rows(tb, QK_W),
                  pl.BlockSpec((QK_W, tb), lambda i: (0, i)), pl.BlockSpec((TOK_W, tb), lambda i: (0, i)),
                  _rows(tb, 1024), _rows(tb, MEM_W), _rows(tb, 1024),
                  _const((1024, 2048)), _const((1, Q_LORA)), _const((1, KV_LORA)),
                  _const((Q_LORA, QK_W)), _const((KV_LORA, QK_W)), _const((KV_LORA, TOK_W)),
                  _rows(tb, HEAD_PAD), _rows(tb, HEAD_PAD), _rows(tb, HEAD_PAD)],
        out_specs=(_rows(tb, 1024), _const((1024, 2048)), _const((Q_LORA, QK_W)), _const((KV_LORA, QK_W)),
                   _const((KV_LORA, TOK_W)), _const((1, Q_LORA)), _const((1, KV_LORA))),
        args=(x, cq, ckv, dq, dkt, dvt, dgate, dqm, dres, win, gq, gkv, wuq, wukv_k, wukv_v, ctab, satab, sbtab),
        sem=("arbitrary",))


def _attn_fwd(q, k, vt, exchange=None):
    s = q.shape[0]
    tq = min(TQ_ATT_FWD, s)
    tk = min(TK_ATT, s)

    def body(q_ref, k_ref, vt_ref, o_ref, lse_ref):
        i = pl.program_id(1)
        nfull = (i * tq) // tk
        krow = _iota((tk, tq), 0)
        qpos = i * tq + _iota((tk, tq), 1)

        def head_tile(hh, st, carry, masked):
            hs = slice(HEAD_PAD * hh, HEAD_PAD * (hh + 1))
            m, l, acc = carry
            sc = _mm_nt(k_ref[pl.ds(st, tk), hs], q_ref[:, hs])
            if masked:
                sc = jnp.where(st + krow <= qpos, sc, -jnp.inf)
            m_new = jnp.maximum(m, jnp.max(sc, axis=0, keepdims=True))
            p = jnp.exp2(sc - m_new)
            a = jnp.exp2(m - m_new)
            l = a * l + jnp.sum(p, axis=0, keepdims=True)
            acc = a * acc + _mm(vt_ref[64 * hh:64 * (hh + 1), pl.ds(st, tk)], p)
            return m_new, l, acc

        def tile(j, carry, masked):
            st = pl.multiple_of(j * tk, tk)
            return tuple(head_tile(hh, st, carry[hh], masked) for hh in range(2))

        def init():
            return (jnp.full((1, tq), -jnp.inf, F32), jnp.zeros((1, tq), F32), jnp.zeros((64, tq), F32))

        carry = lax.fori_loop(0, nfull, functools.partial(tile, masked=False), (init(), init()))
        (ma, la, acca), (mb, lb, accb) = tile(nfull, carry, True)
        o_ref[...] = jnp.concatenate([acca / la, accb / lb], axis=0).T
        lse_ref[...] = jnp.concatenate([jnp.broadcast_to(ma + jnp.log2(la), (64, tq)),
                                        jnp.broadcast_to(mb + jnp.log2(lb), (64, tq))], axis=0).T

    shp = jax.ShapeDtypeStruct((s, TOK_W), F32)
    return _run(
        body, name="attn_fwd", grid=(N_TOK_HEADS // 2, s // tq), out_shape=(shp, shp),
        in_specs=[pl.BlockSpec((tq, 2 * HEAD_PAD), lambda j, i: (i, j)),
                  pl.BlockSpec((s, 2 * HEAD_PAD), lambda j, i: (0, j)),
                  pl.BlockSpec((HEAD_PAD, s), lambda j, i: (j, 0))],
        out_specs=(pl.BlockSpec((tq, HEAD_PAD), lambda j, i: (i, j)),) * 2,
        args=(q, k, vt), sem=("parallel", "arbitrary"), exchange=exchange)


def _attn_stats(o, do, lse_ref, dob_ref, dot_ref, st_ref):
    lane = _iota((o.shape[0], HEAD_PAD), 1)
    dob_ref[...] = do.astype(BF16)
    prod = do * o
    for j in range(N_TOK_HEADS // 2):
        sl = slice(HEAD_PAD * j, HEAD_PAD * (j + 1))
        dot_ref[sl, :] = do[:, sl].T.astype(BF16)
        pj = prod[:, sl]
        da = jnp.sum(jnp.where(lane < 64, pj, 0.0), axis=-1, keepdims=True)
        db = jnp.sum(jnp.where(lane >= 64, pj, 0.0), axis=-1, keepdims=True)
        la = lse_ref[:, HEAD_PAD * j:HEAD_PAD * j + 1]
        lb = lse_ref[:, HEAD_PAD * j + 64:HEAD_PAD * j + 65]
        st_ref[j] = jnp.where(lane == 0, la, jnp.where(lane == 1, lb, jnp.where(lane == 2, da,
                                                                                 jnp.where(lane == 3, db, 0.0))))


def _attn_bwd(q, qt, k, v, dob, dobt, stats, exchange=None):
    s = q.shape[0]
    t = min(TQ_ATT, s)
    nq = s // t

    def body(q_ref, qt_ref, do_ref, dot_ref, st_ref, k_ref, v_ref, dq_ref, dkt_ref, dvt_ref):
        i = pl.program_id(1)

        @pl.when(i == 0)
        def _():
            dkt_ref[...] = jnp.zeros_like(dkt_ref)
            dvt_ref[...] = jnp.zeros_like(dvt_ref)

        lane = _iota((t, HEAD_PAD), 1)
        qpos, kcol = _iota((t, t), 0), _iota((t, t), 1)
        do = do_ref[...]
        stats = st_ref[0]

        def head_tile(hh, ks, dq_acc, masked):
            hs = slice(HEAD_PAD * hh, HEAD_PAD * (hh + 1))
            qh = q_ref[:, hs]
            kh = k_ref[pl.ds(ks, t), hs]
            dom = jnp.where((lane < 64) if hh == 0 else (lane >= 64), do, jnp.zeros_like(do))
            lse = stats[:, hh:hh + 1]
            dlt = stats[:, 2 + hh:3 + hh]
            sc = _mm_nt(qh, kh)
            if masked:
                sc = jnp.where(kcol <= qpos, sc, -jnp.inf)
            p = jnp.exp2(sc - lse)
            dp = _mm_nt(dom, v_ref[pl.ds(ks, t), :])
            ds = (p * (dp - dlt)).astype(BF16)
            dvt_ref[64 * hh:64 * (hh + 1), pl.ds(ks, t)] += _mm(dot_ref[64 * hh:64 * (hh + 1), :], p)
            dkt_ref[HEAD_PAD * hh:HEAD_PAD * hh + 96, pl.ds(ks, t)] += _mm(qt_ref[HEAD_PAD * hh:HEAD_PAD * hh + 96, :], ds)
            return dq_acc + _mm(ds, kh)

        def tile(j, carry, masked):
            ks = pl.multiple_of(j * t, t)
            return tuple(head_tile(hh, ks, carry[hh], masked) for hh in range(2))

        zero = jnp.zeros((t, HEAD_PAD), F32)
        carry = lax.fori_loop(0, i, functools.partial(tile, masked=False), (zero, zero))
        dqa, dqb = tile(i, carry, True)
        dq_ref[...] = jnp.concatenate([dqa, dqb], axis=1) * ATT_SCALE

    return _run(
        body, name="attn_bwd", grid=(N_TOK_HEADS // 2, nq),
        out_shape=(jax.ShapeDtypeStruct((s, QK_W), F32), jax.ShapeDtypeStruct((QK_W, s), F32),
                   jax.ShapeDtypeStruct((TOK_W, s), F32)),
        in_specs=[pl.BlockSpec((t, 2 * HEAD_PAD), lambda j, i: (i, j)),
                  pl.BlockSpec((2 * HEAD_PAD, t), lambda j, i: (j, i)),
                  pl.BlockSpec((t, HEAD_PAD), lambda j, i: (i, j)),
                  pl.BlockSpec((HEAD_PAD, t), lambda j, i: (j, i)),
                  pl.BlockSpec((1, t, HEAD_PAD), lambda j, i: (j, i, 0)),
                  pl.BlockSpec((s, 2 * HEAD_PAD), lambda j, i: (0, j)),
                  pl.BlockSpec((s, HEAD_PAD), lambda j, i: (0, j))],
        out_specs=(pl.BlockSpec((t, 2 * HEAD_PAD), lambda j, i: (i, j)),
                   pl.BlockSpec((2 * HEAD_PAD, s), lambda j, i: (j, 0)),
                   pl.BlockSpec((HEAD_PAD, s), lambda j, i: (j, 0))),
        args=(q, qt, dob, dobt, stats, k, v), sem=("parallel", "arbitrary"), exchange=exchange)


def _mem_kv(mem, wmem):
    def body(m_ref, w_ref, o_ref):
        o_ref[0] = _mm(m_ref[...], w_ref[0]).astype(BF16)

    return pl.pallas_call(
        body, name="mem_kv", grid=(2,), out_shape=jax.ShapeDtypeStruct((2, MEM_LEN, 512), BF16),
        in_specs=[_const((MEM_LEN, 1024)), pl.BlockSpec((1, 1024, 512), lambda l: (l, 0, 0))],
        out_specs=pl.BlockSpec((1, MEM_LEN, 512), lambda l: (l, 0, 0)),
        compiler_params=_params(("parallel",)),
    )(mem, wmem)


def _mem_kv_bwd(mem, dmemkv):
    def body(m_ref, d_ref, o_ref):
        o_ref[...] = _mm_tn(m_ref[...], d_ref[...])

    return pl.pallas_call(
        body, name="mem_kv_bwd", grid=(1,), out_shape=jax.ShapeDtypeStruct((1024, 512), F32),
        in_specs=[_const((MEM_LEN, 1024)), _const((MEM_LEN, 512))], out_specs=_const((1024, 512)),
        compiler_params=_params(("arbitrary",)),
    )(mem, dmemkv)


def _head_mask(lane, sub):
    return (lane < 64) if sub == 0 else (lane >= 64)


def _mem_attn(qm, kv):
    tb = qm.shape[0]
    lane = _iota((tb, HEAD_PAD), 1)
    outs, ps = [], []
    for pp in range(2):
        qp = qm[:, HEAD_PAD * pp:HEAD_PAD * (pp + 1)]
        kp = kv[:, HEAD_PAD * pp:HEAD_PAD * (pp + 1)]
        vp = kv[:, MEM_W + HEAD_PAD * pp:MEM_W + HEAD_PAD * (pp + 1)]
        pair = None
        for sub in range(2):
            qh = jnp.where(_head_mask(lane, sub), qp, jnp.zeros_like(qp))
            sc = _mm_nt(qh, kp) * 0.125
            e = jnp.exp(sc - jnp.max(sc, axis=-1, keepdims=True))
            p = e / jnp.sum(e, axis=-1, keepdims=True)
            o = _mm(p, vp)
            ps.append(p)
            pair = o if sub == 0 else jnp.where(lane < 64, pair, o)
        outs.append(pair)
    return jnp.concatenate(outs, axis=1), ps


def _mem_attn_bwd(dmo, qm, kv, ps):
    tb = qm.shape[0]
    lane = _iota((tb, HEAD_PAD), 1)
    dqs, dks, dvs = [], [], []
    for pp in range(2):
        qp = qm[:, HEAD_PAD * pp:HEAD_PAD * (pp + 1)]
        kp = kv[:, HEAD_PAD * pp:HEAD_PAD * (pp + 1)]
        vp = kv[:, MEM_W + HEAD_PAD * pp:MEM_W + HEAD_PAD * (pp + 1)]
        dop = dmo[:, HEAD_PAD * pp:HEAD_PAD * (pp + 1)]
        dq_pair, dk_pair, dv_pair = None, None, None
        for sub in range(2):
            msk = _head_mask(lane, sub)
            p = ps[2 * pp + sub]
            qh = jnp.where(msk, qp, jnp.zeros_like(qp))
            doh = jnp.where(msk, dop, 0.0).astype(BF16)
            dv = _mm_tn(p, doh)
            dp = _mm_nt(doh, vp)
            ds = (p * (dp - jnp.sum(dp * p, axis=-1, keepdims=True)) * 0.125).astype(BF16)
            dq = _mm(ds, kp)
            dk = _mm_tn(ds, qh)
            if sub == 0:
                dq_pair, dk_pair, dv_pair = dq, dk, dv
            else:
                dq_pair = jnp.where(lane < 64, dq_pair, dq)
                dk_pair, dv_pair = dk_pair + dk, dv_pair + dv
        dqs.append(dq_pair)
        dks.append(dk_pair)
        dvs.append(dv_pair)
    return jnp.concatenate(dqs, axis=1), jnp.concatenate(dks + dvs, axis=1)


def _mix_core(tok, gate, qm, kv, wout, h_in, g, b):
    mem_out, ps = _mem_attn(qm, kv)
    cat = jnp.concatenate([tok, mem_out], axis=1)
    sg = jax.nn.sigmoid(gate)
    sl = gate * sg
    y = cat * sl
    r = ALPHA * h_in + _mm(y, wout)
    mu = jnp.mean(r, axis=-1, keepdims=True)
    xc = r - mu
    rstd = lax.rsqrt(jnp.mean(xc * xc, axis=-1, keepdims=True) + NORM_EPS)
    xh = xc * rstd
    return xh * g + b, (ps, cat, sg, sl, y, xh, rstd)


def _mix_fwd(tok, gate, qm, kv, wout, h_in, g, b):
    s = tok.shape[0]
    tb = min(TB_MIX, s)

    def body(tok_ref, gate_ref, qm_ref, kv_ref, w_ref, h_ref, g_ref, b_ref, o_ref):
        o_ref[...], _ = _mix_core(tok_ref[...], gate_ref[...], qm_ref[...], kv_ref[...], w_ref[...], h_ref[...],
                                  g_ref[...], b_ref[...])

    return pl.pallas_call(
        body, name="mix_fwd", grid=(s // tb,), out_shape=jax.ShapeDtypeStruct((s, 1024), F32),
        in_specs=[_rows(tb, TOK_W), _rows(tb, 1024), _rows(tb, MEM_W), _const((MEM_LEN, 512)), _const((1024, 1024)),
                  _rows(tb, 1024), _const((1, 1024)), _const((1, 1024))],
        out_specs=_rows(tb, 1024), compiler_params=_params(("parallel",)),
    )(tok, gate, qm, kv, wout, h_in, g, b)


def _mix_bwd(tok, gate, qm, kv, wout, h_in, g, b, up, from_loss, lse=None, exchange=None):
    s = tok.shape[0]
    tb = min(TB_MIX, s)
    n_in = 9 if lse is None else 10
    n_tok_out = 1 if lse is None else 3

    def body(*refs):
        tok_ref, gate_ref, qm_ref, kv_ref, w_ref, h_ref, g_ref, b_ref, up_ref = refs[:9]
        dres_ref, tok_out = refs[n_in], refs[n_in + 1:n_in + 1 + n_tok_out]
        dgate_ref, dqm_ref, dw_ref, dkv_ref, dg_ref, db_ref, loss_ref = refs[n_in + 1 + n_tok_out:]

        @pl.when(pl.program_id(0) == 0)
        def _():
            for r in (dw_ref, dkv_ref, dg_ref, db_ref, loss_ref):
                r[...] = jnp.zeros_like(r)

        gate, qm, kv, wout, g = gate_ref[...], qm_ref[...], kv_ref[...], w_ref[...], g_ref[...]
        h_out, (ps, cat, sg, sl, y, xh, rstd) = _mix_core(tok_ref[...], gate, qm, kv, wout, h_ref[...], g, b_ref[...])
        if from_loss:
            diff = h_out - up_ref[...]
            loss_ref[...] += 0.5 * jnp.sum(jnp.mean(diff * diff, axis=-1, keepdims=True), axis=0, keepdims=True)
            dh = diff * (1.0 / D_MODEL)
        else:
            dh = up_ref[...]
        dg_ref[...] += jnp.sum(dh * xh, axis=0, keepdims=True)
        db_ref[...] += jnp.sum(dh, axis=0, keepdims=True)
        dxh = dh * g
        dr = rstd * (dxh - jnp.mean(dxh, axis=-1, keepdims=True) - xh * jnp.mean(dxh * xh, axis=-1, keepdims=True))
        dres_ref[...] = ALPHA * dr
        drb = dr.astype(BF16)
        dy = _mm_nt(drb, wout)
        dw_ref[...] += _mm_tn(y, drb)
        dcat = dy * sl
        dgate_ref[...] = dy * cat * (sg * (1.0 + gate * (1.0 - sg)))
        if lse is None:
            tok_out[0][...] = dcat[:, :TOK_W]
        else:
            _attn_stats(tok_ref[...], dcat[:, :TOK_W], refs[9], *tok_out)
        dqm, dkv = _mem_attn_bwd(dcat[:, TOK_W:], qm, kv, ps)
        dqm_ref[...] = dqm
        dkv_ref[...] += dkv

    npair = N_TOK_HEADS // 2
    tok_shapes = [jax.ShapeDtypeStruct((s, TOK_W), F32)] if lse is None else [
        jax.ShapeDtypeStruct((s, TOK_W), BF16), jax.ShapeDtypeStruct((TOK_W, s), BF16),
        jax.ShapeDtypeStruct((npair, s, HEAD_PAD), F32)]
    tok_specs = [_rows(tb, TOK_W)] if lse is None else [
        _rows(tb, TOK_W), pl.BlockSpec((TOK_W, tb), lambda i: (0, i)),
        pl.BlockSpec((npair, tb, HEAD_PAD), lambda i: (0, i, 0))]
    outs = (jax.ShapeDtypeStruct((s, 1024), F32), *tok_shapes,
            jax.ShapeDtypeStruct((s, 1024), F32), jax.ShapeDtypeStruct((s, MEM_W), F32),
            jax.ShapeDtypeStruct((1024, 1024), F32), jax.ShapeDtypeStruct((MEM_LEN, 512), F32),
            jax.ShapeDtypeStruct((1, 1024), F32), jax.ShapeDtypeStruct((1, 1024), F32),
            jax.ShapeDtypeStruct((1, 1), F32))
    args = (tok, gate, qm, kv, wout, h_in, g, b, up) + (() if lse is None else (lse,))
    return _run(
        body, name="mix_bwd_loss" if from_loss else "mix_bwd", grid=(s // tb,), out_shape=outs,
        in_specs=[_rows(tb, TOK_W), _rows(tb, 1024), _rows(tb, MEM_W), _const((MEM_LEN, 512)), _const((1024, 1024)),
                  _rows(tb, 1024), _const((1, 1024)), _const((1, 1024)), _rows(tb, 1024)]
        + ([] if lse is None else [_rows(tb, TOK_W)]),
        out_specs=(_rows(tb, 1024), *tok_specs, _rows(tb, 1024), _rows(tb, MEM_W), _const((1024, 1024)),
                   _const((MEM_LEN, 512)), _const((1, 1024)), _const((1, 1024)), _const((1, 1))),
        args=args, sem=("arbitrary",), exchange=exchange)


def _shift_down(u, tail, k):
    if k == 0:
        return u
    r = pltpu.roll(u, k, 0)
    row8 = _iota((8, u.shape[1]), 0)
    head = jnp.where(row8 < k, pltpu.roll(tail, k, 0), r[:8])
    return jnp.concatenate([head, r[8:]], axis=0)


def _shift_up(d, head, k):
    if k == 0:
        return d
    n = d.shape[0]
    r = pltpu.roll(d, n - k, 0)
    row8 = _iota((8, d.shape[1]), 0)
    last = jnp.where(row8 >= 8 - k, pltpu.roll(head, 8 - k, 0), r[n - 8:])
    return jnp.concatenate([r[:n - 8], last], axis=0)


def _scan_down(a, b):
    n = a.shape[0]
    row = _iota(a.shape, 0)
    s = 1
    while s < n:
        ok = row >= s
        a_s = jnp.where(ok, pltpu.roll(a, s, 0), 1.0)
        b_s = jnp.where(ok, pltpu.roll(b, s, 0), 0.0)
        b = a * b_s + b
        a = a * a_s
        s *= 2
    return a, b


def _scan_up(a, b):
    n = a.shape[0]
    row = _iota(a.shape, 0)
    s = 1
    while s < n:
        ok = row < n - s
        a_s = jnp.where(ok, pltpu.roll(a, n - s, 0), 1.0)
        b_s = jnp.where(ok, pltpu.roll(b, n - s, 0), 0.0)
        b = a * b_s + b
        a = a * a_s
        s *= 2
    return a, b


def _neg_expm1(x):
    poly = -x * (1.0 + x * (0.5 + x * (1.0 / 6.0 + x * (1.0 / 24.0 + x * (1.0 / 120.0)))))
    return jnp.where(x > -0.1, poly, 1.0 - jnp.exp(x))


def _softplus(x):
    return jnp.maximum(x, 0.0) + jnp.log(1.0 + jnp.exp(-jnp.abs(x)))


def _lru_gates(u, tail, cw, cb, wr, br, wi, bi, lam):
    us = [_shift_down(u, tail, k) for k in range(4)]
    xc = cb + us[3] * cw[0:1] + us[2] * cw[1:2] + us[1] * cw[2:3] + us[0] * cw[3:4]
    xb = xc.astype(BF16)
    pre_r = jnp.concatenate([_mm(xb[:, 256 * g:256 * (g + 1)], wr[g]) for g in range(3)], axis=1) + br
    pre_i = jnp.concatenate([_mm(xb[:, 256 * g:256 * (g + 1)], wi[g]) for g in range(3)], axis=1) + bi
    rg, ig = jax.nn.sigmoid(pre_r), jax.nn.sigmoid(pre_i)
    clam = -LRU_C * _softplus(-lam)
    la = clam * rg
    a = jnp.exp(la)
    mm = jnp.sqrt(_neg_expm1(2.0 * la))
    return us, xc, xb, rg, ig, clam, la, a, mm


def _lru_fwd(h, win, cw, cb, wr, br, wi, bi, lam):
    s = h.shape[0]
    tb = min(TB_LRU, s)

    def body(h_ref, win_ref, cw_ref, cb_ref, wr_ref, br_ref, wi_ref, bi_ref, lam_ref,
             u_ref, gate_ref, qm_ref, hs_ref, tail_sc, carry_sc):
        @pl.when(pl.program_id(0) == 0)
        def _():
            tail_sc[...] = jnp.zeros_like(tail_sc)
            carry_sc[...] = jnp.zeros_like(carry_sc)

        hb = h_ref[...].astype(BF16)
        z = jnp.concatenate([_mm(hb, win_ref[sh]) for sh in range(4)], axis=1)
        u = z[:, :TOK_W]
        u_ref[...] = u
        gate_ref[...] = z[:, TOK_W:TOK_W + 1024]
        qm_ref[...] = z[:, TOK_W + 1024:].astype(BF16)
        _, xc, _, _, ig, _, _, a, mm = _lru_gates(u, tail_sc[...], cw_ref[...], cb_ref[...], wr_ref[...], br_ref[...],
                                                 wi_ref[...], bi_ref[...], lam_ref[...])
        big_a, big_b = _scan_down(a, mm * (ig * xc))
        hs = big_a * carry_sc[0:1, :] + big_b
        hs_ref[...] = hs
        tail_sc[...] = u[tb - 8:, :]
        carry_sc[...] = jnp.broadcast_to(hs[tb - 1:tb, :], carry_sc.shape)

    outs = (jax.ShapeDtypeStruct((s, TOK_W), F32), jax.ShapeDtypeStruct((s, 1024), F32),
            jax.ShapeDtypeStruct((s, MEM_W), BF16), jax.ShapeDtypeStruct((s, TOK_W), F32))
    return pl.pallas_call(
        body, name="lru_fwd", grid=(s // tb,), out_shape=outs,
        in_specs=[_rows(tb, 1024), _const((4, 1024, 512)), _const((4, TOK_W)), _const((1, TOK_W)),
                  _const((3, 256, 256)), _const((1, TOK_W)), _const((3, 256, 256)), _const((1, TOK_W)),
                  _const((1, TOK_W))],
        out_specs=(_rows(tb, TOK_W), _rows(tb, 1024), _rows(tb, MEM_W), _rows(tb, TOK_W)),
        scratch_shapes=[pltpu.VMEM((8, TOK_W), F32), pltpu.VMEM((8, TOK_W), F32)],
        compiler_params=_params(),
    )(h, win, cw, cb, wr, br, wi, bi, lam)


def _lru_bwd(dhs, dgate, dqm, dres, h, u, hs, win, cw, cb, wr, br, wi, bi, lam):
    s = h.shape[0]
    tb = min(TB_LRU, s)
    nb = s // tb

    def rev(w):
        return pl.BlockSpec((tb, w), lambda i: (nb - 1 - i, 0))

    def prev_tail(w):
        return pl.BlockSpec((8, w), lambda i: (jnp.maximum((nb - 1 - i) * (tb // 8) - 1, 0), 0))

    def body(dhs_ref, dgate_ref, dqm_ref, dres_ref, h_ref, u_ref, hs_ref, ut_ref, hst_ref, win_ref, cw_ref, cb_ref,
             wr_ref, br_ref, wi_ref, bi_ref, lam_ref,
             dh_ref, dwin_ref, dcw_ref, dcb_ref, dwr_ref, dbr_ref, dwi_ref, dbi_ref, dlam_ref, ecar_sc, dxc_sc):
        i = pl.program_id(0)

        @pl.when(i == 0)
        def _():
            for r in (dwin_ref, dcw_ref, dcb_ref, dwr_ref, dbr_ref, dwi_ref, dbi_ref, dlam_ref, ecar_sc, dxc_sc):
                r[...] = jnp.zeros_like(r)

        first = (i == nb - 1)
        u = u_ref[...]
        utail = jnp.where(first, 0.0, ut_ref[...])
        hstail = jnp.where(first, 0.0, hst_ref[...])
        cw, wr, wi, lam = cw_ref[...], wr_ref[...], wi_ref[...], lam_ref[...]
        us, xc, xb, rg, ig, clam, la, a, mm = _lru_gates(u, utail, cw, cb_ref[...], wr, br_ref[...], wi, bi_ref[...], lam)
        row = _iota(a.shape, 0)
        a_next = jnp.where(row < tb - 1, pltpu.roll(a, tb - 1, 0), 1.0)
        big_a, big_b = _scan_up(a_next, dhs_ref[...])
        e = big_a * ecar_sc[0:1, :] + big_b
        ecar_sc[...] = jnp.broadcast_to(a[0:1, :] * e[0:1, :], ecar_sc.shape)
        hs_prev = _shift_down(hs_ref[...], hstail, 1)
        da = e * hs_prev
        ix = ig * xc
        dmm = e * ix
        dix = e * mm
        dla = da * a - dmm * (a * a) / mm
        dlam_ref[...] += jnp.sum(dla * rg, axis=0, keepdims=True)
        dpr = (dla * clam) * rg * (1.0 - rg)
        dpi = (dix * xc) * ig * (1.0 - ig)
        dbr_ref[...] += jnp.sum(dpr, axis=0, keepdims=True)
        dbi_ref[...] += jnp.sum(dpi, axis=0, keepdims=True)
        dprb, dpib = dpr.astype(BF16), dpi.astype(BF16)
        dxc_g = []
        for g in range(3):
            sl = slice(256 * g, 256 * (g + 1))
            dwr_ref[g] += _mm_tn(xb[:, sl], dprb[:, sl])
            dwi_ref[g] += _mm_tn(xb[:, sl], dpib[:, sl])
            dxc_g.append(_mm_nt(dprb[:, sl], wr[g]) + _mm_nt(dpib[:, sl], wi[g]))
        dxc = dix * ig + jnp.concatenate(dxc_g, axis=1)
        dcb_ref[...] += jnp.sum(dxc, axis=0, keepdims=True)
        dcw_ref[...] += jnp.concatenate([jnp.sum(dxc * us[3 - tap], axis=0, keepdims=True) for tap in range(4)], axis=0)
        head = dxc_sc[...]
        du = dxc * cw[3:4]
        for k in range(1, 4):
            du = du + _shift_up(dxc, head, k) * cw[3 - k:4 - k]
        dxc_sc[...] = dxc[:8, :]
        dz = jnp.concatenate([du, dgate_ref[...], dqm_ref[...]], axis=1).astype(BF16)
        hb = h_ref[...].astype(BF16)
        dh = dres_ref[...]
        for sh in range(4):
            dzs = dz[:, 512 * sh:512 * (sh + 1)]
            dh = dh + _mm_nt(dzs, win_ref[sh])
            dwin_ref[sh] += _mm_tn(hb, dzs)
        dh_ref[...] = dh

        @pl.when(i == nb - 1)
        def _():
            dlam_ref[...] = dlam_ref[...] * (LRU_C * jax.nn.sigmoid(-lam))

    outs = (jax.ShapeDtypeStruct((s, 1024), F32), jax.ShapeDtypeStruct((4, 1024, 512), F32),
            jax.ShapeDtypeStruct((4, TOK_W), F32), jax.ShapeDtypeStruct((1, TOK_W), F32),
            jax.ShapeDtypeStruct((3, 256, 256), F32), jax.ShapeDtypeStruct((1, TOK_W), F32),
            jax.ShapeDtypeStruct((3, 256, 256), F32), jax.ShapeDtypeStruct((1, TOK_W), F32),
            jax.ShapeDtypeStruct((1, TOK_W), F32))
    return pl.pallas_call(
        body, name="lru_bwd", grid=(nb,), out_shape=outs,
        in_specs=[rev(TOK_W), rev(1024), rev(MEM_W), rev(1024), rev(1024), rev(TOK_W), rev(TOK_W),
                  prev_tail(TOK_W), prev_tail(TOK_W),
                  _const((4, 1024, 512)), _const((4, TOK_W)), _const((1, TOK_W)), _const((3, 256, 256)),
                  _const((1, TOK_W)), _const((3, 256, 256)), _const((1, TOK_W)), _const((1, TOK_W))],
        out_specs=(rev(1024), _const((4, 1024, 512)), _const((4, TOK_W)), _const((1, TOK_W)), _const((3, 256, 256)),
                   _const((1, TOK_W)), _const((3, 256, 256)), _const((1, TOK_W)), _const((1, TOK_W))),
        scratch_shapes=[pltpu.VMEM((8, TOK_W), F32), pltpu.VMEM((8, TOK_W), F32)],
        compiler_params=_params(),
    )(dhs, dgate, dqm, dres, h, u, hs, u, hs, win, cw, cb, wr, br, wi, bi, lam)


def _adamw_update(w_ref, g_ref, m_ref, v_ref, d_ref, nm_ref, nv_ref):
    g = g_ref[...]
    nm = ADAM_B1 * m_ref[...] + (1.0 - ADAM_B1) * g
    nv = ADAM_B2 * v_ref[...] + (1.0 - ADAM_B2) * (g * g)
    m_hat = nm / (1.0 - ADAM_B1 ** ADAM_STEP)
    v_hat = nv / (1.0 - ADAM_B2 ** ADAM_STEP)
    d_ref[...] = -ADAM_LR * (m_hat / (jnp.sqrt(v_hat) + ADAM_EPS) + ADAM_WD * w_ref[...])
    nm_ref[...] = nm
    nv_ref[...] = nv


def _adamw(name, w, g, m, v):
    rows, cols = w.shape
    tb = 256 if rows % 256 == 0 else rows

    def body(*refs):
        _adamw_update(*refs)

    shp = jax.ShapeDtypeStruct((rows, cols), F32)
    return pl.pallas_call(
        body, name="adamw_" + name, grid=(rows // tb,), out_shape=(shp, shp, shp),
        in_specs=[_rows(tb, cols)] * 4, out_specs=(_rows(tb, cols),) * 3,
        compiler_params=_params(("parallel",)),
    )(w, g, m, v)


def _adamw_small(items):
    n = len(items)

    def body(*refs):
        for k in range(n):
            _adamw_update(*refs[4 * k:4 * k + 4], *refs[4 * n + 3 * k:4 * n + 3 * k + 3])

    args = [a for it in items for a in it]
    shapes = [jax.ShapeDtypeStruct(it[0].shape, F32) for it in items for _ in range(3)]
    outs = pl.pallas_call(
        body, name="adamw_small", grid=(1,), out_shape=tuple(shapes),
        in_specs=[_const(a.shape) for a in args], out_specs=tuple(_const(sh.shape) for sh in shapes),
        compiler_params=_params(),
    )(*args)
    return [outs[3 * k:3 * k + 3] for k in range(n)]


def _row_block(rows, cap=2048):
    return max(t for t in range(8, cap + 1, 8) if rows % t == 0)


def _add2(a, b):
    rows = a.shape[0]
    tb = _row_block(rows)

    def body(a_ref, b_ref, o_ref):
        o_ref[...] = a_ref[...] + b_ref[...]

    return pl.pallas_call(
        body, name="add_sibling", grid=(rows // tb,), out_shape=jax.ShapeDtypeStruct(a.shape, F32),
        in_specs=[_rows(tb, 128)] * 2, out_specs=_rows(tb, 128), compiler_params=_params(("parallel",)),
    )(a, b)


def _sum_slots(landed, own, rows):
    tb = _row_block(rows, 1024)

    def body(l_ref, o_ref, out_ref):
        t = 2 * lax.axis_index("x") + lax.axis_index("y")
        r = [jnp.where(t == s, o_ref[s], l_ref[s].astype(F32)) for s in range(4)]
        out_ref[...] = ((r[0] + r[1]) + r[2]) + r[3]

    return pl.pallas_call(
        body, name="sum_chips", grid=(rows // tb,), out_shape=jax.ShapeDtypeStruct((rows, 128), F32),
        in_specs=[pl.BlockSpec((4, tb, 128), lambda i: (0, i, 0))] * 2, out_specs=_rows(tb, 128),
        compiler_params=_params(("parallel",)),
    )(landed, own)


_ANY = pl.BlockSpec(memory_space=pl.ANY)


def _place():
    x, y, c = lax.axis_index("x"), lax.axis_index("y"), lax.axis_index("c")
    return x, y, c, [(1 - x, y), (x, 1 - y), (1 - x, 1 - y)]


def _remote(src, dst, ssem, rsem, to):
    return pltpu.make_async_remote_copy(src_ref=src, dst_ref=dst, send_sem=ssem, recv_sem=rsem, device_id=to,
                                        device_id_type=MESH)


class _Exchange:
    def __init__(self, ins, out_shape, sems, start, finish):
        self.ins, self.out_shape, self.sems, self.start, self.finish = ins, out_shape, sems, start, finish


def _run(body, *, name, grid, in_specs, out_specs, out_shape, args, scratch=(), sem, exchange=None):
    if exchange is None:
        return pl.pallas_call(body, name=name, grid=grid, out_shape=tuple(out_shape), in_specs=list(in_specs),
                              out_specs=tuple(out_specs), scratch_shapes=list(scratch),
                              compiler_params=_params(sem))(*args)
    n_in, n_out, n_sc = len(args), len(out_shape), len(scratch)
    k_in, k_out = len(exchange.ins), len(exchange.out_shape)

    def fused(*refs):
        ins, refs = refs[:n_in], refs[n_in:]
        xin, refs = refs[:k_in], refs[k_in:]
        outs, refs = refs[:n_out], refs[n_out:]
        xout, refs = refs[:k_out], refs[k_out:]
        sc, xsem = refs[:n_sc], refs[n_sc:]
        first = pl.program_id(0) == 0
        last = pl.program_id(0) == grid[0] - 1
        for a in range(1, len(grid)):
            first = first & (pl.program_id(a) == 0)
            last = last & (pl.program_id(a) == grid[a] - 1)

        @pl.when(first)
        def _():
            exchange.start(xin, xout, xsem)

        body(*ins, *outs, *sc)

        @pl.when(last)
        def _():
            exchange.finish(xin, xout, xsem)

    return pl.pallas_call(
        fused, name=name, grid=grid, out_shape=(*out_shape, *exchange.out_shape),
        in_specs=[*in_specs, *[_ANY] * k_in], out_specs=(*out_specs, *[_ANY] * k_out),
        scratch_shapes=[*scratch, *exchange.sems],
        compiler_params=_params(("arbitrary",) * len(grid)),
    )(*args, *exchange.ins)


def _run_exchange(exchange, name):
    def body(*refs):
        k_in, k_out = len(exchange.ins), len(exchange.out_shape)
        xin, xout, xsem = refs[:k_in], refs[k_in:k_in + k_out], refs[k_in + k_out:]
        exchange.start(xin, xout, xsem)
        exchange.finish(xin, xout, xsem)

    return pl.pallas_call(
        body, name=name, out_shape=tuple(exchange.out_shape), in_specs=[_ANY] * len(exchange.ins),
        out_specs=tuple([_ANY] * len(exchange.out_shape)), scratch_shapes=list(exchange.sems),
    )(*exchange.ins)


def _gather_shards(wsh):
    _, hh, _ = wsh.shape

    def first_hop(w_ref, out_ref, ssems, rsems):
        x, y, c, chips = _place()
        t = 2 * x + y
        return [_remote(w_ref.at[c], out_ref.at[t, c], ssems.at[j], rsems.at[j], (cx, cy, c))
                for j, (cx, cy) in enumerate(chips)]

    def start(xin, xout, xsem):
        for cp in first_hop(xin[0], xout[0], *xsem):
            cp.start()

    def finish(xin, xout, xsem):
        out_ref, (ssems, rsems) = xout[0], xsem
        first = first_hop(xin[0], out_ref, *xsem)
        x, y, c, chips = _place()
        passed = []
        for j, (cx, cy) in enumerate(chips):
            got = out_ref.at[2 * cx + cy, c]
            _remote(got, got, ssems.at[j], rsems.at[j], (cx, cy, c)).wait_recv()
            cp = _remote(got, got, ssems.at[3 + j], rsems.at[3 + j], (x, y, 1 - c))
            cp.start()
            passed.append(cp)
        for j, (cx, cy) in enumerate(chips):
            got = out_ref.at[2 * cx + cy, 1 - c]
            _remote(got, got, ssems.at[3 + j], rsems.at[3 + j], (x, y, 1 - c)).wait_recv()
        for cp in first + passed:
            cp.wait_send()

    return _Exchange([wsh], [jax.ShapeDtypeStruct((4, 2, hh, 128), wsh.dtype)],
                     [pltpu.SemaphoreType.DMA((6,)), pltpu.SemaphoreType.DMA((6,))], start, finish)


def _gathered(landed, own):
    t = 2 * lax.axis_index("x") + lax.axis_index("y")
    return lax.dynamic_update_slice(landed, own[None], (t, 0, 0, 0))


def _swap_sibling(v):
    def copy(xin, xout, xsem):
        x, y, c, _ = _place()
        return _remote(xin[0], xout[0], xsem[0], xsem[1], (x, y, 1 - c))

    return _Exchange([v], [jax.ShapeDtypeStruct(v.shape, v.dtype)],
                     [pltpu.SemaphoreType.DMA, pltpu.SemaphoreType.DMA],
                     lambda *a: copy(*a).start(), lambda *a: copy(*a).wait())


def _scatter_chips(parts):
    n = len(parts)

    def copies(xin, xout, ssems, rsems):
        x, y, c, chips = _place()
        t = 2 * x + y
        return [_remote(xin[k].at[2 * cx + cy], xout[k].at[t], ssems.at[n * j + k], rsems.at[n * j + k], (cx, cy, c))
                for j, (cx, cy) in enumerate(chips) for k in range(n)]

    def start(xin, xout, xsem):
        for cp in copies(xin, xout, *xsem):
            cp.start()

    def finish(xin, xout, xsem):
        ssems, rsems = xsem
        x, y, c, chips = _place()
        for j, (cx, cy) in enumerate(chips):
            for k in range(n):
                got = xout[k].at[2 * cx + cy]
                _remote(got, got, ssems.at[n * j + k], rsems.at[n * j + k], (cx, cy, c)).wait_recv()
        for cp in copies(xin, xout, *xsem):
            cp.wait_send()

    return _Exchange(parts, [jax.ShapeDtypeStruct(a.shape, a.dtype) for a in parts],
                     [pltpu.SemaphoreType.DMA((3 * n,)), pltpu.SemaphoreType.DMA((3 * n,))], start, finish)


def _share_reduced(piece, eighth):
    def copies(t_ref, mine_r, sib_ref, rall_ref, ssems, rsems, lsem):
        x, y, c, _ = _place()
        me = 4 * x + 2 * y + c
        loc = pltpu.make_async_copy(mine_r, rall_ref.at[me], lsem)
        sends = [_remote(t_ref, sib_ref, ssems.at[0], rsems.at[0], (x, y, 1 - c))]
        peers = []
        for mask in range(1, 8):
            px = 1 - x if mask & 4 else x
            py = 1 - y if mask & 2 else y
            pc = 1 - c if mask & 1 else c
            peers.append((mask, px, py, pc))
            sends.append(_remote(mine_r, rall_ref.at[me], ssems.at[mask], rsems.at[mask], (px, py, pc)))
        return loc, sends, peers

    def start(xin, xout, xsem):
        loc, sends, _ = copies(*xin, *xout, *xsem)
        for cp in [loc] + sends:
            cp.start()

    def finish(xin, xout, xsem):
        (sib_ref, rall_ref), (ssems, rsems, _) = xout, xsem
        loc, sends, peers = copies(*xin, *xout, *xsem)
        x, y, c, _ = _place()
        _remote(sib_ref, sib_ref, ssems.at[0], rsems.at[0], (x, y, 1 - c)).wait_recv()
        for mask, px, py, pc in peers:
            got = rall_ref.at[4 * px + 2 * py + pc]
            _remote(got, got, ssems.at[mask], rsems.at[mask], (px, py, pc)).wait_recv()
        for cp in sends:
            cp.wait_send()
        loc.wait()

    return _Exchange([piece, eighth],
                     [jax.ShapeDtypeStruct(piece.shape, F32), jax.ShapeDtypeStruct((8, *eighth.shape), F32)],
                     [pltpu.SemaphoreType.DMA((8,)), pltpu.SemaphoreType.DMA((8,)), pltpu.SemaphoreType.DMA],
                     start, finish)


def _ceil_to(n, m):
    return -(-n // m) * m


def _pack_bf16(parts):
    blocks = [p.reshape(-1, 128) for p in parts]
    rows = jnp.concatenate([jnp.pad(b, ((0, -b.shape[0] % 16), (0, 0))) for b in blocks])
    hw = _ceil_to(rows.shape[0], 32) // 2
    return jnp.pad(rows, ((0, 2 * hw - rows.shape[0]), (0, 0))).reshape(2, hw, 128)


def _segments(wall, parts):
    wall = wall.reshape(4, -1, 128)
    out, row = [], 0
    for p in parts:
        n = p.size // 128
        out.append(wall[:, row:row + n].reshape(4, *p.shape))
        row += _ceil_to(n, 16)
    return out


class _GradReduce:
    def __init__(self, sharded, replicated, c_idx, wire_bf16=False):
        self.c_idx, self.wire_bf16 = c_idx, wire_bf16
        self.rowwise = [(n, g.shape[1:]) for n, g in sharded if math.prod(g.shape[1:]) % 128 == 0]
        self.small = [(n, g.shape[1:]) for n, g in sharded if math.prod(g.shape[1:]) % 128 != 0]
        self.replicated = [(n, g.shape[0]) for n, g in replicated]
        by_name = dict(sharded)
        blocks = [by_name[n].reshape(4, -1, 128) for n, _ in self.rowwise]
        if self.small:
            rest = jnp.concatenate([by_name[n].reshape(4, -1) for n, _ in self.small], axis=1)
            blocks.append(jnp.pad(rest, ((0, 0), (0, -rest.shape[1] % 128))).reshape(4, -1, 128))
        blocks = [jnp.pad(b, ((0, 0), (0, -b.shape[1] % 8), (0, 0))) for b in blocks]
        rows = sum(b.shape[1] for b in blocks)
        self.hs = _ceil_to(rows, 256) // 2
        sh = jnp.concatenate(blocks + [jnp.zeros((4, 2 * self.hs - rows, 128), F32)], axis=1)
        rp = jnp.concatenate([g for _, g in replicated])
        self.rr = _ceil_to(_ceil_to(rp.shape[0], 128) // 128, 64) // 8
        rp = jnp.pad(rp, (0, 8 * self.rr * 128 - rp.shape[0])).reshape(4, 2, self.rr, 128)
        self.hh = self.hs + self.rr

        def half(c):
            return jnp.concatenate([lax.dynamic_slice_in_dim(sh, c * self.hs, self.hs, axis=1),
                                    lax.dynamic_index_in_dim(rp, c, axis=1, keepdims=False)],
                                   axis=1).reshape(4 * self.hh, 128)

        self.mine, self.other = half(c_idx), half(1 - c_idx)

    def swap(self):
        return _swap_sibling(self.other)

    def swapped(self, got):
        self.chip_sum = _add2(self.mine, got).reshape(4, self.hh, 128)

    def scatter(self):
        self.own_r = self.chip_sum[:, self.hs:]
        if self.wire_bf16:
            return _scatter_chips([self.chip_sum[:, :self.hs].astype(BF16), self.own_r])
        return _scatter_chips([self.chip_sum])

    def scattered(self, landed, landed_r=None):
        if landed_r is None:
            landed_r = landed[:, self.hs:]
        self.piece = _sum_slots(landed, self.chip_sum, self.hs)
        self.eighth = _sum_slots(landed_r, self.own_r, self.rr)

    def share(self):
        return _share_reduced(self.piece, self.eighth)

    def shared(self, sibling, rall):
        mine, sib = self.piece, sibling
        self.shard = jnp.where(self.c_idx == 0, jnp.concatenate([mine, sib]), jnp.concatenate([sib, mine]))
        self.rall = rall

    def reduced(self):
        out, row = {}, 0
        for name, shape in self.rowwise:
            rows = math.prod(shape) // 128
            out[name] = self.shard[row:row + rows].reshape(shape)
            row += _ceil_to(rows, 8)
        for group, flat in ((self.small, self.shard[row:].reshape(-1)), (self.replicated, self.rall.reshape(-1))):
            off = 0
            for name, shape in group:
                n = math.prod(shape) if isinstance(shape, tuple) else shape
                out[name] = flat[off:off + n]
                off += n
        return out


def _col_shards(w2d):
    rows, cols = w2d.shape
    return w2d.reshape(rows, 4, cols // 4).transpose(1, 0, 2)


_WIN0_PARTS = ((0, 384, 1280), (384, 640, 1664), (640, 672, 1984), (672, 1696, 0), (1696, 1952, 1024))


def _win0_aligned(shards):
    def cols(a, b):
        return [shards[s][:, max(a, 488 * s) - 488 * s:min(b, 488 * (s + 1)) - 488 * s]
                for s in range(4) if max(a, 488 * s) < min(b, 488 * (s + 1))]

    zeros = jnp.zeros((1024, 64), shards.dtype)
    return jnp.concatenate(cols(672, 1696) + cols(1696, 1952) + cols(0, 384) + cols(384, 640)
                           + [zeros] + cols(640, 672) + [zeros[:, :32]], axis=1)


def _win0_shards(dwin0p):
    shards = []
    for s in range(4):
        lo, hi = 488 * s, 488 * (s + 1)
        cols = [dwin0p[:, p + max(lo, a) - a:p + min(hi, b) - a] for a, b, p in _WIN0_PARTS if max(lo, a) < min(hi, b)]
        shards.append(jnp.concatenate(cols, axis=1))
    return jnp.stack(shards)


def _block_diag4(w):
    eye = jnp.eye(4, dtype=w.dtype)
    return jnp.einsum("gaij,ab->gaibj", w.reshape(3, 4, 64, 64), eye).reshape(3, 256, 256)


def _diag_blocks4(w):
    w5 = w.reshape(3, 4, 64, 4, 64)
    return jnp.stack([w5[:, a, :, a, :] for a in range(4)], axis=1).reshape(12, 64, 64)


def kernel(x, mem, positions, mla_w_in, mla_q_norm, mla_w_uq, mla_kv_norm, mla_w_ukv, lru_w_in, lru_conv_w, lru_conv_b, lru_w_rgate, lru_b_rgate, lru_w_igate, lru_b_igate, lru_lambda, w_mem_kv, w_out, ln_g, ln_b, loss_target, m_mla_w_in, m_mla_q_norm, m_mla_w_uq, m_mla_kv_norm, m_mla_w_ukv, m_lru_w_in, m_lru_conv_w, m_lru_conv_b, m_lru_w_rgate, m_lru_b_rgate, m_lru_w_igate, m_lru_b_igate, m_lru_lambda, m_w_mem_kv, m_w_out, m_ln_g, m_ln_b, v_mla_w_in, v_mla_q_norm, v_mla_w_uq, v_mla_kv_norm, v_mla_w_ukv, v_lru_w_in, v_lru_conv_w, v_lru_conv_b, v_lru_w_rgate, v_lru_b_rgate, v_lru_w_igate, v_lru_b_igate, v_lru_lambda, v_w_mem_kv, v_w_out, v_ln_g, v_ln_b):
    s = x.shape[1]
    c_idx = lax.axis_index("c")
    x2, mem2, tgt2 = x[0], mem[0], loss_target[0]

    first = [p.astype(BF16) for p in (mla_w_in[0], mla_w_uq[0], mla_w_ukv[0])]
    buf = _pack_bf16(first)
    mla_shards = _segments(_gathered(_run_exchange(_gather_shards(buf), "gather_weights")[0], buf), first)

    mid = [w_mem_kv.astype(BF16), w_out[0].astype(BF16)]
    buf_mid = _pack_bf16(mid)

    def mid_weights(landed):
        wmem, wout0 = _segments(_gathered(landed[0], buf_mid), mid)
        return wmem.transpose(1, 0, 2, 3).reshape(2, 1024, 512), wout0.reshape(1024, 1024)

    small = jnp.concatenate([lru_conv_w[0].reshape(-1), lru_conv_b[0], lru_b_rgate[0], lru_b_igate[0], lru_lambda[0]])
    late = [lru_w_in[0].astype(BF16), w_out[1].astype(BF16), lax.bitcast_convert_type(small, BF16)]
    buf_late = _pack_bf16(late)

    def late_weights(landed):
        win1, wout1, small_bits = _segments(_gathered(landed[0], buf_late), late)
        small_all = lax.bitcast_convert_type(small_bits, F32)
        cw = small_all[:, :768].reshape(4, 4, 192).transpose(1, 0, 2).reshape(4, TOK_W)
        cb, br, bi, lam = (small_all[:, 768 + 192 * k:960 + 192 * k].reshape(1, TOK_W) for k in range(4))
        return win1, wout1.reshape(1024, 1024), cw, cb, br, bi, lam

    def reduce_late(g):
        return _GradReduce(
            [("lru_w_in", g["lru_w_in"]), ("lru_conv_w", _col_shards(g["lru_conv_w"])),
             ("lru_conv_b", _col_shards(g["lru_conv_b"])), ("lru_b_rgate", _col_shards(g["lru_b_rgate"])),
             ("lru_b_igate", _col_shards(g["lru_b_igate"])), ("lru_lambda", _col_shards(g["lru_lambda"])),
             ("w_mem_kv1", g["w_mem_kv1"].reshape(4, 256, 512)), ("w_out1", g["w_out1"].reshape(4, 256, 1024))],
            [("lru_w_rgate", g["lru_w_rgate"].reshape(-1)), ("lru_w_igate", g["lru_w_igate"].reshape(-1)),
             ("ln_g1", g["ln_g1"].reshape(-1)), ("ln_b1", g["ln_b1"].reshape(-1))], c_idx)

    g0, late_red = _local_step(
        x2, mem2, positions.reshape(s, 1), tgt2, *mla_shards, mla_q_norm, mla_kv_norm, lru_w_rgate[0], lru_w_igate[0],
        ln_g, ln_b, mid_weights, late_weights, _gather_shards(buf_mid), _gather_shards(buf_late), reduce_late)

    early_red = _GradReduce(
        [("mla_w_in", g0["mla_w_in"]), ("mla_w_uq", _col_shards(g0["mla_w_uq"])),
         ("mla_w_ukv", _col_shards(g0["mla_w_ukv"])), ("w_mem_kv0", g0["w_mem_kv0"].reshape(4, 256, 512)),
         ("w_out0", g0["w_out0"].reshape(4, 256, 1024))],
        [("mla_q_norm", g0["mla_q_norm"].reshape(-1)), ("mla_kv_norm", g0["mla_kv_norm"].reshape(-1)),
         ("ln_g0", g0["ln_g0"].reshape(-1)), ("ln_b0", g0["ln_b0"].reshape(-1)), ("loss", g0["loss"].reshape(-1))],
        c_idx, wire_bf16=True)
    early_red.swapped(*_run_exchange(early_red.swap(), "swap_sibling"))
    early_red.scattered(*_run_exchange(early_red.scatter(), "scatter_chips"))
    early_red.shared(*_run_exchange(early_red.share(), "share_reduced"))
    red = {**late_red.reduced(), **early_red.reduced()}
    red["w_mem_kv"] = jnp.concatenate([red["w_mem_kv0"], red["w_mem_kv1"]])
    red["w_out"] = jnp.concatenate([red["w_out0"], red["w_out1"]])
    red["ln_g"] = jnp.concatenate([red["ln_g0"], red["ln_g1"]])
    red["ln_b"] = jnp.concatenate([red["ln_b0"], red["ln_b1"]])

    weights = dict(mla_w_in=mla_w_in, mla_q_norm=mla_q_norm, mla_w_uq=mla_w_uq, mla_kv_norm=mla_kv_norm,
                   mla_w_ukv=mla_w_ukv, lru_w_in=lru_w_in, lru_conv_w=lru_conv_w, lru_conv_b=lru_conv_b,
                   lru_w_rgate=lru_w_rgate, lru_b_rgate=lru_b_rgate, lru_w_igate=lru_w_igate, lru_b_igate=lru_b_igate,
                   lru_lambda=lru_lambda, w_mem_kv=w_mem_kv, w_out=w_out, ln_g=ln_g, ln_b=ln_b)
    m_in = dict(mla_w_in=m_mla_w_in, mla_q_norm=m_mla_q_norm, mla_w_uq=m_mla_w_uq, mla_kv_norm=m_mla_kv_norm,
                mla_w_ukv=m_mla_w_ukv, lru_w_in=m_lru_w_in, lru_conv_w=m_lru_conv_w, lru_conv_b=m_lru_conv_b,
                lru_w_rgate=m_lru_w_rgate, lru_b_rgate=m_lru_b_rgate, lru_w_igate=m_lru_w_igate,
                lru_b_igate=m_lru_b_igate, lru_lambda=m_lru_lambda, w_mem_kv=m_w_mem_kv, w_out=m_w_out, ln_g=m_ln_g,
                ln_b=m_ln_b)
    v_in = dict(mla_w_in=v_mla_w_in, mla_q_norm=v_mla_q_norm, mla_w_uq=v_mla_w_uq, mla_kv_norm=v_mla_kv_norm,
                mla_w_ukv=v_mla_w_ukv, lru_w_in=v_lru_w_in, lru_conv_w=v_lru_conv_w, lru_conv_b=v_lru_conv_b,
                lru_w_rgate=v_lru_w_rgate, lru_b_rgate=v_lru_b_rgate, lru_w_igate=v_lru_w_igate,
                lru_b_igate=v_lru_b_igate, lru_lambda=v_lru_lambda, w_mem_kv=v_w_mem_kv, w_out=v_w_out, ln_g=v_ln_g,
                ln_b=v_ln_b)
    order = ["mla_w_in", "mla_q_norm", "mla_w_uq", "mla_kv_norm", "mla_w_ukv", "lru_w_in", "lru_conv_w", "lru_conv_b",
             "lru_w_rgate", "lru_b_rgate", "lru_w_igate", "lru_b_igate", "lru_lambda", "w_mem_kv", "w_out", "ln_g",
             "ln_b"]
    grads, deltas, new_m, new_v = {}, {}, {}, {}

    def operands(name):
        shape = weights[name].shape
        two_d = (math.prod(shape[:-1]), shape[-1])
        return [a.reshape(two_d) for a in (weights[name], red[name], m_in[name], v_in[name])]

    def keep(name, g2, d2, m2, v2):
        shape = weights[name].shape
        grads[name], deltas[name] = g2.reshape(shape), d2.reshape(shape)
        new_m[name], new_v[name] = m2.reshape(shape), v2.reshape(shape)

    small = [n for n in order if weights[n].size <= 4096]
    ops = [operands(n) for n in small]
    for name, op, res in zip(small, ops, _adamw_small(ops)):
        keep(name, op[1], *res)
    for name in order:
        if name not in small:
            op = operands(name)
            keep(name, op[1], *_adamw(name, *op))
    return (red["loss"][0], g0["x"][None], *[grads[n] for n in order], *[deltas[n] for n in order],
            *[new_m[n] for n in order], *[new_v[n] for n in order])


def _local_step(x2, mem2, pos_col, tgt2, win0_sh, wuq_sh, wukv_sh, gq, gkv, w_rgate, w_igate, ln_g, ln_b,
                mid_weights, late_weights, gather_mid=None, gather_late=None, reduce_late=None):
    s = x2.shape[0]
    win0p = _win0_aligned(win0_sh)
    wuq_p = jnp.pad(wuq_sh.reshape(4, Q_LORA, 3, 96).transpose(1, 0, 2, 3).reshape(Q_LORA, 12, 96),
                    ((0, 0), (0, 0), (0, 32))).reshape(Q_LORA, QK_W)
    wukv3 = wukv_sh.reshape(4, KV_LORA, 3, 128).transpose(1, 0, 2, 3).reshape(KV_LORA, 12, 128)
    wk_p = jnp.pad(wukv3[:, :, :64], ((0, 0), (0, 0), (0, 64))).reshape(KV_LORA, QK_W)
    wv = wukv3[:, :, 64:].reshape(KV_LORA, TOK_W)
    wr_bd = _block_diag4(w_rgate).astype(BF16)
    wi_bd = _block_diag4(w_igate).astype(BF16)
    half = 16
    inv_freq = ROPE_THETA ** (-jnp.arange(half, dtype=F32) / half)
    inv_lane = jnp.concatenate([jnp.zeros((64,), F32), inv_freq, inv_freq, jnp.zeros((32,), F32)]).reshape(1, HEAD_PAD)

    gate0, qm0, cq, ckv, q_p, q_t, k_p, v_b, v_t, ctab, satab, sbtab, *landed = _mla_proj_fwd(
        x2, win0p, gq, gkv, wuq_p, wk_p, wv, pos_col, inv_lane, exchange=gather_mid)
    wmem, wout0 = mid_weights(landed)
    memkv = _mem_kv(mem2, wmem)
    tok0, lse, *landed = _attn_fwd(q_p, k_p, v_t, exchange=gather_late)
    win1, wout1, cw, cb, br, bi, lam = late_weights(landed)
    g0, b0, g1, b1 = ln_g[0:1], ln_b[0:1], ln_g[1:2], ln_b[1:2]
    h1 = _mix_fwd(tok0, gate0, qm0, memkv[0], wout0, x2, g0, b0)
    u1, gate1, qm1, hs1 = _lru_fwd(h1, win1, cw, cb, wr_bd, br, wi_bd, bi, lam)

    dres1, dtok1, dgate1, dqm1, dwout1, dmemkv1, dg1, db1, loss = _mix_bwd(
        hs1, gate1, qm1, memkv[1], wout1, h1, g1, b1, tgt2, True)
    dh1, dwin1, dcw, dcb, dwr_bd, dbr, dwi_bd, dbi, dlam = _lru_bwd(
        dtok1, dgate1, dqm1, dres1, h1, u1, hs1, win1, cw, cb, wr_bd, br, wi_bd, bi, lam)
    late = {"lru_w_in": dwin1, "lru_conv_w": dcw, "lru_conv_b": dcb, "lru_b_rgate": dbr, "lru_b_igate": dbi,
            "lru_lambda": dlam, "w_mem_kv1": _mem_kv_bwd(mem2, dmemkv1), "w_out1": dwout1,
            "lru_w_rgate": _diag_blocks4(dwr_bd), "lru_w_igate": _diag_blocks4(dwi_bd), "ln_g1": dg1, "ln_b1": db1}
    red = reduce_late(late) if reduce_late is not None else None

    dres0, dob, dobt, stats, dgate0, dqm0, dwout0, dmemkv0, dg0, db0, _, *got = _mix_bwd(
        tok0, gate0, qm0, memkv[0], wout0, x2, g0, b0, dh1, False, lse=lse, exchange=red.swap() if red else None)
    if red:
        red.swapped(*got)
    dq_p, dk_t, dv_t, *got = _attn_bwd(q_p, q_t, k_p, v_b, dob, dobt, stats,
                                       exchange=red.scatter() if red else None)
    if red:
        red.scattered(*got)
    if red:
        red.shared(*_run_exchange(red.share(), "share_reduced"))
    dx, dwin0p, dwuq_p, dwk_p, dwv, dgq, dgkv = _mla_proj_bwd(
        x2, cq, ckv, dq_p, dk_t, dv_t, dgate0, dqm0, dres0, win0p, gq, gkv, wuq_p, wk_p, wv,
        ctab, satab, sbtab)

    dwin0 = _win0_shards(dwin0p)
    dwuq = dwuq_p.reshape(Q_LORA, 12, 128)[:, :, :96].reshape(Q_LORA, 1152)
    dwukv = jnp.concatenate([dwk_p.reshape(KV_LORA, 12, 128)[:, :, :64], dwv.reshape(KV_LORA, 12, 64)],
                            axis=2).reshape(KV_LORA, 1536)
    early = {"x": dx, "loss": loss, "mla_w_in": dwin0, "mla_w_uq": dwuq, "mla_w_ukv": dwukv,
             "w_mem_kv0": _mem_kv_bwd(mem2, dmemkv0), "w_out0": dwout0, "mla_q_norm": dgq, "mla_kv_norm": dgkv,
             "ln_g0": dg0, "ln_b0": db0}
    return early, (red if red else late)
```

```python
import functools
import math

import jax
import jax.numpy as jnp
from jax import lax
from jax.experimental import pallas as pl
from jax.experimental.pallas import tpu as pltpu

F32, BF16 = jnp.float32, jnp.bfloat16
MESH = pl.DeviceIdType.MESH

D_MODEL = 1024
N_TOK_HEADS = 12
TOK_W = 768
MEM_W = 256
MEM_LEN = 256
Q_LORA, KV_LORA = 384, 256
HEAD_PAD = 128
QK_W = N_TOK_HEADS * HEAD_PAD
ATT_SCALE = 1.0 / math.sqrt(96.0)
ATT_SCALE_LOG2 = ATT_SCALE * math.log2(math.e)
ROPE_THETA = 10000.0
LRU_C = 8.0
ALPHA = 4.0 ** 0.25
NORM_EPS = 1e-6
ADAM_LR, ADAM_B1, ADAM_B2, ADAM_EPS, ADAM_WD, ADAM_STEP = 0.001, 0.9, 0.999, 1e-08, 0.01, 10

TB_PROJ = 512
TB_PROJ_BWD = 512
TB_MIX = 512
TB_LRU = 256
TQ_ATT = 512
TQ_ATT_FWD = 1024
TK_ATT = 1024
VMEM_LIMIT = 56 * 1024 * 1024
VMEM_LIMIT_PROJ_BWD = 60 * 1024 * 1024


def _mm(a, b):
    return jnp.dot(a.astype(BF16), b.astype(BF16), preferred_element_type=F32)


def _mm_nt(a, b):
    return lax.dot_general(a.astype(BF16), b.astype(BF16), (((1,), (1,)), ((), ())), preferred_element_type=F32)


def _mm_tn(a, b):
    return lax.dot_general(a.astype(BF16), b.astype(BF16), (((0,), (0,)), ((), ())), preferred_element_type=F32)


def _rows(tb, w):
    return pl.BlockSpec((tb, w), lambda i: (i, 0))


def _const(shape):
    n = len(shape)
    return pl.BlockSpec(shape, lambda i: (0,) * n)


def _params(sem=("arbitrary",), vmem_limit=None):
    return pltpu.CompilerParams(dimension_semantics=sem, vmem_limit_bytes=vmem_limit or VMEM_LIMIT)


def _iota(shape, dim):
    return lax.broadcasted_iota(jnp.int32, shape, dim)


def _rope_tables(pos, inv_lane):
    ang = pos.astype(F32) * inv_lane
    lane = _iota(ang.shape, 1)
    cs, sn = jnp.cos(ang), jnp.sin(ang)
    return (jnp.where(lane < 64, 1.0, jnp.where(lane < 96, cs, 0.0)),
            jnp.where((lane >= 64) & (lane < 80), -sn, 0.0), jnp.where((lane >= 80) & (lane < 96), sn, 0.0))


def _rope(t, c, sa, sb):
    return t * c + pltpu.roll(t, 112, 1) * sa + pltpu.roll(t, 16, 1) * sb


def _rope_t(d, c, sa, sb):
    return d * c + pltpu.roll(d * sa, 16, 1) + pltpu.roll(d * sb, 112, 1)


def _rms(c, g):
    r = lax.rsqrt(jnp.mean(c * c, axis=-1, keepdims=True) + NORM_EPS)
    xh = c * r
    return xh * g, xh, r


def _mla_proj_fwd(x, win, gq, gkv, wuq, wukv_k, wukv_v, pos_col, inv_lane, exchange=None):
    s = x.shape[0]
    tb = min(TB_PROJ, s)

    def body(x_ref, win_ref, gq_ref, gkv_ref, wuq_ref, wk_ref, wv_ref, pos_ref, inv_ref,
             gate_ref, qm_ref, cq_ref, ckv_ref, q_ref, qt_ref, k_ref, v_ref, vt_ref, c_ref, sa_ref, sb_ref):
        z = _mm(x_ref[...], win_ref[...])
        gate_ref[...] = z[:, 0:1024]
        qm_ref[...] = z[:, 1024:1280].astype(BF16)
        cq = z[:, 1280:1664]
        ckv = z[:, 1664:1920]
        cq_ref[...] = cq
        ckv_ref[...] = ckv
        c, sa, sb = _rope_tables(pos_ref[...], inv_ref[...])
        c_ref[...], sa_ref[...], sb_ref[...] = c, sa, sb
        nq, _, _ = _rms(cq, gq_ref[...])
        nkv, _, _ = _rms(ckv, gkv_ref[...])
        qf = _mm(nq, wuq_ref[...])
        kf = _mm(nkv, wk_ref[...])
        vf = _mm(nkv, wv_ref[...])
        v_ref[...] = vf.astype(BF16)
        for j in range(N_TOK_HEADS // 2):
            sl = slice(HEAD_PAD * j, HEAD_PAD * (j + 1))
            vt_ref[sl, :] = vf[:, sl].T.astype(BF16)
        kr = _rope(z[:, 1920:2048], c, sa, sb)
        for h in range(N_TOK_HEADS):
            sl = slice(HEAD_PAD * h, HEAD_PAD * (h + 1))
            qh = _rope(qf[:, sl], c, sa, sb) * ATT_SCALE_LOG2
            q_ref[:, sl] = qh.astype(BF16)
            qt_ref[sl, :] = qh.T.astype(BF16)
            k_ref[:, sl] = (kf[:, sl] + kr).astype(BF16)

    outs = (jax.ShapeDtypeStruct((s, 1024), F32), jax.ShapeDtypeStruct((s, MEM_W), BF16),
            jax.ShapeDtypeStruct((s, Q_LORA), F32), jax.ShapeDtypeStruct((s, KV_LORA), F32),
            jax.ShapeDtypeStruct((s, QK_W), BF16), jax.ShapeDtypeStruct((QK_W, s), BF16),
            jax.ShapeDtypeStruct((s, QK_W), BF16),
            jax.ShapeDtypeStruct((s, TOK_W), BF16), jax.ShapeDtypeStruct((TOK_W, s), BF16),
            *[jax.ShapeDtypeStruct((s, HEAD_PAD), F32)] * 3)

    def cols(w):
        return pl.BlockSpec((w, tb), lambda i: (0, i))

    return _run(
        body, name="mla_proj_fwd", grid=(s // tb,), out_shape=outs,
        in_specs=[_rows(tb, 1024), _const((1024, 2048)), _const((1, Q_LORA)), _const((1, KV_LORA)),
                  _const((Q_LORA, QK_W)), _const((KV_LORA, QK_W)), _const((KV_LORA, TOK_W)),
                  _rows(tb, 1), _const((1, HEAD_PAD))],
        out_specs=(_rows(tb, 1024), _rows(tb, MEM_W), _rows(tb, Q_LORA), _rows(tb, KV_LORA),
                   _rows(tb, QK_W), cols(QK_W), _rows(tb, QK_W), _rows(tb, TOK_W), cols(TOK_W),
                   _rows(tb, HEAD_PAD), _rows(tb, HEAD_PAD), _rows(tb, HEAD_PAD)),
        args=(x, win, gq, gkv, wuq, wukv_k, wukv_v, pos_col, inv_lane), sem=("parallel",), exchange=exchange)


def _mla_proj_bwd(x, cq, ckv, dq, dkt, dvt, dgate, dqm, dres, win, gq, gkv, wuq, wukv_k, wukv_v, ctab, satab, sbtab):
    s = x.shape[0]
    tb = min(TB_PROJ_BWD, s)

    def body(x_ref, cq_ref, ckv_ref, dq_ref, dkt_ref, dvt_ref, dgate_ref, dqm_ref, dres_ref, win_ref, gq_ref, gkv_ref,
             wuq_ref, wk_ref, wv_ref, c_ref, sa_ref, sb_ref,
             dx_ref, dwin_ref, dwuq_ref, dwk_ref, dwv_ref, dgq_ref, dgkv_ref):
        @pl.when(pl.program_id(0) == 0)
        def _():
            for r in (dwin_ref, dwuq_ref, dwk_ref, dwv_ref, dgq_ref, dgkv_ref):
                r[...] = jnp.zeros_like(r)

        c, sa, sb = c_ref[...], sa_ref[...], sb_ref[...]
        lane = _iota((tb, HEAD_PAD), 1)
        gq, gkv = gq_ref[...], gkv_ref[...]
        nq, xhq, rq = _rms(cq_ref[...], gq)
        nkv, xhk, rk = _rms(ckv_ref[...], gkv)
        dkp = dkt_ref[...].T * math.log(2.0)
        dqs, dkr = [], jnp.zeros((tb, HEAD_PAD), F32)
        for h in range(N_TOK_HEADS):
            sl = slice(HEAD_PAD * h, HEAD_PAD * (h + 1))
            dqs.append(_rope_t(dq_ref[:, sl], c, sa, sb).astype(BF16))
            dkr = dkr + dkp[:, sl]
        dqf = jnp.concatenate(dqs, axis=1)
        dkr = jnp.where((lane >= 64) & (lane < 96), _rope_t(dkr, c, sa, sb), 0.0)
        dvb = dvt_ref[...].T.astype(BF16)
        dkb = dkp.astype(BF16)
        dnq = _mm_nt(dqf, wuq_ref[...])
        dwuq_ref[...] += _mm_tn(nq, dqf)
        dgq_ref[...] += jnp.sum(dnq * xhq, axis=0, keepdims=True)
        dxh = dnq * gq
        dcq = rq * (dxh - xhq * jnp.mean(dxh * xhq, axis=-1, keepdims=True))
        dnkv = _mm_nt(dkb, wk_ref[...]) + _mm_nt(dvb, wv_ref[...])
        nkvb = nkv.astype(BF16)
        dwk_ref[...] += _mm_tn(nkvb, dkb)
        dwv_ref[...] += _mm_tn(nkvb, dvb)
        dgkv_ref[...] += jnp.sum(dnkv * xhk, axis=0, keepdims=True)
        dxh = dnkv * gkv
        dckv = rk * (dxh - xhk * jnp.mean(dxh * xhk, axis=-1, keepdims=True))
        dz = jnp.concatenate([dgate_ref[...], dqm_ref[...], dcq, dckv, dkr], axis=1).astype(BF16)
        dx_ref[...] = _mm_nt(dz, win_ref[...]) + dres_ref[...]
        dwin_ref[...] += _mm_tn(x_ref[...], dz)

    outs = (jax.ShapeDtypeStruct((s, 1024), F32), jax.ShapeDtypeStruct((1024, 2048), F32),
            jax.ShapeDtypeStruct((Q_LORA, QK_W), F32), jax.ShapeDtypeStruct((KV_LORA, QK_W), F32),
            jax.ShapeDtypeStruct((KV_LORA, TOK_W), F32), jax.ShapeDtypeStruct((1, Q_LORA), F32),
            jax.ShapeDtypeStruct((1, KV_LORA), F32))
    return _run(
        body, name="mla_proj_bwd", grid=(s // tb,), out_shape=outs,
        in_specs=[_rows(tb, 1024), _rows(tb, Q_LORA), _rows(tb, KV_LORA), _rows(tb, QK_W),
                  pl.BlockSpec((QK_W, tb), lambda i: (0, i)), pl.BlockSpec((TOK_W, tb), lambda i: (0, i)),
                  _rows(tb, 1024), _rows(tb, MEM_W), _rows(tb, 1024),
                  _const((1024, 2048)), _const((1, Q_LORA)), _const((1, KV_LORA)),
                  _const((Q_LORA, QK_W)), _const((KV_LORA, QK_W)), _const((KV_LORA, TOK_W)),
                  _rows(tb, HEAD_PAD), _rows(tb, HEAD_PAD), _rows(tb, HEAD_PAD)],
        out_specs=(_rows(tb, 1024), _const((1024, 2048)), _const((Q_LORA, QK_W)), _const((KV_LORA, QK_W)),
                   _const((KV_LORA, TOK_W)), _const((1, Q_LORA)), _const((1, KV_LORA))),
        args=(x, cq, ckv, dq, dkt, dvt, dgate, dqm, dres, win, gq, gkv, wuq, wukv_k, wukv_v, ctab, satab, sbtab),
        sem=("arbitrary",), vmem_limit=VMEM_LIMIT_PROJ_BWD)


def _attn_fwd(q, k, vt, exchange=None):
    s = q.shape[0]
    tq = min(TQ_ATT_FWD, s)
    tk = min(TK_ATT, s)

    def body(q_ref, k_ref, vt_ref, o_ref, lse_ref):
        i = pl.program_id(1)
        nfull = (i * tq) // tk
        krow = _iota((tk, tq), 0)
        qpos = i * tq + _iota((tk, tq), 1)

        def head_tile(hh, st, carry, masked):
            hs = slice(HEAD_PAD * hh, HEAD_PAD * (hh + 1))
            m, l, acc = carry
            sc = _mm_nt(k_ref[pl.ds(st, tk), hs], q_ref[:, hs])
            if masked:
                sc = jnp.where(st + krow <= qpos, sc, -jnp.inf)
            m_new = jnp.maximum(m, jnp.max(sc, axis=0, keepdims=True))
            p = jnp.exp2(sc - m_new)
            a = jnp.exp2(m - m_new)
            l = a * l + jnp.sum(p, axis=0, keepdims=True)
            acc = a * acc + _mm(vt_ref[64 * hh:64 * (hh + 1), pl.ds(st, tk)], p)
            return m_new, l, acc

        def tile(j, carry, masked):
            st = pl.multiple_of(j * tk, tk)
            return tuple(head_tile(hh, st, carry[hh], masked) for hh in range(2))

        def init():
            return (jnp.full((1, tq), -jnp.inf, F32), jnp.zeros((1, tq), F32), jnp.zeros((64, tq), F32))

        carry = lax.fori_loop(0, nfull, functools.partial(tile, masked=False), (init(), init()))
        (ma, la, acca), (mb, lb, accb) = tile(nfull, carry, True)
        o_ref[...] = jnp.concatenate([acca / la, accb / lb], axis=0).T
        lse_ref[...] = jnp.concatenate([jnp.broadcast_to(ma + jnp.log2(la), (64, tq)),
                                        jnp.broadcast_to(mb + jnp.log2(lb), (64, tq))], axis=0).T

    shp = jax.ShapeDtypeStruct((s, TOK_W), F32)
    return _run(
        body, name="attn_fwd", grid=(N_TOK_HEADS // 2, s // tq), out_shape=(shp, shp),
        in_specs=[pl.BlockSpec((tq, 2 * HEAD_PAD), lambda j, i: (i, j)),
                  pl.BlockSpec((s, 2 * HEAD_PAD), lambda j, i: (0, j)),
                  pl.BlockSpec((HEAD_PAD, s), lambda j, i: (j, 0))],
        out_specs=(pl.BlockSpec((tq, HEAD_PAD), lambda j, i: (i, j)),) * 2,
        args=(q, k, vt), sem=("parallel", "arbitrary"), exchange=exchange)


def _attn_stats(o, do, lse_ref, dob_ref, dot_ref, st_ref):
    lane = _iota((o.shape[0], HEAD_PAD), 1)
    dob_ref[...] = do.astype(BF16)
    prod = do * o
    for j in range(N_TOK_HEADS // 2):
        sl = slice(HEAD_PAD * j, HEAD_PAD * (j + 1))
        dot_ref[sl, :] = do[:, sl].T.astype(BF16)
        pj = prod[:, sl]
        da = jnp.sum(jnp.where(lane < 64, pj, 0.0), axis=-1, keepdims=True)
        db = jnp.sum(jnp.where(lane >= 64, pj, 0.0), axis=-1, keepdims=True)
        la = lse_ref[:, HEAD_PAD * j:HEAD_PAD * j + 1]
        lb = lse_ref[:, HEAD_PAD * j + 64:HEAD_PAD * j + 65]
        st_ref[j] = jnp.where(lane == 0, la, jnp.where(lane == 1, lb, jnp.where(lane == 2, da,
                                                                                 jnp.where(lane == 3, db, 0.0))))


def _attn_bwd(q, qt, k, v, dob, dobt, stats, exchange=None):
    s = q.shape[0]
    t = min(TQ_ATT, s)
    nq = s // t

    def body(q_ref, qt_ref, do_ref, dot_ref, st_ref, k_ref, v_ref, dq_ref, dkt_ref, dvt_ref):
        i = pl.program_id(1)

        @pl.when(i == 0)
        def _():
            dkt_ref[...] = jnp.zeros_like(dkt_ref)
            dvt_ref[...] = jnp.zeros_like(dvt_ref)

        lane = _iota((t, HEAD_PAD), 1)
        qpos, kcol = _iota((t, t), 0), _iota((t, t), 1)
        do = do_ref[...]
        stats = st_ref[0]

        def head_tile(hh, ks, dq_acc, masked):
            hs = slice(HEAD_PAD * hh, HEAD_PAD * (hh + 1))
            qh = q_ref[:, hs]
            kh = k_ref[pl.ds(ks, t), hs]
            dom = jnp.where((lane < 64) if hh == 0 else (lane >= 64), do, jnp.zeros_like(do))
            lse = stats[:, hh:hh + 1]
            dlt = stats[:, 2 + hh:3 + hh]
            sc = _mm_nt(qh, kh)
            if masked:
                sc = jnp.where(kcol <= qpos, sc, -jnp.inf)
            p = jnp.exp2(sc - lse)
            dp = _mm_nt(dom, v_ref[pl.ds(ks, t), :])
            ds = (p * (dp - dlt)).astype(BF16)
            dvt_ref[64 * hh:64 * (hh + 1), pl.ds(ks, t)] += _mm(dot_ref[64 * hh:64 * (hh + 1), :], p)
            dkt_ref[HEAD_PAD * hh:HEAD_PAD * hh + 96, pl.ds(ks, t)] += _mm(qt_ref[HEAD_PAD * hh:HEAD_PAD * hh + 96, :], ds)
            return dq_acc + _mm(ds, kh)

        def tile(j, carry, masked):
            ks = pl.multiple_of(j * t, t)
            return tuple(head_tile(hh, ks, carry[hh], masked) for hh in range(2))

        zero = jnp.zeros((t, HEAD_PAD), F32)
        carry = lax.fori_loop(0, i, functools.partial(tile, masked=False), (zero, zero))
        dqa, dqb = tile(i, carry, True)
        dq_ref[...] = jnp.concatenate([dqa, dqb], axis=1) * ATT_SCALE

    return _run(
        body, name="attn_bwd", grid=(N_TOK_HEADS // 2, nq),
        out_shape=(jax.ShapeDtypeStruct((s, QK_W), F32), jax.ShapeDtypeStruct((QK_W, s), F32),
                   jax.ShapeDtypeStruct((TOK_W, s), F32)),
        in_specs=[pl.BlockSpec((t, 2 * HEAD_PAD), lambda j, i: (i, j)),
                  pl.BlockSpec((2 * HEAD_PAD, t), lambda j, i: (j, i)),
                  pl.BlockSpec((t, HEAD_PAD), lambda j, i: (i, j)),
                  pl.BlockSpec((HEAD_PAD, t), lambda j, i: (j, i)),
                  pl.BlockSpec((1, t, HEAD_PAD), lambda j, i: (j, i, 0)),
                  pl.BlockSpec((s, 2 * HEAD_PAD), lambda j, i: (0, j)),
                  pl.BlockSpec((s, HEAD_PAD), lambda j, i: (0, j))],
        out_specs=(pl.BlockSpec((t, 2 * HEAD_PAD), lambda j, i: (i, j)),
                   pl.BlockSpec((2 * HEAD_PAD, s), lambda j, i: (j, 0)),
                   pl.BlockSpec((HEAD_PAD, s), lambda j, i: (j, 0))),
        args=(q, qt, dob, dobt, stats, k, v), sem=("parallel", "arbitrary"), exchange=exchange)


def _mem_kv(mem, wmem):
    def body(m_ref, w_ref, o_ref):
        o_ref[0] = _mm(m_ref[...], w_ref[0]).astype(BF16)

    return pl.pallas_call(
        body, name="mem_kv", grid=(2,), out_shape=jax.ShapeDtypeStruct((2, MEM_LEN, 512), BF16),
        in_specs=[_const((MEM_LEN, 1024)), pl.BlockSpec((1, 1024, 512), lambda l: (l, 0, 0))],
        out_specs=pl.BlockSpec((1, MEM_LEN, 512), lambda l: (l, 0, 0)),
        compiler_params=_params(("parallel",)),
    )(mem, wmem)


def _mem_kv_bwd(mem, dmemkv):
    def body(m_ref, d_ref, o_ref):
        o_ref[...] = _mm_tn(m_ref[...], d_ref[...])

    return pl.pallas_call(
        body, name="mem_kv_bwd", grid=(1,), out_shape=jax.ShapeDtypeStruct((1024, 512), F32),
        in_specs=[_const((MEM_LEN, 1024)), _const((MEM_LEN, 512))], out_specs=_const((1024, 512)),
        compiler_params=_params(("arbitrary",)),
    )(mem, dmemkv)


def _head_mask(lane, sub):
    return (lane < 64) if sub == 0 else (lane >= 64)


def _mem_attn(qm, kv):
    tb = qm.shape[0]
    lane = _iota((tb, HEAD_PAD), 1)
    outs, ps = [], []
    for pp in range(2):
        qp = qm[:, HEAD_PAD * pp:HEAD_PAD * (pp + 1)]
        kp = kv[:, HEAD_PAD * pp:HEAD_PAD * (pp + 1)]
        vp = kv[:, MEM_W + HEAD_PAD * pp:MEM_W + HEAD_PAD * (pp + 1)]
        pair = None
        for sub in range(2):
            qh = jnp.where(_head_mask(lane, sub), qp, jnp.zeros_like(qp))
            sc = _mm_nt(qh, kp) * 0.125
            e = jnp.exp(sc - jnp.max(sc, axis=-1, keepdims=True))
            p = e / jnp.sum(e, axis=-1, keepdims=True)
            o = _mm(p, vp)
            ps.append(p)
            pair = o if sub == 0 else jnp.where(lane < 64, pair, o)
        outs.append(pair)
    return jnp.concatenate(outs, axis=1), ps


def _mem_attn_bwd(dmo, qm, kv, ps):
    tb = qm.shape[0]
    lane = _iota((tb, HEAD_PAD), 1)
    dqs, dks, dvs = [], [], []
    for pp in range(2):
        qp = qm[:, HEAD_PAD * pp:HEAD_PAD * (pp + 1)]
        kp = kv[:, HEAD_PAD * pp:HEAD_PAD * (pp + 1)]
        vp = kv[:, MEM_W + HEAD_PAD * pp:MEM_W + HEAD_PAD * (pp + 1)]
        dop = dmo[:, HEAD_PAD * pp:HEAD_PAD * (pp + 1)]
        dq_pair, dk_pair, dv_pair = None, None, None
        for sub in range(2):
            msk = _head_mask(lane, sub)
            p = ps[2 * pp + sub]
            qh = jnp.where(msk, qp, jnp.zeros_like(qp))
            doh = jnp.where(msk, dop, 0.0).astype(BF16)
            dv = _mm_tn(p, doh)
            dp = _mm_nt(doh, vp)
            ds = (p * (dp - jnp.sum(dp * p, axis=-1, keepdims=True)) * 0.125).astype(BF16)
            dq = _mm(ds, kp)
            dk = _mm_tn(ds, qh)
            if sub == 0:
                dq_pair, dk_pair, dv_pair = dq, dk, dv
            else:
                dq_pair = jnp.where(lane < 64, dq_pair, dq)
                dk_pair, dv_pair = dk_pair + dk, dv_pair + dv
        dqs.append(dq_pair)
        dks.append(dk_pair)
        dvs.append(dv_pair)
    return jnp.concatenate(dqs, axis=1), jnp.concatenate(dks + dvs, axis=1)


def _mix_core(tok, gate, qm, kv, wout, h_in, g, b):
    mem_out, ps = _mem_attn(qm, kv)
    cat = jnp.concatenate([tok, mem_out], axis=1)
    sg = jax.nn.sigmoid(gate)
    sl = gate * sg
    y = cat * sl
    r = ALPHA * h_in + _mm(y, wout)
    mu = jnp.mean(r, axis=-1, keepdims=True)
    xc = r - mu
    rstd = lax.rsqrt(jnp.mean(xc * xc, axis=-1, keepdims=True) + NORM_EPS)
    xh = xc * rstd
    return xh * g + b, (ps, cat, sg, sl, y, xh, rstd)


def _mix_fwd(tok, gate, qm, kv, wout, h_in, g, b):
    s = tok.shape[0]
    tb = min(TB_MIX, s)

    def body(tok_ref, gate_ref, qm_ref, kv_ref, w_ref, h_ref, g_ref, b_ref, o_ref):
        o_ref[...], _ = _mix_core(tok_ref[...], gate_ref[...], qm_ref[...], kv_ref[...], w_ref[...], h_ref[...],
                                  g_ref[...], b_ref[...])

    return pl.pallas_call(
        body, name="mix_fwd", grid=(s // tb,), out_shape=jax.ShapeDtypeStruct((s, 1024), F32),
        in_specs=[_rows(tb, TOK_W), _rows(tb, 1024), _rows(tb, MEM_W), _const((MEM_LEN, 512)), _const((1024, 1024)),
                  _rows(tb, 1024), _const((1, 1024)), _const((1, 1024))],
        out_specs=_rows(tb, 1024), compiler_params=_params(("parallel",)),
    )(tok, gate, qm, kv, wout, h_in, g, b)


def _mix_bwd(tok, gate, qm, kv, wout, h_in, g, b, up, from_loss, lse=None, exchange=None):
    s = tok.shape[0]
    tb = min(TB_MIX, s)
    n_in = 9 if lse is None else 10
    n_tok_out = 1 if lse is None else 3

    def body(*refs):
        tok_ref, gate_ref, qm_ref, kv_ref, w_ref, h_ref, g_ref, b_ref, up_ref = refs[:9]
        dres_ref, tok_out = refs[n_in], refs[n_in + 1:n_in + 1 + n_tok_out]
        dgate_ref, dqm_ref, dw_ref, dkv_ref, dg_ref, db_ref, loss_ref = refs[n_in + 1 + n_tok_out:]

        @pl.when(pl.program_id(0) == 0)
        def _():
            for r in (dw_ref, dkv_ref, dg_ref, db_ref, loss_ref):
                r[...] = jnp.zeros_like(r)

        gate, qm, kv, wout, g = gate_ref[...], qm_ref[...], kv_ref[...], w_ref[...], g_ref[...]
        h_out, (ps, cat, sg, sl, y, xh, rstd) = _mix_core(tok_ref[...], gate, qm, kv, wout, h_ref[...], g, b_ref[...])
        if from_loss:
            diff = h_out - up_ref[...]
            loss_ref[...] += 0.5 * jnp.sum(jnp.mean(diff * diff, axis=-1, keepdims=True), axis=0, keepdims=True)
            dh = diff * (1.0 / D_MODEL)
        else:
            dh = up_ref[...]
        dg_ref[...] += jnp.sum(dh * xh, axis=0, keepdims=True)
        db_ref[...] += jnp.sum(dh, axis=0, keepdims=True)
        dxh = dh * g
        dr = rstd * (dxh - jnp.mean(dxh, axis=-1, keepdims=True) - xh * jnp.mean(dxh * xh, axis=-1, keepdims=True))
        dres_ref[...] = ALPHA * dr
        drb = dr.astype(BF16)
        dy = _mm_nt(drb, wout)
        dw_ref[...] += _mm_tn(y, drb)
        dcat = dy * sl
        dgate_ref[...] = dy * cat * (sg * (1.0 + gate * (1.0 - sg)))
        if lse is None:
            tok_out[0][...] = dcat[:, :TOK_W]
        else:
            _attn_stats(tok_ref[...], dcat[:, :TOK_W], refs[9], *tok_out)
        dqm, dkv = _mem_attn_bwd(dcat[:, TOK_W:], qm, kv, ps)
        dqm_ref[...] = dqm
        dkv_ref[...] += dkv

    npair = N_TOK_HEADS // 2
    tok_shapes = [jax.ShapeDtypeStruct((s, TOK_W), F32)] if lse is None else [
        jax.ShapeDtypeStruct((s, TOK_W), BF16), jax.ShapeDtypeStruct((TOK_W, s), BF16),
        jax.ShapeDtypeStruct((npair, s, HEAD_PAD), F32)]
    tok_specs = [_rows(tb, TOK_W)] if lse is None else [
        _rows(tb, TOK_W), pl.BlockSpec((TOK_W, tb), lambda i: (0, i)),
        pl.BlockSpec((npair, tb, HEAD_PAD), lambda i: (0, i, 0))]
    outs = (jax.ShapeDtypeStruct((s, 1024), F32), *tok_shapes,
            jax.ShapeDtypeStruct((s, 1024), F32), jax.ShapeDtypeStruct((s, MEM_W), F32),
            jax.ShapeDtypeStruct((1024, 1024), F32), jax.ShapeDtypeStruct((MEM_LEN, 512), F32),
            jax.ShapeDtypeStruct((1, 1024), F32), jax.ShapeDtypeStruct((1, 1024), F32),
            jax.ShapeDtypeStruct((1, 1), F32))
    args = (tok, gate, qm, kv, wout, h_in, g, b, up) + (() if lse is None else (lse,))
    return _run(
        body, name="mix_bwd_loss" if from_loss else "mix_bwd", grid=(s // tb,), out_shape=outs,
        in_specs=[_rows(tb, TOK_W), _rows(tb, 1024), _rows(tb, MEM_W), _const((MEM_LEN, 512)), _const((1024, 1024)),
                  _rows(tb, 1024), _const((1, 1024)), _const((1, 1024)), _rows(tb, 1024)]
        + ([] if lse is None else [_rows(tb, TOK_W)]),
        out_specs=(_rows(tb, 1024), *tok_specs, _rows(tb, 1024), _rows(tb, MEM_W), _const((1024, 1024)),
                   _const((MEM_LEN, 512)), _const((1, 1024)), _const((1, 1024)), _const((1, 1))),
        args=args, sem=("arbitrary",), exchange=exchange)


def _shift_down(u, tail, k):
    if k == 0:
        return u
    r = pltpu.roll(u, k, 0)
    row8 = _iota((8, u.shape[1]), 0)
    head = jnp.where(row8 < k, pltpu.roll(tail, k, 0), r[:8])
    return jnp.concatenate([head, r[8:]], axis=0)


def _shift_up(d, head, k):
    if k == 0:
        return d
    n = d.shape[0]
    r = pltpu.roll(d, n - k, 0)
    row8 = _iota((8, d.shape[1]), 0)
    last = jnp.where(row8 >= 8 - k, pltpu.roll(head, 8 - k, 0), r[n - 8:])
    return jnp.concatenate([r[:n - 8], last], axis=0)


def _scan_down(a, b):
    n = a.shape[0]
    row = _iota(a.shape, 0)
    s = 1
    while s < n:
        ok = row >= s
        a_s = jnp.where(ok, pltpu.roll(a, s, 0), 1.0)
        b_s = jnp.where(ok, pltpu.roll(b, s, 0), 0.0)
        b = a * b_s + b
        a = a * a_s
        s *= 2
    return a, b


def _scan_up(a, b):
    n = a.shape[0]
    row = _iota(a.shape, 0)
    s = 1
    while s < n:
        ok = row < n - s
        a_s = jnp.where(ok, pltpu.roll(a, n - s, 0), 1.0)
        b_s = jnp.where(ok, pltpu.roll(b, n - s, 0), 0.0)
        b = a * b_s + b
        a = a * a_s
        s *= 2
    return a, b


def _neg_expm1(x):
    poly = -x * (1.0 + x * (0.5 + x * (1.0 / 6.0 + x * (1.0 / 24.0 + x * (1.0 / 120.0)))))
    return jnp.where(x > -0.1, poly, 1.0 - jnp.exp(x))


def _softplus(x):
    return jnp.maximum(x, 0.0) + jnp.log(1.0 + jnp.exp(-jnp.abs(x)))


def _lru_gates(u, tail, cw, cb, wr, br, wi, bi, lam):
    us = [_shift_down(u, tail, k) for k in range(4)]
    xc = cb + us[3] * cw[0:1] + us[2] * cw[1:2] + us[1] * cw[2:3] + us[0] * cw[3:4]
    xb = xc.astype(BF16)
    pre_r = jnp.concatenate([_mm(xb[:, 256 * g:256 * (g + 1)], wr[g]) for g in range(3)], axis=1) + br
    pre_i = jnp.concatenate([_mm(xb[:, 256 * g:256 * (g + 1)], wi[g]) for g in range(3)], axis=1) + bi
    rg, ig = jax.nn.sigmoid(pre_r), jax.nn.sigmoid(pre_i)
    clam = -LRU_C * _softplus(-lam)
    la = clam * rg
    a = jnp.exp(la)
    mm = jnp.sqrt(_neg_expm1(2.0 * la))
    return us, xc, xb, rg, ig, clam, la, a, mm


def _lru_fwd(h, win, cw, cb, wr, br, wi, bi, lam):
    s = h.shape[0]
    tb = min(TB_LRU, s)

    def body(h_ref, win_ref, cw_ref, cb_ref, wr_ref, br_ref, wi_ref, bi_ref, lam_ref,
             u_ref, gate_ref, qm_ref, hs_ref, tail_sc, carry_sc):
        @pl.when(pl.program_id(0) == 0)
        def _():
            tail_sc[...] = jnp.zeros_like(tail_sc)
            carry_sc[...] = jnp.zeros_like(carry_sc)

        hb = h_ref[...].astype(BF16)
        z = jnp.concatenate([_mm(hb, win_ref[sh]) for sh in range(4)], axis=1)
        u = z[:, :TOK_W]
        u_ref[...] = u
        gate_ref[...] = z[:, TOK_W:TOK_W + 1024]
        qm_ref[...] = z[:, TOK_W + 1024:].astype(BF16)
        _, xc, _, _, ig, _, _, a, mm = _lru_gates(u, tail_sc[...], cw_ref[...], cb_ref[...], wr_ref[...], br_ref[...],
                                                 wi_ref[...], bi_ref[...], lam_ref[...])
        big_a, big_b = _scan_down(a, mm * (ig * xc))
        hs = big_a * carry_sc[0:1, :] + big_b
        hs_ref[...] = hs
        tail_sc[...] = u[tb - 8:, :]
        carry_sc[...] = jnp.broadcast_to(hs[tb - 1:tb, :], carry_sc.shape)

    outs = (jax.ShapeDtypeStruct((s, TOK_W), F32), jax.ShapeDtypeStruct((s, 1024), F32),
            jax.ShapeDtypeStruct((s, MEM_W), BF16), jax.ShapeDtypeStruct((s, TOK_W), F32))
    return pl.pallas_call(
        body, name="lru_fwd", grid=(s // tb,), out_shape=outs,
        in_specs=[_rows(tb, 1024), _const((4, 1024, 512)), _const((4, TOK_W)), _const((1, TOK_W)),
                  _const((3, 256, 256)), _const((1, TOK_W)), _const((3, 256, 256)), _const((1, TOK_W)),
                  _const((1, TOK_W))],
        out_specs=(_rows(tb, TOK_W), _rows(tb, 1024), _rows(tb, MEM_W), _rows(tb, TOK_W)),
        scratch_shapes=[pltpu.VMEM((8, TOK_W), F32), pltpu.VMEM((8, TOK_W), F32)],
        compiler_params=_params(),
    )(h, win, cw, cb, wr, br, wi, bi, lam)


def _lru_bwd(dhs, dgate, dqm, dres, h, u, hs, win, cw, cb, wr, br, wi, bi, lam):
    s = h.shape[0]
    tb = min(TB_LRU, s)
    nb = s // tb

    def rev(w):
        return pl.BlockSpec((tb, w), lambda i: (nb - 1 - i, 0))

    def prev_tail(w):
        return pl.BlockSpec((8, w), lambda i: (jnp.maximum((nb - 1 - i) * (tb // 8) - 1, 0), 0))

    def body(dhs_ref, dgate_ref, dqm_ref, dres_ref, h_ref, u_ref, hs_ref, ut_ref, hst_ref, win_ref, cw_ref, cb_ref,
             wr_ref, br_ref, wi_ref, bi_ref, lam_ref,
             dh_ref, dwin_ref, dcw_ref, dcb_ref, dwr_ref, dbr_ref, dwi_ref, dbi_ref, dlam_ref, ecar_sc, dxc_sc):
        i = pl.program_id(0)

        @pl.when(i == 0)
        def _():
            for r in (dwin_ref, dcw_ref, dcb_ref, dwr_ref, dbr_ref, dwi_ref, dbi_ref, dlam_ref, ecar_sc, dxc_sc):
                r[...] = jnp.zeros_like(r)

        first = (i == nb - 1)
        u = u_ref[...]
        utail = jnp.where(first, 0.0, ut_ref[...])
        hstail = jnp.where(first, 0.0, hst_ref[...])
        cw, wr, wi, lam = cw_ref[...], wr_ref[...], wi_ref[...], lam_ref[...]
        us, xc, xb, rg, ig, clam, la, a, mm = _lru_gates(u, utail, cw, cb_ref[...], wr, br_ref[...], wi, bi_ref[...], lam)
        row = _iota(a.shape, 0)
        a_next = jnp.where(row < tb - 1, pltpu.roll(a, tb - 1, 0), 1.0)
        big_a, big_b = _scan_up(a_next, dhs_ref[...])
        e = big_a * ecar_sc[0:1, :] + big_b
        ecar_sc[...] = jnp.broadcast_to(a[0:1, :] * e[0:1, :], ecar_sc.shape)
        hs_prev = _shift_down(hs_ref[...], hstail, 1)
        da = e * hs_prev
        ix = ig * xc
        dmm = e * ix
        dix = e * mm
        dla = da * a - dmm * (a * a) / mm
        dlam_ref[...] += jnp.sum(dla * rg, axis=0, keepdims=True)
        dpr = (dla * clam) * rg * (1.0 - rg)
        dpi = (dix * xc) * ig * (1.0 - ig)
        dbr_ref[...] += jnp.sum(dpr, axis=0, keepdims=True)
        dbi_ref[...] += jnp.sum(dpi, axis=0, keepdims=True)
        dprb, dpib = dpr.astype(BF16), dpi.astype(BF16)
        dxc_g = []
        for g in range(3):
            sl = slice(256 * g, 256 * (g + 1))
            dwr_ref[g] += _mm_tn(xb[:, sl], dprb[:, sl])
            dwi_ref[g] += _mm_tn(xb[:, sl], dpib[:, sl])
            dxc_g.append(_mm_nt(dprb[:, sl], wr[g]) + _mm_nt(dpib[:, sl], wi[g]))
        dxc = dix * ig + jnp.concatenate(dxc_g, axis=1)
        dcb_ref[...] += jnp.sum(dxc, axis=0, keepdims=True)
        dcw_ref[...] += jnp.concatenate([jnp.sum(dxc * us[3 - tap], axis=0, keepdims=True) for tap in range(4)], axis=0)
        head = dxc_sc[...]
        du = dxc * cw[3:4]
        for k in range(1, 4):
            du = du + _shift_up(dxc, head, k) * cw[3 - k:4 - k]
        dxc_sc[...] = dxc[:8, :]
        dz = jnp.concatenate([du, dgate_ref[...], dqm_ref[...]], axis=1).astype(BF16)
        hb = h_ref[...].astype(BF16)
        dh = dres_ref[...]
        for sh in range(4):
            dzs = dz[:, 512 * sh:512 * (sh + 1)]
            dh = dh + _mm_nt(dzs, win_ref[sh])
            dwin_ref[sh] += _mm_tn(hb, dzs)
        dh_ref[...] = dh

        @pl.when(i == nb - 1)
        def _():
            dlam_ref[...] = dlam_ref[...] * (LRU_C * jax.nn.sigmoid(-lam))

    outs = (jax.ShapeDtypeStruct((s, 1024), F32), jax.ShapeDtypeStruct((4, 1024, 512), F32),
            jax.ShapeDtypeStruct((4, TOK_W), F32), jax.ShapeDtypeStruct((1, TOK_W), F32),
            jax.ShapeDtypeStruct((3, 256, 256), F32), jax.ShapeDtypeStruct((1, TOK_W), F32),
            jax.ShapeDtypeStruct((3, 256, 256), F32), jax.ShapeDtypeStruct((1, TOK_W), F32),
            jax.ShapeDtypeStruct((1, TOK_W), F32))
    return pl.pallas_call(
        body, name="lru_bwd", grid=(nb,), out_shape=outs,
        in_specs=[rev(TOK_W), rev(1024), rev(MEM_W), rev(1024), rev(1024), rev(TOK_W), rev(TOK_W),
                  prev_tail(TOK_W), prev_tail(TOK_W),
                  _const((4, 1024, 512)), _const((4, TOK_W)), _const((1, TOK_W)), _const((3, 256, 256)),
                  _const((1, TOK_W)), _const((3, 256, 256)), _const((1, TOK_W)), _const((1, TOK_W))],
        out_specs=(rev(1024), _const((4, 1024, 512)), _const((4, TOK_W)), _const((1, TOK_W)), _const((3, 256, 256)),
                   _const((1, TOK_W)), _const((3, 256, 256)), _const((1, TOK_W)), _const((1, TOK_W))),
        scratch_shapes=[pltpu.VMEM((8, TOK_W), F32), pltpu.VMEM((8, TOK_W), F32)],
        compiler_params=_params(),
    )(dhs, dgate, dqm, dres, h, u, hs, u, hs, win, cw, cb, wr, br, wi, bi, lam)


def _adamw_update(w_ref, g_ref, m_ref, v_ref, d_ref, nm_ref, nv_ref):
    g = g_ref[...]
    nm = ADAM_B1 * m_ref[...] + (1.0 - ADAM_B1) * g
    nv = ADAM_B2 * v_ref[...] + (1.0 - ADAM_B2) * (g * g)
    m_hat = nm / (1.0 - ADAM_B1 ** ADAM_STEP)
    v_hat = nv / (1.0 - ADAM_B2 ** ADAM_STEP)
    d_ref[...] = -ADAM_LR * (m_hat / (jnp.sqrt(v_hat) + ADAM_EPS) + ADAM_WD * w_ref[...])
    nm_ref[...] = nm
    nv_ref[...] = nv


def _adamw(name, w, g, m, v):
    rows, cols = w.shape
    tb = 256 if rows % 256 == 0 else rows

    def body(*refs):
        _adamw_update(*refs)

    shp = jax.ShapeDtypeStruct((rows, cols), F32)
    return pl.pallas_call(
        body, name="adamw_" + name, grid=(rows // tb,), out_shape=(shp, shp, shp),
        in_specs=[_rows(tb, cols)] * 4, out_specs=(_rows(tb, cols),) * 3,
        compiler_params=_params(("parallel",)),
    )(w, g, m, v)


def _adamw_small(items):
    n = len(items)

    def body(*refs):
        for k in range(n):
            _adamw_update(*refs[4 * k:4 * k + 4], *refs[4 * n + 3 * k:4 * n + 3 * k + 3])

    args = [a for it in items for a in it]
    shapes = [jax.ShapeDtypeStruct(it[0].shape, F32) for it in items for _ in range(3)]
    outs = pl.pallas_call(
        body, name="adamw_small", grid=(1,), out_shape=tuple(shapes),
        in_specs=[_const(a.shape) for a in args], out_specs=tuple(_const(sh.shape) for sh in shapes),
        compiler_params=_params(),
    )(*args)
    return [outs[3 * k:3 * k + 3] for k in range(n)]


def _row_block(rows, cap=2048):
    return max(t for t in range(8, cap + 1, 8) if rows % t == 0)


def _add2(a, b):
    rows = a.shape[0]
    tb = _row_block(rows)

    def body(a_ref, b_ref, o_ref):
        o_ref[...] = a_ref[...] + b_ref[...]

    return pl.pallas_call(
        body, name="add_sibling", grid=(rows // tb,), out_shape=jax.ShapeDtypeStruct(a.shape, F32),
        in_specs=[_rows(tb, 128)] * 2, out_specs=_rows(tb, 128), compiler_params=_params(("parallel",)),
    )(a, b)


def _sum_slots(landed, own, rows):
    tb = _row_block(rows, 1024)

    def body(l_ref, o_ref, out_ref):
        t = 2 * lax.axis_index("x") + lax.axis_index("y")
        r = [jnp.where(t == s, o_ref[s], l_ref[s].astype(F32)) for s in range(4)]
        out_ref[...] = ((r[0] + r[1]) + r[2]) + r[3]

    return pl.pallas_call(
        body, name="sum_chips", grid=(rows // tb,), out_shape=jax.ShapeDtypeStruct((rows, 128), F32),
        in_specs=[pl.BlockSpec((4, tb, 128), lambda i: (0, i, 0))] * 2, out_specs=_rows(tb, 128),
        compiler_params=_params(("parallel",)),
    )(landed, own)


_ANY = pl.BlockSpec(memory_space=pl.ANY)


def _place():
    x, y, c = lax.axis_index("x"), lax.axis_index("y"), lax.axis_index("c")
    return x, y, c, [(1 - x, y), (x, 1 - y), (1 - x, 1 - y)]


def _remote(src, dst, ssem, rsem, to):
    return pltpu.make_async_remote_copy(src_ref=src, dst_ref=dst, send_sem=ssem, recv_sem=rsem, device_id=to,
                                        device_id_type=MESH)


class _Exchange:
    def __init__(self, ins, out_shape, sems, start, finish):
        self.ins, self.out_shape, self.sems, self.start, self.finish = ins, out_shape, sems, start, finish


def _run(body, *, name, grid, in_specs, out_specs, out_shape, args, scratch=(), sem, exchange=None,
         vmem_limit=VMEM_LIMIT):
    if exchange is None:
        return pl.pallas_call(body, name=name, grid=grid, out_shape=tuple(out_shape), in_specs=list(in_specs),
                              out_specs=tuple(out_specs), scratch_shapes=list(scratch),
                              compiler_params=_params(sem, vmem_limit))(*args)
    n_in, n_out, n_sc = len(args), len(out_shape), len(scratch)
    k_in, k_out = len(exchange.ins), len(exchange.out_shape)

    def fused(*refs):
        ins, refs = refs[:n_in], refs[n_in:]
        xin, refs = refs[:k_in], refs[k_in:]
        outs, refs = refs[:n_out], refs[n_out:]
        xout, refs = refs[:k_out], refs[k_out:]
        sc, xsem = refs[:n_sc], refs[n_sc:]
        first = pl.program_id(0) == 0
        last = pl.program_id(0) == grid[0] - 1
        for a in range(1, len(grid)):
            first = first & (pl.program_id(a) == 0)
            last = last & (pl.program_id(a) == grid[a] - 1)

        @pl.when(first)
        def _():
            exchange.start(xin, xout, xsem)

        body(*ins, *outs, *sc)

        @pl.when(last)
        def _():
            exchange.finish(xin, xout, xsem)

    return pl.pallas_call(
        fused, name=name, grid=grid, out_shape=(*out_shape, *exchange.out_shape),
        in_specs=[*in_specs, *[_ANY] * k_in], out_specs=(*out_specs, *[_ANY] * k_out),
        scratch_shapes=[*scratch, *exchange.sems],
        compiler_params=_params(("arbitrary",) * len(grid), vmem_limit),
    )(*args, *exchange.ins)


def _run_exchange(exchange, name):
    def body(*refs):
        k_in, k_out = len(exchange.ins), len(exchange.out_shape)
        xin, xout, xsem = refs[:k_in], refs[k_in:k_in + k_out], refs[k_in + k_out:]
        exchange.start(xin, xout, xsem)
        exchange.finish(xin, xout, xsem)

    return pl.pallas_call(
        body, name=name, out_shape=tuple(exchange.out_shape), in_specs=[_ANY] * len(exchange.ins),
        out_specs=tuple([_ANY] * len(exchange.out_shape)), scratch_shapes=list(exchange.sems),
    )(*exchange.ins)


def _gather_shards(wsh):
    _, hh, _ = wsh.shape

    def first_hop(w_ref, out_ref, ssems, rsems):
        x, y, c, chips = _place()
        t = 2 * x + y
        return [_remote(w_ref.at[c], out_ref.at[t, c], ssems.at[j], rsems.at[j], (cx, cy, c))
                for j, (cx, cy) in enumerate(chips)]

    def start(xin, xout, xsem):
        for cp in first_hop(xin[0], xout[0], *xsem):
            cp.start()

    def finish(xin, xout, xsem):
        out_ref, (ssems, rsems) = xout[0], xsem
        first = first_hop(xin[0], out_ref, *xsem)
        x, y, c, chips = _place()
        passed = []
        for j, (cx, cy) in enumerate(chips):
            got = out_ref.at[2 * cx + cy, c]
            _remote(got, got, ssems.at[j], rsems.at[j], (cx, cy, c)).wait_recv()
            cp = _remote(got, got, ssems.at[3 + j], rsems.at[3 + j], (x, y, 1 - c))
            cp.start()
            passed.append(cp)
        for j, (cx, cy) in enumerate(chips):
            got = out_ref.at[2 * cx + cy, 1 - c]
            _remote(got, got, ssems.at[3 + j], rsems.at[3 + j], (x, y, 1 - c)).wait_recv()
        for cp in first + passed:
            cp.wait_send()

    return _Exchange([wsh], [jax.ShapeDtypeStruct((4, 2, hh, 128), wsh.dtype)],
                     [pltpu.SemaphoreType.DMA((6,)), pltpu.SemaphoreType.DMA((6,))], start, finish)


def _gathered(landed, own):
    t = 2 * lax.axis_index("x") + lax.axis_index("y")
    return lax.dynamic_update_slice(landed, own[None], (t, 0, 0, 0))


def _swap_sibling(v):
    def copy(xin, xout, xsem):
        x, y, c, _ = _place()
        return _remote(xin[0], xout[0], xsem[0], xsem[1], (x, y, 1 - c))

    return _Exchange([v], [jax.ShapeDtypeStruct(v.shape, v.dtype)],
                     [pltpu.SemaphoreType.DMA, pltpu.SemaphoreType.DMA],
                     lambda *a: copy(*a).start(), lambda *a: copy(*a).wait())


def _scatter_chips(parts):
    n = len(parts)

    def copies(xin, xout, ssems, rsems):
        x, y, c, chips = _place()
        t = 2 * x + y
        return [_remote(xin[k].at[2 * cx + cy], xout[k].at[t], ssems.at[n * j + k], rsems.at[n * j + k], (cx, cy, c))
                for j, (cx, cy) in enumerate(chips) for k in range(n)]

    def start(xin, xout, xsem):
        for cp in copies(xin, xout, *xsem):
            cp.start()

    def finish(xin, xout, xsem):
        ssems, rsems = xsem
        x, y, c, chips = _place()
        for j, (cx, cy) in enumerate(chips):
            for k in range(n):
                got = xout[k].at[2 * cx + cy]
                _remote(got, got, ssems.at[n * j + k], rsems.at[n * j + k], (cx, cy, c)).wait_recv()
        for cp in copies(xin, xout, *xsem):
            cp.wait_send()

    return _Exchange(parts, [jax.ShapeDtypeStruct(a.shape, a.dtype) for a in parts],
                     [pltpu.SemaphoreType.DMA((3 * n,)), pltpu.SemaphoreType.DMA((3 * n,))], start, finish)


def _share_reduced(piece, eighth):
    def copies(t_ref, mine_r, sib_ref, rall_ref, ssems, rsems, lsem):
        x, y, c, _ = _place()
        me = 4 * x + 2 * y + c
        loc = pltpu.make_async_copy(mine_r, rall_ref.at[me], lsem)
        sends = [_remote(t_ref, sib_ref, ssems.at[0], rsems.at[0], (x, y, 1 - c))]
        peers = []
        for mask in range(1, 8):
            px = 1 - x if mask & 4 else x
            py = 1 - y if mask & 2 else y
            pc = 1 - c if mask & 1 else c
            peers.append((mask, px, py, pc))
            sends.append(_remote(mine_r, rall_ref.at[me], ssems.at[mask], rsems.at[mask], (px, py, pc)))
        return loc, sends, peers

    def start(xin, xout, xsem):
        loc, sends, _ = copies(*xin, *xout, *xsem)
        for cp in [loc] + sends:
            cp.start()

    def finish(xin, xout, xsem):
        (sib_ref, rall_ref), (ssems, rsems, _) = xout, xsem
        loc, sends, peers = copies(*xin, *xout, *xsem)
        x, y, c, _ = _place()
        _remote(sib_ref, sib_ref, ssems.at[0], rsems.at[0], (x, y, 1 - c)).wait_recv()
        for mask, px, py, pc in peers:
            got = rall_ref.at[4 * px + 2 * py + pc]
            _remote(got, got, ssems.at[mask], rsems.at[mask], (px, py, pc)).wait_recv()
        for cp in sends:
            cp.wait_send()
        loc.wait()

    return _Exchange([piece, eighth],
                     [jax.ShapeDtypeStruct(piece.shape, F32), jax.ShapeDtypeStruct((8, *eighth.shape), F32)],
                     [pltpu.SemaphoreType.DMA((8,)), pltpu.SemaphoreType.DMA((8,)), pltpu.SemaphoreType.DMA],
                     start, finish)


def _ceil_to(n, m):
    return -(-n // m) * m


def _pack_bf16(parts):
    blocks = [p.reshape(-1, 128) for p in parts]
    rows = jnp.concatenate([jnp.pad(b, ((0, -b.shape[0] % 16), (0, 0))) for b in blocks])
    hw = _ceil_to(rows.shape[0], 32) // 2
    return jnp.pad(rows, ((0, 2 * hw - rows.shape[0]), (0, 0))).reshape(2, hw, 128)


def _segments(wall, parts):
    wall = wall.reshape(4, -1, 128)
    out, row = [], 0
    for p in parts:
        n = p.size // 128
        out.append(wall[:, row:row + n].reshape(4, *p.shape))
        row += _ceil_to(n, 16)
    return out


class _GradReduce:
    def __init__(self, sharded, replicated, c_idx, wire_bf16=False):
        self.c_idx, self.wire_bf16 = c_idx, wire_bf16
        self.rowwise = [(n, g.shape[1:]) for n, g in sharded if math.prod(g.shape[1:]) % 128 == 0]
        self.small = [(n, g.shape[1:]) for n, g in sharded if math.prod(g.shape[1:]) % 128 != 0]
        self.replicated = [(n, g.shape[0]) for n, g in replicated]
        by_name = dict(sharded)
        blocks = [by_name[n].reshape(4, -1, 128) for n, _ in self.rowwise]
        if self.small:
            rest = jnp.concatenate([by_name[n].reshape(4, -1) for n, _ in self.small], axis=1)
            blocks.append(jnp.pad(rest, ((0, 0), (0, -rest.shape[1] % 128))).reshape(4, -1, 128))
        blocks = [jnp.pad(b, ((0, 0), (0, -b.shape[1] % 8), (0, 0))) for b in blocks]
        rows = sum(b.shape[1] for b in blocks)
        self.hs = _ceil_to(rows, 256) // 2
        sh = jnp.concatenate(blocks + [jnp.zeros((4, 2 * self.hs - rows, 128), F32)], axis=1)
        rp = jnp.concatenate([g for _, g in replicated])
        self.rr = _ceil_to(_ceil_to(rp.shape[0], 128) // 128, 64) // 8
        rp = jnp.pad(rp, (0, 8 * self.rr * 128 - rp.shape[0])).reshape(4, 2, self.rr, 128)
        self.hh = self.hs + self.rr

        def half(c):
            return jnp.concatenate([lax.dynamic_slice_in_dim(sh, c * self.hs, self.hs, axis=1),
                                    lax.dynamic_index_in_dim(rp, c, axis=1, keepdims=False)],
                                   axis=1).reshape(4 * self.hh, 128)

        self.mine, self.other = half(c_idx), half(1 - c_idx)

    def swap(self):
        return _swap_sibling(self.other)

    def swapped(self, got):
        self.chip_sum = _add2(self.mine, got).reshape(4, self.hh, 128)

    def scatter(self):
        self.own_r = self.chip_sum[:, self.hs:]
        if self.wire_bf16:
            return _scatter_chips([self.chip_sum[:, :self.hs].astype(BF16), self.own_r])
        return _scatter_chips([self.chip_sum])

    def scattered(self, landed, landed_r=None):
        if landed_r is None:
            landed_r = landed[:, self.hs:]
        self.piece = _sum_slots(landed, self.chip_sum, self.hs)
        self.eighth = _sum_slots(landed_r, self.own_r, self.rr)

    def share(self):
        return _share_reduced(self.piece, self.eighth)

    def shared(self, sibling, rall):
        mine, sib = self.piece, sibling
        self.shard = jnp.where(self.c_idx == 0, jnp.concatenate([mine, sib]), jnp.concatenate([sib, mine]))
        self.rall = rall

    def reduced(self):
        out, row = {}, 0
        for name, shape in self.rowwise:
            rows = math.prod(shape) // 128
            out[name] = self.shard[row:row + rows].reshape(shape)
            row += _ceil_to(rows, 8)
        for group, flat in ((self.small, self.shard[row:].reshape(-1)), (self.replicated, self.rall.reshape(-1))):
            off = 0
            for name, shape in group:
                n = math.prod(shape) if isinstance(shape, tuple) else shape
                out[name] = flat[off:off + n]
                off += n
        return out


def _col_shards(w2d):
    rows, cols = w2d.shape
    return w2d.reshape(rows, 4, cols // 4).transpose(1, 0, 2)


_WIN0_PARTS = ((0, 384, 1280), (384, 640, 1664), (640, 672, 1984), (672, 1696, 0), (1696, 1952, 1024))


def _win0_aligned(shards):
    def cols(a, b):
        return [shards[s][:, max(a, 488 * s) - 488 * s:min(b, 488 * (s + 1)) - 488 * s]
                for s in range(4) if max(a, 488 * s) < min(b, 488 * (s + 1))]

    zeros = jnp.zeros((1024, 64), shards.dtype)
    return jnp.concatenate(cols(672, 1696) + cols(1696, 1952) + cols(0, 384) + cols(384, 640)
                           + [zeros] + cols(640, 672) + [zeros[:, :32]], axis=1)


def _win0_shards(dwin0p):
    shards = []
    for s in range(4):
        lo, hi = 488 * s, 488 * (s + 1)
        cols = [dwin0p[:, p + max(lo, a) - a:p + min(hi, b) - a] for a, b, p in _WIN0_PARTS if max(lo, a) < min(hi, b)]
        shards.append(jnp.concatenate(cols, axis=1))
    return jnp.stack(shards)


def _block_diag4(w):
    eye = jnp.eye(4, dtype=w.dtype)
    return jnp.einsum("gaij,ab->gaibj", w.reshape(3, 4, 64, 64), eye).reshape(3, 256, 256)


def _diag_blocks4(w):
    w5 = w.reshape(3, 4, 64, 4, 64)
    return jnp.stack([w5[:, a, :, a, :] for a in range(4)], axis=1).reshape(12, 64, 64)


def kernel(x, mem, positions, mla_w_in, mla_q_norm, mla_w_uq, mla_kv_norm, mla_w_ukv, lru_w_in, lru_conv_w, lru_conv_b, lru_w_rgate, lru_b_rgate, lru_w_igate, lru_b_igate, lru_lambda, w_mem_kv, w_out, ln_g, ln_b, loss_target, m_mla_w_in, m_mla_q_norm, m_mla_w_uq, m_mla_kv_norm, m_mla_w_ukv, m_lru_w_in, m_lru_conv_w, m_lru_conv_b, m_lru_w_rgate, m_lru_b_rgate, m_lru_w_igate, m_lru_b_igate, m_lru_lambda, m_w_mem_kv, m_w_out, m_ln_g, m_ln_b, v_mla_w_in, v_mla_q_norm, v_mla_w_uq, v_mla_kv_norm, v_mla_w_ukv, v_lru_w_in, v_lru_conv_w, v_lru_conv_b, v_lru_w_rgate, v_lru_b_rgate, v_lru_w_igate, v_lru_b_igate, v_lru_lambda, v_w_mem_kv, v_w_out, v_ln_g, v_ln_b):
    s = x.shape[1]
    c_idx = lax.axis_index("c")
    x2, mem2, tgt2 = x[0], mem[0], loss_target[0]

    first = [p.astype(BF16) for p in (mla_w_in[0], mla_w_uq[0], mla_w_ukv[0])]
    buf = _pack_bf16(first)
    mla_shards = _segments(_gathered(_run_exchange(_gather_shards(buf), "gather_weights")[0], buf), first)

    mid = [w_mem_kv.astype(BF16), w_out[0].astype(BF16)]
    buf_mid = _pack_bf16(mid)

    def mid_weights(landed):
        wmem, wout0 = _segments(_gathered(landed[0], buf_mid), mid)
        return wmem.transpose(1, 0, 2, 3).reshape(2, 1024, 512), wout0.reshape(1024, 1024)

    small = jnp.concatenate([lru_conv_w[0].reshape(-1), lru_conv_b[0], lru_b_rgate[0], lru_b_igate[0], lru_lambda[0]])
    late = [lru_w_in[0].astype(BF16), w_out[1].astype(BF16), lax.bitcast_convert_type(small, BF16)]
    buf_late = _pack_bf16(late)

    def late_weights(landed):
        win1, wout1, small_bits = _segments(_gathered(landed[0], buf_late), late)
        small_all = lax.bitcast_convert_type(small_bits, F32)
        cw = small_all[:, :768].reshape(4, 4, 192).transpose(1, 0, 2).reshape(4, TOK_W)
        cb, br, bi, lam = (small_all[:, 768 + 192 * k:960 + 192 * k].reshape(1, TOK_W) for k in range(4))
        return win1, wout1.reshape(1024, 1024), cw, cb, br, bi, lam

    def reduce_late(g):
        return _GradReduce(
            [("lru_w_in", g["lru_w_in"]), ("lru_conv_w", _col_shards(g["lru_conv_w"])),
             ("lru_conv_b", _col_shards(g["lru_conv_b"])), ("lru_b_rgate", _col_shards(g["lru_b_rgate"])),
             ("lru_b_igate", _col_shards(g["lru_b_igate"])), ("lru_lambda", _col_shards(g["lru_lambda"])),
             ("w_mem_kv1", g["w_mem_kv1"].reshape(4, 256, 512)), ("w_out1", g["w_out1"].reshape(4, 256, 1024))],
            [("lru_w_rgate", g["lru_w_rgate"].reshape(-1)), ("lru_w_igate", g["lru_w_igate"].reshape(-1)),
             ("ln_g1", g["ln_g1"].reshape(-1)), ("ln_b1", g["ln_b1"].reshape(-1))], c_idx)

    g0, late_red = _local_step(
        x2, mem2, positions.reshape(s, 1), tgt2, *mla_shards, mla_q_norm, mla_kv_norm, lru_w_rgate[0], lru_w_igate[0],
        ln_g, ln_b, mid_weights, late_weights, _gather_shards(buf_mid), _gather_shards(buf_late), reduce_late)

    early_red = _GradReduce(
        [("mla_w_in", g0["mla_w_in"]), ("mla_w_uq", _col_shards(g0["mla_w_uq"])),
         ("mla_w_ukv", _col_shards(g0["mla_w_ukv"])), ("w_mem_kv0", g0["w_mem_kv0"].reshape(4, 256, 512)),
         ("w_out0", g0["w_out0"].reshape(4, 256, 1024))],
        [("mla_q_norm", g0["mla_q_norm"].reshape(-1)), ("mla_kv_norm", g0["mla_kv_norm"].reshape(-1)),
         ("ln_g0", g0["ln_g0"].reshape(-1)), ("ln_b0", g0["ln_b0"].reshape(-1)), ("loss", g0["loss"].reshape(-1))],
        c_idx, wire_bf16=True)
    early_red.swapped(*_run_exchange(early_red.swap(), "swap_sibling"))
    early_red.scattered(*_run_exchange(early_red.scatter(), "scatter_chips"))
    early_red.shared(*_run_exchange(early_red.share(), "share_reduced"))
    red = {**late_red.reduced(), **early_red.reduced()}
    red["w_mem_kv"] = jnp.concatenate([red["w_mem_kv0"], red["w_mem_kv1"]])
    red["w_out"] = jnp.concatenate([red["w_out0"], red["w_out1"]])
    red["ln_g"] = jnp.concatenate([red["ln_g0"], red["ln_g1"]])
    red["ln_b"] = jnp.concatenate([red["ln_b0"], red["ln_b1"]])

    weights = dict(mla_w_in=mla_w_in, mla_q_norm=mla_q_norm, mla_w_uq=mla_w_uq, mla_kv_norm=mla_kv_norm,
                   mla_w_ukv=mla_w_ukv, lru_w_in=lru_w_in, lru_conv_w=lru_conv_w, lru_conv_b=lru_conv_b,
                   lru_w_rgate=lru_w_rgate, lru_b_rgate=lru_b_rgate, lru_w_igate=lru_w_igate, lru_b_igate=lru_b_igate,
                   lru_lambda=lru_lambda, w_mem_kv=w_mem_kv, w_out=w_out, ln_g=ln_g, ln_b=ln_b)
    m_in = dict(mla_w_in=m_mla_w_in, mla_q_norm=m_mla_q_norm, mla_w_uq=m_mla_w_uq, mla_kv_norm=m_mla_kv_norm,
                mla_w_ukv=m_mla_w_ukv, lru_w_in=m_lru_w_in, lru_conv_w=m_lru_conv_w, lru_conv_b=m_lru_conv_b,
                lru_w_rgate=m_lru_w_rgate, lru_b_rgate=m_lru_b_rgate, lru_w_igate=m_lru_w_igate,
                lru_b_igate=m_lru_b_igate, lru_lambda=m_lru_lambda, w_mem_kv=m_w_mem_kv, w_out=m_w_out, ln_g=m_ln_g,
                ln_b=m_ln_b)
    v_in = dict(mla_w_in=v_mla_w_in, mla_q_norm=v_mla_q_norm, mla_w_uq=v_mla_w_uq, mla_kv_norm=v_mla_kv_norm,
                mla_w_ukv=v_mla_w_ukv, lru_w_in=v_lru_w_in, lru_conv_w=v_lru_conv_w, lru_conv_b=v_lru_conv_b,
                lru_w_rgate=v_lru_w_rgate, lru_b_rgate=v_lru_b_rgate, lru_w_igate=v_lru_w_igate,
                lru_b_igate=v_lru_b_igate, lru_lambda=v_lru_lambda, w_mem_kv=v_w_mem_kv, w_out=v_w_out, ln_g=v_ln_g,
                ln_b=v_ln_b)
    order = ["mla_w_in", "mla_q_norm", "mla_w_uq", "mla_kv_norm", "mla_w_ukv", "lru_w_in", "lru_conv_w", "lru_conv_b",
             "lru_w_rgate", "lru_b_rgate", "lru_w_igate", "lru_b_igate", "lru_lambda", "w_mem_kv", "w_out", "ln_g",
             "ln_b"]
    grads, deltas, new_m, new_v = {}, {}, {}, {}

    def operands(name):
        shape = weights[name].shape
        two_d = (math.prod(shape[:-1]), shape[-1])
        return [a.reshape(two_d) for a in (weights[name], red[name], m_in[name], v_in[name])]

    def keep(name, g2, d2, m2, v2):
        shape = weights[name].shape
        grads[name], deltas[name] = g2.reshape(shape), d2.reshape(shape)
        new_m[name], new_v[name] = m2.reshape(shape), v2.reshape(shape)

    small = [n for n in order if weights[n].size <= 4096]
    ops = [operands(n) for n in small]
    for name, op, res in zip(small, ops, _adamw_small(ops)):
        keep(name, op[1], *res)
    for name in order:
        if name not in small:
            op = operands(name)
            keep(name, op[1], *_adamw(name, *op))
    return (red["loss"][0], g0["x"][None], *[grads[n] for n in order], *[deltas[n] for n in order],
            *[new_m[n] for n in order], *[new_v[n] for n in order])


def _local_step(x2, mem2, pos_col, tgt2, win0_sh, wuq_sh, wukv_sh, gq, gkv, w_rgate, w_igate, ln_g, ln_b,
                mid_weights, late_weights, gather_mid=None, gather_late=None, reduce_late=None):
    s = x2.shape[0]
    win0p = _win0_aligned(win0_sh)
    wuq_p = jnp.pad(wuq_sh.reshape(4, Q_LORA, 3, 96).transpose(1, 0, 2, 3).reshape(Q_LORA, 12, 96),
                    ((0, 0), (0, 0), (0, 32))).reshape(Q_LORA, QK_W)
    wukv3 = wukv_sh.reshape(4, KV_LORA, 3, 128).transpose(1, 0, 2, 3).reshape(KV_LORA, 12, 128)
    wk_p = jnp.pad(wukv3[:, :, :64], ((0, 0), (0, 0), (0, 64))).reshape(KV_LORA, QK_W)
    wv = wukv3[:, :, 64:].reshape(KV_LORA, TOK_W)
    wr_bd = _block_diag4(w_rgate).astype(BF16)
    wi_bd = _block_diag4(w_igate).astype(BF16)
    half = 16
    inv_freq = ROPE_THETA ** (-jnp.arange(half, dtype=F32) / half)
    inv_lane = jnp.concatenate([jnp.zeros((64,), F32), inv_freq, inv_freq, jnp.zeros((32,), F32)]).reshape(1, HEAD_PAD)

    gate0, qm0, cq, ckv, q_p, q_t, k_p, v_b, v_t, ctab, satab, sbtab, *landed = _mla_proj_fwd(
        x2, win0p, gq, gkv, wuq_p, wk_p, wv, pos_col, inv_lane, exchange=gather_mid)
    wmem, wout0 = mid_weights(landed)
    memkv = _mem_kv(mem2, wmem)
    tok0, lse, *landed = _attn_fwd(q_p, k_p, v_t, exchange=gather_late)
    win1, wout1, cw, cb, br, bi, lam = late_weights(landed)
    g0, b0, g1, b1 = ln_g[0:1], ln_b[0:1], ln_g[1:2], ln_b[1:2]
    h1 = _mix_fwd(tok0, gate0, qm0, memkv[0], wout0, x2, g0, b0)
    u1, gate1, qm1, hs1 = _lru_fwd(h1, win1, cw, cb, wr_bd, br, wi_bd, bi, lam)

    dres1, dtok1, dgate1, dqm1, dwout1, dmemkv1, dg1, db1, loss = _mix_bwd(
        hs1, gate1, qm1, memkv[1], wout1, h1, g1, b1, tgt2, True)
    dh1, dwin1, dcw, dcb, dwr_bd, dbr, dwi_bd, dbi, dlam = _lru_bwd(
        dtok1, dgate1, dqm1, dres1, h1, u1, hs1, win1, cw, cb, wr_bd, br, wi_bd, bi, lam)
    late = {"lru_w_in": dwin1, "lru_conv_w": dcw, "lru_conv_b": dcb, "lru_b_rgate": dbr, "lru_b_igate": dbi,
            "lru_lambda": dlam, "w_mem_kv1": _mem_kv_bwd(mem2, dmemkv1), "w_out1": dwout1,
            "lru_w_rgate": _diag_blocks4(dwr_bd), "lru_w_igate": _diag_blocks4(dwi_bd), "ln_g1": dg1, "ln_b1": db1}
    red = reduce_late(late) if reduce_late is not None else None

    dres0, dob, dobt, stats, dgate0, dqm0, dwout0, dmemkv0, dg0, db0, _, *got = _mix_bwd(
        tok0, gate0, qm0, memkv[0], wout0, x2, g0, b0, dh1, False, lse=lse, exchange=red.swap() if red else None)
    if red:
        red.swapped(*got)
    dq_p, dk_t, dv_t, *got = _attn_bwd(q_p, q_t, k_p, v_b, dob, dobt, stats,
                                       exchange=red.scatter() if red else None)
    if red:
        red.scattered(*got)
    if red:
        red.shared(*_run_exchange(red.share(), "share_reduced"))
    dx, dwin0p, dwuq_p, dwk_p, dwv, dgq, dgkv = _mla_proj_bwd(
        x2, cq, ckv, dq_p, dk_t, dv_t, dgate0, dqm0, dres0, win0p, gq, gkv, wuq_p, wk_p, wv,
        ctab, satab, sbtab)

    dwin0 = _win0_shards(dwin0p)
    dwuq = dwuq_p.reshape(Q_LORA, 12, 128)[:, :, :96].reshape(Q_LORA, 1152)
    dwukv = jnp.concatenate([dwk_p.reshape(KV_LORA, 12, 128)[:, :, :64], dwv.reshape(KV_LORA, 12, 64)],
                            axis=2).reshape(KV_LORA, 1536)
    early = {"x": dx, "loss": loss, "mla_w_in": dwin0, "mla_w_uq": dwuq, "mla_w_ukv": dwukv,
             "w_mem_kv0": _mem_kv_bwd(mem2, dmemkv0), "w_out0": dwout0, "mla_q_norm": dgq, "mla_kv_norm": dgkv,
             "ln_g0": dg0, "ln_b0": db0}
    return early, (red if red else late)
```

```python
import functools
import math

import jax
import jax.numpy as jnp
from jax import lax
from jax.experimental import pallas as pl
from jax.experimental.pallas import tpu as pltpu

F32, BF16 = jnp.float32, jnp.bfloat16
MESH = pl.DeviceIdType.MESH

D_MODEL = 1024
N_TOK_HEADS = 12
TOK_W = 768
MEM_W = 256
MEM_LEN = 256
Q_LORA, KV_LORA = 384, 256
HEAD_PAD = 128
QK_W = N_TOK_HEADS * HEAD_PAD
ATT_SCALE = 1.0 / math.sqrt(96.0)
ATT_SCALE_LOG2 = ATT_SCALE * math.log2(math.e)
ROPE_THETA = 10000.0
LRU_C = 8.0
ALPHA = 4.0 ** 0.25
NORM_EPS = 1e-6
ADAM_LR, ADAM_B1, ADAM_B2, ADAM_EPS, ADAM_WD, ADAM_STEP = 0.001, 0.9, 0.999, 1e-08, 0.01, 10

TB_PROJ = 512
TB_PROJ_BWD = 512
TB_MIX = 512
TB_LRU = 256
TQ_ATT = 512
TQ_ATT_FWD = 1024
TK_ATT = 1024
VMEM_LIMIT = 56 * 1024 * 1024
VMEM_LIMIT_PROJ_BWD = 60 * 1024 * 1024


def _mm(a, b):
    return jnp.dot(a.astype(BF16), b.astype(BF16), preferred_element_type=F32)


def _mm_nt(a, b):
    return lax.dot_general(a.astype(BF16), b.astype(BF16), (((1,), (1,)), ((), ())), preferred_element_type=F32)


def _mm_tn(a, b):
    return lax.dot_general(a.astype(BF16), b.astype(BF16), (((0,), (0,)), ((), ())), preferred_element_type=F32)


def _rows(tb, w):
    return pl.BlockSpec((tb, w), lambda i: (i, 0))


def _const(shape):
    n = len(shape)
    return pl.BlockSpec(shape, lambda i: (0,) * n)


def _params(sem=("arbitrary",), vmem_limit=None):
    return pltpu.CompilerParams(dimension_semantics=sem, vmem_limit_bytes=vmem_limit or VMEM_LIMIT)


def _iota(shape, dim):
    return lax.broadcasted_iota(jnp.int32, shape, dim)


def _rope_tables(pos, inv_lane):
    ang = pos.astype(F32) * inv_lane
    lane = _iota(ang.shape, 1)
    cs, sn = jnp.cos(ang), jnp.sin(ang)
    return (jnp.where(lane < 64, 1.0, jnp.where(lane < 96, cs, 0.0)),
            jnp.where((lane >= 64) & (lane < 80), -sn, 0.0), jnp.where((lane >= 80) & (lane < 96), sn, 0.0))


def _rope(t, c, sa, sb):
    return t * c + pltpu.roll(t, 112, 1) * sa + pltpu.roll(t, 16, 1) * sb


def _rope_t(d, c, sa, sb):
    return d * c + pltpu.roll(d * sa, 16, 1) + pltpu.roll(d * sb, 112, 1)


def _rms(c, g):
    r = lax.rsqrt(jnp.mean(c * c, axis=-1, keepdims=True) + NORM_EPS)
    xh = c * r
    return xh * g, xh, r


def _mla_proj_fwd(x, win, gq, gkv, wuq, wukv_k, wukv_v, pos_col, inv_lane, exchange=None):
    s = x.shape[0]
    tb = min(TB_PROJ, s)

    def body(x_ref, win_ref, gq_ref, gkv_ref, wuq_ref, wk_ref, wv_ref, pos_ref, inv_ref,
             gate_ref, qm_ref, cq_ref, ckv_ref, q_ref, qt_ref, k_ref, v_ref, vt_ref, c_ref, sa_ref, sb_ref):
        z = _mm(x_ref[...], win_ref[...])
        gate_ref[...] = z[:, 0:1024]
        qm_ref[...] = z[:, 1024:1280].astype(BF16)
        cq = z[:, 1280:1664]
        ckv = z[:, 1664:1920]
        cq_ref[...] = cq
        ckv_ref[...] = ckv
        c, sa, sb = _rope_tables(pos_ref[...], inv_ref[...])
        c_ref[...], sa_ref[...], sb_ref[...] = c, sa, sb
        nq, _, _ = _rms(cq, gq_ref[...])
        nkv, _, _ = _rms(ckv, gkv_ref[...])
        qf = _mm(nq, wuq_ref[...])
        kf = _mm(nkv, wk_ref[...])
        vf = _mm(nkv, wv_ref[...])
        v_ref[...] = vf.astype(BF16)
        for j in range(N_TOK_HEADS // 2):
            sl = slice(HEAD_PAD * j, HEAD_PAD * (j + 1))
            vt_ref[sl, :] = vf[:, sl].T.astype(BF16)
        kr = _rope(z[:, 1920:2048], c, sa, sb)
        for h in range(N_TOK_HEADS):
            sl = slice(HEAD_PAD * h, HEAD_PAD * (h + 1))
            qh = _rope(qf[:, sl], c, sa, sb) * ATT_SCALE_LOG2
            q_ref[:, sl] = qh.astype(BF16)
            qt_ref[sl, :] = qh.T.astype(BF16)
            k_ref[:, sl] = (kf[:, sl] + kr).astype(BF16)

    outs = (jax.ShapeDtypeStruct((s, 1024), F32), jax.ShapeDtypeStruct((s, MEM_W), BF16),
            jax.ShapeDtypeStruct((s, Q_LORA), F32), jax.ShapeDtypeStruct((s, KV_LORA), F32),
            jax.ShapeDtypeStruct((s, QK_W), BF16), jax.ShapeDtypeStruct((QK_W, s), BF16),
            jax.ShapeDtypeStruct((s, QK_W), BF16),
            jax.ShapeDtypeStruct((s, TOK_W), BF16), jax.ShapeDtypeStruct((TOK_W, s), BF16),
            *[jax.ShapeDtypeStruct((s, HEAD_PAD), F32)] * 3)

    def cols(w):
        return pl.BlockSpec((w, tb), lambda i: (0, i))

    return _run(
        body, name="mla_proj_fwd", grid=(s // tb,), out_shape=outs,
        in_specs=[_rows(tb, 1024), _const((1024, 2048)), _const((1, Q_LORA)), _const((1, KV_LORA)),
                  _const((Q_LORA, QK_W)), _const((KV_LORA, QK_W)), _const((KV_LORA, TOK_W)),
                  _rows(tb, 1), _const((1, HEAD_PAD))],
        out_specs=(_rows(tb, 1024), _rows(tb, MEM_W), _rows(tb, Q_LORA), _rows(tb, KV_LORA),
                   _rows(tb, QK_W), cols(QK_W), _rows(tb, QK_W), _rows(tb, TOK_W), cols(TOK_W),
                   _rows(tb, HEAD_PAD), _rows(tb, HEAD_PAD), _rows(tb, HEAD_PAD)),
        args=(x, win, gq, gkv, wuq, wukv_k, wukv_v, pos_col, inv_lane), sem=("parallel",), exchange=exchange)


def _mla_proj_bwd(x, cq, ckv, dq, dkt, dvt, dgate, dqm, dres, win, gq, gkv, wuq, wukv_k, wukv_v, ctab, satab, sbtab):
    s = x.shape[0]
    tb = min(TB_PROJ_BWD, s)

    def body(x_ref, cq_ref, ckv_ref, dq_ref, dkt_ref, dvt_ref, dgate_ref, dqm_ref, dres_ref, win_ref, gq_ref, gkv_ref,
             wuq_ref, wk_ref, wv_ref, c_ref, sa_ref, sb_ref,
             dx_ref, dwin_ref, dwuq_ref, dwk_ref, dwv_ref, dgq_ref, dgkv_ref):
        @pl.when(pl.program_id(0) == 0)
        def _():
            for r in (dwin_ref, dwuq_ref, dwk_ref, dwv_ref, dgq_ref, dgkv_ref):
                r[...] = jnp.zeros_like(r)

        c, sa, sb = c_ref[...], sa_ref[...], sb_ref[...]
        lane = _iota((tb, HEAD_PAD), 1)
        gq, gkv = gq_ref[...], gkv_ref[...]
        nq, xhq, rq = _rms(cq_ref[...], gq)
        nkv, xhk, rk = _rms(ckv_ref[...], gkv)
        dkp = dkt_ref[...].T * math.log(2.0)
        dqs, dkr = [], jnp.zeros((tb, HEAD_PAD), F32)
        for h in range(N_TOK_HEADS):
            sl = slice(HEAD_PAD * h, HEAD_PAD * (h + 1))
            dqs.append(_rope_t(dq_ref[:, sl], c, sa, sb).astype(BF16))
            dkr = dkr + dkp[:, sl]
        dqf = jnp.concatenate(dqs, axis=1)
        dkr = jnp.where((lane >= 64) & (lane < 96), _rope_t(dkr, c, sa, sb), 0.0)
        dvb = dvt_ref[...].T.astype(BF16)
        dkb = dkp.astype(BF16)
        dnq = _mm_nt(dqf, wuq_ref[...])
        dwuq_ref[...] += _mm_tn(nq, dqf)
        dgq_ref[...] += jnp.sum(dnq * xhq, axis=0, keepdims=True)
        dxh = dnq * gq
        dcq = rq * (dxh - xhq * jnp.mean(dxh * xhq, axis=-1, keepdims=True))
        dnkv = _mm_nt(dkb, wk_ref[...]) + _mm_nt(dvb, wv_ref[...])
        nkvb = nkv.astype(BF16)
        dwk_ref[...] += _mm_tn(nkvb, dkb)
        dwv_ref[...] += _mm_tn(nkvb, dvb)
        dgkv_ref[...] += jnp.sum(dnkv * xhk, axis=0, keepdims=True)
        dxh = dnkv * gkv
        dckv = rk * (dxh - xhk * jnp.mean(dxh * xhk, axis=-1, keepdims=True))
        dz = jnp.concatenate([dgate_ref[...], dqm_ref[...], dcq, dckv, dkr], axis=1).astype(BF16)
        dx_ref[...] = _mm_nt(dz, win_ref[...]) + dres_ref[...]
        dwin_ref[...] += _mm_tn(x_ref[...], dz)

    outs = (jax.ShapeDtypeStruct((s, 1024), F32), jax.ShapeDtypeStruct((1024, 2048), F32),
            jax.ShapeDtypeStruct((Q_LORA, QK_W), F32), jax.ShapeDtypeStruct((KV_LORA, QK_W), F32),
            jax.ShapeDtypeStruct((KV_LORA, TOK_W), F32), jax.ShapeDtypeStruct((1, Q_LORA), F32),
            jax.ShapeDtypeStruct((1, KV_LORA), F32))
    return _run(
        body, name="mla_proj_bwd", grid=(s // tb,), out_shape=outs,
        in_specs=[_rows(tb, 1024), _rows(tb, Q_LORA), _rows(tb, KV_LORA), _rows(tb, QK_W),
                  pl.BlockSpec((QK_W, tb), lambda i: (0, i)), pl.BlockSpec((TOK_W, tb), lambda i: (0, i)),
                  _rows(tb, 1024), _rows(tb, MEM_W), _rows(tb, 1024),
                  _const((1024, 2048)), _const((1, Q_LORA)), _const((1, KV_LORA)),
                  _const((Q_LORA, QK_W)), _const((KV_LORA, QK_W)), _const((KV_LORA, TOK_W)),
                  _rows(tb, HEAD_PAD), _rows(tb, HEAD_PAD), _rows(tb, HEAD_PAD)],
        out_specs=(_rows(tb, 1024), _const((1024, 2048)), _const((Q_LORA, QK_W)), _const((KV_LORA, QK_W)),
                   _const((KV_LORA, TOK_W)), _const((1, Q_LORA)), _const((1, KV_LORA))),
        args=(x, cq, ckv, dq, dkt, dvt, dgate, dqm, dres, win, gq, gkv, wuq, wukv_k, wukv_v, ctab, satab, sbtab),
        sem=("arbitrary",), vmem_limit=VMEM_LIMIT_PROJ_BWD)


def _attn_fwd(q, k, vt, exchange=None):
    s = q.shape[0]
    tq = min(TQ_ATT_FWD, s)
    tk = min(TK_ATT, s)

    def body(q_ref, k_ref, vt_ref, o_ref, lse_ref):
        i = pl.program_id(1)
        nfull = (i * tq) // tk
        krow = _iota((tk, tq), 0)
        qpos = i * tq + _iota((tk, tq), 1)

        def head_tile(hh, st, carry, masked):
            hs = slice(HEAD_PAD * hh, HEAD_PAD * (hh + 1))
            m, l, acc = carry
            sc = _mm_nt(k_ref[pl.ds(st, tk), hs], q_ref[:, hs])
            if masked:
                sc = jnp.where(st + krow <= qpos, sc, -jnp.inf)
            m_new = jnp.maximum(m, jnp.max(sc, axis=0, keepdims=True))
            p = jnp.exp2(sc - m_new)
            a = jnp.exp2(m - m_new)
            l = a * l + jnp.sum(p, axis=0, keepdims=True)
            acc = a * acc + _mm(vt_ref[64 * hh:64 * (hh + 1), pl.ds(st, tk)], p)
            return m_new, l, acc

        def tile(j, carry, masked):
            st = pl.multiple_of(j * tk, tk)
            return tuple(head_tile(hh, st, carry[hh], masked) for hh in range(2))

        def init():
            return (jnp.full((1, tq), -jnp.inf, F32), jnp.zeros((1, tq), F32), jnp.zeros((64, tq), F32))

        carry = lax.fori_loop(0, nfull, functools.partial(tile, masked=False), (init(), init()))
        (ma, la, acca), (mb, lb, accb) = tile(nfull, carry, True)
        o_ref[...] = jnp.concatenate([acca / la, accb / lb], axis=0).T
        lse_ref[...] = jnp.concatenate([jnp.broadcast_to(ma + jnp.log2(la), (64, tq)),
                                        jnp.broadcast_to(mb + jnp.log2(lb), (64, tq))], axis=0).T

    shp = jax.ShapeDtypeStruct((s, TOK_W), F32)
    return _run(
        body, name="attn_fwd", grid=(N_TOK_HEADS // 2, s // tq), out_shape=(shp, shp),
        in_specs=[pl.BlockSpec((tq, 2 * HEAD_PAD), lambda j, i: (i, j)),
                  pl.BlockSpec((s, 2 * HEAD_PAD), lambda j, i: (0, j)),
                  pl.BlockSpec((HEAD_PAD, s), lambda j, i: (j, 0))],
        out_specs=(pl.BlockSpec((tq, HEAD_PAD), lambda j, i: (i, j)),) * 2,
        args=(q, k, vt), sem=("parallel", "arbitrary"), exchange=exchange)


def _attn_stats(o, do, lse_ref, dob_ref, dot_ref, st_ref):
    lane = _iota((o.shape[0], HEAD_PAD), 1)
    dob_ref[...] = do.astype(BF16)
    prod = do * o
    for j in range(N_TOK_HEADS // 2):
        sl = slice(HEAD_PAD * j, HEAD_PAD * (j + 1))
        dot_ref[sl, :] = do[:, sl].T.astype(BF16)
        pj = prod[:, sl]
        da = jnp.sum(jnp.where(lane < 64, pj, 0.0), axis=-1, keepdims=True)
        db = jnp.sum(jnp.where(lane >= 64, pj, 0.0), axis=-1, keepdims=True)
        la = lse_ref[:, HEAD_PAD * j:HEAD_PAD * j + 1]
        lb = lse_ref[:, HEAD_PAD * j + 64:HEAD_PAD * j + 65]
        st_ref[j] = jnp.where(lane == 0, la, jnp.where(lane == 1, lb, jnp.where(lane == 2, da,
                                                                                 jnp.where(lane == 3, db, 0.0))))


def _attn_bwd(q, qt, k, v, dob, dobt, stats, exchange=None):
    s = q.shape[0]
    t = min(TQ_ATT, s)
    nq = s // t

    def body(q_ref, qt_ref, do_ref, dot_ref, st_ref, k_ref, v_ref, dq_ref, dkt_ref, dvt_ref):
        i = pl.program_id(1)

        @pl.when(i == 0)
        def _():
            dkt_ref[...] = jnp.zeros_like(dkt_ref)
            dvt_ref[...] = jnp.zeros_like(dvt_ref)

        lane = _iota((t, HEAD_PAD), 1)
        qpos, kcol = _iota((t, t), 0), _iota((t, t), 1)
        do = do_ref[...]
        stats = st_ref[0]

        def head_tile(hh, ks, dq_acc, masked):
            hs = slice(HEAD_PAD * hh, HEAD_PAD * (hh + 1))
            qh = q_ref[:, hs]
            kh = k_ref[pl.ds(ks, t), hs]
            dom = jnp.where((lane < 64) if hh == 0 else (lane >= 64), do, jnp.zeros_like(do))
            lse = stats[:, hh:hh + 1]
            dlt = stats[:, 2 + hh:3 + hh]
            sc = _mm_nt(qh, kh)
            if masked:
                sc = jnp.where(kcol <= qpos, sc, -jnp.inf)
            p = jnp.exp2(sc - lse)
            dp = _mm_nt(dom, v_ref[pl.ds(ks, t), :])
            ds = (p * (dp - dlt)).astype(BF16)
            dvt_ref[64 * hh:64 * (hh + 1), pl.ds(ks, t)] += _mm(dot_ref[64 * hh:64 * (hh + 1), :], p)
            dkt_ref[HEAD_PAD * hh:HEAD_PAD * hh + 96, pl.ds(ks, t)] += _mm(qt_ref[HEAD_PAD * hh:HEAD_PAD * hh + 96, :], ds)
            return dq_acc + _mm(ds, kh)

        def tile(j, carry, masked):
            ks = pl.multiple_of(j * t, t)
            return tuple(head_tile(hh, ks, carry[hh], masked) for hh in range(2))

        zero = jnp.zeros((t, HEAD_PAD), F32)
        carry = lax.fori_loop(0, i, functools.partial(tile, masked=False), (zero, zero))
        dqa, dqb = tile(i, carry, True)
        dq_ref[...] = jnp.concatenate([dqa, dqb], axis=1) * ATT_SCALE

    return _run(
        body, name="attn_bwd", grid=(N_TOK_HEADS // 2, nq),
        out_shape=(jax.ShapeDtypeStruct((s, QK_W), F32), jax.ShapeDtypeStruct((QK_W, s), F32),
                   jax.ShapeDtypeStruct((TOK_W, s), F32)),
        in_specs=[pl.BlockSpec((t, 2 * HEAD_PAD), lambda j, i: (i, j)),
                  pl.BlockSpec((2 * HEAD_PAD, t), lambda j, i: (j, i)),
                  pl.BlockSpec((t, HEAD_PAD), lambda j, i: (i, j)),
                  pl.BlockSpec((HEAD_PAD, t), lambda j, i: (j, i)),
                  pl.BlockSpec((1, t, HEAD_PAD), lambda j, i: (j, i, 0)),
                  pl.BlockSpec((s, 2 * HEAD_PAD), lambda j, i: (0, j)),
                  pl.BlockSpec((s, HEAD_PAD), lambda j, i: (0, j))],
        out_specs=(pl.BlockSpec((t, 2 * HEAD_PAD), lambda j, i: (i, j)),
                   pl.BlockSpec((2 * HEAD_PAD, s), lambda j, i: (j, 0)),
                   pl.BlockSpec((HEAD_PAD, s), lambda j, i: (j, 0))),
        args=(q, qt, dob, dobt, stats, k, v), sem=("parallel", "arbitrary"), exchange=exchange)


def _mem_kv(mem, wmem):
    def body(m_ref, w_ref, o_ref):
        o_ref[0] = _mm(m_ref[...], w_ref[0]).astype(BF16)

    return pl.pallas_call(
        body, name="mem_kv", grid=(2,), out_shape=jax.ShapeDtypeStruct((2, MEM_LEN, 512), BF16),
        in_specs=[_const((MEM_LEN, 1024)), pl.BlockSpec((1, 1024, 512), lambda l: (l, 0, 0))],
        out_specs=pl.BlockSpec((1, MEM_LEN, 512), lambda l: (l, 0, 0)),
        compiler_params=_params(("parallel",)),
    )(mem, wmem)


def _mem_kv_bwd(mem, dmemkv):
    def body(m_ref, d_ref, o_ref):
        o_ref[...] = _mm_tn(m_ref[...], d_ref[...])

    return pl.pallas_call(
        body, name="mem_kv_bwd", grid=(1,), out_shape=jax.ShapeDtypeStruct((1024, 512), F32),
        in_specs=[_const((MEM_LEN, 1024)), _const((MEM_LEN, 512))], out_specs=_const((1024, 512)),
        compiler_params=_params(("arbitrary",)),
    )(mem, dmemkv)


def _head_mask(lane, sub):
    return (lane < 64) if sub == 0 else (lane >= 64)


def _mem_attn(qm, kv):
    tb = qm.shape[0]
    lane = _iota((tb, HEAD_PAD), 1)
    outs, ps = [], []
    for pp in range(2):
        qp = qm[:, HEAD_PAD * pp:HEAD_PAD * (pp + 1)]
        kp = kv[:, HEAD_PAD * pp:HEAD_PAD * (pp + 1)]
        vp = kv[:, MEM_W + HEAD_PAD * pp:MEM_W + HEAD_PAD * (pp + 1)]
        pair = None
        for sub in range(2):
            qh = jnp.where(_head_mask(lane, sub), qp, jnp.zeros_like(qp))
            sc = _mm_nt(qh, kp) * 0.125
            e = jnp.exp(sc - jnp.max(sc, axis=-1, keepdims=True))
            p = e / jnp.sum(e, axis=-1, keepdims=True)
            o = _mm(p, vp)
            ps.append(p)
            pair = o if sub == 0 else jnp.where(lane < 64, pair, o)
        outs.append(pair)
    return jnp.concatenate(outs, axis=1), ps


def _mem_attn_bwd(dmo, qm, kv, ps):
    tb = qm.shape[0]
    lane = _iota((tb, HEAD_PAD), 1)
    dqs, dks, dvs = [], [], []
    for pp in range(2):
        qp = qm[:, HEAD_PAD * pp:HEAD_PAD * (pp + 1)]
        kp = kv[:, HEAD_PAD * pp:HEAD_PAD * (pp + 1)]
        vp = kv[:, MEM_W + HEAD_PAD * pp:MEM_W + HEAD_PAD * (pp + 1)]
        dop = dmo[:, HEAD_PAD * pp:HEAD_PAD * (pp + 1)]
        dq_pair, dk_pair, dv_pair = None, None, None
        for sub in range(2):
            msk = _head_mask(lane, sub)
            p = ps[2 * pp + sub]
            qh = jnp.where(msk, qp, jnp.zeros_like(qp))
            doh = jnp.where(msk, dop, 0.0).astype(BF16)
            dv = _mm_tn(p, doh)
            dp = _mm_nt(doh, vp)
            ds = (p * (dp - jnp.sum(dp * p, axis=-1, keepdims=True)) * 0.125).astype(BF16)
            dq = _mm(ds, kp)
            dk = _mm_tn(ds, qh)
            if sub == 0:
                dq_pair, dk_pair, dv_pair = dq, dk, dv
            else:
                dq_pair = jnp.where(lane < 64, dq_pair, dq)
                dk_pair, dv_pair = dk_pair + dk, dv_pair + dv
        dqs.append(dq_pair)
        dks.append(dk_pair)
        dvs.append(dv_pair)
    return jnp.concatenate(dqs, axis=1), jnp.concatenate(dks + dvs, axis=1)


def _mix_core(tok, gate, qm, kv, wout, h_in, g, b):
    mem_out, ps = _mem_attn(qm, kv)
    cat = jnp.concatenate([tok, mem_out], axis=1)
    sg = jax.nn.sigmoid(gate)
    sl = gate * sg
    y = cat * sl
    r = ALPHA * h_in + _mm(y, wout)
    mu = jnp.mean(r, axis=-1, keepdims=True)
    xc = r - mu
    rstd = lax.rsqrt(jnp.mean(xc * xc, axis=-1, keepdims=True) + NORM_EPS)
    xh = xc * rstd
    return xh * g + b, (ps, cat, sg, sl, y, xh, rstd)


def _mix_fwd(tok, gate, qm, kv, wout, h_in, g, b):
    s = tok.shape[0]
    tb = min(TB_MIX, s)

    def body(tok_ref, gate_ref, qm_ref, kv_ref, w_ref, h_ref, g_ref, b_ref, o_ref):
        o_ref[...], _ = _mix_core(tok_ref[...], gate_ref[...], qm_ref[...], kv_ref[...], w_ref[...], h_ref[...],
                                  g_ref[...], b_ref[...])

    return pl.pallas_call(
        body, name="mix_fwd", grid=(s // tb,), out_shape=jax.ShapeDtypeStruct((s, 1024), F32),
        in_specs=[_rows(tb, TOK_W), _rows(tb, 1024), _rows(tb, MEM_W), _const((MEM_LEN, 512)), _const((1024, 1024)),
                  _rows(tb, 1024), _const((1, 1024)), _const((1, 1024))],
        out_specs=_rows(tb, 1024), compiler_params=_params(("parallel",)),
    )(tok, gate, qm, kv, wout, h_in, g, b)


def _mix_bwd(tok, gate, qm, kv, wout, h_in, g, b, up, from_loss, lse=None, exchange=None):
    s = tok.shape[0]
    tb = min(TB_MIX, s)
    n_in = 9 if lse is None else 10
    n_tok_out = 1 if lse is None else 3

    def body(*refs):
        tok_ref, gate_ref, qm_ref, kv_ref, w_ref, h_ref, g_ref, b_ref, up_ref = refs[:9]
        dres_ref, tok_out = refs[n_in], refs[n_in + 1:n_in + 1 + n_tok_out]
        dgate_ref, dqm_ref, dw_ref, dkv_ref, dg_ref, db_ref, loss_ref = refs[n_in + 1 + n_tok_out:]

        @pl.when(pl.program_id(0) == 0)
        def _():
            for r in (dw_ref, dkv_ref, dg_ref, db_ref, loss_ref):
                r[...] = jnp.zeros_like(r)

        gate, qm, kv, wout, g = gate_ref[...], qm_ref[...], kv_ref[...], w_ref[...], g_ref[...]
        h_out, (ps, cat, sg, sl, y, xh, rstd) = _mix_core(tok_ref[...], gate, qm, kv, wout, h_ref[...], g, b_ref[...])
        if from_loss:
            diff = h_out - up_ref[...]
            loss_ref[...] += 0.5 * jnp.sum(jnp.mean(diff * diff, axis=-1, keepdims=True), axis=0, keepdims=True)
            dh = diff * (1.0 / D_MODEL)
        else:
            dh = up_ref[...]
        dg_ref[...] += jnp.sum(dh * xh, axis=0, keepdims=True)
        db_ref[...] += jnp.sum(dh, axis=0, keepdims=True)
        dxh = dh * g
        dr = rstd * (dxh - jnp.mean(dxh, axis=-1, keepdims=True) - xh * jnp.mean(dxh * xh, axis=-1, keepdims=True))
        dres_ref[...] = ALPHA * dr
        drb = dr.astype(BF16)
        dy = _mm_nt(drb, wout)
        dw_ref[...] += _mm_tn(y, drb)
        dcat = dy * sl
        dgate_ref[...] = dy * cat * (sg * (1.0 + gate * (1.0 - sg)))
        if lse is None:
            tok_out[0][...] = dcat[:, :TOK_W]
        else:
            _attn_stats(tok_ref[...], dcat[:, :TOK_W], refs[9], *tok_out)
        dqm, dkv = _mem_attn_bwd(dcat[:, TOK_W:], qm, kv, ps)
        dqm_ref[...] = dqm
        dkv_ref[...] += dkv

    npair = N_TOK_HEADS // 2
    tok_shapes = [jax.ShapeDtypeStruct((s, TOK_W), F32)] if lse is None else [
        jax.ShapeDtypeStruct((s, TOK_W), BF16), jax.ShapeDtypeStruct((TOK_W, s), BF16),
        jax.ShapeDtypeStruct((npair, s, HEAD_PAD), F32)]
    tok_specs = [_rows(tb, TOK_W)] if lse is None else [
        _rows(tb, TOK_W), pl.BlockSpec((TOK_W, tb), lambda i: (0, i)),
        pl.BlockSpec((npair, tb, HEAD_PAD), lambda i: (0, i, 0))]
    outs = (jax.ShapeDtypeStruct((s, 1024), F32), *tok_shapes,
            jax.ShapeDtypeStruct((s, 1024), F32), jax.ShapeDtypeStruct((s, MEM_W), F32),
            jax.ShapeDtypeStruct((1024, 1024), F32), jax.ShapeDtypeStruct((MEM_LEN, 512), F32),
            jax.ShapeDtypeStruct((1, 1024), F32), jax.ShapeDtypeStruct((1, 1024), F32),
            jax.ShapeDtypeStruct((1, 1), F32))
    args = (tok, gate, qm, kv, wout, h_in, g, b, up) + (() if lse is None else (lse,))
    return _run(
        body, name="mix_bwd_loss" if from_loss else "mix_bwd", grid=(s // tb,), out_shape=outs,
        in_specs=[_rows(tb, TOK_W), _rows(tb, 1024), _rows(tb, MEM_W), _const((MEM_LEN, 512)), _const((1024, 1024)),
                  _rows(tb, 1024), _const((1, 1024)), _const((1, 1024)), _rows(tb, 1024)]
        + ([] if lse is None else [_rows(tb, TOK_W)]),
        out_specs=(_rows(tb, 1024), *tok_specs, _rows(tb, 1024), _rows(tb, MEM_W), _const((1024, 1024)),
                   _const((MEM_LEN, 512)), _const((1, 1024)), _const((1, 1024)), _const((1, 1))),
        args=args, sem=("arbitrary",), exchange=exchange)


def _shift_down(u, tail, k):
    if k == 0:
        return u
    r = pltpu.roll(u, k, 0)
    row8 = _iota((8, u.shape[1]), 0)
    head = jnp.where(row8 < k, pltpu.roll(tail, k, 0), r[:8])
    return jnp.concatenate([head, r[8:]], axis=0)


def _shift_up(d, head, k):
    if k == 0:
        return d
    n = d.shape[0]
    r = pltpu.roll(d, n - k, 0)
    row8 = _iota((8, d.shape[1]), 0)
    last = jnp.where(row8 >= 8 - k, pltpu.roll(head, 8 - k, 0), r[n - 8:])
    return jnp.concatenate([r[:n - 8], last], axis=0)


def _scan_down(a, b):
    n = a.shape[0]
    row = _iota(a.shape, 0)
    s = 1
    while s < n:
        ok = row >= s
        a_s = jnp.where(ok, pltpu.roll(a, s, 0), 1.0)
        b_s = jnp.where(ok, pltpu.roll(b, s, 0), 0.0)
        b = a * b_s + b
        a = a * a_s
        s *= 2
    return a, b


def _scan_up(a, b):
    n = a.shape[0]
    row = _iota(a.shape, 0)
    s = 1
    while s < n:
        ok = row < n - s
        a_s = jnp.where(ok, pltpu.roll(a, n - s, 0), 1.0)
        b_s = jnp.where(ok, pltpu.roll(b, n - s, 0), 0.0)
        b = a * b_s + b
        a = a * a_s
        s *= 2
    return a, b


def _softplus(x):
    return jnp.maximum(x, 0.0) + jnp.log(1.0 + jnp.exp(-jnp.abs(x)))


def _lru_gates(u, tail, cw, cb, wr, br, wi, bi, lam):
    us = [_shift_down(u, tail, k) for k in range(4)]
    xc = cb + us[3] * cw[0:1] + us[2] * cw[1:2] + us[1] * cw[2:3] + us[0] * cw[3:4]
    xb = xc.astype(BF16)
    pre_r = jnp.concatenate([_mm(xb[:, 256 * g:256 * (g + 1)], wr[g]) for g in range(3)], axis=1) + br
    pre_i = jnp.concatenate([_mm(xb[:, 256 * g:256 * (g + 1)], wi[g]) for g in range(3)], axis=1) + bi
    rg, ig = jax.nn.sigmoid(pre_r), jax.nn.sigmoid(pre_i)
    clam = -LRU_C * _softplus(-lam)
    la = clam * rg
    a = jnp.exp(la)
    mm = jnp.sqrt(-jnp.tanh(la) * (a * a + 1.0))
    return us, xc, xb, rg, ig, clam, la, a, mm


def _lru_fwd(h, win, cw, cb, wr, br, wi, bi, lam):
    s = h.shape[0]
    tb = min(TB_LRU, s)

    def body(h_ref, win_ref, cw_ref, cb_ref, wr_ref, br_ref, wi_ref, bi_ref, lam_ref,
             u_ref, gate_ref, qm_ref, hs_ref, tail_sc, carry_sc):
        @pl.when(pl.program_id(0) == 0)
        def _():
            tail_sc[...] = jnp.zeros_like(tail_sc)
            carry_sc[...] = jnp.zeros_like(carry_sc)

        hb = h_ref[...].astype(BF16)
        z = jnp.concatenate([_mm(hb, win_ref[sh]) for sh in range(4)], axis=1)
        u = z[:, :TOK_W]
        u_ref[...] = u
        gate_ref[...] = z[:, TOK_W:TOK_W + 1024]
        qm_ref[...] = z[:, TOK_W + 1024:].astype(BF16)
        _, xc, _, _, ig, _, _, a, mm = _lru_gates(u, tail_sc[...], cw_ref[...], cb_ref[...], wr_ref[...], br_ref[...],
                                                 wi_ref[...], bi_ref[...], lam_ref[...])
        big_a, big_b = _scan_down(a, mm * (ig * xc))
        hs = big_a * carry_sc[0:1, :] + big_b
        hs_ref[...] = hs
        tail_sc[...] = u[tb - 8:, :]
        carry_sc[...] = jnp.broadcast_to(hs[tb - 1:tb, :], carry_sc.shape)

    outs = (jax.ShapeDtypeStruct((s, TOK_W), F32), jax.ShapeDtypeStruct((s, 1024), F32),
            jax.ShapeDtypeStruct((s, MEM_W), BF16), jax.ShapeDtypeStruct((s, TOK_W), F32))
    return pl.pallas_call(
        body, name="lru_fwd", grid=(s // tb,), out_shape=outs,
        in_specs=[_rows(tb, 1024), _const((4, 1024, 512)), _const((4, TOK_W)), _const((1, TOK_W)),
                  _const((3, 256, 256)), _const((1, TOK_W)), _const((3, 256, 256)), _const((1, TOK_W)),
                  _const((1, TOK_W))],
        out_specs=(_rows(tb, TOK_W), _rows(tb, 1024), _rows(tb, MEM_W), _rows(tb, TOK_W)),
        scratch_shapes=[pltpu.VMEM((8, TOK_W), F32), pltpu.VMEM((8, TOK_W), F32)],
        compiler_params=_params(),
    )(h, win, cw, cb, wr, br, wi, bi, lam)


def _lru_bwd(dhs, dgate, dqm, dres, h, u, hs, win, cw, cb, wr, br, wi, bi, lam):
    s = h.shape[0]
    tb = min(TB_LRU, s)
    nb = s // tb

    def rev(w):
        return pl.BlockSpec((tb, w), lambda i: (nb - 1 - i, 0))

    def prev_tail(w):
        return pl.BlockSpec((8, w), lambda i: (jnp.maximum((nb - 1 - i) * (tb // 8) - 1, 0), 0))

    def body(dhs_ref, dgate_ref, dqm_ref, dres_ref, h_ref, u_ref, hs_ref, ut_ref, hst_ref, win_ref, cw_ref, cb_ref,
             wr_ref, br_ref, wi_ref, bi_ref, lam_ref,
             dh_ref, dwin_ref, dcw_ref, dcb_ref, dwr_ref, dbr_ref, dwi_ref, dbi_ref, dlam_ref, ecar_sc, dxc_sc):
        i = pl.program_id(0)

        @pl.when(i == 0)
        def _():
            for r in (dwin_ref, dcw_ref, dcb_ref, dwr_ref, dbr_ref, dwi_ref, dbi_ref, dlam_ref, ecar_sc, dxc_sc):
                r[...] = jnp.zeros_like(r)

        first = (i == nb - 1)
        u = u_ref[...]
        utail = jnp.where(first, 0.0, ut_ref[...])
        hstail = jnp.where(first, 0.0, hst_ref[...])
        cw, wr, wi, lam = cw_ref[...], wr_ref[...], wi_ref[...], lam_ref[...]
        us, xc, xb, rg, ig, clam, la, a, mm = _lru_gates(u, utail, cw, cb_ref[...], wr, br_ref[...], wi, bi_ref[...], lam)
        row = _iota(a.shape, 0)
        a_next = jnp.where(row < tb - 1, pltpu.roll(a, tb - 1, 0), 1.0)
        big_a, big_b = _scan_up(a_next, dhs_ref[...])
        e = big_a * ecar_sc[0:1, :] + big_b
        ecar_sc[...] = jnp.broadcast_to(a[0:1, :] * e[0:1, :], ecar_sc.shape)
        hs_prev = _shift_down(hs_ref[...], hstail, 1)
        da = e * hs_prev
        ix = ig * xc
        dmm = e * ix
        dix = e * mm
        dla = da * a - dmm * (a * a) / mm
        dlam_ref[...] += jnp.sum(dla * rg, axis=0, keepdims=True)
        dpr = (dla * clam) * rg * (1.0 - rg)
        dpi = (dix * xc) * ig * (1.0 - ig)
        dbr_ref[...] += jnp.sum(dpr, axis=0, keepdims=True)
        dbi_ref[...] += jnp.sum(dpi, axis=0, keepdims=True)
        dprb, dpib = dpr.astype(BF16), dpi.astype(BF16)
        dxc_g = []
        for g in range(3):
            sl = slice(256 * g, 256 * (g + 1))
            dwr_ref[g] += _mm_tn(xb[:, sl], dprb[:, sl])
            dwi_ref[g] += _mm_tn(xb[:, sl], dpib[:, sl])
            dxc_g.append(_mm_nt(dprb[:, sl], wr[g]) + _mm_nt(dpib[:, sl], wi[g]))
        dxc = dix * ig + jnp.concatenate(dxc_g, axis=1)
        dcb_ref[...] += jnp.sum(dxc, axis=0, keepdims=True)
        dcw_ref[...] += jnp.concatenate([jnp.sum(dxc * us[3 - tap], axis=0, keepdims=True) for tap in range(4)], axis=0)
        head = dxc_sc[...]
        du = dxc * cw[3:4]
        for k in range(1, 4):
            du = du + _shift_up(dxc, head, k) * cw[3 - k:4 - k]
        dxc_sc[...] = dxc[:8, :]
        dz = jnp.concatenate([du, dgate_ref[...], dqm_ref[...]], axis=1).astype(BF16)
        hb = h_ref[...].astype(BF16)
        dh = dres_ref[...]
        for sh in range(4):
            dzs = dz[:, 512 * sh:512 * (sh + 1)]
            dh = dh + _mm_nt(dzs, win_ref[sh])
            dwin_ref[sh] += _mm_tn(hb, dzs)
        dh_ref[...] = dh

        @pl.when(i == nb - 1)
        def _():
            dlam_ref[...] = dlam_ref[...] * (LRU_C * jax.nn.sigmoid(-lam))

    outs = (jax.ShapeDtypeStruct((s, 1024), F32), jax.ShapeDtypeStruct((4, 1024, 512), F32),
            jax.ShapeDtypeStruct((4, TOK_W), F32), jax.ShapeDtypeStruct((1, TOK_W), F32),
            jax.ShapeDtypeStruct((3, 256, 256), F32), jax.ShapeDtypeStruct((1, TOK_W), F32),
            jax.ShapeDtypeStruct((3, 256, 256), F32), jax.ShapeDtypeStruct((1, TOK_W), F32),
            jax.ShapeDtypeStruct((1, TOK_W), F32))
    return pl.pallas_call(
        body, name="lru_bwd", grid=(nb,), out_shape=outs,
        in_specs=[rev(TOK_W), rev(1024), rev(MEM_W), rev(1024), rev(1024), rev(TOK_W), rev(TOK_W),
                  prev_tail(TOK_W), prev_tail(TOK_W),
                  _const((4, 1024, 512)), _const((4, TOK_W)), _const((1, TOK_W)), _const((3, 256, 256)),
                  _const((1, TOK_W)), _const((3, 256, 256)), _const((1, TOK_W)), _const((1, TOK_W))],
        out_specs=(rev(1024), _const((4, 1024, 512)), _const((4, TOK_W)), _const((1, TOK_W)), _const((3, 256, 256)),
                   _const((1, TOK_W)), _const((3, 256, 256)), _const((1, TOK_W)), _const((1, TOK_W))),
        scratch_shapes=[pltpu.VMEM((8, TOK_W), F32), pltpu.VMEM((8, TOK_W), F32)],
        compiler_params=_params(),
    )(dhs, dgate, dqm, dres, h, u, hs, u, hs, win, cw, cb, wr, br, wi, bi, lam)


def _adamw_update(w_ref, g_ref, m_ref, v_ref, d_ref, nm_ref, nv_ref):
    g = g_ref[...]
    nm = ADAM_B1 * m_ref[...] + (1.0 - ADAM_B1) * g
    nv = ADAM_B2 * v_ref[...] + (1.0 - ADAM_B2) * (g * g)
    m_hat = nm / (1.0 - ADAM_B1 ** ADAM_STEP)
    v_hat = nv / (1.0 - ADAM_B2 ** ADAM_STEP)
    d_ref[...] = -ADAM_LR * (m_hat / (jnp.sqrt(v_hat) + ADAM_EPS) + ADAM_WD * w_ref[...])
    nm_ref[...] = nm
    nv_ref[...] = nv


def _adamw(name, w, g, m, v):
    rows, cols = w.shape
    tb = 256 if rows % 256 == 0 else rows

    def body(*refs):
        _adamw_update(*refs)

    shp = jax.ShapeDtypeStruct((rows, cols), F32)
    return pl.pallas_call(
        body, name="adamw_" + name, grid=(rows // tb,), out_shape=(shp, shp, shp),
        in_specs=[_rows(tb, cols)] * 4, out_specs=(_rows(tb, cols),) * 3,
        compiler_params=_params(("parallel",)),
    )(w, g, m, v)


def _adamw_small(items):
    n = len(items)

    def body(*refs):
        for k in range(n):
            _adamw_update(*refs[4 * k:4 * k + 4], *refs[4 * n + 3 * k:4 * n + 3 * k + 3])

    args = [a for it in items for a in it]
    shapes = [jax.ShapeDtypeStruct(it[0].shape, F32) for it in items for _ in range(3)]
    outs = pl.pallas_call(
        body, name="adamw_small", grid=(1,), out_shape=tuple(shapes),
        in_specs=[_const(a.shape) for a in args], out_specs=tuple(_const(sh.shape) for sh in shapes),
        compiler_params=_params(),
    )(*args)
    return [outs[3 * k:3 * k + 3] for k in range(n)]


def _row_block(rows, cap=2048):
    return max(t for t in range(8, cap + 1, 8) if rows % t == 0)


def _add2(a, b):
    rows = a.shape[0]
    tb = _row_block(rows)

    def body(a_ref, b_ref, o_ref):
        o_ref[...] = a_ref[...] + b_ref[...]

    return pl.pallas_call(
        body, name="add_sibling", grid=(rows // tb,), out_shape=jax.ShapeDtypeStruct(a.shape, F32),
        in_specs=[_rows(tb, 128)] * 2, out_specs=_rows(tb, 128), compiler_params=_params(("parallel",)),
    )(a, b)


def _sum_slots(landed, own, rows):
    tb = _row_block(rows, 1024)

    def body(l_ref, o_ref, out_ref):
        t = 2 * lax.axis_index("x") + lax.axis_index("y")
        r = [jnp.where(t == s, o_ref[s], l_ref[s].astype(F32)) for s in range(4)]
        out_ref[...] = ((r[0] + r[1]) + r[2]) + r[3]

    return pl.pallas_call(
        body, name="sum_chips", grid=(rows // tb,), out_shape=jax.ShapeDtypeStruct((rows, 128), F32),
        in_specs=[pl.BlockSpec((4, tb, 128), lambda i: (0, i, 0))] * 2, out_specs=_rows(tb, 128),
        compiler_params=_params(("parallel",)),
    )(landed, own)


_ANY = pl.BlockSpec(memory_space=pl.ANY)


def _place():
    x, y, c = lax.axis_index("x"), lax.axis_index("y"), lax.axis_index("c")
    return x, y, c, [(1 - x, y), (x, 1 - y), (1 - x, 1 - y)]


def _remote(src, dst, ssem, rsem, to):
    return pltpu.make_async_remote_copy(src_ref=src, dst_ref=dst, send_sem=ssem, recv_sem=rsem, device_id=to,
                                        device_id_type=MESH)


class _Exchange:
    def __init__(self, ins, out_shape, sems, start, finish):
        self.ins, self.out_shape, self.sems, self.start, self.finish = ins, out_shape, sems, start, finish


def _run(body, *, name, grid, in_specs, out_specs, out_shape, args, scratch=(), sem, exchange=None,
         vmem_limit=VMEM_LIMIT):
    if exchange is None:
        return pl.pallas_call(body, name=name, grid=grid, out_shape=tuple(out_shape), in_specs=list(in_specs),
                              out_specs=tuple(out_specs), scratch_shapes=list(scratch),
                              compiler_params=_params(sem, vmem_limit))(*args)
    n_in, n_out, n_sc = len(args), len(out_shape), len(scratch)
    k_in, k_out = len(exchange.ins), len(exchange.out_shape)

    def fused(*refs):
        ins, refs = refs[:n_in], refs[n_in:]
        xin, refs = refs[:k_in], refs[k_in:]
        outs, refs = refs[:n_out], refs[n_out:]
        xout, refs = refs[:k_out], refs[k_out:]
        sc, xsem = refs[:n_sc], refs[n_sc:]
        first = pl.program_id(0) == 0
        last = pl.program_id(0) == grid[0] - 1
        for a in range(1, len(grid)):
            first = first & (pl.program_id(a) == 0)
            last = last & (pl.program_id(a) == grid[a] - 1)

        @pl.when(first)
        def _():
            exchange.start(xin, xout, xsem)

        body(*ins, *outs, *sc)

        @pl.when(last)
        def _():
            exchange.finish(xin, xout, xsem)

    return pl.pallas_call(
        fused, name=name, grid=grid, out_shape=(*out_shape, *exchange.out_shape),
        in_specs=[*in_specs, *[_ANY] * k_in], out_specs=(*out_specs, *[_ANY] * k_out),
        scratch_shapes=[*scratch, *exchange.sems],
        compiler_params=_params(("arbitrary",) * len(grid), vmem_limit),
    )(*args, *exchange.ins)


def _run_exchange(exchange, name):
    def body(*refs):
        k_in, k_out = len(exchange.ins), len(exchange.out_shape)
        xin, xout, xsem = refs[:k_in], refs[k_in:k_in + k_out], refs[k_in + k_out:]
        exchange.start(xin, xout, xsem)
        exchange.finish(xin, xout, xsem)

    return pl.pallas_call(
        body, name=name, out_shape=tuple(exchange.out_shape), in_specs=[_ANY] * len(exchange.ins),
        out_specs=tuple([_ANY] * len(exchange.out_shape)), scratch_shapes=list(exchange.sems),
    )(*exchange.ins)


def _gather_shards(wsh):
    _, hh, _ = wsh.shape

    def first_hop(w_ref, out_ref, ssems, rsems):
        x, y, c, chips = _place()
        t = 2 * x + y
        return [_remote(w_ref.at[c], out_ref.at[t, c], ssems.at[j], rsems.at[j], (cx, cy, c))
                for j, (cx, cy) in enumerate(chips)]

    def start(xin, xout, xsem):
        for cp in first_hop(xin[0], xout[0], *xsem):
            cp.start()

    def finish(xin, xout, xsem):
        out_ref, (ssems, rsems) = xout[0], xsem
        first = first_hop(xin[0], out_ref, *xsem)
        x, y, c, chips = _place()
        passed = []
        for j, (cx, cy) in enumerate(chips):
            got = out_ref.at[2 * cx + cy, c]
            _remote(got, got, ssems.at[j], rsems.at[j], (cx, cy, c)).wait_recv()
            cp = _remote(got, got, ssems.at[3 + j], rsems.at[3 + j], (x, y, 1 - c))
            cp.start()
            passed.append(cp)
        for j, (cx, cy) in enumerate(chips):
            got = out_ref.at[2 * cx + cy, 1 - c]
            _remote(got, got, ssems.at[3 + j], rsems.at[3 + j], (x, y, 1 - c)).wait_recv()
        for cp in first + passed:
            cp.wait_send()

    return _Exchange([wsh], [jax.ShapeDtypeStruct((4, 2, hh, 128), wsh.dtype)],
                     [pltpu.SemaphoreType.DMA((6,)), pltpu.SemaphoreType.DMA((6,))], start, finish)


def _gathered(landed, own):
    t = 2 * lax.axis_index("x") + lax.axis_index("y")
    return lax.dynamic_update_slice(landed, own[None], (t, 0, 0, 0))


def _swap_sibling(v):
    def copy(xin, xout, xsem):
        x, y, c, _ = _place()
        return _remote(xin[0], xout[0], xsem[0], xsem[1], (x, y, 1 - c))

    return _Exchange([v], [jax.ShapeDtypeStruct(v.shape, v.dtype)],
                     [pltpu.SemaphoreType.DMA, pltpu.SemaphoreType.DMA],
                     lambda *a: copy(*a).start(), lambda *a: copy(*a).wait())


def _scatter_chips(parts):
    n = len(parts)

    def copies(xin, xout, ssems, rsems):
        x, y, c, chips = _place()
        t = 2 * x + y
        return [_remote(xin[k].at[2 * cx + cy], xout[k].at[t], ssems.at[n * j + k], rsems.at[n * j + k], (cx, cy, c))
                for j, (cx, cy) in enumerate(chips) for k in range(n)]

    def start(xin, xout, xsem):
        for cp in copies(xin, xout, *xsem):
            cp.start()

    def finish(xin, xout, xsem):
        ssems, rsems = xsem
        x, y, c, chips = _place()
        for j, (cx, cy) in enumerate(chips):
            for k in range(n):
                got = xout[k].at[2 * cx + cy]
                _remote(got, got, ssems.at[n * j + k], rsems.at[n * j + k], (cx, cy, c)).wait_recv()
        for cp in copies(xin, xout, *xsem):
            cp.wait_send()

    return _Exchange(parts, [jax.ShapeDtypeStruct(a.shape, a.dtype) for a in parts],
                     [pltpu.SemaphoreType.DMA((3 * n,)), pltpu.SemaphoreType.DMA((3 * n,))], start, finish)


def _share_reduced(piece, eighth):
    def copies(t_ref, mine_r, sib_ref, rall_ref, ssems, rsems, lsem):
        x, y, c, _ = _place()
        me = 4 * x + 2 * y + c
        loc = pltpu.make_async_copy(mine_r, rall_ref.at[me], lsem)
        sends = [_remote(t_ref, sib_ref, ssems.at[0], rsems.at[0], (x, y, 1 - c))]
        peers = []
        for mask in range(1, 8):
            px = 1 - x if mask & 4 else x
            py = 1 - y if mask & 2 else y
            pc = 1 - c if mask & 1 else c
            peers.append((mask, px, py, pc))
            sends.append(_remote(mine_r, rall_ref.at[me], ssems.at[mask], rsems.at[mask], (px, py, pc)))
        return loc, sends, peers

    def start(xin, xout, xsem):
        loc, sends, _ = copies(*xin, *xout, *xsem)
        for cp in [loc] + sends:
            cp.start()

    def finish(xin, xout, xsem):
        (sib_ref, rall_ref), (ssems, rsems, _) = xout, xsem
        loc, sends, peers = copies(*xin, *xout, *xsem)
        x, y, c, _ = _place()
        _remote(sib_ref, sib_ref, ssems.at[0], rsems.at[0], (x, y, 1 - c)).wait_recv()
        for mask, px, py, pc in peers:
            got = rall_ref.at[4 * px + 2 * py + pc]
            _remote(got, got, ssems.at[mask], rsems.at[mask], (px, py, pc)).wait_recv()
        for cp in sends:
            cp.wait_send()
        loc.wait()

    return _Exchange([piece, eighth],
                     [jax.ShapeDtypeStruct(piece.shape, F32), jax.ShapeDtypeStruct((8, *eighth.shape), F32)],
                     [pltpu.SemaphoreType.DMA((8,)), pltpu.SemaphoreType.DMA((8,)), pltpu.SemaphoreType.DMA],
                     start, finish)


def _ceil_to(n, m):
    return -(-n // m) * m


def _pack_bf16(parts):
    blocks = [p.reshape(-1, 128) for p in parts]
    rows = jnp.concatenate([jnp.pad(b, ((0, -b.shape[0] % 16), (0, 0))) for b in blocks])
    hw = _ceil_to(rows.shape[0], 32) // 2
    return jnp.pad(rows, ((0, 2 * hw - rows.shape[0]), (0, 0))).reshape(2, hw, 128)


def _segments(wall, parts):
    wall = wall.reshape(4, -1, 128)
    out, row = [], 0
    for p in parts:
        n = p.size // 128
        out.append(wall[:, row:row + n].reshape(4, *p.shape))
        row += _ceil_to(n, 16)
    return out


class _GradReduce:
    def __init__(self, sharded, replicated, c_idx, wire_bf16=False):
        self.c_idx, self.wire_bf16 = c_idx, wire_bf16
        self.rowwise = [(n, g.shape[1:]) for n, g in sharded if math.prod(g.shape[1:]) % 128 == 0]
        self.small = [(n, g.shape[1:]) for n, g in sharded if math.prod(g.shape[1:]) % 128 != 0]
        self.replicated = [(n, g.shape[0]) for n, g in replicated]
        by_name = dict(sharded)
        blocks = [by_name[n].reshape(4, -1, 128) for n, _ in self.rowwise]
        if self.small:
            rest = jnp.concatenate([by_name[n].reshape(4, -1) for n, _ in self.small], axis=1)
            blocks.append(jnp.pad(rest, ((0, 0), (0, -rest.shape[1] % 128))).reshape(4, -1, 128))
        blocks = [jnp.pad(b, ((0, 0), (0, -b.shape[1] % 8), (0, 0))) for b in blocks]
        rows = sum(b.shape[1] for b in blocks)
        self.hs = _ceil_to(rows, 256) // 2
        sh = jnp.concatenate(blocks + [jnp.zeros((4, 2 * self.hs - rows, 128), F32)], axis=1)
        rp = jnp.concatenate([g for _, g in replicated])
        self.rr = _ceil_to(_ceil_to(rp.shape[0], 128) // 128, 64) // 8
        rp = jnp.pad(rp, (0, 8 * self.rr * 128 - rp.shape[0])).reshape(4, 2, self.rr, 128)
        self.hh = self.hs + self.rr

        def half(c):
            return jnp.concatenate([lax.dynamic_slice_in_dim(sh, c * self.hs, self.hs, axis=1),
                                    lax.dynamic_index_in_dim(rp, c, axis=1, keepdims=False)],
                                   axis=1).reshape(4 * self.hh, 128)

        self.mine, self.other = half(c_idx), half(1 - c_idx)

    def swap(self):
        return _swap_sibling(self.other)

    def swapped(self, got):
        self.chip_sum = _add2(self.mine, got).reshape(4, self.hh, 128)

    def scatter(self):
        self.own_r = self.chip_sum[:, self.hs:]
        if self.wire_bf16:
            return _scatter_chips([self.chip_sum[:, :self.hs].astype(BF16), self.own_r])
        return _scatter_chips([self.chip_sum])

    def scattered(self, landed, landed_r=None):
        if landed_r is None:
            landed_r = landed[:, self.hs:]
        self.piece = _sum_slots(landed, self.chip_sum, self.hs)
        self.eighth = _sum_slots(landed_r, self.own_r, self.rr)

    def share(self):
        return _share_reduced(self.piece, self.eighth)

    def shared(self, sibling, rall):
        mine, sib = self.piece, sibling
        self.shard = jnp.where(self.c_idx == 0, jnp.concatenate([mine, sib]), jnp.concatenate([sib, mine]))
        self.rall = rall

    def reduced(self):
        out, row = {}, 0
        for name, shape in self.rowwise:
            rows = math.prod(shape) // 128
            out[name] = self.shard[row:row + rows].reshape(shape)
            row += _ceil_to(rows, 8)
        for group, flat in ((self.small, self.shard[row:].reshape(-1)), (self.replicated, self.rall.reshape(-1))):
            off = 0
            for name, shape in group:
                n = math.prod(shape) if isinstance(shape, tuple) else shape
                out[name] = flat[off:off + n]
                off += n
        return out


def _col_shards(w2d):
    rows, cols = w2d.shape
    return w2d.reshape(rows, 4, cols // 4).transpose(1, 0, 2)


_WIN0_PARTS = ((0, 384, 1280), (384, 640, 1664), (640, 672, 1984), (672, 1696, 0), (1696, 1952, 1024))


def _win0_aligned(shards):
    def cols(a, b):
        return [shards[s][:, max(a, 488 * s) - 488 * s:min(b, 488 * (s + 1)) - 488 * s]
                for s in range(4) if max(a, 488 * s) < min(b, 488 * (s + 1))]

    zeros = jnp.zeros((1024, 64), shards.dtype)
    return jnp.concatenate(cols(672, 1696) + cols(1696, 1952) + cols(0, 384) + cols(384, 640)
                           + [zeros] + cols(640, 672) + [zeros[:, :32]], axis=1)


def _win0_shards(dwin0p):
    shards = []
    for s in range(4):
        lo, hi = 488 * s, 488 * (s + 1)
        cols = [dwin0p[:, p + max(lo, a) - a:p + min(hi, b) - a] for a, b, p in _WIN0_PARTS if max(lo, a) < min(hi, b)]
        shards.append(jnp.concatenate(cols, axis=1))
    return jnp.stack(shards)


def _block_diag4(w):
    eye = jnp.eye(4, dtype=w.dtype)
    return jnp.einsum("gaij,ab->gaibj", w.reshape(3, 4, 64, 64), eye).reshape(3, 256, 256)


def _diag_blocks4(w):
    w5 = w.reshape(3, 4, 64, 4, 64)
    return jnp.stack([w5[:, a, :, a, :] for a in range(4)], axis=1).reshape(12, 64, 64)


def kernel(x, mem, positions, mla_w_in, mla_q_norm, mla_w_uq, mla_kv_norm, mla_w_ukv, lru_w_in, lru_conv_w, lru_conv_b, lru_w_rgate, lru_b_rgate, lru_w_igate, lru_b_igate, lru_lambda, w_mem_kv, w_out, ln_g, ln_b, loss_target, m_mla_w_in, m_mla_q_norm, m_mla_w_uq, m_mla_kv_norm, m_mla_w_ukv, m_lru_w_in, m_lru_conv_w, m_lru_conv_b, m_lru_w_rgate, m_lru_b_rgate, m_lru_w_igate, m_lru_b_igate, m_lru_lambda, m_w_mem_kv, m_w_out, m_ln_g, m_ln_b, v_mla_w_in, v_mla_q_norm, v_mla_w_uq, v_mla_kv_norm, v_mla_w_ukv, v_lru_w_in, v_lru_conv_w, v_lru_conv_b, v_lru_w_rgate, v_lru_b_rgate, v_lru_w_igate, v_lru_b_igate, v_lru_lambda, v_w_mem_kv, v_w_out, v_ln_g, v_ln_b):
    s = x.shape[1]
    c_idx = lax.axis_index("c")
    x2, mem2, tgt2 = x[0], mem[0], loss_target[0]

    first = [p.astype(BF16) for p in (mla_w_in[0], mla_w_uq[0], mla_w_ukv[0])]
    buf = _pack_bf16(first)
    mla_shards = _segments(_gathered(_run_exchange(_gather_shards(buf), "gather_weights")[0], buf), first)

    mid = [w_mem_kv.astype(BF16), w_out[0].astype(BF16)]
    buf_mid = _pack_bf16(mid)

    def mid_weights(landed):
        wmem, wout0 = _segments(_gathered(landed[0], buf_mid), mid)
        return wmem.transpose(1, 0, 2, 3).reshape(2, 1024, 512), wout0.reshape(1024, 1024)

    small = jnp.concatenate([lru_conv_w[0].reshape(-1), lru_conv_b[0], lru_b_rgate[0], lru_b_igate[0], lru_lambda[0]])
    late = [lru_w_in[0].astype(BF16), w_out[1].astype(BF16), lax.bitcast_convert_type(small, BF16)]
    buf_late = _pack_bf16(late)

    def late_weights(landed):
        win1, wout1, small_bits = _segments(_gathered(landed[0], buf_late), late)
        small_all = lax.bitcast_convert_type(small_bits, F32)
        cw = small_all[:, :768].reshape(4, 4, 192).transpose(1, 0, 2).reshape(4, TOK_W)
        cb, br, bi, lam = (small_all[:, 768 + 192 * k:960 + 192 * k].reshape(1, TOK_W) for k in range(4))
        return win1, wout1.reshape(1024, 1024), cw, cb, br, bi, lam

    def reduce_late(g):
        return _GradReduce(
            [("lru_w_in", g["lru_w_in"]), ("lru_conv_w", _col_shards(g["lru_conv_w"])),
             ("lru_conv_b", _col_shards(g["lru_conv_b"])), ("lru_b_rgate", _col_shards(g["lru_b_rgate"])),
             ("lru_b_igate", _col_shards(g["lru_b_igate"])), ("lru_lambda", _col_shards(g["lru_lambda"])),
             ("w_mem_kv1", g["w_mem_kv1"].reshape(4, 256, 512)), ("w_out1", g["w_out1"].reshape(4, 256, 1024))],
            [("lru_w_rgate", g["lru_w_rgate"].reshape(-1)), ("lru_w_igate", g["lru_w_igate"].reshape(-1)),
             ("ln_g1", g["ln_g1"].reshape(-1)), ("ln_b1", g["ln_b1"].reshape(-1))], c_idx)

    g0, late_red = _local_step(
        x2, mem2, positions.reshape(s, 1), tgt2, *mla_shards, mla_q_norm, mla_kv_norm, lru_w_rgate[0], lru_w_igate[0],
        ln_g, ln_b, mid_weights, late_weights, _gather_shards(buf_mid), _gather_shards(buf_late), reduce_late)

    early_red = _GradReduce(
        [("mla_w_in", g0["mla_w_in"]), ("mla_w_uq", _col_shards(g0["mla_w_uq"])),
         ("mla_w_ukv", _col_shards(g0["mla_w_ukv"])), ("w_mem_kv0", g0["w_mem_kv0"].reshape(4, 256, 512)),
         ("w_out0", g0["w_out0"].reshape(4, 256, 1024))],
        [("mla_q_norm", g0["mla_q_norm"].reshape(-1)), ("mla_kv_norm", g0["mla_kv_norm"].reshape(-1)),
         ("ln_g0", g0["ln_g0"].reshape(-1)), ("ln_b0", g0["ln_b0"].reshape(-1)), ("loss", g0["loss"].reshape(-1))],
        c_idx, wire_bf16=True)
    early_red.swapped(*_run_exchange(early_red.swap(), "swap_sibling"))
    early_red.scattered(*_run_exchange(early_red.scatter(), "scatter_chips"))
    early_red.shared(*_run_exchange(early_red.share(), "share_reduced"))
    red = {**late_red.reduced(), **early_red.reduced()}
    red["w_mem_kv"] = jnp.concatenate([red["w_mem_kv0"], red["w_mem_kv1"]])
    red["w_out"] = jnp.concatenate([red["w_out0"], red["w_out1"]])
    red["ln_g"] = jnp.concatenate([red["ln_g0"], red["ln_g1"]])
    red["ln_b"] = jnp.concatenate([red["ln_b0"], red["ln_b1"]])

    weights = dict(mla_w_in=mla_w_in, mla_q_norm=mla_q_norm, mla_w_uq=mla_w_uq, mla_kv_norm=mla_kv_norm,
                   mla_w_ukv=mla_w_ukv, lru_w_in=lru_w_in, lru_conv_w=lru_conv_w, lru_conv_b=lru_conv_b,
                   lru_w_rgate=lru_w_rgate, lru_b_rgate=lru_b_rgate, lru_w_igate=lru_w_igate, lru_b_igate=lru_b_igate,
                   lru_lambda=lru_lambda, w_mem_kv=w_mem_kv, w_out=w_out, ln_g=ln_g, ln_b=ln_b)
    m_in = dict(mla_w_in=m_mla_w_in, mla_q_norm=m_mla_q_norm, mla_w_uq=m_mla_w_uq, mla_kv_norm=m_mla_kv_norm,
                mla_w_ukv=m_mla_w_ukv, lru_w_in=m_lru_w_in, lru_conv_w=m_lru_conv_w, lru_conv_b=m_lru_conv_b,
                lru_w_rgate=m_lru_w_rgate, lru_b_rgate=m_lru_b_rgate, lru_w_igate=m_lru_w_igate,
                lru_b_igate=m_lru_b_igate, lru_lambda=m_lru_lambda, w_mem_kv=m_w_mem_kv, w_out=m_w_out, ln_g=m_ln_g,
                ln_b=m_ln_b)
    v_in = dict(mla_w_in=v_mla_w_in, mla_q_norm=v_mla_q_norm, mla_w_uq=v_mla_w_uq, mla_kv_norm=v_mla_kv_norm,
                mla_w_ukv=v_mla_w_ukv, lru_w_in=v_lru_w_in, lru_conv_w=v_lru_conv_w, lru_conv_b=v_lru_conv_b,
                lru_w_rgate=v_lru_w_rgate, lru_b_rgate=v_lru_b_rgate, lru_w_igate=v_lru_w_igate,
                lru_b_igate=v_lru_b_igate, lru_lambda=v_lru_lambda, w_mem_kv=v_w_mem_kv, w_out=v_w_out, ln_g=v_ln_g,
                ln_b=v_ln_b)
    order = ["mla_w_in", "mla_q_norm", "mla_w_uq", "mla_kv_norm", "mla_w_ukv", "lru_w_in", "lru_conv_w", "lru_conv_b",
             "lru_w_rgate", "lru_b_rgate", "lru_w_igate", "lru_b_igate", "lru_lambda", "w_mem_kv", "w_out", "ln_g",
             "ln_b"]
    grads, deltas, new_m, new_v = {}, {}, {}, {}

    def operands(name):
        shape = weights[name].shape
        two_d = (math.prod(shape[:-1]), shape[-1])
        return [a.reshape(two_d) for a in (weights[name], red[name], m_in[name], v_in[name])]

    def keep(name, g2, d2, m2, v2):
        shape = weights[name].shape
        grads[name], deltas[name] = g2.reshape(shape), d2.reshape(shape)
        new_m[name], new_v[name] = m2.reshape(shape), v2.reshape(shape)

    small = [n for n in order if weights[n].size <= 4096]
    ops = [operands(n) for n in small]
    for name, op, res in zip(small, ops, _adamw_small(ops)):
        keep(name, op[1], *res)
    for name in order:
        if name not in small:
            op = operands(name)
            keep(name, op[1], *_adamw(name, *op))
    return (red["loss"][0], g0["x"][None], *[grads[n] for n in order], *[deltas[n] for n in order],
            *[new_m[n] for n in order], *[new_v[n] for n in order])


def _local_step(x2, mem2, pos_col, tgt2, win0_sh, wuq_sh, wukv_sh, gq, gkv, w_rgate, w_igate, ln_g, ln_b,
                mid_weights, late_weights, gather_mid=None, gather_late=None, reduce_late=None):
    s = x2.shape[0]
    win0p = _win0_aligned(win0_sh)
    wuq_p = jnp.pad(wuq_sh.reshape(4, Q_LORA, 3, 96).transpose(1, 0, 2, 3).reshape(Q_LORA, 12, 96),
                    ((0, 0), (0, 0), (0, 32))).reshape(Q_LORA, QK_W)
    wukv3 = wukv_sh.reshape(4, KV_LORA, 3, 128).transpose(1, 0, 2, 3).reshape(KV_LORA, 12, 128)
    wk_p = jnp.pad(wukv3[:, :, :64], ((0, 0), (0, 0), (0, 64))).reshape(KV_LORA, QK_W)
    wv = wukv3[:, :, 64:].reshape(KV_LORA, TOK_W)
    wr_bd = _block_diag4(w_rgate).astype(BF16)
    wi_bd = _block_diag4(w_igate).astype(BF16)
    half = 16
    inv_freq = ROPE_THETA ** (-jnp.arange(half, dtype=F32) / half)
    inv_lane = jnp.concatenate([jnp.zeros((64,), F32), inv_freq, inv_freq, jnp.zeros((32,), F32)]).reshape(1, HEAD_PAD)

    gate0, qm0, cq, ckv, q_p, q_t, k_p, v_b, v_t, ctab, satab, sbtab, *landed = _mla_proj_fwd(
        x2, win0p, gq, gkv, wuq_p, wk_p, wv, pos_col, inv_lane, exchange=gather_mid)
    wmem, wout0 = mid_weights(landed)
    memkv = _mem_kv(mem2, wmem)
    tok0, lse, *landed = _attn_fwd(q_p, k_p, v_t, exchange=gather_late)
    win1, wout1, cw, cb, br, bi, lam = late_weights(landed)
    g0, b0, g1, b1 = ln_g[0:1], ln_b[0:1], ln_g[1:2], ln_b[1:2]
    h1 = _mix_fwd(tok0, gate0, qm0, memkv[0], wout0, x2, g0, b0)
    u1, gate1, qm1, hs1 = _lru_fwd(h1, win1, cw, cb, wr_bd, br, wi_bd, bi, lam)

    dres1, dtok1, dgate1, dqm1, dwout1, dmemkv1, dg1, db1, loss = _mix_bwd(
        hs1, gate1, qm1, memkv[1], wout1, h1, g1, b1, tgt2, True)
    dh1, dwin1, dcw, dcb, dwr_bd, dbr, dwi_bd, dbi, dlam = _lru_bwd(
        dtok1, dgate1, dqm1, dres1, h1, u1, hs1, win1, cw, cb, wr_bd, br, wi_bd, bi, lam)
    late = {"lru_w_in": dwin1, "lru_conv_w": dcw, "lru_conv_b": dcb, "lru_b_rgate": dbr, "lru_b_igate": dbi,
            "lru_lambda": dlam, "w_mem_kv1": _mem_kv_bwd(mem2, dmemkv1), "w_out1": dwout1,
            "lru_w_rgate": _diag_blocks4(dwr_bd), "lru_w_igate": _diag_blocks4(dwi_bd), "ln_g1": dg1, "ln_b1": db1}
    red = reduce_late(late) if reduce_late is not None else None

    dres0, dob, dobt, stats, dgate0, dqm0, dwout0, dmemkv0, dg0, db0, _, *got = _mix_bwd(
        tok0, gate0, qm0, memkv[0], wout0, x2, g0, b0, dh1, False, lse=lse, exchange=red.swap() if red else None)
    if red:
        red.swapped(*got)
    dq_p, dk_t, dv_t, *got = _attn_bwd(q_p, q_t, k_p, v_b, dob, dobt, stats,
                                       exchange=red.scatter() if red else None)
    if red:
        red.scattered(*got)
    if red:
        red.shared(*_run_exchange(red.share(), "share_reduced"))
    dx, dwin0p, dwuq_p, dwk_p, dwv, dgq, dgkv = _mla_proj_bwd(
        x2, cq, ckv, dq_p, dk_t, dv_t, dgate0, dqm0, dres0, win0p, gq, gkv, wuq_p, wk_p, wv,
        ctab, satab, sbtab)

    dwin0 = _win0_shards(dwin0p)
    dwuq = dwuq_p.reshape(Q_LORA, 12, 128)[:, :, :96].reshape(Q_LORA, 1152)
    dwukv = jnp.concatenate([dwk_p.reshape(KV_LORA, 12, 128)[:, :, :64], dwv.reshape(KV_LORA, 12, 64)],
                            axis=2).reshape(KV_LORA, 1536)
    early = {"x": dx, "loss": loss, "mla_w_in": dwin0, "mla_w_uq": dwuq, "mla_w_ukv": dwukv,
             "w_mem_kv0": _mem_kv_bwd(mem2, dmemkv0), "w_out0": dwout0, "mla_q_norm": dgq, "mla_kv_norm": dgkv,
             "ln_g0": dg0, "ln_b0": db0}
    return early, (red if red else late)
```

```python
import functools
import math

import jax
import jax.numpy as jnp
from jax import lax
from jax.experimental import pallas as pl
from jax.experimental.pallas import tpu as pltpu

F32, BF16 = jnp.float32, jnp.bfloat16
MESH = pl.DeviceIdType.MESH

D_MODEL = 1024
N_TOK_HEADS = 12
TOK_W = 768
MEM_W = 256
MEM_LEN = 256
Q_LORA, KV_LORA = 384, 256
HEAD_PAD = 128
QK_W = N_TOK_HEADS * HEAD_PAD
ATT_SCALE = 1.0 / math.sqrt(96.0)
ATT_SCALE_LOG2 = ATT_SCALE * math.log2(math.e)
ROPE_THETA = 10000.0
LRU_C = 8.0
ALPHA = 4.0 ** 0.25
NORM_EPS = 1e-6
ADAM_LR, ADAM_B1, ADAM_B2, ADAM_EPS, ADAM_WD, ADAM_STEP = 0.001, 0.9, 0.999, 1e-08, 0.01, 10

TB_PROJ = 512
TB_PROJ_BWD = 512
TB_MIX = 512
TB_LRU = 256
TQ_ATT = 512
TQ_ATT_FWD = 1024
TK_ATT = 1024
VMEM_LIMIT = 56 * 1024 * 1024
VMEM_LIMIT_PROJ_BWD = 60 * 1024 * 1024


def _mm(a, b):
    return jnp.dot(a.astype(BF16), b.astype(BF16), preferred_element_type=F32)


def _mm_nt(a, b):
    return lax.dot_general(a.astype(BF16), b.astype(BF16), (((1,), (1,)), ((), ())), preferred_element_type=F32)


def _mm_tn(a, b):
    return lax.dot_general(a.astype(BF16), b.astype(BF16), (((0,), (0,)), ((), ())), preferred_element_type=F32)


def _rows(tb, w):
    return pl.BlockSpec((tb, w), lambda i: (i, 0))


def _const(shape):
    n = len(shape)
    return pl.BlockSpec(shape, lambda i: (0,) * n)


def _params(sem=("arbitrary",), vmem_limit=None):
    return pltpu.CompilerParams(dimension_semantics=sem, vmem_limit_bytes=vmem_limit or VMEM_LIMIT)


def _iota(shape, dim):
    return lax.broadcasted_iota(jnp.int32, shape, dim)


def _rope_tables(pos, inv_lane):
    ang = pos.astype(F32) * inv_lane
    lane = _iota(ang.shape, 1)
    cs, sn = jnp.cos(ang), jnp.sin(ang)
    return (jnp.where(lane < 64, 1.0, jnp.where(lane < 96, cs, 0.0)),
            jnp.where((lane >= 64) & (lane < 80), -sn, 0.0), jnp.where((lane >= 80) & (lane < 96), sn, 0.0))


def _rope(t, c, sa, sb):
    return t * c + pltpu.roll(t, 112, 1) * sa + pltpu.roll(t, 16, 1) * sb


def _rope_t(d, c, sa, sb):
    return d * c + pltpu.roll(d * sa, 16, 1) + pltpu.roll(d * sb, 112, 1)


def _rms(c, g):
    r = lax.rsqrt(jnp.mean(c * c, axis=-1, keepdims=True) + NORM_EPS)
    xh = c * r
    return xh * g, xh, r


def _mla_proj_fwd(x, win, gq, gkv, wuq, wukv_k, wukv_v, pos_col, inv_lane, exchange=None):
    s = x.shape[0]
    tb = min(TB_PROJ, s)

    def body(x_ref, win_ref, gq_ref, gkv_ref, wuq_ref, wk_ref, wv_ref, pos_ref, inv_ref,
             gate_ref, qm_ref, cq_ref, ckv_ref, q_ref, qt_ref, k_ref, v_ref, vt_ref, c_ref, sa_ref, sb_ref):
        z = _mm(x_ref[...], win_ref[...])
        gate_ref[...] = z[:, 0:1024]
        qm_ref[...] = z[:, 1024:1280].astype(BF16)
        cq = z[:, 1280:1664]
        ckv = z[:, 1664:1920]
        cq_ref[...] = cq
        ckv_ref[...] = ckv
        c, sa, sb = _rope_tables(pos_ref[...], inv_ref[...])
        c_ref[...], sa_ref[...], sb_ref[...] = c, sa, sb
        nq, _, _ = _rms(cq, gq_ref[...])
        nkv, _, _ = _rms(ckv, gkv_ref[...])
        qf = _mm(nq, wuq_ref[...])
        kf = _mm(nkv, wk_ref[...])
        vf = _mm(nkv, wv_ref[...])
        v_ref[...] = vf.astype(BF16)
        for j in range(N_TOK_HEADS // 2):
            sl = slice(HEAD_PAD * j, HEAD_PAD * (j + 1))
            vt_ref[sl, :] = vf[:, sl].T.astype(BF16)
        kr = _rope(z[:, 1920:2048], c, sa, sb)
        for h in range(N_TOK_HEADS):
            sl = slice(HEAD_PAD * h, HEAD_PAD * (h + 1))
            qh = _rope(qf[:, sl], c, sa, sb) * ATT_SCALE_LOG2
            q_ref[:, sl] = qh.astype(BF16)
            qt_ref[sl, :] = qh.T.astype(BF16)
            k_ref[:, sl] = (kf[:, sl] + kr).astype(BF16)

    outs = (jax.ShapeDtypeStruct((s, 1024), F32), jax.ShapeDtypeStruct((s, MEM_W), BF16),
            jax.ShapeDtypeStruct((s, Q_LORA), F32), jax.ShapeDtypeStruct((s, KV_LORA), F32),
            jax.ShapeDtypeStruct((s, QK_W), BF16), jax.ShapeDtypeStruct((QK_W, s), BF16),
            jax.ShapeDtypeStruct((s, QK_W), BF16),
            jax.ShapeDtypeStruct((s, TOK_W), BF16), jax.ShapeDtypeStruct((TOK_W, s), BF16),
            *[jax.ShapeDtypeStruct((s, HEAD_PAD), F32)] * 3)

    def cols(w):
        return pl.BlockSpec((w, tb), lambda i: (0, i))

    return _run(
        body, name="mla_proj_fwd", grid=(s // tb,), out_shape=outs,
        in_specs=[_rows(tb, 1024), _const((1024, 2048)), _const((1, Q_LORA)), _const((1, KV_LORA)),
                  _const((Q_LORA, QK_W)), _const((KV_LORA, QK_W)), _const((KV_LORA, TOK_W)),
                  _rows(tb, 1), _const((1, HEAD_PAD))],
        out_specs=(_rows(tb, 1024), _rows(tb, MEM_W), _rows(tb, Q_LORA), _rows(tb, KV_LORA),
                   _rows(tb, QK_W), cols(QK_W), _rows(tb, QK_W), _rows(tb, TOK_W), cols(TOK_W),
                   _rows(tb, HEAD_PAD), _rows(tb, HEAD_PAD), _rows(tb, HEAD_PAD)),
        args=(x, win, gq, gkv, wuq, wukv_k, wukv_v, pos_col, inv_lane), sem=("parallel",), exchange=exchange)


def _mla_proj_bwd(x, cq, ckv, dq, dkt, dvt, dgate, dqm, dres, win, gq, gkv, wuq, wukv_k, wukv_v, ctab, satab, sbtab):
    s = x.shape[0]
    tb = min(TB_PROJ_BWD, s)

    def body(x_ref, cq_ref, ckv_ref, dq_ref, dkt_ref, dvt_ref, dgate_ref, dqm_ref, dres_ref, win_ref, gq_ref, gkv_ref,
             wuq_ref, wk_ref, wv_ref, c_ref, sa_ref, sb_ref,
             dx_ref, dwin_ref, dwuq_ref, dwk_ref, dwv_ref, dgq_ref, dgkv_ref):
        @pl.when(pl.program_id(0) == 0)
        def _():
            for r in (dwin_ref, dwuq_ref, dwk_ref, dwv_ref, dgq_ref, dgkv_ref):
                r[...] = jnp.zeros_like(r)

        c, sa, sb = c_ref[...], sa_ref[...], sb_ref[...]
        lane = _iota((tb, HEAD_PAD), 1)
        gq, gkv = gq_ref[...], gkv_ref[...]
        nq, xhq, rq = _rms(cq_ref[...], gq)
        nkv, xhk, rk = _rms(ckv_ref[...], gkv)
        dkp = dkt_ref[...].T * math.log(2.0)
        dqs, dkr = [], jnp.zeros((tb, HEAD_PAD), F32)
        for h in range(N_TOK_HEADS):
            sl = slice(HEAD_PAD * h, HEAD_PAD * (h + 1))
            dqs.append(_rope_t(dq_ref[:, sl], c, sa, sb).astype(BF16))
            dkr = dkr + dkp[:, sl]
        dqf = jnp.concatenate(dqs, axis=1)
        dkr = jnp.where((lane >= 64) & (lane < 96), _rope_t(dkr, c, sa, sb), 0.0)
        dvb = dvt_ref[...].T.astype(BF16)
        dkb = dkp.astype(BF16)
        dnq = _mm_nt(dqf, wuq_ref[...])
        dwuq_ref[...] += _mm_tn(nq, dqf)
        dgq_ref[...] += jnp.sum(dnq * xhq, axis=0, keepdims=True)
        dxh = dnq * gq
        dcq = rq * (dxh - xhq * jnp.mean(dxh * xhq, axis=-1, keepdims=True))
        dnkv = _mm_nt(dkb, wk_ref[...]) + _mm_nt(dvb, wv_ref[...])
        nkvb = nkv.astype(BF16)
        dwk_ref[...] += _mm_tn(nkvb, dkb)
        dwv_ref[...] += _mm_tn(nkvb, dvb)
        dgkv_ref[...] += jnp.sum(dnkv * xhk, axis=0, keepdims=True)
        dxh = dnkv * gkv
        dckv = rk * (dxh - xhk * jnp.mean(dxh * xhk, axis=-1, keepdims=True))
        dz = jnp.concatenate([dgate_ref[...], dqm_ref[...], dcq, dckv, dkr], axis=1).astype(BF16)
        dx_ref[...] = _mm_nt(dz, win_ref[...]) + dres_ref[...]
        dwin_ref[...] += _mm_tn(x_ref[...], dz)

    outs = (jax.ShapeDtypeStruct((s, 1024), F32), jax.ShapeDtypeStruct((1024, 2048), F32),
            jax.ShapeDtypeStruct((Q_LORA, QK_W), F32), jax.ShapeDtypeStruct((KV_LORA, QK_W), F32),
            jax.ShapeDtypeStruct((KV_LORA, TOK_W), F32), jax.ShapeDtypeStruct((1, Q_LORA), F32),
            jax.ShapeDtypeStruct((1, KV_LORA), F32))
    return _run(
        body, name="mla_proj_bwd", grid=(s // tb,), out_shape=outs,
        in_specs=[_rows(tb, 1024), _rows(tb, Q_LORA), _rows(tb, KV_LORA), _rows(tb, QK_W),
                  pl.BlockSpec((QK_W, tb), lambda i: (0, i)), pl.BlockSpec((TOK_W, tb), lambda i: (0, i)),
                  _rows(tb, 1024), _rows(tb, MEM_W), _rows(tb, 1024),
                  _const((1024, 2048)), _const((1, Q_LORA)), _const((1, KV_LORA)),
                  _const((Q_LORA, QK_W)), _const((KV_LORA, QK_W)), _const((KV_LORA, TOK_W)),
                  _rows(tb, HEAD_PAD), _rows(tb, HEAD_PAD), _rows(tb, HEAD_PAD)],
        out_specs=(_rows(tb, 1024), _const((1024, 2048)), _const((Q_LORA, QK_W)), _const((KV_LORA, QK_W)),
                   _const((KV_LORA, TOK_W)), _const((1, Q_LORA)), _const((1, KV_LORA))),
        args=(x, cq, ckv, dq, dkt, dvt, dgate, dqm, dres, win, gq, gkv, wuq, wukv_k, wukv_v, ctab, satab, sbtab),
        sem=("arbitrary",), vmem_limit=VMEM_LIMIT_PROJ_BWD)


def _attn_fwd(q, k, vt, exchange=None):
    s = q.shape[0]
    tq = min(TQ_ATT_FWD, s)
    tk = min(TK_ATT, s)

    def body(q_ref, k_ref, vt_ref, o_ref, lse_ref):
        i = pl.program_id(1)
        nfull = (i * tq) // tk
        krow = _iota((tk, tq), 0)
        qpos = i * tq + _iota((tk, tq), 1)

        def head_tile(hh, st, carry, masked):
            hs = slice(HEAD_PAD * hh, HEAD_PAD * (hh + 1))
            m, l, acc = carry
            sc = _mm_nt(k_ref[pl.ds(st, tk), hs], q_ref[:, hs])
            if masked:
                sc = jnp.where(st + krow <= qpos, sc, -jnp.inf)
            m_new = jnp.maximum(m, jnp.max(sc, axis=0, keepdims=True))
            p = jnp.exp2(sc - m_new)
            a = jnp.exp2(m - m_new)
            l = a * l + jnp.sum(p, axis=0, keepdims=True)
            acc = a * acc + _mm(vt_ref[64 * hh:64 * (hh + 1), pl.ds(st, tk)], p)
            return m_new, l, acc

        def tile(j, carry, masked):
            st = pl.multiple_of(j * tk, tk)
            return tuple(head_tile(hh, st, carry[hh], masked) for hh in range(2))

        def init():
            return (jnp.full((1, tq), -jnp.inf, F32), jnp.zeros((1, tq), F32), jnp.zeros((64, tq), F32))

        carry = lax.fori_loop(0, nfull, functools.partial(tile, masked=False), (init(), init()))
        (ma, la, acca), (mb, lb, accb) = tile(nfull, carry, True)
        o_ref[...] = jnp.concatenate([acca / la, accb / lb], axis=0).T
        lse_ref[...] = jnp.concatenate([jnp.broadcast_to(ma + jnp.log2(la), (64, tq)),
                                        jnp.broadcast_to(mb + jnp.log2(lb), (64, tq))], axis=0).T

    shp = jax.ShapeDtypeStruct((s, TOK_W), F32)
    return _run(
        body, name="attn_fwd", grid=(N_TOK_HEADS // 2, s // tq), out_shape=(shp, shp),
        in_specs=[pl.BlockSpec((tq, 2 * HEAD_PAD), lambda j, i: (i, j)),
                  pl.BlockSpec((s, 2 * HEAD_PAD), lambda j, i: (0, j)),
                  pl.BlockSpec((HEAD_PAD, s), lambda j, i: (j, 0))],
        out_specs=(pl.BlockSpec((tq, HEAD_PAD), lambda j, i: (i, j)),) * 2,
        args=(q, k, vt), sem=("parallel", "arbitrary"), exchange=exchange)


def _attn_stats(o, do, lse_ref, dob_ref, dot_ref, st_ref):
    lane = _iota((o.shape[0], HEAD_PAD), 1)
    dob_ref[...] = do.astype(BF16)
    prod = do * o
    for j in range(N_TOK_HEADS // 2):
        sl = slice(HEAD_PAD * j, HEAD_PAD * (j + 1))
        dot_ref[sl, :] = do[:, sl].T.astype(BF16)
        pj = prod[:, sl]
        da = jnp.sum(jnp.where(lane < 64, pj, 0.0), axis=-1, keepdims=True)
        db = jnp.sum(jnp.where(lane >= 64, pj, 0.0), axis=-1, keepdims=True)
        la = lse_ref[:, HEAD_PAD * j:HEAD_PAD * j + 1]
        lb = lse_ref[:, HEAD_PAD * j + 64:HEAD_PAD * j + 65]
        st_ref[j] = jnp.where(lane == 0, la, jnp.where(lane == 1, lb, jnp.where(lane == 2, da,
                                                                                 jnp.where(lane == 3, db, 0.0))))


def _attn_bwd(q, qt, k, v, dob, dobt, stats, exchange=None):
    s = q.shape[0]
    t = min(TQ_ATT, s)
    nq = s // t

    def body(q_ref, qt_ref, do_ref, dot_ref, st_ref, k_ref, v_ref, dq_ref, dkt_ref, dvt_ref):
        i = pl.program_id(1)

        @pl.when(i == 0)
        def _():
            dkt_ref[...] = jnp.zeros_like(dkt_ref)
            dvt_ref[...] = jnp.zeros_like(dvt_ref)

        lane = _iota((t, HEAD_PAD), 1)
        qpos, kcol = _iota((t, t), 0), _iota((t, t), 1)
        do = do_ref[...]
        stats = st_ref[0]

        def head_tile(hh, ks, dq_acc, masked):
            hs = slice(HEAD_PAD * hh, HEAD_PAD * (hh + 1))
            qh = q_ref[:, hs]
            kh = k_ref[pl.ds(ks, t), hs]
            dom = jnp.where((lane < 64) if hh == 0 else (lane >= 64), do, jnp.zeros_like(do))
            lse = stats[:, hh:hh + 1]
            dlt = stats[:, 2 + hh:3 + hh]
            sc = _mm_nt(qh, kh)
            if masked:
                sc = jnp.where(kcol <= qpos, sc, -jnp.inf)
            p = jnp.exp2(sc - lse)
            dp = _mm_nt(dom, v_ref[pl.ds(ks, t), :])
            ds = (p * (dp - dlt)).astype(BF16)
            dvt_ref[64 * hh:64 * (hh + 1), pl.ds(ks, t)] += _mm(dot_ref[64 * hh:64 * (hh + 1), :], p)
            dkt_ref[HEAD_PAD * hh:HEAD_PAD * hh + 96, pl.ds(ks, t)] += _mm(qt_ref[HEAD_PAD * hh:HEAD_PAD * hh + 96, :], ds)
            return dq_acc + _mm(ds, kh)

        def tile(j, carry, masked):
            ks = pl.multiple_of(j * t, t)
            return tuple(head_tile(hh, ks, carry[hh], masked) for hh in range(2))

        zero = jnp.zeros((t, HEAD_PAD), F32)
        carry = lax.fori_loop(0, i, functools.partial(tile, masked=False), (zero, zero))
        dqa, dqb = tile(i, carry, True)
        dq_ref[...] = jnp.concatenate([dqa, dqb], axis=1) * ATT_SCALE

    return _run(
        body, name="attn_bwd", grid=(N_TOK_HEADS // 2, nq),
        out_shape=(jax.ShapeDtypeStruct((s, QK_W), F32), jax.ShapeDtypeStruct((QK_W, s), F32),
                   jax.ShapeDtypeStruct((TOK_W, s), F32)),
        in_specs=[pl.BlockSpec((t, 2 * HEAD_PAD), lambda j, i: (i, j)),
                  pl.BlockSpec((2 * HEAD_PAD, t), lambda j, i: (j, i)),
                  pl.BlockSpec((t, HEAD_PAD), lambda j, i: (i, j)),
                  pl.BlockSpec((HEAD_PAD, t), lambda j, i: (j, i)),
                  pl.BlockSpec((1, t, HEAD_PAD), lambda j, i: (j, i, 0)),
                  pl.BlockSpec((s, 2 * HEAD_PAD), lambda j, i: (0, j)),
                  pl.BlockSpec((s, HEAD_PAD), lambda j, i: (0, j))],
        out_specs=(pl.BlockSpec((t, 2 * HEAD_PAD), lambda j, i: (i, j)),
                   pl.BlockSpec((2 * HEAD_PAD, s), lambda j, i: (j, 0)),
                   pl.BlockSpec((HEAD_PAD, s), lambda j, i: (j, 0))),
        args=(q, qt, dob, dobt, stats, k, v), sem=("parallel", "arbitrary"), exchange=exchange)


def _mem_kv(mem, wmem):
    def body(m_ref, w_ref, o_ref):
        o_ref[0] = _mm(m_ref[...], w_ref[0]).astype(BF16)

    return pl.pallas_call(
        body, name="mem_kv", grid=(2,), out_shape=jax.ShapeDtypeStruct((2, MEM_LEN, 512), BF16),
        in_specs=[_const((MEM_LEN, 1024)), pl.BlockSpec((1, 1024, 512), lambda l: (l, 0, 0))],
        out_specs=pl.BlockSpec((1, MEM_LEN, 512), lambda l: (l, 0, 0)),
        compiler_params=_params(("parallel",)),
    )(mem, wmem)


def _mem_kv_bwd(mem, dmemkv):
    def body(m_ref, d_ref, o_ref):
        o_ref[...] = _mm_tn(m_ref[...], d_ref[...])

    return pl.pallas_call(
        body, name="mem_kv_bwd", grid=(1,), out_shape=jax.ShapeDtypeStruct((1024, 512), F32),
        in_specs=[_const((MEM_LEN, 1024)), _const((MEM_LEN, 512))], out_specs=_const((1024, 512)),
        compiler_params=_params(("arbitrary",)),
    )(mem, dmemkv)


def _head_mask(lane, sub):
    return (lane < 64) if sub == 0 else (lane >= 64)


def _mem_attn(qm, kv):
    tb = qm.shape[0]
    lane = _iota((tb, HEAD_PAD), 1)
    outs, ps = [], []
    for pp in range(2):
        qp = qm[:, HEAD_PAD * pp:HEAD_PAD * (pp + 1)]
        kp = kv[:, HEAD_PAD * pp:HEAD_PAD * (pp + 1)]
        vp = kv[:, MEM_W + HEAD_PAD * pp:MEM_W + HEAD_PAD * (pp + 1)]
        pair = None
        for sub in range(2):
            qh = jnp.where(_head_mask(lane, sub), qp, jnp.zeros_like(qp))
            sc = _mm_nt(qh, kp) * 0.125
            e = jnp.exp(sc - jnp.max(sc, axis=-1, keepdims=True))
            p = e / jnp.sum(e, axis=-1, keepdims=True)
            o = _mm(p, vp)
            ps.append(p)
            pair = o if sub == 0 else jnp.where(lane < 64, pair, o)
        outs.append(pair)
    return jnp.concatenate(outs, axis=1), ps


def _mem_attn_bwd(dmo, qm, kv, ps):
    tb = qm.shape[0]
    lane = _iota((tb, HEAD_PAD), 1)
    dqs, dks, dvs = [], [], []
    for pp in range(2):
        qp = qm[:, HEAD_PAD * pp:HEAD_PAD * (pp + 1)]
        kp = kv[:, HEAD_PAD * pp:HEAD_PAD * (pp + 1)]
        vp = kv[:, MEM_W + HEAD_PAD * pp:MEM_W + HEAD_PAD * (pp + 1)]
        dop = dmo[:, HEAD_PAD * pp:HEAD_PAD * (pp + 1)]
        dq_pair, dk_pair, dv_pair = None, None, None
        for sub in range(2):
            msk = _head_mask(lane, sub)
            p = ps[2 * pp + sub]
            qh = jnp.where(msk, qp, jnp.zeros_like(qp))
            doh = jnp.where(msk, dop, 0.0).astype(BF16)
            dv = _mm_tn(p, doh)
            dp = _mm_nt(doh, vp)
            ds = (p * (dp - jnp.sum(dp * p, axis=-1, keepdims=True)) * 0.125).astype(BF16)
            dq = _mm(ds, kp)
            dk = _mm_tn(ds, qh)
            if sub == 0:
                dq_pair, dk_pair, dv_pair = dq, dk, dv
            else:
                dq_pair = jnp.where(lane < 64, dq_pair, dq)
                dk_pair, dv_pair = dk_pair + dk, dv_pair + dv
        dqs.append(dq_pair)
        dks.append(dk_pair)
        dvs.append(dv_pair)
    return jnp.concatenate(dqs, axis=1), jnp.concatenate(dks + dvs, axis=1)


def _mix_core(tok, gate, qm, kv, wout, h_in, g, b):
    mem_out, ps = _mem_attn(qm, kv)
    cat = jnp.concatenate([tok, mem_out], axis=1)
    sg = jax.nn.sigmoid(gate)
    sl = gate * sg
    y = cat * sl
    r = ALPHA * h_in + _mm(y, wout)
    mu = jnp.mean(r, axis=-1, keepdims=True)
    xc = r - mu
    rstd = lax.rsqrt(jnp.mean(xc * xc, axis=-1, keepdims=True) + NORM_EPS)
    xh = xc * rstd
    return xh * g + b, (ps, cat, sg, sl, y, xh, rstd)


def _mix_fwd(tok, gate, qm, kv, wout, h_in, g, b):
    s = tok.shape[0]
    tb = min(TB_MIX, s)

    def body(tok_ref, gate_ref, qm_ref, kv_ref, w_ref, h_ref, g_ref, b_ref, o_ref):
        o_ref[...], _ = _mix_core(tok_ref[...], gate_ref[...], qm_ref[...], kv_ref[...], w_ref[...], h_ref[...],
                                  g_ref[...], b_ref[...])

    return pl.pallas_call(
        body, name="mix_fwd", grid=(s // tb,), out_shape=jax.ShapeDtypeStruct((s, 1024), F32),
        in_specs=[_rows(tb, TOK_W), _rows(tb, 1024), _rows(tb, MEM_W), _const((MEM_LEN, 512)), _const((1024, 1024)),
                  _rows(tb, 1024), _const((1, 1024)), _const((1, 1024))],
        out_specs=_rows(tb, 1024), compiler_params=_params(("parallel",)),
    )(tok, gate, qm, kv, wout, h_in, g, b)


def _mix_bwd(tok, gate, qm, kv, wout, h_in, g, b, up, from_loss, lse=None, exchange=None):
    s = tok.shape[0]
    tb = min(TB_MIX, s)
    n_in = 9 if lse is None else 10
    n_tok_out = 1 if lse is None else 3

    def body(*refs):
        tok_ref, gate_ref, qm_ref, kv_ref, w_ref, h_ref, g_ref, b_ref, up_ref = refs[:9]
        dres_ref, tok_out = refs[n_in], refs[n_in + 1:n_in + 1 + n_tok_out]
        dgate_ref, dqm_ref, dw_ref, dkv_ref, dg_ref, db_ref, loss_ref = refs[n_in + 1 + n_tok_out:]

        @pl.when(pl.program_id(0) == 0)
        def _():
            for r in (dw_ref, dkv_ref, dg_ref, db_ref, loss_ref):
                r[...] = jnp.zeros_like(r)

        gate, qm, kv, wout, g = gate_ref[...], qm_ref[...], kv_ref[...], w_ref[...], g_ref[...]
        h_out, (ps, cat, sg, sl, y, xh, rstd) = _mix_core(tok_ref[...], gate, qm, kv, wout, h_ref[...], g, b_ref[...])
        if from_loss:
            diff = h_out - up_ref[...]
            loss_ref[...] += 0.5 * jnp.sum(jnp.mean(diff * diff, axis=-1, keepdims=True), axis=0, keepdims=True)
            dh = diff * (1.0 / D_MODEL)
        else:
            dh = up_ref[...]
        dg_ref[...] += jnp.sum(dh * xh, axis=0, keepdims=True)
        db_ref[...] += jnp.sum(dh, axis=0, keepdims=True)
        dxh = dh * g
        dr = rstd * (dxh - jnp.mean(dxh, axis=-1, keepdims=True) - xh * jnp.mean(dxh * xh, axis=-1, keepdims=True))
        dres_ref[...] = ALPHA * dr
        drb = dr.astype(BF16)
        dy = _mm_nt(drb, wout)
        dw_ref[...] += _mm_tn(y, drb)
        dcat = dy * sl
        dgate_ref[...] = dy * cat * (sg * (1.0 + gate * (1.0 - sg)))
        if lse is None:
            tok_out[0][...] = dcat[:, :TOK_W]
        else:
            _attn_stats(tok_ref[...], dcat[:, :TOK_W], refs[9], *tok_out)
        dqm, dkv = _mem_attn_bwd(dcat[:, TOK_W:], qm, kv, ps)
        dqm_ref[...] = dqm
        dkv_ref[...] += dkv

    npair = N_TOK_HEADS // 2
    tok_shapes = [jax.ShapeDtypeStruct((s, TOK_W), F32)] if lse is None else [
        jax.ShapeDtypeStruct((s, TOK_W), BF16), jax.ShapeDtypeStruct((TOK_W, s), BF16),
        jax.ShapeDtypeStruct((npair, s, HEAD_PAD), F32)]
    tok_specs = [_rows(tb, TOK_W)] if lse is None else [
        _rows(tb, TOK_W), pl.BlockSpec((TOK_W, tb), lambda i: (0, i)),
        pl.BlockSpec((npair, tb, HEAD_PAD), lambda i: (0, i, 0))]
    outs = (jax.ShapeDtypeStruct((s, 1024), F32), *tok_shapes,
            jax.ShapeDtypeStruct((s, 1024), F32), jax.ShapeDtypeStruct((s, MEM_W), F32),
            jax.ShapeDtypeStruct((1024, 1024), F32), jax.ShapeDtypeStruct((MEM_LEN, 512), F32),
            jax.ShapeDtypeStruct((1, 1024), F32), jax.ShapeDtypeStruct((1, 1024), F32),
            jax.ShapeDtypeStruct((1, 1), F32))
    args = (tok, gate, qm, kv, wout, h_in, g, b, up) + (() if lse is None else (lse,))
    return _run(
        body, name="mix_bwd_loss" if from_loss else "mix_bwd", grid=(s // tb,), out_shape=outs,
        in_specs=[_rows(tb, TOK_W), _rows(tb, 1024), _rows(tb, MEM_W), _const((MEM_LEN, 512)), _const((1024, 1024)),
                  _rows(tb, 1024), _const((1, 1024)), _const((1, 1024)), _rows(tb, 1024)]
        + ([] if lse is None else [_rows(tb, TOK_W)]),
        out_specs=(_rows(tb, 1024), *tok_specs, _rows(tb, 1024), _rows(tb, MEM_W), _const((1024, 1024)),
                   _const((MEM_LEN, 512)), _const((1, 1024)), _const((1, 1024)), _const((1, 1))),
        args=args, sem=("arbitrary",), exchange=exchange)


def _shift_down(u, tail, k):
    if k == 0:
        return u
    r = pltpu.roll(u, k, 0)
    row8 = _iota((8, u.shape[1]), 0)
    head = jnp.where(row8 < k, pltpu.roll(tail, k, 0), r[:8])
    return jnp.concatenate([head, r[8:]], axis=0)


def _shift_up(d, head, k):
    if k == 0:
        return d
    n = d.shape[0]
    r = pltpu.roll(d, n - k, 0)
    row8 = _iota((8, d.shape[1]), 0)
    last = jnp.where(row8 >= 8 - k, pltpu.roll(head, 8 - k, 0), r[n - 8:])
    return jnp.concatenate([r[:n - 8], last], axis=0)


def _scan_down(a, b):
    n = a.shape[0]
    row = _iota(a.shape, 0)
    s = 1
    while s < n:
        ok = row >= s
        a_s = jnp.where(ok, pltpu.roll(a, s, 0), 1.0)
        b_s = jnp.where(ok, pltpu.roll(b, s, 0), 0.0)
        b = a * b_s + b
        a = a * a_s
        s *= 2
    return a, b


def _scan_up(a, b):
    n = a.shape[0]
    row = _iota(a.shape, 0)
    s = 1
    while s < n:
        ok = row < n - s
        a_s = jnp.where(ok, pltpu.roll(a, n - s, 0), 1.0)
        b_s = jnp.where(ok, pltpu.roll(b, n - s, 0), 0.0)
        b = a * b_s + b
        a = a * a_s
        s *= 2
    return a, b


def _softplus(x):
    return jnp.maximum(x, 0.0) + jnp.log(1.0 + jnp.exp(-jnp.abs(x)))


def _lru_gates(u, tail, cw, cb, wr, br, wi, bi, lam):
    us = [_shift_down(u, tail, k) for k in range(4)]
    xc = cb + us[3] * cw[0:1] + us[2] * cw[1:2] + us[1] * cw[2:3] + us[0] * cw[3:4]
    xb = xc.astype(BF16)
    pre_r = jnp.concatenate([_mm(xb[:, 256 * g:256 * (g + 1)], wr[g]) for g in range(3)], axis=1) + br
    pre_i = jnp.concatenate([_mm(xb[:, 256 * g:256 * (g + 1)], wi[g]) for g in range(3)], axis=1) + bi
    rg, ig = jax.nn.sigmoid(pre_r), jax.nn.sigmoid(pre_i)
    clam = -LRU_C * _softplus(-lam)
    la = clam * rg
    a = jnp.exp(la)
    mm = jnp.sqrt(-jnp.tanh(la) * (a * a + 1.0))
    return us, xc, xb, rg, ig, clam, la, a, mm


def _lru_fwd(h, win, cw, cb, wr, br, wi, bi, lam):
    s = h.shape[0]
    tb = min(TB_LRU, s)

    def body(h_ref, win_ref, cw_ref, cb_ref, wr_ref, br_ref, wi_ref, bi_ref, lam_ref,
             u_ref, gate_ref, qm_ref, hs_ref, tail_sc, carry_sc):
        @pl.when(pl.program_id(0) == 0)
        def _():
            tail_sc[...] = jnp.zeros_like(tail_sc)
            carry_sc[...] = jnp.zeros_like(carry_sc)

        hb = h_ref[...].astype(BF16)
        z = jnp.concatenate([_mm(hb, win_ref[sh]) for sh in range(4)], axis=1)
        u = z[:, :TOK_W]
        u_ref[...] = u
        gate_ref[...] = z[:, TOK_W:TOK_W + 1024]
        qm_ref[...] = z[:, TOK_W + 1024:].astype(BF16)
        _, xc, _, _, ig, _, _, a, mm = _lru_gates(u, tail_sc[...], cw_ref[...], cb_ref[...], wr_ref[...], br_ref[...],
                                                 wi_ref[...], bi_ref[...], lam_ref[...])
        big_a, big_b = _scan_down(a, mm * (ig * xc))
        hs = big_a * carry_sc[0:1, :] + big_b
        hs_ref[...] = hs
        tail_sc[...] = u[tb - 8:, :]
        carry_sc[...] = jnp.broadcast_to(hs[tb - 1:tb, :], carry_sc.shape)

    outs = (jax.ShapeDtypeStruct((s, TOK_W), F32), jax.ShapeDtypeStruct((s, 1024), F32),
            jax.ShapeDtypeStruct((s, MEM_W), BF16), jax.ShapeDtypeStruct((s, TOK_W), F32))
    return pl.pallas_call(
        body, name="lru_fwd", grid=(s // tb,), out_shape=outs,
        in_specs=[_rows(tb, 1024), _const((4, 1024, 512)), _const((4, TOK_W)), _const((1, TOK_W)),
                  _const((3, 256, 256)), _const((1, TOK_W)), _const((3, 256, 256)), _const((1, TOK_W)),
                  _const((1, TOK_W))],
        out_specs=(_rows(tb, TOK_W), _rows(tb, 1024), _rows(tb, MEM_W), _rows(tb, TOK_W)),
        scratch_shapes=[pltpu.VMEM((8, TOK_W), F32), pltpu.VMEM((8, TOK_W), F32)],
        compiler_params=_params(),
    )(h, win, cw, cb, wr, br, wi, bi, lam)


def _lru_bwd(dhs, dgate, dqm, dres, h, u, hs, win, cw, cb, wr, br, wi, bi, lam):
    s = h.shape[0]
    tb = min(TB_LRU, s)
    nb = s // tb

    def rev(w):
        return pl.BlockSpec((tb, w), lambda i: (nb - 1 - i, 0))

    def prev_tail(w):
        return pl.BlockSpec((8, w), lambda i: (jnp.maximum((nb - 1 - i) * (tb // 8) - 1, 0), 0))

    def body(dhs_ref, dgate_ref, dqm_ref, dres_ref, h_ref, u_ref, hs_ref, ut_ref, hst_ref, win_ref, cw_ref, cb_ref,
             wr_ref, br_ref, wi_ref, bi_ref, lam_ref,
             dh_ref, dwin_ref, dcw_ref, dcb_ref, dwr_ref, dbr_ref, dwi_ref, dbi_ref, dlam_ref, ecar_sc, dxc_sc):
        i = pl.program_id(0)

        @pl.when(i == 0)
        def _():
            for r in (dwin_ref, dcw_ref, dcb_ref, dwr_ref, dbr_ref, dwi_ref, dbi_ref, dlam_ref, ecar_sc, dxc_sc):
                r[...] = jnp.zeros_like(r)

        first = (i == nb - 1)
        u = u_ref[...]
        utail = jnp.where(first, 0.0, ut_ref[...])
        hstail = jnp.where(first, 0.0, hst_ref[...])
        cw, wr, wi, lam = cw_ref[...], wr_ref[...], wi_ref[...], lam_ref[...]
        us, xc, xb, rg, ig, clam, la, a, mm = _lru_gates(u, utail, cw, cb_ref[...], wr, br_ref[...], wi, bi_ref[...], lam)
        row = _iota(a.shape, 0)
        a_next = jnp.where(row < tb - 1, pltpu.roll(a, tb - 1, 0), 1.0)
        big_a, big_b = _scan_up(a_next, dhs_ref[...])
        e = big_a * ecar_sc[0:1, :] + big_b
        ecar_sc[...] = jnp.broadcast_to(a[0:1, :] * e[0:1, :], ecar_sc.shape)
        hs_prev = _shift_down(hs_ref[...], hstail, 1)
        da = e * hs_prev
        ix = ig * xc
        dmm = e * ix
        dix = e * mm
        dla = da * a - dmm * (a * a) / mm
        dlam_ref[...] += jnp.sum(dla * rg, axis=0, keepdims=True)
        dpr = (dla * clam) * rg * (1.0 - rg)
        dpi = (dix * xc) * ig * (1.0 - ig)
        dbr_ref[...] += jnp.sum(dpr, axis=0, keepdims=True)
        dbi_ref[...] += jnp.sum(dpi, axis=0, keepdims=True)
        dprb, dpib = dpr.astype(BF16), dpi.astype(BF16)
        dxc_g = []
        for g in range(3):
            sl = slice(256 * g, 256 * (g + 1))
            dwr_ref[g] += _mm_tn(xb[:, sl], dprb[:, sl])
            dwi_ref[g] += _mm_tn(xb[:, sl], dpib[:, sl])
            dxc_g.append(_mm_nt(dprb[:, sl], wr[g]) + _mm_nt(dpib[:, sl], wi[g]))
        dxc = dix * ig + jnp.concatenate(dxc_g, axis=1)
        dcb_ref[...] += jnp.sum(dxc, axis=0, keepdims=True)
        dcw_ref[...] += jnp.concatenate([jnp.sum(dxc * us[3 - tap], axis=0, keepdims=True) for tap in range(4)], axis=0)
        head = dxc_sc[...]
        du = dxc * cw[3:4]
        for k in range(1, 4):
            du = du + _shift_up(dxc, head, k) * cw[3 - k:4 - k]
        dxc_sc[...] = dxc[:8, :]
        dz = jnp.concatenate([du, dgate_ref[...], dqm_ref[...]], axis=1).astype(BF16)
        hb = h_ref[...].astype(BF16)
        dh = dres_ref[...]
        for sh in range(4):
            dzs = dz[:, 512 * sh:512 * (sh + 1)]
            dh = dh + _mm_nt(dzs, win_ref[sh])
            dwin_ref[sh] += _mm_tn(hb, dzs)
        dh_ref[...] = dh

        @pl.when(i == nb - 1)
        def _():
            dlam_ref[...] = dlam_ref[...] * (LRU_C * jax.nn.sigmoid(-lam))

    outs = (jax.ShapeDtypeStruct((s, 1024), F32), jax.ShapeDtypeStruct((4, 1024, 512), F32),
            jax.ShapeDtypeStruct((4, TOK_W), F32), jax.ShapeDtypeStruct((1, TOK_W), F32),
            jax.ShapeDtypeStruct((3, 256, 256), F32), jax.ShapeDtypeStruct((1, TOK_W), F32),
            jax.ShapeDtypeStruct((3, 256, 256), F32), jax.ShapeDtypeStruct((1, TOK_W), F32),
            jax.ShapeDtypeStruct((1, TOK_W), F32))
    return pl.pallas_call(
        body, name="lru_bwd", grid=(nb,), out_shape=outs,
        in_specs=[rev(TOK_W), rev(1024), rev(MEM_W), rev(1024), rev(1024), rev(TOK_W), rev(TOK_W),
                  prev_tail(TOK_W), prev_tail(TOK_W),
                  _const((4, 1024, 512)), _const((4, TOK_W)), _const((1, TOK_W)), _const((3, 256, 256)),
                  _const((1, TOK_W)), _const((3, 256, 256)), _const((1, TOK_W)), _const((1, TOK_W))],
        out_specs=(rev(1024), _const((4, 1024, 512)), _const((4, TOK_W)), _const((1, TOK_W)), _const((3, 256, 256)),
                   _const((1, TOK_W)), _const((3, 256, 256)), _const((1, TOK_W)), _const((1, TOK_W))),
        scratch_shapes=[pltpu.VMEM((8, TOK_W), F32), pltpu.VMEM((8, TOK_W), F32)],
        compiler_params=_params(),
    )(dhs, dgate, dqm, dres, h, u, hs, u, hs, win, cw, cb, wr, br, wi, bi, lam)


def _adamw_update(w_ref, g_ref, m_ref, v_ref, d_ref, nm_ref, nv_ref):
    g = g_ref[...]
    nm = ADAM_B1 * m_ref[...] + (1.0 - ADAM_B1) * g
    nv = ADAM_B2 * v_ref[...] + (1.0 - ADAM_B2) * (g * g)
    m_hat = nm / (1.0 - ADAM_B1 ** ADAM_STEP)
    v_hat = nv / (1.0 - ADAM_B2 ** ADAM_STEP)
    d_ref[...] = -ADAM_LR * (m_hat / (jnp.sqrt(v_hat) + ADAM_EPS) + ADAM_WD * w_ref[...])
    nm_ref[...] = nm
    nv_ref[...] = nv


def _adamw(name, w, g, m, v):
    rows, cols = w.shape
    tb = 256 if rows % 256 == 0 else rows

    def body(*refs):
        _adamw_update(*refs)

    shp = jax.ShapeDtypeStruct((rows, cols), F32)
    return pl.pallas_call(
        body, name="adamw_" + name, grid=(rows // tb,), out_shape=(shp, shp, shp),
        in_specs=[_rows(tb, cols)] * 4, out_specs=(_rows(tb, cols),) * 3,
        compiler_params=_params(("parallel",)),
    )(w, g, m, v)


def _adamw_small(items):
    n = len(items)

    def body(*refs):
        for k in range(n):
            _adamw_update(*refs[4 * k:4 * k + 4], *refs[4 * n + 3 * k:4 * n + 3 * k + 3])

    args = [a for it in items for a in it]
    shapes = [jax.ShapeDtypeStruct(it[0].shape, F32) for it in items for _ in range(3)]
    outs = pl.pallas_call(
        body, name="adamw_small", grid=(1,), out_shape=tuple(shapes),
        in_specs=[_const(a.shape) for a in args], out_specs=tuple(_const(sh.shape) for sh in shapes),
        compiler_params=_params(),
    )(*args)
    return [outs[3 * k:3 * k + 3] for k in range(n)]


def _row_block(rows, cap=2048):
    return max(t for t in range(8, cap + 1, 8) if rows % t == 0)


def _add2(a, b):
    rows = a.shape[0]
    tb = _row_block(rows)

    def body(a_ref, b_ref, o_ref):
        o_ref[...] = a_ref[...] + b_ref[...]

    return pl.pallas_call(
        body, name="add_sibling", grid=(rows // tb,), out_shape=jax.ShapeDtypeStruct(a.shape, F32),
        in_specs=[_rows(tb, 128)] * 2, out_specs=_rows(tb, 128), compiler_params=_params(("parallel",)),
    )(a, b)


def _sum_slots(landed, own, rows):
    tb = _row_block(rows, 1024)

    def body(l_ref, o_ref, out_ref):
        t = 2 * lax.axis_index("x") + lax.axis_index("y")
        r = [jnp.where(t == s, o_ref[s], l_ref[s].astype(F32)) for s in range(4)]
        out_ref[...] = ((r[0] + r[1]) + r[2]) + r[3]

    return pl.pallas_call(
        body, name="sum_chips", grid=(rows // tb,), out_shape=jax.ShapeDtypeStruct((rows, 128), F32),
        in_specs=[pl.BlockSpec((4, tb, 128), lambda i: (0, i, 0))] * 2, out_specs=_rows(tb, 128),
        compiler_params=_params(("parallel",)),
    )(landed, own)


_ANY = pl.BlockSpec(memory_space=pl.ANY)


def _place():
    x, y, c = lax.axis_index("x"), lax.axis_index("y"), lax.axis_index("c")
    return x, y, c, [(1 - x, y), (x, 1 - y), (1 - x, 1 - y)]


def _remote(src, dst, ssem, rsem, to):
    return pltpu.make_async_remote_copy(src_ref=src, dst_ref=dst, send_sem=ssem, recv_sem=rsem, device_id=to,
                                        device_id_type=MESH)


class _Exchange:
    def __init__(self, ins, out_shape, sems, start, finish):
        self.ins, self.out_shape, self.sems, self.start, self.finish = ins, out_shape, sems, start, finish


def _run(body, *, name, grid, in_specs, out_specs, out_shape, args, scratch=(), sem, exchange=None,
         vmem_limit=VMEM_LIMIT):
    if exchange is None:
        return pl.pallas_call(body, name=name, grid=grid, out_shape=tuple(out_shape), in_specs=list(in_specs),
                              out_specs=tuple(out_specs), scratch_shapes=list(scratch),
                              compiler_params=_params(sem, vmem_limit))(*args)
    n_in, n_out, n_sc = len(args), len(out_shape), len(scratch)
    k_in, k_out = len(exchange.ins), len(exchange.out_shape)

    def fused(*refs):
        ins, refs = refs[:n_in], refs[n_in:]
        xin, refs = refs[:k_in], refs[k_in:]
        outs, refs = refs[:n_out], refs[n_out:]
        xout, refs = refs[:k_out], refs[k_out:]
        sc, xsem = refs[:n_sc], refs[n_sc:]
        first = pl.program_id(0) == 0
        last = pl.program_id(0) == grid[0] - 1
        for a in range(1, len(grid)):
            first = first & (pl.program_id(a) == 0)
            last = last & (pl.program_id(a) == grid[a] - 1)

        @pl.when(first)
        def _():
            exchange.start(xin, xout, xsem)

        body(*ins, *outs, *sc)

        @pl.when(last)
        def _():
            exchange.finish(xin, xout, xsem)

    return pl.pallas_call(
        fused, name=name, grid=grid, out_shape=(*out_shape, *exchange.out_shape),
        in_specs=[*in_specs, *[_ANY] * k_in], out_specs=(*out_specs, *[_ANY] * k_out),
        scratch_shapes=[*scratch, *exchange.sems],
        compiler_params=_params(("arbitrary",) * len(grid), vmem_limit),
    )(*args, *exchange.ins)


def _run_exchange(exchange, name):
    def body(*refs):
        k_in, k_out = len(exchange.ins), len(exchange.out_shape)
        xin, xout, xsem = refs[:k_in], refs[k_in:k_in + k_out], refs[k_in + k_out:]
        exchange.start(xin, xout, xsem)
        exchange.finish(xin, xout, xsem)

    return pl.pallas_call(
        body, name=name, out_shape=tuple(exchange.out_shape), in_specs=[_ANY] * len(exchange.ins),
        out_specs=tuple([_ANY] * len(exchange.out_shape)), scratch_shapes=list(exchange.sems),
    )(*exchange.ins)


def _gather_shards(wsh):
    _, hh, _ = wsh.shape

    def first_hop(w_ref, out_ref, ssems, rsems):
        x, y, c, chips = _place()
        t = 2 * x + y
        return [_remote(w_ref.at[c], out_ref.at[t, c], ssems.at[j], rsems.at[j], (cx, cy, c))
                for j, (cx, cy) in enumerate(chips)]

    def start(xin, xout, xsem):
        for cp in first_hop(xin[0], xout[0], *xsem):
            cp.start()

    def finish(xin, xout, xsem):
        out_ref, (ssems, rsems) = xout[0], xsem
        first = first_hop(xin[0], out_ref, *xsem)
        x, y, c, chips = _place()
        passed = []
        for j, (cx, cy) in enumerate(chips):
            got = out_ref.at[2 * cx + cy, c]
            _remote(got, got, ssems.at[j], rsems.at[j], (cx, cy, c)).wait_recv()
            cp = _remote(got, got, ssems.at[3 + j], rsems.at[3 + j], (x, y, 1 - c))
            cp.start()
            passed.append(cp)
        for j, (cx, cy) in enumerate(chips):
            got = out_ref.at[2 * cx + cy, 1 - c]
            _remote(got, got, ssems.at[3 + j], rsems.at[3 + j], (x, y, 1 - c)).wait_recv()
        for cp in first + passed:
            cp.wait_send()

    return _Exchange([wsh], [jax.ShapeDtypeStruct((4, 2, hh, 128), wsh.dtype)],
                     [pltpu.SemaphoreType.DMA((6,)), pltpu.SemaphoreType.DMA((6,))], start, finish)


def _gathered(landed, own):
    t = 2 * lax.axis_index("x") + lax.axis_index("y")
    return lax.dynamic_update_slice(landed, own[None], (t, 0, 0, 0))


def _swap_sibling(sh, rp):
    hs, rr = sh.shape[1] // 2, rp.shape[2]

    def copies(xin, xout, xsem):
        x, y, c, _ = _place()
        sib = (x, y, 1 - c)
        rows = xin[0].at[:, pl.ds(pl.multiple_of((1 - c) * hs, 8), hs)]
        return [_remote(rows, xout[0].at[:, pl.ds(0, hs)], xsem[0].at[0], xsem[1].at[0], sib),
                _remote(xin[1].at[:, 1 - c], xout[0].at[:, pl.ds(hs, rr)], xsem[0].at[1], xsem[1].at[1], sib)]

    def start(*a):
        for cp in copies(*a):
            cp.start()

    def finish(*a):
        for cp in copies(*a):
            cp.wait()

    return _Exchange([sh, rp], [jax.ShapeDtypeStruct((4, hs + rr, 128), F32)],
                     [pltpu.SemaphoreType.DMA((2,)), pltpu.SemaphoreType.DMA((2,))], start, finish)


def _scatter_chips(parts):
    n = len(parts)

    def copies(xin, xout, ssems, rsems):
        x, y, c, chips = _place()
        t = 2 * x + y
        return [_remote(xin[k].at[2 * cx + cy], xout[k].at[t], ssems.at[n * j + k], rsems.at[n * j + k], (cx, cy, c))
                for j, (cx, cy) in enumerate(chips) for k in range(n)]

    def start(xin, xout, xsem):
        for cp in copies(xin, xout, *xsem):
            cp.start()

    def finish(xin, xout, xsem):
        ssems, rsems = xsem
        x, y, c, chips = _place()
        for j, (cx, cy) in enumerate(chips):
            for k in range(n):
                got = xout[k].at[2 * cx + cy]
                _remote(got, got, ssems.at[n * j + k], rsems.at[n * j + k], (cx, cy, c)).wait_recv()
        for cp in copies(xin, xout, *xsem):
            cp.wait_send()

    return _Exchange(parts, [jax.ShapeDtypeStruct(a.shape, a.dtype) for a in parts],
                     [pltpu.SemaphoreType.DMA((3 * n,)), pltpu.SemaphoreType.DMA((3 * n,))], start, finish)


def _share_reduced(piece, eighth):
    def copies(t_ref, mine_r, sib_ref, rall_ref, ssems, rsems, lsem):
        x, y, c, _ = _place()
        me = 4 * x + 2 * y + c
        loc = pltpu.make_async_copy(mine_r, rall_ref.at[me], lsem)
        sends = [_remote(t_ref, sib_ref, ssems.at[0], rsems.at[0], (x, y, 1 - c))]
        peers = []
        for mask in range(1, 8):
            px = 1 - x if mask & 4 else x
            py = 1 - y if mask & 2 else y
            pc = 1 - c if mask & 1 else c
            peers.append((mask, px, py, pc))
            sends.append(_remote(mine_r, rall_ref.at[me], ssems.at[mask], rsems.at[mask], (px, py, pc)))
        return loc, sends, peers

    def start(xin, xout, xsem):
        loc, sends, _ = copies(*xin, *xout, *xsem)
        for cp in [loc] + sends:
            cp.start()

    def finish(xin, xout, xsem):
        (sib_ref, rall_ref), (ssems, rsems, _) = xout, xsem
        loc, sends, peers = copies(*xin, *xout, *xsem)
        x, y, c, _ = _place()
        _remote(sib_ref, sib_ref, ssems.at[0], rsems.at[0], (x, y, 1 - c)).wait_recv()
        for mask, px, py, pc in peers:
            got = rall_ref.at[4 * px + 2 * py + pc]
            _remote(got, got, ssems.at[mask], rsems.at[mask], (px, py, pc)).wait_recv()
        for cp in sends:
            cp.wait_send()
        loc.wait()

    return _Exchange([piece, eighth],
                     [jax.ShapeDtypeStruct(piece.shape, F32), jax.ShapeDtypeStruct((8, *eighth.shape), F32)],
                     [pltpu.SemaphoreType.DMA((8,)), pltpu.SemaphoreType.DMA((8,)), pltpu.SemaphoreType.DMA],
                     start, finish)


def _ceil_to(n, m):
    return -(-n // m) * m


def _pack_bf16(parts):
    blocks = [p.reshape(-1, 128) for p in parts]
    rows = jnp.concatenate([jnp.pad(b, ((0, -b.shape[0] % 16), (0, 0))) for b in blocks])
    hw = _ceil_to(rows.shape[0], 32) // 2
    return jnp.pad(rows, ((0, 2 * hw - rows.shape[0]), (0, 0))).reshape(2, hw, 128)


def _segments(wall, parts):
    wall = wall.reshape(4, -1, 128)
    out, row = [], 0
    for p in parts:
        n = p.size // 128
        out.append(wall[:, row:row + n].reshape(4, *p.shape))
        row += _ceil_to(n, 16)
    return out


class _GradReduce:
    def __init__(self, sharded, replicated, c_idx, wire_bf16=False):
        self.c_idx, self.wire_bf16 = c_idx, wire_bf16
        self.rowwise = [(n, g.shape[1:]) for n, g in sharded if math.prod(g.shape[1:]) % 128 == 0]
        self.small = [(n, g.shape[1:]) for n, g in sharded if math.prod(g.shape[1:]) % 128 != 0]
        self.replicated = [(n, g.shape[0]) for n, g in replicated]
        by_name = dict(sharded)
        blocks = [by_name[n].reshape(4, -1, 128) for n, _ in self.rowwise]
        if self.small:
            rest = jnp.concatenate([by_name[n].reshape(4, -1) for n, _ in self.small], axis=1)
            blocks.append(jnp.pad(rest, ((0, 0), (0, -rest.shape[1] % 128))).reshape(4, -1, 128))
        blocks = [jnp.pad(b, ((0, 0), (0, -b.shape[1] % 8), (0, 0))) for b in blocks]
        rows = sum(b.shape[1] for b in blocks)
        self.hs = _ceil_to(rows, 256) // 2
        sh = jnp.concatenate(blocks + [jnp.zeros((4, 2 * self.hs - rows, 128), F32)], axis=1)
        rp = jnp.concatenate([g for _, g in replicated])
        self.rr = _ceil_to(_ceil_to(rp.shape[0], 128) // 128, 64) // 8
        rp = jnp.pad(rp, (0, 8 * self.rr * 128 - rp.shape[0])).reshape(4, 2, self.rr, 128)
        self.hh = self.hs + self.rr

        self.sh, self.rp = sh, rp
        self.mine = jnp.concatenate([lax.dynamic_slice_in_dim(sh, c_idx * self.hs, self.hs, axis=1),
                                     lax.dynamic_index_in_dim(rp, c_idx, axis=1, keepdims=False)],
                                    axis=1).reshape(4 * self.hh, 128)

    def swap(self):
        return _swap_sibling(self.sh, self.rp)

    def swapped(self, got):
        self.chip_sum = _add2(self.mine, got.reshape(4 * self.hh, 128)).reshape(4, self.hh, 128)

    def scatter(self):
        self.own_r = self.chip_sum[:, self.hs:]
        if self.wire_bf16:
            return _scatter_chips([self.chip_sum[:, :self.hs].astype(BF16), self.own_r])
        return _scatter_chips([self.chip_sum])

    def scattered(self, landed, landed_r=None):
        if landed_r is None:
            landed_r = landed[:, self.hs:]
        self.piece = _sum_slots(landed, self.chip_sum, self.hs)
        self.eighth = _sum_slots(landed_r, self.own_r, self.rr)

    def share(self):
        return _share_reduced(self.piece, self.eighth)

    def shared(self, sibling, rall):
        mine, sib = self.piece, sibling
        self.shard = jnp.where(self.c_idx == 0, jnp.concatenate([mine, sib]), jnp.concatenate([sib, mine]))
        self.rall = rall

    def reduced(self):
        out, row = {}, 0
        for name, shape in self.rowwise:
            rows = math.prod(shape) // 128
            out[name] = self.shard[row:row + rows].reshape(shape)
            row += _ceil_to(rows, 8)
        for group, flat in ((self.small, self.shard[row:].reshape(-1)), (self.replicated, self.rall.reshape(-1))):
            off = 0
            for name, shape in group:
                n = math.prod(shape) if isinstance(shape, tuple) else shape
                out[name] = flat[off:off + n]
                off += n
        return out


def _col_shards(w2d):
    rows, cols = w2d.shape
    return w2d.reshape(rows, 4, cols // 4).transpose(1, 0, 2)


_WIN0_PARTS = ((0, 384, 1280), (384, 640, 1664), (640, 672, 1984), (672, 1696, 0), (1696, 1952, 1024))


def _win0_aligned(shards):
    def cols(a, b):
        return [shards[s][:, max(a, 488 * s) - 488 * s:min(b, 488 * (s + 1)) - 488 * s]
                for s in range(4) if max(a, 488 * s) < min(b, 488 * (s + 1))]

    zeros = jnp.zeros((1024, 64), shards.dtype)
    return jnp.concatenate(cols(672, 1696) + cols(1696, 1952) + cols(0, 384) + cols(384, 640)
                           + [zeros] + cols(640, 672) + [zeros[:, :32]], axis=1)


def _win0_shards(dwin0p):
    shards = []
    for s in range(4):
        lo, hi = 488 * s, 488 * (s + 1)
        cols = [dwin0p[:, p + max(lo, a) - a:p + min(hi, b) - a] for a, b, p in _WIN0_PARTS if max(lo, a) < min(hi, b)]
        shards.append(jnp.concatenate(cols, axis=1))
    return jnp.stack(shards)


def _block_diag4(w):
    eye = jnp.eye(4, dtype=w.dtype)
    return jnp.einsum("gaij,ab->gaibj", w.reshape(3, 4, 64, 64), eye).reshape(3, 256, 256)


def _diag_blocks4(w):
    w5 = w.reshape(3, 4, 64, 4, 64)
    return jnp.stack([w5[:, a, :, a, :] for a in range(4)], axis=1).reshape(12, 64, 64)


def kernel(x, mem, positions, mla_w_in, mla_q_norm, mla_w_uq, mla_kv_norm, mla_w_ukv, lru_w_in, lru_conv_w, lru_conv_b, lru_w_rgate, lru_b_rgate, lru_w_igate, lru_b_igate, lru_lambda, w_mem_kv, w_out, ln_g, ln_b, loss_target, m_mla_w_in, m_mla_q_norm, m_mla_w_uq, m_mla_kv_norm, m_mla_w_ukv, m_lru_w_in, m_lru_conv_w, m_lru_conv_b, m_lru_w_rgate, m_lru_b_rgate, m_lru_w_igate, m_lru_b_igate, m_lru_lambda, m_w_mem_kv, m_w_out, m_ln_g, m_ln_b, v_mla_w_in, v_mla_q_norm, v_mla_w_uq, v_mla_kv_norm, v_mla_w_ukv, v_lru_w_in, v_lru_conv_w, v_lru_conv_b, v_lru_w_rgate, v_lru_b_rgate, v_lru_w_igate, v_lru_b_igate, v_lru_lambda, v_w_mem_kv, v_w_out, v_ln_g, v_ln_b):
    s = x.shape[1]
    c_idx = lax.axis_index("c")
    x2, mem2, tgt2 = x[0], mem[0], loss_target[0]

    first = [p.astype(BF16) for p in (mla_w_in[0], mla_w_uq[0], mla_w_ukv[0])]
    buf = _pack_bf16(first)
    mla_shards = _segments(_gathered(_run_exchange(_gather_shards(buf), "gather_weights")[0], buf), first)

    mid = [w_mem_kv.astype(BF16), w_out[0].astype(BF16)]
    buf_mid = _pack_bf16(mid)

    def mid_weights(landed):
        wmem, wout0 = _segments(_gathered(landed[0], buf_mid), mid)
        return wmem.transpose(1, 0, 2, 3).reshape(2, 1024, 512), wout0.reshape(1024, 1024)

    small = jnp.concatenate([lru_conv_w[0].reshape(-1), lru_conv_b[0], lru_b_rgate[0], lru_b_igate[0], lru_lambda[0]])
    late = [lru_w_in[0].astype(BF16), w_out[1].astype(BF16), lax.bitcast_convert_type(small, BF16)]
    buf_late = _pack_bf16(late)

    def late_weights(landed):
        win1, wout1, small_bits = _segments(_gathered(landed[0], buf_late), late)
        small_all = lax.bitcast_convert_type(small_bits, F32)
        cw = small_all[:, :768].reshape(4, 4, 192).transpose(1, 0, 2).reshape(4, TOK_W)
        cb, br, bi, lam = (small_all[:, 768 + 192 * k:960 + 192 * k].reshape(1, TOK_W) for k in range(4))
        return win1, wout1.reshape(1024, 1024), cw, cb, br, bi, lam

    def reduce_late(g):
        return _GradReduce(
            [("lru_w_in", g["lru_w_in"]), ("lru_conv_w", _col_shards(g["lru_conv_w"])),
             ("lru_conv_b", _col_shards(g["lru_conv_b"])), ("lru_b_rgate", _col_shards(g["lru_b_rgate"])),
             ("lru_b_igate", _col_shards(g["lru_b_igate"])), ("lru_lambda", _col_shards(g["lru_lambda"])),
             ("w_mem_kv1", g["w_mem_kv1"].reshape(4, 256, 512)), ("w_out1", g["w_out1"].reshape(4, 256, 1024))],
            [("lru_w_rgate", g["lru_w_rgate"].reshape(-1)), ("lru_w_igate", g["lru_w_igate"].reshape(-1)),
             ("ln_g1", g["ln_g1"].reshape(-1)), ("ln_b1", g["ln_b1"].reshape(-1))], c_idx)

    g0, late_red = _local_step(
        x2, mem2, positions.reshape(s, 1), tgt2, *mla_shards, mla_q_norm, mla_kv_norm, lru_w_rgate[0], lru_w_igate[0],
        ln_g, ln_b, mid_weights, late_weights, _gather_shards(buf_mid), _gather_shards(buf_late), reduce_late)

    early_red = _GradReduce(
        [("mla_w_in", g0["mla_w_in"]), ("mla_w_uq", _col_shards(g0["mla_w_uq"])),
         ("mla_w_ukv", _col_shards(g0["mla_w_ukv"])), ("w_mem_kv0", g0["w_mem_kv0"].reshape(4, 256, 512)),
         ("w_out0", g0["w_out0"].reshape(4, 256, 1024))],
        [("mla_q_norm", g0["mla_q_norm"].reshape(-1)), ("mla_kv_norm", g0["mla_kv_norm"].reshape(-1)),
         ("ln_g0", g0["ln_g0"].reshape(-1)), ("ln_b0", g0["ln_b0"].reshape(-1)), ("loss", g0["loss"].reshape(-1))],
        c_idx, wire_bf16=True)
    early_red.swapped(*_run_exchange(early_red.swap(), "swap_sibling"))
    early_red.scattered(*_run_exchange(early_red.scatter(), "scatter_chips"))
    early_red.shared(*_run_exchange(early_red.share(), "share_reduced"))
    red = {**late_red.reduced(), **early_red.reduced()}
    red["w_mem_kv"] = jnp.concatenate([red["w_mem_kv0"], red["w_mem_kv1"]])
    red["w_out"] = jnp.concatenate([red["w_out0"], red["w_out1"]])
    red["ln_g"] = jnp.concatenate([red["ln_g0"], red["ln_g1"]])
    red["ln_b"] = jnp.concatenate([red["ln_b0"], red["ln_b1"]])

    weights = dict(mla_w_in=mla_w_in, mla_q_norm=mla_q_norm, mla_w_uq=mla_w_uq, mla_kv_norm=mla_kv_norm,
                   mla_w_ukv=mla_w_ukv, lru_w_in=lru_w_in, lru_conv_w=lru_conv_w, lru_conv_b=lru_conv_b,
                   lru_w_rgate=lru_w_rgate, lru_b_rgate=lru_b_rgate, lru_w_igate=lru_w_igate, lru_b_igate=lru_b_igate,
                   lru_lambda=lru_lambda, w_mem_kv=w_mem_kv, w_out=w_out, ln_g=ln_g, ln_b=ln_b)
    m_in = dict(mla_w_in=m_mla_w_in, mla_q_norm=m_mla_q_norm, mla_w_uq=m_mla_w_uq, mla_kv_norm=m_mla_kv_norm,
                mla_w_ukv=m_mla_w_ukv, lru_w_in=m_lru_w_in, lru_conv_w=m_lru_conv_w, lru_conv_b=m_lru_conv_b,
                lru_w_rgate=m_lru_w_rgate, lru_b_rgate=m_lru_b_rgate, lru_w_igate=m_lru_w_igate,
                lru_b_igate=m_lru_b_igate, lru_lambda=m_lru_lambda, w_mem_kv=m_w_mem_kv, w_out=m_w_out, ln_g=m_ln_g,
                ln_b=m_ln_b)
    v_in = dict(mla_w_in=v_mla_w_in, mla_q_norm=v_mla_q_norm, mla_w_uq=v_mla_w_uq, mla_kv_norm=v_mla_kv_norm,
                mla_w_ukv=v_mla_w_ukv, lru_w_in=v_lru_w_in, lru_conv_w=v_lru_conv_w, lru_conv_b=v_lru_conv_b,
                lru_w_rgate=v_lru_w_rgate, lru_b_rgate=v_lru_b_rgate, lru_w_igate=v_lru_w_igate,
                lru_b_igate=v_lru_b_igate, lru_lambda=v_lru_lambda, w_mem_kv=v_w_mem_kv, w_out=v_w_out, ln_g=v_ln_g,
                ln_b=v_ln_b)
    order = ["mla_w_in", "mla_q_norm", "mla_w_uq", "mla_kv_norm", "mla_w_ukv", "lru_w_in", "lru_conv_w", "lru_conv_b",
             "lru_w_rgate", "lru_b_rgate", "lru_w_igate", "lru_b_igate", "lru_lambda", "w_mem_kv", "w_out", "ln_g",
             "ln_b"]
    grads, deltas, new_m, new_v = {}, {}, {}, {}

    def operands(name):
        shape = weights[name].shape
        two_d = (math.prod(shape[:-1]), shape[-1])
        return [a.reshape(two_d) for a in (weights[name], red[name], m_in[name], v_in[name])]

    def keep(name, g2, d2, m2, v2):
        shape = weights[name].shape
        grads[name], deltas[name] = g2.reshape(shape), d2.reshape(shape)
        new_m[name], new_v[name] = m2.reshape(shape), v2.reshape(shape)

    small = [n for n in order if weights[n].size <= 4096]
    ops = [operands(n) for n in small]
    for name, op, res in zip(small, ops, _adamw_small(ops)):
        keep(name, op[1], *res)
    for name in order:
        if name not in small:
            op = operands(name)
            keep(name, op[1], *_adamw(name, *op))
    return (red["loss"][0], g0["x"][None], *[grads[n] for n in order], *[deltas[n] for n in order],
            *[new_m[n] for n in order], *[new_v[n] for n in order])


def _local_step(x2, mem2, pos_col, tgt2, win0_sh, wuq_sh, wukv_sh, gq, gkv, w_rgate, w_igate, ln_g, ln_b,
                mid_weights, late_weights, gather_mid=None, gather_late=None, reduce_late=None):
    s = x2.shape[0]
    win0p = _win0_aligned(win0_sh)
    wuq_p = jnp.pad(wuq_sh.reshape(4, Q_LORA, 3, 96).transpose(1, 0, 2, 3).reshape(Q_LORA, 12, 96),
                    ((0, 0), (0, 0), (0, 32))).reshape(Q_LORA, QK_W)
    wukv3 = wukv_sh.reshape(4, KV_LORA, 3, 128).transpose(1, 0, 2, 3).reshape(KV_LORA, 12, 128)
    wk_p = jnp.pad(wukv3[:, :, :64], ((0, 0), (0, 0), (0, 64))).reshape(KV_LORA, QK_W)
    wv = wukv3[:, :, 64:].reshape(KV_LORA, TOK_W)
    wr_bd = _block_diag4(w_rgate).astype(BF16)
    wi_bd = _block_diag4(w_igate).astype(BF16)
    half = 16
    inv_freq = ROPE_THETA ** (-jnp.arange(half, dtype=F32) / half)
    inv_lane = jnp.concatenate([jnp.zeros((64,), F32), inv_freq, inv_freq, jnp.zeros((32,), F32)]).reshape(1, HEAD_PAD)

    gate0, qm0, cq, ckv, q_p, q_t, k_p, v_b, v_t, ctab, satab, sbtab, *landed = _mla_proj_fwd(
        x2, win0p, gq, gkv, wuq_p, wk_p, wv, pos_col, inv_lane, exchange=gather_mid)
    wmem, wout0 = mid_weights(landed)
    memkv = _mem_kv(mem2, wmem)
    tok0, lse, *landed = _attn_fwd(q_p, k_p, v_t, exchange=gather_late)
    win1, wout1, cw, cb, br, bi, lam = late_weights(landed)
    g0, b0, g1, b1 = ln_g[0:1], ln_b[0:1], ln_g[1:2], ln_b[1:2]
    h1 = _mix_fwd(tok0, gate0, qm0, memkv[0], wout0, x2, g0, b0)
    u1, gate1, qm1, hs1 = _lru_fwd(h1, win1, cw, cb, wr_bd, br, wi_bd, bi, lam)

    dres1, dtok1, dgate1, dqm1, dwout1, dmemkv1, dg1, db1, loss = _mix_bwd(
        hs1, gate1, qm1, memkv[1], wout1, h1, g1, b1, tgt2, True)
    dh1, dwin1, dcw, dcb, dwr_bd, dbr, dwi_bd, dbi, dlam = _lru_bwd(
        dtok1, dgate1, dqm1, dres1, h1, u1, hs1, win1, cw, cb, wr_bd, br, wi_bd, bi, lam)
    late = {"lru_w_in": dwin1, "lru_conv_w": dcw, "lru_conv_b": dcb, "lru_b_rgate": dbr, "lru_b_igate": dbi,
            "lru_lambda": dlam, "w_mem_kv1": _mem_kv_bwd(mem2, dmemkv1), "w_out1": dwout1,
            "lru_w_rgate": _diag_blocks4(dwr_bd), "lru_w_igate": _diag_blocks4(dwi_bd), "ln_g1": dg1, "ln_b1": db1}
    red = reduce_late(late) if reduce_late is not None else None

    dres0, dob, dobt, stats, dgate0, dqm0, dwout0, dmemkv0, dg0, db0, _, *got = _mix_bwd(
        tok0, gate0, qm0, memkv[0], wout0, x2, g0, b0, dh1, False, lse=lse, exchange=red.swap() if red else None)
    if red:
        red.swapped(*got)
    dq_p, dk_t, dv_t, *got = _attn_bwd(q_p, q_t, k_p, v_b, dob, dobt, stats,
                                       exchange=red.scatter() if red else None)
    if red:
        red.scattered(*got)
    if red:
        red.shared(*_run_exchange(red.share(), "share_reduced"))
    dx, dwin0p, dwuq_p, dwk_p, dwv, dgq, dgkv = _mla_proj_bwd(
        x2, cq, ckv, dq_p, dk_t, dv_t, dgate0, dqm0, dres0, win0p, gq, gkv, wuq_p, wk_p, wv,
        ctab, satab, sbtab)

    dwin0 = _win0_shards(dwin0p)
    dwuq = dwuq_p.reshape(Q_LORA, 12, 128)[:, :, :96].reshape(Q_LORA, 1152)
    dwukv = jnp.concatenate([dwk_p.reshape(KV_LORA, 12, 128)[:, :, :64], dwv.reshape(KV_LORA, 12, 64)],
                            axis=2).reshape(KV_LORA, 1536)
    early = {"x": dx, "loss": loss, "mla_w_in": dwin0, "mla_w_uq": dwuq, "mla_w_ukv": dwukv,
             "w_mem_kv0": _mem_kv_bwd(mem2, dmemkv0), "w_out0": dwout0, "mla_q_norm": dgq, "mla_kv_norm": dgkv,
             "ln_g0": dg0, "ln_b0": db0}
    return early, (red if red else late)
```

```python
import functools
import math

import jax
import jax.numpy as jnp
from jax import lax
from jax.experimental import pallas as pl
from jax.experimental.pallas import tpu as pltpu

F32, BF16 = jnp.float32, jnp.bfloat16
MESH = pl.DeviceIdType.MESH

D_MODEL = 1024
N_TOK_HEADS = 12
TOK_W = 768
MEM_W = 256
MEM_LEN = 256
Q_LORA, KV_LORA = 384, 256
HEAD_PAD = 128
QK_W = N_TOK_HEADS * HEAD_PAD
ATT_SCALE = 1.0 / math.sqrt(96.0)
ATT_SCALE_LOG2 = ATT_SCALE * math.log2(math.e)
ROPE_THETA = 10000.0
LRU_C = 8.0
ALPHA = 4.0 ** 0.25
NORM_EPS = 1e-6
ADAM_LR, ADAM_B1, ADAM_B2, ADAM_EPS, ADAM_WD, ADAM_STEP = 0.001, 0.9, 0.999, 1e-08, 0.01, 10

TB_PROJ = 512
TB_PROJ_BWD = 512
TB_MIX = 512
TB_LRU = 256
TQ_ATT = 512
TQ_ATT_FWD = 1024
TK_ATT = 1024
VMEM_LIMIT = 56 * 1024 * 1024
VMEM_LIMIT_PROJ_BWD = 60 * 1024 * 1024


def _mm(a, b):
    return jnp.dot(a.astype(BF16), b.astype(BF16), preferred_element_type=F32)


def _mm_nt(a, b):
    return lax.dot_general(a.astype(BF16), b.astype(BF16), (((1,), (1,)), ((), ())), preferred_element_type=F32)


def _mm_tn(a, b):
    return lax.dot_general(a.astype(BF16), b.astype(BF16), (((0,), (0,)), ((), ())), preferred_element_type=F32)


def _rows(tb, w):
    return pl.BlockSpec((tb, w), lambda i: (i, 0))


def _const(shape):
    n = len(shape)
    return pl.BlockSpec(shape, lambda i: (0,) * n)


def _params(sem=("arbitrary",), vmem_limit=None):
    return pltpu.CompilerParams(dimension_semantics=sem, vmem_limit_bytes=vmem_limit or VMEM_LIMIT)


def _iota(shape, dim):
    return lax.broadcasted_iota(jnp.int32, shape, dim)


def _rope_tables(pos, inv_lane):
    ang = pos.astype(F32) * inv_lane
    lane = _iota(ang.shape, 1)
    cs, sn = jnp.cos(ang), jnp.sin(ang)
    return (jnp.where(lane < 64, 1.0, jnp.where(lane < 96, cs, 0.0)),
            jnp.where((lane >= 64) & (lane < 80), -sn, 0.0), jnp.where((lane >= 80) & (lane < 96), sn, 0.0))


def _rope(t, c, sa, sb):
    return t * c + pltpu.roll(t, 112, 1) * sa + pltpu.roll(t, 16, 1) * sb


def _rope_t(d, c, sa, sb):
    return d * c + pltpu.roll(d * sa, 16, 1) + pltpu.roll(d * sb, 112, 1)


def _rms(c, g):
    r = lax.rsqrt(jnp.mean(c * c, axis=-1, keepdims=True) + NORM_EPS)
    xh = c * r
    return xh * g, xh, r


def _mla_proj_fwd(x, win, gq, gkv, wuq, wukv_k, wukv_v, pos_col, inv_lane, exchange=None):
    s = x.shape[0]
    tb = min(TB_PROJ, s)

    def body(x_ref, win_ref, gq_ref, gkv_ref, wuq_ref, wk_ref, wv_ref, pos_ref, inv_ref,
             gate_ref, qm_ref, cq_ref, ckv_ref, q_ref, qt_ref, k_ref, v_ref, vt_ref, c_ref, sa_ref, sb_ref):
        z = _mm(x_ref[...], win_ref[...])
        gate_ref[...] = z[:, 0:1024]
        qm_ref[...] = z[:, 1024:1280].astype(BF16)
        cq = z[:, 1280:1664]
        ckv = z[:, 1664:1920]
        cq_ref[...] = cq
        ckv_ref[...] = ckv
        c, sa, sb = _rope_tables(pos_ref[...], inv_ref[...])
        c_ref[...], sa_ref[...], sb_ref[...] = c, sa, sb
        nq, _, _ = _rms(cq, gq_ref[...])
        nkv, _, _ = _rms(ckv, gkv_ref[...])
        qf = _mm(nq, wuq_ref[...])
        kf = _mm(nkv, wk_ref[...])
        vf = _mm(nkv, wv_ref[...])
        v_ref[...] = vf.astype(BF16)
        for j in range(N_TOK_HEADS // 2):
            sl = slice(HEAD_PAD * j, HEAD_PAD * (j + 1))
            vt_ref[sl, :] = vf[:, sl].T.astype(BF16)
        kr = _rope(z[:, 1920:2048], c, sa, sb)
        for h in range(N_TOK_HEADS):
            sl = slice(HEAD_PAD * h, HEAD_PAD * (h + 1))
            qh = _rope(qf[:, sl], c, sa, sb) * ATT_SCALE_LOG2
            q_ref[:, sl] = qh.astype(BF16)
            qt_ref[sl, :] = qh.T.astype(BF16)
            k_ref[:, sl] = (kf[:, sl] + kr).astype(BF16)

    outs = (jax.ShapeDtypeStruct((s, 1024), F32), jax.ShapeDtypeStruct((s, MEM_W), BF16),
            jax.ShapeDtypeStruct((s, Q_LORA), F32), jax.ShapeDtypeStruct((s, KV_LORA), F32),
            jax.ShapeDtypeStruct((s, QK_W), BF16), jax.ShapeDtypeStruct((QK_W, s), BF16),
            jax.ShapeDtypeStruct((s, QK_W), BF16),
            jax.ShapeDtypeStruct((s, TOK_W), BF16), jax.ShapeDtypeStruct((TOK_W, s), BF16),
            *[jax.ShapeDtypeStruct((s, HEAD_PAD), F32)] * 3)

    def cols(w):
        return pl.BlockSpec((w, tb), lambda i: (0, i))

    return _run(
        body, name="mla_proj_fwd", grid=(s // tb,), out_shape=outs,
        in_specs=[_rows(tb, 1024), _const((1024, 2048)), _const((1, Q_LORA)), _const((1, KV_LORA)),
                  _const((Q_LORA, QK_W)), _const((KV_LORA, QK_W)), _const((KV_LORA, TOK_W)),
                  _rows(tb, 1), _const((1, HEAD_PAD))],
        out_specs=(_rows(tb, 1024), _rows(tb, MEM_W), _rows(tb, Q_LORA), _rows(tb, KV_LORA),
                   _rows(tb, QK_W), cols(QK_W), _rows(tb, QK_W), _rows(tb, TOK_W), cols(TOK_W),
                   _rows(tb, HEAD_PAD), _rows(tb, HEAD_PAD), _rows(tb, HEAD_PAD)),
        args=(x, win, gq, gkv, wuq, wukv_k, wukv_v, pos_col, inv_lane), sem=("parallel",), exchange=exchange)


def _mla_proj_bwd(x, cq, ckv, dq, dkt, dvt, dgate, dqm, dres, win, gq, gkv, wuq, wukv_k, wukv_v, ctab, satab, sbtab):
    s = x.shape[0]
    tb = min(TB_PROJ_BWD, s)

    def body(x_ref, cq_ref, ckv_ref, dq_ref, dkt_ref, dvt_ref, dgate_ref, dqm_ref, dres_ref, win_ref, gq_ref, gkv_ref,
             wuq_ref, wk_ref, wv_ref, c_ref, sa_ref, sb_ref,
             dx_ref, dwin_ref, dwuq_ref, dwk_ref, dwv_ref, dgq_ref, dgkv_ref):
        @pl.when(pl.program_id(0) == 0)
        def _():
            for r in (dwin_ref, dwuq_ref, dwk_ref, dwv_ref, dgq_ref, dgkv_ref):
                r[...] = jnp.zeros_like(r)

        c, sa, sb = c_ref[...], sa_ref[...], sb_ref[...]
        lane = _iota((tb, HEAD_PAD), 1)
        gq, gkv = gq_ref[...], gkv_ref[...]
        nq, xhq, rq = _rms(cq_ref[...], gq)
        nkv, xhk, rk = _rms(ckv_ref[...], gkv)
        dkp = dkt_ref[...].T * math.log(2.0)
        dqs, dkr = [], jnp.zeros((tb, HEAD_PAD), F32)
        for h in range(N_TOK_HEADS):
            sl = slice(HEAD_PAD * h, HEAD_PAD * (h + 1))
            dqs.append(_rope_t(dq_ref[:, sl], c, sa, sb).astype(BF16))
            dkr = dkr + dkp[:, sl]
        dqf = jnp.concatenate(dqs, axis=1)
        dkr = jnp.where((lane >= 64) & (lane < 96), _rope_t(dkr, c, sa, sb), 0.0)
        dvb = dvt_ref[...].T.astype(BF16)
        dkb = dkp.astype(BF16)
        dnq = _mm_nt(dqf, wuq_ref[...])
        dwuq_ref[...] += _mm_tn(nq, dqf)
        dgq_ref[...] += jnp.sum(dnq * xhq, axis=0, keepdims=True)
        dxh = dnq * gq
        dcq = rq * (dxh - xhq * jnp.mean(dxh * xhq, axis=-1, keepdims=True))
        dnkv = _mm_nt(dkb, wk_ref[...]) + _mm_nt(dvb, wv_ref[...])
        nkvb = nkv.astype(BF16)
        dwk_ref[...] += _mm_tn(nkvb, dkb)
        dwv_ref[...] += _mm_tn(nkvb, dvb)
        dgkv_ref[...] += jnp.sum(dnkv * xhk, axis=0, keepdims=True)
        dxh = dnkv * gkv
        dckv = rk * (dxh - xhk * jnp.mean(dxh * xhk, axis=-1, keepdims=True))
        dz = jnp.concatenate([dgate_ref[...], dqm_ref[...], dcq, dckv, dkr], axis=1).astype(BF16)
        dx_ref[...] = _mm_nt(dz, win_ref[...]) + dres_ref[...]
        dwin_ref[...] += _mm_tn(x_ref[...], dz)

    outs = (jax.ShapeDtypeStruct((s, 1024), F32), jax.ShapeDtypeStruct((1024, 2048), F32),
            jax.ShapeDtypeStruct((Q_LORA, QK_W), F32), jax.ShapeDtypeStruct((KV_LORA, QK_W), F32),
            jax.ShapeDtypeStruct((KV_LORA, TOK_W), F32), jax.ShapeDtypeStruct((1, Q_LORA), F32),
            jax.ShapeDtypeStruct((1, KV_LORA), F32))
    return _run(
        body, name="mla_proj_bwd", grid=(s // tb,), out_shape=outs,
        in_specs=[_rows(tb, 1024), _rows(tb, Q_LORA), _rows(tb, KV_LORA), _rows(tb, QK_W),
                  pl.BlockSpec((QK_W, tb), lambda i: (0, i)), pl.BlockSpec((TOK_W, tb), lambda i: (0, i)),
                  _rows(tb, 1024), _rows(tb, MEM_W), _rows(tb, 1024),
                  _const((1024, 2048)), _const((1, Q_LORA)), _const((1, KV_LORA)),
                  _const((Q_LORA, QK_W)), _const((KV_LORA, QK_W)), _const((KV_LORA, TOK_W)),
                  _rows(tb, HEAD_PAD), _rows(tb, HEAD_PAD), _rows(tb, HEAD_PAD)],
        out_specs=(_rows(tb, 1024), _const((1024, 2048)), _const((Q_LORA, QK_W)), _const((KV_LORA, QK_W)),
                   _const((KV_LORA, TOK_W)), _const((1, Q_LORA)), _const((1, KV_LORA))),
        args=(x, cq, ckv, dq, dkt, dvt, dgate, dqm, dres, win, gq, gkv, wuq, wukv_k, wukv_v, ctab, satab, sbtab),
        sem=("arbitrary",), vmem_limit=VMEM_LIMIT_PROJ_BWD)


def _attn_fwd(q, k, vt, exchange=None):
    s = q.shape[0]
    tq = min(TQ_ATT_FWD, s)
    tk = min(TK_ATT, s)

    def body(q_ref, k_ref, vt_ref, o_ref, lse_ref):
        i = pl.program_id(1)
        nfull = (i * tq) // tk
        krow = _iota((tk, tq), 0)
        qpos = i * tq + _iota((tk, tq), 1)

        def head_tile(hh, st, carry, masked):
            hs = slice(HEAD_PAD * hh, HEAD_PAD * (hh + 1))
            m, l, acc = carry
            sc = _mm_nt(k_ref[pl.ds(st, tk), hs], q_ref[:, hs])
            if masked:
                sc = jnp.where(st + krow <= qpos, sc, -jnp.inf)
            m_new = jnp.maximum(m, jnp.max(sc, axis=0, keepdims=True))
            p = jnp.exp2(sc - m_new)
            a = jnp.exp2(m - m_new)
            l = a * l + jnp.sum(p, axis=0, keepdims=True)
            acc = a * acc + _mm(vt_ref[64 * hh:64 * (hh + 1), pl.ds(st, tk)], p)
            return m_new, l, acc

        def tile(j, carry, masked):
            st = pl.multiple_of(j * tk, tk)
            return tuple(head_tile(hh, st, carry[hh], masked) for hh in range(2))

        def init():
            return (jnp.full((1, tq), -jnp.inf, F32), jnp.zeros((1, tq), F32), jnp.zeros((64, tq), F32))

        carry = lax.fori_loop(0, nfull, functools.partial(tile, masked=False), (init(), init()))
        (ma, la, acca), (mb, lb, accb) = tile(nfull, carry, True)
        o_ref[...] = jnp.concatenate([acca / la, accb / lb], axis=0).T
        lse_ref[...] = jnp.concatenate([jnp.broadcast_to(ma + jnp.log2(la), (64, tq)),
                                        jnp.broadcast_to(mb + jnp.log2(lb), (64, tq))], axis=0).T

    shp = jax.ShapeDtypeStruct((s, TOK_W), F32)
    return _run(
        body, name="attn_fwd", grid=(N_TOK_HEADS // 2, s // tq), out_shape=(shp, shp),
        in_specs=[pl.BlockSpec((tq, 2 * HEAD_PAD), lambda j, i: (i, j)),
                  pl.BlockSpec((s, 2 * HEAD_PAD), lambda j, i: (0, j)),
                  pl.BlockSpec((HEAD_PAD, s), lambda j, i: (j, 0))],
        out_specs=(pl.BlockSpec((tq, HEAD_PAD), lambda j, i: (i, j)),) * 2,
        args=(q, k, vt), sem=("parallel", "arbitrary"), exchange=exchange)


def _attn_stats(o, do, lse_ref, dob_ref, dot_ref, st_ref):
    lane = _iota((o.shape[0], HEAD_PAD), 1)
    dob_ref[...] = do.astype(BF16)
    prod = do * o
    for j in range(N_TOK_HEADS // 2):
        sl = slice(HEAD_PAD * j, HEAD_PAD * (j + 1))
        dot_ref[sl, :] = do[:, sl].T.astype(BF16)
        pj = prod[:, sl]
        da = jnp.sum(jnp.where(lane < 64, pj, 0.0), axis=-1, keepdims=True)
        db = jnp.sum(jnp.where(lane >= 64, pj, 0.0), axis=-1, keepdims=True)
        la = lse_ref[:, HEAD_PAD * j:HEAD_PAD * j + 1]
        lb = lse_ref[:, HEAD_PAD * j + 64:HEAD_PAD * j + 65]
        st_ref[j] = jnp.where(lane == 0, la, jnp.where(lane == 1, lb, jnp.where(lane == 2, da,
                                                                                 jnp.where(lane == 3, db, 0.0))))


def _attn_bwd(q, qt, k, v, dob, dobt, stats, exchange=None):
    s = q.shape[0]
    t = min(TQ_ATT, s)
    nq = s // t

    def body(q_ref, qt_ref, do_ref, dot_ref, st_ref, k_ref, v_ref, dq_ref, dkt_ref, dvt_ref):
        i = pl.program_id(1)

        @pl.when(i == 0)
        def _():
            dkt_ref[...] = jnp.zeros_like(dkt_ref)
            dvt_ref[...] = jnp.zeros_like(dvt_ref)

        lane = _iota((t, HEAD_PAD), 1)
        qpos, kcol = _iota((t, t), 0), _iota((t, t), 1)
        do = do_ref[...]
        stats = st_ref[0]

        def head_tile(hh, ks, dq_acc, masked):
            hs = slice(HEAD_PAD * hh, HEAD_PAD * (hh + 1))
            qh = q_ref[:, hs]
            kh = k_ref[pl.ds(ks, t), hs]
            dom = jnp.where((lane < 64) if hh == 0 else (lane >= 64), do, jnp.zeros_like(do))
            lse = stats[:, hh:hh + 1]
            dlt = stats[:, 2 + hh:3 + hh]
            sc = _mm_nt(qh, kh)
            if masked:
                sc = jnp.where(kcol <= qpos, sc, -jnp.inf)
            p = jnp.exp2(sc - lse)
            dp = _mm_nt(dom, v_ref[pl.ds(ks, t), :])
            ds = (p * (dp - dlt)).astype(BF16)
            dvt_ref[64 * hh:64 * (hh + 1), pl.ds(ks, t)] += _mm(dot_ref[64 * hh:64 * (hh + 1), :], p)
            dkt_ref[HEAD_PAD * hh:HEAD_PAD * hh + 96, pl.ds(ks, t)] += _mm(qt_ref[HEAD_PAD * hh:HEAD_PAD * hh + 96, :], ds)
            return dq_acc + _mm(ds, kh)

        def tile(j, carry, masked):
            ks = pl.multiple_of(j * t, t)
            return tuple(head_tile(hh, ks, carry[hh], masked) for hh in range(2))

        zero = jnp.zeros((t, HEAD_PAD), F32)
        carry = lax.fori_loop(0, i, functools.partial(tile, masked=False), (zero, zero))
        dqa, dqb = tile(i, carry, True)
        dq_ref[...] = jnp.concatenate([dqa, dqb], axis=1) * ATT_SCALE

    return _run(
        body, name="attn_bwd", grid=(N_TOK_HEADS // 2, nq),
        out_shape=(jax.ShapeDtypeStruct((s, QK_W), F32), jax.ShapeDtypeStruct((QK_W, s), F32),
                   jax.ShapeDtypeStruct((TOK_W, s), F32)),
        in_specs=[pl.BlockSpec((t, 2 * HEAD_PAD), lambda j, i: (i, j)),
                  pl.BlockSpec((2 * HEAD_PAD, t), lambda j, i: (j, i)),
                  pl.BlockSpec((t, HEAD_PAD), lambda j, i: (i, j)),
                  pl.BlockSpec((HEAD_PAD, t), lambda j, i: (j, i)),
                  pl.BlockSpec((1, t, HEAD_PAD), lambda j, i: (j, i, 0)),
                  pl.BlockSpec((s, 2 * HEAD_PAD), lambda j, i: (0, j)),
                  pl.BlockSpec((s, HEAD_PAD), lambda j, i: (0, j))],
        out_specs=(pl.BlockSpec((t, 2 * HEAD_PAD), lambda j, i: (i, j)),
                   pl.BlockSpec((2 * HEAD_PAD, s), lambda j, i: (j, 0)),
                   pl.BlockSpec((HEAD_PAD, s), lambda j, i: (j, 0))),
        args=(q, qt, dob, dobt, stats, k, v), sem=("parallel", "arbitrary"), exchange=exchange)


def _mem_kv(mem, wmem):
    def body(m_ref, w_ref, o_ref):
        o_ref[0] = _mm(m_ref[...], w_ref[0]).astype(BF16)

    return pl.pallas_call(
        body, name="mem_kv", grid=(2,), out_shape=jax.ShapeDtypeStruct((2, MEM_LEN, 512), BF16),
        in_specs=[_const((MEM_LEN, 1024)), pl.BlockSpec((1, 1024, 512), lambda l: (l, 0, 0))],
        out_specs=pl.BlockSpec((1, MEM_LEN, 512), lambda l: (l, 0, 0)),
        compiler_params=_params(("parallel",)),
    )(mem, wmem)


def _mem_kv_bwd(mem, dmemkv):
    def body(m_ref, d_ref, o_ref):
        o_ref[...] = _mm_tn(m_ref[...], d_ref[...])

    return pl.pallas_call(
        body, name="mem_kv_bwd", grid=(1,), out_shape=jax.ShapeDtypeStruct((1024, 512), F32),
        in_specs=[_const((MEM_LEN, 1024)), _const((MEM_LEN, 512))], out_specs=_const((1024, 512)),
        compiler_params=_params(("arbitrary",)),
    )(mem, dmemkv)


def _head_mask(lane, sub):
    return (lane < 64) if sub == 0 else (lane >= 64)


def _mem_attn(qm, kv):
    tb = qm.shape[0]
    lane = _iota((tb, HEAD_PAD), 1)
    outs, ps = [], []
    for pp in range(2):
        qp = qm[:, HEAD_PAD * pp:HEAD_PAD * (pp + 1)]
        kp = kv[:, HEAD_PAD * pp:HEAD_PAD * (pp + 1)]
        vp = kv[:, MEM_W + HEAD_PAD * pp:MEM_W + HEAD_PAD * (pp + 1)]
        pair = None
        for sub in range(2):
            qh = jnp.where(_head_mask(lane, sub), qp, jnp.zeros_like(qp))
            sc = _mm_nt(qh, kp) * 0.125
            e = jnp.exp(sc - jnp.max(sc, axis=-1, keepdims=True))
            p = e / jnp.sum(e, axis=-1, keepdims=True)
            o = _mm(p, vp)
            ps.append(p)
            pair = o if sub == 0 else jnp.where(lane < 64, pair, o)
        outs.append(pair)
    return jnp.concatenate(outs, axis=1), ps


def _mem_attn_bwd(dmo, qm, kv, ps):
    tb = qm.shape[0]
    lane = _iota((tb, HEAD_PAD), 1)
    dqs, dks, dvs = [], [], []
    for pp in range(2):
        qp = qm[:, HEAD_PAD * pp:HEAD_PAD * (pp + 1)]
        kp = kv[:, HEAD_PAD * pp:HEAD_PAD * (pp + 1)]
        vp = kv[:, MEM_W + HEAD_PAD * pp:MEM_W + HEAD_PAD * (pp + 1)]
        dop = dmo[:, HEAD_PAD * pp:HEAD_PAD * (pp + 1)]
        dq_pair, dk_pair, dv_pair = None, None, None
        for sub in range(2):
            msk = _head_mask(lane, sub)
            p = ps[2 * pp + sub]
            qh = jnp.where(msk, qp, jnp.zeros_like(qp))
            doh = jnp.where(msk, dop, 0.0).astype(BF16)
            dv = _mm_tn(p, doh)
            dp = _mm_nt(doh, vp)
            ds = (p * (dp - jnp.sum(dp * p, axis=-1, keepdims=True)) * 0.125).astype(BF16)
            dq = _mm(ds, kp)
            dk = _mm_tn(ds, qh)
            if sub == 0:
                dq_pair, dk_pair, dv_pair = dq, dk, dv
            else:
                dq_pair = jnp.where(lane < 64, dq_pair, dq)
                dk_pair, dv_pair = dk_pair + dk, dv_pair + dv
        dqs.append(dq_pair)
        dks.append(dk_pair)
        dvs.append(dv_pair)
    return jnp.concatenate(dqs, axis=1), jnp.concatenate(dks + dvs, axis=1)


def _mix_core(tok, gate, qm, kv, wout, h_in, g, b):
    mem_out, ps = _mem_attn(qm, kv)
    cat = jnp.concatenate([tok, mem_out], axis=1)
    sg = jax.nn.sigmoid(gate)
    sl = gate * sg
    y = cat * sl
    r = ALPHA * h_in + _mm(y, wout)
    mu = jnp.mean(r, axis=-1, keepdims=True)
    xc = r - mu
    rstd = lax.rsqrt(jnp.mean(xc * xc, axis=-1, keepdims=True) + NORM_EPS)
    xh = xc * rstd
    return xh * g + b, (ps, cat, sg, sl, y, xh, rstd)


def _mix_fwd(tok, gate, qm, kv, wout, h_in, g, b):
    s = tok.shape[0]
    tb = min(TB_MIX, s)

    def body(tok_ref, gate_ref, qm_ref, kv_ref, w_ref, h_ref, g_ref, b_ref, o_ref):
        o_ref[...], _ = _mix_core(tok_ref[...], gate_ref[...], qm_ref[...], kv_ref[...], w_ref[...], h_ref[...],
                                  g_ref[...], b_ref[...])

    return pl.pallas_call(
        body, name="mix_fwd", grid=(s // tb,), out_shape=jax.ShapeDtypeStruct((s, 1024), F32),
        in_specs=[_rows(tb, TOK_W), _rows(tb, 1024), _rows(tb, MEM_W), _const((MEM_LEN, 512)), _const((1024, 1024)),
                  _rows(tb, 1024), _const((1, 1024)), _const((1, 1024))],
        out_specs=_rows(tb, 1024), compiler_params=_params(("parallel",)),
    )(tok, gate, qm, kv, wout, h_in, g, b)


def _mix_bwd(tok, gate, qm, kv, wout, h_in, g, b, up, from_loss, lse=None, exchange=None):
    s = tok.shape[0]
    tb = min(TB_MIX, s)
    n_in = 9 if lse is None else 10
    n_tok_out = 1 if lse is None else 3

    def body(*refs):
        tok_ref, gate_ref, qm_ref, kv_ref, w_ref, h_ref, g_ref, b_ref, up_ref = refs[:9]
        dres_ref, tok_out = refs[n_in], refs[n_in + 1:n_in + 1 + n_tok_out]
        dgate_ref, dqm_ref, dw_ref, dkv_ref, dg_ref, db_ref, loss_ref = refs[n_in + 1 + n_tok_out:]

        @pl.when(pl.program_id(0) == 0)
        def _():
            for r in (dw_ref, dkv_ref, dg_ref, db_ref, loss_ref):
                r[...] = jnp.zeros_like(r)

        gate, qm, kv, wout, g = gate_ref[...], qm_ref[...], kv_ref[...], w_ref[...], g_ref[...]
        h_out, (ps, cat, sg, sl, y, xh, rstd) = _mix_core(tok_ref[...], gate, qm, kv, wout, h_ref[...], g, b_ref[...])
        if from_loss:
            diff = h_out - up_ref[...]
            loss_ref[...] += 0.5 * jnp.sum(jnp.mean(diff * diff, axis=-1, keepdims=True), axis=0, keepdims=True)
            dh = diff * (1.0 / D_MODEL)
        else:
            dh = up_ref[...]
        dg_ref[...] += jnp.sum(dh * xh, axis=0, keepdims=True)
        db_ref[...] += jnp.sum(dh, axis=0, keepdims=True)
        dxh = dh * g
        dr = rstd * (dxh - jnp.mean(dxh, axis=-1, keepdims=True) - xh * jnp.mean(dxh * xh, axis=-1, keepdims=True))
        dres_ref[...] = ALPHA * dr
        drb = dr.astype(BF16)
        dy = _mm_nt(drb, wout)
        dw_ref[...] += _mm_tn(y, drb)
        dcat = dy * sl
        dgate_ref[...] = dy * cat * (sg * (1.0 + gate * (1.0 - sg)))
        if lse is None:
            tok_out[0][...] = dcat[:, :TOK_W]
        else:
            _attn_stats(tok_ref[...], dcat[:, :TOK_W], refs[9], *tok_out)
        dqm, dkv = _mem_attn_bwd(dcat[:, TOK_W:], qm, kv, ps)
        dqm_ref[...] = dqm
        dkv_ref[...] += dkv

    npair = N_TOK_HEADS // 2
    tok_shapes = [jax.ShapeDtypeStruct((s, TOK_W), F32)] if lse is None else [
        jax.ShapeDtypeStruct((s, TOK_W), BF16), jax.ShapeDtypeStruct((TOK_W, s), BF16),
        jax.ShapeDtypeStruct((npair, s, HEAD_PAD), F32)]
    tok_specs = [_rows(tb, TOK_W)] if lse is None else [
        _rows(tb, TOK_W), pl.BlockSpec((TOK_W, tb), lambda i: (0, i)),
        pl.BlockSpec((npair, tb, HEAD_PAD), lambda i: (0, i, 0))]
    outs = (jax.ShapeDtypeStruct((s, 1024), F32), *tok_shapes,
            jax.ShapeDtypeStruct((s, 1024), F32), jax.ShapeDtypeStruct((s, MEM_W), F32),
            jax.ShapeDtypeStruct((1024, 1024), F32), jax.ShapeDtypeStruct((MEM_LEN, 512), F32),
            jax.ShapeDtypeStruct((1, 1024), F32), jax.ShapeDtypeStruct((1, 1024), F32),
            jax.ShapeDtypeStruct((1, 1), F32))
    args = (tok, gate, qm, kv, wout, h_in, g, b, up) + (() if lse is None else (lse,))
    return _run(
        body, name="mix_bwd_loss" if from_loss else "mix_bwd", grid=(s // tb,), out_shape=outs,
        in_specs=[_rows(tb, TOK_W), _rows(tb, 1024), _rows(tb, MEM_W), _const((MEM_LEN, 512)), _const((1024, 1024)),
                  _rows(tb, 1024), _const((1, 1024)), _const((1, 1024)), _rows(tb, 1024)]
        + ([] if lse is None else [_rows(tb, TOK_W)]),
        out_specs=(_rows(tb, 1024), *tok_specs, _rows(tb, 1024), _rows(tb, MEM_W), _const((1024, 1024)),
                   _const((MEM_LEN, 512)), _const((1, 1024)), _const((1, 1024)), _const((1, 1))),
        args=args, sem=("arbitrary",), exchange=exchange)


def _shift_down(u, tail, k):
    if k == 0:
        return u
    r = pltpu.roll(u, k, 0)
    row8 = _iota((8, u.shape[1]), 0)
    head = jnp.where(row8 < k, pltpu.roll(tail, k, 0), r[:8])
    return jnp.concatenate([head, r[8:]], axis=0)


def _shift_up(d, head, k):
    if k == 0:
        return d
    n = d.shape[0]
    r = pltpu.roll(d, n - k, 0)
    row8 = _iota((8, d.shape[1]), 0)
    last = jnp.where(row8 >= 8 - k, pltpu.roll(head, 8 - k, 0), r[n - 8:])
    return jnp.concatenate([r[:n - 8], last], axis=0)


def _scan_down(a, b):
    n = a.shape[0]
    row = _iota(a.shape, 0)
    s = 1
    while s < n:
        ok = row >= s
        a_s = jnp.where(ok, pltpu.roll(a, s, 0), 1.0)
        b_s = jnp.where(ok, pltpu.roll(b, s, 0), 0.0)
        b = a * b_s + b
        a = a * a_s
        s *= 2
    return a, b


def _scan_up(a, b):
    n = a.shape[0]
    row = _iota(a.shape, 0)
    s = 1
    while s < n:
        ok = row < n - s
        a_s = jnp.where(ok, pltpu.roll(a, n - s, 0), 1.0)
        b_s = jnp.where(ok, pltpu.roll(b, n - s, 0), 0.0)
        b = a * b_s + b
        a = a * a_s
        s *= 2
    return a, b


def _softplus(x):
    return jnp.maximum(x, 0.0) + jnp.log(1.0 + jnp.exp(-jnp.abs(x)))


def _lru_gates(u, tail, cw, cb, wr, br, wi, bi, lam):
    us = [_shift_down(u, tail, k) for k in range(4)]
    xc = cb + us[3] * cw[0:1] + us[2] * cw[1:2] + us[1] * cw[2:3] + us[0] * cw[3:4]
    xb = xc.astype(BF16)
    pre_r = jnp.concatenate([_mm(xb[:, 256 * g:256 * (g + 1)], wr[g]) for g in range(3)], axis=1) + br
    pre_i = jnp.concatenate([_mm(xb[:, 256 * g:256 * (g + 1)], wi[g]) for g in range(3)], axis=1) + bi
    rg, ig = jax.nn.sigmoid(pre_r), jax.nn.sigmoid(pre_i)
    clam = -LRU_C * _softplus(-lam)
    la = clam * rg
    a = jnp.exp(la)
    mm = jnp.sqrt(-jnp.tanh(la) * (a * a + 1.0))
    return us, xc, xb, rg, ig, clam, la, a, mm


def _lru_fwd(h, win, cw, cb, wr, br, wi, bi, lam):
    s = h.shape[0]
    tb = min(TB_LRU, s)

    def body(h_ref, win_ref, cw_ref, cb_ref, wr_ref, br_ref, wi_ref, bi_ref, lam_ref,
             u_ref, gate_ref, qm_ref, hs_ref, tail_sc, carry_sc):
        @pl.when(pl.program_id(0) == 0)
        def _():
            tail_sc[...] = jnp.zeros_like(tail_sc)
            carry_sc[...] = jnp.zeros_like(carry_sc)

        hb = h_ref[...].astype(BF16)
        z = jnp.concatenate([_mm(hb, win_ref[sh]) for sh in range(4)], axis=1)
        u = z[:, :TOK_W]
        u_ref[...] = u
        gate_ref[...] = z[:, TOK_W:TOK_W + 1024]
        qm_ref[...] = z[:, TOK_W + 1024:].astype(BF16)
        _, xc, _, _, ig, _, _, a, mm = _lru_gates(u, tail_sc[...], cw_ref[...], cb_ref[...], wr_ref[...], br_ref[...],
                                                 wi_ref[...], bi_ref[...], lam_ref[...])
        big_a, big_b = _scan_down(a, mm * (ig * xc))
        hs = big_a * carry_sc[0:1, :] + big_b
        hs_ref[...] = hs
        tail_sc[...] = u[tb - 8:, :]
        carry_sc[...] = jnp.broadcast_to(hs[tb - 1:tb, :], carry_sc.shape)

    outs = (jax.ShapeDtypeStruct((s, TOK_W), F32), jax.ShapeDtypeStruct((s, 1024), F32),
            jax.ShapeDtypeStruct((s, MEM_W), BF16), jax.ShapeDtypeStruct((s, TOK_W), F32))
    return pl.pallas_call(
        body, name="lru_fwd", grid=(s // tb,), out_shape=outs,
        in_specs=[_rows(tb, 1024), _const((4, 1024, 512)), _const((4, TOK_W)), _const((1, TOK_W)),
                  _const((3, 256, 256)), _const((1, TOK_W)), _const((3, 256, 256)), _const((1, TOK_W)),
                  _const((1, TOK_W))],
        out_specs=(_rows(tb, TOK_W), _rows(tb, 1024), _rows(tb, MEM_W), _rows(tb, TOK_W)),
        scratch_shapes=[pltpu.VMEM((8, TOK_W), F32), pltpu.VMEM((8, TOK_W), F32)],
        compiler_params=_params(),
    )(h, win, cw, cb, wr, br, wi, bi, lam)


def _lru_bwd(dhs, dgate, dqm, dres, h, u, hs, win, cw, cb, wr, br, wi, bi, lam):
    s = h.shape[0]
    tb = min(TB_LRU, s)
    nb = s // tb

    def rev(w):
        return pl.BlockSpec((tb, w), lambda i: (nb - 1 - i, 0))

    def prev_tail(w):
        return pl.BlockSpec((8, w), lambda i: (jnp.maximum((nb - 1 - i) * (tb // 8) - 1, 0), 0))

    def body(dhs_ref, dgate_ref, dqm_ref, dres_ref, h_ref, u_ref, hs_ref, ut_ref, hst_ref, win_ref, cw_ref, cb_ref,
             wr_ref, br_ref, wi_ref, bi_ref, lam_ref,
             dh_ref, dwin_ref, dcw_ref, dcb_ref, dwr_ref, dbr_ref, dwi_ref, dbi_ref, dlam_ref, ecar_sc, dxc_sc):
        i = pl.program_id(0)

        @pl.when(i == 0)
        def _():
            for r in (dwin_ref, dcw_ref, dcb_ref, dwr_ref, dbr_ref, dwi_ref, dbi_ref, dlam_ref, ecar_sc, dxc_sc):
                r[...] = jnp.zeros_like(r)

        first = (i == nb - 1)
        u = u_ref[...]
        utail = jnp.where(first, 0.0, ut_ref[...])
        hstail = jnp.where(first, 0.0, hst_ref[...])
        cw, wr, wi, lam = cw_ref[...], wr_ref[...], wi_ref[...], lam_ref[...]
        us, xc, xb, rg, ig, clam, la, a, mm = _lru_gates(u, utail, cw, cb_ref[...], wr, br_ref[...], wi, bi_ref[...], lam)
        row = _iota(a.shape, 0)
        a_next = jnp.where(row < tb - 1, pltpu.roll(a, tb - 1, 0), 1.0)
        big_a, big_b = _scan_up(a_next, dhs_ref[...])
        e = big_a * ecar_sc[0:1, :] + big_b
        ecar_sc[...] = jnp.broadcast_to(a[0:1, :] * e[0:1, :], ecar_sc.shape)
        hs_prev = _shift_down(hs_ref[...], hstail, 1)
        da = e * hs_prev
        ix = ig * xc
        dmm = e * ix
        dix = e * mm
        dla = da * a - dmm * (a * a) / mm
        dlam_ref[...] += jnp.sum(dla * rg, axis=0, keepdims=True)
        dpr = (dla * clam) * rg * (1.0 - rg)
        dpi = (dix * xc) * ig * (1.0 - ig)
        dbr_ref[...] += jnp.sum(dpr, axis=0, keepdims=True)
        dbi_ref[...] += jnp.sum(dpi, axis=0, keepdims=True)
        dprb, dpib = dpr.astype(BF16), dpi.astype(BF16)
        dxc_g = []
        for g in range(3):
            sl = slice(256 * g, 256 * (g + 1))
            dwr_ref[g] += _mm_tn(xb[:, sl], dprb[:, sl])
            dwi_ref[g] += _mm_tn(xb[:, sl], dpib[:, sl])
            dxc_g.append(_mm_nt(dprb[:, sl], wr[g]) + _mm_nt(dpib[:, sl], wi[g]))
        dxc = dix * ig + jnp.concatenate(dxc_g, axis=1)
        dcb_ref[...] += jnp.sum(dxc, axis=0, keepdims=True)
        dcw_ref[...] += jnp.concatenate([jnp.sum(dxc * us[3 - tap], axis=0, keepdims=True) for tap in range(4)], axis=0)
        head = dxc_sc[...]
        du = dxc * cw[3:4]
        for k in range(1, 4):
            du = du + _shift_up(dxc, head, k) * cw[3 - k:4 - k]
        dxc_sc[...] = dxc[:8, :]
        dz = jnp.concatenate([du, dgate_ref[...], dqm_ref[...]], axis=1).astype(BF16)
        hb = h_ref[...].astype(BF16)
        dh = dres_ref[...]
        for sh in range(4):
            dzs = dz[:, 512 * sh:512 * (sh + 1)]
            dh = dh + _mm_nt(dzs, win_ref[sh])
            dwin_ref[sh] += _mm_tn(hb, dzs)
        dh_ref[...] = dh

        @pl.when(i == nb - 1)
        def _():
            dlam_ref[...] = dlam_ref[...] * (LRU_C * jax.nn.sigmoid(-lam))

    outs = (jax.ShapeDtypeStruct((s, 1024), F32), jax.ShapeDtypeStruct((4, 1024, 512), F32),
            jax.ShapeDtypeStruct((4, TOK_W), F32), jax.ShapeDtypeStruct((1, TOK_W), F32),
            jax.ShapeDtypeStruct((3, 256, 256), F32), jax.ShapeDtypeStruct((1, TOK_W), F32),
            jax.ShapeDtypeStruct((3, 256, 256), F32), jax.ShapeDtypeStruct((1, TOK_W), F32),
            jax.ShapeDtypeStruct((1, TOK_W), F32))
    return pl.pallas_call(
        body, name="lru_bwd", grid=(nb,), out_shape=outs,
        in_specs=[rev(TOK_W), rev(1024), rev(MEM_W), rev(1024), rev(1024), rev(TOK_W), rev(TOK_W),
                  prev_tail(TOK_W), prev_tail(TOK_W),
                  _const((4, 1024, 512)), _const((4, TOK_W)), _const((1, TOK_W)), _const((3, 256, 256)),
                  _const((1, TOK_W)), _const((3, 256, 256)), _const((1, TOK_W)), _const((1, TOK_W))],
        out_specs=(rev(1024), _const((4, 1024, 512)), _const((4, TOK_W)), _const((1, TOK_W)), _const((3, 256, 256)),
                   _const((1, TOK_W)), _const((3, 256, 256)), _const((1, TOK_W)), _const((1, TOK_W))),
        scratch_shapes=[pltpu.VMEM((8, TOK_W), F32), pltpu.VMEM((8, TOK_W), F32)],
        compiler_params=_params(),
    )(dhs, dgate, dqm, dres, h, u, hs, u, hs, win, cw, cb, wr, br, wi, bi, lam)


def _adamw_update(w_ref, g_ref, m_ref, v_ref, d_ref, nm_ref, nv_ref):
    g = g_ref[...]
    nm = ADAM_B1 * m_ref[...] + (1.0 - ADAM_B1) * g
    nv = ADAM_B2 * v_ref[...] + (1.0 - ADAM_B2) * (g * g)
    m_hat = nm / (1.0 - ADAM_B1 ** ADAM_STEP)
    v_hat = nv / (1.0 - ADAM_B2 ** ADAM_STEP)
    d_ref[...] = -ADAM_LR * (m_hat / (jnp.sqrt(v_hat) + ADAM_EPS) + ADAM_WD * w_ref[...])
    nm_ref[...] = nm
    nv_ref[...] = nv


def _adamw(name, w, g, m, v):
    rows, cols = w.shape
    tb = 256 if rows % 256 == 0 else rows

    def body(*refs):
        _adamw_update(*refs)

    shp = jax.ShapeDtypeStruct((rows, cols), F32)
    return pl.pallas_call(
        body, name="adamw_" + name, grid=(rows // tb,), out_shape=(shp, shp, shp),
        in_specs=[_rows(tb, cols)] * 4, out_specs=(_rows(tb, cols),) * 3,
        compiler_params=_params(("parallel",)),
    )(w, g, m, v)


def _adamw_small(items):
    n = len(items)

    def body(*refs):
        for k in range(n):
            _adamw_update(*refs[4 * k:4 * k + 4], *refs[4 * n + 3 * k:4 * n + 3 * k + 3])

    args = [a for it in items for a in it]
    shapes = [jax.ShapeDtypeStruct(it[0].shape, F32) for it in items for _ in range(3)]
    outs = pl.pallas_call(
        body, name="adamw_small", grid=(1,), out_shape=tuple(shapes),
        in_specs=[_const(a.shape) for a in args], out_specs=tuple(_const(sh.shape) for sh in shapes),
        compiler_params=_params(),
    )(*args)
    return [outs[3 * k:3 * k + 3] for k in range(n)]


def _row_block(rows, cap=2048):
    return max(t for t in range(8, cap + 1, 8) if rows % t == 0)


def _add_own_half(sh, got, c_idx):
    hs = sh.shape[1] // 2
    tb = _row_block(hs, 1024)
    nb = hs // tb

    def body(c_ref, a_ref, b_ref, o_ref):
        o_ref[...] = a_ref[...] + b_ref[...]

    return pl.pallas_call(
        body, name="add_sibling",
        grid_spec=pltpu.PrefetchScalarGridSpec(
            num_scalar_prefetch=1, grid=(4, nb),
            in_specs=[pl.BlockSpec((1, tb, 128), lambda s, i, c: (s, c[0] * nb + i, 0)),
                      pl.BlockSpec((1, tb, 128), lambda s, i, c: (s, i, 0))],
            out_specs=pl.BlockSpec((1, tb, 128), lambda s, i, c: (s, i, 0))),
        out_shape=jax.ShapeDtypeStruct((4, hs, 128), F32),
        compiler_params=_params(("parallel", "parallel")),
    )(c_idx.reshape(1).astype(jnp.int32), sh, got)


def _add2(a, b):
    rows = a.shape[0]
    tb = _row_block(rows)

    def body(a_ref, b_ref, o_ref):
        o_ref[...] = a_ref[...] + b_ref[...]

    return pl.pallas_call(
        body, name="add_sibling", grid=(rows // tb,), out_shape=jax.ShapeDtypeStruct(a.shape, F32),
        in_specs=[_rows(tb, 128)] * 2, out_specs=_rows(tb, 128), compiler_params=_params(("parallel",)),
    )(a, b)


def _sum_slots(landed, own, rows):
    tb = _row_block(rows, 1024)

    def body(l_ref, o_ref, out_ref):
        t = 2 * lax.axis_index("x") + lax.axis_index("y")
        r = [jnp.where(t == s, o_ref[s], l_ref[s].astype(F32)) for s in range(4)]
        out_ref[...] = ((r[0] + r[1]) + r[2]) + r[3]

    return pl.pallas_call(
        body, name="sum_chips", grid=(rows // tb,), out_shape=jax.ShapeDtypeStruct((rows, 128), F32),
        in_specs=[pl.BlockSpec((4, tb, 128), lambda i: (0, i, 0))] * 2, out_specs=_rows(tb, 128),
        compiler_params=_params(("parallel",)),
    )(landed, own)


_ANY = pl.BlockSpec(memory_space=pl.ANY)


def _place():
    x, y, c = lax.axis_index("x"), lax.axis_index("y"), lax.axis_index("c")
    return x, y, c, [(1 - x, y), (x, 1 - y), (1 - x, 1 - y)]


def _remote(src, dst, ssem, rsem, to):
    return pltpu.make_async_remote_copy(src_ref=src, dst_ref=dst, send_sem=ssem, recv_sem=rsem, device_id=to,
                                        device_id_type=MESH)


class _Exchange:
    def __init__(self, ins, out_shape, sems, start, finish):
        self.ins, self.out_shape, self.sems, self.start, self.finish = ins, out_shape, sems, start, finish


def _run(body, *, name, grid, in_specs, out_specs, out_shape, args, scratch=(), sem, exchange=None,
         vmem_limit=VMEM_LIMIT):
    if exchange is None:
        return pl.pallas_call(body, name=name, grid=grid, out_shape=tuple(out_shape), in_specs=list(in_specs),
                              out_specs=tuple(out_specs), scratch_shapes=list(scratch),
                              compiler_params=_params(sem, vmem_limit))(*args)
    n_in, n_out, n_sc = len(args), len(out_shape), len(scratch)
    k_in, k_out = len(exchange.ins), len(exchange.out_shape)

    def fused(*refs):
        ins, refs = refs[:n_in], refs[n_in:]
        xin, refs = refs[:k_in], refs[k_in:]
        outs, refs = refs[:n_out], refs[n_out:]
        xout, refs = refs[:k_out], refs[k_out:]
        sc, xsem = refs[:n_sc], refs[n_sc:]
        first = pl.program_id(0) == 0
        last = pl.program_id(0) == grid[0] - 1
        for a in range(1, len(grid)):
            first = first & (pl.program_id(a) == 0)
            last = last & (pl.program_id(a) == grid[a] - 1)

        @pl.when(first)
        def _():
            exchange.start(xin, xout, xsem)

        body(*ins, *outs, *sc)

        @pl.when(last)
        def _():
            exchange.finish(xin, xout, xsem)

    return pl.pallas_call(
        fused, name=name, grid=grid, out_shape=(*out_shape, *exchange.out_shape),
        in_specs=[*in_specs, *[_ANY] * k_in], out_specs=(*out_specs, *[_ANY] * k_out),
        scratch_shapes=[*scratch, *exchange.sems],
        compiler_params=_params(("arbitrary",) * len(grid), vmem_limit),
    )(*args, *exchange.ins)


def _run_exchange(exchange, name):
    def body(*refs):
        k_in, k_out = len(exchange.ins), len(exchange.out_shape)
        xin, xout, xsem = refs[:k_in], refs[k_in:k_in + k_out], refs[k_in + k_out:]
        exchange.start(xin, xout, xsem)
        exchange.finish(xin, xout, xsem)

    return pl.pallas_call(
        body, name=name, out_shape=tuple(exchange.out_shape), in_specs=[_ANY] * len(exchange.ins),
        out_specs=tuple([_ANY] * len(exchange.out_shape)), scratch_shapes=list(exchange.sems),
    )(*exchange.ins)


def _gather_shards(wsh):
    _, hh, _ = wsh.shape

    def first_hop(w_ref, out_ref, ssems, rsems):
        x, y, c, chips = _place()
        t = 2 * x + y
        return [_remote(w_ref.at[c], out_ref.at[t, c], ssems.at[j], rsems.at[j], (cx, cy, c))
                for j, (cx, cy) in enumerate(chips)]

    def start(xin, xout, xsem):
        for cp in first_hop(xin[0], xout[0], *xsem):
            cp.start()

    def finish(xin, xout, xsem):
        out_ref, (ssems, rsems) = xout[0], xsem
        first = first_hop(xin[0], out_ref, *xsem)
        x, y, c, chips = _place()
        passed = []
        for j, (cx, cy) in enumerate(chips):
            got = out_ref.at[2 * cx + cy, c]
            _remote(got, got, ssems.at[j], rsems.at[j], (cx, cy, c)).wait_recv()
            cp = _remote(got, got, ssems.at[3 + j], rsems.at[3 + j], (x, y, 1 - c))
            cp.start()
            passed.append(cp)
        for j, (cx, cy) in enumerate(chips):
            got = out_ref.at[2 * cx + cy, 1 - c]
            _remote(got, got, ssems.at[3 + j], rsems.at[3 + j], (x, y, 1 - c)).wait_recv()
        for cp in first + passed:
            cp.wait_send()

    return _Exchange([wsh], [jax.ShapeDtypeStruct((4, 2, hh, 128), wsh.dtype)],
                     [pltpu.SemaphoreType.DMA((6,)), pltpu.SemaphoreType.DMA((6,))], start, finish)


def _gathered(landed, own):
    t = 2 * lax.axis_index("x") + lax.axis_index("y")
    return lax.dynamic_update_slice(landed, own[None], (t, 0, 0, 0))


def _swap_sibling(sh, rp):
    hs, rr = sh.shape[1] // 2, rp.shape[2]

    def copies(xin, xout, xsem):
        x, y, c, _ = _place()
        sib = (x, y, 1 - c)
        rows = xin[0].at[:, pl.ds(pl.multiple_of((1 - c) * hs, 8), hs)]
        return [_remote(rows, xout[0].at[:, pl.ds(0, hs)], xsem[0].at[0], xsem[1].at[0], sib),
                _remote(xin[1].at[:, 1 - c], xout[0].at[:, pl.ds(hs, rr)], xsem[0].at[1], xsem[1].at[1], sib)]

    def start(*a):
        for cp in copies(*a):
            cp.start()

    def finish(*a):
        for cp in copies(*a):
            cp.wait()

    return _Exchange([sh, rp], [jax.ShapeDtypeStruct((4, hs + rr, 128), F32)],
                     [pltpu.SemaphoreType.DMA((2,)), pltpu.SemaphoreType.DMA((2,))], start, finish)


def _scatter_chips(parts):
    n = len(parts)

    def copies(xin, xout, ssems, rsems):
        x, y, c, chips = _place()
        t = 2 * x + y
        return [_remote(xin[k].at[2 * cx + cy], xout[k].at[t], ssems.at[n * j + k], rsems.at[n * j + k], (cx, cy, c))
                for j, (cx, cy) in enumerate(chips) for k in range(n)]

    def start(xin, xout, xsem):
        for cp in copies(xin, xout, *xsem):
            cp.start()

    def finish(xin, xout, xsem):
        ssems, rsems = xsem
        x, y, c, chips = _place()
        for j, (cx, cy) in enumerate(chips):
            for k in range(n):
                got = xout[k].at[2 * cx + cy]
                _remote(got, got, ssems.at[n * j + k], rsems.at[n * j + k], (cx, cy, c)).wait_recv()
        for cp in copies(xin, xout, *xsem):
            cp.wait_send()

    return _Exchange(parts, [jax.ShapeDtypeStruct(a.shape, a.dtype) for a in parts],
                     [pltpu.SemaphoreType.DMA((3 * n,)), pltpu.SemaphoreType.DMA((3 * n,))], start, finish)


def _share_reduced(piece, eighth):
    def copies(t_ref, mine_r, sib_ref, rall_ref, ssems, rsems, lsem):
        x, y, c, _ = _place()
        me = 4 * x + 2 * y + c
        loc = pltpu.make_async_copy(mine_r, rall_ref.at[me], lsem)
        sends = [_remote(t_ref, sib_ref, ssems.at[0], rsems.at[0], (x, y, 1 - c))]
        peers = []
        for mask in range(1, 8):
            px = 1 - x if mask & 4 else x
            py = 1 - y if mask & 2 else y
            pc = 1 - c if mask & 1 else c
            peers.append((mask, px, py, pc))
            sends.append(_remote(mine_r, rall_ref.at[me], ssems.at[mask], rsems.at[mask], (px, py, pc)))
        return loc, sends, peers

    def start(xin, xout, xsem):
        loc, sends, _ = copies(*xin, *xout, *xsem)
        for cp in [loc] + sends:
            cp.start()

    def finish(xin, xout, xsem):
        (sib_ref, rall_ref), (ssems, rsems, _) = xout, xsem
        loc, sends, peers = copies(*xin, *xout, *xsem)
        x, y, c, _ = _place()
        _remote(sib_ref, sib_ref, ssems.at[0], rsems.at[0], (x, y, 1 - c)).wait_recv()
        for mask, px, py, pc in peers:
            got = rall_ref.at[4 * px + 2 * py + pc]
            _remote(got, got, ssems.at[mask], rsems.at[mask], (px, py, pc)).wait_recv()
        for cp in sends:
            cp.wait_send()
        loc.wait()

    return _Exchange([piece, eighth],
                     [jax.ShapeDtypeStruct(piece.shape, F32), jax.ShapeDtypeStruct((8, *eighth.shape), F32)],
                     [pltpu.SemaphoreType.DMA((8,)), pltpu.SemaphoreType.DMA((8,)), pltpu.SemaphoreType.DMA],
                     start, finish)


def _ceil_to(n, m):
    return -(-n // m) * m


def _pack_bf16(parts):
    blocks = [p.reshape(-1, 128) for p in parts]
    rows = jnp.concatenate([jnp.pad(b, ((0, -b.shape[0] % 16), (0, 0))) for b in blocks])
    hw = _ceil_to(rows.shape[0], 32) // 2
    return jnp.pad(rows, ((0, 2 * hw - rows.shape[0]), (0, 0))).reshape(2, hw, 128)


def _segments(wall, parts):
    wall = wall.reshape(4, -1, 128)
    out, row = [], 0
    for p in parts:
        n = p.size // 128
        out.append(wall[:, row:row + n].reshape(4, *p.shape))
        row += _ceil_to(n, 16)
    return out


class _GradReduce:
    def __init__(self, sharded, replicated, c_idx, wire_bf16=False):
        self.c_idx, self.wire_bf16 = c_idx, wire_bf16
        self.rowwise = [(n, g.shape[1:]) for n, g in sharded if math.prod(g.shape[1:]) % 128 == 0]
        self.small = [(n, g.shape[1:]) for n, g in sharded if math.prod(g.shape[1:]) % 128 != 0]
        self.replicated = [(n, g.shape[0]) for n, g in replicated]
        by_name = dict(sharded)
        blocks = [by_name[n].reshape(4, -1, 128) for n, _ in self.rowwise]
        if self.small:
            rest = jnp.concatenate([by_name[n].reshape(4, -1) for n, _ in self.small], axis=1)
            blocks.append(jnp.pad(rest, ((0, 0), (0, -rest.shape[1] % 128))).reshape(4, -1, 128))
        blocks = [jnp.pad(b, ((0, 0), (0, -b.shape[1] % 8), (0, 0))) for b in blocks]
        rows = sum(b.shape[1] for b in blocks)
        self.hs = _ceil_to(rows, 256) // 2
        sh = jnp.concatenate(blocks + [jnp.zeros((4, 2 * self.hs - rows, 128), F32)], axis=1)
        rp = jnp.concatenate([g for _, g in replicated])
        self.rr = _ceil_to(_ceil_to(rp.shape[0], 128) // 128, 64) // 8
        rp = jnp.pad(rp, (0, 8 * self.rr * 128 - rp.shape[0])).reshape(4, 2, self.rr, 128)

        self.sh, self.rp = sh, rp

    def swap(self):
        return _swap_sibling(self.sh, self.rp)

    def swapped(self, got):
        self.chip_sum = _add_own_half(self.sh, got, self.c_idx)
        mine_r = lax.dynamic_index_in_dim(self.rp, self.c_idx, axis=1, keepdims=False)
        self.chip_r = _add2(mine_r.reshape(-1, 128), got[:, self.hs:].reshape(-1, 128)).reshape(4, self.rr, 128)

    def scatter(self):
        return _scatter_chips([self.chip_sum.astype(BF16) if self.wire_bf16 else self.chip_sum, self.chip_r])

    def scattered(self, landed, landed_r):
        self.piece = _sum_slots(landed, self.chip_sum, self.hs)
        self.eighth = _sum_slots(landed_r, self.chip_r, self.rr)

    def share(self):
        return _share_reduced(self.piece, self.eighth)

    def shared(self, sibling, rall):
        mine, sib = self.piece, sibling
        self.shard = jnp.where(self.c_idx == 0, jnp.concatenate([mine, sib]), jnp.concatenate([sib, mine]))
        self.rall = rall

    def reduced(self):
        out, row = {}, 0
        for name, shape in self.rowwise:
            rows = math.prod(shape) // 128
            out[name] = self.shard[row:row + rows].reshape(shape)
            row += _ceil_to(rows, 8)
        for group, flat in ((self.small, self.shard[row:].reshape(-1)), (self.replicated, self.rall.reshape(-1))):
            off = 0
            for name, shape in group:
                n = math.prod(shape) if isinstance(shape, tuple) else shape
                out[name] = flat[off:off + n]
                off += n
        return out


def _col_shards(w2d):
    rows, cols = w2d.shape
    return w2d.reshape(rows, 4, cols // 4).transpose(1, 0, 2)


_WIN0_PARTS = ((0, 384, 1280), (384, 640, 1664), (640, 672, 1984), (672, 1696, 0), (1696, 1952, 1024))


def _win0_aligned(shards):
    def cols(a, b):
        return [shards[s][:, max(a, 488 * s) - 488 * s:min(b, 488 * (s + 1)) - 488 * s]
                for s in range(4) if max(a, 488 * s) < min(b, 488 * (s + 1))]

    zeros = jnp.zeros((1024, 64), shards.dtype)
    return jnp.concatenate(cols(672, 1696) + cols(1696, 1952) + cols(0, 384) + cols(384, 640)
                           + [zeros] + cols(640, 672) + [zeros[:, :32]], axis=1)


def _win0_shards(dwin0p):
    shards = []
    for s in range(4):
        lo, hi = 488 * s, 488 * (s + 1)
        cols = [dwin0p[:, p + max(lo, a) - a:p + min(hi, b) - a] for a, b, p in _WIN0_PARTS if max(lo, a) < min(hi, b)]
        shards.append(jnp.concatenate(cols, axis=1))
    return jnp.stack(shards)


def _block_diag4(w):
    eye = jnp.eye(4, dtype=w.dtype)
    return jnp.einsum("gaij,ab->gaibj", w.reshape(3, 4, 64, 64), eye).reshape(3, 256, 256)


def _diag_blocks4(w):
    w5 = w.reshape(3, 4, 64, 4, 64)
    return jnp.stack([w5[:, a, :, a, :] for a in range(4)], axis=1).reshape(12, 64, 64)


def kernel(x, mem, positions, mla_w_in, mla_q_norm, mla_w_uq, mla_kv_norm, mla_w_ukv, lru_w_in, lru_conv_w, lru_conv_b, lru_w_rgate, lru_b_rgate, lru_w_igate, lru_b_igate, lru_lambda, w_mem_kv, w_out, ln_g, ln_b, loss_target, m_mla_w_in, m_mla_q_norm, m_mla_w_uq, m_mla_kv_norm, m_mla_w_ukv, m_lru_w_in, m_lru_conv_w, m_lru_conv_b, m_lru_w_rgate, m_lru_b_rgate, m_lru_w_igate, m_lru_b_igate, m_lru_lambda, m_w_mem_kv, m_w_out, m_ln_g, m_ln_b, v_mla_w_in, v_mla_q_norm, v_mla_w_uq, v_mla_kv_norm, v_mla_w_ukv, v_lru_w_in, v_lru_conv_w, v_lru_conv_b, v_lru_w_rgate, v_lru_b_rgate, v_lru_w_igate, v_lru_b_igate, v_lru_lambda, v_w_mem_kv, v_w_out, v_ln_g, v_ln_b):
    s = x.shape[1]
    c_idx = lax.axis_index("c")
    x2, mem2, tgt2 = x[0], mem[0], loss_target[0]

    first = [p.astype(BF16) for p in (mla_w_in[0], mla_w_uq[0], mla_w_ukv[0])]
    buf = _pack_bf16(first)
    mla_shards = _segments(_gathered(_run_exchange(_gather_shards(buf), "gather_weights")[0], buf), first)

    mid = [w_mem_kv.astype(BF16), w_out[0].astype(BF16)]
    buf_mid = _pack_bf16(mid)

    def mid_weights(landed):
        wmem, wout0 = _segments(_gathered(landed[0], buf_mid), mid)
        return wmem.transpose(1, 0, 2, 3).reshape(2, 1024, 512), wout0.reshape(1024, 1024)

    small = jnp.concatenate([lru_conv_w[0].reshape(-1), lru_conv_b[0], lru_b_rgate[0], lru_b_igate[0], lru_lambda[0]])
    late = [lru_w_in[0].astype(BF16), w_out[1].astype(BF16), lax.bitcast_convert_type(small, BF16)]
    buf_late = _pack_bf16(late)

    def late_weights(landed):
        win1, wout1, small_bits = _segments(_gathered(landed[0], buf_late), late)
        small_all = lax.bitcast_convert_type(small_bits, F32)
        cw = small_all[:, :768].reshape(4, 4, 192).transpose(1, 0, 2).reshape(4, TOK_W)
        cb, br, bi, lam = (small_all[:, 768 + 192 * k:960 + 192 * k].reshape(1, TOK_W) for k in range(4))
        return win1, wout1.reshape(1024, 1024), cw, cb, br, bi, lam

    def reduce_late(g):
        return _GradReduce(
            [("lru_w_in", g["lru_w_in"]), ("lru_conv_w", _col_shards(g["lru_conv_w"])),
             ("lru_conv_b", _col_shards(g["lru_conv_b"])), ("lru_b_rgate", _col_shards(g["lru_b_rgate"])),
             ("lru_b_igate", _col_shards(g["lru_b_igate"])), ("lru_lambda", _col_shards(g["lru_lambda"])),
             ("w_mem_kv1", g["w_mem_kv1"].reshape(4, 256, 512)), ("w_out1", g["w_out1"].reshape(4, 256, 1024))],
            [("lru_w_rgate", g["lru_w_rgate"].reshape(-1)), ("lru_w_igate", g["lru_w_igate"].reshape(-1)),
             ("ln_g1", g["ln_g1"].reshape(-1)), ("ln_b1", g["ln_b1"].reshape(-1))], c_idx)

    g0, late_red = _local_step(
        x2, mem2, positions.reshape(s, 1), tgt2, *mla_shards, mla_q_norm, mla_kv_norm, lru_w_rgate[0], lru_w_igate[0],
        ln_g, ln_b, mid_weights, late_weights, _gather_shards(buf_mid), _gather_shards(buf_late), reduce_late)

    early_red = _GradReduce(
        [("mla_w_in", g0["mla_w_in"]), ("mla_w_uq", _col_shards(g0["mla_w_uq"])),
         ("mla_w_ukv", _col_shards(g0["mla_w_ukv"])), ("w_mem_kv0", g0["w_mem_kv0"].reshape(4, 256, 512)),
         ("w_out0", g0["w_out0"].reshape(4, 256, 1024))],
        [("mla_q_norm", g0["mla_q_norm"].reshape(-1)), ("mla_kv_norm", g0["mla_kv_norm"].reshape(-1)),
         ("ln_g0", g0["ln_g0"].reshape(-1)), ("ln_b0", g0["ln_b0"].reshape(-1)), ("loss", g0["loss"].reshape(-1))],
        c_idx, wire_bf16=True)
    early_red.swapped(*_run_exchange(early_red.swap(), "swap_sibling"))
    early_red.scattered(*_run_exchange(early_red.scatter(), "scatter_chips"))
    early_red.shared(*_run_exchange(early_red.share(), "share_reduced"))
    red = {**late_red.reduced(), **early_red.reduced()}
    red["w_mem_kv"] = jnp.concatenate([red["w_mem_kv0"], red["w_mem_kv1"]])
    red["w_out"] = jnp.concatenate([red["w_out0"], red["w_out1"]])
    red["ln_g"] = jnp.concatenate([red["ln_g0"], red["ln_g1"]])
    red["ln_b"] = jnp.concatenate([red["ln_b0"], red["ln_b1"]])

    weights = dict(mla_w_in=mla_w_in, mla_q_norm=mla_q_norm, mla_w_uq=mla_w_uq, mla_kv_norm=mla_kv_norm,
                   mla_w_ukv=mla_w_ukv, lru_w_in=lru_w_in, lru_conv_w=lru_conv_w, lru_conv_b=lru_conv_b,
                   lru_w_rgate=lru_w_rgate, lru_b_rgate=lru_b_rgate, lru_w_igate=lru_w_igate, lru_b_igate=lru_b_igate,
                   lru_lambda=lru_lambda, w_mem_kv=w_mem_kv, w_out=w_out, ln_g=ln_g, ln_b=ln_b)
    m_in = dict(mla_w_in=m_mla_w_in, mla_q_norm=m_mla_q_norm, mla_w_uq=m_mla_w_uq, mla_kv_norm=m_mla_kv_norm,
                mla_w_ukv=m_mla_w_ukv, lru_w_in=m_lru_w_in, lru_conv_w=m_lru_conv_w, lru_conv_b=m_lru_conv_b,
                lru_w_rgate=m_lru_w_rgate, lru_b_rgate=m_lru_b_rgate, lru_w_igate=m_lru_w_igate,
                lru_b_igate=m_lru_b_igate, lru_lambda=m_lru_lambda, w_mem_kv=m_w_mem_kv, w_out=m_w_out, ln_g=m_ln_g,
                ln_b=m_ln_b)
    v_in = dict(mla_w_in=v_mla_w_in, mla_q_norm=v_mla_q_norm, mla_w_uq=v_mla_w_uq, mla_kv_norm=v_mla_kv_norm,
                mla_w_ukv=v_mla_w_ukv, lru_w_in=v_lru_w_in, lru_conv_w=v_lru_conv_w, lru_conv_b=v_lru_conv_b,
                lru_w_rgate=v_lru_w_rgate, lru_b_rgate=v_lru_b_rgate, lru_w_igate=v_lru_w_igate,
                lru_b_igate=v_lru_b_igate, lru_lambda=v_lru_lambda, w_mem_kv=v_w_mem_kv, w_out=v_w_out, ln_g=v_ln_g,
                ln_b=v_ln_b)
    order = ["mla_w_in", "mla_q_norm", "mla_w_uq", "mla_kv_norm", "mla_w_ukv", "lru_w_in", "lru_conv_w", "lru_conv_b",
             "lru_w_rgate", "lru_b_rgate", "lru_w_igate", "lru_b_igate", "lru_lambda", "w_mem_kv", "w_out", "ln_g",
             "ln_b"]
    grads, deltas, new_m, new_v = {}, {}, {}, {}

    def operands(name):
        shape = weights[name].shape
        two_d = (math.prod(shape[:-1]), shape[-1])
        return [a.reshape(two_d) for a in (weights[name], red[name], m_in[name], v_in[name])]

    def keep(name, g2, d2, m2, v2):
        shape = weights[name].shape
        grads[name], deltas[name] = g2.reshape(shape), d2.reshape(shape)
        new_m[name], new_v[name] = m2.reshape(shape), v2.reshape(shape)

    small = [n for n in order if weights[n].size <= 4096]
    ops = [operands(n) for n in small]
    for name, op, res in zip(small, ops, _adamw_small(ops)):
        keep(name, op[1], *res)
    for name in order:
        if name not in small:
            op = operands(name)
            keep(name, op[1], *_adamw(name, *op))
    return (red["loss"][0], g0["x"][None], *[grads[n] for n in order], *[deltas[n] for n in order],
            *[new_m[n] for n in order], *[new_v[n] for n in order])


def _local_step(x2, mem2, pos_col, tgt2, win0_sh, wuq_sh, wukv_sh, gq, gkv, w_rgate, w_igate, ln_g, ln_b,
                mid_weights, late_weights, gather_mid=None, gather_late=None, reduce_late=None):
    s = x2.shape[0]
    win0p = _win0_aligned(win0_sh)
    wuq_p = jnp.pad(wuq_sh.reshape(4, Q_LORA, 3, 96).transpose(1, 0, 2, 3).reshape(Q_LORA, 12, 96),
                    ((0, 0), (0, 0), (0, 32))).reshape(Q_LORA, QK_W)
    wukv3 = wukv_sh.reshape(4, KV_LORA, 3, 128).transpose(1, 0, 2, 3).reshape(KV_LORA, 12, 128)
    wk_p = jnp.pad(wukv3[:, :, :64], ((0, 0), (0, 0), (0, 64))).reshape(KV_LORA, QK_W)
    wv = wukv3[:, :, 64:].reshape(KV_LORA, TOK_W)
    wr_bd = _block_diag4(w_rgate).astype(BF16)
    wi_bd = _block_diag4(w_igate).astype(BF16)
    half = 16
    inv_freq = ROPE_THETA ** (-jnp.arange(half, dtype=F32) / half)
    inv_lane = jnp.concatenate([jnp.zeros((64,), F32), inv_freq, inv_freq, jnp.zeros((32,), F32)]).reshape(1, HEAD_PAD)

    gate0, qm0, cq, ckv, q_p, q_t, k_p, v_b, v_t, ctab, satab, sbtab, *landed = _mla_proj_fwd(
        x2, win0p, gq, gkv, wuq_p, wk_p, wv, pos_col, inv_lane, exchange=gather_mid)
    wmem, wout0 = mid_weights(landed)
    memkv = _mem_kv(mem2, wmem)
    tok0, lse, *landed = _attn_fwd(q_p, k_p, v_t, exchange=gather_late)
    win1, wout1, cw, cb, br, bi, lam = late_weights(landed)
    g0, b0, g1, b1 = ln_g[0:1], ln_b[0:1], ln_g[1:2], ln_b[1:2]
    h1 = _mix_fwd(tok0, gate0, qm0, memkv[0], wout0, x2, g0, b0)
    u1, gate1, qm1, hs1 = _lru_fwd(h1, win1, cw, cb, wr_bd, br, wi_bd, bi, lam)

    dres1, dtok1, dgate1, dqm1, dwout1, dmemkv1, dg1, db1, loss = _mix_bwd(
        hs1, gate1, qm1, memkv[1], wout1, h1, g1, b1, tgt2, True)
    dh1, dwin1, dcw, dcb, dwr_bd, dbr, dwi_bd, dbi, dlam = _lru_bwd(
        dtok1, dgate1, dqm1, dres1, h1, u1, hs1, win1, cw, cb, wr_bd, br, wi_bd, bi, lam)
    late = {"lru_w_in": dwin1, "lru_conv_w": dcw, "lru_conv_b": dcb, "lru_b_rgate": dbr, "lru_b_igate": dbi,
            "lru_lambda": dlam, "w_mem_kv1": _mem_kv_bwd(mem2, dmemkv1), "w_out1": dwout1,
            "lru_w_rgate": _diag_blocks4(dwr_bd), "lru_w_igate": _diag_blocks4(dwi_bd), "ln_g1": dg1, "ln_b1": db1}
    red = reduce_late(late) if reduce_late is not None else None

    dres0, dob, dobt, stats, dgate0, dqm0, dwout0, dmemkv0, dg0, db0, _, *got = _mix_bwd(
        tok0, gate0, qm0, memkv[0], wout0, x2, g0, b0, dh1, False, lse=lse, exchange=red.swap() if red else None)
    if red:
        red.swapped(*got)
    dq_p, dk_t, dv_t, *got = _attn_bwd(q_p, q_t, k_p, v_b, dob, dobt, stats,
                                       exchange=red.scatter() if red else None)
    if red:
        red.scattered(*got)
    if red:
        red.shared(*_run_exchange(red.share(), "share_reduced"))
    dx, dwin0p, dwuq_p, dwk_p, dwv, dgq, dgkv = _mla_proj_bwd(
        x2, cq, ckv, dq_p, dk_t, dv_t, dgate0, dqm0, dres0, win0p, gq, gkv, wuq_p, wk_p, wv,
        ctab, satab, sbtab)

    dwin0 = _win0_shards(dwin0p)
    dwuq = dwuq_p.reshape(Q_LORA, 12, 128)[:, :, :96].reshape(Q_LORA, 1152)
    dwukv = jnp.concatenate([dwk_p.reshape(KV_LORA, 12, 128)[:, :, :64], dwv.reshape(KV_LORA, 12, 64)],
                            axis=2).reshape(KV_LORA, 1536)
    early = {"x": dx, "loss": loss, "mla_w_in": dwin0, "mla_w_uq": dwuq, "mla_w_ukv": dwukv,
             "w_mem_kv0": _mem_kv_bwd(mem2, dmemkv0), "w_out0": dwout0, "mla_q_norm": dgq, "mla_kv_norm": dgkv,
             "ln_g0": dg0, "ln_b0": db0}
    return early, (red if red else late)
```

```python
import functools
import math

import jax
import jax.numpy as jnp
from jax import lax
from jax.experimental import pallas as pl
from jax.experimental.pallas import tpu as pltpu

F32, BF16 = jnp.float32, jnp.bfloat16
MESH = pl.DeviceIdType.MESH

D_MODEL = 1024
N_TOK_HEADS = 12
TOK_W = 768
MEM_W = 256
MEM_LEN = 256
Q_LORA, KV_LORA = 384, 256
HEAD_PAD = 128
QK_W = N_TOK_HEADS * HEAD_PAD
ATT_SCALE = 1.0 / math.sqrt(96.0)
ATT_SCALE_LOG2 = ATT_SCALE * math.log2(math.e)
ROPE_THETA = 10000.0
LRU_C = 8.0
ALPHA = 4.0 ** 0.25
NORM_EPS = 1e-6
ADAM_LR, ADAM_B1, ADAM_B2, ADAM_EPS, ADAM_WD, ADAM_STEP = 0.001, 0.9, 0.999, 1e-08, 0.01, 10

TB_PROJ = 512
TB_PROJ_BWD = 512
TB_MIX = 512
TB_LRU = 256
TQ_ATT = 512
TQ_ATT_FWD = 1024
TK_ATT = 1024
VMEM_LIMIT = 56 * 1024 * 1024
VMEM_LIMIT_PROJ_BWD = 60 * 1024 * 1024


def _mm(a, b):
    return jnp.dot(a.astype(BF16), b.astype(BF16), preferred_element_type=F32)


def _mm_nt(a, b):
    return lax.dot_general(a.astype(BF16), b.astype(BF16), (((1,), (1,)), ((), ())), preferred_element_type=F32)


def _mm_tn(a, b):
    return lax.dot_general(a.astype(BF16), b.astype(BF16), (((0,), (0,)), ((), ())), preferred_element_type=F32)


def _rows(tb, w):
    return pl.BlockSpec((tb, w), lambda i: (i, 0))


def _const(shape):
    n = len(shape)
    return pl.BlockSpec(shape, lambda i: (0,) * n)


def _params(sem=("arbitrary",), vmem_limit=None):
    return pltpu.CompilerParams(dimension_semantics=sem, vmem_limit_bytes=vmem_limit or VMEM_LIMIT)


def _iota(shape, dim):
    return lax.broadcasted_iota(jnp.int32, shape, dim)


def _rope_tables(pos, inv_lane):
    ang = pos.astype(F32) * inv_lane
    lane = _iota(ang.shape, 1)
    cs, sn = jnp.cos(ang), jnp.sin(ang)
    return (jnp.where(lane < 64, 1.0, jnp.where(lane < 96, cs, 0.0)),
            jnp.where((lane >= 64) & (lane < 80), -sn, 0.0), jnp.where((lane >= 80) & (lane < 96), sn, 0.0))


def _rope(t, c, sa, sb):
    return t * c + pltpu.roll(t, 112, 1) * sa + pltpu.roll(t, 16, 1) * sb


def _rope_t(d, c, sa, sb):
    return d * c + pltpu.roll(d * sa, 16, 1) + pltpu.roll(d * sb, 112, 1)


def _rms(c, g):
    r = lax.rsqrt(jnp.mean(c * c, axis=-1, keepdims=True) + NORM_EPS)
    xh = c * r
    return xh * g, xh, r


def _mla_proj_fwd(x, win, gq, gkv, wuq, wukv_k, wukv_v, pos_col, inv_lane, exchange=None):
    s = x.shape[0]
    tb = min(TB_PROJ, s)

    def body(x_ref, win_ref, gq_ref, gkv_ref, wuq_ref, wk_ref, wv_ref, pos_ref, inv_ref,
             gate_ref, qm_ref, cq_ref, ckv_ref, q_ref, qt_ref, k_ref, v_ref, vt_ref, c_ref, sa_ref, sb_ref):
        z = _mm(x_ref[...], win_ref[...])
        gate_ref[...] = z[:, 0:1024]
        qm_ref[...] = z[:, 1024:1280].astype(BF16)
        cq = z[:, 1280:1664]
        ckv = z[:, 1664:1920]
        cq_ref[...] = cq
        ckv_ref[...] = ckv
        c, sa, sb = _rope_tables(pos_ref[...], inv_ref[...])
        c_ref[...], sa_ref[...], sb_ref[...] = c, sa, sb
        nq, _, _ = _rms(cq, gq_ref[...])
        nkv, _, _ = _rms(ckv, gkv_ref[...])
        qf = _mm(nq, wuq_ref[...])
        kf = _mm(nkv, wk_ref[...])
        vf = _mm(nkv, wv_ref[...])
        v_ref[...] = vf.astype(BF16)
        for j in range(N_TOK_HEADS // 2):
            sl = slice(HEAD_PAD * j, HEAD_PAD * (j + 1))
            vt_ref[sl, :] = vf[:, sl].T.astype(BF16)
        kr = _rope(z[:, 1920:2048], c, sa, sb)
        for h in range(N_TOK_HEADS):
            sl = slice(HEAD_PAD * h, HEAD_PAD * (h + 1))
            qh = _rope(qf[:, sl], c, sa, sb) * ATT_SCALE_LOG2
            q_ref[:, sl] = qh.astype(BF16)
            qt_ref[sl, :] = qh.T.astype(BF16)
            k_ref[:, sl] = (kf[:, sl] + kr).astype(BF16)

    outs = (jax.ShapeDtypeStruct((s, 1024), F32), jax.ShapeDtypeStruct((s, MEM_W), BF16),
            jax.ShapeDtypeStruct((s, Q_LORA), F32), jax.ShapeDtypeStruct((s, KV_LORA), F32),
            jax.ShapeDtypeStruct((s, QK_W), BF16), jax.ShapeDtypeStruct((QK_W, s), BF16),
            jax.ShapeDtypeStruct((s, QK_W), BF16),
            jax.ShapeDtypeStruct((s, TOK_W), BF16), jax.ShapeDtypeStruct((TOK_W, s), BF16),
            *[jax.ShapeDtypeStruct((s, HEAD_PAD), F32)] * 3)

    def cols(w):
        return pl.BlockSpec((w, tb), lambda i: (0, i))

    return _run(
        body, name="mla_proj_fwd", grid=(s // tb,), out_shape=outs,
        in_specs=[_rows(tb, 1024), _const((1024, 2048)), _const((1, Q_LORA)), _const((1, KV_LORA)),
                  _const((Q_LORA, QK_W)), _const((KV_LORA, QK_W)), _const((KV_LORA, TOK_W)),
                  _rows(tb, 1), _const((1, HEAD_PAD))],
        out_specs=(_rows(tb, 1024), _rows(tb, MEM_W), _rows(tb, Q_LORA), _rows(tb, KV_LORA),
                   _rows(tb, QK_W), cols(QK_W), _rows(tb, QK_W), _rows(tb, TOK_W), cols(TOK_W),
                   _rows(tb, HEAD_PAD), _rows(tb, HEAD_PAD), _rows(tb, HEAD_PAD)),
        args=(x, win, gq, gkv, wuq, wukv_k, wukv_v, pos_col, inv_lane), sem=("parallel",), exchange=exchange)


def _mla_proj_bwd(x, cq, ckv, dq, dkt, dvt, dgate, dqm, dres, win, gq, gkv, wuq, wukv_k, wukv_v, ctab, satab, sbtab):
    s = x.shape[0]
    tb = min(TB_PROJ_BWD, s)

    def body(x_ref, cq_ref, ckv_ref, dq_ref, dkt_ref, dvt_ref, dgate_ref, dqm_ref, dres_ref, win_ref, gq_ref, gkv_ref,
             wuq_ref, wk_ref, wv_ref, c_ref, sa_ref, sb_ref,
             dx_ref, dwin_ref, dwuq_ref, dwk_ref, dwv_ref, dgq_ref, dgkv_ref):
        @pl.when(pl.program_id(0) == 0)
        def _():
            for r in (dwin_ref, dwuq_ref, dwk_ref, dwv_ref, dgq_ref, dgkv_ref):
                r[...] = jnp.zeros_like(r)

        c, sa, sb = c_ref[...], sa_ref[...], sb_ref[...]
        lane = _iota((tb, HEAD_PAD), 1)
        gq, gkv = gq_ref[...], gkv_ref[...]
        nq, xhq, rq = _rms(cq_ref[...], gq)
        nkv, xhk, rk = _rms(ckv_ref[...], gkv)
        dkp = dkt_ref[...].T * math.log(2.0)
        dqs, dkr = [], jnp.zeros((tb, HEAD_PAD), F32)
        for h in range(N_TOK_HEADS):
            sl = slice(HEAD_PAD * h, HEAD_PAD * (h + 1))
            dqs.append(_rope_t(dq_ref[:, sl], c, sa, sb).astype(BF16))
            dkr = dkr + dkp[:, sl]
        dqf = jnp.concatenate(dqs, axis=1)
        dkr = jnp.where((lane >= 64) & (lane < 96), _rope_t(dkr, c, sa, sb), 0.0)
        dvb = dvt_ref[...].T.astype(BF16)
        dkb = dkp.astype(BF16)
        dnq = _mm_nt(dqf, wuq_ref[...])
        dwuq_ref[...] += _mm_tn(nq, dqf)
        dgq_ref[...] += jnp.sum(dnq * xhq, axis=0, keepdims=True)
        dxh = dnq * gq
        dcq = rq * (dxh - xhq * jnp.mean(dxh * xhq, axis=-1, keepdims=True))
        dnkv = _mm_nt(dkb, wk_ref[...]) + _mm_nt(dvb, wv_ref[...])
        nkvb = nkv.astype(BF16)
        dwk_ref[...] += _mm_tn(nkvb, dkb)
        dwv_ref[...] += _mm_tn(nkvb, dvb)
        dgkv_ref[...] += jnp.sum(dnkv * xhk, axis=0, keepdims=True)
        dxh = dnkv * gkv
        dckv = rk * (dxh - xhk * jnp.mean(dxh * xhk, axis=-1, keepdims=True))
        dz = jnp.concatenate([dgate_ref[...], dqm_ref[...], dcq, dckv, dkr], axis=1).astype(BF16)
        dx_ref[...] = _mm_nt(dz, win_ref[...]) + dres_ref[...]
        dwin_ref[...] += _mm_tn(x_ref[...], dz)

    outs = (jax.ShapeDtypeStruct((s, 1024), F32), jax.ShapeDtypeStruct((1024, 2048), F32),
            jax.ShapeDtypeStruct((Q_LORA, QK_W), F32), jax.ShapeDtypeStruct((KV_LORA, QK_W), F32),
            jax.ShapeDtypeStruct((KV_LORA, TOK_W), F32), jax.ShapeDtypeStruct((1, Q_LORA), F32),
            jax.ShapeDtypeStruct((1, KV_LORA), F32))
    return _run(
        body, name="mla_proj_bwd", grid=(s // tb,), out_shape=outs,
        in_specs=[_rows(tb, 1024), _rows(tb, Q_LORA), _rows(tb, KV_LORA), _rows(tb, QK_W),
                  pl.BlockSpec((QK_W, tb), lambda i: (0, i)), pl.BlockSpec((TOK_W, tb), lambda i: (0, i)),
                  _rows(tb, 1024), _rows(tb, MEM_W), _rows(tb, 1024),
                  _const((1024, 2048)), _const((1, Q_LORA)), _const((1, KV_LORA)),
                  _const((Q_LORA, QK_W)), _const((KV_LORA, QK_W)), _const((KV_LORA, TOK_W)),
                  _rows(tb, HEAD_PAD), _rows(tb, HEAD_PAD), _rows(tb, HEAD_PAD)],
        out_specs=(_rows(tb, 1024), _const((1024, 2048)), _const((Q_LORA, QK_W)), _const((KV_LORA, QK_W)),
                   _const((KV_LORA, TOK_W)), _const((1, Q_LORA)), _const((1, KV_LORA))),
        args=(x, cq, ckv, dq, dkt, dvt, dgate, dqm, dres, win, gq, gkv, wuq, wukv_k, wukv_v, ctab, satab, sbtab),
        sem=("arbitrary",), vmem_limit=VMEM_LIMIT_PROJ_BWD)


def _attn_fwd(q, k, vt, exchange=None):
    s = q.shape[0]
    tq = min(TQ_ATT_FWD, s)
    tk = min(TK_ATT, s)

    def body(q_ref, k_ref, vt_ref, o_ref, lse_ref):
        i = pl.program_id(1)
        nfull = (i * tq) // tk
        krow = _iota((tk, tq), 0)
        qpos = i * tq + _iota((tk, tq), 1)

        ones = jnp.ones((16, tk), BF16)

        def head_tile(hh, st, carry, masked):
            hs = slice(HEAD_PAD * hh, HEAD_PAD * (hh + 1))
            m, acc = carry
            sc = _mm_nt(k_ref[pl.ds(st, tk), hs], q_ref[:, hs])
            if masked:
                sc = jnp.where(st + krow <= qpos, sc, -jnp.inf)
            m_new = jnp.maximum(m, jnp.max(sc, axis=0, keepdims=True))
            p = jnp.exp2(sc - m_new).astype(BF16)
            a = jnp.exp2(m - m_new)
            va = jnp.concatenate([vt_ref[64 * hh:64 * (hh + 1), pl.ds(st, tk)], ones], axis=0)
            return m_new, a * acc + jnp.dot(va, p, preferred_element_type=F32)

        def tile(j, carry, masked):
            st = pl.multiple_of(j * tk, tk)
            return tuple(head_tile(hh, st, carry[hh], masked) for hh in range(2))

        def init():
            return (jnp.full((1, tq), -jnp.inf, F32), jnp.zeros((80, tq), F32))

        carry = lax.fori_loop(0, nfull, functools.partial(tile, masked=False), (init(), init()))
        (ma, acca), (mb, accb) = tile(nfull, carry, True)
        la, lb = acca[64:65], accb[64:65]
        o_ref[...] = jnp.concatenate([acca[:64] / la, accb[:64] / lb], axis=0).T
        lse_ref[...] = jnp.concatenate([jnp.broadcast_to(ma + jnp.log2(la), (64, tq)),
                                        jnp.broadcast_to(mb + jnp.log2(lb), (64, tq))], axis=0).T

    shp = jax.ShapeDtypeStruct((s, TOK_W), F32)
    return _run(
        body, name="attn_fwd", grid=(N_TOK_HEADS // 2, s // tq), out_shape=(shp, shp),
        in_specs=[pl.BlockSpec((tq, 2 * HEAD_PAD), lambda j, i: (i, j)),
                  pl.BlockSpec((s, 2 * HEAD_PAD), lambda j, i: (0, j)),
                  pl.BlockSpec((HEAD_PAD, s), lambda j, i: (j, 0))],
        out_specs=(pl.BlockSpec((tq, HEAD_PAD), lambda j, i: (i, j)),) * 2,
        args=(q, k, vt), sem=("parallel", "arbitrary"), exchange=exchange)


def _attn_stats(o, do, lse_ref, dob_ref, dot_ref, st_ref):
    lane = _iota((o.shape[0], HEAD_PAD), 1)
    dob_ref[...] = do.astype(BF16)
    prod = do * o
    for j in range(N_TOK_HEADS // 2):
        sl = slice(HEAD_PAD * j, HEAD_PAD * (j + 1))
        dot_ref[sl, :] = do[:, sl].T.astype(BF16)
        pj = prod[:, sl]
        da = jnp.sum(jnp.where(lane < 64, pj, 0.0), axis=-1, keepdims=True)
        db = jnp.sum(jnp.where(lane >= 64, pj, 0.0), axis=-1, keepdims=True)
        la = lse_ref[:, HEAD_PAD * j:HEAD_PAD * j + 1]
        lb = lse_ref[:, HEAD_PAD * j + 64:HEAD_PAD * j + 65]
        st_ref[j] = jnp.where(lane == 0, la, jnp.where(lane == 1, lb, jnp.where(lane == 2, da,
                                                                                 jnp.where(lane == 3, db, 0.0))))


def _attn_bwd(q, qt, k, v, dob, dobt, stats, exchange=None):
    s = q.shape[0]
    t = min(TQ_ATT, s)
    nq = s // t

    def body(q_ref, qt_ref, do_ref, dot_ref, st_ref, k_ref, v_ref, dq_ref, dkt_ref, dvt_ref):
        i = pl.program_id(1)

        @pl.when(i == 0)
        def _():
            dkt_ref[...] = jnp.zeros_like(dkt_ref)
            dvt_ref[...] = jnp.zeros_like(dvt_ref)

        lane = _iota((t, HEAD_PAD), 1)
        qpos, kcol = _iota((t, t), 0), _iota((t, t), 1)
        do = do_ref[...]
        stats = st_ref[0]

        def head_tile(hh, ks, dq_acc, masked):
            hs = slice(HEAD_PAD * hh, HEAD_PAD * (hh + 1))
            qh = q_ref[:, hs]
            kh = k_ref[pl.ds(ks, t), hs]
            dom = jnp.where((lane < 64) if hh == 0 else (lane >= 64), do, jnp.zeros_like(do))
            lse = stats[:, hh:hh + 1]
            dlt = stats[:, 2 + hh:3 + hh]
            sc = _mm_nt(qh, kh)
            if masked:
                sc = jnp.where(kcol <= qpos, sc, -jnp.inf)
            p = jnp.exp2(sc - lse)
            dp = _mm_nt(dom, v_ref[pl.ds(ks, t), :])
            ds = (p * (dp - dlt)).astype(BF16)
            dvt_ref[64 * hh:64 * (hh + 1), pl.ds(ks, t)] += _mm(dot_ref[64 * hh:64 * (hh + 1), :], p)
            dkt_ref[HEAD_PAD * hh:HEAD_PAD * hh + 96, pl.ds(ks, t)] += _mm(qt_ref[HEAD_PAD * hh:HEAD_PAD * hh + 96, :], ds)
            return dq_acc + _mm(ds, kh)

        def tile(j, carry, masked):
            ks = pl.multiple_of(j * t, t)
            return tuple(head_tile(hh, ks, carry[hh], masked) for hh in range(2))

        zero = jnp.zeros((t, HEAD_PAD), F32)
        carry = lax.fori_loop(0, i, functools.partial(tile, masked=False), (zero, zero))
        dqa, dqb = tile(i, carry, True)
        dq_ref[...] = jnp.concatenate([dqa, dqb], axis=1) * ATT_SCALE

    return _run(
        body, name="attn_bwd", grid=(N_TOK_HEADS // 2, nq),
        out_shape=(jax.ShapeDtypeStruct((s, QK_W), F32), jax.ShapeDtypeStruct((QK_W, s), F32),
                   jax.ShapeDtypeStruct((TOK_W, s), F32)),
        in_specs=[pl.BlockSpec((t, 2 * HEAD_PAD), lambda j, i: (i, j)),
                  pl.BlockSpec((2 * HEAD_PAD, t), lambda j, i: (j, i)),
                  pl.BlockSpec((t, HEAD_PAD), lambda j, i: (i, j)),
                  pl.BlockSpec((HEAD_PAD, t), lambda j, i: (j, i)),
                  pl.BlockSpec((1, t, HEAD_PAD), lambda j, i: (j, i, 0)),
                  pl.BlockSpec((s, 2 * HEAD_PAD), lambda j, i: (0, j)),
                  pl.BlockSpec((s, HEAD_PAD), lambda j, i: (0, j))],
        out_specs=(pl.BlockSpec((t, 2 * HEAD_PAD), lambda j, i: (i, j)),
                   pl.BlockSpec((2 * HEAD_PAD, s), lambda j, i: (j, 0)),
                   pl.BlockSpec((HEAD_PAD, s), lambda j, i: (j, 0))),
        args=(q, qt, dob, dobt, stats, k, v), sem=("parallel", "arbitrary"), exchange=exchange)


def _mem_kv(mem, wmem):
    def body(m_ref, w_ref, o_ref):
        o_ref[0] = _mm(m_ref[...], w_ref[0]).astype(BF16)

    return pl.pallas_call(
        body, name="mem_kv", grid=(2,), out_shape=jax.ShapeDtypeStruct((2, MEM_LEN, 512), BF16),
        in_specs=[_const((MEM_LEN, 1024)), pl.BlockSpec((1, 1024, 512), lambda l: (l, 0, 0))],
        out_specs=pl.BlockSpec((1, MEM_LEN, 512), lambda l: (l, 0, 0)),
        compiler_params=_params(("parallel",)),
    )(mem, wmem)


def _mem_kv_bwd(mem, dmemkv):
    def body(m_ref, d_ref, o_ref):
        o_ref[...] = _mm_tn(m_ref[...], d_ref[...])

    return pl.pallas_call(
        body, name="mem_kv_bwd", grid=(1,), out_shape=jax.ShapeDtypeStruct((1024, 512), F32),
        in_specs=[_const((MEM_LEN, 1024)), _const((MEM_LEN, 512))], out_specs=_const((1024, 512)),
        compiler_params=_params(("arbitrary",)),
    )(mem, dmemkv)


def _head_mask(lane, sub):
    return (lane < 64) if sub == 0 else (lane >= 64)


def _mem_attn(qm, kv):
    tb = qm.shape[0]
    lane = _iota((tb, HEAD_PAD), 1)
    outs, ps = [], []
    for pp in range(2):
        qp = qm[:, HEAD_PAD * pp:HEAD_PAD * (pp + 1)]
        kp = kv[:, HEAD_PAD * pp:HEAD_PAD * (pp + 1)]
        vp = kv[:, MEM_W + HEAD_PAD * pp:MEM_W + HEAD_PAD * (pp + 1)]
        pair = None
        for sub in range(2):
            qh = jnp.where(_head_mask(lane, sub), qp, jnp.zeros_like(qp))
            sc = _mm_nt(qh, kp) * 0.125
            e = jnp.exp(sc - jnp.max(sc, axis=-1, keepdims=True))
            p = e / jnp.sum(e, axis=-1, keepdims=True)
            o = _mm(p, vp)
            ps.append(p)
            pair = o if sub == 0 else jnp.where(lane < 64, pair, o)
        outs.append(pair)
    return jnp.concatenate(outs, axis=1), ps


def _mem_attn_bwd(dmo, qm, kv, ps):
    tb = qm.shape[0]
    lane = _iota((tb, HEAD_PAD), 1)
    dqs, dks, dvs = [], [], []
    for pp in range(2):
        qp = qm[:, HEAD_PAD * pp:HEAD_PAD * (pp + 1)]
        kp = kv[:, HEAD_PAD * pp:HEAD_PAD * (pp + 1)]
        vp = kv[:, MEM_W + HEAD_PAD * pp:MEM_W + HEAD_PAD * (pp + 1)]
        dop = dmo[:, HEAD_PAD * pp:HEAD_PAD * (pp + 1)]
        dq_pair, dk_pair, dv_pair = None, None, None
        for sub in range(2):
            msk = _head_mask(lane, sub)
            p = ps[2 * pp + sub]
            qh = jnp.where(msk, qp, jnp.zeros_like(qp))
            doh = jnp.where(msk, dop, 0.0).astype(BF16)
            dv = _mm_tn(p, doh)
            dp = _mm_nt(doh, vp)
            ds = (p * (dp - jnp.sum(dp * p, axis=-1, keepdims=True)) * 0.125).astype(BF16)
            dq = _mm(ds, kp)
            dk = _mm_tn(ds, qh)
            if sub == 0:
                dq_pair, dk_pair, dv_pair = dq, dk, dv
            else:
                dq_pair = jnp.where(lane < 64, dq_pair, dq)
                dk_pair, dv_pair = dk_pair + dk, dv_pair + dv
        dqs.append(dq_pair)
        dks.append(dk_pair)
        dvs.append(dv_pair)
    return jnp.concatenate(dqs, axis=1), jnp.concatenate(dks + dvs, axis=1)


def _mix_core(tok, gate, qm, kv, wout, h_in, g, b):
    mem_out, ps = _mem_attn(qm, kv)
    cat = jnp.concatenate([tok, mem_out], axis=1)
    sg = jax.nn.sigmoid(gate)
    sl = gate * sg
    y = cat * sl
    r = ALPHA * h_in + _mm(y, wout)
    mu = jnp.mean(r, axis=-1, keepdims=True)
    xc = r - mu
    rstd = lax.rsqrt(jnp.mean(xc * xc, axis=-1, keepdims=True) + NORM_EPS)
    xh = xc * rstd
    return xh * g + b, (ps, cat, sg, sl, y, xh, rstd)


def _mix_fwd(tok, gate, qm, kv, wout, h_in, g, b):
    s = tok.shape[0]
    tb = min(TB_MIX, s)

    def body(tok_ref, gate_ref, qm_ref, kv_ref, w_ref, h_ref, g_ref, b_ref, o_ref):
        o_ref[...], _ = _mix_core(tok_ref[...], gate_ref[...], qm_ref[...], kv_ref[...], w_ref[...], h_ref[...],
                                  g_ref[...], b_ref[...])

    return pl.pallas_call(
        body, name="mix_fwd", grid=(s // tb,), out_shape=jax.ShapeDtypeStruct((s, 1024), F32),
        in_specs=[_rows(tb, TOK_W), _rows(tb, 1024), _rows(tb, MEM_W), _const((MEM_LEN, 512)), _const((1024, 1024)),
                  _rows(tb, 1024), _const((1, 1024)), _const((1, 1024))],
        out_specs=_rows(tb, 1024), compiler_params=_params(("parallel",)),
    )(tok, gate, qm, kv, wout, h_in, g, b)


def _mix_bwd(tok, gate, qm, kv, wout, h_in, g, b, up, from_loss, lse=None, exchange=None):
    s = tok.shape[0]
    tb = min(TB_MIX, s)
    n_in = 9 if lse is None else 10
    n_tok_out = 1 if lse is None else 3

    def body(*refs):
        tok_ref, gate_ref, qm_ref, kv_ref, w_ref, h_ref, g_ref, b_ref, up_ref = refs[:9]
        dres_ref, tok_out = refs[n_in], refs[n_in + 1:n_in + 1 + n_tok_out]
        dgate_ref, dqm_ref, dw_ref, dkv_ref, dg_ref, db_ref, loss_ref = refs[n_in + 1 + n_tok_out:]

        @pl.when(pl.program_id(0) == 0)
        def _():
            for r in (dw_ref, dkv_ref, dg_ref, db_ref, loss_ref):
                r[...] = jnp.zeros_like(r)

        gate, qm, kv, wout, g = gate_ref[...], qm_ref[...], kv_ref[...], w_ref[...], g_ref[...]
        h_out, (ps, cat, sg, sl, y, xh, rstd) = _mix_core(tok_ref[...], gate, qm, kv, wout, h_ref[...], g, b_ref[...])
        if from_loss:
            diff = h_out - up_ref[...]
            loss_ref[...] += 0.5 * jnp.sum(jnp.mean(diff * diff, axis=-1, keepdims=True), axis=0, keepdims=True)
            dh = diff * (1.0 / D_MODEL)
        else:
            dh = up_ref[...]
        dg_ref[...] += jnp.sum(dh * xh, axis=0, keepdims=True)
        db_ref[...] += jnp.sum(dh, axis=0, keepdims=True)
        dxh = dh * g
        dr = rstd * (dxh - jnp.mean(dxh, axis=-1, keepdims=True) - xh * jnp.mean(dxh * xh, axis=-1, keepdims=True))
        dres_ref[...] = ALPHA * dr
        drb = dr.astype(BF16)
        dy = _mm_nt(drb, wout)
        dw_ref[...] += _mm_tn(y, drb)
        dcat = dy * sl
        dgate_ref[...] = dy * cat * (sg * (1.0 + gate * (1.0 - sg)))
        if lse is None:
            tok_out[0][...] = dcat[:, :TOK_W]
        else:
            _attn_stats(tok_ref[...], dcat[:, :TOK_W], refs[9], *tok_out)
        dqm, dkv = _mem_attn_bwd(dcat[:, TOK_W:], qm, kv, ps)
        dqm_ref[...] = dqm
        dkv_ref[...] += dkv

    npair = N_TOK_HEADS // 2
    tok_shapes = [jax.ShapeDtypeStruct((s, TOK_W), F32)] if lse is None else [
        jax.ShapeDtypeStruct((s, TOK_W), BF16), jax.ShapeDtypeStruct((TOK_W, s), BF16),
        jax.ShapeDtypeStruct((npair, s, HEAD_PAD), F32)]
    tok_specs = [_rows(tb, TOK_W)] if lse is None else [
        _rows(tb, TOK_W), pl.BlockSpec((TOK_W, tb), lambda i: (0, i)),
        pl.BlockSpec((npair, tb, HEAD_PAD), lambda i: (0, i, 0))]
    outs = (jax.ShapeDtypeStruct((s, 1024), F32), *tok_shapes,
            jax.ShapeDtypeStruct((s, 1024), F32), jax.ShapeDtypeStruct((s, MEM_W), F32),
            jax.ShapeDtypeStruct((1024, 1024), F32), jax.ShapeDtypeStruct((MEM_LEN, 512), F32),
            jax.ShapeDtypeStruct((1, 1024), F32), jax.ShapeDtypeStruct((1, 1024), F32),
            jax.ShapeDtypeStruct((1, 1), F32))
    args = (tok, gate, qm, kv, wout, h_in, g, b, up) + (() if lse is None else (lse,))
    return _run(
        body, name="mix_bwd_loss" if from_loss else "mix_bwd", grid=(s // tb,), out_shape=outs,
        in_specs=[_rows(tb, TOK_W), _rows(tb, 1024), _rows(tb, MEM_W), _const((MEM_LEN, 512)), _const((1024, 1024)),
                  _rows(tb, 1024), _const((1, 1024)), _const((1, 1024)), _rows(tb, 1024)]
        + ([] if lse is None else [_rows(tb, TOK_W)]),
        out_specs=(_rows(tb, 1024), *tok_specs, _rows(tb, 1024), _rows(tb, MEM_W), _const((1024, 1024)),
                   _const((MEM_LEN, 512)), _const((1, 1024)), _const((1, 1024)), _const((1, 1))),
        args=args, sem=("arbitrary",), exchange=exchange)


def _shift_down(u, tail, k):
    if k == 0:
        return u
    r = pltpu.roll(u, k, 0)
    row8 = _iota((8, u.shape[1]), 0)
    head = jnp.where(row8 < k, pltpu.roll(tail, k, 0), r[:8])
    return jnp.concatenate([head, r[8:]], axis=0)


def _shift_up(d, head, k):
    if k == 0:
        return d
    n = d.shape[0]
    r = pltpu.roll(d, n - k, 0)
    row8 = _iota((8, d.shape[1]), 0)
    last = jnp.where(row8 >= 8 - k, pltpu.roll(head, 8 - k, 0), r[n - 8:])
    return jnp.concatenate([r[:n - 8], last], axis=0)


def _scan_down(a, b):
    n = a.shape[0]
    row = _iota(a.shape, 0)
    s = 1
    while s < n:
        ok = row >= s
        a_s = jnp.where(ok, pltpu.roll(a, s, 0), 1.0)
        b_s = jnp.where(ok, pltpu.roll(b, s, 0), 0.0)
        b = a * b_s + b
        a = a * a_s
        s *= 2
    return a, b


def _scan_up(a, b):
    n = a.shape[0]
    row = _iota(a.shape, 0)
    s = 1
    while s < n:
        ok = row < n - s
        a_s = jnp.where(ok, pltpu.roll(a, n - s, 0), 1.0)
        b_s = jnp.where(ok, pltpu.roll(b, n - s, 0), 0.0)
        b = a * b_s + b
        a = a * a_s
        s *= 2
    return a, b


def _softplus(x):
    return jnp.maximum(x, 0.0) + jnp.log(1.0 + jnp.exp(-jnp.abs(x)))


def _lru_gates(u, tail, cw, cb, wr, br, wi, bi, lam):
    us = [_shift_down(u, tail, k) for k in range(4)]
    xc = cb + us[3] * cw[0:1] + us[2] * cw[1:2] + us[1] * cw[2:3] + us[0] * cw[3:4]
    xb = xc.astype(BF16)
    pre_r = jnp.concatenate([_mm(xb[:, 256 * g:256 * (g + 1)], wr[g]) for g in range(3)], axis=1) + br
    pre_i = jnp.concatenate([_mm(xb[:, 256 * g:256 * (g + 1)], wi[g]) for g in range(3)], axis=1) + bi
    rg, ig = jax.nn.sigmoid(pre_r), jax.nn.sigmoid(pre_i)
    clam = -LRU_C * _softplus(-lam)
    la = clam * rg
    a = jnp.exp(la)
    mm = jnp.sqrt(-jnp.tanh(la) * (a * a + 1.0))
    return us, xc, xb, rg, ig, clam, la, a, mm


def _lru_fwd(h, win, cw, cb, wr, br, wi, bi, lam):
    s = h.shape[0]
    tb = min(TB_LRU, s)

    def body(h_ref, win_ref, cw_ref, cb_ref, wr_ref, br_ref, wi_ref, bi_ref, lam_ref,
             u_ref, gate_ref, qm_ref, hs_ref, tail_sc, carry_sc):
        @pl.when(pl.program_id(0) == 0)
        def _():
            tail_sc[...] = jnp.zeros_like(tail_sc)
            carry_sc[...] = jnp.zeros_like(carry_sc)

        hb = h_ref[...].astype(BF16)
        z = jnp.concatenate([_mm(hb, win_ref[sh]) for sh in range(4)], axis=1)
        u = z[:, :TOK_W]
        u_ref[...] = u
        gate_ref[...] = z[:, TOK_W:TOK_W + 1024]
        qm_ref[...] = z[:, TOK_W + 1024:].astype(BF16)
        _, xc, _, _, ig, _, _, a, mm = _lru_gates(u, tail_sc[...], cw_ref[...], cb_ref[...], wr_ref[...], br_ref[...],
                                                 wi_ref[...], bi_ref[...], lam_ref[...])
        big_a, big_b = _scan_down(a, mm * (ig * xc))
        hs = big_a * carry_sc[0:1, :] + big_b
        hs_ref[...] = hs
        tail_sc[...] = u[tb - 8:, :]
        carry_sc[...] = jnp.broadcast_to(hs[tb - 1:tb, :], carry_sc.shape)

    outs = (jax.ShapeDtypeStruct((s, TOK_W), F32), jax.ShapeDtypeStruct((s, 1024), F32),
            jax.ShapeDtypeStruct((s, MEM_W), BF16), jax.ShapeDtypeStruct((s, TOK_W), F32))
    return pl.pallas_call(
        body, name="lru_fwd", grid=(s // tb,), out_shape=outs,
        in_specs=[_rows(tb, 1024), _const((4, 1024, 512)), _const((4, TOK_W)), _const((1, TOK_W)),
                  _const((3, 256, 256)), _const((1, TOK_W)), _const((3, 256, 256)), _const((1, TOK_W)),
                  _const((1, TOK_W))],
        out_specs=(_rows(tb, TOK_W), _rows(tb, 1024), _rows(tb, MEM_W), _rows(tb, TOK_W)),
        scratch_shapes=[pltpu.VMEM((8, TOK_W), F32), pltpu.VMEM((8, TOK_W), F32)],
        compiler_params=_params(),
    )(h, win, cw, cb, wr, br, wi, bi, lam)


def _lru_bwd(dhs, dgate, dqm, dres, h, u, hs, win, cw, cb, wr, br, wi, bi, lam):
    s = h.shape[0]
    tb = min(TB_LRU, s)
    nb = s // tb

    def rev(w):
        return pl.BlockSpec((tb, w), lambda i: (nb - 1 - i, 0))

    def prev_tail(w):
        return pl.BlockSpec((8, w), lambda i: (jnp.maximum((nb - 1 - i) * (tb // 8) - 1, 0), 0))

    def body(dhs_ref, dgate_ref, dqm_ref, dres_ref, h_ref, u_ref, hs_ref, ut_ref, hst_ref, win_ref, cw_ref, cb_ref,
             wr_ref, br_ref, wi_ref, bi_ref, lam_ref,
             dh_ref, dwin_ref, dcw_ref, dcb_ref, dwr_ref, dbr_ref, dwi_ref, dbi_ref, dlam_ref, ecar_sc, dxc_sc):
        i = pl.program_id(0)

        @pl.when(i == 0)
        def _():
            for r in (dwin_ref, dcw_ref, dcb_ref, dwr_ref, dbr_ref, dwi_ref, dbi_ref, dlam_ref, ecar_sc, dxc_sc):
                r[...] = jnp.zeros_like(r)

        first = (i == nb - 1)
        u = u_ref[...]
        utail = jnp.where(first, 0.0, ut_ref[...])
        hstail = jnp.where(first, 0.0, hst_ref[...])
        cw, wr, wi, lam = cw_ref[...], wr_ref[...], wi_ref[...], lam_ref[...]
        us, xc, xb, rg, ig, clam, la, a, mm = _lru_gates(u, utail, cw, cb_ref[...], wr, br_ref[...], wi, bi_ref[...], lam)
        row = _iota(a.shape, 0)
        a_next = jnp.where(row < tb - 1, pltpu.roll(a, tb - 1, 0), 1.0)
        big_a, big_b = _scan_up(a_next, dhs_ref[...])
        e = big_a * ecar_sc[0:1, :] + big_b
        ecar_sc[...] = jnp.broadcast_to(a[0:1, :] * e[0:1, :], ecar_sc.shape)
        hs_prev = _shift_down(hs_ref[...], hstail, 1)
        da = e * hs_prev
        ix = ig * xc
        dmm = e * ix
        dix = e * mm
        dla = da * a - dmm * (a * a) / mm
        dlam_ref[...] += jnp.sum(dla * rg, axis=0, keepdims=True)
        dpr = (dla * clam) * rg * (1.0 - rg)
        dpi = (dix * xc) * ig * (1.0 - ig)
        dbr_ref[...] += jnp.sum(dpr, axis=0, keepdims=True)
        dbi_ref[...] += jnp.sum(dpi, axis=0, keepdims=True)
        dprb, dpib = dpr.astype(BF16), dpi.astype(BF16)
        dxc_g = []
        for g in range(3):
            sl = slice(256 * g, 256 * (g + 1))
            dwr_ref[g] += _mm_tn(xb[:, sl], dprb[:, sl])
            dwi_ref[g] += _mm_tn(xb[:, sl], dpib[:, sl])
            dxc_g.append(_mm_nt(dprb[:, sl], wr[g]) + _mm_nt(dpib[:, sl], wi[g]))
        dxc = dix * ig + jnp.concatenate(dxc_g, axis=1)
        dcb_ref[...] += jnp.sum(dxc, axis=0, keepdims=True)
        dcw_ref[...] += jnp.concatenate([jnp.sum(dxc * us[3 - tap], axis=0, keepdims=True) for tap in range(4)], axis=0)
        head = dxc_sc[...]
        du = dxc * cw[3:4]
        for k in range(1, 4):
            du = du + _shift_up(dxc, head, k) * cw[3 - k:4 - k]
        dxc_sc[...] = dxc[:8, :]
        dz = jnp.concatenate([du, dgate_ref[...], dqm_ref[...]], axis=1).astype(BF16)
        hb = h_ref[...].astype(BF16)
        dh = dres_ref[...]
        for sh in range(4):
            dzs = dz[:, 512 * sh:512 * (sh + 1)]
            dh = dh + _mm_nt(dzs, win_ref[sh])
            dwin_ref[sh] += _mm_tn(hb, dzs)
        dh_ref[...] = dh

        @pl.when(i == nb - 1)
        def _():
            dlam_ref[...] = dlam_ref[...] * (LRU_C * jax.nn.sigmoid(-lam))

    outs = (jax.ShapeDtypeStruct((s, 1024), F32), jax.ShapeDtypeStruct((4, 1024, 512), F32),
            jax.ShapeDtypeStruct((4, TOK_W), F32), jax.ShapeDtypeStruct((1, TOK_W), F32),
            jax.ShapeDtypeStruct((3, 256, 256), F32), jax.ShapeDtypeStruct((1, TOK_W), F32),
            jax.ShapeDtypeStruct((3, 256, 256), F32), jax.ShapeDtypeStruct((1, TOK_W), F32),
            jax.ShapeDtypeStruct((1, TOK_W), F32))
    return pl.pallas_call(
        body, name="lru_bwd", grid=(nb,), out_shape=outs,
        in_specs=[rev(TOK_W), rev(1024), rev(MEM_W), rev(1024), rev(1024), rev(TOK_W), rev(TOK_W),
                  prev_tail(TOK_W), prev_tail(TOK_W),
                  _const((4, 1024, 512)), _const((4, TOK_W)), _const((1, TOK_W)), _const((3, 256, 256)),
                  _const((1, TOK_W)), _const((3, 256, 256)), _const((1, TOK_W)), _const((1, TOK_W))],
        out_specs=(rev(1024), _const((4, 1024, 512)), _const((4, TOK_W)), _const((1, TOK_W)), _const((3, 256, 256)),
                   _const((1, TOK_W)), _const((3, 256, 256)), _const((1, TOK_W)), _const((1, TOK_W))),
        scratch_shapes=[pltpu.VMEM((8, TOK_W), F32), pltpu.VMEM((8, TOK_W), F32)],
        compiler_params=_params(),
    )(dhs, dgate, dqm, dres, h, u, hs, u, hs, win, cw, cb, wr, br, wi, bi, lam)


def _adamw_update(w_ref, g_ref, m_ref, v_ref, d_ref, nm_ref, nv_ref):
    g = g_ref[...]
    nm = ADAM_B1 * m_ref[...] + (1.0 - ADAM_B1) * g
    nv = ADAM_B2 * v_ref[...] + (1.0 - ADAM_B2) * (g * g)
    m_hat = nm / (1.0 - ADAM_B1 ** ADAM_STEP)
    v_hat = nv / (1.0 - ADAM_B2 ** ADAM_STEP)
    d_ref[...] = -ADAM_LR * (m_hat / (jnp.sqrt(v_hat) + ADAM_EPS) + ADAM_WD * w_ref[...])
    nm_ref[...] = nm
    nv_ref[...] = nv


def _adamw(name, w, g, m, v):
    rows, cols = w.shape
    tb = 256 if rows % 256 == 0 else rows

    def body(*refs):
        _adamw_update(*refs)

    shp = jax.ShapeDtypeStruct((rows, cols), F32)
    return pl.pallas_call(
        body, name="adamw_" + name, grid=(rows // tb,), out_shape=(shp, shp, shp),
        in_specs=[_rows(tb, cols)] * 4, out_specs=(_rows(tb, cols),) * 3,
        compiler_params=_params(("parallel",)),
    )(w, g, m, v)


def _adamw_small(items):
    n = len(items)

    def body(*refs):
        for k in range(n):
            _adamw_update(*refs[4 * k:4 * k + 4], *refs[4 * n + 3 * k:4 * n + 3 * k + 3])

    args = [a for it in items for a in it]
    shapes = [jax.ShapeDtypeStruct(it[0].shape, F32) for it in items for _ in range(3)]
    outs = pl.pallas_call(
        body, name="adamw_small", grid=(1,), out_shape=tuple(shapes),
        in_specs=[_const(a.shape) for a in args], out_specs=tuple(_const(sh.shape) for sh in shapes),
        compiler_params=_params(),
    )(*args)
    return [outs[3 * k:3 * k + 3] for k in range(n)]


def _row_block(rows, cap=2048):
    return max(t for t in range(8, cap + 1, 8) if rows % t == 0)


def _add_own_half(sh, got, c_idx):
    hs = sh.shape[1] // 2
    tb = _row_block(hs, 1024)
    nb = hs // tb

    def body(c_ref, a_ref, b_ref, o_ref):
        o_ref[...] = a_ref[...] + b_ref[...]

    return pl.pallas_call(
        body, name="add_sibling",
        grid_spec=pltpu.PrefetchScalarGridSpec(
            num_scalar_prefetch=1, grid=(4, nb),
            in_specs=[pl.BlockSpec((1, tb, 128), lambda s, i, c: (s, c[0] * nb + i, 0)),
                      pl.BlockSpec((1, tb, 128), lambda s, i, c: (s, i, 0))],
            out_specs=pl.BlockSpec((1, tb, 128), lambda s, i, c: (s, i, 0))),
        out_shape=jax.ShapeDtypeStruct((4, hs, 128), F32),
        compiler_params=_params(("parallel", "parallel")),
    )(c_idx.reshape(1).astype(jnp.int32), sh, got)


def _add2(a, b):
    rows = a.shape[0]
    tb = _row_block(rows)

    def body(a_ref, b_ref, o_ref):
        o_ref[...] = a_ref[...] + b_ref[...]

    return pl.pallas_call(
        body, name="add_sibling", grid=(rows // tb,), out_shape=jax.ShapeDtypeStruct(a.shape, F32),
        in_specs=[_rows(tb, 128)] * 2, out_specs=_rows(tb, 128), compiler_params=_params(("parallel",)),
    )(a, b)


def _sum_slots(landed, own, rows):
    tb = _row_block(rows, 1024)

    def body(l_ref, o_ref, out_ref):
        t = 2 * lax.axis_index("x") + lax.axis_index("y")
        r = [jnp.where(t == s, o_ref[s], l_ref[s].astype(F32)) for s in range(4)]
        out_ref[...] = ((r[0] + r[1]) + r[2]) + r[3]

    return pl.pallas_call(
        body, name="sum_chips", grid=(rows // tb,), out_shape=jax.ShapeDtypeStruct((rows, 128), F32),
        in_specs=[pl.BlockSpec((4, tb, 128), lambda i: (0, i, 0))] * 2, out_specs=_rows(tb, 128),
        compiler_params=_params(("parallel",)),
    )(landed, own)


_ANY = pl.BlockSpec(memory_space=pl.ANY)


def _place():
    x, y, c = lax.axis_index("x"), lax.axis_index("y"), lax.axis_index("c")
    return x, y, c, [(1 - x, y), (x, 1 - y), (1 - x, 1 - y)]


def _remote(src, dst, ssem, rsem, to):
    return pltpu.make_async_remote_copy(src_ref=src, dst_ref=dst, send_sem=ssem, recv_sem=rsem, device_id=to,
                                        device_id_type=MESH)


class _Exchange:
    def __init__(self, ins, out_shape, sems, start, finish):
        self.ins, self.out_shape, self.sems, self.start, self.finish = ins, out_shape, sems, start, finish


def _run(body, *, name, grid, in_specs, out_specs, out_shape, args, scratch=(), sem, exchange=None,
         vmem_limit=VMEM_LIMIT):
    if exchange is None:
        return pl.pallas_call(body, name=name, grid=grid, out_shape=tuple(out_shape), in_specs=list(in_specs),
                              out_specs=tuple(out_specs), scratch_shapes=list(scratch),
                              compiler_params=_params(sem, vmem_limit))(*args)
    n_in, n_out, n_sc = len(args), len(out_shape), len(scratch)
    k_in, k_out = len(exchange.ins), len(exchange.out_shape)

    def fused(*refs):
        ins, refs = refs[:n_in], refs[n_in:]
        xin, refs = refs[:k_in], refs[k_in:]
        outs, refs = refs[:n_out], refs[n_out:]
        xout, refs = refs[:k_out], refs[k_out:]
        sc, xsem = refs[:n_sc], refs[n_sc:]
        first = pl.program_id(0) == 0
        last = pl.program_id(0) == grid[0] - 1
        for a in range(1, len(grid)):
            first = first & (pl.program_id(a) == 0)
            last = last & (pl.program_id(a) == grid[a] - 1)

        @pl.when(first)
        def _():
            exchange.start(xin, xout, xsem)

        body(*ins, *outs, *sc)

        @pl.when(last)
        def _():
            exchange.finish(xin, xout, xsem)

    return pl.pallas_call(
        fused, name=name, grid=grid, out_shape=(*out_shape, *exchange.out_shape),
        in_specs=[*in_specs, *[_ANY] * k_in], out_specs=(*out_specs, *[_ANY] * k_out),
        scratch_shapes=[*scratch, *exchange.sems],
        compiler_params=_params(("arbitrary",) * len(grid), vmem_limit),
    )(*args, *exchange.ins)


def _run_exchange(exchange, name):
    def body(*refs):
        k_in, k_out = len(exchange.ins), len(exchange.out_shape)
        xin, xout, xsem = refs[:k_in], refs[k_in:k_in + k_out], refs[k_in + k_out:]
        exchange.start(xin, xout, xsem)
        exchange.finish(xin, xout, xsem)

    return pl.pallas_call(
        body, name=name, out_shape=tuple(exchange.out_shape), in_specs=[_ANY] * len(exchange.ins),
        out_specs=tuple([_ANY] * len(exchange.out_shape)), scratch_shapes=list(exchange.sems),
    )(*exchange.ins)


def _gather_shards(wsh):
    _, hh, _ = wsh.shape

    def first_hop(w_ref, out_ref, ssems, rsems):
        x, y, c, chips = _place()
        t = 2 * x + y
        return [_remote(w_ref.at[c], out_ref.at[t, c], ssems.at[j], rsems.at[j], (cx, cy, c))
                for j, (cx, cy) in enumerate(chips)]

    def start(xin, xout, xsem):
        for cp in first_hop(xin[0], xout[0], *xsem):
            cp.start()

    def finish(xin, xout, xsem):
        out_ref, (ssems, rsems) = xout[0], xsem
        first = first_hop(xin[0], out_ref, *xsem)
        x, y, c, chips = _place()
        passed = []
        for j, (cx, cy) in enumerate(chips):
            got = out_ref.at[2 * cx + cy, c]
            _remote(got, got, ssems.at[j], rsems.at[j], (cx, cy, c)).wait_recv()
            cp = _remote(got, got, ssems.at[3 + j], rsems.at[3 + j], (x, y, 1 - c))
            cp.start()
            passed.append(cp)
        for j, (cx, cy) in enumerate(chips):
            got = out_ref.at[2 * cx + cy, 1 - c]
            _remote(got, got, ssems.at[3 + j], rsems.at[3 + j], (x, y, 1 - c)).wait_recv()
        for cp in first + passed:
            cp.wait_send()

    return _Exchange([wsh], [jax.ShapeDtypeStruct((4, 2, hh, 128), wsh.dtype)],
                     [pltpu.SemaphoreType.DMA((6,)), pltpu.SemaphoreType.DMA((6,))], start, finish)


def _gathered(landed, own):
    t = 2 * lax.axis_index("x") + lax.axis_index("y")
    return lax.dynamic_update_slice(landed, own[None], (t, 0, 0, 0))


def _swap_sibling(sh, rp):
    hs, rr = sh.shape[1] // 2, rp.shape[2]

    def copies(xin, xout, xsem):
        x, y, c, _ = _place()
        sib = (x, y, 1 - c)
        rows = xin[0].at[:, pl.ds(pl.multiple_of((1 - c) * hs, 8), hs)]
        return [_remote(rows, xout[0].at[:, pl.ds(0, hs)], xsem[0].at[0], xsem[1].at[0], sib),
                _remote(xin[1].at[:, 1 - c], xout[0].at[:, pl.ds(hs, rr)], xsem[0].at[1], xsem[1].at[1], sib)]

    def start(*a):
        for cp in copies(*a):
            cp.start()

    def finish(*a):
        for cp in copies(*a):
            cp.wait()

    return _Exchange([sh, rp], [jax.ShapeDtypeStruct((4, hs + rr, 128), F32)],
                     [pltpu.SemaphoreType.DMA((2,)), pltpu.SemaphoreType.DMA((2,))], start, finish)


def _scatter_chips(parts):
    n = len(parts)

    def copies(xin, xout, ssems, rsems):
        x, y, c, chips = _place()
        t = 2 * x + y
        return [_remote(xin[k].at[2 * cx + cy], xout[k].at[t], ssems.at[n * j + k], rsems.at[n * j + k], (cx, cy, c))
                for j, (cx, cy) in enumerate(chips) for k in range(n)]

    def start(xin, xout, xsem):
        for cp in copies(xin, xout, *xsem):
            cp.start()

    def finish(xin, xout, xsem):
        ssems, rsems = xsem
        x, y, c, chips = _place()
        for j, (cx, cy) in enumerate(chips):
            for k in range(n):
                got = xout[k].at[2 * cx + cy]
                _remote(got, got, ssems.at[n * j + k], rsems.at[n * j + k], (cx, cy, c)).wait_recv()
        for cp in copies(xin, xout, *xsem):
            cp.wait_send()

    return _Exchange(parts, [jax.ShapeDtypeStruct(a.shape, a.dtype) for a in parts],
                     [pltpu.SemaphoreType.DMA((3 * n,)), pltpu.SemaphoreType.DMA((3 * n,))], start, finish)


def _share_reduced(piece, eighth):
    def copies(t_ref, mine_r, sib_ref, rall_ref, ssems, rsems, lsem):
        x, y, c, _ = _place()
        me = 4 * x + 2 * y + c
        loc = pltpu.make_async_copy(mine_r, rall_ref.at[me], lsem)
        sends = [_remote(t_ref, sib_ref, ssems.at[0], rsems.at[0], (x, y, 1 - c))]
        peers = []
        for mask in range(1, 8):
            px = 1 - x if mask & 4 else x
            py = 1 - y if mask & 2 else y
            pc = 1 - c if mask & 1 else c
            peers.append((mask, px, py, pc))
            sends.append(_remote(mine_r, rall_ref.at[me], ssems.at[mask], rsems.at[mask], (px, py, pc)))
        return loc, sends, peers

    def start(xin, xout, xsem):
        loc, sends, _ = copies(*xin, *xout, *xsem)
        for cp in [loc] + sends:
            cp.start()

    def finish(xin, xout, xsem):
        (sib_ref, rall_ref), (ssems, rsems, _) = xout, xsem
        loc, sends, peers = copies(*xin, *xout, *xsem)
        x, y, c, _ = _place()
        _remote(sib_ref, sib_ref, ssems.at[0], rsems.at[0], (x, y, 1 - c)).wait_recv()
        for mask, px, py, pc in peers:
            got = rall_ref.at[4 * px + 2 * py + pc]
            _remote(got, got, ssems.at[mask], rsems.at[mask], (px, py, pc)).wait_recv()
        for cp in sends:
            cp.wait_send()
        loc.wait()

    return _Exchange([piece, eighth],
                     [jax.ShapeDtypeStruct(piece.shape, F32), jax.ShapeDtypeStruct((8, *eighth.shape), F32)],
                     [pltpu.SemaphoreType.DMA((8,)), pltpu.SemaphoreType.DMA((8,)), pltpu.SemaphoreType.DMA],
                     start, finish)


def _ceil_to(n, m):
    return -(-n // m) * m


def _pack_bf16(parts):
    blocks = [p.reshape(-1, 128) for p in parts]
    rows = jnp.concatenate([jnp.pad(b, ((0, -b.shape[0] % 16), (0, 0))) for b in blocks])
    hw = _ceil_to(rows.shape[0], 32) // 2
    return jnp.pad(rows, ((0, 2 * hw - rows.shape[0]), (0, 0))).reshape(2, hw, 128)


def _segments(wall, parts):
    wall = wall.reshape(4, -1, 128)
    out, row = [], 0
    for p in parts:
        n = p.size // 128
        out.append(wall[:, row:row + n].reshape(4, *p.shape))
        row += _ceil_to(n, 16)
    return out


class _GradReduce:
    def __init__(self, sharded, replicated, c_idx, wire_bf16=False):
        self.c_idx, self.wire_bf16 = c_idx, wire_bf16
        self.rowwise = [(n, g.shape[1:]) for n, g in sharded if math.prod(g.shape[1:]) % 128 == 0]
        self.small = [(n, g.shape[1:]) for n, g in sharded if math.prod(g.shape[1:]) % 128 != 0]
        self.replicated = [(n, g.shape[0]) for n, g in replicated]
        by_name = dict(sharded)
        blocks = [by_name[n].reshape(4, -1, 128) for n, _ in self.rowwise]
        if self.small:
            rest = jnp.concatenate([by_name[n].reshape(4, -1) for n, _ in self.small], axis=1)
            blocks.append(jnp.pad(rest, ((0, 0), (0, -rest.shape[1] % 128))).reshape(4, -1, 128))
        blocks = [jnp.pad(b, ((0, 0), (0, -b.shape[1] % 8), (0, 0))) for b in blocks]
        rows = sum(b.shape[1] for b in blocks)
        self.hs = _ceil_to(rows, 256) // 2
        sh = jnp.concatenate(blocks + [jnp.zeros((4, 2 * self.hs - rows, 128), F32)], axis=1)
        rp = jnp.concatenate([g for _, g in replicated])
        self.rr = _ceil_to(_ceil_to(rp.shape[0], 128) // 128, 64) // 8
        rp = jnp.pad(rp, (0, 8 * self.rr * 128 - rp.shape[0])).reshape(4, 2, self.rr, 128)

        self.sh, self.rp = sh, rp

    def swap(self):
        return _swap_sibling(self.sh, self.rp)

    def swapped(self, got):
        self.chip_sum = _add_own_half(self.sh, got, self.c_idx)
        mine_r = lax.dynamic_index_in_dim(self.rp, self.c_idx, axis=1, keepdims=False)
        self.chip_r = _add2(mine_r.reshape(-1, 128), got[:, self.hs:].reshape(-1, 128)).reshape(4, self.rr, 128)

    def scatter(self):
        return _scatter_chips([self.chip_sum.astype(BF16) if self.wire_bf16 else self.chip_sum, self.chip_r])

    def scattered(self, landed, landed_r):
        self.piece = _sum_slots(landed, self.chip_sum, self.hs)
        self.eighth = _sum_slots(landed_r, self.chip_r, self.rr)

    def share(self):
        return _share_reduced(self.piece, self.eighth)

    def shared(self, sibling, rall):
        mine, sib = self.piece, sibling
        self.shard = jnp.where(self.c_idx == 0, jnp.concatenate([mine, sib]), jnp.concatenate([sib, mine]))
        self.rall = rall

    def reduced(self):
        out, row = {}, 0
        for name, shape in self.rowwise:
            rows = math.prod(shape) // 128
            out[name] = self.shard[row:row + rows].reshape(shape)
            row += _ceil_to(rows, 8)
        for group, flat in ((self.small, self.shard[row:].reshape(-1)), (self.replicated, self.rall.reshape(-1))):
            off = 0
            for name, shape in group:
                n = math.prod(shape) if isinstance(shape, tuple) else shape
                out[name] = flat[off:off + n]
                off += n
        return out


def _col_shards(w2d):
    rows, cols = w2d.shape
    return w2d.reshape(rows, 4, cols // 4).transpose(1, 0, 2)


_WIN0_PARTS = ((0, 384, 1280), (384, 640, 1664), (640, 672, 1984), (672, 1696, 0), (1696, 1952, 1024))


def _win0_aligned(shards):
    def cols(a, b):
        return [shards[s][:, max(a, 488 * s) - 488 * s:min(b, 488 * (s + 1)) - 488 * s]
                for s in range(4) if max(a, 488 * s) < min(b, 488 * (s + 1))]

    zeros = jnp.zeros((1024, 64), shards.dtype)
    return jnp.concatenate(cols(672, 1696) + cols(1696, 1952) + cols(0, 384) + cols(384, 640)
                           + [zeros] + cols(640, 672) + [zeros[:, :32]], axis=1)


def _win0_shards(dwin0p):
    shards = []
    for s in range(4):
        lo, hi = 488 * s, 488 * (s + 1)
        cols = [dwin0p[:, p + max(lo, a) - a:p + min(hi, b) - a] for a, b, p in _WIN0_PARTS if max(lo, a) < min(hi, b)]
        shards.append(jnp.concatenate(cols, axis=1))
    return jnp.stack(shards)


def _block_diag4(w):
    eye = jnp.eye(4, dtype=w.dtype)
    return jnp.einsum("gaij,ab->gaibj", w.reshape(3, 4, 64, 64), eye).reshape(3, 256, 256)


def _diag_blocks4(w):
    w5 = w.reshape(3, 4, 64, 4, 64)
    return jnp.stack([w5[:, a, :, a, :] for a in range(4)], axis=1).reshape(12, 64, 64)


def kernel(x, mem, positions, mla_w_in, mla_q_norm, mla_w_uq, mla_kv_norm, mla_w_ukv, lru_w_in, lru_conv_w, lru_conv_b, lru_w_rgate, lru_b_rgate, lru_w_igate, lru_b_igate, lru_lambda, w_mem_kv, w_out, ln_g, ln_b, loss_target, m_mla_w_in, m_mla_q_norm, m_mla_w_uq, m_mla_kv_norm, m_mla_w_ukv, m_lru_w_in, m_lru_conv_w, m_lru_conv_b, m_lru_w_rgate, m_lru_b_rgate, m_lru_w_igate, m_lru_b_igate, m_lru_lambda, m_w_mem_kv, m_w_out, m_ln_g, m_ln_b, v_mla_w_in, v_mla_q_norm, v_mla_w_uq, v_mla_kv_norm, v_mla_w_ukv, v_lru_w_in, v_lru_conv_w, v_lru_conv_b, v_lru_w_rgate, v_lru_b_rgate, v_lru_w_igate, v_lru_b_igate, v_lru_lambda, v_w_mem_kv, v_w_out, v_ln_g, v_ln_b):
    s = x.shape[1]
    c_idx = lax.axis_index("c")
    x2, mem2, tgt2 = x[0], mem[0], loss_target[0]

    first = [p.astype(BF16) for p in (mla_w_in[0], mla_w_uq[0], mla_w_ukv[0])]
    buf = _pack_bf16(first)
    mla_shards = _segments(_gathered(_run_exchange(_gather_shards(buf), "gather_weights")[0], buf), first)

    mid = [w_mem_kv.astype(BF16), w_out[0].astype(BF16)]
    buf_mid = _pack_bf16(mid)

    def mid_weights(landed):
        wmem, wout0 = _segments(_gathered(landed[0], buf_mid), mid)
        return wmem.transpose(1, 0, 2, 3).reshape(2, 1024, 512), wout0.reshape(1024, 1024)

    small = jnp.concatenate([lru_conv_w[0].reshape(-1), lru_conv_b[0], lru_b_rgate[0], lru_b_igate[0], lru_lambda[0]])
    late = [lru_w_in[0].astype(BF16), w_out[1].astype(BF16), lax.bitcast_convert_type(small, BF16)]
    buf_late = _pack_bf16(late)

    def late_weights(landed):
        win1, wout1, small_bits = _segments(_gathered(landed[0], buf_late), late)
        small_all = lax.bitcast_convert_type(small_bits, F32)
        cw = small_all[:, :768].reshape(4, 4, 192).transpose(1, 0, 2).reshape(4, TOK_W)
        cb, br, bi, lam = (small_all[:, 768 + 192 * k:960 + 192 * k].reshape(1, TOK_W) for k in range(4))
        return win1, wout1.reshape(1024, 1024), cw, cb, br, bi, lam

    def reduce_late(g):
        return _GradReduce(
            [("lru_w_in", g["lru_w_in"]), ("lru_conv_w", _col_shards(g["lru_conv_w"])),
             ("lru_conv_b", _col_shards(g["lru_conv_b"])), ("lru_b_rgate", _col_shards(g["lru_b_rgate"])),
             ("lru_b_igate", _col_shards(g["lru_b_igate"])), ("lru_lambda", _col_shards(g["lru_lambda"])),
             ("w_mem_kv1", g["w_mem_kv1"].reshape(4, 256, 512)), ("w_out1", g["w_out1"].reshape(4, 256, 1024))],
            [("lru_w_rgate", g["lru_w_rgate"].reshape(-1)), ("lru_w_igate", g["lru_w_igate"].reshape(-1)),
             ("ln_g1", g["ln_g1"].reshape(-1)), ("ln_b1", g["ln_b1"].reshape(-1))], c_idx)

    g0, late_red = _local_step(
        x2, mem2, positions.reshape(s, 1), tgt2, *mla_shards, mla_q_norm, mla_kv_norm, lru_w_rgate[0], lru_w_igate[0],
        ln_g, ln_b, mid_weights, late_weights, _gather_shards(buf_mid), _gather_shards(buf_late), reduce_late)

    early_red = _GradReduce(
        [("mla_w_in", g0["mla_w_in"]), ("mla_w_uq", _col_shards(g0["mla_w_uq"])),
         ("mla_w_ukv", _col_shards(g0["mla_w_ukv"])), ("w_mem_kv0", g0["w_mem_kv0"].reshape(4, 256, 512)),
         ("w_out0", g0["w_out0"].reshape(4, 256, 1024))],
        [("mla_q_norm", g0["mla_q_norm"].reshape(-1)), ("mla_kv_norm", g0["mla_kv_norm"].reshape(-1)),
         ("ln_g0", g0["ln_g0"].reshape(-1)), ("ln_b0", g0["ln_b0"].reshape(-1)), ("loss", g0["loss"].reshape(-1))],
        c_idx, wire_bf16=True)
    early_red.swapped(*_run_exchange(early_red.swap(), "swap_sibling"))
    early_red.scattered(*_run_exchange(early_red.scatter(), "scatter_chips"))
    early_red.shared(*_run_exchange(early_red.share(), "share_reduced"))
    red = {**late_red.reduced(), **early_red.reduced()}
    red["w_mem_kv"] = jnp.concatenate([red["w_mem_kv0"], red["w_mem_kv1"]])
    red["w_out"] = jnp.concatenate([red["w_out0"], red["w_out1"]])
    red["ln_g"] = jnp.concatenate([red["ln_g0"], red["ln_g1"]])
    red["ln_b"] = jnp.concatenate([red["ln_b0"], red["ln_b1"]])

    weights = dict(mla_w_in=mla_w_in, mla_q_norm=mla_q_norm, mla_w_uq=mla_w_uq, mla_kv_norm=mla_kv_norm,
                   mla_w_ukv=mla_w_ukv, lru_w_in=lru_w_in, lru_conv_w=lru_conv_w, lru_conv_b=lru_conv_b,
                   lru_w_rgate=lru_w_rgate, lru_b_rgate=lru_b_rgate, lru_w_igate=lru_w_igate, lru_b_igate=lru_b_igate,
                   lru_lambda=lru_lambda, w_mem_kv=w_mem_kv, w_out=w_out, ln_g=ln_g, ln_b=ln_b)
    m_in = dict(mla_w_in=m_mla_w_in, mla_q_norm=m_mla_q_norm, mla_w_uq=m_mla_w_uq, mla_kv_norm=m_mla_kv_norm,
                mla_w_ukv=m_mla_w_ukv, lru_w_in=m_lru_w_in, lru_conv_w=m_lru_conv_w, lru_conv_b=m_lru_conv_b,
                lru_w_rgate=m_lru_w_rgate, lru_b_rgate=m_lru_b_rgate, lru_w_igate=m_lru_w_igate,
                lru_b_igate=m_lru_b_igate, lru_lambda=m_lru_lambda, w_mem_kv=m_w_mem_kv, w_out=m_w_out, ln_g=m_ln_g,
                ln_b=m_ln_b)
    v_in = dict(mla_w_in=v_mla_w_in, mla_q_norm=v_mla_q_norm, mla_w_uq=v_mla_w_uq, mla_kv_norm=v_mla_kv_norm,
                mla_w_ukv=v_mla_w_ukv, lru_w_in=v_lru_w_in, lru_conv_w=v_lru_conv_w, lru_conv_b=v_lru_conv_b,
                lru_w_rgate=v_lru_w_rgate, lru_b_rgate=v_lru_b_rgate, lru_w_igate=v_lru_w_igate,
                lru_b_igate=v_lru_b_igate, lru_lambda=v_lru_lambda, w_mem_kv=v_w_mem_kv, w_out=v_w_out, ln_g=v_ln_g,
                ln_b=v_ln_b)
    order = ["mla_w_in", "mla_q_norm", "mla_w_uq", "mla_kv_norm", "mla_w_ukv", "lru_w_in", "lru_conv_w", "lru_conv_b",
             "lru_w_rgate", "lru_b_rgate", "lru_w_igate", "lru_b_igate", "lru_lambda", "w_mem_kv", "w_out", "ln_g",
             "ln_b"]
    grads, deltas, new_m, new_v = {}, {}, {}, {}

    def operands(name):
        shape = weights[name].shape
        two_d = (math.prod(shape[:-1]), shape[-1])
        return [a.reshape(two_d) for a in (weights[name], red[name], m_in[name], v_in[name])]

    def keep(name, g2, d2, m2, v2):
        shape = weights[name].shape
        grads[name], deltas[name] = g2.reshape(shape), d2.reshape(shape)
        new_m[name], new_v[name] = m2.reshape(shape), v2.reshape(shape)

    small = [n for n in order if weights[n].size <= 4096]
    ops = [operands(n) for n in small]
    for name, op, res in zip(small, ops, _adamw_small(ops)):
        keep(name, op[1], *res)
    for name in order:
        if name not in small:
            op = operands(name)
            keep(name, op[1], *_adamw(name, *op))
    return (red["loss"][0], g0["x"][None], *[grads[n] for n in order], *[deltas[n] for n in order],
            *[new_m[n] for n in order], *[new_v[n] for n in order])


def _local_step(x2, mem2, pos_col, tgt2, win0_sh, wuq_sh, wukv_sh, gq, gkv, w_rgate, w_igate, ln_g, ln_b,
                mid_weights, late_weights, gather_mid=None, gather_late=None, reduce_late=None):
    s = x2.shape[0]
    win0p = _win0_aligned(win0_sh)
    wuq_p = jnp.pad(wuq_sh.reshape(4, Q_LORA, 3, 96).transpose(1, 0, 2, 3).reshape(Q_LORA, 12, 96),
                    ((0, 0), (0, 0), (0, 32))).reshape(Q_LORA, QK_W)
    wukv3 = wukv_sh.reshape(4, KV_LORA, 3, 128).transpose(1, 0, 2, 3).reshape(KV_LORA, 12, 128)
    wk_p = jnp.pad(wukv3[:, :, :64], ((0, 0), (0, 0), (0, 64))).reshape(KV_LORA, QK_W)
    wv = wukv3[:, :, 64:].reshape(KV_LORA, TOK_W)
    wr_bd = _block_diag4(w_rgate).astype(BF16)
    wi_bd = _block_diag4(w_igate).astype(BF16)
    half = 16
    inv_freq = ROPE_THETA ** (-jnp.arange(half, dtype=F32) / half)
    inv_lane = jnp.concatenate([jnp.zeros((64,), F32), inv_freq, inv_freq, jnp.zeros((32,), F32)]).reshape(1, HEAD_PAD)

    gate0, qm0, cq, ckv, q_p, q_t, k_p, v_b, v_t, ctab, satab, sbtab, *landed = _mla_proj_fwd(
        x2, win0p, gq, gkv, wuq_p, wk_p, wv, pos_col, inv_lane, exchange=gather_mid)
    wmem, wout0 = mid_weights(landed)
    memkv = _mem_kv(mem2, wmem)
    tok0, lse, *landed = _attn_fwd(q_p, k_p, v_t, exchange=gather_late)
    win1, wout1, cw, cb, br, bi, lam = late_weights(landed)
    g0, b0, g1, b1 = ln_g[0:1], ln_b[0:1], ln_g[1:2], ln_b[1:2]
    h1 = _mix_fwd(tok0, gate0, qm0, memkv[0], wout0, x2, g0, b0)
    u1, gate1, qm1, hs1 = _lru_fwd(h1, win1, cw, cb, wr_bd, br, wi_bd, bi, lam)

    dres1, dtok1, dgate1, dqm1, dwout1, dmemkv1, dg1, db1, loss = _mix_bwd(
        hs1, gate1, qm1, memkv[1], wout1, h1, g1, b1, tgt2, True)
    dh1, dwin1, dcw, dcb, dwr_bd, dbr, dwi_bd, dbi, dlam = _lru_bwd(
        dtok1, dgate1, dqm1, dres1, h1, u1, hs1, win1, cw, cb, wr_bd, br, wi_bd, bi, lam)
    late = {"lru_w_in": dwin1, "lru_conv_w": dcw, "lru_conv_b": dcb, "lru_b_rgate": dbr, "lru_b_igate": dbi,
            "lru_lambda": dlam, "w_mem_kv1": _mem_kv_bwd(mem2, dmemkv1), "w_out1": dwout1,
            "lru_w_rgate": _diag_blocks4(dwr_bd), "lru_w_igate": _diag_blocks4(dwi_bd), "ln_g1": dg1, "ln_b1": db1}
    red = reduce_late(late) if reduce_late is not None else None

    dres0, dob, dobt, stats, dgate0, dqm0, dwout0, dmemkv0, dg0, db0, _, *got = _mix_bwd(
        tok0, gate0, qm0, memkv[0], wout0, x2, g0, b0, dh1, False, lse=lse, exchange=red.swap() if red else None)
    if red:
        red.swapped(*got)
    dq_p, dk_t, dv_t, *got = _attn_bwd(q_p, q_t, k_p, v_b, dob, dobt, stats,
                                       exchange=red.scatter() if red else None)
    if red:
        red.scattered(*got)
    if red:
        red.shared(*_run_exchange(red.share(), "share_reduced"))
    dx, dwin0p, dwuq_p, dwk_p, dwv, dgq, dgkv = _mla_proj_bwd(
        x2, cq, ckv, dq_p, dk_t, dv_t, dgate0, dqm0, dres0, win0p, gq, gkv, wuq_p, wk_p, wv,
        ctab, satab, sbtab)

    dwin0 = _win0_shards(dwin0p)
    dwuq = dwuq_p.reshape(Q_LORA, 12, 128)[:, :, :96].reshape(Q_LORA, 1152)
    dwukv = jnp.concatenate([dwk_p.reshape(KV_LORA, 12, 128)[:, :, :64], dwv.reshape(KV_LORA, 12, 64)],
                            axis=2).reshape(KV_LORA, 1536)
    early = {"x": dx, "loss": loss, "mla_w_in": dwin0, "mla_w_uq": dwuq, "mla_w_ukv": dwukv,
             "w_mem_kv0": _mem_kv_bwd(mem2, dmemkv0), "w_out0": dwout0, "mla_q_norm": dgq, "mla_kv_norm": dgkv,
             "ln_g0": dg0, "ln_b0": db0}
    return early, (red if red else late)
```

```python
import functools
import math

import jax
import jax.numpy as jnp
from jax import lax
from jax.experimental import pallas as pl
from jax.experimental.pallas import tpu as pltpu

F32, BF16 = jnp.float32, jnp.bfloat16
MESH = pl.DeviceIdType.MESH

D_MODEL = 1024
N_TOK_HEADS = 12
TOK_W = 768
MEM_W = 256
MEM_LEN = 256
Q_LORA, KV_LORA = 384, 256
HEAD_PAD = 128
QK_W = N_TOK_HEADS * HEAD_PAD
ATT_SCALE = 1.0 / math.sqrt(96.0)
ATT_SCALE_LOG2 = ATT_SCALE * math.log2(math.e)
ROPE_THETA = 10000.0
LRU_C = 8.0
ALPHA = 4.0 ** 0.25
NORM_EPS = 1e-6
ADAM_LR, ADAM_B1, ADAM_B2, ADAM_EPS, ADAM_WD, ADAM_STEP = 0.001, 0.9, 0.999, 1e-08, 0.01, 10

TB_PROJ = 512
TB_PROJ_BWD = 512
TB_MIX = 512
TB_LRU = 256
TQ_ATT = 512
TQ_ATT_FWD = 1024
TK_ATT = 1024
VMEM_LIMIT = 56 * 1024 * 1024
VMEM_LIMIT_PROJ_BWD = 60 * 1024 * 1024


def _mm(a, b):
    return jnp.dot(a.astype(BF16), b.astype(BF16), preferred_element_type=F32)


def _mm_nt(a, b):
    return lax.dot_general(a.astype(BF16), b.astype(BF16), (((1,), (1,)), ((), ())), preferred_element_type=F32)


def _mm_tn(a, b):
    return lax.dot_general(a.astype(BF16), b.astype(BF16), (((0,), (0,)), ((), ())), preferred_element_type=F32)


def _rows(tb, w):
    return pl.BlockSpec((tb, w), lambda i: (i, 0))


def _const(shape):
    n = len(shape)
    return pl.BlockSpec(shape, lambda i: (0,) * n)


def _params(sem=("arbitrary",), vmem_limit=None):
    return pltpu.CompilerParams(dimension_semantics=sem, vmem_limit_bytes=vmem_limit or VMEM_LIMIT)


def _iota(shape, dim):
    return lax.broadcasted_iota(jnp.int32, shape, dim)


def _rope_tables(pos, inv_lane):
    ang = pos.astype(F32) * inv_lane
    lane = _iota(ang.shape, 1)
    cs, sn = jnp.cos(ang), jnp.sin(ang)
    return (jnp.where(lane < 64, 1.0, jnp.where(lane < 96, cs, 0.0)),
            jnp.where((lane >= 64) & (lane < 80), -sn, 0.0), jnp.where((lane >= 80) & (lane < 96), sn, 0.0))


def _rope(t, c, sa, sb):
    return t * c + pltpu.roll(t, 112, 1) * sa + pltpu.roll(t, 16, 1) * sb


def _rope_t(d, c, sa, sb):
    return d * c + pltpu.roll(d * sa, 16, 1) + pltpu.roll(d * sb, 112, 1)


def _rms(c, g):
    r = lax.rsqrt(jnp.mean(c * c, axis=-1, keepdims=True) + NORM_EPS)
    xh = c * r
    return xh * g, xh, r


def _mla_proj_fwd(x, win, gq, gkv, wuq, wukv_k, wukv_v, pos_col, inv_lane, exchange=None):
    s = x.shape[0]
    tb = min(TB_PROJ, s)

    def body(x_ref, win_ref, gq_ref, gkv_ref, wuq_ref, wk_ref, wv_ref, pos_ref, inv_ref,
             gate_ref, qm_ref, cq_ref, ckv_ref, q_ref, qt_ref, k_ref, v_ref, vt_ref, c_ref, sa_ref, sb_ref):
        z = _mm(x_ref[...], win_ref[...])
        gate_ref[...] = z[:, 0:1024]
        qm_ref[...] = z[:, 1024:1280].astype(BF16)
        cq = z[:, 1280:1664]
        ckv = z[:, 1664:1920]
        cq_ref[...] = cq
        ckv_ref[...] = ckv
        c, sa, sb = _rope_tables(pos_ref[...], inv_ref[...])
        c_ref[...], sa_ref[...], sb_ref[...] = c, sa, sb
        nq, _, _ = _rms(cq, gq_ref[...])
        nkv, _, _ = _rms(ckv, gkv_ref[...])
        qf = _mm(nq, wuq_ref[...])
        kf = _mm(nkv, wk_ref[...])
        vf = _mm(nkv, wv_ref[...])
        v_ref[...] = vf.astype(BF16)
        for j in range(N_TOK_HEADS // 2):
            sl = slice(HEAD_PAD * j, HEAD_PAD * (j + 1))
            vt_ref[sl, :] = vf[:, sl].T.astype(BF16)
        kr = _rope(z[:, 1920:2048], c, sa, sb)
        for h in range(N_TOK_HEADS):
            sl = slice(HEAD_PAD * h, HEAD_PAD * (h + 1))
            qh = _rope(qf[:, sl], c, sa, sb) * ATT_SCALE_LOG2
            q_ref[:, sl] = qh.astype(BF16)
            qt_ref[sl, :] = qh.T.astype(BF16)
            k_ref[:, sl] = (kf[:, sl] + kr).astype(BF16)

    outs = (jax.ShapeDtypeStruct((s, 1024), F32), jax.ShapeDtypeStruct((s, MEM_W), BF16),
            jax.ShapeDtypeStruct((s, Q_LORA), F32), jax.ShapeDtypeStruct((s, KV_LORA), F32),
            jax.ShapeDtypeStruct((s, QK_W), BF16), jax.ShapeDtypeStruct((QK_W, s), BF16),
            jax.ShapeDtypeStruct((s, QK_W), BF16),
            jax.ShapeDtypeStruct((s, TOK_W), BF16), jax.ShapeDtypeStruct((TOK_W, s), BF16),
            *[jax.ShapeDtypeStruct((s, HEAD_PAD), F32)] * 3)

    def cols(w):
        return pl.BlockSpec((w, tb), lambda i: (0, i))

    return _run(
        body, name="mla_proj_fwd", grid=(s // tb,), out_shape=outs,
        in_specs=[_rows(tb, 1024), _const((1024, 2048)), _const((1, Q_LORA)), _const((1, KV_LORA)),
                  _const((Q_LORA, QK_W)), _const((KV_LORA, QK_W)), _const((KV_LORA, TOK_W)),
                  _rows(tb, 1), _const((1, HEAD_PAD))],
        out_specs=(_rows(tb, 1024), _rows(tb, MEM_W), _rows(tb, Q_LORA), _rows(tb, KV_LORA),
                   _rows(tb, QK_W), cols(QK_W), _rows(tb, QK_W), _rows(tb, TOK_W), cols(TOK_W),
                   _rows(tb, HEAD_PAD), _rows(tb, HEAD_PAD), _rows(tb, HEAD_PAD)),
        args=(x, win, gq, gkv, wuq, wukv_k, wukv_v, pos_col, inv_lane), sem=("parallel",), exchange=exchange)


def _mla_proj_bwd(x, cq, ckv, dq, dkt, dvt, dgate, dqm, dres, win, gq, gkv, wuq, wukv_k, wukv_v, ctab, satab, sbtab):
    s = x.shape[0]
    tb = min(TB_PROJ_BWD, s)

    def body(x_ref, cq_ref, ckv_ref, dq_ref, dkt_ref, dvt_ref, dgate_ref, dqm_ref, dres_ref, win_ref, gq_ref, gkv_ref,
             wuq_ref, wk_ref, wv_ref, c_ref, sa_ref, sb_ref,
             dx_ref, dwin_ref, dwuq_ref, dwk_ref, dwv_ref, dgq_ref, dgkv_ref):
        @pl.when(pl.program_id(0) == 0)
        def _():
            for r in (dwin_ref, dwuq_ref, dwk_ref, dwv_ref, dgq_ref, dgkv_ref):
                r[...] = jnp.zeros_like(r)

        c, sa, sb = c_ref[...], sa_ref[...], sb_ref[...]
        lane = _iota((tb, HEAD_PAD), 1)
        gq, gkv = gq_ref[...], gkv_ref[...]
        nq, xhq, rq = _rms(cq_ref[...], gq)
        nkv, xhk, rk = _rms(ckv_ref[...], gkv)
        dkp = dkt_ref[...].T * math.log(2.0)
        dqs, dkr = [], jnp.zeros((tb, HEAD_PAD), F32)
        for h in range(N_TOK_HEADS):
            sl = slice(HEAD_PAD * h, HEAD_PAD * (h + 1))
            dqs.append(_rope_t(dq_ref[:, sl], c, sa, sb).astype(BF16))
            dkr = dkr + dkp[:, sl]
        dqf = jnp.concatenate(dqs, axis=1)
        dkr = jnp.where((lane >= 64) & (lane < 96), _rope_t(dkr, c, sa, sb), 0.0)
        dvb = dvt_ref[...].T.astype(BF16)
        dkb = dkp.astype(BF16)
        dnq = _mm_nt(dqf, wuq_ref[...])
        dwuq_ref[...] += _mm_tn(nq, dqf)
        dgq_ref[...] += jnp.sum(dnq * xhq, axis=0, keepdims=True)
        dxh = dnq * gq
        dcq = rq * (dxh - xhq * jnp.mean(dxh * xhq, axis=-1, keepdims=True))
        dnkv = _mm_nt(dkb, wk_ref[...]) + _mm_nt(dvb, wv_ref[...])
        nkvb = nkv.astype(BF16)
        dwk_ref[...] += _mm_tn(nkvb, dkb)
        dwv_ref[...] += _mm_tn(nkvb, dvb)
        dgkv_ref[...] += jnp.sum(dnkv * xhk, axis=0, keepdims=True)
        dxh = dnkv * gkv
        dckv = rk * (dxh - xhk * jnp.mean(dxh * xhk, axis=-1, keepdims=True))
        dz = jnp.concatenate([dgate_ref[...], dqm_ref[...], dcq, dckv, dkr], axis=1).astype(BF16)
        dx_ref[...] = _mm_nt(dz, win_ref[...]) + dres_ref[...]
        dwin_ref[...] += _mm_tn(x_ref[...], dz)

    outs = (jax.ShapeDtypeStruct((s, 1024), F32), jax.ShapeDtypeStruct((1024, 2048), F32),
            jax.ShapeDtypeStruct((Q_LORA, QK_W), F32), jax.ShapeDtypeStruct((KV_LORA, QK_W), F32),
            jax.ShapeDtypeStruct((KV_LORA, TOK_W), F32), jax.ShapeDtypeStruct((1, Q_LORA), F32),
            jax.ShapeDtypeStruct((1, KV_LORA), F32))
    return _run(
        body, name="mla_proj_bwd", grid=(s // tb,), out_shape=outs,
        in_specs=[_rows(tb, 1024), _rows(tb, Q_LORA), _rows(tb, KV_LORA), _rows(tb, QK_W),
                  pl.BlockSpec((QK_W, tb), lambda i: (0, i)), pl.BlockSpec((TOK_W, tb), lambda i: (0, i)),
                  _rows(tb, 1024), _rows(tb, MEM_W), _rows(tb, 1024),
                  _const((1024, 2048)), _const((1, Q_LORA)), _const((1, KV_LORA)),
                  _const((Q_LORA, QK_W)), _const((KV_LORA, QK_W)), _const((KV_LORA, TOK_W)),
                  _rows(tb, HEAD_PAD), _rows(tb, HEAD_PAD), _rows(tb, HEAD_PAD)],
        out_specs=(_rows(tb, 1024), _const((1024, 2048)), _const((Q_LORA, QK_W)), _const((KV_LORA, QK_W)),
                   _const((KV_LORA, TOK_W)), _const((1, Q_LORA)), _const((1, KV_LORA))),
        args=(x, cq, ckv, dq, dkt, dvt, dgate, dqm, dres, win, gq, gkv, wuq, wukv_k, wukv_v, ctab, satab, sbtab),
        sem=("arbitrary",), vmem_limit=VMEM_LIMIT_PROJ_BWD)


def _attn_fwd(q, k, vt, exchange=None):
    s = q.shape[0]
    tq = min(TQ_ATT_FWD, s)
    tk = min(TK_ATT, s)

    def body(q_ref, k_ref, vt_ref, o_ref, lse_ref):
        i = pl.program_id(1)
        nfull = (i * tq) // tk
        krow = _iota((tk, tq), 0)
        qpos = i * tq + _iota((tk, tq), 1)

        ones = jnp.ones((16, tk), BF16)

        def head_tile(hh, st, carry, masked):
            hs = slice(HEAD_PAD * hh, HEAD_PAD * (hh + 1))
            m, acc = carry
            sc = _mm_nt(k_ref[pl.ds(st, tk), hs], q_ref[:, hs])
            if masked:
                sc = jnp.where(st + krow <= qpos, sc, -jnp.inf)
            m_new = jnp.maximum(m, jnp.max(sc, axis=0, keepdims=True))
            p = jnp.exp2(sc - m_new).astype(BF16)
            a = jnp.exp2(m - m_new)
            va = jnp.concatenate([vt_ref[64 * hh:64 * (hh + 1), pl.ds(st, tk)], ones], axis=0)
            return m_new, a * acc + jnp.dot(va, p, preferred_element_type=F32)

        def tile(j, carry, masked):
            st = pl.multiple_of(j * tk, tk)
            return tuple(head_tile(hh, st, carry[hh], masked) for hh in range(2))

        def init():
            return (jnp.full((1, tq), -jnp.inf, F32), jnp.zeros((80, tq), F32))

        carry = lax.fori_loop(0, nfull, functools.partial(tile, masked=False), (init(), init()))
        (ma, acca), (mb, accb) = tile(nfull, carry, True)
        la, lb = acca[64:65], accb[64:65]
        o_ref[...] = jnp.concatenate([acca[:64] / la, accb[:64] / lb], axis=0).T
        lse_ref[...] = jnp.concatenate([jnp.broadcast_to(ma + jnp.log2(la), (64, tq)),
                                        jnp.broadcast_to(mb + jnp.log2(lb), (64, tq))], axis=0).T

    shp = jax.ShapeDtypeStruct((s, TOK_W), F32)
    return _run(
        body, name="attn_fwd", grid=(N_TOK_HEADS // 2, s // tq), out_shape=(shp, shp),
        in_specs=[pl.BlockSpec((tq, 2 * HEAD_PAD), lambda j, i: (i, j)),
                  pl.BlockSpec((s, 2 * HEAD_PAD), lambda j, i: (0, j)),
                  pl.BlockSpec((HEAD_PAD, s), lambda j, i: (j, 0))],
        out_specs=(pl.BlockSpec((tq, HEAD_PAD), lambda j, i: (i, j)),) * 2,
        args=(q, k, vt), sem=("parallel", "arbitrary"), exchange=exchange)


def _attn_stats(o, do, lse_ref, dob_ref, dot_ref, st_ref):
    lane = _iota((o.shape[0], HEAD_PAD), 1)
    dob_ref[...] = do.astype(BF16)
    prod = do * o
    for j in range(N_TOK_HEADS // 2):
        sl = slice(HEAD_PAD * j, HEAD_PAD * (j + 1))
        dot_ref[sl, :] = do[:, sl].T.astype(BF16)
        pj = prod[:, sl]
        da = jnp.sum(jnp.where(lane < 64, pj, 0.0), axis=-1, keepdims=True)
        db = jnp.sum(jnp.where(lane >= 64, pj, 0.0), axis=-1, keepdims=True)
        la = lse_ref[:, HEAD_PAD * j:HEAD_PAD * j + 1]
        lb = lse_ref[:, HEAD_PAD * j + 64:HEAD_PAD * j + 65]
        st_ref[j] = jnp.where(lane == 0, la, jnp.where(lane == 1, lb, jnp.where(lane == 2, da,
                                                                                 jnp.where(lane == 3, db, 0.0))))


def _attn_bwd(q, qt, k, v, dob, dobt, stats, exchange=None):
    s = q.shape[0]
    t = min(TQ_ATT, s)
    nq = s // t

    def body(q_ref, qt_ref, do_ref, dot_ref, st_ref, k_ref, v_ref, dq_ref, dkt_ref, dvt_ref):
        i = pl.program_id(1)

        @pl.when(i == 0)
        def _():
            dkt_ref[...] = jnp.zeros_like(dkt_ref)
            dvt_ref[...] = jnp.zeros_like(dvt_ref)

        lane = _iota((t, HEAD_PAD), 1)
        qpos, kcol = _iota((t, t), 0), _iota((t, t), 1)
        do = do_ref[...]
        stats = st_ref[0]

        def head_tile(hh, ks, dq_acc, masked):
            hs = slice(HEAD_PAD * hh, HEAD_PAD * (hh + 1))
            qh = q_ref[:, hs]
            kh = k_ref[pl.ds(ks, t), hs]
            dom = jnp.where((lane < 64) if hh == 0 else (lane >= 64), do, jnp.zeros_like(do))
            lse = stats[:, hh:hh + 1]
            dlt = stats[:, 2 + hh:3 + hh]
            sc = _mm_nt(qh, kh)
            if masked:
                sc = jnp.where(kcol <= qpos, sc, -jnp.inf)
            p = jnp.exp2(sc - lse)
            dp = _mm_nt(dom, v_ref[pl.ds(ks, t), :])
            ds = (p * (dp - dlt)).astype(BF16)
            dvt_ref[64 * hh:64 * (hh + 1), pl.ds(ks, t)] += _mm(dot_ref[64 * hh:64 * (hh + 1), :], p)
            dkt_ref[HEAD_PAD * hh:HEAD_PAD * hh + 96, pl.ds(ks, t)] += _mm(qt_ref[HEAD_PAD * hh:HEAD_PAD * hh + 96, :], ds)
            return dq_acc + _mm(ds, kh)

        def tile(j, carry, masked):
            ks = pl.multiple_of(j * t, t)
            return tuple(head_tile(hh, ks, carry[hh], masked) for hh in range(2))

        zero = jnp.zeros((t, HEAD_PAD), F32)
        carry = lax.fori_loop(0, i, functools.partial(tile, masked=False), (zero, zero))
        dqa, dqb = tile(i, carry, True)
        dq_ref[...] = jnp.concatenate([dqa, dqb], axis=1) * ATT_SCALE

    return _run(
        body, name="attn_bwd", grid=(N_TOK_HEADS // 2, nq),
        out_shape=(jax.ShapeDtypeStruct((s, QK_W), F32), jax.ShapeDtypeStruct((QK_W, s), F32),
                   jax.ShapeDtypeStruct((TOK_W, s), F32)),
        in_specs=[pl.BlockSpec((t, 2 * HEAD_PAD), lambda j, i: (i, j)),
                  pl.BlockSpec((2 * HEAD_PAD, t), lambda j, i: (j, i)),
                  pl.BlockSpec((t, HEAD_PAD), lambda j, i: (i, j)),
                  pl.BlockSpec((HEAD_PAD, t), lambda j, i: (j, i)),
                  pl.BlockSpec((1, t, HEAD_PAD), lambda j, i: (j, i, 0)),
                  pl.BlockSpec((s, 2 * HEAD_PAD), lambda j, i: (0, j)),
                  pl.BlockSpec((s, HEAD_PAD), lambda j, i: (0, j))],
        out_specs=(pl.BlockSpec((t, 2 * HEAD_PAD), lambda j, i: (i, j)),
                   pl.BlockSpec((2 * HEAD_PAD, s), lambda j, i: (j, 0)),
                   pl.BlockSpec((HEAD_PAD, s), lambda j, i: (j, 0))),
        args=(q, qt, dob, dobt, stats, k, v), sem=("parallel", "arbitrary"), exchange=exchange)


def _mem_kv(mem, wmem):
    def body(m_ref, w_ref, o_ref):
        o_ref[0] = _mm(m_ref[...], w_ref[0]).astype(BF16)

    return pl.pallas_call(
        body, name="mem_kv", grid=(2,), out_shape=jax.ShapeDtypeStruct((2, MEM_LEN, 512), BF16),
        in_specs=[_const((MEM_LEN, 1024)), pl.BlockSpec((1, 1024, 512), lambda l: (l, 0, 0))],
        out_specs=pl.BlockSpec((1, MEM_LEN, 512), lambda l: (l, 0, 0)),
        compiler_params=_params(("parallel",)),
    )(mem, wmem)


def _mem_kv_bwd(mem, dmemkv):
    def body(m_ref, d_ref, o_ref):
        o_ref[...] = _mm_tn(m_ref[...], d_ref[...])

    return pl.pallas_call(
        body, name="mem_kv_bwd", grid=(1,), out_shape=jax.ShapeDtypeStruct((1024, 512), F32),
        in_specs=[_const((MEM_LEN, 1024)), _const((MEM_LEN, 512))], out_specs=_const((1024, 512)),
        compiler_params=_params(("arbitrary",)),
    )(mem, dmemkv)


def _head_mask(lane, sub):
    return (lane < 64) if sub == 0 else (lane >= 64)


def _mem_attn(qm, kv):
    tb = qm.shape[0]
    lane = _iota((tb, HEAD_PAD), 1)
    outs, ps = [], []
    for pp in range(2):
        qp = qm[:, HEAD_PAD * pp:HEAD_PAD * (pp + 1)]
        kp = kv[:, HEAD_PAD * pp:HEAD_PAD * (pp + 1)]
        vp = kv[:, MEM_W + HEAD_PAD * pp:MEM_W + HEAD_PAD * (pp + 1)]
        pair = None
        for sub in range(2):
            qh = jnp.where(_head_mask(lane, sub), qp, jnp.zeros_like(qp))
            sc = _mm_nt(qh, kp) * 0.125
            e = jnp.exp(sc - jnp.max(sc, axis=-1, keepdims=True))
            p = e / jnp.sum(e, axis=-1, keepdims=True)
            o = _mm(p, vp)
            ps.append(p)
            pair = o if sub == 0 else jnp.where(lane < 64, pair, o)
        outs.append(pair)
    return jnp.concatenate(outs, axis=1), ps


def _mem_attn_bwd(dmo, qm, kv, ps):
    tb = qm.shape[0]
    lane = _iota((tb, HEAD_PAD), 1)
    dqs, dks, dvs = [], [], []
    for pp in range(2):
        qp = qm[:, HEAD_PAD * pp:HEAD_PAD * (pp + 1)]
        kp = kv[:, HEAD_PAD * pp:HEAD_PAD * (pp + 1)]
        vp = kv[:, MEM_W + HEAD_PAD * pp:MEM_W + HEAD_PAD * (pp + 1)]
        dop = dmo[:, HEAD_PAD * pp:HEAD_PAD * (pp + 1)]
        dq_pair, dk_pair, dv_pair = None, None, None
        for sub in range(2):
            msk = _head_mask(lane, sub)
            p = ps[2 * pp + sub]
            qh = jnp.where(msk, qp, jnp.zeros_like(qp))
            doh = jnp.where(msk, dop, 0.0).astype(BF16)
            dv = _mm_tn(p, doh)
            dp = _mm_nt(doh, vp)
            ds = (p * (dp - jnp.sum(dp * p, axis=-1, keepdims=True)) * 0.125).astype(BF16)
            dq = _mm(ds, kp)
            dk = _mm_tn(ds, qh)
            if sub == 0:
                dq_pair, dk_pair, dv_pair = dq, dk, dv
            else:
                dq_pair = jnp.where(lane < 64, dq_pair, dq)
                dk_pair, dv_pair = dk_pair + dk, dv_pair + dv
        dqs.append(dq_pair)
        dks.append(dk_pair)
        dvs.append(dv_pair)
    return jnp.concatenate(dqs, axis=1), jnp.concatenate(dks + dvs, axis=1)


def _mix_core(tok, gate, qm, kv, wout, h_in, g, b):
    mem_out, ps = _mem_attn(qm, kv)
    cat = jnp.concatenate([tok, mem_out], axis=1)
    sg = jax.nn.sigmoid(gate)
    sl = gate * sg
    y = cat * sl
    r = ALPHA * h_in + _mm(y, wout)
    mu = jnp.mean(r, axis=-1, keepdims=True)
    xc = r - mu
    rstd = lax.rsqrt(jnp.mean(xc * xc, axis=-1, keepdims=True) + NORM_EPS)
    xh = xc * rstd
    return xh * g + b, (ps, cat, sg, sl, y, xh, rstd)


def _mix_fwd(tok, gate, qm, kv, wout, h_in, g, b):
    s = tok.shape[0]
    tb = min(TB_MIX, s)

    def body(tok_ref, gate_ref, qm_ref, kv_ref, w_ref, h_ref, g_ref, b_ref, o_ref):
        o_ref[...], _ = _mix_core(tok_ref[...], gate_ref[...], qm_ref[...], kv_ref[...], w_ref[...], h_ref[...],
                                  g_ref[...], b_ref[...])

    return pl.pallas_call(
        body, name="mix_fwd", grid=(s // tb,), out_shape=jax.ShapeDtypeStruct((s, 1024), F32),
        in_specs=[_rows(tb, TOK_W), _rows(tb, 1024), _rows(tb, MEM_W), _const((MEM_LEN, 512)), _const((1024, 1024)),
                  _rows(tb, 1024), _const((1, 1024)), _const((1, 1024))],
        out_specs=_rows(tb, 1024), compiler_params=_params(("parallel",)),
    )(tok, gate, qm, kv, wout, h_in, g, b)


def _mix_bwd(tok, gate, qm, kv, wout, h_in, g, b, up, from_loss, lse=None, exchange=None):
    s = tok.shape[0]
    tb = min(TB_MIX, s)
    n_in = 9 if lse is None else 10
    n_tok_out = 1 if lse is None else 3

    def body(*refs):
        tok_ref, gate_ref, qm_ref, kv_ref, w_ref, h_ref, g_ref, b_ref, up_ref = refs[:9]
        dres_ref, tok_out = refs[n_in], refs[n_in + 1:n_in + 1 + n_tok_out]
        dgate_ref, dqm_ref, dw_ref, dkv_ref, dg_ref, db_ref, loss_ref = refs[n_in + 1 + n_tok_out:]

        @pl.when(pl.program_id(0) == 0)
        def _():
            for r in (dw_ref, dkv_ref, dg_ref, db_ref, loss_ref):
                r[...] = jnp.zeros_like(r)

        gate, qm, kv, wout, g = gate_ref[...], qm_ref[...], kv_ref[...], w_ref[...], g_ref[...]
        h_out, (ps, cat, sg, sl, y, xh, rstd) = _mix_core(tok_ref[...], gate, qm, kv, wout, h_ref[...], g, b_ref[...])
        if from_loss:
            diff = h_out - up_ref[...]
            loss_ref[...] += 0.5 * jnp.sum(jnp.mean(diff * diff, axis=-1, keepdims=True), axis=0, keepdims=True)
            dh = diff * (1.0 / D_MODEL)
        else:
            dh = up_ref[...]
        dg_ref[...] += jnp.sum(dh * xh, axis=0, keepdims=True)
        db_ref[...] += jnp.sum(dh, axis=0, keepdims=True)
        dxh = dh * g
        dr = rstd * (dxh - jnp.mean(dxh, axis=-1, keepdims=True) - xh * jnp.mean(dxh * xh, axis=-1, keepdims=True))
        dres_ref[...] = ALPHA * dr
        drb = dr.astype(BF16)
        dy = _mm_nt(drb, wout)
        dw_ref[...] += _mm_tn(y, drb)
        dcat = dy * sl
        dgate_ref[...] = dy * cat * (sg * (1.0 + gate * (1.0 - sg)))
        if lse is None:
            tok_out[0][...] = dcat[:, :TOK_W]
        else:
            _attn_stats(tok_ref[...], dcat[:, :TOK_W], refs[9], *tok_out)
        dqm, dkv = _mem_attn_bwd(dcat[:, TOK_W:], qm, kv, ps)
        dqm_ref[...] = dqm
        dkv_ref[...] += dkv

    npair = N_TOK_HEADS // 2
    tok_shapes = [jax.ShapeDtypeStruct((s, TOK_W), F32)] if lse is None else [
        jax.ShapeDtypeStruct((s, TOK_W), BF16), jax.ShapeDtypeStruct((TOK_W, s), BF16),
        jax.ShapeDtypeStruct((npair, s, HEAD_PAD), F32)]
    tok_specs = [_rows(tb, TOK_W)] if lse is None else [
        _rows(tb, TOK_W), pl.BlockSpec((TOK_W, tb), lambda i: (0, i)),
        pl.BlockSpec((npair, tb, HEAD_PAD), lambda i: (0, i, 0))]
    outs = (jax.ShapeDtypeStruct((s, 1024), F32), *tok_shapes,
            jax.ShapeDtypeStruct((s, 1024), F32), jax.ShapeDtypeStruct((s, MEM_W), F32),
            jax.ShapeDtypeStruct((1024, 1024), F32), jax.ShapeDtypeStruct((MEM_LEN, 512), F32),
            jax.ShapeDtypeStruct((1, 1024), F32), jax.ShapeDtypeStruct((1, 1024), F32),
            jax.ShapeDtypeStruct((1, 1), F32))
    args = (tok, gate, qm, kv, wout, h_in, g, b, up) + (() if lse is None else (lse,))
    return _run(
        body, name="mix_bwd_loss" if from_loss else "mix_bwd", grid=(s // tb,), out_shape=outs,
        in_specs=[_rows(tb, TOK_W), _rows(tb, 1024), _rows(tb, MEM_W), _const((MEM_LEN, 512)), _const((1024, 1024)),
                  _rows(tb, 1024), _const((1, 1024)), _const((1, 1024)), _rows(tb, 1024)]
        + ([] if lse is None else [_rows(tb, TOK_W)]),
        out_specs=(_rows(tb, 1024), *tok_specs, _rows(tb, 1024), _rows(tb, MEM_W), _const((1024, 1024)),
                   _const((MEM_LEN, 512)), _const((1, 1024)), _const((1, 1024)), _const((1, 1))),
        args=args, sem=("arbitrary",), exchange=exchange)


def _shift_down(u, tail, k):
    if k == 0:
        return u
    r = pltpu.roll(u, k, 0)
    row8 = _iota((8, u.shape[1]), 0)
    head = jnp.where(row8 < k, pltpu.roll(tail, k, 0), r[:8])
    return jnp.concatenate([head, r[8:]], axis=0)


def _shift_up(d, head, k):
    if k == 0:
        return d
    n = d.shape[0]
    r = pltpu.roll(d, n - k, 0)
    row8 = _iota((8, d.shape[1]), 0)
    last = jnp.where(row8 >= 8 - k, pltpu.roll(head, 8 - k, 0), r[n - 8:])
    return jnp.concatenate([r[:n - 8], last], axis=0)


def _scan_down(a, b):
    n = a.shape[0]
    row = _iota(a.shape, 0)
    s = 1
    while s < n:
        ok = row >= s
        a_s = jnp.where(ok, pltpu.roll(a, s, 0), 1.0)
        b_s = jnp.where(ok, pltpu.roll(b, s, 0), 0.0)
        b = a * b_s + b
        a = a * a_s
        s *= 2
    return a, b


def _scan_up(a, b):
    n = a.shape[0]
    row = _iota(a.shape, 0)
    s = 1
    while s < n:
        ok = row < n - s
        a_s = jnp.where(ok, pltpu.roll(a, n - s, 0), 1.0)
        b_s = jnp.where(ok, pltpu.roll(b, n - s, 0), 0.0)
        b = a * b_s + b
        a = a * a_s
        s *= 2
    return a, b


def _softplus(x):
    return jnp.maximum(x, 0.0) + jnp.log(1.0 + jnp.exp(-jnp.abs(x)))


def _lru_gates(u, tail, cw, cb, wr, br, wi, bi, lam):
    us = [_shift_down(u, tail, k) for k in range(4)]
    xc = cb + us[3] * cw[0:1] + us[2] * cw[1:2] + us[1] * cw[2:3] + us[0] * cw[3:4]
    xb = xc.astype(BF16)
    pre_r = jnp.concatenate([_mm(xb[:, 256 * g:256 * (g + 1)], wr[g]) for g in range(3)], axis=1) + br
    pre_i = jnp.concatenate([_mm(xb[:, 256 * g:256 * (g + 1)], wi[g]) for g in range(3)], axis=1) + bi
    rg, ig = jax.nn.sigmoid(pre_r), jax.nn.sigmoid(pre_i)
    clam = -LRU_C * _softplus(-lam)
    la = clam * rg
    a = jnp.exp(la)
    mm = jnp.sqrt(-jnp.tanh(la) * (a * a + 1.0))
    return us, xc, xb, rg, ig, clam, la, a, mm


def _lru_fwd(h, win, cw, cb, wr, br, wi, bi, lam):
    s = h.shape[0]
    tb = min(TB_LRU, s)

    def body(h_ref, win_ref, cw_ref, cb_ref, wr_ref, br_ref, wi_ref, bi_ref, lam_ref,
             u_ref, gate_ref, qm_ref, hs_ref, tail_sc, carry_sc):
        @pl.when(pl.program_id(0) == 0)
        def _():
            tail_sc[...] = jnp.zeros_like(tail_sc)
            carry_sc[...] = jnp.zeros_like(carry_sc)

        hb = h_ref[...].astype(BF16)
        z = jnp.concatenate([_mm(hb, win_ref[sh]) for sh in range(4)], axis=1)
        u = z[:, :TOK_W]
        u_ref[...] = u
        gate_ref[...] = z[:, TOK_W:TOK_W + 1024]
        qm_ref[...] = z[:, TOK_W + 1024:].astype(BF16)
        _, xc, _, _, ig, _, _, a, mm = _lru_gates(u, tail_sc[...], cw_ref[...], cb_ref[...], wr_ref[...], br_ref[...],
                                                 wi_ref[...], bi_ref[...], lam_ref[...])
        big_a, big_b = _scan_down(a, mm * (ig * xc))
        hs = big_a * carry_sc[0:1, :] + big_b
        hs_ref[...] = hs
        tail_sc[...] = u[tb - 8:, :]
        carry_sc[...] = jnp.broadcast_to(hs[tb - 1:tb, :], carry_sc.shape)

    outs = (jax.ShapeDtypeStruct((s, TOK_W), F32), jax.ShapeDtypeStruct((s, 1024), F32),
            jax.ShapeDtypeStruct((s, MEM_W), BF16), jax.ShapeDtypeStruct((s, TOK_W), F32))
    return pl.pallas_call(
        body, name="lru_fwd", grid=(s // tb,), out_shape=outs,
        in_specs=[_rows(tb, 1024), _const((4, 1024, 512)), _const((4, TOK_W)), _const((1, TOK_W)),
                  _const((3, 256, 256)), _const((1, TOK_W)), _const((3, 256, 256)), _const((1, TOK_W)),
                  _const((1, TOK_W))],
        out_specs=(_rows(tb, TOK_W), _rows(tb, 1024), _rows(tb, MEM_W), _rows(tb, TOK_W)),
        scratch_shapes=[pltpu.VMEM((8, TOK_W), F32), pltpu.VMEM((8, TOK_W), F32)],
        compiler_params=_params(),
    )(h, win, cw, cb, wr, br, wi, bi, lam)


def _lru_bwd(dhs, dgate, dqm, dres, h, u, hs, win, cw, cb, wr, br, wi, bi, lam):
    s = h.shape[0]
    tb = min(TB_LRU, s)
    nb = s // tb

    def rev(w):
        return pl.BlockSpec((tb, w), lambda i: (nb - 1 - i, 0))

    def prev_tail(w):
        return pl.BlockSpec((8, w), lambda i: (jnp.maximum((nb - 1 - i) * (tb // 8) - 1, 0), 0))

    def body(dhs_ref, dgate_ref, dqm_ref, dres_ref, h_ref, u_ref, hs_ref, ut_ref, hst_ref, win_ref, cw_ref, cb_ref,
             wr_ref, br_ref, wi_ref, bi_ref, lam_ref,
             dh_ref, dwin_ref, dcw_ref, dcb_ref, dwr_ref, dbr_ref, dwi_ref, dbi_ref, dlam_ref, ecar_sc, dxc_sc):
        i = pl.program_id(0)

        @pl.when(i == 0)
        def _():
            for r in (dwin_ref, dcw_ref, dcb_ref, dwr_ref, dbr_ref, dwi_ref, dbi_ref, dlam_ref, ecar_sc, dxc_sc):
                r[...] = jnp.zeros_like(r)

        first = (i == nb - 1)
        u = u_ref[...]
        utail = jnp.where(first, 0.0, ut_ref[...])
        hstail = jnp.where(first, 0.0, hst_ref[...])
        cw, wr, wi, lam = cw_ref[...], wr_ref[...], wi_ref[...], lam_ref[...]
        us, xc, xb, rg, ig, clam, la, a, mm = _lru_gates(u, utail, cw, cb_ref[...], wr, br_ref[...], wi, bi_ref[...], lam)
        row = _iota(a.shape, 0)
        a_next = jnp.where(row < tb - 1, pltpu.roll(a, tb - 1, 0), 1.0)
        big_a, big_b = _scan_up(a_next, dhs_ref[...])
        e = big_a * ecar_sc[0:1, :] + big_b
        ecar_sc[...] = jnp.broadcast_to(a[0:1, :] * e[0:1, :], ecar_sc.shape)
        hs_prev = _shift_down(hs_ref[...], hstail, 1)
        da = e * hs_prev
        ix = ig * xc
        dmm = e * ix
        dix = e * mm
        dla = da * a - dmm * (a * a) / mm
        dlam_ref[...] += jnp.sum(dla * rg, axis=0, keepdims=True)
        dpr = (dla * clam) * rg * (1.0 - rg)
        dpi = (dix * xc) * ig * (1.0 - ig)
        dbr_ref[...] += jnp.sum(dpr, axis=0, keepdims=True)
        dbi_ref[...] += jnp.sum(dpi, axis=0, keepdims=True)
        dprb, dpib = dpr.astype(BF16), dpi.astype(BF16)
        dxc_g = []
        for g in range(3):
            sl = slice(256 * g, 256 * (g + 1))
            dwr_ref[g] += _mm_tn(xb[:, sl], dprb[:, sl])
            dwi_ref[g] += _mm_tn(xb[:, sl], dpib[:, sl])
            dxc_g.append(_mm_nt(dprb[:, sl], wr[g]) + _mm_nt(dpib[:, sl], wi[g]))
        dxc = dix * ig + jnp.concatenate(dxc_g, axis=1)
        dcb_ref[...] += jnp.sum(dxc, axis=0, keepdims=True)
        dcw_ref[...] += jnp.concatenate([jnp.sum(dxc * us[3 - tap], axis=0, keepdims=True) for tap in range(4)], axis=0)
        head = dxc_sc[...]
        du = dxc * cw[3:4]
        for k in range(1, 4):
            du = du + _shift_up(dxc, head, k) * cw[3 - k:4 - k]
        dxc_sc[...] = dxc[:8, :]
        dz = jnp.concatenate([du, dgate_ref[...], dqm_ref[...]], axis=1).astype(BF16)
        hb = h_ref[...].astype(BF16)
        dh = dres_ref[...]
        for sh in range(4):
            dzs = dz[:, 512 * sh:512 * (sh + 1)]
            dh = dh + _mm_nt(dzs, win_ref[sh])
            dwin_ref[sh] += _mm_tn(hb, dzs)
        dh_ref[...] = dh

        @pl.when(i == nb - 1)
        def _():
            dlam_ref[...] = dlam_ref[...] * (LRU_C * jax.nn.sigmoid(-lam))

    outs = (jax.ShapeDtypeStruct((s, 1024), F32), jax.ShapeDtypeStruct((4, 1024, 512), F32),
            jax.ShapeDtypeStruct((4, TOK_W), F32), jax.ShapeDtypeStruct((1, TOK_W), F32),
            jax.ShapeDtypeStruct((3, 256, 256), F32), jax.ShapeDtypeStruct((1, TOK_W), F32),
            jax.ShapeDtypeStruct((3, 256, 256), F32), jax.ShapeDtypeStruct((1, TOK_W), F32),
            jax.ShapeDtypeStruct((1, TOK_W), F32))
    return pl.pallas_call(
        body, name="lru_bwd", grid=(nb,), out_shape=outs,
        in_specs=[rev(TOK_W), rev(1024), rev(MEM_W), rev(1024), rev(1024), rev(TOK_W), rev(TOK_W),
                  prev_tail(TOK_W), prev_tail(TOK_W),
                  _const((4, 1024, 512)), _const((4, TOK_W)), _const((1, TOK_W)), _const((3, 256, 256)),
                  _const((1, TOK_W)), _const((3, 256, 256)), _const((1, TOK_W)), _const((1, TOK_W))],
        out_specs=(rev(1024), _const((4, 1024, 512)), _const((4, TOK_W)), _const((1, TOK_W)), _const((3, 256, 256)),
                   _const((1, TOK_W)), _const((3, 256, 256)), _const((1, TOK_W)), _const((1, TOK_W))),
        scratch_shapes=[pltpu.VMEM((8, TOK_W), F32), pltpu.VMEM((8, TOK_W), F32)],
        compiler_params=_params(),
    )(dhs, dgate, dqm, dres, h, u, hs, u, hs, win, cw, cb, wr, br, wi, bi, lam)


def _adamw_update(w_ref, g_ref, m_ref, v_ref, d_ref, nm_ref, nv_ref):
    g = g_ref[...]
    nm = ADAM_B1 * m_ref[...] + (1.0 - ADAM_B1) * g
    nv = ADAM_B2 * v_ref[...] + (1.0 - ADAM_B2) * (g * g)
    m_hat = nm / (1.0 - ADAM_B1 ** ADAM_STEP)
    v_hat = nv / (1.0 - ADAM_B2 ** ADAM_STEP)
    d_ref[...] = -ADAM_LR * (m_hat / (jnp.sqrt(v_hat) + ADAM_EPS) + ADAM_WD * w_ref[...])
    nm_ref[...] = nm
    nv_ref[...] = nv


def _adamw(name, w, g, m, v):
    rows, cols = w.shape
    tb = 256 if rows % 256 == 0 else rows

    def body(*refs):
        _adamw_update(*refs)

    shp = jax.ShapeDtypeStruct((rows, cols), F32)
    return pl.pallas_call(
        body, name="adamw_" + name, grid=(rows // tb,), out_shape=(shp, shp, shp),
        in_specs=[_rows(tb, cols)] * 4, out_specs=(_rows(tb, cols),) * 3,
        compiler_params=_params(("parallel",)),
    )(w, g, m, v)


def _adamw_small(items):
    n = len(items)

    def body(*refs):
        for k in range(n):
            _adamw_update(*refs[4 * k:4 * k + 4], *refs[4 * n + 3 * k:4 * n + 3 * k + 3])

    args = [a for it in items for a in it]
    shapes = [jax.ShapeDtypeStruct(it[0].shape, F32) for it in items for _ in range(3)]
    outs = pl.pallas_call(
        body, name="adamw_small", grid=(1,), out_shape=tuple(shapes),
        in_specs=[_const(a.shape) for a in args], out_specs=tuple(_const(sh.shape) for sh in shapes),
        compiler_params=_params(),
    )(*args)
    return [outs[3 * k:3 * k + 3] for k in range(n)]


def _row_block(rows, cap=2048):
    return max(t for t in range(8, cap + 1, 8) if rows % t == 0)


def _add_own_half(sh, got, c_idx, also_bf16):
    hs = sh.shape[1] // 2
    tb = _row_block(hs, 1024)
    nb = hs // tb

    def body(c_ref, a_ref, b_ref, o_ref, *wire_ref):
        total = a_ref[...] + b_ref[...]
        o_ref[...] = total
        if also_bf16:
            wire_ref[0][...] = total.astype(BF16)

    blk = pl.BlockSpec((1, tb, 128), lambda s, i, c: (s, i, 0))
    n_out = 2 if also_bf16 else 1
    return pl.pallas_call(
        body, name="add_sibling",
        grid_spec=pltpu.PrefetchScalarGridSpec(
            num_scalar_prefetch=1, grid=(4, nb),
            in_specs=[pl.BlockSpec((1, tb, 128), lambda s, i, c: (s, c[0] * nb + i, 0)), blk],
            out_specs=(blk,) * n_out),
        out_shape=(jax.ShapeDtypeStruct((4, hs, 128), F32), jax.ShapeDtypeStruct((4, hs, 128), BF16))[:n_out],
        compiler_params=_params(("parallel", "parallel")),
    )(c_idx.reshape(1).astype(jnp.int32), sh, got)


def _add2(a, b):
    rows = a.shape[0]
    tb = _row_block(rows)

    def body(a_ref, b_ref, o_ref):
        o_ref[...] = a_ref[...] + b_ref[...]

    return pl.pallas_call(
        body, name="add_sibling", grid=(rows // tb,), out_shape=jax.ShapeDtypeStruct(a.shape, F32),
        in_specs=[_rows(tb, 128)] * 2, out_specs=_rows(tb, 128), compiler_params=_params(("parallel",)),
    )(a, b)


def _sum_slots(landed, own, rows):
    tb = _row_block(rows, 1024)

    def body(l_ref, o_ref, out_ref):
        t = 2 * lax.axis_index("x") + lax.axis_index("y")
        r = [jnp.where(t == s, o_ref[s], l_ref[s].astype(F32)) for s in range(4)]
        out_ref[...] = ((r[0] + r[1]) + r[2]) + r[3]

    return pl.pallas_call(
        body, name="sum_chips", grid=(rows // tb,), out_shape=jax.ShapeDtypeStruct((rows, 128), F32),
        in_specs=[pl.BlockSpec((4, tb, 128), lambda i: (0, i, 0))] * 2, out_specs=_rows(tb, 128),
        compiler_params=_params(("parallel",)),
    )(landed, own)


_ANY = pl.BlockSpec(memory_space=pl.ANY)


def _place():
    x, y, c = lax.axis_index("x"), lax.axis_index("y"), lax.axis_index("c")
    return x, y, c, [(1 - x, y), (x, 1 - y), (1 - x, 1 - y)]


def _remote(src, dst, ssem, rsem, to):
    return pltpu.make_async_remote_copy(src_ref=src, dst_ref=dst, send_sem=ssem, recv_sem=rsem, device_id=to,
                                        device_id_type=MESH)


class _Exchange:
    def __init__(self, ins, out_shape, sems, start, finish):
        self.ins, self.out_shape, self.sems, self.start, self.finish = ins, out_shape, sems, start, finish


def _run(body, *, name, grid, in_specs, out_specs, out_shape, args, scratch=(), sem, exchange=None,
         vmem_limit=VMEM_LIMIT):
    if exchange is None:
        return pl.pallas_call(body, name=name, grid=grid, out_shape=tuple(out_shape), in_specs=list(in_specs),
                              out_specs=tuple(out_specs), scratch_shapes=list(scratch),
                              compiler_params=_params(sem, vmem_limit))(*args)
    n_in, n_out, n_sc = len(args), len(out_shape), len(scratch)
    k_in, k_out = len(exchange.ins), len(exchange.out_shape)

    def fused(*refs):
        ins, refs = refs[:n_in], refs[n_in:]
        xin, refs = refs[:k_in], refs[k_in:]
        outs, refs = refs[:n_out], refs[n_out:]
        xout, refs = refs[:k_out], refs[k_out:]
        sc, xsem = refs[:n_sc], refs[n_sc:]
        first = pl.program_id(0) == 0
        last = pl.program_id(0) == grid[0] - 1
        for a in range(1, len(grid)):
            first = first & (pl.program_id(a) == 0)
            last = last & (pl.program_id(a) == grid[a] - 1)

        @pl.when(first)
        def _():
            exchange.start(xin, xout, xsem)

        body(*ins, *outs, *sc)

        @pl.when(last)
        def _():
            exchange.finish(xin, xout, xsem)

    return pl.pallas_call(
        fused, name=name, grid=grid, out_shape=(*out_shape, *exchange.out_shape),
        in_specs=[*in_specs, *[_ANY] * k_in], out_specs=(*out_specs, *[_ANY] * k_out),
        scratch_shapes=[*scratch, *exchange.sems],
        compiler_params=_params(("arbitrary",) * len(grid), vmem_limit),
    )(*args, *exchange.ins)


def _run_exchange(exchange, name):
    def body(*refs):
        k_in, k_out = len(exchange.ins), len(exchange.out_shape)
        xin, xout, xsem = refs[:k_in], refs[k_in:k_in + k_out], refs[k_in + k_out:]
        exchange.start(xin, xout, xsem)
        exchange.finish(xin, xout, xsem)

    return pl.pallas_call(
        body, name=name, out_shape=tuple(exchange.out_shape), in_specs=[_ANY] * len(exchange.ins),
        out_specs=tuple([_ANY] * len(exchange.out_shape)), scratch_shapes=list(exchange.sems),
    )(*exchange.ins)


def _gather_shards(wsh):
    _, hh, _ = wsh.shape

    def first_hop(w_ref, out_ref, ssems, rsems):
        x, y, c, chips = _place()
        t = 2 * x + y
        return [_remote(w_ref.at[c], out_ref.at[t, c], ssems.at[j], rsems.at[j], (cx, cy, c))
                for j, (cx, cy) in enumerate(chips)]

    def start(xin, xout, xsem):
        for cp in first_hop(xin[0], xout[0], *xsem):
            cp.start()

    def finish(xin, xout, xsem):
        out_ref, (ssems, rsems) = xout[0], xsem
        first = first_hop(xin[0], out_ref, *xsem)
        x, y, c, chips = _place()
        passed = []
        for j, (cx, cy) in enumerate(chips):
            got = out_ref.at[2 * cx + cy, c]
            _remote(got, got, ssems.at[j], rsems.at[j], (cx, cy, c)).wait_recv()
            cp = _remote(got, got, ssems.at[3 + j], rsems.at[3 + j], (x, y, 1 - c))
            cp.start()
            passed.append(cp)
        for j, (cx, cy) in enumerate(chips):
            got = out_ref.at[2 * cx + cy, 1 - c]
            _remote(got, got, ssems.at[3 + j], rsems.at[3 + j], (x, y, 1 - c)).wait_recv()
        for cp in first + passed:
            cp.wait_send()

    return _Exchange([wsh], [jax.ShapeDtypeStruct((4, 2, hh, 128), wsh.dtype)],
                     [pltpu.SemaphoreType.DMA((6,)), pltpu.SemaphoreType.DMA((6,))], start, finish)


def _gathered(landed, own):
    t = 2 * lax.axis_index("x") + lax.axis_index("y")
    return lax.dynamic_update_slice(landed, own[None], (t, 0, 0, 0))


def _swap_sibling(sh, rp):
    hs, rr = sh.shape[1] // 2, rp.shape[2]

    def copies(xin, xout, xsem):
        x, y, c, _ = _place()
        sib = (x, y, 1 - c)
        rows = xin[0].at[:, pl.ds(pl.multiple_of((1 - c) * hs, 8), hs)]
        return [_remote(rows, xout[0].at[:, pl.ds(0, hs)], xsem[0].at[0], xsem[1].at[0], sib),
                _remote(xin[1].at[:, 1 - c], xout[0].at[:, pl.ds(hs, rr)], xsem[0].at[1], xsem[1].at[1], sib)]

    def start(*a):
        for cp in copies(*a):
            cp.start()

    def finish(*a):
        for cp in copies(*a):
            cp.wait()

    return _Exchange([sh, rp], [jax.ShapeDtypeStruct((4, hs + rr, 128), F32)],
                     [pltpu.SemaphoreType.DMA((2,)), pltpu.SemaphoreType.DMA((2,))], start, finish)


def _scatter_chips(parts):
    n = len(parts)

    def copies(xin, xout, ssems, rsems):
        x, y, c, chips = _place()
        t = 2 * x + y
        return [_remote(xin[k].at[2 * cx + cy], xout[k].at[t], ssems.at[n * j + k], rsems.at[n * j + k], (cx, cy, c))
                for j, (cx, cy) in enumerate(chips) for k in range(n)]

    def start(xin, xout, xsem):
        for cp in copies(xin, xout, *xsem):
            cp.start()

    def finish(xin, xout, xsem):
        ssems, rsems = xsem
        x, y, c, chips = _place()
        for j, (cx, cy) in enumerate(chips):
            for k in range(n):
                got = xout[k].at[2 * cx + cy]
                _remote(got, got, ssems.at[n * j + k], rsems.at[n * j + k], (cx, cy, c)).wait_recv()
        for cp in copies(xin, xout, *xsem):
            cp.wait_send()

    return _Exchange(parts, [jax.ShapeDtypeStruct(a.shape, a.dtype) for a in parts],
                     [pltpu.SemaphoreType.DMA((3 * n,)), pltpu.SemaphoreType.DMA((3 * n,))], start, finish)


def _share_reduced(piece, eighth):
    def copies(t_ref, mine_r, sib_ref, rall_ref, ssems, rsems, lsem):
        x, y, c, _ = _place()
        me = 4 * x + 2 * y + c
        loc = pltpu.make_async_copy(mine_r, rall_ref.at[me], lsem)
        sends = [_remote(t_ref, sib_ref, ssems.at[0], rsems.at[0], (x, y, 1 - c))]
        peers = []
        for mask in range(1, 8):
            px = 1 - x if mask & 4 else x
            py = 1 - y if mask & 2 else y
            pc = 1 - c if mask & 1 else c
            peers.append((mask, px, py, pc))
            sends.append(_remote(mine_r, rall_ref.at[me], ssems.at[mask], rsems.at[mask], (px, py, pc)))
        return loc, sends, peers

    def start(xin, xout, xsem):
        loc, sends, _ = copies(*xin, *xout, *xsem)
        for cp in [loc] + sends:
            cp.start()

    def finish(xin, xout, xsem):
        (sib_ref, rall_ref), (ssems, rsems, _) = xout, xsem
        loc, sends, peers = copies(*xin, *xout, *xsem)
        x, y, c, _ = _place()
        _remote(sib_ref, sib_ref, ssems.at[0], rsems.at[0], (x, y, 1 - c)).wait_recv()
        for mask, px, py, pc in peers:
            got = rall_ref.at[4 * px + 2 * py + pc]
            _remote(got, got, ssems.at[mask], rsems.at[mask], (px, py, pc)).wait_recv()
        for cp in sends:
            cp.wait_send()
        loc.wait()

    return _Exchange([piece, eighth],
                     [jax.ShapeDtypeStruct(piece.shape, F32), jax.ShapeDtypeStruct((8, *eighth.shape), F32)],
                     [pltpu.SemaphoreType.DMA((8,)), pltpu.SemaphoreType.DMA((8,)), pltpu.SemaphoreType.DMA],
                     start, finish)


def _ceil_to(n, m):
    return -(-n // m) * m


def _pack_bf16(parts):
    blocks = [p.reshape(-1, 128) for p in parts]
    rows = jnp.concatenate([jnp.pad(b, ((0, -b.shape[0] % 16), (0, 0))) for b in blocks])
    hw = _ceil_to(rows.shape[0], 32) // 2
    return jnp.pad(rows, ((0, 2 * hw - rows.shape[0]), (0, 0))).reshape(2, hw, 128)


def _segments(wall, parts):
    wall = wall.reshape(4, -1, 128)
    out, row = [], 0
    for p in parts:
        n = p.size // 128
        out.append(wall[:, row:row + n].reshape(4, *p.shape))
        row += _ceil_to(n, 16)
    return out


class _GradReduce:
    def __init__(self, sharded, replicated, c_idx, wire_bf16=False):
        self.c_idx, self.wire_bf16 = c_idx, wire_bf16
        self.rowwise = [(n, g.shape[1:]) for n, g in sharded if math.prod(g.shape[1:]) % 128 == 0]
        self.small = [(n, g.shape[1:]) for n, g in sharded if math.prod(g.shape[1:]) % 128 != 0]
        self.replicated = [(n, g.shape[0]) for n, g in replicated]
        by_name = dict(sharded)
        blocks = [by_name[n].reshape(4, -1, 128) for n, _ in self.rowwise]
        if self.small:
            rest = jnp.concatenate([by_name[n].reshape(4, -1) for n, _ in self.small], axis=1)
            blocks.append(jnp.pad(rest, ((0, 0), (0, -rest.shape[1] % 128))).reshape(4, -1, 128))
        blocks = [jnp.pad(b, ((0, 0), (0, -b.shape[1] % 8), (0, 0))) for b in blocks]
        rows = sum(b.shape[1] for b in blocks)
        self.hs = _ceil_to(rows, 256) // 2
        sh = jnp.concatenate(blocks + [jnp.zeros((4, 2 * self.hs - rows, 128), F32)], axis=1)
        rp = jnp.concatenate([g for _, g in replicated])
        self.rr = _ceil_to(_ceil_to(rp.shape[0], 128) // 128, 64) // 8
        rp = jnp.pad(rp, (0, 8 * self.rr * 128 - rp.shape[0])).reshape(4, 2, self.rr, 128)

        self.sh, self.rp = sh, rp

    def swap(self):
        return _swap_sibling(self.sh, self.rp)

    def swapped(self, got):
        self.chip_sum, *wire = _add_own_half(self.sh, got, self.c_idx, self.wire_bf16)
        self.wire = wire[0] if wire else self.chip_sum
        mine_r = lax.dynamic_index_in_dim(self.rp, self.c_idx, axis=1, keepdims=False)
        self.chip_r = _add2(mine_r.reshape(-1, 128), got[:, self.hs:].reshape(-1, 128)).reshape(4, self.rr, 128)

    def scatter(self):
        return _scatter_chips([self.wire, self.chip_r])

    def scattered(self, landed, landed_r):
        self.piece = _sum_slots(landed, self.chip_sum, self.hs)
        self.eighth = _sum_slots(landed_r, self.chip_r, self.rr)

    def share(self):
        return _share_reduced(self.piece, self.eighth)

    def shared(self, sibling, rall):
        mine, sib = self.piece, sibling
        self.shard = jnp.where(self.c_idx == 0, jnp.concatenate([mine, sib]), jnp.concatenate([sib, mine]))
        self.rall = rall

    def reduced(self):
        out, row = {}, 0
        for name, shape in self.rowwise:
            rows = math.prod(shape) // 128
            out[name] = self.shard[row:row + rows].reshape(shape)
            row += _ceil_to(rows, 8)
        for group, flat in ((self.small, self.shard[row:].reshape(-1)), (self.replicated, self.rall.reshape(-1))):
            off = 0
            for name, shape in group:
                n = math.prod(shape) if isinstance(shape, tuple) else shape
                out[name] = flat[off:off + n]
                off += n
        return out


def _col_shards(w2d):
    rows, cols = w2d.shape
    return w2d.reshape(rows, 4, cols // 4).transpose(1, 0, 2)


_WIN0_PARTS = ((0, 384, 1280), (384, 640, 1664), (640, 672, 1984), (672, 1696, 0), (1696, 1952, 1024))


def _win0_aligned(shards):
    def cols(a, b):
        return [shards[s][:, max(a, 488 * s) - 488 * s:min(b, 488 * (s + 1)) - 488 * s]
                for s in range(4) if max(a, 488 * s) < min(b, 488 * (s + 1))]

    zeros = jnp.zeros((1024, 64), shards.dtype)
    return jnp.concatenate(cols(672, 1696) + cols(1696, 1952) + cols(0, 384) + cols(384, 640)
                           + [zeros] + cols(640, 672) + [zeros[:, :32]], axis=1)


def _win0_shards(dwin0p):
    shards = []
    for s in range(4):
        lo, hi = 488 * s, 488 * (s + 1)
        cols = [dwin0p[:, p + max(lo, a) - a:p + min(hi, b) - a] for a, b, p in _WIN0_PARTS if max(lo, a) < min(hi, b)]
        shards.append(jnp.concatenate(cols, axis=1))
    return jnp.stack(shards)


def _block_diag4(w):
    eye = jnp.eye(4, dtype=w.dtype)
    return jnp.einsum("gaij,ab->gaibj", w.reshape(3, 4, 64, 64), eye).reshape(3, 256, 256)


def _diag_blocks4(w):
    w5 = w.reshape(3, 4, 64, 4, 64)
    return jnp.stack([w5[:, a, :, a, :] for a in range(4)], axis=1).reshape(12, 64, 64)


def kernel(x, mem, positions, mla_w_in, mla_q_norm, mla_w_uq, mla_kv_norm, mla_w_ukv, lru_w_in, lru_conv_w, lru_conv_b, lru_w_rgate, lru_b_rgate, lru_w_igate, lru_b_igate, lru_lambda, w_mem_kv, w_out, ln_g, ln_b, loss_target, m_mla_w_in, m_mla_q_norm, m_mla_w_uq, m_mla_kv_norm, m_mla_w_ukv, m_lru_w_in, m_lru_conv_w, m_lru_conv_b, m_lru_w_rgate, m_lru_b_rgate, m_lru_w_igate, m_lru_b_igate, m_lru_lambda, m_w_mem_kv, m_w_out, m_ln_g, m_ln_b, v_mla_w_in, v_mla_q_norm, v_mla_w_uq, v_mla_kv_norm, v_mla_w_ukv, v_lru_w_in, v_lru_conv_w, v_lru_conv_b, v_lru_w_rgate, v_lru_b_rgate, v_lru_w_igate, v_lru_b_igate, v_lru_lambda, v_w_mem_kv, v_w_out, v_ln_g, v_ln_b):
    s = x.shape[1]
    c_idx = lax.axis_index("c")
    x2, mem2, tgt2 = x[0], mem[0], loss_target[0]

    first = [p.astype(BF16) for p in (mla_w_in[0], mla_w_uq[0], mla_w_ukv[0])]
    buf = _pack_bf16(first)
    mla_shards = _segments(_gathered(_run_exchange(_gather_shards(buf), "gather_weights")[0], buf), first)

    mid = [w_mem_kv.astype(BF16), w_out[0].astype(BF16)]
    buf_mid = _pack_bf16(mid)

    def mid_weights(landed):
        wmem, wout0 = _segments(_gathered(landed[0], buf_mid), mid)
        return wmem.transpose(1, 0, 2, 3).reshape(2, 1024, 512), wout0.reshape(1024, 1024)

    small = jnp.concatenate([lru_conv_w[0].reshape(-1), lru_conv_b[0], lru_b_rgate[0], lru_b_igate[0], lru_lambda[0]])
    late = [lru_w_in[0].astype(BF16), w_out[1].astype(BF16), lax.bitcast_convert_type(small, BF16)]
    buf_late = _pack_bf16(late)

    def late_weights(landed):
        win1, wout1, small_bits = _segments(_gathered(landed[0], buf_late), late)
        small_all = lax.bitcast_convert_type(small_bits, F32)
        cw = small_all[:, :768].reshape(4, 4, 192).transpose(1, 0, 2).reshape(4, TOK_W)
        cb, br, bi, lam = (small_all[:, 768 + 192 * k:960 + 192 * k].reshape(1, TOK_W) for k in range(4))
        return win1, wout1.reshape(1024, 1024), cw, cb, br, bi, lam

    def reduce_late(g):
        return _GradReduce(
            [("lru_w_in", g["lru_w_in"]), ("lru_conv_w", _col_shards(g["lru_conv_w"])),
             ("lru_conv_b", _col_shards(g["lru_conv_b"])), ("lru_b_rgate", _col_shards(g["lru_b_rgate"])),
             ("lru_b_igate", _col_shards(g["lru_b_igate"])), ("lru_lambda", _col_shards(g["lru_lambda"])),
             ("w_mem_kv1", g["w_mem_kv1"].reshape(4, 256, 512)), ("w_out1", g["w_out1"].reshape(4, 256, 1024))],
            [("lru_w_rgate", g["lru_w_rgate"].reshape(-1)), ("lru_w_igate", g["lru_w_igate"].reshape(-1)),
             ("ln_g1", g["ln_g1"].reshape(-1)), ("ln_b1", g["ln_b1"].reshape(-1))], c_idx)

    g0, late_red = _local_step(
        x2, mem2, positions.reshape(s, 1), tgt2, *mla_shards, mla_q_norm, mla_kv_norm, lru_w_rgate[0], lru_w_igate[0],
        ln_g, ln_b, mid_weights, late_weights, _gather_shards(buf_mid), _gather_shards(buf_late), reduce_late)

    early_red = _GradReduce(
        [("mla_w_in", g0["mla_w_in"]), ("mla_w_uq", _col_shards(g0["mla_w_uq"])),
         ("mla_w_ukv", _col_shards(g0["mla_w_ukv"])), ("w_mem_kv0", g0["w_mem_kv0"].reshape(4, 256, 512)),
         ("w_out0", g0["w_out0"].reshape(4, 256, 1024))],
        [("mla_q_norm", g0["mla_q_norm"].reshape(-1)), ("mla_kv_norm", g0["mla_kv_norm"].reshape(-1)),
         ("ln_g0", g0["ln_g0"].reshape(-1)), ("ln_b0", g0["ln_b0"].reshape(-1)), ("loss", g0["loss"].reshape(-1))],
        c_idx, wire_bf16=True)
    early_red.swapped(*_run_exchange(early_red.swap(), "swap_sibling"))
    early_red.scattered(*_run_exchange(early_red.scatter(), "scatter_chips"))
    early_red.shared(*_run_exchange(early_red.share(), "share_reduced"))
    red = {**late_red.reduced(), **early_red.reduced()}
    red["w_mem_kv"] = jnp.concatenate([red["w_mem_kv0"], red["w_mem_kv1"]])
    red["w_out"] = jnp.concatenate([red["w_out0"], red["w_out1"]])
    red["ln_g"] = jnp.concatenate([red["ln_g0"], red["ln_g1"]])
    red["ln_b"] = jnp.concatenate([red["ln_b0"], red["ln_b1"]])

    weights = dict(mla_w_in=mla_w_in, mla_q_norm=mla_q_norm, mla_w_uq=mla_w_uq, mla_kv_norm=mla_kv_norm,
                   mla_w_ukv=mla_w_ukv, lru_w_in=lru_w_in, lru_conv_w=lru_conv_w, lru_conv_b=lru_conv_b,
                   lru_w_rgate=lru_w_rgate, lru_b_rgate=lru_b_rgate, lru_w_igate=lru_w_igate, lru_b_igate=lru_b_igate,
                   lru_lambda=lru_lambda, w_mem_kv=w_mem_kv, w_out=w_out, ln_g=ln_g, ln_b=ln_b)
    m_in = dict(mla_w_in=m_mla_w_in, mla_q_norm=m_mla_q_norm, mla_w_uq=m_mla_w_uq, mla_kv_norm=m_mla_kv_norm,
                mla_w_ukv=m_mla_w_ukv, lru_w_in=m_lru_w_in, lru_conv_w=m_lru_conv_w, lru_conv_b=m_lru_conv_b,
                lru_w_rgate=m_lru_w_rgate, lru_b_rgate=m_lru_b_rgate, lru_w_igate=m_lru_w_igate,
                lru_b_igate=m_lru_b_igate, lru_lambda=m_lru_lambda, w_mem_kv=m_w_mem_kv, w_out=m_w_out, ln_g=m_ln_g,
                ln_b=m_ln_b)
    v_in = dict(mla_w_in=v_mla_w_in, mla_q_norm=v_mla_q_norm, mla_w_uq=v_mla_w_uq, mla_kv_norm=v_mla_kv_norm,
                mla_w_ukv=v_mla_w_ukv, lru_w_in=v_lru_w_in, lru_conv_w=v_lru_conv_w, lru_conv_b=v_lru_conv_b,
                lru_w_rgate=v_lru_w_rgate, lru_b_rgate=v_lru_b_rgate, lru_w_igate=v_lru_w_igate,
                lru_b_igate=v_lru_b_igate, lru_lambda=v_lru_lambda, w_mem_kv=v_w_mem_kv, w_out=v_w_out, ln_g=v_ln_g,
                ln_b=v_ln_b)
    order = ["mla_w_in", "mla_q_norm", "mla_w_uq", "mla_kv_norm", "mla_w_ukv", "lru_w_in", "lru_conv_w", "lru_conv_b",
             "lru_w_rgate", "lru_b_rgate", "lru_w_igate", "lru_b_igate", "lru_lambda", "w_mem_kv", "w_out", "ln_g",
             "ln_b"]
    grads, deltas, new_m, new_v = {}, {}, {}, {}

    def operands(name):
        shape = weights[name].shape
        two_d = (math.prod(shape[:-1]), shape[-1])
        return [a.reshape(two_d) for a in (weights[name], red[name], m_in[name], v_in[name])]

    def keep(name, g2, d2, m2, v2):
        shape = weights[name].shape
        grads[name], deltas[name] = g2.reshape(shape), d2.reshape(shape)
        new_m[name], new_v[name] = m2.reshape(shape), v2.reshape(shape)

    small = [n for n in order if weights[n].size <= 4096]
    ops = [operands(n) for n in small]
    for name, op, res in zip(small, ops, _adamw_small(ops)):
        keep(name, op[1], *res)
    for name in order:
        if name not in small:
            op = operands(name)
            keep(name, op[1], *_adamw(name, *op))
    return (red["loss"][0], g0["x"][None], *[grads[n] for n in order], *[deltas[n] for n in order],
            *[new_m[n] for n in order], *[new_v[n] for n in order])


def _local_step(x2, mem2, pos_col, tgt2, win0_sh, wuq_sh, wukv_sh, gq, gkv, w_rgate, w_igate, ln_g, ln_b,
                mid_weights, late_weights, gather_mid=None, gather_late=None, reduce_late=None):
    s = x2.shape[0]
    win0p = _win0_aligned(win0_sh)
    wuq_p = jnp.pad(wuq_sh.reshape(4, Q_LORA, 3, 96).transpose(1, 0, 2, 3).reshape(Q_LORA, 12, 96),
                    ((0, 0), (0, 0), (0, 32))).reshape(Q_LORA, QK_W)
    wukv3 = wukv_sh.reshape(4, KV_LORA, 3, 128).transpose(1, 0, 2, 3).reshape(KV_LORA, 12, 128)
    wk_p = jnp.pad(wukv3[:, :, :64], ((0, 0), (0, 0), (0, 64))).reshape(KV_LORA, QK_W)
    wv = wukv3[:, :, 64:].reshape(KV_LORA, TOK_W)
    wr_bd = _block_diag4(w_rgate).astype(BF16)
    wi_bd = _block_diag4(w_igate).astype(BF16)
    half = 16
    inv_freq = ROPE_THETA ** (-jnp.arange(half, dtype=F32) / half)
    inv_lane = jnp.concatenate([jnp.zeros((64,), F32), inv_freq, inv_freq, jnp.zeros((32,), F32)]).reshape(1, HEAD_PAD)

    gate0, qm0, cq, ckv, q_p, q_t, k_p, v_b, v_t, ctab, satab, sbtab, *landed = _mla_proj_fwd(
        x2, win0p, gq, gkv, wuq_p, wk_p, wv, pos_col, inv_lane, exchange=gather_mid)
    wmem, wout0 = mid_weights(landed)
    memkv = _mem_kv(mem2, wmem)
    tok0, lse, *landed = _attn_fwd(q_p, k_p, v_t, exchange=gather_late)
    win1, wout1, cw, cb, br, bi, lam = late_weights(landed)
    g0, b0, g1, b1 = ln_g[0:1], ln_b[0:1], ln_g[1:2], ln_b[1:2]
    h1 = _mix_fwd(tok0, gate0, qm0, memkv[0], wout0, x2, g0, b0)
    u1, gate1, qm1, hs1 = _lru_fwd(h1, win1, cw, cb, wr_bd, br, wi_bd, bi, lam)

    dres1, dtok1, dgate1, dqm1, dwout1, dmemkv1, dg1, db1, loss = _mix_bwd(
        hs1, gate1, qm1, memkv[1], wout1, h1, g1, b1, tgt2, True)
    dh1, dwin1, dcw, dcb, dwr_bd, dbr, dwi_bd, dbi, dlam = _lru_bwd(
        dtok1, dgate1, dqm1, dres1, h1, u1, hs1, win1, cw, cb, wr_bd, br, wi_bd, bi, lam)
    late = {"lru_w_in": dwin1, "lru_conv_w": dcw, "lru_conv_b": dcb, "lru_b_rgate": dbr, "lru_b_igate": dbi,
            "lru_lambda": dlam, "w_mem_kv1": _mem_kv_bwd(mem2, dmemkv1), "w_out1": dwout1,
            "lru_w_rgate": _diag_blocks4(dwr_bd), "lru_w_igate": _diag_blocks4(dwi_bd), "ln_g1": dg1, "ln_b1": db1}
    red = reduce_late(late) if reduce_late is not None else None

    dres0, dob, dobt, stats, dgate0, dqm0, dwout0, dmemkv0, dg0, db0, _, *got = _mix_bwd(
        tok0, gate0, qm0, memkv[0], wout0, x2, g0, b0, dh1, False, lse=lse, exchange=red.swap() if red else None)
    if red:
        red.swapped(*got)
    dq_p, dk_t, dv_t, *got = _attn_bwd(q_p, q_t, k_p, v_b, dob, dobt, stats,
                                       exchange=red.scatter() if red else None)
    if red:
        red.scattered(*got)
    if red:
        red.shared(*_run_exchange(red.share(), "share_reduced"))
    dx, dwin0p, dwuq_p, dwk_p, dwv, dgq, dgkv = _mla_proj_bwd(
        x2, cq, ckv, dq_p, dk_t, dv_t, dgate0, dqm0, dres0, win0p, gq, gkv, wuq_p, wk_p, wv,
        ctab, satab, sbtab)

    dwin0 = _win0_shards(dwin0p)
    dwuq = dwuq_p.reshape(Q_LORA, 12, 128)[:, :, :96].reshape(Q_LORA, 1152)
    dwukv = jnp.concatenate([dwk_p.reshape(KV_LORA, 12, 128)[:, :, :64], dwv.reshape(KV_LORA, 12, 64)],
                            axis=2).reshape(KV_LORA, 1536)
    early = {"x": dx, "loss": loss, "mla_w_in": dwin0, "mla_w_uq": dwuq, "mla_w_ukv": dwukv,
             "w_mem_kv0": _mem_kv_bwd(mem2, dmemkv0), "w_out0": dwout0, "mla_q_norm": dgq, "mla_kv_norm": dgkv,
             "ln_g0": dg0, "ln_b0": db0}
    return early, (red if red else late)
```

```python
import functools
import math

import jax
import jax.numpy as jnp
from jax import lax
from jax.experimental import pallas as pl
from jax.experimental.pallas import tpu as pltpu

F32, BF16 = jnp.float32, jnp.bfloat16
MESH = pl.DeviceIdType.MESH

D_MODEL = 1024
N_TOK_HEADS = 12
TOK_W = 768
MEM_W = 256
MEM_LEN = 256
Q_LORA, KV_LORA = 384, 256
HEAD_PAD = 128
QK_W = N_TOK_HEADS * HEAD_PAD
ATT_SCALE = 1.0 / math.sqrt(96.0)
ATT_SCALE_LOG2 = ATT_SCALE * math.log2(math.e)
ROPE_THETA = 10000.0
LRU_C = 8.0
ALPHA = 4.0 ** 0.25
NORM_EPS = 1e-6
ADAM_LR, ADAM_B1, ADAM_B2, ADAM_EPS, ADAM_WD, ADAM_STEP = 0.001, 0.9, 0.999, 1e-08, 0.01, 10

TB_PROJ = 512
TB_PROJ_BWD = 512
TB_MIX = 512
TB_MIX_FWD = 1024
TB_LRU = 256
TQ_ATT = 512
TQ_ATT_FWD = 1024
TK_ATT = 1024
VMEM_LIMIT = 56 * 1024 * 1024
VMEM_LIMIT_PROJ_BWD = 60 * 1024 * 1024


def _mm(a, b):
    return jnp.dot(a.astype(BF16), b.astype(BF16), preferred_element_type=F32)


def _mm_nt(a, b):
    return lax.dot_general(a.astype(BF16), b.astype(BF16), (((1,), (1,)), ((), ())), preferred_element_type=F32)


def _mm_tn(a, b):
    return lax.dot_general(a.astype(BF16), b.astype(BF16), (((0,), (0,)), ((), ())), preferred_element_type=F32)


def _rows(tb, w):
    return pl.BlockSpec((tb, w), lambda i: (i, 0))


def _const(shape):
    n = len(shape)
    return pl.BlockSpec(shape, lambda i: (0,) * n)


def _params(sem=("arbitrary",), vmem_limit=None):
    return pltpu.CompilerParams(dimension_semantics=sem, vmem_limit_bytes=vmem_limit or VMEM_LIMIT)


def _iota(shape, dim):
    return lax.broadcasted_iota(jnp.int32, shape, dim)


def _rope_tables(pos, inv_lane):
    ang = pos.astype(F32) * inv_lane
    lane = _iota(ang.shape, 1)
    cs, sn = jnp.cos(ang), jnp.sin(ang)
    return (jnp.where(lane < 64, 1.0, jnp.where(lane < 96, cs, 0.0)),
            jnp.where((lane >= 64) & (lane < 80), -sn, 0.0), jnp.where((lane >= 80) & (lane < 96), sn, 0.0))


def _rope(t, c, sa, sb):
    return t * c + pltpu.roll(t, 112, 1) * sa + pltpu.roll(t, 16, 1) * sb


def _rope_t(d, c, sa, sb):
    return d * c + pltpu.roll(d * sa, 16, 1) + pltpu.roll(d * sb, 112, 1)


def _rms(c, g):
    r = lax.rsqrt(jnp.mean(c * c, axis=-1, keepdims=True) + NORM_EPS)
    xh = c * r
    return xh * g, xh, r


def _mla_proj_fwd(x, win, gq, gkv, wuq, wukv_k, wukv_v, pos_col, inv_lane, exchange=None):
    s = x.shape[0]
    tb = min(TB_PROJ, s)

    def body(x_ref, win_ref, gq_ref, gkv_ref, wuq_ref, wk_ref, wv_ref, pos_ref, inv_ref,
             gate_ref, qm_ref, cq_ref, ckv_ref, q_ref, qt_ref, k_ref, v_ref, vt_ref, c_ref, sa_ref, sb_ref):
        z = _mm(x_ref[...], win_ref[...])
        gate_ref[...] = z[:, 0:1024]
        qm_ref[...] = z[:, 1024:1280].astype(BF16)
        cq = z[:, 1280:1664]
        ckv = z[:, 1664:1920]
        cq_ref[...] = cq
        ckv_ref[...] = ckv
        c, sa, sb = _rope_tables(pos_ref[...], inv_ref[...])
        c_ref[...], sa_ref[...], sb_ref[...] = c, sa, sb
        nq, _, _ = _rms(cq, gq_ref[...])
        nkv, _, _ = _rms(ckv, gkv_ref[...])
        qf = _mm(nq, wuq_ref[...])
        kf = _mm(nkv, wk_ref[...])
        vf = _mm(nkv, wv_ref[...])
        v_ref[...] = vf.astype(BF16)
        for j in range(N_TOK_HEADS // 2):
            sl = slice(HEAD_PAD * j, HEAD_PAD * (j + 1))
            vt_ref[sl, :] = vf[:, sl].T.astype(BF16)
        kr = _rope(z[:, 1920:2048], c, sa, sb)
        for h in range(N_TOK_HEADS):
            sl = slice(HEAD_PAD * h, HEAD_PAD * (h + 1))
            qh = _rope(qf[:, sl], c, sa, sb) * ATT_SCALE_LOG2
            q_ref[:, sl] = qh.astype(BF16)
            qt_ref[sl, :] = qh.T.astype(BF16)
            k_ref[:, sl] = (kf[:, sl] + kr).astype(BF16)

    outs = (jax.ShapeDtypeStruct((s, 1024), F32), jax.ShapeDtypeStruct((s, MEM_W), BF16),
            jax.ShapeDtypeStruct((s, Q_LORA), F32), jax.ShapeDtypeStruct((s, KV_LORA), F32),
            jax.ShapeDtypeStruct((s, QK_W), BF16), jax.ShapeDtypeStruct((QK_W, s), BF16),
            jax.ShapeDtypeStruct((s, QK_W), BF16),
            jax.ShapeDtypeStruct((s, TOK_W), BF16), jax.ShapeDtypeStruct((TOK_W, s), BF16),
            *[jax.ShapeDtypeStruct((s, HEAD_PAD), F32)] * 3)

    def cols(w):
        return pl.BlockSpec((w, tb), lambda i: (0, i))

    return _run(
        body, name="mla_proj_fwd", grid=(s // tb,), out_shape=outs,
        in_specs=[_rows(tb, 1024), _const((1024, 2048)), _const((1, Q_LORA)), _const((1, KV_LORA)),
                  _const((Q_LORA, QK_W)), _const((KV_LORA, QK_W)), _const((KV_LORA, TOK_W)),
                  _rows(tb, 1), _const((1, HEAD_PAD))],
        out_specs=(_rows(tb, 1024), _rows(tb, MEM_W), _rows(tb, Q_LORA), _rows(tb, KV_LORA),
                   _rows(tb, QK_W), cols(QK_W), _rows(tb, QK_W), _rows(tb, TOK_W), cols(TOK_W),
                   _rows(tb, HEAD_PAD), _rows(tb, HEAD_PAD), _rows(tb, HEAD_PAD)),
        args=(x, win, gq, gkv, wuq, wukv_k, wukv_v, pos_col, inv_lane), sem=("parallel",), exchange=exchange)


def _mla_proj_bwd(x, cq, ckv, dq, dkt, dvt, dgate, dqm, dres, win, gq, gkv, wuq, wukv_k, wukv_v, ctab, satab, sbtab):
    s = x.shape[0]
    tb = min(TB_PROJ_BWD, s)

    def body(x_ref, cq_ref, ckv_ref, dq_ref, dkt_ref, dvt_ref, dgate_ref, dqm_ref, dres_ref, win_ref, gq_ref, gkv_ref,
             wuq_ref, wk_ref, wv_ref, c_ref, sa_ref, sb_ref,
             dx_ref, dwin_ref, dwuq_ref, dwk_ref, dwv_ref, dgq_ref, dgkv_ref):
        @pl.when(pl.program_id(0) == 0)
        def _():
            for r in (dwin_ref, dwuq_ref, dwk_ref, dwv_ref, dgq_ref, dgkv_ref):
                r[...] = jnp.zeros_like(r)

        c, sa, sb = c_ref[...], sa_ref[...], sb_ref[...]
        lane = _iota((tb, HEAD_PAD), 1)
        gq, gkv = gq_ref[...], gkv_ref[...]
        nq, xhq, rq = _rms(cq_ref[...], gq)
        nkv, xhk, rk = _rms(ckv_ref[...], gkv)
        dkp = dkt_ref[...].T * math.log(2.0)
        dqs, dkr = [], jnp.zeros((tb, HEAD_PAD), F32)
        for h in range(N_TOK_HEADS):
            sl = slice(HEAD_PAD * h, HEAD_PAD * (h + 1))
            dqs.append(_rope_t(dq_ref[:, sl], c, sa, sb).astype(BF16))
            dkr = dkr + dkp[:, sl]
        dqf = jnp.concatenate(dqs, axis=1)
        dkr = jnp.where((lane >= 64) & (lane < 96), _rope_t(dkr, c, sa, sb), 0.0)
        dvb = dvt_ref[...].T.astype(BF16)
        dkb = dkp.astype(BF16)
        dnq = _mm_nt(dqf, wuq_ref[...])
        dwuq_ref[...] += _mm_tn(nq, dqf)
        dgq_ref[...] += jnp.sum(dnq * xhq, axis=0, keepdims=True)
        dxh = dnq * gq
        dcq = rq * (dxh - xhq * jnp.mean(dxh * xhq, axis=-1, keepdims=True))
        dnkv = _mm_nt(dkb, wk_ref[...]) + _mm_nt(dvb, wv_ref[...])
        nkvb = nkv.astype(BF16)
        dwk_ref[...] += _mm_tn(nkvb, dkb)
        dwv_ref[...] += _mm_tn(nkvb, dvb)
        dgkv_ref[...] += jnp.sum(dnkv * xhk, axis=0, keepdims=True)
        dxh = dnkv * gkv
        dckv = rk * (dxh - xhk * jnp.mean(dxh * xhk, axis=-1, keepdims=True))
        dz = jnp.concatenate([dgate_ref[...], dqm_ref[...], dcq, dckv, dkr], axis=1).astype(BF16)
        dx_ref[...] = _mm_nt(dz, win_ref[...]) + dres_ref[...]
        dwin_ref[...] += _mm_tn(x_ref[...], dz)

    outs = (jax.ShapeDtypeStruct((s, 1024), F32), jax.ShapeDtypeStruct((1024, 2048), F32),
            jax.ShapeDtypeStruct((Q_LORA, QK_W), F32), jax.ShapeDtypeStruct((KV_LORA, QK_W), F32),
            jax.ShapeDtypeStruct((KV_LORA, TOK_W), F32), jax.ShapeDtypeStruct((1, Q_LORA), F32),
            jax.ShapeDtypeStruct((1, KV_LORA), F32))
    return _run(
        body, name="mla_proj_bwd", grid=(s // tb,), out_shape=outs,
        in_specs=[_rows(tb, 1024), _rows(tb, Q_LORA), _rows(tb, KV_LORA), _rows(tb, QK_W),
                  pl.BlockSpec((QK_W, tb), lambda i: (0, i)), pl.BlockSpec((TOK_W, tb), lambda i: (0, i)),
                  _rows(tb, 1024), _rows(tb, MEM_W), _rows(tb, 1024),
                  _const((1024, 2048)), _const((1, Q_LORA)), _const((1, KV_LORA)),
                  _const((Q_LORA, QK_W)), _const((KV_LORA, QK_W)), _const((KV_LORA, TOK_W)),
                  _rows(tb, HEAD_PAD), _rows(tb, HEAD_PAD), _rows(tb, HEAD_PAD)],
        out_specs=(_rows(tb, 1024), _const((1024, 2048)), _const((Q_LORA, QK_W)), _const((KV_LORA, QK_W)),
                   _const((KV_LORA, TOK_W)), _const((1, Q_LORA)), _const((1, KV_LORA))),
        args=(x, cq, ckv, dq, dkt, dvt, dgate, dqm, dres, win, gq, gkv, wuq, wukv_k, wukv_v, ctab, satab, sbtab),
        sem=("arbitrary",), vmem_limit=VMEM_LIMIT_PROJ_BWD)


def _attn_fwd(q, k, vt, exchange=None):
    s = q.shape[0]
    tq = min(TQ_ATT_FWD, s)
    tk = min(TK_ATT, s)

    def body(q_ref, k_ref, vt_ref, o_ref, lse_ref):
        i = pl.program_id(1)
        nfull = (i * tq) // tk
        krow = _iota((tk, tq), 0)
        qpos = i * tq + _iota((tk, tq), 1)

        ones = jnp.ones((16, tk), BF16)

        def head_tile(hh, st, carry, masked):
            hs = slice(HEAD_PAD * hh, HEAD_PAD * (hh + 1))
            m, acc = carry
            sc = _mm_nt(k_ref[pl.ds(st, tk), hs], q_ref[:, hs])
            if masked:
                sc = jnp.where(st + krow <= qpos, sc, -jnp.inf)
            m_new = jnp.maximum(m, jnp.max(sc, axis=0, keepdims=True))
            p = jnp.exp2(sc - m_new).astype(BF16)
            a = jnp.exp2(m - m_new)
            va = jnp.concatenate([vt_ref[64 * hh:64 * (hh + 1), pl.ds(st, tk)], ones], axis=0)
            return m_new, a * acc + jnp.dot(va, p, preferred_element_type=F32)

        def tile(j, carry, masked):
            st = pl.multiple_of(j * tk, tk)
            return tuple(head_tile(hh, st, carry[hh], masked) for hh in range(2))

        def init():
            return (jnp.full((1, tq), -jnp.inf, F32), jnp.zeros((80, tq), F32))

        carry = lax.fori_loop(0, nfull, functools.partial(tile, masked=False), (init(), init()))
        (ma, acca), (mb, accb) = tile(nfull, carry, True)
        la, lb = acca[64:65], accb[64:65]
        o_ref[...] = jnp.concatenate([acca[:64] / la, accb[:64] / lb], axis=0).T
        lse_ref[...] = jnp.concatenate([jnp.broadcast_to(ma + jnp.log2(la), (64, tq)),
                                        jnp.broadcast_to(mb + jnp.log2(lb), (64, tq))], axis=0).T

    shp = jax.ShapeDtypeStruct((s, TOK_W), F32)
    return _run(
        body, name="attn_fwd", grid=(N_TOK_HEADS // 2, s // tq), out_shape=(shp, shp),
        in_specs=[pl.BlockSpec((tq, 2 * HEAD_PAD), lambda j, i: (i, j)),
                  pl.BlockSpec((s, 2 * HEAD_PAD), lambda j, i: (0, j)),
                  pl.BlockSpec((HEAD_PAD, s), lambda j, i: (j, 0))],
        out_specs=(pl.BlockSpec((tq, HEAD_PAD), lambda j, i: (i, j)),) * 2,
        args=(q, k, vt), sem=("parallel", "arbitrary"), exchange=exchange)


def _attn_stats(o, do, lse_ref, dob_ref, dot_ref, st_ref):
    lane = _iota((o.shape[0], HEAD_PAD), 1)
    dob_ref[...] = do.astype(BF16)
    prod = do * o
    for j in range(N_TOK_HEADS // 2):
        sl = slice(HEAD_PAD * j, HEAD_PAD * (j + 1))
        dot_ref[sl, :] = do[:, sl].T.astype(BF16)
        pj = prod[:, sl]
        da = jnp.sum(jnp.where(lane < 64, pj, 0.0), axis=-1, keepdims=True)
        db = jnp.sum(jnp.where(lane >= 64, pj, 0.0), axis=-1, keepdims=True)
        la = lse_ref[:, HEAD_PAD * j:HEAD_PAD * j + 1]
        lb = lse_ref[:, HEAD_PAD * j + 64:HEAD_PAD * j + 65]
        st_ref[j] = jnp.where(lane == 0, la, jnp.where(lane == 1, lb, jnp.where(lane == 2, da,
                                                                                 jnp.where(lane == 3, db, 0.0))))


def _attn_bwd(q, qt, k, v, dob, dobt, stats, exchange=None):
    s = q.shape[0]
    t = min(TQ_ATT, s)
    nq = s // t

    def body(q_ref, qt_ref, do_ref, dot_ref, st_ref, k_ref, v_ref, dq_ref, dkt_ref, dvt_ref):
        i = pl.program_id(1)

        @pl.when(i == 0)
        def _():
            dkt_ref[...] = jnp.zeros_like(dkt_ref)
            dvt_ref[...] = jnp.zeros_like(dvt_ref)

        lane = _iota((t, HEAD_PAD), 1)
        qpos, kcol = _iota((t, t), 0), _iota((t, t), 1)
        do = do_ref[...]
        stats = st_ref[0]

        def head_tile(hh, ks, dq_acc, masked):
            hs = slice(HEAD_PAD * hh, HEAD_PAD * (hh + 1))
            qh = q_ref[:, hs]
            kh = k_ref[pl.ds(ks, t), hs]
            dom = jnp.where((lane < 64) if hh == 0 else (lane >= 64), do, jnp.zeros_like(do))
            lse = stats[:, hh:hh + 1]
            dlt = stats[:, 2 + hh:3 + hh]
            sc = _mm_nt(qh, kh)
            if masked:
                sc = jnp.where(kcol <= qpos, sc, -jnp.inf)
            p = jnp.exp2(sc - lse)
            dp = _mm_nt(dom, v_ref[pl.ds(ks, t), :])
            ds = (p * (dp - dlt)).astype(BF16)
            dvt_ref[64 * hh:64 * (hh + 1), pl.ds(ks, t)] += _mm(dot_ref[64 * hh:64 * (hh + 1), :], p)
            dkt_ref[HEAD_PAD * hh:HEAD_PAD * hh + 96, pl.ds(ks, t)] += _mm(qt_ref[HEAD_PAD * hh:HEAD_PAD * hh + 96, :], ds)
            return dq_acc + _mm(ds, kh)

        def tile(j, carry, masked):
            ks = pl.multiple_of(j * t, t)
            return tuple(head_tile(hh, ks, carry[hh], masked) for hh in range(2))

        zero = jnp.zeros((t, HEAD_PAD), F32)
        carry = lax.fori_loop(0, i, functools.partial(tile, masked=False), (zero, zero))
        dqa, dqb = tile(i, carry, True)
        dq_ref[...] = jnp.concatenate([dqa, dqb], axis=1) * ATT_SCALE

    return _run(
        body, name="attn_bwd", grid=(N_TOK_HEADS // 2, nq),
        out_shape=(jax.ShapeDtypeStruct((s, QK_W), F32), jax.ShapeDtypeStruct((QK_W, s), F32),
                   jax.ShapeDtypeStruct((TOK_W, s), F32)),
        in_specs=[pl.BlockSpec((t, 2 * HEAD_PAD), lambda j, i: (i, j)),
                  pl.BlockSpec((2 * HEAD_PAD, t), lambda j, i: (j, i)),
                  pl.BlockSpec((t, HEAD_PAD), lambda j, i: (i, j)),
                  pl.BlockSpec((HEAD_PAD, t), lambda j, i: (j, i)),
                  pl.BlockSpec((1, t, HEAD_PAD), lambda j, i: (j, i, 0)),
                  pl.BlockSpec((s, 2 * HEAD_PAD), lambda j, i: (0, j)),
                  pl.BlockSpec((s, HEAD_PAD), lambda j, i: (0, j))],
        out_specs=(pl.BlockSpec((t, 2 * HEAD_PAD), lambda j, i: (i, j)),
                   pl.BlockSpec((2 * HEAD_PAD, s), lambda j, i: (j, 0)),
                   pl.BlockSpec((HEAD_PAD, s), lambda j, i: (j, 0))),
        args=(q, qt, dob, dobt, stats, k, v), sem=("parallel", "arbitrary"), exchange=exchange)


def _mem_kv(mem, wmem):
    def body(m_ref, w_ref, o_ref):
        o_ref[0] = _mm(m_ref[...], w_ref[0]).astype(BF16)

    return pl.pallas_call(
        body, name="mem_kv", grid=(2,), out_shape=jax.ShapeDtypeStruct((2, MEM_LEN, 512), BF16),
        in_specs=[_const((MEM_LEN, 1024)), pl.BlockSpec((1, 1024, 512), lambda l: (l, 0, 0))],
        out_specs=pl.BlockSpec((1, MEM_LEN, 512), lambda l: (l, 0, 0)),
        compiler_params=_params(("parallel",)),
    )(mem, wmem)


def _mem_kv_bwd(mem, dmemkv):
    def body(m_ref, d_ref, o_ref):
        o_ref[...] = _mm_tn(m_ref[...], d_ref[...])

    return pl.pallas_call(
        body, name="mem_kv_bwd", grid=(1,), out_shape=jax.ShapeDtypeStruct((1024, 512), F32),
        in_specs=[_const((MEM_LEN, 1024)), _const((MEM_LEN, 512))], out_specs=_const((1024, 512)),
        compiler_params=_params(("arbitrary",)),
    )(mem, dmemkv)


def _head_mask(lane, sub):
    return (lane < 64) if sub == 0 else (lane >= 64)


def _mem_attn(qm, kv):
    tb = qm.shape[0]
    lane = _iota((tb, HEAD_PAD), 1)
    outs, ps = [], []
    for pp in range(2):
        qp = qm[:, HEAD_PAD * pp:HEAD_PAD * (pp + 1)]
        kp = kv[:, HEAD_PAD * pp:HEAD_PAD * (pp + 1)]
        vp = kv[:, MEM_W + HEAD_PAD * pp:MEM_W + HEAD_PAD * (pp + 1)]
        pair = None
        for sub in range(2):
            qh = jnp.where(_head_mask(lane, sub), qp, jnp.zeros_like(qp))
            sc = _mm_nt(qh, kp) * 0.125
            e = jnp.exp(sc - jnp.max(sc, axis=-1, keepdims=True))
            p = e / jnp.sum(e, axis=-1, keepdims=True)
            o = _mm(p, vp)
            ps.append(p)
            pair = o if sub == 0 else jnp.where(lane < 64, pair, o)
        outs.append(pair)
    return jnp.concatenate(outs, axis=1), ps


def _mem_attn_bwd(dmo, qm, kv, ps):
    tb = qm.shape[0]
    lane = _iota((tb, HEAD_PAD), 1)
    dqs, dks, dvs = [], [], []
    for pp in range(2):
        qp = qm[:, HEAD_PAD * pp:HEAD_PAD * (pp + 1)]
        kp = kv[:, HEAD_PAD * pp:HEAD_PAD * (pp + 1)]
        vp = kv[:, MEM_W + HEAD_PAD * pp:MEM_W + HEAD_PAD * (pp + 1)]
        dop = dmo[:, HEAD_PAD * pp:HEAD_PAD * (pp + 1)]
        dq_pair, dk_pair, dv_pair = None, None, None
        for sub in range(2):
            msk = _head_mask(lane, sub)
            p = ps[2 * pp + sub]
            qh = jnp.where(msk, qp, jnp.zeros_like(qp))
            doh = jnp.where(msk, dop, 0.0).astype(BF16)
            dv = _mm_tn(p, doh)
            dp = _mm_nt(doh, vp)
            ds = (p * (dp - jnp.sum(dp * p, axis=-1, keepdims=True)) * 0.125).astype(BF16)
            dq = _mm(ds, kp)
            dk = _mm_tn(ds, qh)
            if sub == 0:
                dq_pair, dk_pair, dv_pair = dq, dk, dv
            else:
                dq_pair = jnp.where(lane < 64, dq_pair, dq)
                dk_pair, dv_pair = dk_pair + dk, dv_pair + dv
        dqs.append(dq_pair)
        dks.append(dk_pair)
        dvs.append(dv_pair)
    return jnp.concatenate(dqs, axis=1), jnp.concatenate(dks + dvs, axis=1)


def _mix_core(tok, gate, qm, kv, wout, h_in, g, b):
    mem_out, ps = _mem_attn(qm, kv)
    cat = jnp.concatenate([tok, mem_out], axis=1)
    sg = jax.nn.sigmoid(gate)
    sl = gate * sg
    y = cat * sl
    r = ALPHA * h_in + _mm(y, wout)
    mu = jnp.mean(r, axis=-1, keepdims=True)
    xc = r - mu
    rstd = lax.rsqrt(jnp.mean(xc * xc, axis=-1, keepdims=True) + NORM_EPS)
    xh = xc * rstd
    return xh * g + b, (ps, cat, sg, sl, y, xh, rstd)


def _mix_fwd(tok, gate, qm, kv, wout, h_in, g, b):
    s = tok.shape[0]
    tb = min(TB_MIX_FWD, s)

    def body(tok_ref, gate_ref, qm_ref, kv_ref, w_ref, h_ref, g_ref, b_ref, o_ref):
        o_ref[...], _ = _mix_core(tok_ref[...], gate_ref[...], qm_ref[...], kv_ref[...], w_ref[...], h_ref[...],
                                  g_ref[...], b_ref[...])

    return pl.pallas_call(
        body, name="mix_fwd", grid=(s // tb,), out_shape=jax.ShapeDtypeStruct((s, 1024), F32),
        in_specs=[_rows(tb, TOK_W), _rows(tb, 1024), _rows(tb, MEM_W), _const((MEM_LEN, 512)), _const((1024, 1024)),
                  _rows(tb, 1024), _const((1, 1024)), _const((1, 1024))],
        out_specs=_rows(tb, 1024), compiler_params=_params(("parallel",)),
    )(tok, gate, qm, kv, wout, h_in, g, b)


def _mix_bwd(tok, gate, qm, kv, wout, h_in, g, b, up, from_loss, lse=None, exchange=None):
    s = tok.shape[0]
    tb = min(TB_MIX, s)
    n_in = 9 if lse is None else 10
    n_tok_out = 1 if lse is None else 3

    def body(*refs):
        tok_ref, gate_ref, qm_ref, kv_ref, w_ref, h_ref, g_ref, b_ref, up_ref = refs[:9]
        dres_ref, tok_out = refs[n_in], refs[n_in + 1:n_in + 1 + n_tok_out]
        dgate_ref, dqm_ref, dw_ref, dkv_ref, dg_ref, db_ref, loss_ref = refs[n_in + 1 + n_tok_out:]

        @pl.when(pl.program_id(0) == 0)
        def _():
            for r in (dw_ref, dkv_ref, dg_ref, db_ref, loss_ref):
                r[...] = jnp.zeros_like(r)

        gate, qm, kv, wout, g = gate_ref[...], qm_ref[...], kv_ref[...], w_ref[...], g_ref[...]
        h_out, (ps, cat, sg, sl, y, xh, rstd) = _mix_core(tok_ref[...], gate, qm, kv, wout, h_ref[...], g, b_ref[...])
        if from_loss:
            diff = h_out - up_ref[...]
            loss_ref[...] += 0.5 * jnp.sum(jnp.mean(diff * diff, axis=-1, keepdims=True), axis=0, keepdims=True)
            dh = diff * (1.0 / D_MODEL)
        else:
            dh = up_ref[...]
        dg_ref[...] += jnp.sum(dh * xh, axis=0, keepdims=True)
        db_ref[...] += jnp.sum(dh, axis=0, keepdims=True)
        dxh = dh * g
        dr = rstd * (dxh - jnp.mean(dxh, axis=-1, keepdims=True) - xh * jnp.mean(dxh * xh, axis=-1, keepdims=True))
        dres_ref[...] = ALPHA * dr
        drb = dr.astype(BF16)
        dy = _mm_nt(drb, wout)
        dw_ref[...] += _mm_tn(y, drb)
        dcat = dy * sl
        dgate_ref[...] = dy * cat * (sg * (1.0 + gate * (1.0 - sg)))
        if lse is None:
            tok_out[0][...] = dcat[:, :TOK_W]
        else:
            _attn_stats(tok_ref[...], dcat[:, :TOK_W], refs[9], *tok_out)
        dqm, dkv = _mem_attn_bwd(dcat[:, TOK_W:], qm, kv, ps)
        dqm_ref[...] = dqm
        dkv_ref[...] += dkv

    npair = N_TOK_HEADS // 2
    tok_shapes = [jax.ShapeDtypeStruct((s, TOK_W), F32)] if lse is None else [
        jax.ShapeDtypeStruct((s, TOK_W), BF16), jax.ShapeDtypeStruct((TOK_W, s), BF16),
        jax.ShapeDtypeStruct((npair, s, HEAD_PAD), F32)]
    tok_specs = [_rows(tb, TOK_W)] if lse is None else [
        _rows(tb, TOK_W), pl.BlockSpec((TOK_W, tb), lambda i: (0, i)),
        pl.BlockSpec((npair, tb, HEAD_PAD), lambda i: (0, i, 0))]
    outs = (jax.ShapeDtypeStruct((s, 1024), F32), *tok_shapes,
            jax.ShapeDtypeStruct((s, 1024), F32), jax.ShapeDtypeStruct((s, MEM_W), F32),
            jax.ShapeDtypeStruct((1024, 1024), F32), jax.ShapeDtypeStruct((MEM_LEN, 512), F32),
            jax.ShapeDtypeStruct((1, 1024), F32), jax.ShapeDtypeStruct((1, 1024), F32),
            jax.ShapeDtypeStruct((1, 1), F32))
    args = (tok, gate, qm, kv, wout, h_in, g, b, up) + (() if lse is None else (lse,))
    return _run(
        body, name="mix_bwd_loss" if from_loss else "mix_bwd", grid=(s // tb,), out_shape=outs,
        in_specs=[_rows(tb, TOK_W), _rows(tb, 1024), _rows(tb, MEM_W), _const((MEM_LEN, 512)), _const((1024, 1024)),
                  _rows(tb, 1024), _const((1, 1024)), _const((1, 1024)), _rows(tb, 1024)]
        + ([] if lse is None else [_rows(tb, TOK_W)]),
        out_specs=(_rows(tb, 1024), *tok_specs, _rows(tb, 1024), _rows(tb, MEM_W), _const((1024, 1024)),
                   _const((MEM_LEN, 512)), _const((1, 1024)), _const((1, 1024)), _const((1, 1))),
        args=args, sem=("arbitrary",), exchange=exchange)


def _shift_down(u, tail, k):
    if k == 0:
        return u
    r = pltpu.roll(u, k, 0)
    row8 = _iota((8, u.shape[1]), 0)
    head = jnp.where(row8 < k, pltpu.roll(tail, k, 0), r[:8])
    return jnp.concatenate([head, r[8:]], axis=0)


def _shift_up(d, head, k):
    if k == 0:
        return d
    n = d.shape[0]
    r = pltpu.roll(d, n - k, 0)
    row8 = _iota((8, d.shape[1]), 0)
    last = jnp.where(row8 >= 8 - k, pltpu.roll(head, 8 - k, 0), r[n - 8:])
    return jnp.concatenate([r[:n - 8], last], axis=0)


def _scan_down(a, b):
    n = a.shape[0]
    row = _iota(a.shape, 0)
    s = 1
    while s < n:
        ok = row >= s
        a_s = jnp.where(ok, pltpu.roll(a, s, 0), 1.0)
        b_s = jnp.where(ok, pltpu.roll(b, s, 0), 0.0)
        b = a * b_s + b
        a = a * a_s
        s *= 2
    return a, b


def _scan_up(a, b):
    n = a.shape[0]
    row = _iota(a.shape, 0)
    s = 1
    while s < n:
        ok = row < n - s
        a_s = jnp.where(ok, pltpu.roll(a, n - s, 0), 1.0)
        b_s = jnp.where(ok, pltpu.roll(b, n - s, 0), 0.0)
        b = a * b_s + b
        a = a * a_s
        s *= 2
    return a, b


def _softplus(x):
    return jnp.maximum(x, 0.0) + jnp.log(1.0 + jnp.exp(-jnp.abs(x)))


def _lru_gates(u, tail, cw, cb, wr, br, wi, bi, lam):
    us = [_shift_down(u, tail, k) for k in range(4)]
    xc = cb + us[3] * cw[0:1] + us[2] * cw[1:2] + us[1] * cw[2:3] + us[0] * cw[3:4]
    xb = xc.astype(BF16)
    pre_r = jnp.concatenate([_mm(xb[:, 256 * g:256 * (g + 1)], wr[g]) for g in range(3)], axis=1) + br
    pre_i = jnp.concatenate([_mm(xb[:, 256 * g:256 * (g + 1)], wi[g]) for g in range(3)], axis=1) + bi
    rg, ig = jax.nn.sigmoid(pre_r), jax.nn.sigmoid(pre_i)
    clam = -LRU_C * _softplus(-lam)
    la = clam * rg
    a = jnp.exp(la)
    mm = jnp.sqrt(-jnp.tanh(la) * (a * a + 1.0))
    return us, xc, xb, rg, ig, clam, la, a, mm


def _lru_fwd(h, win, cw, cb, wr, br, wi, bi, lam):
    s = h.shape[0]
    tb = min(TB_LRU, s)

    def body(h_ref, win_ref, cw_ref, cb_ref, wr_ref, br_ref, wi_ref, bi_ref, lam_ref,
             u_ref, gate_ref, qm_ref, hs_ref, tail_sc, carry_sc):
        @pl.when(pl.program_id(0) == 0)
        def _():
            tail_sc[...] = jnp.zeros_like(tail_sc)
            carry_sc[...] = jnp.zeros_like(carry_sc)

        hb = h_ref[...].astype(BF16)
        z = jnp.concatenate([_mm(hb, win_ref[sh]) for sh in range(4)], axis=1)
        u = z[:, :TOK_W]
        u_ref[...] = u
        gate_ref[...] = z[:, TOK_W:TOK_W + 1024]
        qm_ref[...] = z[:, TOK_W + 1024:].astype(BF16)
        _, xc, _, _, ig, _, _, a, mm = _lru_gates(u, tail_sc[...], cw_ref[...], cb_ref[...], wr_ref[...], br_ref[...],
                                                 wi_ref[...], bi_ref[...], lam_ref[...])
        big_a, big_b = _scan_down(a, mm * (ig * xc))
        hs = big_a * carry_sc[0:1, :] + big_b
        hs_ref[...] = hs
        tail_sc[...] = u[tb - 8:, :]
        carry_sc[...] = jnp.broadcast_to(hs[tb - 1:tb, :], carry_sc.shape)

    outs = (jax.ShapeDtypeStruct((s, TOK_W), F32), jax.ShapeDtypeStruct((s, 1024), F32),
            jax.ShapeDtypeStruct((s, MEM_W), BF16), jax.ShapeDtypeStruct((s, TOK_W), F32))
    return pl.pallas_call(
        body, name="lru_fwd", grid=(s // tb,), out_shape=outs,
        in_specs=[_rows(tb, 1024), _const((4, 1024, 512)), _const((4, TOK_W)), _const((1, TOK_W)),
                  _const((3, 256, 256)), _const((1, TOK_W)), _const((3, 256, 256)), _const((1, TOK_W)),
                  _const((1, TOK_W))],
        out_specs=(_rows(tb, TOK_W), _rows(tb, 1024), _rows(tb, MEM_W), _rows(tb, TOK_W)),
        scratch_shapes=[pltpu.VMEM((8, TOK_W), F32), pltpu.VMEM((8, TOK_W), F32)],
        compiler_params=_params(),
    )(h, win, cw, cb, wr, br, wi, bi, lam)


def _lru_bwd(dhs, dgate, dqm, dres, h, u, hs, win, cw, cb, wr, br, wi, bi, lam):
    s = h.shape[0]
    tb = min(TB_LRU, s)
    nb = s // tb

    def rev(w):
        return pl.BlockSpec((tb, w), lambda i: (nb - 1 - i, 0))

    def prev_tail(w):
        return pl.BlockSpec((8, w), lambda i: (jnp.maximum((nb - 1 - i) * (tb // 8) - 1, 0), 0))

    def body(dhs_ref, dgate_ref, dqm_ref, dres_ref, h_ref, u_ref, hs_ref, ut_ref, hst_ref, win_ref, cw_ref, cb_ref,
             wr_ref, br_ref, wi_ref, bi_ref, lam_ref,
             dh_ref, dwin_ref, dcw_ref, dcb_ref, dwr_ref, dbr_ref, dwi_ref, dbi_ref, dlam_ref, ecar_sc, dxc_sc):
        i = pl.program_id(0)

        @pl.when(i == 0)
        def _():
            for r in (dwin_ref, dcw_ref, dcb_ref, dwr_ref, dbr_ref, dwi_ref, dbi_ref, dlam_ref, ecar_sc, dxc_sc):
                r[...] = jnp.zeros_like(r)

        first = (i == nb - 1)
        u = u_ref[...]
        utail = jnp.where(first, 0.0, ut_ref[...])
        hstail = jnp.where(first, 0.0, hst_ref[...])
        cw, wr, wi, lam = cw_ref[...], wr_ref[...], wi_ref[...], lam_ref[...]
        us, xc, xb, rg, ig, clam, la, a, mm = _lru_gates(u, utail, cw, cb_ref[...], wr, br_ref[...], wi, bi_ref[...], lam)
        row = _iota(a.shape, 0)
        a_next = jnp.where(row < tb - 1, pltpu.roll(a, tb - 1, 0), 1.0)
        big_a, big_b = _scan_up(a_next, dhs_ref[...])
        e = big_a * ecar_sc[0:1, :] + big_b
        ecar_sc[...] = jnp.broadcast_to(a[0:1, :] * e[0:1, :], ecar_sc.shape)
        hs_prev = _shift_down(hs_ref[...], hstail, 1)
        da = e * hs_prev
        ix = ig * xc
        dmm = e * ix
        dix = e * mm
        dla = da * a - dmm * (a * a) / mm
        dlam_ref[...] += jnp.sum(dla * rg, axis=0, keepdims=True)
        dpr = (dla * clam) * rg * (1.0 - rg)
        dpi = (dix * xc) * ig * (1.0 - ig)
        dbr_ref[...] += jnp.sum(dpr, axis=0, keepdims=True)
        dbi_ref[...] += jnp.sum(dpi, axis=0, keepdims=True)
        dprb, dpib = dpr.astype(BF16), dpi.astype(BF16)
        dxc_g = []
        for g in range(3):
            sl = slice(256 * g, 256 * (g + 1))
            dwr_ref[g] += _mm_tn(xb[:, sl], dprb[:, sl])
            dwi_ref[g] += _mm_tn(xb[:, sl], dpib[:, sl])
            dxc_g.append(_mm_nt(dprb[:, sl], wr[g]) + _mm_nt(dpib[:, sl], wi[g]))
        dxc = dix * ig + jnp.concatenate(dxc_g, axis=1)
        dcb_ref[...] += jnp.sum(dxc, axis=0, keepdims=True)
        dcw_ref[...] += jnp.concatenate([jnp.sum(dxc * us[3 - tap], axis=0, keepdims=True) for tap in range(4)], axis=0)
        head = dxc_sc[...]
        du = dxc * cw[3:4]
        for k in range(1, 4):
            du = du + _shift_up(dxc, head, k) * cw[3 - k:4 - k]
        dxc_sc[...] = dxc[:8, :]
        dz = jnp.concatenate([du, dgate_ref[...], dqm_ref[...]], axis=1).astype(BF16)
        hb = h_ref[...].astype(BF16)
        dh = dres_ref[...]
        for sh in range(4):
            dzs = dz[:, 512 * sh:512 * (sh + 1)]
            dh = dh + _mm_nt(dzs, win_ref[sh])
            dwin_ref[sh] += _mm_tn(hb, dzs)
        dh_ref[...] = dh

        @pl.when(i == nb - 1)
        def _():
            dlam_ref[...] = dlam_ref[...] * (LRU_C * jax.nn.sigmoid(-lam))

    outs = (jax.ShapeDtypeStruct((s, 1024), F32), jax.ShapeDtypeStruct((4, 1024, 512), F32),
            jax.ShapeDtypeStruct((4, TOK_W), F32), jax.ShapeDtypeStruct((1, TOK_W), F32),
            jax.ShapeDtypeStruct((3, 256, 256), F32), jax.ShapeDtypeStruct((1, TOK_W), F32),
            jax.ShapeDtypeStruct((3, 256, 256), F32), jax.ShapeDtypeStruct((1, TOK_W), F32),
            jax.ShapeDtypeStruct((1, TOK_W), F32))
    return pl.pallas_call(
        body, name="lru_bwd", grid=(nb,), out_shape=outs,
        in_specs=[rev(TOK_W), rev(1024), rev(MEM_W), rev(1024), rev(1024), rev(TOK_W), rev(TOK_W),
                  prev_tail(TOK_W), prev_tail(TOK_W),
                  _const((4, 1024, 512)), _const((4, TOK_W)), _const((1, TOK_W)), _const((3, 256, 256)),
                  _const((1, TOK_W)), _const((3, 256, 256)), _const((1, TOK_W)), _const((1, TOK_W))],
        out_specs=(rev(1024), _const((4, 1024, 512)), _const((4, TOK_W)), _const((1, TOK_W)), _const((3, 256, 256)),
                   _const((1, TOK_W)), _const((3, 256, 256)), _const((1, TOK_W)), _const((1, TOK_W))),
        scratch_shapes=[pltpu.VMEM((8, TOK_W), F32), pltpu.VMEM((8, TOK_W), F32)],
        compiler_params=_params(),
    )(dhs, dgate, dqm, dres, h, u, hs, u, hs, win, cw, cb, wr, br, wi, bi, lam)


def _adamw_update(w_ref, g_ref, m_ref, v_ref, d_ref, nm_ref, nv_ref):
    g = g_ref[...]
    nm = ADAM_B1 * m_ref[...] + (1.0 - ADAM_B1) * g
    nv = ADAM_B2 * v_ref[...] + (1.0 - ADAM_B2) * (g * g)
    m_hat = nm / (1.0 - ADAM_B1 ** ADAM_STEP)
    v_hat = nv / (1.0 - ADAM_B2 ** ADAM_STEP)
    d_ref[...] = -ADAM_LR * (m_hat / (jnp.sqrt(v_hat) + ADAM_EPS) + ADAM_WD * w_ref[...])
    nm_ref[...] = nm
    nv_ref[...] = nv


def _adamw(name, w, g, m, v):
    rows, cols = w.shape
    tb = 256 if rows % 256 == 0 else rows

    def body(*refs):
        _adamw_update(*refs)

    shp = jax.ShapeDtypeStruct((rows, cols), F32)
    return pl.pallas_call(
        body, name="adamw_" + name, grid=(rows // tb,), out_shape=(shp, shp, shp),
        in_specs=[_rows(tb, cols)] * 4, out_specs=(_rows(tb, cols),) * 3,
        compiler_params=_params(("parallel",)),
    )(w, g, m, v)


def _adamw_small(items):
    n = len(items)

    def body(*refs):
        for k in range(n):
            _adamw_update(*refs[4 * k:4 * k + 4], *refs[4 * n + 3 * k:4 * n + 3 * k + 3])

    args = [a for it in items for a in it]
    shapes = [jax.ShapeDtypeStruct(it[0].shape, F32) for it in items for _ in range(3)]
    outs = pl.pallas_call(
        body, name="adamw_small", grid=(1,), out_shape=tuple(shapes),
        in_specs=[_const(a.shape) for a in args], out_specs=tuple(_const(sh.shape) for sh in shapes),
        compiler_params=_params(),
    )(*args)
    return [outs[3 * k:3 * k + 3] for k in range(n)]


def _row_block(rows, cap=2048):
    return max(t for t in range(8, cap + 1, 8) if rows % t == 0)


def _add_own_half(sh, got, c_idx, also_bf16):
    hs = sh.shape[1] // 2
    tb = _row_block(hs, 1024)
    nb = hs // tb

    def body(c_ref, a_ref, b_ref, o_ref, *wire_ref):
        total = a_ref[...] + b_ref[...]
        o_ref[...] = total
        if also_bf16:
            wire_ref[0][...] = total.astype(BF16)

    blk = pl.BlockSpec((1, tb, 128), lambda s, i, c: (s, i, 0))
    n_out = 2 if also_bf16 else 1
    return pl.pallas_call(
        body, name="add_sibling",
        grid_spec=pltpu.PrefetchScalarGridSpec(
            num_scalar_prefetch=1, grid=(4, nb),
            in_specs=[pl.BlockSpec((1, tb, 128), lambda s, i, c: (s, c[0] * nb + i, 0)), blk],
            out_specs=(blk,) * n_out),
        out_shape=(jax.ShapeDtypeStruct((4, hs, 128), F32), jax.ShapeDtypeStruct((4, hs, 128), BF16))[:n_out],
        compiler_params=_params(("parallel", "parallel")),
    )(c_idx.reshape(1).astype(jnp.int32), sh, got)


def _add2(a, b):
    rows = a.shape[0]
    tb = _row_block(rows)

    def body(a_ref, b_ref, o_ref):
        o_ref[...] = a_ref[...] + b_ref[...]

    return pl.pallas_call(
        body, name="add_sibling", grid=(rows // tb,), out_shape=jax.ShapeDtypeStruct(a.shape, F32),
        in_specs=[_rows(tb, 128)] * 2, out_specs=_rows(tb, 128), compiler_params=_params(("parallel",)),
    )(a, b)


def _sum_slots(landed, own, rows):
    tb = _row_block(rows, 1024)

    def body(l_ref, o_ref, out_ref):
        t = 2 * lax.axis_index("x") + lax.axis_index("y")
        r = [jnp.where(t == s, o_ref[s], l_ref[s].astype(F32)) for s in range(4)]
        out_ref[...] = ((r[0] + r[1]) + r[2]) + r[3]

    return pl.pallas_call(
        body, name="sum_chips", grid=(rows // tb,), out_shape=jax.ShapeDtypeStruct((rows, 128), F32),
        in_specs=[pl.BlockSpec((4, tb, 128), lambda i: (0, i, 0))] * 2, out_specs=_rows(tb, 128),
        compiler_params=_params(("parallel",)),
    )(landed, own)


_ANY = pl.BlockSpec(memory_space=pl.ANY)


def _place():
    x, y, c = lax.axis_index("x"), lax.axis_index("y"), lax.axis_index("c")
    return x, y, c, [(1 - x, y), (x, 1 - y), (1 - x, 1 - y)]


def _remote(src, dst, ssem, rsem, to):
    return pltpu.make_async_remote_copy(src_ref=src, dst_ref=dst, send_sem=ssem, recv_sem=rsem, device_id=to,
                                        device_id_type=MESH)


class _Exchange:
    def __init__(self, ins, out_shape, sems, start, finish):
        self.ins, self.out_shape, self.sems, self.start, self.finish = ins, out_shape, sems, start, finish


def _run(body, *, name, grid, in_specs, out_specs, out_shape, args, scratch=(), sem, exchange=None,
         vmem_limit=VMEM_LIMIT):
    if exchange is None:
        return pl.pallas_call(body, name=name, grid=grid, out_shape=tuple(out_shape), in_specs=list(in_specs),
                              out_specs=tuple(out_specs), scratch_shapes=list(scratch),
                              compiler_params=_params(sem, vmem_limit))(*args)
    n_in, n_out, n_sc = len(args), len(out_shape), len(scratch)
    k_in, k_out = len(exchange.ins), len(exchange.out_shape)

    def fused(*refs):
        ins, refs = refs[:n_in], refs[n_in:]
        xin, refs = refs[:k_in], refs[k_in:]
        outs, refs = refs[:n_out], refs[n_out:]
        xout, refs = refs[:k_out], refs[k_out:]
        sc, xsem = refs[:n_sc], refs[n_sc:]
        first = pl.program_id(0) == 0
        last = pl.program_id(0) == grid[0] - 1
        for a in range(1, len(grid)):
            first = first & (pl.program_id(a) == 0)
            last = last & (pl.program_id(a) == grid[a] - 1)

        @pl.when(first)
        def _():
            exchange.start(xin, xout, xsem)

        body(*ins, *outs, *sc)

        @pl.when(last)
        def _():
            exchange.finish(xin, xout, xsem)

    return pl.pallas_call(
        fused, name=name, grid=grid, out_shape=(*out_shape, *exchange.out_shape),
        in_specs=[*in_specs, *[_ANY] * k_in], out_specs=(*out_specs, *[_ANY] * k_out),
        scratch_shapes=[*scratch, *exchange.sems],
        compiler_params=_params(("arbitrary",) * len(grid), vmem_limit),
    )(*args, *exchange.ins)


def _run_exchange(exchange, name):
    def body(*refs):
        k_in, k_out = len(exchange.ins), len(exchange.out_shape)
        xin, xout, xsem = refs[:k_in], refs[k_in:k_in + k_out], refs[k_in + k_out:]
        exchange.start(xin, xout, xsem)
        exchange.finish(xin, xout, xsem)

    return pl.pallas_call(
        body, name=name, out_shape=tuple(exchange.out_shape), in_specs=[_ANY] * len(exchange.ins),
        out_specs=tuple([_ANY] * len(exchange.out_shape)), scratch_shapes=list(exchange.sems),
    )(*exchange.ins)


def _gather_shards(wsh):
    _, hh, _ = wsh.shape

    def first_hop(w_ref, out_ref, ssems, rsems):
        x, y, c, chips = _place()
        t = 2 * x + y
        return [_remote(w_ref.at[c], out_ref.at[t, c], ssems.at[j], rsems.at[j], (cx, cy, c))
                for j, (cx, cy) in enumerate(chips)]

    def start(xin, xout, xsem):
        for cp in first_hop(xin[0], xout[0], *xsem):
            cp.start()

    def finish(xin, xout, xsem):
        out_ref, (ssems, rsems) = xout[0], xsem
        first = first_hop(xin[0], out_ref, *xsem)
        x, y, c, chips = _place()
        passed = []
        for j, (cx, cy) in enumerate(chips):
            got = out_ref.at[2 * cx + cy, c]
            _remote(got, got, ssems.at[j], rsems.at[j], (cx, cy, c)).wait_recv()
            cp = _remote(got, got, ssems.at[3 + j], rsems.at[3 + j], (x, y, 1 - c))
            cp.start()
            passed.append(cp)
        for j, (cx, cy) in enumerate(chips):
            got = out_ref.at[2 * cx + cy, 1 - c]
            _remote(got, got, ssems.at[3 + j], rsems.at[3 + j], (x, y, 1 - c)).wait_recv()
        for cp in first + passed:
            cp.wait_send()

    return _Exchange([wsh], [jax.ShapeDtypeStruct((4, 2, hh, 128), wsh.dtype)],
                     [pltpu.SemaphoreType.DMA((6,)), pltpu.SemaphoreType.DMA((6,))], start, finish)


def _gathered(landed, own):
    t = 2 * lax.axis_index("x") + lax.axis_index("y")
    return lax.dynamic_update_slice(landed, own[None], (t, 0, 0, 0))


def _swap_sibling(sh, rp):
    hs, rr = sh.shape[1] // 2, rp.shape[2]

    def copies(xin, xout, xsem):
        x, y, c, _ = _place()
        sib = (x, y, 1 - c)
        rows = xin[0].at[:, pl.ds(pl.multiple_of((1 - c) * hs, 8), hs)]
        return [_remote(rows, xout[0].at[:, pl.ds(0, hs)], xsem[0].at[0], xsem[1].at[0], sib),
                _remote(xin[1].at[:, 1 - c], xout[0].at[:, pl.ds(hs, rr)], xsem[0].at[1], xsem[1].at[1], sib)]

    def start(*a):
        for cp in copies(*a):
            cp.start()

    def finish(*a):
        for cp in copies(*a):
            cp.wait()

    return _Exchange([sh, rp], [jax.ShapeDtypeStruct((4, hs + rr, 128), F32)],
                     [pltpu.SemaphoreType.DMA((2,)), pltpu.SemaphoreType.DMA((2,))], start, finish)


def _scatter_chips(parts):
    n = len(parts)

    def copies(xin, xout, ssems, rsems):
        x, y, c, chips = _place()
        t = 2 * x + y
        return [_remote(xin[k].at[2 * cx + cy], xout[k].at[t], ssems.at[n * j + k], rsems.at[n * j + k], (cx, cy, c))
                for j, (cx, cy) in enumerate(chips) for k in range(n)]

    def start(xin, xout, xsem):
        for cp in copies(xin, xout, *xsem):
            cp.start()

    def finish(xin, xout, xsem):
        ssems, rsems = xsem
        x, y, c, chips = _place()
        for j, (cx, cy) in enumerate(chips):
            for k in range(n):
                got = xout[k].at[2 * cx + cy]
                _remote(got, got, ssems.at[n * j + k], rsems.at[n * j + k], (cx, cy, c)).wait_recv()
        for cp in copies(xin, xout, *xsem):
            cp.wait_send()

    return _Exchange(parts, [jax.ShapeDtypeStruct(a.shape, a.dtype) for a in parts],
                     [pltpu.SemaphoreType.DMA((3 * n,)), pltpu.SemaphoreType.DMA((3 * n,))], start, finish)


def _share_reduced(piece, eighth):
    def copies(t_ref, mine_r, sib_ref, rall_ref, ssems, rsems, lsem):
        x, y, c, _ = _place()
        me = 4 * x + 2 * y + c
        loc = pltpu.make_async_copy(mine_r, rall_ref.at[me], lsem)
        sends = [_remote(t_ref, sib_ref, ssems.at[0], rsems.at[0], (x, y, 1 - c))]
        peers = []
        for mask in range(1, 8):
            px = 1 - x if mask & 4 else x
            py = 1 - y if mask & 2 else y
            pc = 1 - c if mask & 1 else c
            peers.append((mask, px, py, pc))
            sends.append(_remote(mine_r, rall_ref.at[me], ssems.at[mask], rsems.at[mask], (px, py, pc)))
        return loc, sends, peers

    def start(xin, xout, xsem):
        loc, sends, _ = copies(*xin, *xout, *xsem)
        for cp in [loc] + sends:
            cp.start()

    def finish(xin, xout, xsem):
        (sib_ref, rall_ref), (ssems, rsems, _) = xout, xsem
        loc, sends, peers = copies(*xin, *xout, *xsem)
        x, y, c, _ = _place()
        _remote(sib_ref, sib_ref, ssems.at[0], rsems.at[0], (x, y, 1 - c)).wait_recv()
        for mask, px, py, pc in peers:
            got = rall_ref.at[4 * px + 2 * py + pc]
            _remote(got, got, ssems.at[mask], rsems.at[mask], (px, py, pc)).wait_recv()
        for cp in sends:
            cp.wait_send()
        loc.wait()

    return _Exchange([piece, eighth],
                     [jax.ShapeDtypeStruct(piece.shape, F32), jax.ShapeDtypeStruct((8, *eighth.shape), F32)],
                     [pltpu.SemaphoreType.DMA((8,)), pltpu.SemaphoreType.DMA((8,)), pltpu.SemaphoreType.DMA],
                     start, finish)


def _ceil_to(n, m):
    return -(-n // m) * m


def _pack_bf16(parts):
    blocks = [p.reshape(-1, 128) for p in parts]
    rows = jnp.concatenate([jnp.pad(b, ((0, -b.shape[0] % 16), (0, 0))) for b in blocks])
    hw = _ceil_to(rows.shape[0], 32) // 2
    return jnp.pad(rows, ((0, 2 * hw - rows.shape[0]), (0, 0))).reshape(2, hw, 128)


def _segments(wall, parts):
    wall = wall.reshape(4, -1, 128)
    out, row = [], 0
    for p in parts:
        n = p.size // 128
        out.append(wall[:, row:row + n].reshape(4, *p.shape))
        row += _ceil_to(n, 16)
    return out


class _GradReduce:
    def __init__(self, sharded, replicated, c_idx, wire_bf16=False):
        self.c_idx, self.wire_bf16 = c_idx, wire_bf16
        self.rowwise = [(n, g.shape[1:]) for n, g in sharded if math.prod(g.shape[1:]) % 128 == 0]
        self.small = [(n, g.shape[1:]) for n, g in sharded if math.prod(g.shape[1:]) % 128 != 0]
        self.replicated = [(n, g.shape[0]) for n, g in replicated]
        by_name = dict(sharded)
        blocks = [by_name[n].reshape(4, -1, 128) for n, _ in self.rowwise]
        if self.small:
            rest = jnp.concatenate([by_name[n].reshape(4, -1) for n, _ in self.small], axis=1)
            blocks.append(jnp.pad(rest, ((0, 0), (0, -rest.shape[1] % 128))).reshape(4, -1, 128))
        blocks = [jnp.pad(b, ((0, 0), (0, -b.shape[1] % 8), (0, 0))) for b in blocks]
        rows = sum(b.shape[1] for b in blocks)
        self.hs = _ceil_to(rows, 256) // 2
        sh = jnp.concatenate(blocks + [jnp.zeros((4, 2 * self.hs - rows, 128), F32)], axis=1)
        rp = jnp.concatenate([g for _, g in replicated])
        self.rr = _ceil_to(_ceil_to(rp.shape[0], 128) // 128, 64) // 8
        rp = jnp.pad(rp, (0, 8 * self.rr * 128 - rp.shape[0])).reshape(4, 2, self.rr, 128)

        self.sh, self.rp = sh, rp

    def swap(self):
        return _swap_sibling(self.sh, self.rp)

    def swapped(self, got):
        self.chip_sum, *wire = _add_own_half(self.sh, got, self.c_idx, self.wire_bf16)
        self.wire = wire[0] if wire else self.chip_sum
        mine_r = lax.dynamic_index_in_dim(self.rp, self.c_idx, axis=1, keepdims=False)
        self.chip_r = _add2(mine_r.reshape(-1, 128), got[:, self.hs:].reshape(-1, 128)).reshape(4, self.rr, 128)

    def scatter(self):
        return _scatter_chips([self.wire, self.chip_r])

    def scattered(self, landed, landed_r):
        self.piece = _sum_slots(landed, self.chip_sum, self.hs)
        self.eighth = _sum_slots(landed_r, self.chip_r, self.rr)

    def share(self):
        return _share_reduced(self.piece, self.eighth)

    def shared(self, sibling, rall):
        mine, sib = self.piece, sibling
        self.shard = jnp.where(self.c_idx == 0, jnp.concatenate([mine, sib]), jnp.concatenate([sib, mine]))
        self.rall = rall

    def reduced(self):
        out, row = {}, 0
        for name, shape in self.rowwise:
            rows = math.prod(shape) // 128
            out[name] = self.shard[row:row + rows].reshape(shape)
            row += _ceil_to(rows, 8)
        for group, flat in ((self.small, self.shard[row:].reshape(-1)), (self.replicated, self.rall.reshape(-1))):
            off = 0
            for name, shape in group:
                n = math.prod(shape) if isinstance(shape, tuple) else shape
                out[name] = flat[off:off + n]
                off += n
        return out


def _col_shards(w2d):
    rows, cols = w2d.shape
    return w2d.reshape(rows, 4, cols // 4).transpose(1, 0, 2)


_WIN0_PARTS = ((0, 384, 1280), (384, 640, 1664), (640, 672, 1984), (672, 1696, 0), (1696, 1952, 1024))


def _win0_aligned(shards):
    def cols(a, b):
        return [shards[s][:, max(a, 488 * s) - 488 * s:min(b, 488 * (s + 1)) - 488 * s]
                for s in range(4) if max(a, 488 * s) < min(b, 488 * (s + 1))]

    zeros = jnp.zeros((1024, 64), shards.dtype)
    return jnp.concatenate(cols(672, 1696) + cols(1696, 1952) + cols(0, 384) + cols(384, 640)
                           + [zeros] + cols(640, 672) + [zeros[:, :32]], axis=1)


def _win0_shards(dwin0p):
    shards = []
    for s in range(4):
        lo, hi = 488 * s, 488 * (s + 1)
        cols = [dwin0p[:, p + max(lo, a) - a:p + min(hi, b) - a] for a, b, p in _WIN0_PARTS if max(lo, a) < min(hi, b)]
        shards.append(jnp.concatenate(cols, axis=1))
    return jnp.stack(shards)


def _block_diag4(w):
    eye = jnp.eye(4, dtype=w.dtype)
    return jnp.einsum("gaij,ab->gaibj", w.reshape(3, 4, 64, 64), eye).reshape(3, 256, 256)


def _diag_blocks4(w):
    w5 = w.reshape(3, 4, 64, 4, 64)
    return jnp.stack([w5[:, a, :, a, :] for a in range(4)], axis=1).reshape(12, 64, 64)


def kernel(x, mem, positions, mla_w_in, mla_q_norm, mla_w_uq, mla_kv_norm, mla_w_ukv, lru_w_in, lru_conv_w, lru_conv_b, lru_w_rgate, lru_b_rgate, lru_w_igate, lru_b_igate, lru_lambda, w_mem_kv, w_out, ln_g, ln_b, loss_target, m_mla_w_in, m_mla_q_norm, m_mla_w_uq, m_mla_kv_norm, m_mla_w_ukv, m_lru_w_in, m_lru_conv_w, m_lru_conv_b, m_lru_w_rgate, m_lru_b_rgate, m_lru_w_igate, m_lru_b_igate, m_lru_lambda, m_w_mem_kv, m_w_out, m_ln_g, m_ln_b, v_mla_w_in, v_mla_q_norm, v_mla_w_uq, v_mla_kv_norm, v_mla_w_ukv, v_lru_w_in, v_lru_conv_w, v_lru_conv_b, v_lru_w_rgate, v_lru_b_rgate, v_lru_w_igate, v_lru_b_igate, v_lru_lambda, v_w_mem_kv, v_w_out, v_ln_g, v_ln_b):
    s = x.shape[1]
    c_idx = lax.axis_index("c")
    x2, mem2, tgt2 = x[0], mem[0], loss_target[0]

    first = [p.astype(BF16) for p in (mla_w_in[0], mla_w_uq[0], mla_w_ukv[0])]
    buf = _pack_bf16(first)
    mla_shards = _segments(_gathered(_run_exchange(_gather_shards(buf), "gather_weights")[0], buf), first)

    mid = [w_mem_kv.astype(BF16), w_out[0].astype(BF16)]
    buf_mid = _pack_bf16(mid)

    def mid_weights(landed):
        wmem, wout0 = _segments(_gathered(landed[0], buf_mid), mid)
        return wmem.transpose(1, 0, 2, 3).reshape(2, 1024, 512), wout0.reshape(1024, 1024)

    small = jnp.concatenate([lru_conv_w[0].reshape(-1), lru_conv_b[0], lru_b_rgate[0], lru_b_igate[0], lru_lambda[0]])
    late = [lru_w_in[0].astype(BF16), w_out[1].astype(BF16), lax.bitcast_convert_type(small, BF16)]
    buf_late = _pack_bf16(late)

    def late_weights(landed):
        win1, wout1, small_bits = _segments(_gathered(landed[0], buf_late), late)
        small_all = lax.bitcast_convert_type(small_bits, F32)
        cw = small_all[:, :768].reshape(4, 4, 192).transpose(1, 0, 2).reshape(4, TOK_W)
        cb, br, bi, lam = (small_all[:, 768 + 192 * k:960 + 192 * k].reshape(1, TOK_W) for k in range(4))
        return win1, wout1.reshape(1024, 1024), cw, cb, br, bi, lam

    def reduce_late(g):
        return _GradReduce(
            [("lru_w_in", g["lru_w_in"]), ("lru_conv_w", _col_shards(g["lru_conv_w"])),
             ("lru_conv_b", _col_shards(g["lru_conv_b"])), ("lru_b_rgate", _col_shards(g["lru_b_rgate"])),
             ("lru_b_igate", _col_shards(g["lru_b_igate"])), ("lru_lambda", _col_shards(g["lru_lambda"])),
             ("w_mem_kv1", g["w_mem_kv1"].reshape(4, 256, 512)), ("w_out1", g["w_out1"].reshape(4, 256, 1024))],
            [("lru_w_rgate", g["lru_w_rgate"].reshape(-1)), ("lru_w_igate", g["lru_w_igate"].reshape(-1)),
             ("ln_g1", g["ln_g1"].reshape(-1)), ("ln_b1", g["ln_b1"].reshape(-1))], c_idx)

    g0, late_red = _local_step(
        x2, mem2, positions.reshape(s, 1), tgt2, *mla_shards, mla_q_norm, mla_kv_norm, lru_w_rgate[0], lru_w_igate[0],
        ln_g, ln_b, mid_weights, late_weights, _gather_shards(buf_mid), _gather_shards(buf_late), reduce_late)

    early_red = _GradReduce(
        [("mla_w_in", g0["mla_w_in"]), ("mla_w_uq", _col_shards(g0["mla_w_uq"])),
         ("mla_w_ukv", _col_shards(g0["mla_w_ukv"])), ("w_mem_kv0", g0["w_mem_kv0"].reshape(4, 256, 512)),
         ("w_out0", g0["w_out0"].reshape(4, 256, 1024))],
        [("mla_q_norm", g0["mla_q_norm"].reshape(-1)), ("mla_kv_norm", g0["mla_kv_norm"].reshape(-1)),
         ("ln_g0", g0["ln_g0"].reshape(-1)), ("ln_b0", g0["ln_b0"].reshape(-1)), ("loss", g0["loss"].reshape(-1))],
        c_idx, wire_bf16=True)
    early_red.swapped(*_run_exchange(early_red.swap(), "swap_sibling"))
    early_red.scattered(*_run_exchange(early_red.scatter(), "scatter_chips"))
    early_red.shared(*_run_exchange(early_red.share(), "share_reduced"))
    red = {**late_red.reduced(), **early_red.reduced()}
    red["w_mem_kv"] = jnp.concatenate([red["w_mem_kv0"], red["w_mem_kv1"]])
    red["w_out"] = jnp.concatenate([red["w_out0"], red["w_out1"]])
    red["ln_g"] = jnp.concatenate([red["ln_g0"], red["ln_g1"]])
    red["ln_b"] = jnp.concatenate([red["ln_b0"], red["ln_b1"]])

    weights = dict(mla_w_in=mla_w_in, mla_q_norm=mla_q_norm, mla_w_uq=mla_w_uq, mla_kv_norm=mla_kv_norm,
                   mla_w_ukv=mla_w_ukv, lru_w_in=lru_w_in, lru_conv_w=lru_conv_w, lru_conv_b=lru_conv_b,
                   lru_w_rgate=lru_w_rgate, lru_b_rgate=lru_b_rgate, lru_w_igate=lru_w_igate, lru_b_igate=lru_b_igate,
                   lru_lambda=lru_lambda, w_mem_kv=w_mem_kv, w_out=w_out, ln_g=ln_g, ln_b=ln_b)
    m_in = dict(mla_w_in=m_mla_w_in, mla_q_norm=m_mla_q_norm, mla_w_uq=m_mla_w_uq, mla_kv_norm=m_mla_kv_norm,
                mla_w_ukv=m_mla_w_ukv, lru_w_in=m_lru_w_in, lru_conv_w=m_lru_conv_w, lru_conv_b=m_lru_conv_b,
                lru_w_rgate=m_lru_w_rgate, lru_b_rgate=m_lru_b_rgate, lru_w_igate=m_lru_w_igate,
                lru_b_igate=m_lru_b_igate, lru_lambda=m_lru_lambda, w_mem_kv=m_w_mem_kv, w_out=m_w_out, ln_g=m_ln_g,
                ln_b=m_ln_b)
    v_in = dict(mla_w_in=v_mla_w_in, mla_q_norm=v_mla_q_norm, mla_w_uq=v_mla_w_uq, mla_kv_norm=v_mla_kv_norm,
                mla_w_ukv=v_mla_w_ukv, lru_w_in=v_lru_w_in, lru_conv_w=v_lru_conv_w, lru_conv_b=v_lru_conv_b,
                lru_w_rgate=v_lru_w_rgate, lru_b_rgate=v_lru_b_rgate, lru_w_igate=v_lru_w_igate,
                lru_b_igate=v_lru_b_igate, lru_lambda=v_lru_lambda, w_mem_kv=v_w_mem_kv, w_out=v_w_out, ln_g=v_ln_g,
                ln_b=v_ln_b)
    order = ["mla_w_in", "mla_q_norm", "mla_w_uq", "mla_kv_norm", "mla_w_ukv", "lru_w_in", "lru_conv_w", "lru_conv_b",
             "lru_w_rgate", "lru_b_rgate", "lru_w_igate", "lru_b_igate", "lru_lambda", "w_mem_kv", "w_out", "ln_g",
             "ln_b"]
    grads, deltas, new_m, new_v = {}, {}, {}, {}

    def operands(name):
        shape = weights[name].shape
        two_d = (math.prod(shape[:-1]), shape[-1])
        return [a.reshape(two_d) for a in (weights[name], red[name], m_in[name], v_in[name])]

    def keep(name, g2, d2, m2, v2):
        shape = weights[name].shape
        grads[name], deltas[name] = g2.reshape(shape), d2.reshape(shape)
        new_m[name], new_v[name] = m2.reshape(shape), v2.reshape(shape)

    small = [n for n in order if weights[n].size <= 4096]
    ops = [operands(n) for n in small]
    for name, op, res in zip(small, ops, _adamw_small(ops)):
        keep(name, op[1], *res)
    for name in order:
        if name not in small:
            op = operands(name)
            keep(name, op[1], *_adamw(name, *op))
    return (red["loss"][0], g0["x"][None], *[grads[n] for n in order], *[deltas[n] for n in order],
            *[new_m[n] for n in order], *[new_v[n] for n in order])


def _local_step(x2, mem2, pos_col, tgt2, win0_sh, wuq_sh, wukv_sh, gq, gkv, w_rgate, w_igate, ln_g, ln_b,
                mid_weights, late_weights, gather_mid=None, gather_late=None, reduce_late=None):
    s = x2.shape[0]
    win0p = _win0_aligned(win0_sh)
    wuq_p = jnp.pad(wuq_sh.reshape(4, Q_LORA, 3, 96).transpose(1, 0, 2, 3).reshape(Q_LORA, 12, 96),
                    ((0, 0), (0, 0), (0, 32))).reshape(Q_LORA, QK_W)
    wukv3 = wukv_sh.reshape(4, KV_LORA, 3, 128).transpose(1, 0, 2, 3).reshape(KV_LORA, 12, 128)
    wk_p = jnp.pad(wukv3[:, :, :64], ((0, 0), (0, 0), (0, 64))).reshape(KV_LORA, QK_W)
    wv = wukv3[:, :, 64:].reshape(KV_LORA, TOK_W)
    wr_bd = _block_diag4(w_rgate).astype(BF16)
    wi_bd = _block_diag4(w_igate).astype(BF16)
    half = 16
    inv_freq = ROPE_THETA ** (-jnp.arange(half, dtype=F32) / half)
    inv_lane = jnp.concatenate([jnp.zeros((64,), F32), inv_freq, inv_freq, jnp.zeros((32,), F32)]).reshape(1, HEAD_PAD)

    gate0, qm0, cq, ckv, q_p, q_t, k_p, v_b, v_t, ctab, satab, sbtab, *landed = _mla_proj_fwd(
        x2, win0p, gq, gkv, wuq_p, wk_p, wv, pos_col, inv_lane, exchange=gather_mid)
    wmem, wout0 = mid_weights(landed)
    memkv = _mem_kv(mem2, wmem)
    tok0, lse, *landed = _attn_fwd(q_p, k_p, v_t, exchange=gather_late)
    win1, wout1, cw, cb, br, bi, lam = late_weights(landed)
    g0, b0, g1, b1 = ln_g[0:1], ln_b[0:1], ln_g[1:2], ln_b[1:2]
    h1 = _mix_fwd(tok0, gate0, qm0, memkv[0], wout0, x2, g0, b0)
    u1, gate1, qm1, hs1 = _lru_fwd(h1, win1, cw, cb, wr_bd, br, wi_bd, bi, lam)

    dres1, dtok1, dgate1, dqm1, dwout1, dmemkv1, dg1, db1, loss = _mix_bwd(
        hs1, gate1, qm1, memkv[1], wout1, h1, g1, b1, tgt2, True)
    dh1, dwin1, dcw, dcb, dwr_bd, dbr, dwi_bd, dbi, dlam = _lru_bwd(
        dtok1, dgate1, dqm1, dres1, h1, u1, hs1, win1, cw, cb, wr_bd, br, wi_bd, bi, lam)
    late = {"lru_w_in": dwin1, "lru_conv_w": dcw, "lru_conv_b": dcb, "lru_b_rgate": dbr, "lru_b_igate": dbi,
            "lru_lambda": dlam, "w_mem_kv1": _mem_kv_bwd(mem2, dmemkv1), "w_out1": dwout1,
            "lru_w_rgate": _diag_blocks4(dwr_bd), "lru_w_igate": _diag_blocks4(dwi_bd), "ln_g1": dg1, "ln_b1": db1}
    red = reduce_late(late) if reduce_late is not None else None

    dres0, dob, dobt, stats, dgate0, dqm0, dwout0, dmemkv0, dg0, db0, _, *got = _mix_bwd(
        tok0, gate0, qm0, memkv[0], wout0, x2, g0, b0, dh1, False, lse=lse, exchange=red.swap() if red else None)
    if red:
        red.swapped(*got)
    dq_p, dk_t, dv_t, *got = _attn_bwd(q_p, q_t, k_p, v_b, dob, dobt, stats,
                                       exchange=red.scatter() if red else None)
    if red:
        red.scattered(*got)
    if red:
        red.shared(*_run_exchange(red.share(), "share_reduced"))
    dx, dwin0p, dwuq_p, dwk_p, dwv, dgq, dgkv = _mla_proj_bwd(
        x2, cq, ckv, dq_p, dk_t, dv_t, dgate0, dqm0, dres0, win0p, gq, gkv, wuq_p, wk_p, wv,
        ctab, satab, sbtab)

    dwin0 = _win0_shards(dwin0p)
    dwuq = dwuq_p.reshape(Q_LORA, 12, 128)[:, :, :96].reshape(Q_LORA, 1152)
    dwukv = jnp.concatenate([dwk_p.reshape(KV_LORA, 12, 128)[:, :, :64], dwv.reshape(KV_LORA, 12, 64)],
                            axis=2).reshape(KV_LORA, 1536)
    early = {"x": dx, "loss": loss, "mla_w_in": dwin0, "mla_w_uq": dwuq, "mla_w_ukv": dwukv,
             "w_mem_kv0": _mem_kv_bwd(mem2, dmemkv0), "w_out0": dwout0, "mla_q_norm": dgq, "mla_kv_norm": dgkv,
             "ln_g0": dg0, "ln_b0": db0}
    return early, (red if red else late)
```
